```python
import jax, jax.numpy as jnp
from jax import lax
import numpy as np

D_MODEL = 1024
BATCH = 8
SEQ = 2048
DEPTH = 4

ATTN_HEADS = 8
HEAD_DIM = 64
ATTN_WIDTH = ATTN_HEADS * HEAD_DIM
CONV_WIDTH = D_MODEL - ATTN_WIDTH
CONV_KERNEL = 31
D_FF = 2816
FFN_KERNEL = 3
BLOCK_Q = 128
IN_WIDTH = 3 * ATTN_WIDTH + 2 * CONV_WIDTH
EPS = 1e-6

kernel_name = "hybrid_stickbreak_conformer_convffn"


def rms_norm(x, g):
    xf = x.astype(jnp.float32)
    y = xf * lax.rsqrt(jnp.mean(xf * xf, axis=-1, keepdims=True) + EPS)
    return (y * g.astype(jnp.float32)).astype(x.dtype)


def layer_norm(x, g, b):
    xf = x.astype(jnp.float32)
    mu = jnp.mean(xf, axis=-1, keepdims=True)
    xc = xf - mu
    var = jnp.mean(xc * xc, axis=-1, keepdims=True)
    y = xc * lax.rsqrt(var + EPS) * g.astype(jnp.float32) + b.astype(jnp.float32)
    return y.astype(x.dtype)


def causal_depthwise_conv(x, w, b):
    k_width, channels = w.shape
    y = lax.conv_general_dilated(
        x, w[:, None, :].astype(x.dtype),
        window_strides=(1,), padding=[(k_width - 1, 0)],
        dimension_numbers=("NWC", "WIO", "NWC"),
        feature_group_count=channels)
    return y + b.astype(x.dtype)


def stick_breaking_attention(q, k, v):
    seq = q.shape[1]
    scale = q.shape[-1] ** -0.5
    outs = []
    for start in range(0, seq, BLOCK_Q):
        end = min(start + BLOCK_Q, seq)
        qb = q[:, start:end].astype(jnp.float32)
        kb = k[:, :end].astype(jnp.float32)
        vb = v[:, :end].astype(jnp.float32)
        z = jnp.einsum("bqhd,bkhd->bhqk", qb, kb) * scale
        t_idx = start + jnp.arange(end - start)[:, None]
        s_idx = jnp.arange(end)[None, :]
        mask = s_idx < t_idx
        log_beta = jax.nn.log_sigmoid(z)
        log_one_minus = jnp.where(mask, jax.nn.log_sigmoid(-z), 0.0)
        tail = lax.cumsum(log_one_minus, axis=3, reverse=True) - log_one_minus
        weights = jnp.where(mask, jnp.exp(log_beta + tail), 0.0)
        outs.append(jnp.einsum("bhqk,bkhd->bqhd", weights, vb))
    return jnp.concatenate(outs, axis=1).astype(v.dtype)


def _fwd_setup_inputs(seed: int = 0) -> dict:
    key = jax.random.key(seed)
    ks = jax.random.split(key, 16)
    f32 = jnp.float32
    out_scale = (2.0 * DEPTH) ** -0.5

    def nrm(k, shape, s):
        return jax.random.normal(k, shape, f32) * s

    return {
        "x": jax.random.normal(ks[0], (BATCH, SEQ, D_MODEL), f32),
        "norm1_g": 1.0 + nrm(ks[1], (DEPTH, D_MODEL), 0.02),
        "w_in": nrm(ks[2], (DEPTH, D_MODEL, IN_WIDTH), D_MODEL ** -0.5),
        "q_norm_g": 1.0 + nrm(ks[3], (DEPTH, HEAD_DIM), 0.02),
        "k_norm_g": 1.0 + nrm(ks[4], (DEPTH, HEAD_DIM), 0.02),
        "conv_dw_w": nrm(ks[5], (DEPTH, CONV_KERNEL, CONV_WIDTH), CONV_KERNEL ** -0.5),
        "conv_dw_b": nrm(ks[6], (DEPTH, CONV_WIDTH), 0.01),
        "conv_ln_g": 1.0 + nrm(ks[7], (DEPTH, CONV_WIDTH), 0.02),
        "conv_ln_b": nrm(ks[8], (DEPTH, CONV_WIDTH), 0.01),
        "w_out": nrm(ks[9], (DEPTH, D_MODEL, D_MODEL), D_MODEL ** -0.5 * out_scale),
        "norm2_g": 1.0 + nrm(ks[10], (DEPTH, D_MODEL), 0.02),
        "w_up": nrm(ks[11], (DEPTH, D_MODEL, 2 * D_FF), D_MODEL ** -0.5),
        "ffn_dw_w": nrm(ks[12], (DEPTH, FFN_KERNEL, D_FF), FFN_KERNEL ** -0.5),
        "ffn_dw_b": nrm(ks[13], (DEPTH, D_FF), 0.01),
        "w_down": nrm(ks[14], (DEPTH, D_FF, D_MODEL), D_FF ** -0.5 * out_scale),
    }


def _fwd_reference(x, norm1_g, w_in, q_norm_g, k_norm_g, conv_dw_w, conv_dw_b,
              conv_ln_g, conv_ln_b, w_out, norm2_g, w_up, ffn_dw_w, ffn_dw_b,
              w_down):
    bsz, seq, _ = x.shape
    splits = [ATTN_WIDTH, 2 * ATTN_WIDTH, 3 * ATTN_WIDTH, 3 * ATTN_WIDTH + CONV_WIDTH]
    for layer in range(DEPTH):
        h = rms_norm(x, norm1_g[layer])
        proj = h @ w_in[layer]
        q, k, v, glu_a, glu_b = jnp.split(proj, splits, axis=-1)
        q = rms_norm(q.reshape(bsz, seq, ATTN_HEADS, HEAD_DIM), q_norm_g[layer])
        k = rms_norm(k.reshape(bsz, seq, ATTN_HEADS, HEAD_DIM), k_norm_g[layer])
        v = v.reshape(bsz, seq, ATTN_HEADS, HEAD_DIM)
        attn = stick_breaking_attention(q, k, v).reshape(bsz, seq, ATTN_WIDTH)

        c = glu_a * jax.nn.sigmoid(glu_b)
        c = causal_depthwise_conv(c, conv_dw_w[layer], conv_dw_b[layer])
        c = jax.nn.silu(layer_norm(c, conv_ln_g[layer], conv_ln_b[layer]))

        mixed = jnp.concatenate([attn, c], axis=-1) @ w_out[layer]
        x = x + mixed

        h = rms_norm(x, norm2_g[layer])
        gate, val = jnp.split(h @ w_up[layer], 2, axis=-1)
        gate = jax.nn.silu(causal_depthwise_conv(gate, ffn_dw_w[layer], ffn_dw_b[layer]))
        x = x + (gate * val) @ w_down[layer]
    return x


import jax as _jax
import jax.numpy as _jnp

TWIN_FORMAT = 'train_step'
FWD_PARAMS = ['x', 'norm1_g', 'w_in', 'q_norm_g', 'k_norm_g', 'conv_dw_w', 'conv_dw_b', 'conv_ln_g', 'conv_ln_b', 'w_out', 'norm2_g', 'w_up', 'ffn_dw_w', 'ffn_dw_b', 'w_down']
TWIN_WEIGHTS = ['norm1_g', 'w_in', 'q_norm_g', 'k_norm_g', 'conv_dw_w', 'conv_dw_b', 'conv_ln_g', 'conv_ln_b', 'w_out', 'norm2_g', 'w_up', 'ffn_dw_w', 'ffn_dw_b', 'w_down']
TWIN_DIFF_INPUT = 'x'
TWIN_INPUTS = ['x', 'norm1_g', 'w_in', 'q_norm_g', 'k_norm_g', 'conv_dw_w', 'conv_dw_b', 'conv_ln_g', 'conv_ln_b', 'w_out', 'norm2_g', 'w_up', 'ffn_dw_w', 'ffn_dw_b', 'w_down', 'loss_target', 'm_norm1_g', 'm_w_in', 'm_q_norm_g', 'm_k_norm_g', 'm_conv_dw_w', 'm_conv_dw_b', 'm_conv_ln_g', 'm_conv_ln_b', 'm_w_out', 'm_norm2_g', 'm_w_up', 'm_ffn_dw_w', 'm_ffn_dw_b', 'm_w_down', 'v_norm1_g', 'v_w_in', 'v_q_norm_g', 'v_k_norm_g', 'v_conv_dw_w', 'v_conv_dw_b', 'v_conv_ln_g', 'v_conv_ln_b', 'v_w_out', 'v_norm2_g', 'v_w_up', 'v_ffn_dw_w', 'v_ffn_dw_b', 'v_w_down']
TWIN_OUTPUTS = ['loss', 'grad_x', 'grad_norm1_g', 'grad_w_in', 'grad_q_norm_g', 'grad_k_norm_g', 'grad_conv_dw_w', 'grad_conv_dw_b', 'grad_conv_ln_g', 'grad_conv_ln_b', 'grad_w_out', 'grad_norm2_g', 'grad_w_up', 'grad_ffn_dw_w', 'grad_ffn_dw_b', 'grad_w_down', 'delta_norm1_g', 'delta_w_in', 'delta_q_norm_g', 'delta_k_norm_g', 'delta_conv_dw_w', 'delta_conv_dw_b', 'delta_conv_ln_g', 'delta_conv_ln_b', 'delta_w_out', 'delta_norm2_g', 'delta_w_up', 'delta_ffn_dw_w', 'delta_ffn_dw_b', 'delta_w_down', 'new_m_norm1_g', 'new_m_w_in', 'new_m_q_norm_g', 'new_m_k_norm_g', 'new_m_conv_dw_w', 'new_m_conv_dw_b', 'new_m_conv_ln_g', 'new_m_conv_ln_b', 'new_m_w_out', 'new_m_norm2_g', 'new_m_w_up', 'new_m_ffn_dw_w', 'new_m_ffn_dw_b', 'new_m_w_down', 'new_v_norm1_g', 'new_v_w_in', 'new_v_q_norm_g', 'new_v_k_norm_g', 'new_v_conv_dw_w', 'new_v_conv_dw_b', 'new_v_conv_ln_g', 'new_v_conv_ln_b', 'new_v_w_out', 'new_v_norm2_g', 'new_v_w_up', 'new_v_ffn_dw_w', 'new_v_ffn_dw_b', 'new_v_w_down']
TWIN_LEAF_KINDS = {'loss': 'loss', 'grad_x': 'grad_x', 'grad_norm1_g': 'grad_w', 'grad_w_in': 'grad_w', 'grad_q_norm_g': 'grad_w', 'grad_k_norm_g': 'grad_w', 'grad_conv_dw_w': 'grad_w', 'grad_conv_dw_b': 'grad_w', 'grad_conv_ln_g': 'grad_w', 'grad_conv_ln_b': 'grad_w', 'grad_w_out': 'grad_w', 'grad_norm2_g': 'grad_w', 'grad_w_up': 'grad_w', 'grad_ffn_dw_w': 'grad_w', 'grad_ffn_dw_b': 'grad_w', 'grad_w_down': 'grad_w', 'delta_norm1_g': 'delta_w', 'delta_w_in': 'delta_w', 'delta_q_norm_g': 'delta_w', 'delta_k_norm_g': 'delta_w', 'delta_conv_dw_w': 'delta_w', 'delta_conv_dw_b': 'delta_w', 'delta_conv_ln_g': 'delta_w', 'delta_conv_ln_b': 'delta_w', 'delta_w_out': 'delta_w', 'delta_norm2_g': 'delta_w', 'delta_w_up': 'delta_w', 'delta_ffn_dw_w': 'delta_w', 'delta_ffn_dw_b': 'delta_w', 'delta_w_down': 'delta_w', 'new_m_norm1_g': 'new_m', 'new_m_w_in': 'new_m', 'new_m_q_norm_g': 'new_m', 'new_m_k_norm_g': 'new_m', 'new_m_conv_dw_w': 'new_m', 'new_m_conv_dw_b': 'new_m', 'new_m_conv_ln_g': 'new_m', 'new_m_conv_ln_b': 'new_m', 'new_m_w_out': 'new_m', 'new_m_norm2_g': 'new_m', 'new_m_w_up': 'new_m', 'new_m_ffn_dw_w': 'new_m', 'new_m_ffn_dw_b': 'new_m', 'new_m_w_down': 'new_m', 'new_v_norm1_g': 'new_v', 'new_v_w_in': 'new_v', 'new_v_q_norm_g': 'new_v', 'new_v_k_norm_g': 'new_v', 'new_v_conv_dw_w': 'new_v', 'new_v_conv_dw_b': 'new_v', 'new_v_conv_ln_g': 'new_v', 'new_v_conv_ln_b': 'new_v', 'new_v_w_out': 'new_v', 'new_v_norm2_g': 'new_v', 'new_v_w_up': 'new_v', 'new_v_ffn_dw_w': 'new_v', 'new_v_ffn_dw_b': 'new_v', 'new_v_w_down': 'new_v'}


def _forward(args):
    return _fwd_reference(*[args[k] for k in FWD_PARAMS])


def _output_shape():
    out = _jax.eval_shape(lambda: _forward(_fwd_setup_inputs(0)))
    return out.shape, out.dtype

N_MICROBATCH = 1
ADAM_LR = 0.001
ADAM_B1 = 0.9
ADAM_B2 = 0.999
ADAM_EPS = 1e-08
ADAM_WD = 0.01
ADAM_STEP = 10
PER_EXAMPLE_BATCH_AXIS = {'x': 0, 'loss_target': 0}
SHARED_INPUTS = []
_WEIGHT_DTYPES = {'norm1_g': _jnp.float32, 'w_in': _jnp.float32, 'q_norm_g': _jnp.float32, 'k_norm_g': _jnp.float32, 'conv_dw_w': _jnp.float32, 'conv_dw_b': _jnp.float32, 'conv_ln_g': _jnp.float32, 'conv_ln_b': _jnp.float32, 'w_out': _jnp.float32, 'norm2_g': _jnp.float32, 'w_up': _jnp.float32, 'ffn_dw_w': _jnp.float32, 'ffn_dw_b': _jnp.float32, 'w_down': _jnp.float32}
MOMENT_SCALE = {'norm1_g': 4.053447e-01, 'w_in': 4.278684e-02, 'q_norm_g': 9.752701e-01, 'k_norm_g': 9.746289e-01, 'conv_dw_w': 6.210743e-02, 'conv_dw_b': 6.078560e-01, 'conv_ln_g': 8.966312e-01, 'conv_ln_b': 6.470798e-01, 'w_out': 2.896040e-01, 'norm2_g': 1.610498e+00, 'w_up': 3.659027e-02, 'ffn_dw_w': 1.736933e-01, 'ffn_dw_b': 2.178176e-01, 'w_down': 1.685574e-01}


def _to_microbatches(a, axis):
    t = _jnp.moveaxis(a, axis, 0)
    t = t.reshape((N_MICROBATCH, t.shape[0] // N_MICROBATCH) + t.shape[1:])
    return _jnp.moveaxis(t, 1, axis + 1)


def setup_inputs(seed: int = 0) -> dict:
    inp = _fwd_setup_inputs(seed)
    key = _jax.random.fold_in(_jax.random.key(seed), 7919)
    shape, _ = _output_shape()
    out = dict(inp)
    out["loss_target"] = _jax.random.normal(_jax.random.fold_in(key, 0), shape, _jnp.float32)
    for i, name in enumerate(TWIN_WEIGHTS):
        w = inp[name].astype(_jnp.float32)
        if MOMENT_SCALE is None:
            s = _jnp.sqrt(_jnp.mean(_jnp.square(w)) + 1e-30)
        else:
            s = MOMENT_SCALE[name]
        km, kv = _jax.random.split(_jax.random.fold_in(key, i + 1))
        out[name] = w
        out["m_" + name] = s * _jax.random.normal(km, w.shape, _jnp.float32)
        out["v_" + name] = (s * s) * _jax.random.uniform(kv, w.shape, _jnp.float32, 0.5, 1.5)
    if N_MICROBATCH > 1:
        for name, axis in PER_EXAMPLE_BATCH_AXIS.items():
            out[name] = _to_microbatches(out[name], axis)
    return {'x': out['x'], 'norm1_g': out['norm1_g'], 'w_in': out['w_in'], 'q_norm_g': out['q_norm_g'], 'k_norm_g': out['k_norm_g'], 'conv_dw_w': out['conv_dw_w'], 'conv_dw_b': out['conv_dw_b'], 'conv_ln_g': out['conv_ln_g'], 'conv_ln_b': out['conv_ln_b'], 'w_out': out['w_out'], 'norm2_g': out['norm2_g'], 'w_up': out['w_up'], 'ffn_dw_w': out['ffn_dw_w'], 'ffn_dw_b': out['ffn_dw_b'], 'w_down': out['w_down'], 'loss_target': out['loss_target'], 'm_norm1_g': out['m_norm1_g'], 'm_w_in': out['m_w_in'], 'm_q_norm_g': out['m_q_norm_g'], 'm_k_norm_g': out['m_k_norm_g'], 'm_conv_dw_w': out['m_conv_dw_w'], 'm_conv_dw_b': out['m_conv_dw_b'], 'm_conv_ln_g': out['m_conv_ln_g'], 'm_conv_ln_b': out['m_conv_ln_b'], 'm_w_out': out['m_w_out'], 'm_norm2_g': out['m_norm2_g'], 'm_w_up': out['m_w_up'], 'm_ffn_dw_w': out['m_ffn_dw_w'], 'm_ffn_dw_b': out['m_ffn_dw_b'], 'm_w_down': out['m_w_down'], 'v_norm1_g': out['v_norm1_g'], 'v_w_in': out['v_w_in'], 'v_q_norm_g': out['v_q_norm_g'], 'v_k_norm_g': out['v_k_norm_g'], 'v_conv_dw_w': out['v_conv_dw_w'], 'v_conv_dw_b': out['v_conv_dw_b'], 'v_conv_ln_g': out['v_conv_ln_g'], 'v_conv_ln_b': out['v_conv_ln_b'], 'v_w_out': out['v_w_out'], 'v_norm2_g': out['v_norm2_g'], 'v_w_up': out['v_w_up'], 'v_ffn_dw_w': out['v_ffn_dw_w'], 'v_ffn_dw_b': out['v_ffn_dw_b'], 'v_w_down': out['v_w_down']}


def _loss(weights, diff, rest, loss_target):
    with _jax.named_scope("forward"):
        args = {**rest, TWIN_DIFF_INPUT: diff, **{k: w.astype(_WEIGHT_DTYPES[k]) for k, w in weights.items()}}
        y = _forward(args)
    with _jax.named_scope("loss_head"):
        err = _jnp.square(y.astype(_jnp.float32) - loss_target)
        return 0.5 * _jnp.sum(_jnp.mean(err, axis=-1)) if err.ndim else 0.5 * err


def _adamw(w, g, m, v):
    m = ADAM_B1 * m + (1.0 - ADAM_B1) * g
    v = ADAM_B2 * v + (1.0 - ADAM_B2) * _jnp.square(g)
    m_hat = m / (1.0 - ADAM_B1 ** ADAM_STEP)
    v_hat = v / (1.0 - ADAM_B2 ** ADAM_STEP)
    delta = -ADAM_LR * (m_hat / (_jnp.sqrt(v_hat) + ADAM_EPS) + ADAM_WD * w)
    return delta, m, v


def reference(x, norm1_g, w_in, q_norm_g, k_norm_g, conv_dw_w, conv_dw_b, conv_ln_g, conv_ln_b, w_out, norm2_g, w_up, ffn_dw_w, ffn_dw_b, w_down, loss_target, m_norm1_g, m_w_in, m_q_norm_g, m_k_norm_g, m_conv_dw_w, m_conv_dw_b, m_conv_ln_g, m_conv_ln_b, m_w_out, m_norm2_g, m_w_up, m_ffn_dw_w, m_ffn_dw_b, m_w_down, v_norm1_g, v_w_in, v_q_norm_g, v_k_norm_g, v_conv_dw_w, v_conv_dw_b, v_conv_ln_g, v_conv_ln_b, v_w_out, v_norm2_g, v_w_up, v_ffn_dw_w, v_ffn_dw_b, v_w_down):
    given = dict(x=x, norm1_g=norm1_g, w_in=w_in, q_norm_g=q_norm_g, k_norm_g=k_norm_g, conv_dw_w=conv_dw_w, conv_dw_b=conv_dw_b, conv_ln_g=conv_ln_g, conv_ln_b=conv_ln_b, w_out=w_out, norm2_g=norm2_g, w_up=w_up, ffn_dw_w=ffn_dw_w, ffn_dw_b=ffn_dw_b, w_down=w_down, loss_target=loss_target, m_norm1_g=m_norm1_g, m_w_in=m_w_in, m_q_norm_g=m_q_norm_g, m_k_norm_g=m_k_norm_g, m_conv_dw_w=m_conv_dw_w, m_conv_dw_b=m_conv_dw_b, m_conv_ln_g=m_conv_ln_g, m_conv_ln_b=m_conv_ln_b, m_w_out=m_w_out, m_norm2_g=m_norm2_g, m_w_up=m_w_up, m_ffn_dw_w=m_ffn_dw_w, m_ffn_dw_b=m_ffn_dw_b, m_w_down=m_w_down, v_norm1_g=v_norm1_g, v_w_in=v_w_in, v_q_norm_g=v_q_norm_g, v_k_norm_g=v_k_norm_g, v_conv_dw_w=v_conv_dw_w, v_conv_dw_b=v_conv_dw_b, v_conv_ln_g=v_conv_ln_g, v_conv_ln_b=v_conv_ln_b, v_w_out=v_w_out, v_norm2_g=v_norm2_g, v_w_up=v_w_up, v_ffn_dw_w=v_ffn_dw_w, v_ffn_dw_b=v_ffn_dw_b, v_w_down=v_w_down)
    weights = {n: given[n] for n in TWIN_WEIGHTS}
    shared = {n: given[n] for n in SHARED_INPUTS}
    per_example = {n: given[n] for n in ['x']}
    grad_fn = _jax.value_and_grad(_loss, argnums=(0, 1))

    def one_microbatch(ex, loss_target):
        ex = dict(ex)
        diff = ex.pop(TWIN_DIFF_INPUT)
        return grad_fn(weights, diff, {**shared, **ex}, loss_target)

    if N_MICROBATCH == 1:
        loss, (grad_w, grad_x) = one_microbatch(per_example, given["loss_target"])
    else:
        def body(carry, xs):
            loss_sum, grad_sum = carry
            l_k, (gw_k, gx_k) = one_microbatch(xs[0], xs[1])
            with _jax.named_scope("update"):
                return (loss_sum + l_k, _jax.tree.map(_jnp.add, grad_sum, gw_k)), gx_k

        init = (_jnp.zeros((), _jnp.float32), _jax.tree.map(_jnp.zeros_like, weights))
        (loss, grad_w), grad_x = _jax.lax.scan(body, init, (per_example, given["loss_target"]))
    with _jax.named_scope("update"):
        delta_w, new_m, new_v = {}, {}, {}
        for n in TWIN_WEIGHTS:
            delta_w[n], new_m[n], new_v[n] = _adamw(weights[n], grad_w[n], given["m_" + n], given["v_" + n])
    return (loss, grad_x, *[grad_w[n] for n in TWIN_WEIGHTS], *[delta_w[n] for n in TWIN_WEIGHTS],
            *[new_m[n] for n in TWIN_WEIGHTS], *[new_v[n] for n in TWIN_WEIGHTS])
```

```python
import functools

import jax
import jax.numpy as jnp
from jax import lax
from jax.experimental import pallas as pl
from jax.experimental.pallas import tpu as pltpu

F32 = jnp.float32
BF16 = jnp.bfloat16

DEPTH = 4
D_MODEL = 1024
HEADS = 8
HEAD_DIM = 64
ATTN_WIDTH = HEADS * HEAD_DIM
CONV_WIDTH = D_MODEL - ATTN_WIDTH
CONV_KERNEL = 31
D_FF = 2816
FFN_KERNEL = 3
EPS = 1e-6
ADAM_LR, ADAM_B1, ADAM_B2, ADAM_EPS, ADAM_WD, ADAM_STEP = 0.001, 0.9, 0.999, 1e-08, 0.01, 10

N_CHIPS = 4
N_DEV = 8
LANES = 128
SUBLANES = 8
VMEM_LIMIT_BYTES = 56 * 2**20
ATTN_TILE = 256
CONV_PAD = 32
FFN_PAD = 8
MESH = pl.DeviceIdType.MESH


def _params(*sem):
    return pltpu.CompilerParams(dimension_semantics=sem if sem else None, vmem_limit_bytes=VMEM_LIMIT_BYTES)


def _dot(a, b):
    return jnp.dot(a, b, preferred_element_type=F32)


def _dot_nt(a, b):
    return lax.dot_general(a, b, (((1,), (1,)), ((), ())), preferred_element_type=F32)


def _dot_tn(a, b):
    return lax.dot_general(a, b, (((0,), (0,)), ((), ())), preferred_element_type=F32)


def _sigmoid(x):
    return 1.0 / (1.0 + jnp.exp(-x))


def _norm_matmul(x, g, w, *, tm=512, n_split=2):
    S, D = x.shape
    N = w.shape[1]
    tn = N // n_split

    def body(x_ref, g_ref, w_ref, h_ref, y_ref):
        @pl.when(pl.program_id(1) == 0)
        def _():
            xv = x_ref[...]
            r = lax.rsqrt(jnp.mean(xv * xv, axis=-1, keepdims=True) + EPS)
            h_ref[...] = (xv * r * g_ref[...]).astype(BF16)

        y_ref[...] = _dot(h_ref[...], w_ref[...])

    return pl.pallas_call(
        body, name="norm_matmul", grid=(S // tm, n_split),
        in_specs=[pl.BlockSpec((tm, D), lambda i, j: (i, 0)),
                  pl.BlockSpec((1, D), lambda i, j: (0, 0)),
                  pl.BlockSpec((D, tn), lambda i, j: (0, j))],
        out_specs=[pl.BlockSpec((tm, D), lambda i, j: (i, 0)),
                   pl.BlockSpec((tm, tn), lambda i, j: (i, j))],
        out_shape=[jax.ShapeDtypeStruct((S, D), BF16), jax.ShapeDtypeStruct((S, N), F32)],
        compiler_params=_params("parallel", "arbitrary"),
    )(x, g, w)


def _matmul_res(pieces, w, res, *, tm=512):
    S, N = res.shape
    K = w.shape[0]
    widths = [p.shape[1] for p in pieces]
    assert sum(widths) == K

    def body(*refs):
        p_refs, (w_ref, res_ref, o_ref) = refs[:len(pieces)], refs[len(pieces):]
        acc = res_ref[...]
        off = 0
        for p_ref, kp in zip(p_refs, widths):
            acc = acc + _dot(p_ref[...], w_ref[off:off + kp, :])
            off += kp
        o_ref[...] = acc

    return pl.pallas_call(
        body, name="matmul_res", grid=(S // tm,),
        in_specs=[pl.BlockSpec((tm, kp), lambda i: (i, 0)) for kp in widths]
        + [pl.BlockSpec((K, N), lambda i: (0, 0)), pl.BlockSpec((tm, N), lambda i: (i, 0))],
        out_specs=pl.BlockSpec((tm, N), lambda i: (i, 0)),
        out_shape=jax.ShapeDtypeStruct((S, N), F32),
        compiler_params=_params("parallel"),
    )(*pieces, w, res)


def _nt_sum(p_refs, widths, w_ref):
    acc = None
    off = 0
    for p_ref, n_p in zip(p_refs, widths):
        d = _dot_nt(p_ref[...].astype(BF16), w_ref[:, off:off + n_p])
        acc = d if acc is None else acc + d
        off += n_p
    return acc


def _matmul_nt(pieces, w, out_dtype, *, tm=512):
    S = pieces[0].shape[0]
    K, N = w.shape
    widths = [p.shape[1] for p in pieces]
    assert sum(widths) == N

    def body(*refs):
        p_refs, (w_ref, o_ref) = refs[:len(pieces)], refs[len(pieces):]
        o_ref[...] = _nt_sum(p_refs, widths, w_ref).astype(out_dtype)

    return pl.pallas_call(
        body, name="matmul_nt", grid=(S // tm,),
        in_specs=[pl.BlockSpec((tm, n_p), lambda i: (i, 0)) for n_p in widths]
        + [pl.BlockSpec((K, N), lambda i: (0, 0))],
        out_specs=pl.BlockSpec((tm, K), lambda i: (i, 0)),
        out_shape=jax.ShapeDtypeStruct((S, K), out_dtype),
        compiler_params=_params("parallel"),
    )(*pieces, w)


def _matmul_nt_rmsbwd(pieces, w, x, g, dres, *, tm=256):
    S, K = x.shape
    N = w.shape[1]
    widths = [p.shape[1] for p in pieces]
    assert sum(widths) == N

    def body(*refs):
        p_refs, (w_ref, x_ref, g_ref, dres_ref, dx_ref, dg_ref) = refs[:len(pieces)], refs[len(pieces):]
        dh = _nt_sum(p_refs, widths, w_ref)
        xv = x_ref[...]
        r = lax.rsqrt(jnp.mean(xv * xv, axis=-1, keepdims=True) + EPS)
        xh = xv * r
        dxh = dh * g_ref[...]
        dx_ref[...] = dres_ref[...] + r * (dxh - xh * jnp.mean(dxh * xh, axis=-1, keepdims=True))

        @pl.when(pl.program_id(0) == 0)
        def _():
            dg_ref[...] = jnp.zeros_like(dg_ref)

        dg_ref[...] += jnp.sum(dh * xh, axis=0, keepdims=True)

    return pl.pallas_call(
        body, name="matmul_nt_rmsbwd", grid=(S // tm,),
        in_specs=[pl.BlockSpec((tm, n_p), lambda i: (i, 0)) for n_p in widths]
        + [pl.BlockSpec((K, N), lambda i: (0, 0)), pl.BlockSpec((tm, K), lambda i: (i, 0)),
           pl.BlockSpec((1, K), lambda i: (0, 0)), pl.BlockSpec((tm, K), lambda i: (i, 0))],
        out_specs=[pl.BlockSpec((tm, K), lambda i: (i, 0)), pl.BlockSpec((1, K), lambda i: (0, 0))],
        out_shape=[jax.ShapeDtypeStruct((S, K), F32), jax.ShapeDtypeStruct((1, K), F32)],
        compiler_params=_params("arbitrary"),
    )(*pieces, w, x, g, dres)


def _matmul_tn(x, dy, *, tk, tn, ts=512):
    S, M = x.shape
    N = dy.shape[1]
    n_s = S // ts

    def body(x_ref, dy_ref, o_ref, acc_ref):
        s = pl.program_id(2)

        @pl.when(s == 0)
        def _():
            acc_ref[...] = jnp.zeros_like(acc_ref)

        acc_ref[...] += _dot_tn(x_ref[...].astype(BF16), dy_ref[...].astype(BF16))

        @pl.when(s == n_s - 1)
        def _():
            o_ref[...] = acc_ref[...].astype(BF16)

    return pl.pallas_call(
        body, name="matmul_tn", grid=(M // tk, N // tn, n_s),
        in_specs=[pl.BlockSpec((ts, tk), lambda i, j, s: (s, i)),
                  pl.BlockSpec((ts, tn), lambda i, j, s: (s, j))],
        out_specs=pl.BlockSpec((tk, tn), lambda i, j, s: (i, j)),
        out_shape=jax.ShapeDtypeStruct((M, N), BF16),
        scratch_shapes=[pltpu.VMEM((tk, tn), F32)],
        compiler_params=_params("parallel", "parallel", "arbitrary"),
    )(x, dy)


def _tri_consts():
    j = jnp.arange(ATTN_TILE)[:, None]
    s = jnp.arange(ATTN_TILE)[None, :]

    def two(m):
        return jnp.concatenate([m, m], axis=0).astype(BF16)

    return two(j > s), two(j <= s), two(j < s)


def _split_hi_lo(a):
    hi = a.astype(BF16)
    lo = (a - hi.astype(F32)).astype(BF16)
    return jnp.concatenate([hi, lo], axis=1)


def _log_terms(s):
    lom = -(jnp.maximum(s, 0.0) + jnp.log(1.0 + jnp.exp(-jnp.abs(s))))
    return lom, lom + s


def _causal_mask():
    t = lax.broadcasted_iota(jnp.int32, (ATTN_TILE, ATTN_TILE), 0)
    s = lax.broadcasted_iota(jnp.int32, (ATTN_TILE, ATTN_TILE), 1)
    return s < t


def _attn_prep(h, q_ref, k_ref, v_ref, qg_ref, kg_ref, qn_s, kn_s, vb_s, n_tiles):
    T = ATTN_TILE
    lanes = slice(HEAD_DIM * h, HEAD_DIM * (h + 1))
    scale = HEAD_DIM ** -0.5

    def prep(i, carry):
        rows = pl.ds(pl.multiple_of(i * T, T), T)
        q = q_ref[rows, lanes]
        k = k_ref[rows, lanes]
        rq = lax.rsqrt(jnp.mean(q * q, axis=-1, keepdims=True) + EPS)
        rk = lax.rsqrt(jnp.mean(k * k, axis=-1, keepdims=True) + EPS)
        qn_s[h, rows, :] = (q * rq * qg_ref[...] * scale).astype(BF16)
        kn_s[h, rows, :] = (k * rk * kg_ref[...]).astype(BF16)
        vb_s[h, rows, :] = v_ref[rows, lanes].astype(BF16)
        return carry

    lax.fori_loop(0, n_tiles, prep, 0)


def _attn_fwd(proj, qg, kg):
    S = proj.shape[0]
    T = ATTN_TILE
    n_tiles = S // T
    suffix, _, _ = _tri_consts()
    pairs = HEADS // 2
    q_blk, k_blk, v_blk = 0, ATTN_WIDTH // LANES, 2 * ATTN_WIDTH // LANES

    def body(q_ref, k_ref, v_ref, qg_ref, kg_ref, tri_ref, o_ref, lt_ref, qn_s, kn_s, vb_s):
        for h in range(2):
            lanes = slice(HEAD_DIM * h, HEAD_DIM * (h + 1))
            _attn_prep(h, q_ref, k_ref, v_ref, qg_ref, kg_ref, qn_s, kn_s, vb_s, n_tiles)

            def q_tile(qi, carry0):
                qrows = pl.ds(pl.multiple_of(qi * T, T), T)
                qt = qn_s[h, qrows, :]

                def tile(kj, carry, diag):
                    acc, c = carry
                    krows = pl.ds(pl.multiple_of(kj * T, T), T)
                    s = _dot_nt(qt, kn_s[h, krows, :])
                    lom, lb = _log_terms(s)
                    if diag:
                        mask = _causal_mask()
                        lom = jnp.where(mask, lom, 0.0)
                    tail = _dot(_split_hi_lo(lom), tri_ref[...])
                    tot = tail[:, 0:1] + lom[:, 0:1]
                    w = jnp.exp(lb + tail + c)
                    if diag:
                        w = jnp.where(mask, w, 0.0)
                    acc = acc + _dot(w.astype(BF16), vb_s[h, krows, :])
                    return acc, c + tot

                carry = tile(qi, (jnp.zeros((T, HEAD_DIM), F32), jnp.zeros((T, 1), F32)), True)
                acc, c = lax.fori_loop(0, qi, lambda t, cr: tile(qi - 1 - t, cr, False), carry)
                o_ref[qrows, lanes] = acc.astype(BF16)
                lt_ref[qrows, lanes] = jnp.broadcast_to(c, (T, HEAD_DIM))
                return carry0

            lax.fori_loop(0, n_tiles, q_tile, 0)

    return pl.pallas_call(
        body, name="attn_fwd", grid=(pairs,),
        in_specs=[pl.BlockSpec((S, LANES), lambda p: (0, q_blk + p)),
                  pl.BlockSpec((S, LANES), lambda p: (0, k_blk + p)),
                  pl.BlockSpec((S, LANES), lambda p: (0, v_blk + p)),
                  pl.BlockSpec((1, HEAD_DIM), lambda p: (0, 0)),
                  pl.BlockSpec((1, HEAD_DIM), lambda p: (0, 0)),
                  pl.BlockSpec((2 * T, T), lambda p: (0, 0))],
        out_specs=[pl.BlockSpec((S, LANES), lambda p: (0, p)),
                   pl.BlockSpec((None, S, LANES), lambda p: (p, 0, 0))],
        out_shape=[jax.ShapeDtypeStruct((S, ATTN_WIDTH), BF16),
                   jax.ShapeDtypeStruct((pairs, S, LANES), F32)],
        scratch_shapes=[pltpu.VMEM((2, S, HEAD_DIM), BF16)] * 3,
        compiler_params=_params("parallel"),
    )(proj, proj, proj, qg, kg, suffix)


def _attn_bwd(proj, qg, kg, ltot, dmix):
    S = proj.shape[0]
    T = ATTN_TILE
    n_tiles = S // T
    _, prefix_incl, prefix_excl = _tri_consts()
    pairs = HEADS // 2
    q_blk, k_blk, v_blk = 0, ATTN_WIDTH // LANES, 2 * ATTN_WIDTH // LANES
    scale = HEAD_DIM ** -0.5

    def body(q_ref, k_ref, v_ref, qg_ref, kg_ref, lt_ref, do_ref, ti_ref, te_ref,
             dq_ref, dk_ref, dv_ref, dqg_ref, dkg_ref, qn_s, kn_s, vb_s, dq_s, dk_s, dv_s):
        @pl.when(pl.program_id(0) == 0)
        def _():
            dqg_ref[...] = jnp.zeros_like(dqg_ref)
            dkg_ref[...] = jnp.zeros_like(dkg_ref)

        for h in range(2):
            lanes = slice(HEAD_DIM * h, HEAD_DIM * (h + 1))
            _attn_prep(h, q_ref, k_ref, v_ref, qg_ref, kg_ref, qn_s, kn_s, vb_s, n_tiles)
            dk_s[...] = jnp.zeros_like(dk_s)
            dv_s[...] = jnp.zeros_like(dv_s)

            def q_tile(qi, carry0):
                qrows = pl.ds(pl.multiple_of(qi * T, T), T)
                qt = qn_s[h, qrows, :]
                dob = do_ref[qrows, lanes].astype(BF16)
                lt = lt_ref[qrows, HEAD_DIM * h:HEAD_DIM * h + 1]

                def tile(kj, carry, diag):
                    dq, p_lom, p_e = carry
                    krows = pl.ds(pl.multiple_of(kj * T, T), T)
                    kt = kn_s[h, krows, :]
                    s = _dot_nt(qt, kt)
                    lom, lb = _log_terms(s)
                    if diag:
                        mask = _causal_mask()
                        lom = jnp.where(mask, lom, 0.0)
                    pin = _dot(_split_hi_lo(lom), ti_ref[...])
                    w = jnp.exp(lb + ((lt - p_lom) - pin))
                    if diag:
                        w = jnp.where(mask, w, 0.0)
                    e = w * _dot_nt(dob, vb_s[h, krows, :])
                    gex = _dot(_split_hi_lo(e), te_ref[...])
                    dz = e - jnp.exp(lb) * (e + (p_e + gex))
                    if diag:
                        dz = jnp.where(mask, dz, 0.0)
                    dzb = dz.astype(BF16)
                    dq = dq + _dot(dzb, kt)
                    dk_s[krows, :] += _dot_tn(dzb, qt)
                    dv_s[krows, :] += _dot_tn(w.astype(BF16), dob)
                    p_lom = p_lom + pin[:, T - 1:T]
                    p_e = p_e + gex[:, T - 1:T] + e[:, T - 1:T]
                    return dq, p_lom, p_e

                zero = jnp.zeros((T, 1), F32)
                carry = lax.fori_loop(0, qi, lambda kj, cr: tile(kj, cr, False),
                                      (jnp.zeros((T, HEAD_DIM), F32), zero, zero))
                dq_s[qrows, :] = tile(qi, carry, True)[0]
                return carry0

            lax.fori_loop(0, n_tiles, q_tile, 0)

            def finish(i, carry):
                dqg, dkg = carry
                rows = pl.ds(pl.multiple_of(i * T, T), T)
                q = q_ref[rows, lanes]
                k = k_ref[rows, lanes]
                rq = lax.rsqrt(jnp.mean(q * q, axis=-1, keepdims=True) + EPS)
                rk = lax.rsqrt(jnp.mean(k * k, axis=-1, keepdims=True) + EPS)
                qh = q * rq
                kh = k * rk
                dqn = dq_s[rows, :] * scale
                dkn = dk_s[rows, :]
                dqh = dqn * qg_ref[...]
                dkh = dkn * kg_ref[...]
                dq_ref[rows, lanes] = (rq * (dqh - qh * jnp.mean(dqh * qh, axis=-1, keepdims=True))).astype(BF16)
                dk_ref[rows, lanes] = (rk * (dkh - kh * jnp.mean(dkh * kh, axis=-1, keepdims=True))).astype(BF16)
                dv_ref[rows, lanes] = dv_s[rows, :].astype(BF16)
                return (dqg + jnp.sum(dqn * qh, axis=0, keepdims=True),
                        dkg + jnp.sum(dkn * kh, axis=0, keepdims=True))

            zero = jnp.zeros((1, HEAD_DIM), F32)
            dqg, dkg = lax.fori_loop(0, n_tiles, finish, (zero, zero))
            dqg_ref[...] += dqg
            dkg_ref[...] += dkg

    blk = lambda off: pl.BlockSpec((S, LANES), lambda p: (0, off + p))
    row64 = pl.BlockSpec((1, HEAD_DIM), lambda p: (0, 0))
    tri = pl.BlockSpec((2 * T, T), lambda p: (0, 0))
    return pl.pallas_call(
        body, name="attn_bwd", grid=(pairs,),
        in_specs=[blk(q_blk), blk(k_blk), blk(v_blk), row64, row64,
                  pl.BlockSpec((None, S, LANES), lambda p: (p, 0, 0)), blk(0), tri, tri],
        out_specs=[blk(0), blk(0), blk(0), row64, row64],
        out_shape=[jax.ShapeDtypeStruct((S, ATTN_WIDTH), BF16)] * 3 + [jax.ShapeDtypeStruct((1, HEAD_DIM), F32)] * 2,
        scratch_shapes=[pltpu.VMEM((2, S, HEAD_DIM), BF16)] * 3 + [pltpu.VMEM((S, HEAD_DIM), F32)] * 3,
        compiler_params=_params("arbitrary"),
    )(proj, proj, proj, qg, kg, ltot, dmix, prefix_incl, prefix_excl)


def _shifted(win, n_rows):
    return [win if b == 0 else pltpu.roll(win, n_rows - b, 0) for b in range(SUBLANES)]


def _taps(variants, offsets, tm):
    return {o: variants[o % SUBLANES][(o // SUBLANES) * SUBLANES:(o // SUBLANES) * SUBLANES + tm, :] for o in offsets}


def _fold_rows(a):
    return jnp.sum(a.reshape(a.shape[0] // SUBLANES, SUBLANES, a.shape[1]), axis=0)


def _glu_conv_fwd(proj, w, bias, *, tm=256):
    S = proj.shape[0]
    CB = LANES
    a_blk, b_blk = 3 * ATTN_WIDTH // CB, (3 * ATTN_WIDTH + CONV_WIDTH) // CB
    n_rows = tm + CONV_PAD

    def body(a_ref, b_ref, w_ref, bias_ref, c1_ref, pad_s):
        pad_s[0:CONV_PAD, :] = jnp.zeros((CONV_PAD, CB), F32)

        def fill(i, carry):
            rows = pl.ds(pl.multiple_of(i * tm, tm), tm)
            pad_s[pl.ds(pl.multiple_of(CONV_PAD + i * tm, SUBLANES), tm), :] = a_ref[rows, :] * _sigmoid(b_ref[rows, :])
            return carry

        lax.fori_loop(0, S // tm, fill, 0)

        def conv(i, carry):
            r0 = pl.multiple_of(i * tm, tm)
            taps = _taps(_shifted(pad_s[pl.ds(r0, n_rows), :], n_rows), range(2, 2 + CONV_KERNEL), tm)
            acc = jnp.broadcast_to(bias_ref[...], (tm, CB))
            for k in range(CONV_KERNEL):
                acc = acc + w_ref[k:k + 1, :] * taps[k + 2]
            c1_ref[pl.ds(r0, tm), :] = acc
            return carry

        lax.fori_loop(0, S // tm, conv, 0)

    return pl.pallas_call(
        body, name="glu_conv_fwd", grid=(CONV_WIDTH // CB,),
        in_specs=[pl.BlockSpec((S, CB), lambda j: (0, a_blk + j)), pl.BlockSpec((S, CB), lambda j: (0, b_blk + j)),
                  pl.BlockSpec((CONV_KERNEL, CB), lambda j: (0, j)), pl.BlockSpec((1, CB), lambda j: (0, j))],
        out_specs=pl.BlockSpec((S, CB), lambda j: (0, j)),
        out_shape=jax.ShapeDtypeStruct((S, CONV_WIDTH), F32),
        scratch_shapes=[pltpu.VMEM((S + CONV_PAD, CB), F32)],
        compiler_params=_params("parallel"),
    )(proj, proj, w, bias)


def _glu_conv_bwd(proj, w, dc1, *, tm=256):
    S = proj.shape[0]
    CB = LANES
    a_blk, b_blk = 3 * ATTN_WIDTH // CB, (3 * ATTN_WIDTH + CONV_WIDTH) // CB
    n_rows = tm + CONV_PAD

    def body(a_ref, b_ref, w_ref, dc1_ref, da_ref, db_ref, dw_ref, dbias_ref, pad_s, dpad_s, dw_s):
        pad_s[0:CONV_PAD, :] = jnp.zeros((CONV_PAD, CB), F32)
        dpad_s[S:S + CONV_PAD, :] = jnp.zeros((CONV_PAD, CB), F32)
        dw_s[...] = jnp.zeros_like(dw_s)

        def fill(i, carry):
            rows = pl.ds(pl.multiple_of(i * tm, tm), tm)
            pad_s[pl.ds(pl.multiple_of(CONV_PAD + i * tm, SUBLANES), tm), :] = a_ref[rows, :] * _sigmoid(b_ref[rows, :])
            dpad_s[rows, :] = dc1_ref[rows, :]
            return carry

        lax.fori_loop(0, S // tm, fill, 0)

        def conv(i, carry):
            r0 = pl.multiple_of(i * tm, tm)
            rows = pl.ds(r0, tm)
            taps = _taps(_shifted(dpad_s[pl.ds(r0, n_rows), :], n_rows), range(CONV_KERNEL), tm)
            acc = jnp.zeros((tm, CB), F32)
            for k in range(CONV_KERNEL):
                acc = acc + w_ref[k:k + 1, :] * taps[CONV_KERNEL - 1 - k]
            a = a_ref[rows, :]
            sg = _sigmoid(b_ref[rows, :])
            da_ref[rows, :] = (acc * sg).astype(BF16)
            db_ref[rows, :] = (acc * a * sg * (1.0 - sg)).astype(BF16)
            d = taps[0]
            taps = _taps(_shifted(pad_s[pl.ds(r0, n_rows), :], n_rows), range(2, 2 + CONV_KERNEL), tm)
            for k in range(CONV_KERNEL):
                dw_s[SUBLANES * k:SUBLANES * (k + 1), :] += _fold_rows(d * taps[k + 2])
            dw_s[SUBLANES * CONV_KERNEL:SUBLANES * (CONV_KERNEL + 1), :] += _fold_rows(d)
            return carry

        lax.fori_loop(0, S // tm, conv, 0)
        for k in range(CONV_KERNEL):
            dw_ref[k:k + 1, :] = jnp.sum(dw_s[SUBLANES * k:SUBLANES * (k + 1), :], axis=0, keepdims=True)
        dbias_ref[...] = jnp.sum(dw_s[SUBLANES * CONV_KERNEL:SUBLANES * (CONV_KERNEL + 1), :], axis=0, keepdims=True)

    col = lambda off: pl.BlockSpec((S, CB), lambda j: (0, off + j))
    return pl.pallas_call(
        body, name="glu_conv_bwd", grid=(CONV_WIDTH // CB,),
        in_specs=[col(a_blk), col(b_blk), pl.BlockSpec((CONV_KERNEL, CB), lambda j: (0, j)), col(0)],
        out_specs=[col(0), col(0), pl.BlockSpec((CONV_KERNEL, CB), lambda j: (0, j)), pl.BlockSpec((1, CB), lambda j: (0, j))],
        out_shape=[jax.ShapeDtypeStruct((S, CONV_WIDTH), BF16)] * 2
        + [jax.ShapeDtypeStruct((CONV_KERNEL, CONV_WIDTH), F32), jax.ShapeDtypeStruct((1, CONV_WIDTH), F32)],
        scratch_shapes=[pltpu.VMEM((S + CONV_PAD, CB), F32), pltpu.VMEM((S + CONV_PAD, CB), F32),
                        pltpu.VMEM((SUBLANES * (CONV_KERNEL + 1), CB), F32)],
        compiler_params=_params("parallel"),
    )(proj, proj, w, dc1)


def _ln_stats(c1):
    mu = jnp.mean(c1, axis=-1, keepdims=True)
    xc = c1 - mu
    r = lax.rsqrt(jnp.mean(xc * xc, axis=-1, keepdims=True) + EPS)
    return xc * r, r


def _ln_silu_fwd(c1, g, b, *, tm=512):
    S, C = c1.shape

    def body(c1_ref, g_ref, b_ref, c_ref):
        yh, _ = _ln_stats(c1_ref[...])
        y = yh * g_ref[...] + b_ref[...]
        c_ref[...] = (y * _sigmoid(y)).astype(BF16)

    vec = pl.BlockSpec((1, C), lambda i: (0, 0))
    return pl.pallas_call(
        body, name="ln_silu_fwd", grid=(S // tm,),
        in_specs=[pl.BlockSpec((tm, C), lambda i: (i, 0)), vec, vec],
        out_specs=pl.BlockSpec((tm, C), lambda i: (i, 0)),
        out_shape=jax.ShapeDtypeStruct((S, C), BF16),
        compiler_params=_params("parallel"),
    )(c1, g, b)


def _ln_silu_bwd(c1, g, b, dmix, *, tm=512):
    S, C = c1.shape

    def body(c1_ref, g_ref, b_ref, dc_ref, dc1_ref, dg_ref, db_ref):
        yh, r = _ln_stats(c1_ref[...])
        y = yh * g_ref[...] + b_ref[...]
        sg = _sigmoid(y)
        dy = dc_ref[...] * (sg * (1.0 + y * (1.0 - sg)))
        dyh = dy * g_ref[...]
        dc1_ref[...] = r * (dyh - jnp.mean(dyh, axis=-1, keepdims=True)
                            - yh * jnp.mean(dyh * yh, axis=-1, keepdims=True))

        @pl.when(pl.program_id(0) == 0)
        def _():
            dg_ref[...] = jnp.zeros_like(dg_ref)
            db_ref[...] = jnp.zeros_like(db_ref)

        dg_ref[...] += jnp.sum(dy * yh, axis=0, keepdims=True)
        db_ref[...] += jnp.sum(dy, axis=0, keepdims=True)

    vec = pl.BlockSpec((1, C), lambda i: (0, 0))
    return pl.pallas_call(
        body, name="ln_silu_bwd", grid=(S // tm,),
        in_specs=[pl.BlockSpec((tm, C), lambda i: (i, 0)), vec, vec, pl.BlockSpec((tm, C), lambda i: (i, 1))],
        out_specs=[pl.BlockSpec((tm, C), lambda i: (i, 0)), vec, vec],
        out_shape=[jax.ShapeDtypeStruct((S, C), F32), jax.ShapeDtypeStruct((1, C), F32), jax.ShapeDtypeStruct((1, C), F32)],
        compiler_params=_params("arbitrary"),
    )(c1, g, b, dmix)


FFN_CB = 256


def _ffn_gate(pad_s, w_ref, bias_ref, r0, tm):
    n_rows = tm + FFN_PAD
    taps = _taps(_shifted(pad_s[pl.ds(r0, n_rows), :], n_rows), range(FFN_PAD - 2, FFN_PAD + 1), tm)
    g1 = bias_ref[...] + w_ref[0:1, :] * taps[6] + w_ref[1:2, :] * taps[7] + w_ref[2:3, :] * taps[8]
    return g1, taps


def _ffn_act_fwd(u, w, bias, *, tm=256):
    S = u.shape[0]
    CB = FFN_CB
    nb = D_FF // CB

    def body(g_ref, v_ref, w_ref, bias_ref, o_ref, pad_s):
        pad_s[0:FFN_PAD, :] = jnp.zeros((FFN_PAD, CB), F32)

        def fill(i, carry):
            pad_s[pl.ds(pl.multiple_of(FFN_PAD + i * tm, SUBLANES), tm), :] = g_ref[pl.ds(pl.multiple_of(i * tm, tm), tm), :]
            return carry

        lax.fori_loop(0, S // tm, fill, 0)

        def act(i, carry):
            r0 = pl.multiple_of(i * tm, tm)
            g1, _ = _ffn_gate(pad_s, w_ref, bias_ref, r0, tm)
            o_ref[pl.ds(r0, tm), :] = (g1 * _sigmoid(g1) * v_ref[pl.ds(r0, tm), :]).astype(BF16)
            return carry

        lax.fori_loop(0, S // tm, act, 0)

    return pl.pallas_call(
        body, name="ffn_act_fwd", grid=(nb,),
        in_specs=[pl.BlockSpec((S, CB), lambda j: (0, j)), pl.BlockSpec((S, CB), lambda j: (0, nb + j)),
                  pl.BlockSpec((FFN_KERNEL, CB), lambda j: (0, j)), pl.BlockSpec((1, CB), lambda j: (0, j))],
        out_specs=pl.BlockSpec((S, CB), lambda j: (0, j)),
        out_shape=jax.ShapeDtypeStruct((S, D_FF), BF16),
        scratch_shapes=[pltpu.VMEM((S + FFN_PAD, CB), F32)],
        compiler_params=_params("parallel"),
    )(u, u, w, bias)


def _ffn_act_bwd(u, w, bias, dact, *, tm=256):
    S = u.shape[0]
    CB = FFN_CB
    nb = D_FF // CB

    def body(g_ref, v_ref, w_ref, bias_ref, da_ref, dg_ref, dv_ref, dw_ref, dbias_ref, pad_s, dpad_s, dw_s):
        pad_s[0:FFN_PAD, :] = jnp.zeros((FFN_PAD, CB), F32)
        dpad_s[S:S + FFN_PAD, :] = jnp.zeros((FFN_PAD, CB), F32)
        dw_s[...] = jnp.zeros_like(dw_s)

        def fill(i, carry):
            pad_s[pl.ds(pl.multiple_of(FFN_PAD + i * tm, SUBLANES), tm), :] = g_ref[pl.ds(pl.multiple_of(i * tm, tm), tm), :]
            return carry

        lax.fori_loop(0, S // tm, fill, 0)

        def first(i, carry):
            r0 = pl.multiple_of(i * tm, tm)
            rows = pl.ds(r0, tm)
            g1, taps = _ffn_gate(pad_s, w_ref, bias_ref, r0, tm)
            sg = _sigmoid(g1)
            da = da_ref[rows, :].astype(F32)
            dv_ref[rows, :] = (da * g1 * sg).astype(BF16)
            dg1 = da * v_ref[rows, :] * (sg * (1.0 + g1 * (1.0 - sg)))
            dpad_s[rows, :] = dg1
            for k in range(FFN_KERNEL):
                dw_s[SUBLANES * k:SUBLANES * (k + 1), :] += _fold_rows(dg1 * taps[FFN_PAD - 2 + k])
            dw_s[SUBLANES * FFN_KERNEL:SUBLANES * (FFN_KERNEL + 1), :] += _fold_rows(dg1)
            return carry

        lax.fori_loop(0, S // tm, first, 0)

        def second(i, carry):
            r0 = pl.multiple_of(i * tm, tm)
            n_rows = tm + FFN_PAD
            taps = _taps(_shifted(dpad_s[pl.ds(r0, n_rows), :], n_rows), range(FFN_KERNEL), tm)
            dg_ref[pl.ds(r0, tm), :] = (w_ref[2:3, :] * taps[0] + w_ref[1:2, :] * taps[1] + w_ref[0:1, :] * taps[2]).astype(BF16)
            return carry

        lax.fori_loop(0, S // tm, second, 0)
        for k in range(FFN_KERNEL):
            dw_ref[k:k + 1, :] = jnp.sum(dw_s[SUBLANES * k:SUBLANES * (k + 1), :], axis=0, keepdims=True)
        dbias_ref[...] = jnp.sum(dw_s[SUBLANES * FFN_KERNEL:SUBLANES * (FFN_KERNEL + 1), :], axis=0, keepdims=True)

    col = lambda off: pl.BlockSpec((S, CB), lambda j: (0, off + j))
    wspec = pl.BlockSpec((FFN_KERNEL, CB), lambda j: (0, j))
    bspec = pl.BlockSpec((1, CB), lambda j: (0, j))
    return pl.pallas_call(
        body, name="ffn_act_bwd", grid=(nb,),
        in_specs=[col(0), col(nb), wspec, bspec, col(0)],
        out_specs=[col(0), col(0), wspec, bspec],
        out_shape=[jax.ShapeDtypeStruct((S, D_FF), BF16)] * 2
        + [jax.ShapeDtypeStruct((FFN_KERNEL, D_FF), F32), jax.ShapeDtypeStruct((1, D_FF), F32)],
        scratch_shapes=[pltpu.VMEM((S + FFN_PAD, CB), F32), pltpu.VMEM((S + FFN_PAD, CB), F32),
                        pltpu.VMEM((SUBLANES * (FFN_KERNEL + 1), CB), F32)],
        compiler_params=_params("parallel"),
    )(u, u, w, bias, dact)


def _loss_grad(y, target, *, tm=512):
    S, D = y.shape

    def body(y_ref, t_ref, dy_ref, l_ref):
        d = y_ref[...] - t_ref[...]
        dy_ref[...] = d * (1.0 / D)

        @pl.when(pl.program_id(0) == 0)
        def _():
            l_ref[...] = jnp.zeros_like(l_ref)

        l_ref[...] += 0.5 * jnp.sum(jnp.mean(d * d, axis=-1, keepdims=True), axis=0, keepdims=True)

    dy, l = pl.pallas_call(
        body, name="loss_grad", grid=(S // tm,),
        in_specs=[pl.BlockSpec((tm, D), lambda i: (i, 0))] * 2,
        out_specs=[pl.BlockSpec((tm, D), lambda i: (i, 0)), pl.BlockSpec((SUBLANES, LANES), lambda i: (0, 0))],
        out_shape=[jax.ShapeDtypeStruct((S, D), F32), jax.ShapeDtypeStruct((SUBLANES, LANES), F32)],
        compiler_params=_params("arbitrary"),
    )(y, target)
    return dy, l[0, 0]


def _row_tile(rows, cap=512):
    t = min(rows, cap)
    while rows % t or t % SUBLANES:
        t -= 1
    return t


def _adamw(w, g, m, v):
    R, C = w.shape
    tr = _row_tile(R, 256)

    def body(w_ref, g_ref, m_ref, v_ref, d_ref, nm_ref, nv_ref):
        gv = g_ref[...]
        m1 = ADAM_B1 * m_ref[...] + (1.0 - ADAM_B1) * gv
        v1 = ADAM_B2 * v_ref[...] + (1.0 - ADAM_B2) * (gv * gv)
        m_hat = m1 / (1.0 - ADAM_B1 ** ADAM_STEP)
        v_hat = v1 / (1.0 - ADAM_B2 ** ADAM_STEP)
        d_ref[...] = -ADAM_LR * (m_hat / (jnp.sqrt(v_hat) + ADAM_EPS) + ADAM_WD * w_ref[...])
        nm_ref[...] = m1
        nv_ref[...] = v1

    spec = pl.BlockSpec((tr, C), lambda i: (i, 0))
    return pl.pallas_call(
        body, name="adamw", grid=(R // tr,),
        in_specs=[spec] * 4, out_specs=[spec] * 3,
        out_shape=[jax.ShapeDtypeStruct((R, C), F32)] * 3,
        compiler_params=_params("parallel"),
    )(w, g, m, v)


def _cast_bf16(w):
    R, C = w.shape
    tr = _row_tile(R, 512)

    def body(w_ref, o_ref):
        o_ref[...] = w_ref[...].astype(BF16)

    spec = pl.BlockSpec((tr, C), lambda i: (i, 0))
    return pl.pallas_call(body, name="cast_bf16", grid=(R // tr,), in_specs=[spec], out_specs=spec,
                          out_shape=jax.ShapeDtypeStruct((R, C), BF16), compiler_params=_params("parallel"))(w)


def _add_half(g4, la, c):
    L, _, H, W = g4.shape
    th = _row_tile(H, 256)

    def body(c_ref, g_ref, la_ref, o_ref):
        o_ref[...] = (g_ref[...].astype(F32) + la_ref[...].astype(F32)).astype(BF16)

    return pl.pallas_call(
        body, name="add_half",
        grid_spec=pltpu.PrefetchScalarGridSpec(
            num_scalar_prefetch=1, grid=(L, H // th),
            in_specs=[pl.BlockSpec((None, None, th, W), lambda l, i, c_ref: (l, c_ref[0], i, 0)),
                      pl.BlockSpec((None, th, W), lambda l, i, c_ref: (l, i, 0))],
            out_specs=pl.BlockSpec((None, th, W), lambda l, i, c_ref: (l, i, 0))),
        out_shape=jax.ShapeDtypeStruct((L, H, W), BF16),
        compiler_params=_params("parallel", "parallel"),
    )(c, g4, la)


def _add_parts(p, lb, shard, kind):
    _, L, H, C = lb.shape
    th = _row_tile(H, 256)

    def body(s_ref, p_ref, lb_ref, o_ref):
        acc = p_ref[...].astype(F32)
        for k in range(N_CHIPS - 1):
            acc = acc + lb_ref[k].astype(F32)
        o_ref[...] = acc

    if kind == "col":
        p_spec = pl.BlockSpec((None, th, C), lambda l, i, s_ref: (l, i, s_ref[0]))
    else:
        p_spec = pl.BlockSpec((None, None, th, C), lambda l, i, s_ref: (l, s_ref[0], i, 0))
    return pl.pallas_call(
        body, name="add_parts_" + kind,
        grid_spec=pltpu.PrefetchScalarGridSpec(
            num_scalar_prefetch=1, grid=(L, H // th),
            in_specs=[p_spec, pl.BlockSpec((N_CHIPS - 1, None, th, C), lambda l, i, s_ref: (0, l, i, 0))],
            out_specs=pl.BlockSpec((None, th, C), lambda l, i, s_ref: (l, i, 0))),
        out_shape=jax.ShapeDtypeStruct((L, H, C), F32),
        compiler_params=_params("parallel", "parallel"),
    )(shard, p, lb)


ANY = pl.BlockSpec(memory_space=pl.ANY)


def _place():
    x, y, c = lax.axis_index("x"), lax.axis_index("y"), lax.axis_index("c")
    chips = [(1 - x, y), (x, 1 - y), (1 - x, 1 - y)]
    return x, y, c, chips


def _comm_call(body, name, ins, out_shape, n_remote, n_local):
    scratch = [pltpu.SemaphoreType.DMA((n_remote,)), pltpu.SemaphoreType.DMA((n_remote,))]
    if n_local:
        scratch.append(pltpu.SemaphoreType.DMA((n_local,)))
    return pl.pallas_call(
        body, name=name, in_specs=[ANY] * len(ins), out_specs=[ANY] * len(out_shape), out_shape=out_shape,
        scratch_shapes=scratch, compiler_params=pltpu.CompilerParams(has_side_effects=True),
    )(*ins)


def _remote(src, dst, send, recv, k, to):
    return pltpu.make_async_remote_copy(src_ref=src, dst_ref=dst, send_sem=send.at[k], recv_sem=recv.at[k],
                                        device_id=to, device_id_type=MESH)


def _gather_weights(shards, kinds):
    n = len(shards)
    per = 2 * (N_CHIPS - 1)
    out_shape = [jax.ShapeDtypeStruct((s.shape[0], s.shape[1], N_CHIPS * s.shape[2]) if kind == "col"
                                      else (s.shape[0], N_CHIPS * s.shape[1], s.shape[2]), s.dtype)
                 for s, kind in zip(shards, kinds)]

    def body(*refs):
        srcs, fulls, (send, recv, loc) = refs[:n], refs[n:2 * n], refs[2 * n:]
        x, y, c, chips = _place()
        me = 2 * x + y
        copies, local, passes = [], [], []
        for a in range(n):
            L, R, C = srcs[a].shape
            H = R // 2

            def win(s, h, a=a, R=R, C=C, H=H):
                if kinds[a] == "col":
                    return fulls[a].at[:, pl.ds(h * H, H), pl.ds(pl.multiple_of(s * C, LANES), C)]
                return fulls[a].at[:, pl.ds(pl.multiple_of(s * R + h * H, 16), H), :]

            if kinds[a] == "col":
                own = fulls[a].at[:, :, pl.ds(pl.multiple_of(me * C, LANES), C)]
            else:
                own = fulls[a].at[:, pl.ds(pl.multiple_of(me * R, 16), R), :]
            mine = pltpu.make_async_copy(srcs[a], own, loc.at[a])
            mine.start()
            local.append(mine)
            half = srcs[a].at[:, pl.ds(pl.multiple_of(c * H, 16), H), :]
            for k, chip in enumerate(chips):
                cp = _remote(half, win(me, c), send, recv, a * per + k, (*chip, c))
                cp.start()
                copies.append(cp)
            passes.append(win)
        for a in range(n):
            for k, chip in enumerate(chips):
                s = 2 * chip[0] + chip[1]
                landed = passes[a](s, c)
                _remote(landed, landed, send, recv, a * per + k, (x, y, 1 - c)).wait_recv()
                cp = _remote(landed, landed, send, recv, a * per + N_CHIPS - 1 + k, (x, y, 1 - c))
                cp.start()
                copies.append(cp)
        for a in range(n):
            for k, chip in enumerate(chips):
                s = 2 * chip[0] + chip[1]
                theirs = passes[a](s, 1 - c)
                _remote(theirs, theirs, send, recv, a * per + N_CHIPS - 1 + k, (x, y, 1 - c)).wait_recv()
        for cp in copies:
            cp.wait_send()
        for cp in local:
            cp.wait()

    return _comm_call(body, "gather_weights", shards, out_shape, n * per, n)


def _gather_small(shards):
    n = len(shards)
    out_shape = [jax.ShapeDtypeStruct((N_CHIPS,) + s.shape, s.dtype) for s in shards]

    def body(*refs):
        srcs, outs, (send, recv, loc) = refs[:n], refs[n:2 * n], refs[2 * n:]
        x, y, c, chips = _place()
        me = 2 * x + y
        remote, local = [], []
        for a in range(n):
            local.append(pltpu.make_async_copy(srcs[a], outs[a].at[me], loc.at[a]))
            for k, chip in enumerate(chips):
                remote.append(_remote(srcs[a], outs[a].at[me], send, recv, a * (N_CHIPS - 1) + k, (*chip, c)))
        for cp in local + remote:
            cp.start()
        for cp in remote + local:
            cp.wait()

    return _comm_call(body, "gather_small", shards, out_shape, n * (N_CHIPS - 1), n)


def _exchange_halves(g4s):
    n = len(g4s)
    out_shape = [jax.ShapeDtypeStruct((g.shape[0],) + g.shape[2:], g.dtype) for g in g4s]

    def body(*refs):
        gs, las, (send, recv) = refs[:n], refs[n:2 * n], refs[2 * n:]
        x, y, c, _ = _place()
        cps = [_remote(gs[a].at[:, 1 - c], las[a], send, recv, a, (x, y, 1 - c)) for a in range(n)]
        for cp in cps:
            cp.start()
        for cp in cps:
            cp.wait()

    return _comm_call(body, "exchange_halves", g4s, out_shape, n, 0)


def _scatter_partials(ps, kinds):
    n = len(ps)
    out_shape = []
    for p, kind in zip(ps, kinds):
        L, H, C = (p.shape[0], p.shape[1], p.shape[2] // N_CHIPS) if kind == "col" else (p.shape[0], p.shape[2], p.shape[3])
        out_shape.append(jax.ShapeDtypeStruct((N_CHIPS - 1, L, H, C), p.dtype))

    def body(*refs):
        srcs, lbs, (send, recv) = refs[:n], refs[n:2 * n], refs[2 * n:]
        x, y, c, chips = _place()
        cps = []
        for a in range(n):
            C = lbs[a].shape[3]
            for k, chip in enumerate(chips):
                s = 2 * chip[0] + chip[1]
                src = srcs[a].at[:, :, pl.ds(pl.multiple_of(s * C, LANES), C)] if kinds[a] == "col" else srcs[a].at[:, s]
                cps.append(_remote(src, lbs[a].at[k], send, recv, a * (N_CHIPS - 1) + k, (*chip, c)))
        for cp in cps:
            cp.start()
        for cp in cps:
            cp.wait()

    return _comm_call(body, "scatter_partials", ps, out_shape, n * (N_CHIPS - 1), 0)


def _share_halves(ghs):
    n = len(ghs)
    out_shape = [jax.ShapeDtypeStruct((g.shape[0], 2) + g.shape[1:], g.dtype) for g in ghs]

    def body(*refs):
        srcs, outs, (send, recv, loc) = refs[:n], refs[n:2 * n], refs[2 * n:]
        x, y, c, _ = _place()
        local = [pltpu.make_async_copy(srcs[a], outs[a].at[:, c], loc.at[a]) for a in range(n)]
        remote = [_remote(srcs[a], outs[a].at[:, c], send, recv, a, (x, y, 1 - c)) for a in range(n)]
        for cp in local + remote:
            cp.start()
        for cp in remote + local:
            cp.wait()

    return _comm_call(body, "share_halves", ghs, out_shape, n, n)


def _allreduce_small(part):
    N = part.shape[0]

    def body(p_ref, o_ref, buf, send, recv):
        x, y, c, _ = _place()
        me = 4 * x + 2 * y + c
        buf[me] = p_ref[...]
        cps = []
        for k in range(1, N_DEV):
            peer = (1 - x if k & 4 else x, 1 - y if k & 2 else y, 1 - c if k & 1 else c)
            cps.append(_remote(p_ref, buf.at[me], send, recv, k - 1, peer))
        for cp in cps:
            cp.start()
        for cp in cps:
            cp.wait()
        acc = buf[0]
        for i in range(1, N_DEV):
            acc = acc + buf[i]
        o_ref[...] = acc

    vmem = pl.BlockSpec(memory_space=pltpu.VMEM)
    return pl.pallas_call(
        body, name="allreduce_small", in_specs=[vmem], out_specs=vmem,
        out_shape=jax.ShapeDtypeStruct((N, LANES), F32),
        scratch_shapes=[pltpu.VMEM((N_DEV, N, LANES), F32), pltpu.SemaphoreType.DMA((N_DEV - 1,)),
                        pltpu.SemaphoreType.DMA((N_DEV - 1,))],
        compiler_params=pltpu.CompilerParams(has_side_effects=True, vmem_limit_bytes=VMEM_LIMIT_BYTES),
    )(part)


def _layer_fwd(x, p):
    h1, proj = _norm_matmul(x, p["norm1_g"], p["w_in"])
    attn, ltot = _attn_fwd(proj, p["q_norm_g"], p["k_norm_g"])
    c1 = _glu_conv_fwd(proj, p["conv_dw_w"], p["conv_dw_b"])
    c = _ln_silu_fwd(c1, p["conv_ln_g"], p["conv_ln_b"])
    x_mid = _matmul_res([attn, c], p["w_out"], x)
    h2, u = _norm_matmul(x_mid, p["norm2_g"], p["w_up"])
    act = _ffn_act_fwd(u, p["ffn_dw_w"], p["ffn_dw_b"])
    x_out = _matmul_res([act], p["w_down"], x_mid)
    return x_out, dict(x=x, h1=h1, proj=proj, attn=attn, ltot=ltot, c1=c1, c=c, x_mid=x_mid, h2=h2, u=u, act=act)


def _layer_bwd(dx_out, s, p):
    g = {}
    dact = _matmul_nt([dx_out], p["w_down"], BF16)
    g["w_down"] = _matmul_tn(s["act"], dx_out, tk=D_FF // 2, tn=D_MODEL)
    dgate, dval, g["ffn_dw_w"], g["ffn_dw_b"] = _ffn_act_bwd(s["u"], p["ffn_dw_w"], p["ffn_dw_b"], dact)
    dx_mid, g["norm2_g"] = _matmul_nt_rmsbwd([dgate, dval], p["w_up"], s["x_mid"], p["norm2_g"], dx_out)
    g["w_up"] = jnp.concatenate([_matmul_tn(s["h2"], d, tk=D_MODEL, tn=D_FF // 2) for d in (dgate, dval)], axis=1)
    dmix = _matmul_nt([dx_mid], p["w_out"], F32)
    g["w_out"] = jnp.concatenate([_matmul_tn(m, dx_mid, tk=ATTN_WIDTH, tn=D_MODEL) for m in (s["attn"], s["c"])], axis=0)
    dc1, g["conv_ln_g"], g["conv_ln_b"] = _ln_silu_bwd(s["c1"], p["conv_ln_g"], p["conv_ln_b"], dmix)
    da, db, g["conv_dw_w"], g["conv_dw_b"] = _glu_conv_bwd(s["proj"], p["conv_dw_w"], dc1)
    dq, dk, dv, g["q_norm_g"], g["k_norm_g"] = _attn_bwd(s["proj"], p["q_norm_g"], p["k_norm_g"], s["ltot"], dmix)
    pieces = [dq, dk, dv, da, db]
    dx, g["norm1_g"] = _matmul_nt_rmsbwd(pieces, p["w_in"], s["x"], p["norm1_g"], dx_mid)
    g["w_in"] = jnp.concatenate([_matmul_tn(s["h1"], d, tk=D_MODEL, tn=ATTN_WIDTH) for d in pieces], axis=1)
    return dx, g


WEIGHTS = ("norm1_g", "w_in", "q_norm_g", "k_norm_g", "conv_dw_w", "conv_dw_b", "conv_ln_g", "conv_ln_b",
           "w_out", "norm2_g", "w_up", "ffn_dw_w", "ffn_dw_b", "w_down")
BIG = ("w_in", "w_out", "w_up", "w_down")
BIG_KIND = {"w_in": "col", "w_out": "row", "w_up": "col", "w_down": "row"}
SMALL_SHARDED = ("conv_dw_w", "ffn_dw_w")
REPLICATED = tuple(n for n in WEIGHTS if n not in BIG + SMALL_SHARDED)


def _pack(arrays):
    flat = jnp.concatenate([a.reshape(-1) for a in arrays])
    rows = -(-flat.shape[0] // (SUBLANES * LANES)) * SUBLANES
    return jnp.pad(flat, (0, rows * LANES - flat.shape[0])).reshape(rows, LANES)


def _unpack(packed, shapes):
    flat = packed.reshape(-1)
    out, off = [], 0
    for shape in shapes:
        size = 1
        for d in shape:
            size *= d
        out.append(flat[off:off + size].reshape(shape))
        off += size
    return out


def _unshard_last(stacked):
    n, L, K, C = stacked.shape
    return jnp.transpose(stacked, (1, 2, 0, 3)).reshape(L, K, n * C)


def kernel(x, norm1_g, w_in, q_norm_g, k_norm_g, conv_dw_w, conv_dw_b, conv_ln_g, conv_ln_b, w_out, norm2_g, w_up, ffn_dw_w, ffn_dw_b, w_down, loss_target, m_norm1_g, m_w_in, m_q_norm_g, m_k_norm_g, m_conv_dw_w, m_conv_dw_b, m_conv_ln_g, m_conv_ln_b, m_w_out, m_norm2_g, m_w_up, m_ffn_dw_w, m_ffn_dw_b, m_w_down, v_norm1_g, v_w_in, v_q_norm_g, v_k_norm_g, v_conv_dw_w, v_conv_dw_b, v_conv_ln_g, v_conv_ln_b, v_w_out, v_norm2_g, v_w_up, v_ffn_dw_w, v_ffn_dw_b, v_w_down):
    given = dict(locals())
    w = {n: given[n] for n in WEIGHTS}
    m = {n: given["m_" + n] for n in WEIGHTS}
    v = {n: given["v_" + n] for n in WEIGHTS}
    chip = 2 * lax.axis_index("x") + lax.axis_index("y")
    core = lax.axis_index("c")
    chip_arr = jnp.reshape(chip, (1,)).astype(jnp.int32)
    core_arr = jnp.reshape(core, (1,)).astype(jnp.int32)
    L = DEPTH

    shards_bf16 = [_cast_bf16(w[n].reshape(-1, w[n].shape[2])).reshape(w[n].shape) for n in BIG]
    full = dict(zip(BIG, _gather_weights(shards_bf16, [BIG_KIND[n] for n in BIG])))
    for n, stacked in zip(SMALL_SHARDED, _gather_small([w[n] for n in SMALL_SHARDED])):
        full[n] = _unshard_last(stacked)
    params = []
    for l in range(L):
        p = {n: full[n][l] for n in BIG + SMALL_SHARDED}
        p.update({n: w[n][l][None] for n in REPLICATED})
        params.append(p)

    act = x[0]
    saved = []
    for l in range(L):
        act, s = _layer_fwd(act, params[l])
        saved.append(s)
    dx, loss_part = _loss_grad(act, loss_target[0])
    loss = lax.psum(loss_part, ("x", "y", "c"))
    grads = [None] * L
    for l in reversed(range(L)):
        dx, grads[l] = _layer_bwd(dx, saved[l], params[l])

    g4s = []
    for n in BIG:
        g = jnp.stack([grads[l][n] for l in range(L)])
        if BIG_KIND[n] == "col":
            g4s.append(g.reshape(L, 2, g.shape[1] // 2, g.shape[2]))
        else:
            rows = w[n].shape[1]
            g4s.append(g.reshape(L * N_CHIPS, 2, rows // 2, g.shape[2]))
    received = _exchange_halves(g4s)
    partial = [_add_half(g4, la, core_arr) for g4, la in zip(g4s, received)]
    partial = [p if BIG_KIND[n] == "col" else p.reshape(L, N_CHIPS, p.shape[1], p.shape[2]) for n, p in zip(BIG, partial)]
    landed = _scatter_partials(partial, [BIG_KIND[n] for n in BIG])
    halves = [_add_parts(p, lb, chip_arr, BIG_KIND[n]) for n, p, lb in zip(BIG, partial, landed)]
    grad = {n: g.reshape(w[n].shape) for n, g in zip(BIG, _share_halves(halves))}

    small = REPLICATED + SMALL_SHARDED
    small_full = [jnp.stack([grads[l][n] for l in range(L)]) for n in small]
    summed = _unpack(_allreduce_small(_pack(small_full)), [a.shape for a in small_full])
    for n, g in zip(small, summed):
        if n in REPLICATED:
            grad[n] = g.reshape(w[n].shape)
        else:
            width = w[n].shape[2]
            grad[n] = lax.dynamic_slice_in_dim(g, chip * width, width, axis=2)

    delta, new_m, new_v = {}, {}, {}
    for n in BIG:
        two_d = lambda a: a.reshape(-1, a.shape[2])
        d, nm, nv = _adamw(two_d(w[n]), two_d(grad[n]), two_d(m[n]), two_d(v[n]))
        delta[n], new_m[n], new_v[n] = (a.reshape(w[n].shape) for a in (d, nm, nv))
    shapes = [w[n].shape for n in small]
    packed = _adamw(*[_pack([src[n] for n in small]) for src in (w, grad, m, v)])
    for out, pk in zip((delta, new_m, new_v), packed):
        out.update(zip(small, _unpack(pk, shapes)))

    return (loss, dx[None], *[grad[n] for n in WEIGHTS], *[delta[n] for n in WEIGHTS],
            *[new_m[n] for n in WEIGHTS], *[new_v[n] for n in WEIGHTS])
```

```python
import functools

import jax
import jax.numpy as jnp
from jax import lax
from jax.experimental import pallas as pl
from jax.experimental.pallas import tpu as pltpu

F32 = jnp.float32
BF16 = jnp.bfloat16

DEPTH = 4
D_MODEL = 1024
HEADS = 8
HEAD_DIM = 64
ATTN_WIDTH = HEADS * HEAD_DIM
CONV_WIDTH = D_MODEL - ATTN_WIDTH
CONV_KERNEL = 31
D_FF = 2816
FFN_KERNEL = 3
EPS = 1e-6
ADAM_LR, ADAM_B1, ADAM_B2, ADAM_EPS, ADAM_WD, ADAM_STEP = 0.001, 0.9, 0.999, 1e-08, 0.01, 10

N_CHIPS = 4
N_DEV = 8
LANES = 128
SUBLANES = 8
VMEM_LIMIT_BYTES = 56 * 2**20
ATTN_TILE = 256
CONV_PAD = 32
FFN_PAD = 8
MESH = pl.DeviceIdType.MESH


def _params(*sem):
    return pltpu.CompilerParams(dimension_semantics=sem if sem else None, vmem_limit_bytes=VMEM_LIMIT_BYTES)


def _dot(a, b):
    return jnp.dot(a, b, preferred_element_type=F32)


def _dot_nt(a, b):
    return lax.dot_general(a, b, (((1,), (1,)), ((), ())), preferred_element_type=F32)


def _dot_tn(a, b):
    return lax.dot_general(a, b, (((0,), (0,)), ((), ())), preferred_element_type=F32)


def _sigmoid(x):
    return 1.0 / (1.0 + jnp.exp(-x))


def _norm_matmul(x, g, w, *, tm=512, n_split=2):
    S, D = x.shape
    N = w.shape[1]
    tn = N // n_split

    def body(x_ref, g_ref, w_ref, h_ref, y_ref):
        @pl.when(pl.program_id(1) == 0)
        def _():
            xv = x_ref[...]
            r = lax.rsqrt(jnp.mean(xv * xv, axis=-1, keepdims=True) + EPS)
            h_ref[...] = (xv * r * g_ref[...]).astype(BF16)

        y_ref[...] = _dot(h_ref[...], w_ref[...])

    return pl.pallas_call(
        body, name="norm_matmul", grid=(S // tm, n_split),
        in_specs=[pl.BlockSpec((tm, D), lambda i, j: (i, 0)),
                  pl.BlockSpec((1, D), lambda i, j: (0, 0)),
                  pl.BlockSpec((D, tn), lambda i, j: (0, j))],
        out_specs=[pl.BlockSpec((tm, D), lambda i, j: (i, 0)),
                   pl.BlockSpec((tm, tn), lambda i, j: (i, j))],
        out_shape=[jax.ShapeDtypeStruct((S, D), BF16), jax.ShapeDtypeStruct((S, N), F32)],
        compiler_params=_params("parallel", "arbitrary"),
    )(x, g, w)


def _matmul_res(pieces, w, res, *, tm=512):
    S, N = res.shape
    K = w.shape[0]
    widths = [p.shape[1] for p in pieces]
    assert sum(widths) == K

    def body(*refs):
        p_refs, (w_ref, res_ref, o_ref) = refs[:len(pieces)], refs[len(pieces):]
        acc = res_ref[...]
        off = 0
        for p_ref, kp in zip(p_refs, widths):
            acc = acc + _dot(p_ref[...], w_ref[off:off + kp, :])
            off += kp
        o_ref[...] = acc

    return pl.pallas_call(
        body, name="matmul_res", grid=(S // tm,),
        in_specs=[pl.BlockSpec((tm, kp), lambda i: (i, 0)) for kp in widths]
        + [pl.BlockSpec((K, N), lambda i: (0, 0)), pl.BlockSpec((tm, N), lambda i: (i, 0))],
        out_specs=pl.BlockSpec((tm, N), lambda i: (i, 0)),
        out_shape=jax.ShapeDtypeStruct((S, N), F32),
        compiler_params=_params("parallel"),
    )(*pieces, w, res)


def _nt_sum(p_refs, widths, w_ref):
    acc = None
    off = 0
    for p_ref, n_p in zip(p_refs, widths):
        d = _dot_nt(p_ref[...].astype(BF16), w_ref[:, off:off + n_p])
        acc = d if acc is None else acc + d
        off += n_p
    return acc


def _matmul_nt(pieces, w, out_dtype, *, tm=512):
    S = pieces[0].shape[0]
    K, N = w.shape
    widths = [p.shape[1] for p in pieces]
    assert sum(widths) == N

    def body(*refs):
        p_refs, (w_ref, o_ref) = refs[:len(pieces)], refs[len(pieces):]
        o_ref[...] = _nt_sum(p_refs, widths, w_ref).astype(out_dtype)

    return pl.pallas_call(
        body, name="matmul_nt", grid=(S // tm,),
        in_specs=[pl.BlockSpec((tm, n_p), lambda i: (i, 0)) for n_p in widths]
        + [pl.BlockSpec((K, N), lambda i: (0, 0))],
        out_specs=pl.BlockSpec((tm, K), lambda i: (i, 0)),
        out_shape=jax.ShapeDtypeStruct((S, K), out_dtype),
        compiler_params=_params("parallel"),
    )(*pieces, w)


def _matmul_nt_rmsbwd(pieces, w, x, g, dres, *, tm=256):
    S, K = x.shape
    N = w.shape[1]
    widths = [p.shape[1] for p in pieces]
    assert sum(widths) == N

    def body(*refs):
        p_refs, (w_ref, x_ref, g_ref, dres_ref, dx_ref, dg_ref) = refs[:len(pieces)], refs[len(pieces):]
        dh = _nt_sum(p_refs, widths, w_ref)
        xv = x_ref[...]
        r = lax.rsqrt(jnp.mean(xv * xv, axis=-1, keepdims=True) + EPS)
        xh = xv * r
        dxh = dh * g_ref[...]
        dx_ref[...] = dres_ref[...] + r * (dxh - xh * jnp.mean(dxh * xh, axis=-1, keepdims=True))

        @pl.when(pl.program_id(0) == 0)
        def _():
            dg_ref[...] = jnp.zeros_like(dg_ref)

        dg_ref[...] += jnp.sum(dh * xh, axis=0, keepdims=True)

    return pl.pallas_call(
        body, name="matmul_nt_rmsbwd", grid=(S // tm,),
        in_specs=[pl.BlockSpec((tm, n_p), lambda i: (i, 0)) for n_p in widths]
        + [pl.BlockSpec((K, N), lambda i: (0, 0)), pl.BlockSpec((tm, K), lambda i: (i, 0)),
           pl.BlockSpec((1, K), lambda i: (0, 0)), pl.BlockSpec((tm, K), lambda i: (i, 0))],
        out_specs=[pl.BlockSpec((tm, K), lambda i: (i, 0)), pl.BlockSpec((1, K), lambda i: (0, 0))],
        out_shape=[jax.ShapeDtypeStruct((S, K), F32), jax.ShapeDtypeStruct((1, K), F32)],
        compiler_params=_params("arbitrary"),
    )(*pieces, w, x, g, dres)


def _matmul_tn(x, dy, *, tk, tn, ts=512):
    S, M = x.shape
    N = dy.shape[1]
    n_s = S // ts

    def body(x_ref, dy_ref, o_ref, acc_ref):
        s = pl.program_id(2)

        @pl.when(s == 0)
        def _():
            acc_ref[...] = jnp.zeros_like(acc_ref)

        acc_ref[...] += _dot_tn(x_ref[...].astype(BF16), dy_ref[...].astype(BF16))

        @pl.when(s == n_s - 1)
        def _():
            o_ref[...] = acc_ref[...].astype(BF16)

    return pl.pallas_call(
        body, name="matmul_tn", grid=(M // tk, N // tn, n_s),
        in_specs=[pl.BlockSpec((ts, tk), lambda i, j, s: (s, i)),
                  pl.BlockSpec((ts, tn), lambda i, j, s: (s, j))],
        out_specs=pl.BlockSpec((tk, tn), lambda i, j, s: (i, j)),
        out_shape=jax.ShapeDtypeStruct((M, N), BF16),
        scratch_shapes=[pltpu.VMEM((tk, tn), F32)],
        compiler_params=_params("parallel", "parallel", "arbitrary"),
    )(x, dy)


def _tri_consts():
    j = jnp.arange(ATTN_TILE)[:, None]
    s = jnp.arange(ATTN_TILE)[None, :]

    def two(m):
        return jnp.concatenate([m, m], axis=0).astype(BF16)

    return two(j > s), two(j <= s), two(j < s)


def _split_hi_lo(a):
    hi = a.astype(BF16)
    lo = (a - hi.astype(F32)).astype(BF16)
    return jnp.concatenate([hi, lo], axis=1)


def _log_terms(s):
    lom = -(jnp.maximum(s, 0.0) + jnp.log(1.0 + jnp.exp(-jnp.abs(s))))
    return lom, lom + s


def _causal_mask():
    t = lax.broadcasted_iota(jnp.int32, (ATTN_TILE, ATTN_TILE), 0)
    s = lax.broadcasted_iota(jnp.int32, (ATTN_TILE, ATTN_TILE), 1)
    return s < t


def _attn_prep(h, q_ref, k_ref, v_ref, qg_ref, kg_ref, qn_s, kn_s, vb_s, n_tiles):
    T = ATTN_TILE
    lanes = slice(HEAD_DIM * h, HEAD_DIM * (h + 1))
    scale = HEAD_DIM ** -0.5

    def prep(i, carry):
        rows = pl.ds(pl.multiple_of(i * T, T), T)
        q = q_ref[rows, lanes]
        k = k_ref[rows, lanes]
        rq = lax.rsqrt(jnp.mean(q * q, axis=-1, keepdims=True) + EPS)
        rk = lax.rsqrt(jnp.mean(k * k, axis=-1, keepdims=True) + EPS)
        qn_s[h, rows, :] = (q * rq * qg_ref[...] * scale).astype(BF16)
        kn_s[h, rows, :] = (k * rk * kg_ref[...]).astype(BF16)
        vb_s[h, rows, :] = v_ref[rows, lanes].astype(BF16)
        return carry

    lax.fori_loop(0, n_tiles, prep, 0)


def _attn_fwd(proj, qg, kg):
    S = proj.shape[0]
    T = ATTN_TILE
    n_tiles = S // T
    suffix, _, _ = _tri_consts()
    pairs = HEADS // 2
    q_blk, k_blk, v_blk = 0, ATTN_WIDTH // LANES, 2 * ATTN_WIDTH // LANES

    def body(q_ref, k_ref, v_ref, qg_ref, kg_ref, tri_ref, o_ref, lt_ref, qn_s, kn_s, vb_s):
        for h in range(2):
            lanes = slice(HEAD_DIM * h, HEAD_DIM * (h + 1))
            _attn_prep(h, q_ref, k_ref, v_ref, qg_ref, kg_ref, qn_s, kn_s, vb_s, n_tiles)

            def q_tile(qi, carry0):
                qrows = pl.ds(pl.multiple_of(qi * T, T), T)
                qt = qn_s[h, qrows, :]

                def tile(kj, carry, diag):
                    acc, c = carry
                    krows = pl.ds(pl.multiple_of(kj * T, T), T)
                    s = _dot_nt(qt, kn_s[h, krows, :])
                    lom, lb = _log_terms(s)
                    if diag:
                        mask = _causal_mask()
                        lom = jnp.where(mask, lom, 0.0)
                    tail = _dot(_split_hi_lo(lom), tri_ref[...])
                    tot = tail[:, 0:1] + lom[:, 0:1]
                    w = jnp.exp(lb + tail + c)
                    if diag:
                        w = jnp.where(mask, w, 0.0)
                    acc = acc + _dot(w.astype(BF16), vb_s[h, krows, :])
                    return acc, c + tot

                carry = tile(qi, (jnp.zeros((T, HEAD_DIM), F32), jnp.zeros((T, 1), F32)), True)
                acc, c = lax.fori_loop(0, qi, lambda t, cr: tile(qi - 1 - t, cr, False), carry)
                o_ref[qrows, lanes] = acc.astype(BF16)
                lt_ref[qrows, lanes] = jnp.broadcast_to(c, (T, HEAD_DIM))
                return carry0

            lax.fori_loop(0, n_tiles, q_tile, 0)

    return pl.pallas_call(
        body, name="attn_fwd", grid=(pairs,),
        in_specs=[pl.BlockSpec((S, LANES), lambda p: (0, q_blk + p)),
                  pl.BlockSpec((S, LANES), lambda p: (0, k_blk + p)),
                  pl.BlockSpec((S, LANES), lambda p: (0, v_blk + p)),
                  pl.BlockSpec((1, HEAD_DIM), lambda p: (0, 0)),
                  pl.BlockSpec((1, HEAD_DIM), lambda p: (0, 0)),
                  pl.BlockSpec((2 * T, T), lambda p: (0, 0))],
        out_specs=[pl.BlockSpec((S, LANES), lambda p: (0, p)),
                   pl.BlockSpec((None, S, LANES), lambda p: (p, 0, 0))],
        out_shape=[jax.ShapeDtypeStruct((S, ATTN_WIDTH), BF16),
                   jax.ShapeDtypeStruct((pairs, S, LANES), F32)],
        scratch_shapes=[pltpu.VMEM((2, S, HEAD_DIM), BF16)] * 3,
        compiler_params=_params("parallel"),
    )(proj, proj, proj, qg, kg, suffix)


def _attn_bwd(proj, qg, kg, ltot, dmix):
    S = proj.shape[0]
    T = ATTN_TILE
    n_tiles = S // T
    _, prefix_incl, prefix_excl = _tri_consts()
    pairs = HEADS // 2
    q_blk, k_blk, v_blk = 0, ATTN_WIDTH // LANES, 2 * ATTN_WIDTH // LANES
    scale = HEAD_DIM ** -0.5

    def body(q_ref, k_ref, v_ref, qg_ref, kg_ref, lt_ref, do_ref, ti_ref, te_ref,
             dq_ref, dk_ref, dv_ref, dqg_ref, dkg_ref, qn_s, kn_s, vb_s, dq_s, dk_s, dv_s):
        @pl.when(pl.program_id(0) == 0)
        def _():
            dqg_ref[...] = jnp.zeros_like(dqg_ref)
            dkg_ref[...] = jnp.zeros_like(dkg_ref)

        for h in range(2):
            lanes = slice(HEAD_DIM * h, HEAD_DIM * (h + 1))
            _attn_prep(h, q_ref, k_ref, v_ref, qg_ref, kg_ref, qn_s, kn_s, vb_s, n_tiles)
            dk_s[...] = jnp.zeros_like(dk_s)
            dv_s[...] = jnp.zeros_like(dv_s)

            def q_tile(qi, carry0):
                qrows = pl.ds(pl.multiple_of(qi * T, T), T)
                qt = qn_s[h, qrows, :]
                dob = do_ref[qrows, lanes].astype(BF16)
                lt = lt_ref[qrows, HEAD_DIM * h:HEAD_DIM * h + 1]

                def tile(kj, carry, diag):
                    dq, p_lom, p_e = carry
                    krows = pl.ds(pl.multiple_of(kj * T, T), T)
                    kt = kn_s[h, krows, :]
                    s = _dot_nt(qt, kt)
                    lom, lb = _log_terms(s)
                    if diag:
                        mask = _causal_mask()
                        lom = jnp.where(mask, lom, 0.0)
                    pin = _dot(_split_hi_lo(lom), ti_ref[...])
                    w = jnp.exp(lb + ((lt - p_lom) - pin))
                    if diag:
                        w = jnp.where(mask, w, 0.0)
                    e = w * _dot_nt(dob, vb_s[h, krows, :])
                    gex = _dot(_split_hi_lo(e), te_ref[...])
                    dz = e - jnp.exp(lb) * (e + (p_e + gex))
                    if diag:
                        dz = jnp.where(mask, dz, 0.0)
                    dzb = dz.astype(BF16)
                    dq = dq + _dot(dzb, kt)
                    dk_s[krows, :] += _dot_tn(dzb, qt)
                    dv_s[krows, :] += _dot_tn(w.astype(BF16), dob)
                    p_lom = p_lom + pin[:, T - 1:T]
                    p_e = p_e + gex[:, T - 1:T] + e[:, T - 1:T]
                    return dq, p_lom, p_e

                zero = jnp.zeros((T, 1), F32)
                carry = lax.fori_loop(0, qi, lambda kj, cr: tile(kj, cr, False),
                                      (jnp.zeros((T, HEAD_DIM), F32), zero, zero))
                dq_s[qrows, :] = tile(qi, carry, True)[0]
                return carry0

            lax.fori_loop(0, n_tiles, q_tile, 0)

            def finish(i, carry):
                dqg, dkg = carry
                rows = pl.ds(pl.multiple_of(i * T, T), T)
                q = q_ref[rows, lanes]
                k = k_ref[rows, lanes]
                rq = lax.rsqrt(jnp.mean(q * q, axis=-1, keepdims=True) + EPS)
                rk = lax.rsqrt(jnp.mean(k * k, axis=-1, keepdims=True) + EPS)
                qh = q * rq
                kh = k * rk
                dqn = dq_s[rows, :] * scale
                dkn = dk_s[rows, :]
                dqh = dqn * qg_ref[...]
                dkh = dkn * kg_ref[...]
                dq_ref[rows, lanes] = (rq * (dqh - qh * jnp.mean(dqh * qh, axis=-1, keepdims=True))).astype(BF16)
                dk_ref[rows, lanes] = (rk * (dkh - kh * jnp.mean(dkh * kh, axis=-1, keepdims=True))).astype(BF16)
                dv_ref[rows, lanes] = dv_s[rows, :].astype(BF16)
                return (dqg + jnp.sum(dqn * qh, axis=0, keepdims=True),
                        dkg + jnp.sum(dkn * kh, axis=0, keepdims=True))

            zero = jnp.zeros((1, HEAD_DIM), F32)
            dqg, dkg = lax.fori_loop(0, n_tiles, finish, (zero, zero))
            dqg_ref[...] += dqg
            dkg_ref[...] += dkg

    blk = lambda off: pl.BlockSpec((S, LANES), lambda p: (0, off + p))
    row64 = pl.BlockSpec((1, HEAD_DIM), lambda p: (0, 0))
    tri = pl.BlockSpec((2 * T, T), lambda p: (0, 0))
    return pl.pallas_call(
        body, name="attn_bwd", grid=(pairs,),
        in_specs=[blk(q_blk), blk(k_blk), blk(v_blk), row64, row64,
                  pl.BlockSpec((None, S, LANES), lambda p: (p, 0, 0)), blk(0), tri, tri],
        out_specs=[blk(0), blk(0), blk(0), row64, row64],
        out_shape=[jax.ShapeDtypeStruct((S, ATTN_WIDTH), BF16)] * 3 + [jax.ShapeDtypeStruct((1, HEAD_DIM), F32)] * 2,
        scratch_shapes=[pltpu.VMEM((2, S, HEAD_DIM), BF16)] * 3 + [pltpu.VMEM((S, HEAD_DIM), F32)] * 3,
        compiler_params=_params("arbitrary"),
    )(proj, proj, proj, qg, kg, ltot, dmix, prefix_incl, prefix_excl)


def _shifted(win, n_rows):
    return [win if b == 0 else pltpu.roll(win, n_rows - b, 0) for b in range(SUBLANES)]


def _taps(variants, offsets, tm):
    return {o: variants[o % SUBLANES][(o // SUBLANES) * SUBLANES:(o // SUBLANES) * SUBLANES + tm, :] for o in offsets}


def _fold_rows(a):
    return jnp.sum(a.reshape(a.shape[0] // SUBLANES, SUBLANES, a.shape[1]), axis=0)


def _glu_conv_fwd(proj, w, bias, *, tm=256):
    S = proj.shape[0]
    CB = LANES
    a_blk, b_blk = 3 * ATTN_WIDTH // CB, (3 * ATTN_WIDTH + CONV_WIDTH) // CB
    n_rows = tm + CONV_PAD

    def body(a_ref, b_ref, w_ref, bias_ref, c1_ref, pad_s):
        pad_s[0:CONV_PAD, :] = jnp.zeros((CONV_PAD, CB), F32)

        def fill(i, carry):
            rows = pl.ds(pl.multiple_of(i * tm, tm), tm)
            pad_s[pl.ds(pl.multiple_of(CONV_PAD + i * tm, SUBLANES), tm), :] = a_ref[rows, :] * _sigmoid(b_ref[rows, :])
            return carry

        lax.fori_loop(0, S // tm, fill, 0)

        def conv(i, carry):
            r0 = pl.multiple_of(i * tm, tm)
            taps = _taps(_shifted(pad_s[pl.ds(r0, n_rows), :], n_rows), range(2, 2 + CONV_KERNEL), tm)
            acc = jnp.broadcast_to(bias_ref[...], (tm, CB))
            for k in range(CONV_KERNEL):
                acc = acc + w_ref[k:k + 1, :] * taps[k + 2]
            c1_ref[pl.ds(r0, tm), :] = acc
            return carry

        lax.fori_loop(0, S // tm, conv, 0)

    return pl.pallas_call(
        body, name="glu_conv_fwd", grid=(CONV_WIDTH // CB,),
        in_specs=[pl.BlockSpec((S, CB), lambda j: (0, a_blk + j)), pl.BlockSpec((S, CB), lambda j: (0, b_blk + j)),
                  pl.BlockSpec((CONV_KERNEL, CB), lambda j: (0, j)), pl.BlockSpec((1, CB), lambda j: (0, j))],
        out_specs=pl.BlockSpec((S, CB), lambda j: (0, j)),
        out_shape=jax.ShapeDtypeStruct((S, CONV_WIDTH), F32),
        scratch_shapes=[pltpu.VMEM((S + CONV_PAD, CB), F32)],
        compiler_params=_params("parallel"),
    )(proj, proj, w, bias)


def _glu_conv_bwd(proj, w, dc1, *, tm=256):
    S = proj.shape[0]
    CB = LANES
    a_blk, b_blk = 3 * ATTN_WIDTH // CB, (3 * ATTN_WIDTH + CONV_WIDTH) // CB
    n_rows = tm + CONV_PAD

    def body(a_ref, b_ref, w_ref, dc1_ref, da_ref, db_ref, dw_ref, dbias_ref, pad_s, dpad_s, dw_s):
        pad_s[0:CONV_PAD, :] = jnp.zeros((CONV_PAD, CB), F32)
        dpad_s[S:S + CONV_PAD, :] = jnp.zeros((CONV_PAD, CB), F32)
        dw_s[...] = jnp.zeros_like(dw_s)

        def fill(i, carry):
            rows = pl.ds(pl.multiple_of(i * tm, tm), tm)
            pad_s[pl.ds(pl.multiple_of(CONV_PAD + i * tm, SUBLANES), tm), :] = a_ref[rows, :] * _sigmoid(b_ref[rows, :])
            dpad_s[rows, :] = dc1_ref[rows, :]
            return carry

        lax.fori_loop(0, S // tm, fill, 0)

        def conv(i, carry):
            r0 = pl.multiple_of(i * tm, tm)
            rows = pl.ds(r0, tm)
            taps = _taps(_shifted(dpad_s[pl.ds(r0, n_rows), :], n_rows), range(CONV_KERNEL), tm)
            acc = jnp.zeros((tm, CB), F32)
            for k in range(CONV_KERNEL):
                acc = acc + w_ref[k:k + 1, :] * taps[CONV_KERNEL - 1 - k]
            a = a_ref[rows, :]
            sg = _sigmoid(b_ref[rows, :])
            da_ref[rows, :] = (acc * sg).astype(BF16)
            db_ref[rows, :] = (acc * a * sg * (1.0 - sg)).astype(BF16)
            d = taps[0]
            taps = _taps(_shifted(pad_s[pl.ds(r0, n_rows), :], n_rows), range(2, 2 + CONV_KERNEL), tm)
            for k in range(CONV_KERNEL):
                dw_s[SUBLANES * k:SUBLANES * (k + 1), :] += _fold_rows(d * taps[k + 2])
            dw_s[SUBLANES * CONV_KERNEL:SUBLANES * (CONV_KERNEL + 1), :] += _fold_rows(d)
            return carry

        lax.fori_loop(0, S // tm, conv, 0)
        for k in range(CONV_KERNEL):
            dw_ref[k:k + 1, :] = jnp.sum(dw_s[SUBLANES * k:SUBLANES * (k + 1), :], axis=0, keepdims=True)
        dbias_ref[...] = jnp.sum(dw_s[SUBLANES * CONV_KERNEL:SUBLANES * (CONV_KERNEL + 1), :], axis=0, keepdims=True)

    col = lambda off: pl.BlockSpec((S, CB), lambda j: (0, off + j))
    return pl.pallas_call(
        body, name="glu_conv_bwd", grid=(CONV_WIDTH // CB,),
        in_specs=[col(a_blk), col(b_blk), pl.BlockSpec((CONV_KERNEL, CB), lambda j: (0, j)), col(0)],
        out_specs=[col(0), col(0), pl.BlockSpec((CONV_KERNEL, CB), lambda j: (0, j)), pl.BlockSpec((1, CB), lambda j: (0, j))],
        out_shape=[jax.ShapeDtypeStruct((S, CONV_WIDTH), BF16)] * 2
        + [jax.ShapeDtypeStruct((CONV_KERNEL, CONV_WIDTH), F32), jax.ShapeDtypeStruct((1, CONV_WIDTH), F32)],
        scratch_shapes=[pltpu.VMEM((S + CONV_PAD, CB), F32), pltpu.VMEM((S + CONV_PAD, CB), F32),
                        pltpu.VMEM((SUBLANES * (CONV_KERNEL + 1), CB), F32)],
        compiler_params=_params("parallel"),
    )(proj, proj, w, dc1)


def _ln_stats(c1):
    mu = jnp.mean(c1, axis=-1, keepdims=True)
    xc = c1 - mu
    r = lax.rsqrt(jnp.mean(xc * xc, axis=-1, keepdims=True) + EPS)
    return xc * r, r


def _ln_silu_fwd(c1, g, b, *, tm=512):
    S, C = c1.shape

    def body(c1_ref, g_ref, b_ref, c_ref):
        yh, _ = _ln_stats(c1_ref[...])
        y = yh * g_ref[...] + b_ref[...]
        c_ref[...] = (y * _sigmoid(y)).astype(BF16)

    vec = pl.BlockSpec((1, C), lambda i: (0, 0))
    return pl.pallas_call(
        body, name="ln_silu_fwd", grid=(S // tm,),
        in_specs=[pl.BlockSpec((tm, C), lambda i: (i, 0)), vec, vec],
        out_specs=pl.BlockSpec((tm, C), lambda i: (i, 0)),
        out_shape=jax.ShapeDtypeStruct((S, C), BF16),
        compiler_params=_params("parallel"),
    )(c1, g, b)


def _ln_silu_bwd(c1, g, b, dmix, *, tm=512):
    S, C = c1.shape

    def body(c1_ref, g_ref, b_ref, dc_ref, dc1_ref, dg_ref, db_ref):
        yh, r = _ln_stats(c1_ref[...])
        y = yh * g_ref[...] + b_ref[...]
        sg = _sigmoid(y)
        dy = dc_ref[...] * (sg * (1.0 + y * (1.0 - sg)))
        dyh = dy * g_ref[...]
        dc1_ref[...] = r * (dyh - jnp.mean(dyh, axis=-1, keepdims=True)
                            - yh * jnp.mean(dyh * yh, axis=-1, keepdims=True))

        @pl.when(pl.program_id(0) == 0)
        def _():
            dg_ref[...] = jnp.zeros_like(dg_ref)
            db_ref[...] = jnp.zeros_like(db_ref)

        dg_ref[...] += jnp.sum(dy * yh, axis=0, keepdims=True)
        db_ref[...] += jnp.sum(dy, axis=0, keepdims=True)

    vec = pl.BlockSpec((1, C), lambda i: (0, 0))
    return pl.pallas_call(
        body, name="ln_silu_bwd", grid=(S // tm,),
        in_specs=[pl.BlockSpec((tm, C), lambda i: (i, 0)), vec, vec, pl.BlockSpec((tm, C), lambda i: (i, 1))],
        out_specs=[pl.BlockSpec((tm, C), lambda i: (i, 0)), vec, vec],
        out_shape=[jax.ShapeDtypeStruct((S, C), F32), jax.ShapeDtypeStruct((1, C), F32), jax.ShapeDtypeStruct((1, C), F32)],
        compiler_params=_params("arbitrary"),
    )(c1, g, b, dmix)


FFN_CB = 256


def _ffn_gate(pad_s, w_ref, bias_ref, r0, tm):
    n_rows = tm + FFN_PAD
    taps = _taps(_shifted(pad_s[pl.ds(r0, n_rows), :], n_rows), range(FFN_PAD - 2, FFN_PAD + 1), tm)
    g1 = bias_ref[...] + w_ref[0:1, :] * taps[6] + w_ref[1:2, :] * taps[7] + w_ref[2:3, :] * taps[8]
    return g1, taps


def _ffn_act_fwd(u, w, bias, *, tm=256):
    S = u.shape[0]
    CB = FFN_CB
    nb = D_FF // CB

    def body(g_ref, v_ref, w_ref, bias_ref, o_ref, pad_s):
        pad_s[0:FFN_PAD, :] = jnp.zeros((FFN_PAD, CB), F32)

        def fill(i, carry):
            pad_s[pl.ds(pl.multiple_of(FFN_PAD + i * tm, SUBLANES), tm), :] = g_ref[pl.ds(pl.multiple_of(i * tm, tm), tm), :]
            return carry

        lax.fori_loop(0, S // tm, fill, 0)

        def act(i, carry):
            r0 = pl.multiple_of(i * tm, tm)
            g1, _ = _ffn_gate(pad_s, w_ref, bias_ref, r0, tm)
            o_ref[pl.ds(r0, tm), :] = (g1 * _sigmoid(g1) * v_ref[pl.ds(r0, tm), :]).astype(BF16)
            return carry

        lax.fori_loop(0, S // tm, act, 0)

    return pl.pallas_call(
        body, name="ffn_act_fwd", grid=(nb,),
        in_specs=[pl.BlockSpec((S, CB), lambda j: (0, j)), pl.BlockSpec((S, CB), lambda j: (0, nb + j)),
                  pl.BlockSpec((FFN_KERNEL, CB), lambda j: (0, j)), pl.BlockSpec((1, CB), lambda j: (0, j))],
        out_specs=pl.BlockSpec((S, CB), lambda j: (0, j)),
        out_shape=jax.ShapeDtypeStruct((S, D_FF), BF16),
        scratch_shapes=[pltpu.VMEM((S + FFN_PAD, CB), F32)],
        compiler_params=_params("parallel"),
    )(u, u, w, bias)


def _ffn_act_bwd(u, w, bias, dact, *, tm=256):
    S = u.shape[0]
    CB = FFN_CB
    nb = D_FF // CB

    def body(g_ref, v_ref, w_ref, bias_ref, da_ref, dg_ref, dv_ref, dw_ref, dbias_ref, pad_s, dpad_s, dw_s):
        pad_s[0:FFN_PAD, :] = jnp.zeros((FFN_PAD, CB), F32)
        dpad_s[S:S + FFN_PAD, :] = jnp.zeros((FFN_PAD, CB), F32)
        dw_s[...] = jnp.zeros_like(dw_s)

        def fill(i, carry):
            pad_s[pl.ds(pl.multiple_of(FFN_PAD + i * tm, SUBLANES), tm), :] = g_ref[pl.ds(pl.multiple_of(i * tm, tm), tm), :]
            return carry

        lax.fori_loop(0, S // tm, fill, 0)

        def first(i, carry):
            r0 = pl.multiple_of(i * tm, tm)
            rows = pl.ds(r0, tm)
            g1, taps = _ffn_gate(pad_s, w_ref, bias_ref, r0, tm)
            sg = _sigmoid(g1)
            da = da_ref[rows, :].astype(F32)
            dv_ref[rows, :] = (da * g1 * sg).astype(BF16)
            dg1 = da * v_ref[rows, :] * (sg * (1.0 + g1 * (1.0 - sg)))
            dpad_s[rows, :] = dg1
            for k in range(FFN_KERNEL):
                dw_s[SUBLANES * k:SUBLANES * (k + 1), :] += _fold_rows(dg1 * taps[FFN_PAD - 2 + k])
            dw_s[SUBLANES * FFN_KERNEL:SUBLANES * (FFN_KERNEL + 1), :] += _fold_rows(dg1)
            return carry

        lax.fori_loop(0, S // tm, first, 0)

        def second(i, carry):
            r0 = pl.multiple_of(i * tm, tm)
            n_rows = tm + FFN_PAD
            taps = _taps(_shifted(dpad_s[pl.ds(r0, n_rows), :], n_rows), range(FFN_KERNEL), tm)
            dg_ref[pl.ds(r0, tm), :] = (w_ref[2:3, :] * taps[0] + w_ref[1:2, :] * taps[1] + w_ref[0:1, :] * taps[2]).astype(BF16)
            return carry

        lax.fori_loop(0, S // tm, second, 0)
        for k in range(FFN_KERNEL):
            dw_ref[k:k + 1, :] = jnp.sum(dw_s[SUBLANES * k:SUBLANES * (k + 1), :], axis=0, keepdims=True)
        dbias_ref[...] = jnp.sum(dw_s[SUBLANES * FFN_KERNEL:SUBLANES * (FFN_KERNEL + 1), :], axis=0, keepdims=True)

    col = lambda off: pl.BlockSpec((S, CB), lambda j: (0, off + j))
    wspec = pl.BlockSpec((FFN_KERNEL, CB), lambda j: (0, j))
    bspec = pl.BlockSpec((1, CB), lambda j: (0, j))
    return pl.pallas_call(
        body, name="ffn_act_bwd", grid=(nb,),
        in_specs=[col(0), col(nb), wspec, bspec, col(0)],
        out_specs=[col(0), col(0), wspec, bspec],
        out_shape=[jax.ShapeDtypeStruct((S, D_FF), BF16)] * 2
        + [jax.ShapeDtypeStruct((FFN_KERNEL, D_FF), F32), jax.ShapeDtypeStruct((1, D_FF), F32)],
        scratch_shapes=[pltpu.VMEM((S + FFN_PAD, CB), F32), pltpu.VMEM((S + FFN_PAD, CB), F32),
                        pltpu.VMEM((SUBLANES * (FFN_KERNEL + 1), CB), F32)],
        compiler_params=_params("parallel"),
    )(u, u, w, bias, dact)


def _loss_grad(y, target, *, tm=512):
    S, D = y.shape

    def body(y_ref, t_ref, dy_ref, l_ref):
        d = y_ref[...] - t_ref[...]
        dy_ref[...] = d * (1.0 / D)

        @pl.when(pl.program_id(0) == 0)
        def _():
            l_ref[...] = jnp.zeros_like(l_ref)

        l_ref[...] += 0.5 * jnp.sum(jnp.mean(d * d, axis=-1, keepdims=True), axis=0, keepdims=True)

    dy, l = pl.pallas_call(
        body, name="loss_grad", grid=(S // tm,),
        in_specs=[pl.BlockSpec((tm, D), lambda i: (i, 0))] * 2,
        out_specs=[pl.BlockSpec((tm, D), lambda i: (i, 0)), pl.BlockSpec((SUBLANES, LANES), lambda i: (0, 0))],
        out_shape=[jax.ShapeDtypeStruct((S, D), F32), jax.ShapeDtypeStruct((SUBLANES, LANES), F32)],
        compiler_params=_params("arbitrary"),
    )(y, target)
    return dy, l[0, 0]


def _row_tile(rows, cap=512):
    t = min(rows, cap)
    while rows % t or t % SUBLANES:
        t -= 1
    return t


def _adamw(w, g, m, v):
    R, C = w.shape
    tr = _row_tile(R, 256)

    def body(w_ref, g_ref, m_ref, v_ref, d_ref, nm_ref, nv_ref):
        gv = g_ref[...]
        m1 = ADAM_B1 * m_ref[...] + (1.0 - ADAM_B1) * gv
        v1 = ADAM_B2 * v_ref[...] + (1.0 - ADAM_B2) * (gv * gv)
        m_hat = m1 / (1.0 - ADAM_B1 ** ADAM_STEP)
        v_hat = v1 / (1.0 - ADAM_B2 ** ADAM_STEP)
        d_ref[...] = -ADAM_LR * (m_hat / (jnp.sqrt(v_hat) + ADAM_EPS) + ADAM_WD * w_ref[...])
        nm_ref[...] = m1
        nv_ref[...] = v1

    spec = pl.BlockSpec((tr, C), lambda i: (i, 0))
    return pl.pallas_call(
        body, name="adamw", grid=(R // tr,),
        in_specs=[spec] * 4, out_specs=[spec] * 3,
        out_shape=[jax.ShapeDtypeStruct((R, C), F32)] * 3,
        compiler_params=_params("parallel"),
    )(w, g, m, v)


def _cast_into_full(w, kind, chip):
    L, R, C = w.shape
    tr = _row_tile(R, 512)

    def body(chip_ref, w_ref, o_ref):
        o_ref[...] = w_ref[...].astype(BF16)

    if kind == "col":
        out_shape = jax.ShapeDtypeStruct((L, R, N_CHIPS * C), BF16)
        out_spec = pl.BlockSpec((None, tr, C), lambda l, i, chip_ref: (l, i, chip_ref[0]))
    else:
        out_shape = jax.ShapeDtypeStruct((L, N_CHIPS * R, C), BF16)
        out_spec = pl.BlockSpec((None, tr, C), lambda l, i, chip_ref: (l, chip_ref[0] * (R // tr) + i, 0))
    return pl.pallas_call(
        body, name="cast_into_full_" + kind,
        grid_spec=pltpu.PrefetchScalarGridSpec(
            num_scalar_prefetch=1, grid=(L, R // tr),
            in_specs=[pl.BlockSpec((None, tr, C), lambda l, i, chip_ref: (l, i, 0))], out_specs=out_spec),
        out_shape=out_shape, compiler_params=_params("parallel", "parallel"),
    )(chip, w)


def _add_half(g4, la, c):
    L, _, H, W = g4.shape
    th = _row_tile(H, 256)

    def body(c_ref, g_ref, la_ref, o_ref):
        o_ref[...] = (g_ref[...].astype(F32) + la_ref[...].astype(F32)).astype(BF16)

    return pl.pallas_call(
        body, name="add_half",
        grid_spec=pltpu.PrefetchScalarGridSpec(
            num_scalar_prefetch=1, grid=(L, H // th),
            in_specs=[pl.BlockSpec((None, None, th, W), lambda l, i, c_ref: (l, c_ref[0], i, 0)),
                      pl.BlockSpec((None, th, W), lambda l, i, c_ref: (l, i, 0))],
            out_specs=pl.BlockSpec((None, th, W), lambda l, i, c_ref: (l, i, 0))),
        out_shape=jax.ShapeDtypeStruct((L, H, W), BF16),
        compiler_params=_params("parallel", "parallel"),
    )(c, g4, la)


def _add_parts(p, lb, place, kind):
    _, L, H, C = lb.shape
    th = _row_tile(H, 256)

    def body(s_ref, p_ref, lb_ref, o_ref):
        acc = p_ref[...].astype(F32)
        for k in range(N_CHIPS - 1):
            acc = acc + lb_ref[k].astype(F32)
        o_ref[...] = acc

    if kind == "col":
        p_spec = pl.BlockSpec((None, th, C), lambda l, i, s_ref: (l, i, s_ref[0]))
    else:
        p_spec = pl.BlockSpec((None, None, th, C), lambda l, i, s_ref: (l, s_ref[0], i, 0))
    return pl.pallas_call(
        body, name="add_parts_" + kind,
        grid_spec=pltpu.PrefetchScalarGridSpec(
            num_scalar_prefetch=1, grid=(L, H // th),
            in_specs=[p_spec, pl.BlockSpec((N_CHIPS - 1, None, th, C), lambda l, i, s_ref: (0, l, i, 0))],
            out_specs=pl.BlockSpec((None, None, th, C), lambda l, i, s_ref: (l, s_ref[1], i, 0))),
        out_shape=jax.ShapeDtypeStruct((L, 2, H, C), F32),
        compiler_params=_params("parallel", "parallel"),
    )(place, p, lb)


ANY = pl.BlockSpec(memory_space=pl.ANY)


def _place():
    x, y, c = lax.axis_index("x"), lax.axis_index("y"), lax.axis_index("c")
    chips = [(1 - x, y), (x, 1 - y), (1 - x, 1 - y)]
    return x, y, c, chips


def _comm_call(body, name, ins, out_shape, n_remote, n_local, aliases=None):
    scratch = [pltpu.SemaphoreType.DMA((n_remote,)), pltpu.SemaphoreType.DMA((n_remote,))]
    if n_local:
        scratch.append(pltpu.SemaphoreType.DMA((n_local,)))
    return pl.pallas_call(
        body, name=name, in_specs=[ANY] * len(ins), out_specs=[ANY] * len(out_shape), out_shape=out_shape,
        scratch_shapes=scratch, input_output_aliases=aliases or {},
        compiler_params=pltpu.CompilerParams(has_side_effects=True),
    )(*ins)


def _remote(src, dst, send, recv, k, to):
    return pltpu.make_async_remote_copy(src_ref=src, dst_ref=dst, send_sem=send.at[k], recv_sem=recv.at[k],
                                        device_id=to, device_id_type=MESH)


def _gather_weights(fulls, kinds):
    n = len(fulls)
    per = 2 * (N_CHIPS - 1)
    out_shape = [jax.ShapeDtypeStruct(f.shape, f.dtype) for f in fulls]

    def body(*refs):
        outs, (send, recv) = refs[n:2 * n], refs[2 * n:]
        x, y, c, chips = _place()
        me = 2 * x + y
        copies, wins = [], []
        for a in range(n):
            if kinds[a] == "col":
                R, C = outs[a].shape[1], outs[a].shape[2] // N_CHIPS
            else:
                R, C = outs[a].shape[1] // N_CHIPS, outs[a].shape[2]
            H = R // 2

            def win(s, h, a=a, R=R, C=C, H=H):
                if kinds[a] == "col":
                    return outs[a].at[:, pl.ds(pl.multiple_of(h * H, 16), H), pl.ds(pl.multiple_of(s * C, LANES), C)]
                return outs[a].at[:, pl.ds(pl.multiple_of(s * R + h * H, 16), H), :]

            for k, chip in enumerate(chips):
                cp = _remote(win(me, c), win(me, c), send, recv, a * per + k, (*chip, c))
                cp.start()
                copies.append(cp)
            wins.append(win)
        for a in range(n):
            for k, chip in enumerate(chips):
                landed = wins[a](2 * chip[0] + chip[1], c)
                _remote(landed, landed, send, recv, a * per + k, (x, y, 1 - c)).wait_recv()
                cp = _remote(landed, landed, send, recv, a * per + N_CHIPS - 1 + k, (x, y, 1 - c))
                cp.start()
                copies.append(cp)
        for a in range(n):
            for k, chip in enumerate(chips):
                theirs = wins[a](2 * chip[0] + chip[1], 1 - c)
                _remote(theirs, theirs, send, recv, a * per + N_CHIPS - 1 + k, (x, y, 1 - c)).wait_recv()
        for cp in copies:
            cp.wait_send()

    return _comm_call(body, "gather_weights", fulls, out_shape, n * per, 0, {a: a for a in range(n)})


def _gather_small(shards):
    n = len(shards)
    out_shape = [jax.ShapeDtypeStruct((N_CHIPS,) + s.shape, s.dtype) for s in shards]

    def body(*refs):
        srcs, outs, (send, recv, loc) = refs[:n], refs[n:2 * n], refs[2 * n:]
        x, y, c, chips = _place()
        me = 2 * x + y
        remote, local = [], []
        for a in range(n):
            local.append(pltpu.make_async_copy(srcs[a], outs[a].at[me], loc.at[a]))
            for k, chip in enumerate(chips):
                remote.append(_remote(srcs[a], outs[a].at[me], send, recv, a * (N_CHIPS - 1) + k, (*chip, c)))
        for cp in local + remote:
            cp.start()
        for cp in remote + local:
            cp.wait()

    return _comm_call(body, "gather_small", shards, out_shape, n * (N_CHIPS - 1), n)


def _exchange_halves(g4s):
    n = len(g4s)
    out_shape = [jax.ShapeDtypeStruct((g.shape[0],) + g.shape[2:], g.dtype) for g in g4s]

    def body(*refs):
        gs, las, (send, recv) = refs[:n], refs[n:2 * n], refs[2 * n:]
        x, y, c, _ = _place()
        cps = [_remote(gs[a].at[:, 1 - c], las[a], send, recv, a, (x, y, 1 - c)) for a in range(n)]
        for cp in cps:
            cp.start()
        for cp in cps:
            cp.wait()

    return _comm_call(body, "exchange_halves", g4s, out_shape, n, 0)


def _scatter_partials(ps, kinds):
    n = len(ps)
    out_shape = []
    for p, kind in zip(ps, kinds):
        L, H, C = (p.shape[0], p.shape[1], p.shape[2] // N_CHIPS) if kind == "col" else (p.shape[0], p.shape[2], p.shape[3])
        out_shape.append(jax.ShapeDtypeStruct((N_CHIPS - 1, L, H, C), p.dtype))

    def body(*refs):
        srcs, lbs, (send, recv) = refs[:n], refs[n:2 * n], refs[2 * n:]
        x, y, c, chips = _place()
        cps = []
        for a in range(n):
            C = lbs[a].shape[3]
            for k, chip in enumerate(chips):
                s = 2 * chip[0] + chip[1]
                src = srcs[a].at[:, :, pl.ds(pl.multiple_of(s * C, LANES), C)] if kinds[a] == "col" else srcs[a].at[:, s]
                cps.append(_remote(src, lbs[a].at[k], send, recv, a * (N_CHIPS - 1) + k, (*chip, c)))
        for cp in cps:
            cp.start()
        for cp in cps:
            cp.wait()

    return _comm_call(body, "scatter_partials", ps, out_shape, n * (N_CHIPS - 1), 0)


def _share_halves(g4s):
    n = len(g4s)
    out_shape = [jax.ShapeDtypeStruct(g.shape, g.dtype) for g in g4s]

    def body(*refs):
        outs, (send, recv) = refs[n:2 * n], refs[2 * n:]
        x, y, c, _ = _place()
        cps = [_remote(outs[a].at[:, c], outs[a].at[:, c], send, recv, a, (x, y, 1 - c)) for a in range(n)]
        for cp in cps:
            cp.start()
        for cp in cps:
            cp.wait()

    return _comm_call(body, "share_halves", g4s, out_shape, n, 0, {a: a for a in range(n)})


def _allreduce_small(part):
    N = part.shape[0]

    def body(p_ref, o_ref, buf, send, recv):
        x, y, c, _ = _place()
        me = 4 * x + 2 * y + c
        buf[me] = p_ref[...]
        cps = []
        for k in range(1, N_DEV):
            peer = (1 - x if k & 4 else x, 1 - y if k & 2 else y, 1 - c if k & 1 else c)
            cps.append(_remote(p_ref, buf.at[me], send, recv, k - 1, peer))
        for cp in cps:
            cp.start()
        for cp in cps:
            cp.wait()
        acc = buf[0]
        for i in range(1, N_DEV):
            acc = acc + buf[i]
        o_ref[...] = acc

    vmem = pl.BlockSpec(memory_space=pltpu.VMEM)
    return pl.pallas_call(
        body, name="allreduce_small", in_specs=[vmem], out_specs=vmem,
        out_shape=jax.ShapeDtypeStruct((N, LANES), F32),
        scratch_shapes=[pltpu.VMEM((N_DEV, N, LANES), F32), pltpu.SemaphoreType.DMA((N_DEV - 1,)),
                        pltpu.SemaphoreType.DMA((N_DEV - 1,))],
        compiler_params=pltpu.CompilerParams(has_side_effects=True, vmem_limit_bytes=VMEM_LIMIT_BYTES),
    )(part)


def _layer_fwd(x, p):
    h1, proj = _norm_matmul(x, p["norm1_g"], p["w_in"])
    attn, ltot = _attn_fwd(proj, p["q_norm_g"], p["k_norm_g"])
    c1 = _glu_conv_fwd(proj, p["conv_dw_w"], p["conv_dw_b"])
    c = _ln_silu_fwd(c1, p["conv_ln_g"], p["conv_ln_b"])
    x_mid = _matmul_res([attn, c], p["w_out"], x)
    h2, u = _norm_matmul(x_mid, p["norm2_g"], p["w_up"])
    act = _ffn_act_fwd(u, p["ffn_dw_w"], p["ffn_dw_b"])
    x_out = _matmul_res([act], p["w_down"], x_mid)
    return x_out, dict(x=x, h1=h1, proj=proj, attn=attn, ltot=ltot, c1=c1, c=c, x_mid=x_mid, h2=h2, u=u, act=act)


def _layer_bwd(dx_out, s, p):
    g = {}
    dact = _matmul_nt([dx_out], p["w_down"], BF16)
    g["w_down"] = _matmul_tn(s["act"], dx_out, tk=D_FF // 2, tn=D_MODEL)
    dgate, dval, g["ffn_dw_w"], g["ffn_dw_b"] = _ffn_act_bwd(s["u"], p["ffn_dw_w"], p["ffn_dw_b"], dact)
    dx_mid, g["norm2_g"] = _matmul_nt_rmsbwd([dgate, dval], p["w_up"], s["x_mid"], p["norm2_g"], dx_out)
    g["w_up"] = jnp.concatenate([_matmul_tn(s["h2"], d, tk=D_MODEL, tn=D_FF // 2) for d in (dgate, dval)], axis=1)
    dmix = _matmul_nt([dx_mid], p["w_out"], F32)
    g["w_out"] = jnp.concatenate([_matmul_tn(m, dx_mid, tk=ATTN_WIDTH, tn=D_MODEL) for m in (s["attn"], s["c"])], axis=0)
    dc1, g["conv_ln_g"], g["conv_ln_b"] = _ln_silu_bwd(s["c1"], p["conv_ln_g"], p["conv_ln_b"], dmix)
    da, db, g["conv_dw_w"], g["conv_dw_b"] = _glu_conv_bwd(s["proj"], p["conv_dw_w"], dc1)
    dq, dk, dv, g["q_norm_g"], g["k_norm_g"] = _attn_bwd(s["proj"], p["q_norm_g"], p["k_norm_g"], s["ltot"], dmix)
    pieces = [dq, dk, dv, da, db]
    dx, g["norm1_g"] = _matmul_nt_rmsbwd(pieces, p["w_in"], s["x"], p["norm1_g"], dx_mid)
    g["w_in"] = jnp.concatenate([_matmul_tn(s["h1"], d, tk=D_MODEL, tn=ATTN_WIDTH) for d in pieces], axis=1)
    return dx, g


WEIGHTS = ("norm1_g", "w_in", "q_norm_g", "k_norm_g", "conv_dw_w", "conv_dw_b", "conv_ln_g", "conv_ln_b",
           "w_out", "norm2_g", "w_up", "ffn_dw_w", "ffn_dw_b", "w_down")
BIG = ("w_in", "w_out", "w_up", "w_down")
BIG_KIND = {"w_in": "col", "w_out": "row", "w_up": "col", "w_down": "row"}
SMALL_SHARDED = ("conv_dw_w", "ffn_dw_w")
REPLICATED = tuple(n for n in WEIGHTS if n not in BIG + SMALL_SHARDED)


def _pack(arrays):
    flat = jnp.concatenate([a.reshape(-1) for a in arrays])
    rows = -(-flat.shape[0] // (SUBLANES * LANES)) * SUBLANES
    return jnp.pad(flat, (0, rows * LANES - flat.shape[0])).reshape(rows, LANES)


def _unpack(packed, shapes):
    flat = packed.reshape(-1)
    out, off = [], 0
    for shape in shapes:
        size = 1
        for d in shape:
            size *= d
        out.append(flat[off:off + size].reshape(shape))
        off += size
    return out


def _unshard_last(stacked):
    n, L, K, C = stacked.shape
    return jnp.transpose(stacked, (1, 2, 0, 3)).reshape(L, K, n * C)


def kernel(x, norm1_g, w_in, q_norm_g, k_norm_g, conv_dw_w, conv_dw_b, conv_ln_g, conv_ln_b, w_out, norm2_g, w_up, ffn_dw_w, ffn_dw_b, w_down, loss_target, m_norm1_g, m_w_in, m_q_norm_g, m_k_norm_g, m_conv_dw_w, m_conv_dw_b, m_conv_ln_g, m_conv_ln_b, m_w_out, m_norm2_g, m_w_up, m_ffn_dw_w, m_ffn_dw_b, m_w_down, v_norm1_g, v_w_in, v_q_norm_g, v_k_norm_g, v_conv_dw_w, v_conv_dw_b, v_conv_ln_g, v_conv_ln_b, v_w_out, v_norm2_g, v_w_up, v_ffn_dw_w, v_ffn_dw_b, v_w_down):
    given = dict(locals())
    w = {n: given[n] for n in WEIGHTS}
    m = {n: given["m_" + n] for n in WEIGHTS}
    v = {n: given["v_" + n] for n in WEIGHTS}
    chip = 2 * lax.axis_index("x") + lax.axis_index("y")
    core = lax.axis_index("c")
    chip_arr = jnp.reshape(chip, (1,)).astype(jnp.int32)
    core_arr = jnp.reshape(core, (1,)).astype(jnp.int32)
    L = DEPTH

    own = [_cast_into_full(w[n], BIG_KIND[n], chip_arr) for n in BIG]
    full = dict(zip(BIG, _gather_weights(own, [BIG_KIND[n] for n in BIG])))
    for n, stacked in zip(SMALL_SHARDED, _gather_small([w[n] for n in SMALL_SHARDED])):
        full[n] = _unshard_last(stacked)
    params = []
    for l in range(L):
        p = {n: full[n][l] for n in BIG + SMALL_SHARDED}
        p.update({n: w[n][l][None] for n in REPLICATED})
        params.append(p)

    act = x[0]
    saved = []
    for l in range(L):
        act, s = _layer_fwd(act, params[l])
        saved.append(s)
    dx, loss_part = _loss_grad(act, loss_target[0])
    loss = lax.psum(loss_part, ("x", "y", "c"))
    grads = [None] * L
    for l in reversed(range(L)):
        dx, grads[l] = _layer_bwd(dx, saved[l], params[l])

    g4s = []
    for n in BIG:
        g = jnp.stack([grads[l][n] for l in range(L)])
        if BIG_KIND[n] == "col":
            g4s.append(g.reshape(L, 2, g.shape[1] // 2, g.shape[2]))
        else:
            rows = w[n].shape[1]
            g4s.append(g.reshape(L * N_CHIPS, 2, rows // 2, g.shape[2]))
    received = _exchange_halves(g4s)
    partial = [_add_half(g4, la, core_arr) for g4, la in zip(g4s, received)]
    partial = [p if BIG_KIND[n] == "col" else p.reshape(L, N_CHIPS, p.shape[1], p.shape[2]) for n, p in zip(BIG, partial)]
    landed = _scatter_partials(partial, [BIG_KIND[n] for n in BIG])
    place = jnp.concatenate([chip_arr, core_arr])
    halves = [_add_parts(p, lb, place, BIG_KIND[n]) for n, p, lb in zip(BIG, partial, landed)]
    grad = {n: g.reshape(w[n].shape) for n, g in zip(BIG, _share_halves(halves))}

    small = REPLICATED + SMALL_SHARDED
    small_full = [jnp.stack([grads[l][n] for l in range(L)]) for n in small]
    summed = _unpack(_allreduce_small(_pack(small_full)), [a.shape for a in small_full])
    for n, g in zip(small, summed):
        if n in REPLICATED:
            grad[n] = g.reshape(w[n].shape)
        else:
            width = w[n].shape[2]
            grad[n] = lax.dynamic_slice_in_dim(g, chip * width, width, axis=2)

    delta, new_m, new_v = {}, {}, {}
    for n in BIG:
        two_d = lambda a: a.reshape(-1, a.shape[2])
        d, nm, nv = _adamw(two_d(w[n]), two_d(grad[n]), two_d(m[n]), two_d(v[n]))
        delta[n], new_m[n], new_v[n] = (a.reshape(w[n].shape) for a in (d, nm, nv))
    shapes = [w[n].shape for n in small]
    packed = _adamw(*[_pack([src[n] for n in small]) for src in (w, grad, m, v)])
    for out, pk in zip((delta, new_m, new_v), packed):
        out.update(zip(small, _unpack(pk, shapes)))

    return (loss, dx[None], *[grad[n] for n in WEIGHTS], *[delta[n] for n in WEIGHTS],
            *[new_m[n] for n in WEIGHTS], *[new_v[n] for n in WEIGHTS])
```

```python
import functools

import jax
import jax.numpy as jnp
from jax import lax
from jax.experimental import pallas as pl
from jax.experimental.pallas import tpu as pltpu

F32 = jnp.float32
BF16 = jnp.bfloat16

DEPTH = 4
D_MODEL = 1024
HEADS = 8
HEAD_DIM = 64
ATTN_WIDTH = HEADS * HEAD_DIM
CONV_WIDTH = D_MODEL - ATTN_WIDTH
CONV_KERNEL = 31
D_FF = 2816
FFN_KERNEL = 3
EPS = 1e-6
ADAM_LR, ADAM_B1, ADAM_B2, ADAM_EPS, ADAM_WD, ADAM_STEP = 0.001, 0.9, 0.999, 1e-08, 0.01, 10

N_CHIPS = 4
N_DEV = 8
LANES = 128
SUBLANES = 8
VMEM_LIMIT_BYTES = 56 * 2**20
ATTN_TILE = 256
CONV_PAD = 32
FFN_PAD = 8
MESH = pl.DeviceIdType.MESH


def _params(*sem):
    return pltpu.CompilerParams(dimension_semantics=sem if sem else None, vmem_limit_bytes=VMEM_LIMIT_BYTES)


def _dot(a, b):
    return jnp.dot(a, b, preferred_element_type=F32)


def _dot_nt(a, b):
    return lax.dot_general(a, b, (((1,), (1,)), ((), ())), preferred_element_type=F32)


def _dot_tn(a, b):
    return lax.dot_general(a, b, (((0,), (0,)), ((), ())), preferred_element_type=F32)


def _sigmoid(x):
    return 1.0 / (1.0 + jnp.exp(-x))


def _norm_matmul(x, g, w, *, tm=512, n_split=2):
    S, D = x.shape
    N = w.shape[1]
    tn = N // n_split

    def body(x_ref, g_ref, w_ref, h_ref, y_ref):
        @pl.when(pl.program_id(1) == 0)
        def _():
            xv = x_ref[...]
            r = lax.rsqrt(jnp.mean(xv * xv, axis=-1, keepdims=True) + EPS)
            h_ref[...] = (xv * r * g_ref[...]).astype(BF16)

        y_ref[...] = _dot(h_ref[...], w_ref[...])

    return pl.pallas_call(
        body, name="norm_matmul", grid=(S // tm, n_split),
        in_specs=[pl.BlockSpec((tm, D), lambda i, j: (i, 0)),
                  pl.BlockSpec((1, D), lambda i, j: (0, 0)),
                  pl.BlockSpec((D, tn), lambda i, j: (0, j))],
        out_specs=[pl.BlockSpec((tm, D), lambda i, j: (i, 0)),
                   pl.BlockSpec((tm, tn), lambda i, j: (i, j))],
        out_shape=[jax.ShapeDtypeStruct((S, D), BF16), jax.ShapeDtypeStruct((S, N), F32)],
        compiler_params=_params("parallel", "arbitrary"),
    )(x, g, w)


def _matmul_res(pieces, w, res, *, tm=512):
    S, N = res.shape
    K = w.shape[0]
    widths = [p.shape[1] for p in pieces]
    assert sum(widths) == K

    def body(*refs):
        p_refs, (w_ref, res_ref, o_ref) = refs[:len(pieces)], refs[len(pieces):]
        acc = res_ref[...]
        off = 0
        for p_ref, kp in zip(p_refs, widths):
            acc = acc + _dot(p_ref[...], w_ref[off:off + kp, :])
            off += kp
        o_ref[...] = acc

    return pl.pallas_call(
        body, name="matmul_res", grid=(S // tm,),
        in_specs=[pl.BlockSpec((tm, kp), lambda i: (i, 0)) for kp in widths]
        + [pl.BlockSpec((K, N), lambda i: (0, 0)), pl.BlockSpec((tm, N), lambda i: (i, 0))],
        out_specs=pl.BlockSpec((tm, N), lambda i: (i, 0)),
        out_shape=jax.ShapeDtypeStruct((S, N), F32),
        compiler_params=_params("parallel"),
    )(*pieces, w, res)


def _nt_sum(p_refs, widths, w_ref):
    acc = None
    off = 0
    for p_ref, n_p in zip(p_refs, widths):
        d = _dot_nt(p_ref[...].astype(BF16), w_ref[:, off:off + n_p])
        acc = d if acc is None else acc + d
        off += n_p
    return acc


def _matmul_nt(pieces, w, out_dtype, *, tm=512):
    S = pieces[0].shape[0]
    K, N = w.shape
    widths = [p.shape[1] for p in pieces]
    assert sum(widths) == N

    def body(*refs):
        p_refs, (w_ref, o_ref) = refs[:len(pieces)], refs[len(pieces):]
        o_ref[...] = _nt_sum(p_refs, widths, w_ref).astype(out_dtype)

    return pl.pallas_call(
        body, name="matmul_nt", grid=(S // tm,),
        in_specs=[pl.BlockSpec((tm, n_p), lambda i: (i, 0)) for n_p in widths]
        + [pl.BlockSpec((K, N), lambda i: (0, 0))],
        out_specs=pl.BlockSpec((tm, K), lambda i: (i, 0)),
        out_shape=jax.ShapeDtypeStruct((S, K), out_dtype),
        compiler_params=_params("parallel"),
    )(*pieces, w)


def _matmul_nt_rmsbwd(pieces, w, x, g, dres, *, tm=256):
    S, K = x.shape
    N = w.shape[1]
    widths = [p.shape[1] for p in pieces]
    assert sum(widths) == N

    def body(*refs):
        p_refs, (w_ref, x_ref, g_ref, dres_ref, dx_ref, dg_ref) = refs[:len(pieces)], refs[len(pieces):]
        dh = _nt_sum(p_refs, widths, w_ref)
        xv = x_ref[...]
        r = lax.rsqrt(jnp.mean(xv * xv, axis=-1, keepdims=True) + EPS)
        xh = xv * r
        dxh = dh * g_ref[...]
        dx_ref[...] = dres_ref[...] + r * (dxh - xh * jnp.mean(dxh * xh, axis=-1, keepdims=True))

        @pl.when(pl.program_id(0) == 0)
        def _():
            dg_ref[...] = jnp.zeros_like(dg_ref)

        dg_ref[...] += jnp.sum(dh * xh, axis=0, keepdims=True)

    return pl.pallas_call(
        body, name="matmul_nt_rmsbwd", grid=(S // tm,),
        in_specs=[pl.BlockSpec((tm, n_p), lambda i: (i, 0)) for n_p in widths]
        + [pl.BlockSpec((K, N), lambda i: (0, 0)), pl.BlockSpec((tm, K), lambda i: (i, 0)),
           pl.BlockSpec((1, K), lambda i: (0, 0)), pl.BlockSpec((tm, K), lambda i: (i, 0))],
        out_specs=[pl.BlockSpec((tm, K), lambda i: (i, 0)), pl.BlockSpec((1, K), lambda i: (0, 0))],
        out_shape=[jax.ShapeDtypeStruct((S, K), F32), jax.ShapeDtypeStruct((1, K), F32)],
        compiler_params=_params("arbitrary"),
    )(*pieces, w, x, g, dres)


def _matmul_tn(x, dy, *, tk, tn, ts=512):
    S, M = x.shape
    N = dy.shape[1]
    n_s = S // ts

    def body(x_ref, dy_ref, o_ref, acc_ref):
        s = pl.program_id(2)

        @pl.when(s == 0)
        def _():
            acc_ref[...] = jnp.zeros_like(acc_ref)

        acc_ref[...] += _dot_tn(x_ref[...].astype(BF16), dy_ref[...].astype(BF16))

        @pl.when(s == n_s - 1)
        def _():
            o_ref[...] = acc_ref[...].astype(BF16)

    return pl.pallas_call(
        body, name="matmul_tn", grid=(M // tk, N // tn, n_s),
        in_specs=[pl.BlockSpec((ts, tk), lambda i, j, s: (s, i)),
                  pl.BlockSpec((ts, tn), lambda i, j, s: (s, j))],
        out_specs=pl.BlockSpec((tk, tn), lambda i, j, s: (i, j)),
        out_shape=jax.ShapeDtypeStruct((M, N), BF16),
        scratch_shapes=[pltpu.VMEM((tk, tn), F32)],
        compiler_params=_params("parallel", "parallel", "arbitrary"),
    )(x, dy)


def _tri_consts():
    j = jnp.arange(ATTN_TILE)[:, None]
    s = jnp.arange(ATTN_TILE)[None, :]

    def two(m):
        return jnp.concatenate([m, m], axis=0).astype(BF16)

    return two(j > s), two(j <= s), two(j < s)


def _split_hi_lo(a):
    hi = a.astype(BF16)
    lo = (a - hi.astype(F32)).astype(BF16)
    return jnp.concatenate([hi, lo], axis=1)


def _log_terms(s):
    lom = -(jnp.maximum(s, 0.0) + jnp.log(1.0 + jnp.exp(-jnp.abs(s))))
    return lom, lom + s


def _causal_mask():
    t = lax.broadcasted_iota(jnp.int32, (ATTN_TILE, ATTN_TILE), 0)
    s = lax.broadcasted_iota(jnp.int32, (ATTN_TILE, ATTN_TILE), 1)
    return s < t


def _attn_prep(h, q_ref, k_ref, v_ref, qg_ref, kg_ref, qn_s, kn_s, vb_s, n_tiles):
    T = ATTN_TILE
    lanes = slice(HEAD_DIM * h, HEAD_DIM * (h + 1))
    scale = HEAD_DIM ** -0.5

    def prep(i, carry):
        rows = pl.ds(pl.multiple_of(i * T, T), T)
        q = q_ref[rows, lanes]
        k = k_ref[rows, lanes]
        rq = lax.rsqrt(jnp.mean(q * q, axis=-1, keepdims=True) + EPS)
        rk = lax.rsqrt(jnp.mean(k * k, axis=-1, keepdims=True) + EPS)
        qn_s[h, rows, :] = (q * rq * qg_ref[...] * scale).astype(BF16)
        kn_s[h, rows, :] = (k * rk * kg_ref[...]).astype(BF16)
        vb_s[h, rows, :] = v_ref[rows, lanes].astype(BF16)
        return carry

    lax.fori_loop(0, n_tiles, prep, 0)


def _attn_fwd(proj, qg, kg):
    S = proj.shape[0]
    T = ATTN_TILE
    n_tiles = S // T
    suffix, _, _ = _tri_consts()
    pairs = HEADS // 2
    q_blk, k_blk, v_blk = 0, ATTN_WIDTH // LANES, 2 * ATTN_WIDTH // LANES

    def body(q_ref, k_ref, v_ref, qg_ref, kg_ref, tri_ref, o_ref, lt_ref, qn_s, kn_s, vb_s):
        heads = range(2)
        for h in heads:
            _attn_prep(h, q_ref, k_ref, v_ref, qg_ref, kg_ref, qn_s, kn_s, vb_s, n_tiles)

        def q_tile(qi, carry0):
            qrows = pl.ds(pl.multiple_of(qi * T, T), T)
            qt = [qn_s[h, qrows, :] for h in heads]

            def tile(kj, carry, diag):
                krows = pl.ds(pl.multiple_of(kj * T, T), T)
                s = [_dot_nt(qt[h], kn_s[h, krows, :]) for h in heads]
                terms = [_log_terms(s[h]) for h in heads]
                lom = [terms[h][0] for h in heads]
                if diag:
                    mask = _causal_mask()
                    lom = [jnp.where(mask, lom[h], 0.0) for h in heads]
                tail = [_dot(_split_hi_lo(lom[h]), tri_ref[...]) for h in heads]
                w = [jnp.exp(terms[h][1] + tail[h] + carry[h][1]) for h in heads]
                if diag:
                    w = [jnp.where(mask, w[h], 0.0) for h in heads]
                return tuple((carry[h][0] + _dot(w[h].astype(BF16), vb_s[h, krows, :]),
                              carry[h][1] + (tail[h][:, 0:1] + lom[h][:, 0:1])) for h in heads)

            init = tuple((jnp.zeros((T, HEAD_DIM), F32), jnp.zeros((T, 1), F32)) for h in heads)
            carry = lax.fori_loop(0, qi, lambda t, cr: tile(qi - 1 - t, cr, False), tile(qi, init, True))
            for h in heads:
                lanes = slice(HEAD_DIM * h, HEAD_DIM * (h + 1))
                o_ref[qrows, lanes] = carry[h][0].astype(BF16)
                lt_ref[qrows, lanes] = jnp.broadcast_to(carry[h][1], (T, HEAD_DIM))
            return carry0

        lax.fori_loop(0, n_tiles, q_tile, 0)

    return pl.pallas_call(
        body, name="attn_fwd", grid=(pairs,),
        in_specs=[pl.BlockSpec((S, LANES), lambda p: (0, q_blk + p)),
                  pl.BlockSpec((S, LANES), lambda p: (0, k_blk + p)),
                  pl.BlockSpec((S, LANES), lambda p: (0, v_blk + p)),
                  pl.BlockSpec((1, HEAD_DIM), lambda p: (0, 0)),
                  pl.BlockSpec((1, HEAD_DIM), lambda p: (0, 0)),
                  pl.BlockSpec((2 * T, T), lambda p: (0, 0))],
        out_specs=[pl.BlockSpec((S, LANES), lambda p: (0, p)),
                   pl.BlockSpec((None, S, LANES), lambda p: (p, 0, 0))],
        out_shape=[jax.ShapeDtypeStruct((S, ATTN_WIDTH), BF16),
                   jax.ShapeDtypeStruct((pairs, S, LANES), F32)],
        scratch_shapes=[pltpu.VMEM((2, S, HEAD_DIM), BF16)] * 3,
        compiler_params=_params("parallel"),
    )(proj, proj, proj, qg, kg, suffix)


def _attn_bwd(proj, qg, kg, ltot, dmix):
    S = proj.shape[0]
    T = ATTN_TILE
    n_tiles = S // T
    _, prefix_incl, prefix_excl = _tri_consts()
    pairs = HEADS // 2
    q_blk, k_blk, v_blk = 0, ATTN_WIDTH // LANES, 2 * ATTN_WIDTH // LANES
    scale = HEAD_DIM ** -0.5

    def body(q_ref, k_ref, v_ref, qg_ref, kg_ref, lt_ref, do_ref, ti_ref, te_ref,
             dq_ref, dk_ref, dv_ref, dqg_ref, dkg_ref, qn_s, kn_s, vb_s, dq_s, dk_s, dv_s):
        @pl.when(pl.program_id(0) == 0)
        def _():
            dqg_ref[...] = jnp.zeros_like(dqg_ref)
            dkg_ref[...] = jnp.zeros_like(dkg_ref)

        heads = range(2)
        for h in heads:
            _attn_prep(h, q_ref, k_ref, v_ref, qg_ref, kg_ref, qn_s, kn_s, vb_s, n_tiles)
        dk_s[...] = jnp.zeros_like(dk_s)
        dv_s[...] = jnp.zeros_like(dv_s)

        def q_tile(qi, carry0):
            qrows = pl.ds(pl.multiple_of(qi * T, T), T)
            qt = [qn_s[h, qrows, :] for h in heads]
            dob = [do_ref[qrows, HEAD_DIM * h:HEAD_DIM * (h + 1)].astype(BF16) for h in heads]
            lt = [lt_ref[qrows, HEAD_DIM * h:HEAD_DIM * h + 1] for h in heads]

            def tile(kj, carry, diag):
                krows = pl.ds(pl.multiple_of(kj * T, T), T)
                kt = [kn_s[h, krows, :] for h in heads]
                s = [_dot_nt(qt[h], kt[h]) for h in heads]
                dw = [_dot_nt(dob[h], vb_s[h, krows, :]) for h in heads]
                terms = [_log_terms(s[h]) for h in heads]
                lom = [terms[h][0] for h in heads]
                if diag:
                    mask = _causal_mask()
                    lom = [jnp.where(mask, lom[h], 0.0) for h in heads]
                pin = [_dot(_split_hi_lo(lom[h]), ti_ref[...]) for h in heads]
                w = [jnp.exp(terms[h][1] + ((lt[h] - carry[h][1]) - pin[h])) for h in heads]
                if diag:
                    w = [jnp.where(mask, w[h], 0.0) for h in heads]
                e = [w[h] * dw[h] for h in heads]
                gex = [_dot(_split_hi_lo(e[h]), te_ref[...]) for h in heads]
                dz = [e[h] - jnp.exp(terms[h][1]) * (e[h] + (carry[h][2] + gex[h])) for h in heads]
                if diag:
                    dz = [jnp.where(mask, dz[h], 0.0) for h in heads]
                new = []
                for h in heads:
                    dzb = dz[h].astype(BF16)
                    dk_s[h, krows, :] += _dot_tn(dzb, qt[h])
                    dv_s[h, krows, :] += _dot_tn(w[h].astype(BF16), dob[h])
                    new.append((carry[h][0] + _dot(dzb, kt[h]),
                                carry[h][1] + pin[h][:, T - 1:T],
                                carry[h][2] + gex[h][:, T - 1:T] + e[h][:, T - 1:T]))
                return tuple(new)

            zero = jnp.zeros((T, 1), F32)
            init = tuple((jnp.zeros((T, HEAD_DIM), F32), zero, zero) for h in heads)
            last = tile(qi, lax.fori_loop(0, qi, lambda kj, cr: tile(kj, cr, False), init), True)
            for h in heads:
                dq_s[h, qrows, :] = last[h][0]
            return carry0

        lax.fori_loop(0, n_tiles, q_tile, 0)

        def finish(i, carry):
            rows = pl.ds(pl.multiple_of(i * T, T), T)
            new = []
            for h in heads:
                lanes = slice(HEAD_DIM * h, HEAD_DIM * (h + 1))
                q = q_ref[rows, lanes]
                k = k_ref[rows, lanes]
                rq = lax.rsqrt(jnp.mean(q * q, axis=-1, keepdims=True) + EPS)
                rk = lax.rsqrt(jnp.mean(k * k, axis=-1, keepdims=True) + EPS)
                qh = q * rq
                kh = k * rk
                dqn = dq_s[h, rows, :] * scale
                dkn = dk_s[h, rows, :]
                dqh = dqn * qg_ref[...]
                dkh = dkn * kg_ref[...]
                dq_ref[rows, lanes] = (rq * (dqh - qh * jnp.mean(dqh * qh, axis=-1, keepdims=True))).astype(BF16)
                dk_ref[rows, lanes] = (rk * (dkh - kh * jnp.mean(dkh * kh, axis=-1, keepdims=True))).astype(BF16)
                dv_ref[rows, lanes] = dv_s[h, rows, :].astype(BF16)
                new.append(carry[2 * h] + jnp.sum(dqn * qh, axis=0, keepdims=True))
                new.append(carry[2 * h + 1] + jnp.sum(dkn * kh, axis=0, keepdims=True))
            return tuple(new)

        zero = jnp.zeros((1, HEAD_DIM), F32)
        sums = lax.fori_loop(0, n_tiles, finish, (zero,) * 4)
        dqg_ref[...] += sums[0] + sums[2]
        dkg_ref[...] += sums[1] + sums[3]

    blk = lambda off: pl.BlockSpec((S, LANES), lambda p: (0, off + p))
    row64 = pl.BlockSpec((1, HEAD_DIM), lambda p: (0, 0))
    tri = pl.BlockSpec((2 * T, T), lambda p: (0, 0))
    return pl.pallas_call(
        body, name="attn_bwd", grid=(pairs,),
        in_specs=[blk(q_blk), blk(k_blk), blk(v_blk), row64, row64,
                  pl.BlockSpec((None, S, LANES), lambda p: (p, 0, 0)), blk(0), tri, tri],
        out_specs=[blk(0), blk(0), blk(0), row64, row64],
        out_shape=[jax.ShapeDtypeStruct((S, ATTN_WIDTH), BF16)] * 3 + [jax.ShapeDtypeStruct((1, HEAD_DIM), F32)] * 2,
        scratch_shapes=[pltpu.VMEM((2, S, HEAD_DIM), BF16)] * 3 + [pltpu.VMEM((2, S, HEAD_DIM), F32)] * 3,
        compiler_params=_params("arbitrary"),
    )(proj, proj, proj, qg, kg, ltot, dmix, prefix_incl, prefix_excl)


def _shifted(win, n_rows):
    return [win if b == 0 else pltpu.roll(win, n_rows - b, 0) for b in range(SUBLANES)]


def _taps(variants, offsets, tm):
    return {o: variants[o % SUBLANES][(o // SUBLANES) * SUBLANES:(o // SUBLANES) * SUBLANES + tm, :] for o in offsets}


def _fold_rows(a):
    return jnp.sum(a.reshape(a.shape[0] // SUBLANES, SUBLANES, a.shape[1]), axis=0)


def _glu_conv_fwd(proj, w, bias, *, tm=256):
    S = proj.shape[0]
    CB = LANES
    a_blk, b_blk = 3 * ATTN_WIDTH // CB, (3 * ATTN_WIDTH + CONV_WIDTH) // CB
    n_rows = tm + CONV_PAD

    def body(a_ref, b_ref, w_ref, bias_ref, c1_ref, pad_s):
        pad_s[0:CONV_PAD, :] = jnp.zeros((CONV_PAD, CB), F32)

        def fill(i, carry):
            rows = pl.ds(pl.multiple_of(i * tm, tm), tm)
            pad_s[pl.ds(pl.multiple_of(CONV_PAD + i * tm, SUBLANES), tm), :] = a_ref[rows, :] * _sigmoid(b_ref[rows, :])
            return carry

        lax.fori_loop(0, S // tm, fill, 0)

        def conv(i, carry):
            r0 = pl.multiple_of(i * tm, tm)
            taps = _taps(_shifted(pad_s[pl.ds(r0, n_rows), :], n_rows), range(2, 2 + CONV_KERNEL), tm)
            acc = jnp.broadcast_to(bias_ref[...], (tm, CB))
            for k in range(CONV_KERNEL):
                acc = acc + w_ref[k:k + 1, :] * taps[k + 2]
            c1_ref[pl.ds(r0, tm), :] = acc
            return carry

        lax.fori_loop(0, S // tm, conv, 0)

    return pl.pallas_call(
        body, name="glu_conv_fwd", grid=(CONV_WIDTH // CB,),
        in_specs=[pl.BlockSpec((S, CB), lambda j: (0, a_blk + j)), pl.BlockSpec((S, CB), lambda j: (0, b_blk + j)),
                  pl.BlockSpec((CONV_KERNEL, CB), lambda j: (0, j)), pl.BlockSpec((1, CB), lambda j: (0, j))],
        out_specs=pl.BlockSpec((S, CB), lambda j: (0, j)),
        out_shape=jax.ShapeDtypeStruct((S, CONV_WIDTH), F32),
        scratch_shapes=[pltpu.VMEM((S + CONV_PAD, CB), F32)],
        compiler_params=_params("parallel"),
    )(proj, proj, w, bias)


def _glu_conv_bwd(proj, w, dc1, *, tm=256):
    S = proj.shape[0]
    CB = LANES
    a_blk, b_blk = 3 * ATTN_WIDTH // CB, (3 * ATTN_WIDTH + CONV_WIDTH) // CB
    n_rows = tm + CONV_PAD

    def body(a_ref, b_ref, w_ref, dc1_ref, da_ref, db_ref, dw_ref, dbias_ref, pad_s, dpad_s, dw_s):
        pad_s[0:CONV_PAD, :] = jnp.zeros((CONV_PAD, CB), F32)
        dpad_s[S:S + CONV_PAD, :] = jnp.zeros((CONV_PAD, CB), F32)
        dw_s[...] = jnp.zeros_like(dw_s)

        def fill(i, carry):
            rows = pl.ds(pl.multiple_of(i * tm, tm), tm)
            pad_s[pl.ds(pl.multiple_of(CONV_PAD + i * tm, SUBLANES), tm), :] = a_ref[rows, :] * _sigmoid(b_ref[rows, :])
            dpad_s[rows, :] = dc1_ref[rows, :]
            return carry

        lax.fori_loop(0, S // tm, fill, 0)

        def conv(i, carry):
            r0 = pl.multiple_of(i * tm, tm)
            rows = pl.ds(r0, tm)
            taps = _taps(_shifted(dpad_s[pl.ds(r0, n_rows), :], n_rows), range(CONV_KERNEL), tm)
            acc = jnp.zeros((tm, CB), F32)
            for k in range(CONV_KERNEL):
                acc = acc + w_ref[k:k + 1, :] * taps[CONV_KERNEL - 1 - k]
            a = a_ref[rows, :]
            sg = _sigmoid(b_ref[rows, :])
            da_ref[rows, :] = (acc * sg).astype(BF16)
            db_ref[rows, :] = (acc * a * sg * (1.0 - sg)).astype(BF16)
            d = taps[0]
            taps = _taps(_shifted(pad_s[pl.ds(r0, n_rows), :], n_rows), range(2, 2 + CONV_KERNEL), tm)
            for k in range(CONV_KERNEL):
                dw_s[SUBLANES * k:SUBLANES * (k + 1), :] += _fold_rows(d * taps[k + 2])
            dw_s[SUBLANES * CONV_KERNEL:SUBLANES * (CONV_KERNEL + 1), :] += _fold_rows(d)
            return carry

        lax.fori_loop(0, S // tm, conv, 0)
        for k in range(CONV_KERNEL):
            dw_ref[k:k + 1, :] = jnp.sum(dw_s[SUBLANES * k:SUBLANES * (k + 1), :], axis=0, keepdims=True)
        dbias_ref[...] = jnp.sum(dw_s[SUBLANES * CONV_KERNEL:SUBLANES * (CONV_KERNEL + 1), :], axis=0, keepdims=True)

    col = lambda off: pl.BlockSpec((S, CB), lambda j: (0, off + j))
    return pl.pallas_call(
        body, name="glu_conv_bwd", grid=(CONV_WIDTH // CB,),
        in_specs=[col(a_blk), col(b_blk), pl.BlockSpec((CONV_KERNEL, CB), lambda j: (0, j)), col(0)],
        out_specs=[col(0), col(0), pl.BlockSpec((CONV_KERNEL, CB), lambda j: (0, j)), pl.BlockSpec((1, CB), lambda j: (0, j))],
        out_shape=[jax.ShapeDtypeStruct((S, CONV_WIDTH), BF16)] * 2
        + [jax.ShapeDtypeStruct((CONV_KERNEL, CONV_WIDTH), F32), jax.ShapeDtypeStruct((1, CONV_WIDTH), F32)],
        scratch_shapes=[pltpu.VMEM((S + CONV_PAD, CB), F32), pltpu.VMEM((S + CONV_PAD, CB), F32),
                        pltpu.VMEM((SUBLANES * (CONV_KERNEL + 1), CB), F32)],
        compiler_params=_params("parallel"),
    )(proj, proj, w, dc1)


def _ln_stats(c1):
    mu = jnp.mean(c1, axis=-1, keepdims=True)
    xc = c1 - mu
    r = lax.rsqrt(jnp.mean(xc * xc, axis=-1, keepdims=True) + EPS)
    return xc * r, r


def _ln_silu_fwd(c1, g, b, *, tm=512):
    S, C = c1.shape

    def body(c1_ref, g_ref, b_ref, c_ref):
        yh, _ = _ln_stats(c1_ref[...])
        y = yh * g_ref[...] + b_ref[...]
        c_ref[...] = (y * _sigmoid(y)).astype(BF16)

    vec = pl.BlockSpec((1, C), lambda i: (0, 0))
    return pl.pallas_call(
        body, name="ln_silu_fwd", grid=(S // tm,),
        in_specs=[pl.BlockSpec((tm, C), lambda i: (i, 0)), vec, vec],
        out_specs=pl.BlockSpec((tm, C), lambda i: (i, 0)),
        out_shape=jax.ShapeDtypeStruct((S, C), BF16),
        compiler_params=_params("parallel"),
    )(c1, g, b)


def _ln_silu_bwd(c1, g, b, dmix, *, tm=512):
    S, C = c1.shape

    def body(c1_ref, g_ref, b_ref, dc_ref, dc1_ref, dg_ref, db_ref):
        yh, r = _ln_stats(c1_ref[...])
        y = yh * g_ref[...] + b_ref[...]
        sg = _sigmoid(y)
        dy = dc_ref[...] * (sg * (1.0 + y * (1.0 - sg)))
        dyh = dy * g_ref[...]
        dc1_ref[...] = r * (dyh - jnp.mean(dyh, axis=-1, keepdims=True)
                            - yh * jnp.mean(dyh * yh, axis=-1, keepdims=True))

        @pl.when(pl.program_id(0) == 0)
        def _():
            dg_ref[...] = jnp.zeros_like(dg_ref)
            db_ref[...] = jnp.zeros_like(db_ref)

        dg_ref[...] += jnp.sum(dy * yh, axis=0, keepdims=True)
        db_ref[...] += jnp.sum(dy, axis=0, keepdims=True)

    vec = pl.BlockSpec((1, C), lambda i: (0, 0))
    return pl.pallas_call(
        body, name="ln_silu_bwd", grid=(S // tm,),
        in_specs=[pl.BlockSpec((tm, C), lambda i: (i, 0)), vec, vec, pl.BlockSpec((tm, C), lambda i: (i, 1))],
        out_specs=[pl.BlockSpec((tm, C), lambda i: (i, 0)), vec, vec],
        out_shape=[jax.ShapeDtypeStruct((S, C), F32), jax.ShapeDtypeStruct((1, C), F32), jax.ShapeDtypeStruct((1, C), F32)],
        compiler_params=_params("arbitrary"),
    )(c1, g, b, dmix)


FFN_CB = 256


def _ffn_gate(pad_s, w_ref, bias_ref, r0, tm):
    n_rows = tm + FFN_PAD
    taps = _taps(_shifted(pad_s[pl.ds(r0, n_rows), :], n_rows), range(FFN_PAD - 2, FFN_PAD + 1), tm)
    g1 = bias_ref[...] + w_ref[0:1, :] * taps[6] + w_ref[1:2, :] * taps[7] + w_ref[2:3, :] * taps[8]
    return g1, taps


def _ffn_act_fwd(u, w, bias, *, tm=256):
    S = u.shape[0]
    CB = FFN_CB
    nb = D_FF // CB

    def body(g_ref, v_ref, w_ref, bias_ref, o_ref, pad_s):
        pad_s[0:FFN_PAD, :] = jnp.zeros((FFN_PAD, CB), F32)

        def fill(i, carry):
            pad_s[pl.ds(pl.multiple_of(FFN_PAD + i * tm, SUBLANES), tm), :] = g_ref[pl.ds(pl.multiple_of(i * tm, tm), tm), :]
            return carry

        lax.fori_loop(0, S // tm, fill, 0)

        def act(i, carry):
            r0 = pl.multiple_of(i * tm, tm)
            g1, _ = _ffn_gate(pad_s, w_ref, bias_ref, r0, tm)
            o_ref[pl.ds(r0, tm), :] = (g1 * _sigmoid(g1) * v_ref[pl.ds(r0, tm), :]).astype(BF16)
            return carry

        lax.fori_loop(0, S // tm, act, 0)

    return pl.pallas_call(
        body, name="ffn_act_fwd", grid=(nb,),
        in_specs=[pl.BlockSpec((S, CB), lambda j: (0, j)), pl.BlockSpec((S, CB), lambda j: (0, nb + j)),
                  pl.BlockSpec((FFN_KERNEL, CB), lambda j: (0, j)), pl.BlockSpec((1, CB), lambda j: (0, j))],
        out_specs=pl.BlockSpec((S, CB), lambda j: (0, j)),
        out_shape=jax.ShapeDtypeStruct((S, D_FF), BF16),
        scratch_shapes=[pltpu.VMEM((S + FFN_PAD, CB), F32)],
        compiler_params=_params("parallel"),
    )(u, u, w, bias)


def _ffn_act_bwd(u, w, bias, dact, *, tm=256):
    S = u.shape[0]
    CB = FFN_CB
    nb = D_FF // CB

    def body(g_ref, v_ref, w_ref, bias_ref, da_ref, dg_ref, dv_ref, dw_ref, dbias_ref, pad_s, dpad_s, dw_s):
        pad_s[0:FFN_PAD, :] = jnp.zeros((FFN_PAD, CB), F32)
        dpad_s[S:S + FFN_PAD, :] = jnp.zeros((FFN_PAD, CB), F32)
        dw_s[...] = jnp.zeros_like(dw_s)

        def fill(i, carry):
            pad_s[pl.ds(pl.multiple_of(FFN_PAD + i * tm, SUBLANES), tm), :] = g_ref[pl.ds(pl.multiple_of(i * tm, tm), tm), :]
            return carry

        lax.fori_loop(0, S // tm, fill, 0)

        def first(i, carry):
            r0 = pl.multiple_of(i * tm, tm)
            rows = pl.ds(r0, tm)
            g1, taps = _ffn_gate(pad_s, w_ref, bias_ref, r0, tm)
            sg = _sigmoid(g1)
            da = da_ref[rows, :].astype(F32)
            dv_ref[rows, :] = (da * g1 * sg).astype(BF16)
            dg1 = da * v_ref[rows, :] * (sg * (1.0 + g1 * (1.0 - sg)))
            dpad_s[rows, :] = dg1
            for k in range(FFN_KERNEL):
                dw_s[SUBLANES * k:SUBLANES * (k + 1), :] += _fold_rows(dg1 * taps[FFN_PAD - 2 + k])
            dw_s[SUBLANES * FFN_KERNEL:SUBLANES * (FFN_KERNEL + 1), :] += _fold_rows(dg1)
            return carry

        lax.fori_loop(0, S // tm, first, 0)

        def second(i, carry):
            r0 = pl.multiple_of(i * tm, tm)
            n_rows = tm + FFN_PAD
            taps = _taps(_shifted(dpad_s[pl.ds(r0, n_rows), :], n_rows), range(FFN_KERNEL), tm)
            dg_ref[pl.ds(r0, tm), :] = (w_ref[2:3, :] * taps[0] + w_ref[1:2, :] * taps[1] + w_ref[0:1, :] * taps[2]).astype(BF16)
            return carry

        lax.fori_loop(0, S // tm, second, 0)
        for k in range(FFN_KERNEL):
            dw_ref[k:k + 1, :] = jnp.sum(dw_s[SUBLANES * k:SUBLANES * (k + 1), :], axis=0, keepdims=True)
        dbias_ref[...] = jnp.sum(dw_s[SUBLANES * FFN_KERNEL:SUBLANES * (FFN_KERNEL + 1), :], axis=0, keepdims=True)

    col = lambda off: pl.BlockSpec((S, CB), lambda j: (0, off + j))
    wspec = pl.BlockSpec((FFN_KERNEL, CB), lambda j: (0, j))
    bspec = pl.BlockSpec((1, CB), lambda j: (0, j))
    return pl.pallas_call(
        body, name="ffn_act_bwd", grid=(nb,),
        in_specs=[col(0), col(nb), wspec, bspec, col(0)],
        out_specs=[col(0), col(0), wspec, bspec],
        out_shape=[jax.ShapeDtypeStruct((S, D_FF), BF16)] * 2
        + [jax.ShapeDtypeStruct((FFN_KERNEL, D_FF), F32), jax.ShapeDtypeStruct((1, D_FF), F32)],
        scratch_shapes=[pltpu.VMEM((S + FFN_PAD, CB), F32), pltpu.VMEM((S + FFN_PAD, CB), F32),
                        pltpu.VMEM((SUBLANES * (FFN_KERNEL + 1), CB), F32)],
        compiler_params=_params("parallel"),
    )(u, u, w, bias, dact)


def _loss_grad(y, target, *, tm=512):
    S, D = y.shape

    def body(y_ref, t_ref, dy_ref, l_ref):
        d = y_ref[...] - t_ref[...]
        dy_ref[...] = d * (1.0 / D)

        @pl.when(pl.program_id(0) == 0)
        def _():
            l_ref[...] = jnp.zeros_like(l_ref)

        l_ref[...] += 0.5 * jnp.sum(jnp.mean(d * d, axis=-1, keepdims=True), axis=0, keepdims=True)

    dy, l = pl.pallas_call(
        body, name="loss_grad", grid=(S // tm,),
        in_specs=[pl.BlockSpec((tm, D), lambda i: (i, 0))] * 2,
        out_specs=[pl.BlockSpec((tm, D), lambda i: (i, 0)), pl.BlockSpec((SUBLANES, LANES), lambda i: (0, 0))],
        out_shape=[jax.ShapeDtypeStruct((S, D), F32), jax.ShapeDtypeStruct((SUBLANES, LANES), F32)],
        compiler_params=_params("arbitrary"),
    )(y, target)
    return dy, l[0, 0]


def _row_tile(rows, cap=512):
    t = min(rows, cap)
    while rows % t or t % SUBLANES:
        t -= 1
    return t


def _adamw(w, g, m, v):
    R, C = w.shape
    tr = _row_tile(R, 256)

    def body(w_ref, g_ref, m_ref, v_ref, d_ref, nm_ref, nv_ref):
        gv = g_ref[...]
        m1 = ADAM_B1 * m_ref[...] + (1.0 - ADAM_B1) * gv
        v1 = ADAM_B2 * v_ref[...] + (1.0 - ADAM_B2) * (gv * gv)
        m_hat = m1 / (1.0 - ADAM_B1 ** ADAM_STEP)
        v_hat = v1 / (1.0 - ADAM_B2 ** ADAM_STEP)
        d_ref[...] = -ADAM_LR * (m_hat / (jnp.sqrt(v_hat) + ADAM_EPS) + ADAM_WD * w_ref[...])
        nm_ref[...] = m1
        nv_ref[...] = v1

    spec = pl.BlockSpec((tr, C), lambda i: (i, 0))
    return pl.pallas_call(
        body, name="adamw", grid=(R // tr,),
        in_specs=[spec] * 4, out_specs=[spec] * 3,
        out_shape=[jax.ShapeDtypeStruct((R, C), F32)] * 3,
        compiler_params=_params("parallel"),
    )(w, g, m, v)


def _cast_into_full(w, kind, chip):
    L, R, C = w.shape
    tr = _row_tile(R, 512)

    def body(chip_ref, w_ref, o_ref):
        o_ref[...] = w_ref[...].astype(BF16)

    if kind == "col":
        out_shape = jax.ShapeDtypeStruct((L, R, N_CHIPS * C), BF16)
        out_spec = pl.BlockSpec((None, tr, C), lambda l, i, chip_ref: (l, i, chip_ref[0]))
    else:
        out_shape = jax.ShapeDtypeStruct((L, N_CHIPS * R, C), BF16)
        out_spec = pl.BlockSpec((None, tr, C), lambda l, i, chip_ref: (l, chip_ref[0] * (R // tr) + i, 0))
    return pl.pallas_call(
        body, name="cast_into_full_" + kind,
        grid_spec=pltpu.PrefetchScalarGridSpec(
            num_scalar_prefetch=1, grid=(L, R // tr),
            in_specs=[pl.BlockSpec((None, tr, C), lambda l, i, chip_ref: (l, i, 0))], out_specs=out_spec),
        out_shape=out_shape, compiler_params=_params("parallel", "parallel"),
    )(chip, w)


def _add_half(g4, la, c):
    L, _, H, W = g4.shape
    th = _row_tile(H, 256)

    def body(c_ref, g_ref, la_ref, o_ref):
        o_ref[...] = (g_ref[...].astype(F32) + la_ref[...].astype(F32)).astype(BF16)

    return pl.pallas_call(
        body, name="add_half",
        grid_spec=pltpu.PrefetchScalarGridSpec(
            num_scalar_prefetch=1, grid=(L, H // th),
            in_specs=[pl.BlockSpec((None, None, th, W), lambda l, i, c_ref: (l, c_ref[0], i, 0)),
                      pl.BlockSpec((None, th, W), lambda l, i, c_ref: (l, i, 0))],
            out_specs=pl.BlockSpec((None, th, W), lambda l, i, c_ref: (l, i, 0))),
        out_shape=jax.ShapeDtypeStruct((L, H, W), BF16),
        compiler_params=_params("parallel", "parallel"),
    )(c, g4, la)


def _add_parts(p, lb, place, kind):
    _, L, H, C = lb.shape
    th = _row_tile(H, 256)

    def body(s_ref, p_ref, lb_ref, o_ref):
        acc = p_ref[...].astype(F32)
        for k in range(N_CHIPS - 1):
            acc = acc + lb_ref[k].astype(F32)
        o_ref[...] = acc

    if kind == "col":
        p_spec = pl.BlockSpec((None, th, C), lambda l, i, s_ref: (l, i, s_ref[0]))
    else:
        p_spec = pl.BlockSpec((None, None, th, C), lambda l, i, s_ref: (l, s_ref[0], i, 0))
    return pl.pallas_call(
        body, name="add_parts_" + kind,
        grid_spec=pltpu.PrefetchScalarGridSpec(
            num_scalar_prefetch=1, grid=(L, H // th),
            in_specs=[p_spec, pl.BlockSpec((N_CHIPS - 1, None, th, C), lambda l, i, s_ref: (0, l, i, 0))],
            out_specs=pl.BlockSpec((None, None, th, C), lambda l, i, s_ref: (l, s_ref[1], i, 0))),
        out_shape=jax.ShapeDtypeStruct((L, 2, H, C), F32),
        compiler_params=_params("parallel", "parallel"),
    )(place, p, lb)


ANY = pl.BlockSpec(memory_space=pl.ANY)


def _place():
    x, y, c = lax.axis_index("x"), lax.axis_index("y"), lax.axis_index("c")
    chips = [(1 - x, y), (x, 1 - y), (1 - x, 1 - y)]
    return x, y, c, chips


def _comm_call(body, name, ins, out_shape, n_remote, n_local, aliases=None):
    scratch = [pltpu.SemaphoreType.DMA((n_remote,)), pltpu.SemaphoreType.DMA((n_remote,))]
    if n_local:
        scratch.append(pltpu.SemaphoreType.DMA((n_local,)))
    return pl.pallas_call(
        body, name=name, in_specs=[ANY] * len(ins), out_specs=[ANY] * len(out_shape), out_shape=out_shape,
        scratch_shapes=scratch, input_output_aliases=aliases or {},
        compiler_params=pltpu.CompilerParams(has_side_effects=True),
    )(*ins)


def _remote(src, dst, send, recv, k, to):
    return pltpu.make_async_remote_copy(src_ref=src, dst_ref=dst, send_sem=send.at[k], recv_sem=recv.at[k],
                                        device_id=to, device_id_type=MESH)


def _gather_weights(fulls, kinds):
    n = len(fulls)
    per = 2 * (N_CHIPS - 1)
    out_shape = [jax.ShapeDtypeStruct(f.shape, f.dtype) for f in fulls]

    def body(*refs):
        outs, (send, recv) = refs[n:2 * n], refs[2 * n:]
        x, y, c, chips = _place()
        me = 2 * x + y
        copies, wins = [], []
        for a in range(n):
            if kinds[a] == "col":
                R, C = outs[a].shape[1], outs[a].shape[2] // N_CHIPS
            else:
                R, C = outs[a].shape[1] // N_CHIPS, outs[a].shape[2]
            H = R // 2

            def win(s, h, a=a, R=R, C=C, H=H):
                if kinds[a] == "col":
                    return outs[a].at[:, pl.ds(pl.multiple_of(h * H, 16), H), pl.ds(pl.multiple_of(s * C, LANES), C)]
                return outs[a].at[:, pl.ds(pl.multiple_of(s * R + h * H, 16), H), :]

            for k, chip in enumerate(chips):
                cp = _remote(win(me, c), win(me, c), send, recv, a * per + k, (*chip, c))
                cp.start()
                copies.append(cp)
            wins.append(win)
        for a in range(n):
            for k, chip in enumerate(chips):
                landed = wins[a](2 * chip[0] + chip[1], c)
                _remote(landed, landed, send, recv, a * per + k, (x, y, 1 - c)).wait_recv()
                cp = _remote(landed, landed, send, recv, a * per + N_CHIPS - 1 + k, (x, y, 1 - c))
                cp.start()
                copies.append(cp)
        for a in range(n):
            for k, chip in enumerate(chips):
                theirs = wins[a](2 * chip[0] + chip[1], 1 - c)
                _remote(theirs, theirs, send, recv, a * per + N_CHIPS - 1 + k, (x, y, 1 - c)).wait_recv()
        for cp in copies:
            cp.wait_send()

    return _comm_call(body, "gather_weights", fulls, out_shape, n * per, 0, {a: a for a in range(n)})


def _gather_small(shards):
    n = len(shards)
    out_shape = [jax.ShapeDtypeStruct((N_CHIPS,) + s.shape, s.dtype) for s in shards]

    def body(*refs):
        srcs, outs, (send, recv, loc) = refs[:n], refs[n:2 * n], refs[2 * n:]
        x, y, c, chips = _place()
        me = 2 * x + y
        remote, local = [], []
        for a in range(n):
            local.append(pltpu.make_async_copy(srcs[a], outs[a].at[me], loc.at[a]))
            for k, chip in enumerate(chips):
                remote.append(_remote(srcs[a], outs[a].at[me], send, recv, a * (N_CHIPS - 1) + k, (*chip, c)))
        for cp in local + remote:
            cp.start()
        for cp in remote + local:
            cp.wait()

    return _comm_call(body, "gather_small", shards, out_shape, n * (N_CHIPS - 1), n)


def _exchange_halves(g4s):
    n = len(g4s)
    out_shape = [jax.ShapeDtypeStruct((g.shape[0],) + g.shape[2:], g.dtype) for g in g4s]

    def body(*refs):
        gs, las, (send, recv) = refs[:n], refs[n:2 * n], refs[2 * n:]
        x, y, c, _ = _place()
        cps = [_remote(gs[a].at[:, 1 - c], las[a], send, recv, a, (x, y, 1 - c)) for a in range(n)]
        for cp in cps:
            cp.start()
        for cp in cps:
            cp.wait()

    return _comm_call(body, "exchange_halves", g4s, out_shape, n, 0)


def _scatter_partials(ps, kinds):
    n = len(ps)
    out_shape = []
    for p, kind in zip(ps, kinds):
        L, H, C = (p.shape[0], p.shape[1], p.shape[2] // N_CHIPS) if kind == "col" else (p.shape[0], p.shape[2], p.shape[3])
        out_shape.append(jax.ShapeDtypeStruct((N_CHIPS - 1, L, H, C), p.dtype))

    def body(*refs):
        srcs, lbs, (send, recv) = refs[:n], refs[n:2 * n], refs[2 * n:]
        x, y, c, chips = _place()
        cps = []
        for a in range(n):
            C = lbs[a].shape[3]
            for k, chip in enumerate(chips):
                s = 2 * chip[0] + chip[1]
                src = srcs[a].at[:, :, pl.ds(pl.multiple_of(s * C, LANES), C)] if kinds[a] == "col" else srcs[a].at[:, s]
                cps.append(_remote(src, lbs[a].at[k], send, recv, a * (N_CHIPS - 1) + k, (*chip, c)))
        for cp in cps:
            cp.start()
        for cp in cps:
            cp.wait()

    return _comm_call(body, "scatter_partials", ps, out_shape, n * (N_CHIPS - 1), 0)


def _share_halves(g4s):
    n = len(g4s)
    out_shape = [jax.ShapeDtypeStruct(g.shape, g.dtype) for g in g4s]

    def body(*refs):
        outs, (send, recv) = refs[n:2 * n], refs[2 * n:]
        x, y, c, _ = _place()
        cps = [_remote(outs[a].at[:, c], outs[a].at[:, c], send, recv, a, (x, y, 1 - c)) for a in range(n)]
        for cp in cps:
            cp.start()
        for cp in cps:
            cp.wait()

    return _comm_call(body, "share_halves", g4s, out_shape, n, 0, {a: a for a in range(n)})


def _allreduce_small(part):
    N = part.shape[0]

    def body(p_ref, o_ref, buf, send, recv):
        x, y, c, _ = _place()
        me = 4 * x + 2 * y + c
        buf[me] = p_ref[...]
        cps = []
        for k in range(1, N_DEV):
            peer = (1 - x if k & 4 else x, 1 - y if k & 2 else y, 1 - c if k & 1 else c)
            cps.append(_remote(p_ref, buf.at[me], send, recv, k - 1, peer))
        for cp in cps:
            cp.start()
        for cp in cps:
            cp.wait()
        acc = buf[0]
        for i in range(1, N_DEV):
            acc = acc + buf[i]
        o_ref[...] = acc

    vmem = pl.BlockSpec(memory_space=pltpu.VMEM)
    return pl.pallas_call(
        body, name="allreduce_small", in_specs=[vmem], out_specs=vmem,
        out_shape=jax.ShapeDtypeStruct((N, LANES), F32),
        scratch_shapes=[pltpu.VMEM((N_DEV, N, LANES), F32), pltpu.SemaphoreType.DMA((N_DEV - 1,)),
                        pltpu.SemaphoreType.DMA((N_DEV - 1,))],
        compiler_params=pltpu.CompilerParams(has_side_effects=True, vmem_limit_bytes=VMEM_LIMIT_BYTES),
    )(part)


def _layer_fwd(x, p):
    h1, proj = _norm_matmul(x, p["norm1_g"], p["w_in"])
    attn, ltot = _attn_fwd(proj, p["q_norm_g"], p["k_norm_g"])
    c1 = _glu_conv_fwd(proj, p["conv_dw_w"], p["conv_dw_b"])
    c = _ln_silu_fwd(c1, p["conv_ln_g"], p["conv_ln_b"])
    x_mid = _matmul_res([attn, c], p["w_out"], x)
    h2, u = _norm_matmul(x_mid, p["norm2_g"], p["w_up"])
    act = _ffn_act_fwd(u, p["ffn_dw_w"], p["ffn_dw_b"])
    x_out = _matmul_res([act], p["w_down"], x_mid)
    return x_out, dict(x=x, h1=h1, proj=proj, attn=attn, ltot=ltot, c1=c1, c=c, x_mid=x_mid, h2=h2, u=u, act=act)


def _layer_bwd(dx_out, s, p):
    g = {}
    dact = _matmul_nt([dx_out], p["w_down"], BF16)
    g["w_down"] = _matmul_tn(s["act"], dx_out, tk=D_FF // 2, tn=D_MODEL)
    dgate, dval, g["ffn_dw_w"], g["ffn_dw_b"] = _ffn_act_bwd(s["u"], p["ffn_dw_w"], p["ffn_dw_b"], dact)
    dx_mid, g["norm2_g"] = _matmul_nt_rmsbwd([dgate, dval], p["w_up"], s["x_mid"], p["norm2_g"], dx_out)
    g["w_up"] = jnp.concatenate([_matmul_tn(s["h2"], d, tk=D_MODEL, tn=D_FF // 2) for d in (dgate, dval)], axis=1)
    dmix = _matmul_nt([dx_mid], p["w_out"], F32)
    g["w_out"] = jnp.concatenate([_matmul_tn(m, dx_mid, tk=ATTN_WIDTH, tn=D_MODEL) for m in (s["attn"], s["c"])], axis=0)
    dc1, g["conv_ln_g"], g["conv_ln_b"] = _ln_silu_bwd(s["c1"], p["conv_ln_g"], p["conv_ln_b"], dmix)
    da, db, g["conv_dw_w"], g["conv_dw_b"] = _glu_conv_bwd(s["proj"], p["conv_dw_w"], dc1)
    dq, dk, dv, g["q_norm_g"], g["k_norm_g"] = _attn_bwd(s["proj"], p["q_norm_g"], p["k_norm_g"], s["ltot"], dmix)
    pieces = [dq, dk, dv, da, db]
    dx, g["norm1_g"] = _matmul_nt_rmsbwd(pieces, p["w_in"], s["x"], p["norm1_g"], dx_mid)
    g["w_in"] = jnp.concatenate([_matmul_tn(s["h1"], d, tk=D_MODEL, tn=ATTN_WIDTH) for d in pieces], axis=1)
    return dx, g


WEIGHTS = ("norm1_g", "w_in", "q_norm_g", "k_norm_g", "conv_dw_w", "conv_dw_b", "conv_ln_g", "conv_ln_b",
           "w_out", "norm2_g", "w_up", "ffn_dw_w", "ffn_dw_b", "w_down")
BIG = ("w_in", "w_out", "w_up", "w_down")
BIG_KIND = {"w_in": "col", "w_out": "row", "w_up": "col", "w_down": "row"}
SMALL_SHARDED = ("conv_dw_w", "ffn_dw_w")
REPLICATED = tuple(n for n in WEIGHTS if n not in BIG + SMALL_SHARDED)


def _pack(arrays):
    flat = jnp.concatenate([a.reshape(-1) for a in arrays])
    rows = -(-flat.shape[0] // (SUBLANES * LANES)) * SUBLANES
    return jnp.pad(flat, (0, rows * LANES - flat.shape[0])).reshape(rows, LANES)


def _unpack(packed, shapes):
    flat = packed.reshape(-1)
    out, off = [], 0
    for shape in shapes:
        size = 1
        for d in shape:
            size *= d
        out.append(flat[off:off + size].reshape(shape))
        off += size
    return out


def _unshard_last(stacked):
    n, L, K, C = stacked.shape
    return jnp.transpose(stacked, (1, 2, 0, 3)).reshape(L, K, n * C)


def kernel(x, norm1_g, w_in, q_norm_g, k_norm_g, conv_dw_w, conv_dw_b, conv_ln_g, conv_ln_b, w_out, norm2_g, w_up, ffn_dw_w, ffn_dw_b, w_down, loss_target, m_norm1_g, m_w_in, m_q_norm_g, m_k_norm_g, m_conv_dw_w, m_conv_dw_b, m_conv_ln_g, m_conv_ln_b, m_w_out, m_norm2_g, m_w_up, m_ffn_dw_w, m_ffn_dw_b, m_w_down, v_norm1_g, v_w_in, v_q_norm_g, v_k_norm_g, v_conv_dw_w, v_conv_dw_b, v_conv_ln_g, v_conv_ln_b, v_w_out, v_norm2_g, v_w_up, v_ffn_dw_w, v_ffn_dw_b, v_w_down):
    given = dict(locals())
    w = {n: given[n] for n in WEIGHTS}
    m = {n: given["m_" + n] for n in WEIGHTS}
    v = {n: given["v_" + n] for n in WEIGHTS}
    chip = 2 * lax.axis_index("x") + lax.axis_index("y")
    core = lax.axis_index("c")
    chip_arr = jnp.reshape(chip, (1,)).astype(jnp.int32)
    core_arr = jnp.reshape(core, (1,)).astype(jnp.int32)
    L = DEPTH

    own = [_cast_into_full(w[n], BIG_KIND[n], chip_arr) for n in BIG]
    full = dict(zip(BIG, _gather_weights(own, [BIG_KIND[n] for n in BIG])))
    for n, stacked in zip(SMALL_SHARDED, _gather_small([w[n] for n in SMALL_SHARDED])):
        full[n] = _unshard_last(stacked)
    params = []
    for l in range(L):
        p = {n: full[n][l] for n in BIG + SMALL_SHARDED}
        p.update({n: w[n][l][None] for n in REPLICATED})
        params.append(p)

    act = x[0]
    saved = []
    for l in range(L):
        act, s = _layer_fwd(act, params[l])
        saved.append(s)
    dx, loss_part = _loss_grad(act, loss_target[0])
    loss = lax.psum(loss_part, ("x", "y", "c"))
    grads = [None] * L
    for l in reversed(range(L)):
        dx, grads[l] = _layer_bwd(dx, saved[l], params[l])

    g4s = []
    for n in BIG:
        g = jnp.stack([grads[l][n] for l in range(L)])
        if BIG_KIND[n] == "col":
            g4s.append(g.reshape(L, 2, g.shape[1] // 2, g.shape[2]))
        else:
            rows = w[n].shape[1]
            g4s.append(g.reshape(L * N_CHIPS, 2, rows // 2, g.shape[2]))
    received = _exchange_halves(g4s)
    partial = [_add_half(g4, la, core_arr) for g4, la in zip(g4s, received)]
    partial = [p if BIG_KIND[n] == "col" else p.reshape(L, N_CHIPS, p.shape[1], p.shape[2]) for n, p in zip(BIG, partial)]
    landed = _scatter_partials(partial, [BIG_KIND[n] for n in BIG])
    place = jnp.concatenate([chip_arr, core_arr])
    halves = [_add_parts(p, lb, place, BIG_KIND[n]) for n, p, lb in zip(BIG, partial, landed)]
    grad = {n: g.reshape(w[n].shape) for n, g in zip(BIG, _share_halves(halves))}

    small = REPLICATED + SMALL_SHARDED
    small_full = [jnp.stack([grads[l][n] for l in range(L)]) for n in small]
    summed = _unpack(_allreduce_small(_pack(small_full)), [a.shape for a in small_full])
    for n, g in zip(small, summed):
        if n in REPLICATED:
            grad[n] = g.reshape(w[n].shape)
        else:
            width = w[n].shape[2]
            grad[n] = lax.dynamic_slice_in_dim(g, chip * width, width, axis=2)

    delta, new_m, new_v = {}, {}, {}
    for n in BIG:
        two_d = lambda a: a.reshape(-1, a.shape[2])
        d, nm, nv = _adamw(two_d(w[n]), two_d(grad[n]), two_d(m[n]), two_d(v[n]))
        delta[n], new_m[n], new_v[n] = (a.reshape(w[n].shape) for a in (d, nm, nv))
    shapes = [w[n].shape for n in small]
    packed = _adamw(*[_pack([src[n] for n in small]) for src in (w, grad, m, v)])
    for out, pk in zip((delta, new_m, new_v), packed):
        out.update(zip(small, _unpack(pk, shapes)))

    return (loss, dx[None], *[grad[n] for n in WEIGHTS], *[delta[n] for n in WEIGHTS],
            *[new_m[n] for n in WEIGHTS], *[new_v[n] for n in WEIGHTS])
```

```python
import functools

import jax
import jax.numpy as jnp
from jax import lax
from jax.experimental import pallas as pl
from jax.experimental.pallas import tpu as pltpu

F32 = jnp.float32
BF16 = jnp.bfloat16

DEPTH = 4
D_MODEL = 1024
HEADS = 8
HEAD_DIM = 64
ATTN_WIDTH = HEADS * HEAD_DIM
CONV_WIDTH = D_MODEL - ATTN_WIDTH
CONV_KERNEL = 31
D_FF = 2816
FFN_KERNEL = 3
EPS = 1e-6
ADAM_LR, ADAM_B1, ADAM_B2, ADAM_EPS, ADAM_WD, ADAM_STEP = 0.001, 0.9, 0.999, 1e-08, 0.01, 10

N_CHIPS = 4
N_DEV = 8
LANES = 128
SUBLANES = 8
VMEM_LIMIT_BYTES = 56 * 2**20
ATTN_TILE = 256
CONV_PAD = 32
FFN_PAD = 8
MESH = pl.DeviceIdType.MESH


def _params(*sem):
    return pltpu.CompilerParams(dimension_semantics=sem if sem else None, vmem_limit_bytes=VMEM_LIMIT_BYTES)


def _dot(a, b):
    return jnp.dot(a, b, preferred_element_type=F32)


def _dot_nt(a, b):
    return lax.dot_general(a, b, (((1,), (1,)), ((), ())), preferred_element_type=F32)


def _dot_tn(a, b):
    return lax.dot_general(a, b, (((0,), (0,)), ((), ())), preferred_element_type=F32)


def _sigmoid(x):
    return 1.0 / (1.0 + jnp.exp(-x))


def _norm_matmul(x, g, w, *, tm=512, n_split=2):
    S, D = x.shape
    N = w.shape[1]
    tn = N // n_split

    def body(x_ref, g_ref, w_ref, h_ref, y_ref):
        @pl.when(pl.program_id(1) == 0)
        def _():
            xv = x_ref[...]
            r = lax.rsqrt(jnp.mean(xv * xv, axis=-1, keepdims=True) + EPS)
            h_ref[...] = (xv * r * g_ref[...]).astype(BF16)

        y_ref[...] = _dot(h_ref[...], w_ref[...])

    return pl.pallas_call(
        body, name="norm_matmul", grid=(S // tm, n_split),
        in_specs=[pl.BlockSpec((tm, D), lambda i, j: (i, 0)),
                  pl.BlockSpec((1, D), lambda i, j: (0, 0)),
                  pl.BlockSpec((D, tn), lambda i, j: (0, j))],
        out_specs=[pl.BlockSpec((tm, D), lambda i, j: (i, 0)),
                   pl.BlockSpec((tm, tn), lambda i, j: (i, j))],
        out_shape=[jax.ShapeDtypeStruct((S, D), BF16), jax.ShapeDtypeStruct((S, N), F32)],
        compiler_params=_params("parallel", "arbitrary"),
    )(x, g, w)


def _matmul_res(pieces, w, res, *, tm=512):
    S, N = res.shape
    K = w.shape[0]
    widths = [p.shape[1] for p in pieces]
    assert sum(widths) == K

    def body(*refs):
        p_refs, (w_ref, res_ref, o_ref) = refs[:len(pieces)], refs[len(pieces):]
        acc = res_ref[...]
        off = 0
        for p_ref, kp in zip(p_refs, widths):
            acc = acc + _dot(p_ref[...], w_ref[off:off + kp, :])
            off += kp
        o_ref[...] = acc

    return pl.pallas_call(
        body, name="matmul_res", grid=(S // tm,),
        in_specs=[pl.BlockSpec((tm, kp), lambda i: (i, 0)) for kp in widths]
        + [pl.BlockSpec((K, N), lambda i: (0, 0)), pl.BlockSpec((tm, N), lambda i: (i, 0))],
        out_specs=pl.BlockSpec((tm, N), lambda i: (i, 0)),
        out_shape=jax.ShapeDtypeStruct((S, N), F32),
        compiler_params=_params("parallel"),
    )(*pieces, w, res)


def _nt_sum(p_refs, widths, w_ref):
    acc = None
    off = 0
    for p_ref, n_p in zip(p_refs, widths):
        d = _dot_nt(p_ref[...].astype(BF16), w_ref[:, off:off + n_p])
        acc = d if acc is None else acc + d
        off += n_p
    return acc


def _matmul_nt(pieces, w, out_dtype, *, tm=512):
    S = pieces[0].shape[0]
    K, N = w.shape
    widths = [p.shape[1] for p in pieces]
    assert sum(widths) == N

    def body(*refs):
        p_refs, (w_ref, o_ref) = refs[:len(pieces)], refs[len(pieces):]
        o_ref[...] = _nt_sum(p_refs, widths, w_ref).astype(out_dtype)

    return pl.pallas_call(
        body, name="matmul_nt", grid=(S // tm,),
        in_specs=[pl.BlockSpec((tm, n_p), lambda i: (i, 0)) for n_p in widths]
        + [pl.BlockSpec((K, N), lambda i: (0, 0))],
        out_specs=pl.BlockSpec((tm, K), lambda i: (i, 0)),
        out_shape=jax.ShapeDtypeStruct((S, K), out_dtype),
        compiler_params=_params("parallel"),
    )(*pieces, w)


def _matmul_nt_rmsbwd(pieces, w, x, g, dres, *, tm=256):
    S, K = x.shape
    N = w.shape[1]
    widths = [p.shape[1] for p in pieces]
    assert sum(widths) == N

    def body(*refs):
        p_refs, (w_ref, x_ref, g_ref, dres_ref, dx_ref, dg_ref) = refs[:len(pieces)], refs[len(pieces):]
        dh = _nt_sum(p_refs, widths, w_ref)
        xv = x_ref[...]
        r = lax.rsqrt(jnp.mean(xv * xv, axis=-1, keepdims=True) + EPS)
        xh = xv * r
        dxh = dh * g_ref[...]
        dx_ref[...] = dres_ref[...] + r * (dxh - xh * jnp.mean(dxh * xh, axis=-1, keepdims=True))

        @pl.when(pl.program_id(0) == 0)
        def _():
            dg_ref[...] = jnp.zeros_like(dg_ref)

        dg_ref[...] += jnp.sum(dh * xh, axis=0, keepdims=True)

    return pl.pallas_call(
        body, name="matmul_nt_rmsbwd", grid=(S // tm,),
        in_specs=[pl.BlockSpec((tm, n_p), lambda i: (i, 0)) for n_p in widths]
        + [pl.BlockSpec((K, N), lambda i: (0, 0)), pl.BlockSpec((tm, K), lambda i: (i, 0)),
           pl.BlockSpec((1, K), lambda i: (0, 0)), pl.BlockSpec((tm, K), lambda i: (i, 0))],
        out_specs=[pl.BlockSpec((tm, K), lambda i: (i, 0)), pl.BlockSpec((1, K), lambda i: (0, 0))],
        out_shape=[jax.ShapeDtypeStruct((S, K), F32), jax.ShapeDtypeStruct((1, K), F32)],
        compiler_params=_params("arbitrary"),
    )(*pieces, w, x, g, dres)


def _matmul_tn(x, dy, *, tk, tn, ts=512):
    S, M = x.shape
    N = dy.shape[1]
    n_s = S // ts

    def body(x_ref, dy_ref, o_ref, acc_ref):
        s = pl.program_id(2)

        @pl.when(s == 0)
        def _():
            acc_ref[...] = jnp.zeros_like(acc_ref)

        acc_ref[...] += _dot_tn(x_ref[...].astype(BF16), dy_ref[...].astype(BF16))

        @pl.when(s == n_s - 1)
        def _():
            o_ref[...] = acc_ref[...].astype(BF16)

    return pl.pallas_call(
        body, name="matmul_tn", grid=(M // tk, N // tn, n_s),
        in_specs=[pl.BlockSpec((ts, tk), lambda i, j, s: (s, i)),
                  pl.BlockSpec((ts, tn), lambda i, j, s: (s, j))],
        out_specs=pl.BlockSpec((tk, tn), lambda i, j, s: (i, j)),
        out_shape=jax.ShapeDtypeStruct((M, N), BF16),
        scratch_shapes=[pltpu.VMEM((tk, tn), F32)],
        compiler_params=_params("parallel", "parallel", "arbitrary"),
    )(x, dy)


def _tri_consts():
    j = jnp.arange(ATTN_TILE)[:, None]
    s = jnp.arange(ATTN_TILE)[None, :]

    def two(m):
        return jnp.concatenate([m, m], axis=0).astype(BF16)

    return two(j > s), two(j <= s), two(j < s)


def _split_hi_lo(a):
    hi = a.astype(BF16)
    lo = (a - hi.astype(F32)).astype(BF16)
    return jnp.concatenate([hi, lo], axis=1)


def _log_terms(s):
    lom = -(jnp.maximum(s, 0.0) + jnp.log(1.0 + jnp.exp(-jnp.abs(s))))
    return lom, lom + s


def _causal_mask():
    t = lax.broadcasted_iota(jnp.int32, (ATTN_TILE, ATTN_TILE), 0)
    s = lax.broadcasted_iota(jnp.int32, (ATTN_TILE, ATTN_TILE), 1)
    return s < t


def _attn_prep(h, q_ref, k_ref, v_ref, qg_ref, kg_ref, qn_s, kn_s, vb_s, n_tiles):
    T = ATTN_TILE
    lanes = slice(HEAD_DIM * h, HEAD_DIM * (h + 1))
    scale = HEAD_DIM ** -0.5

    def prep(i, carry):
        rows = pl.ds(pl.multiple_of(i * T, T), T)
        q = q_ref[rows, lanes]
        k = k_ref[rows, lanes]
        rq = lax.rsqrt(jnp.mean(q * q, axis=-1, keepdims=True) + EPS)
        rk = lax.rsqrt(jnp.mean(k * k, axis=-1, keepdims=True) + EPS)
        qn_s[h, rows, :] = (q * rq * qg_ref[...] * scale).astype(BF16)
        kn_s[h, rows, :] = (k * rk * kg_ref[...]).astype(BF16)
        vb_s[h, rows, :] = v_ref[rows, lanes].astype(BF16)
        return carry

    lax.fori_loop(0, n_tiles, prep, 0)


def _attn_fwd(proj, qg, kg, fulls=(), kinds=()):
    S = proj.shape[0]
    n_comm = len(fulls)
    T = ATTN_TILE
    n_tiles = S // T
    suffix, _, _ = _tri_consts()
    pairs = HEADS // 2
    q_blk, k_blk, v_blk = 0, ATTN_WIDTH // LANES, 2 * ATTN_WIDTH // LANES

    def body(*refs):
        q_ref, k_ref, v_ref, qg_ref, kg_ref, tri_ref = refs[:6]
        o_ref, lt_ref = refs[6 + n_comm:8 + n_comm]
        w_refs = refs[8 + n_comm:8 + 2 * n_comm]
        qn_s, kn_s, vb_s = refs[8 + 2 * n_comm:11 + 2 * n_comm]
        sems = refs[11 + 2 * n_comm:]
        if n_comm:
            @pl.when(pl.program_id(0) == 0)
            def _():
                for cp in _gather_copies(w_refs, kinds, *sems, 0):
                    cp.start()

        heads = range(2)
        for h in heads:
            _attn_prep(h, q_ref, k_ref, v_ref, qg_ref, kg_ref, qn_s, kn_s, vb_s, n_tiles)

        def q_tile(qi, carry0):
            qrows = pl.ds(pl.multiple_of(qi * T, T), T)
            qt = [qn_s[h, qrows, :] for h in heads]

            def tile(kj, carry, diag):
                krows = pl.ds(pl.multiple_of(kj * T, T), T)
                s = [_dot_nt(qt[h], kn_s[h, krows, :]) for h in heads]
                terms = [_log_terms(s[h]) for h in heads]
                lom = [terms[h][0] for h in heads]
                if diag:
                    mask = _causal_mask()
                    lom = [jnp.where(mask, lom[h], 0.0) for h in heads]
                tail = [_dot(_split_hi_lo(lom[h]), tri_ref[...]) for h in heads]
                w = [jnp.exp(terms[h][1] + tail[h] + carry[h][1]) for h in heads]
                if diag:
                    w = [jnp.where(mask, w[h], 0.0) for h in heads]
                return tuple((carry[h][0] + _dot(w[h].astype(BF16), vb_s[h, krows, :]),
                              carry[h][1] + (tail[h][:, 0:1] + lom[h][:, 0:1])) for h in heads)

            init = tuple((jnp.zeros((T, HEAD_DIM), F32), jnp.zeros((T, 1), F32)) for h in heads)
            carry = lax.fori_loop(0, qi, lambda t, cr: tile(qi - 1 - t, cr, False), tile(qi, init, True))
            for h in heads:
                lanes = slice(HEAD_DIM * h, HEAD_DIM * (h + 1))
                o_ref[qrows, lanes] = carry[h][0].astype(BF16)
                lt_ref[qrows, lanes] = jnp.broadcast_to(carry[h][1], (T, HEAD_DIM))
            return carry0

        lax.fori_loop(0, n_tiles, q_tile, 0)
        if n_comm:
            @pl.when(pl.program_id(0) == pairs - 1)
            def _():
                for cp in _gather_copies(w_refs, kinds, *sems, 0):
                    cp.wait()

    n_sem = n_comm * (N_CHIPS - 1)
    out = pl.pallas_call(
        body, name="attn_fwd_gather" if n_comm else "attn_fwd", grid=(pairs,),
        in_specs=[pl.BlockSpec((S, LANES), lambda p: (0, q_blk + p)),
                  pl.BlockSpec((S, LANES), lambda p: (0, k_blk + p)),
                  pl.BlockSpec((S, LANES), lambda p: (0, v_blk + p)),
                  pl.BlockSpec((1, HEAD_DIM), lambda p: (0, 0)),
                  pl.BlockSpec((1, HEAD_DIM), lambda p: (0, 0)),
                  pl.BlockSpec((2 * T, T), lambda p: (0, 0))] + [ANY] * n_comm,
        out_specs=[pl.BlockSpec((S, LANES), lambda p: (0, p)),
                   pl.BlockSpec((None, S, LANES), lambda p: (p, 0, 0))] + [ANY] * n_comm,
        out_shape=[jax.ShapeDtypeStruct((S, ATTN_WIDTH), BF16),
                   jax.ShapeDtypeStruct((pairs, S, LANES), F32)] + [jax.ShapeDtypeStruct(f.shape, f.dtype) for f in fulls],
        scratch_shapes=[pltpu.VMEM((2, S, HEAD_DIM), BF16)] * 3 + ([pltpu.SemaphoreType.DMA((n_sem,))] * 2 if n_comm else []),
        input_output_aliases={6 + a: 2 + a for a in range(n_comm)},
        compiler_params=pltpu.CompilerParams(dimension_semantics=("arbitrary",), vmem_limit_bytes=VMEM_LIMIT_BYTES,
                                             has_side_effects=bool(n_comm)),
    )(proj, proj, proj, qg, kg, suffix, *fulls)
    return out[0], out[1], list(out[2:])


def _attn_bwd(proj, qg, kg, ltot, dmix, parts=(), kinds=()):
    S = proj.shape[0]
    n_comm = len(parts)
    T = ATTN_TILE
    n_tiles = S // T
    _, prefix_incl, prefix_excl = _tri_consts()
    pairs = HEADS // 2
    q_blk, k_blk, v_blk = 0, ATTN_WIDTH // LANES, 2 * ATTN_WIDTH // LANES
    scale = HEAD_DIM ** -0.5

    def body(*refs):
        q_ref, k_ref, v_ref, qg_ref, kg_ref, lt_ref, do_ref, ti_ref, te_ref = refs[:9]
        p_refs = refs[9:9 + n_comm]
        dq_ref, dk_ref, dv_ref, dqg_ref, dkg_ref = refs[9 + n_comm:14 + n_comm]
        lb_refs = refs[14 + n_comm:14 + 2 * n_comm]
        qn_s, kn_s, vb_s, dq_s, dk_s, dv_s = refs[14 + 2 * n_comm:20 + 2 * n_comm]
        sems = refs[20 + 2 * n_comm:]

        @pl.when(pl.program_id(0) == 0)
        def _():
            dqg_ref[...] = jnp.zeros_like(dqg_ref)
            dkg_ref[...] = jnp.zeros_like(dkg_ref)
            for cp in _scatter_copies(p_refs, lb_refs, kinds, *sems) if n_comm else ():
                cp.start()

        heads = range(2)
        for h in heads:
            _attn_prep(h, q_ref, k_ref, v_ref, qg_ref, kg_ref, qn_s, kn_s, vb_s, n_tiles)
        dk_s[...] = jnp.zeros_like(dk_s)
        dv_s[...] = jnp.zeros_like(dv_s)

        def q_tile(qi, carry0):
            qrows = pl.ds(pl.multiple_of(qi * T, T), T)
            qt = [qn_s[h, qrows, :] for h in heads]
            dob = [do_ref[qrows, HEAD_DIM * h:HEAD_DIM * (h + 1)].astype(BF16) for h in heads]
            lt = [lt_ref[qrows, HEAD_DIM * h:HEAD_DIM * h + 1] for h in heads]

            def tile(kj, carry, diag):
                krows = pl.ds(pl.multiple_of(kj * T, T), T)
                kt = [kn_s[h, krows, :] for h in heads]
                s = [_dot_nt(qt[h], kt[h]) for h in heads]
                dw = [_dot_nt(dob[h], vb_s[h, krows, :]) for h in heads]
                terms = [_log_terms(s[h]) for h in heads]
                lom = [terms[h][0] for h in heads]
                if diag:
                    mask = _causal_mask()
                    lom = [jnp.where(mask, lom[h], 0.0) for h in heads]
                pin = [_dot(_split_hi_lo(lom[h]), ti_ref[...]) for h in heads]
                w = [jnp.exp(terms[h][1] + ((lt[h] - carry[h][1]) - pin[h])) for h in heads]
                if diag:
                    w = [jnp.where(mask, w[h], 0.0) for h in heads]
                e = [w[h] * dw[h] for h in heads]
                gex = [_dot(_split_hi_lo(e[h]), te_ref[...]) for h in heads]
                dz = [e[h] - jnp.exp(terms[h][1]) * (e[h] + (carry[h][2] + gex[h])) for h in heads]
                if diag:
                    dz = [jnp.where(mask, dz[h], 0.0) for h in heads]
                new = []
                for h in heads:
                    dzb = dz[h].astype(BF16)
                    dk_s[h, krows, :] += _dot_tn(dzb, qt[h])
                    dv_s[h, krows, :] += _dot_tn(w[h].astype(BF16), dob[h])
                    new.append((carry[h][0] + _dot(dzb, kt[h]),
                                carry[h][1] + pin[h][:, T - 1:T],
                                carry[h][2] + gex[h][:, T - 1:T] + e[h][:, T - 1:T]))
                return tuple(new)

            zero = jnp.zeros((T, 1), F32)
            init = tuple((jnp.zeros((T, HEAD_DIM), F32), zero, zero) for h in heads)
            last = tile(qi, lax.fori_loop(0, qi, lambda kj, cr: tile(kj, cr, False), init), True)
            for h in heads:
                dq_s[h, qrows, :] = last[h][0]
            return carry0

        lax.fori_loop(0, n_tiles, q_tile, 0)

        def finish(i, carry):
            rows = pl.ds(pl.multiple_of(i * T, T), T)
            new = []
            for h in heads:
                lanes = slice(HEAD_DIM * h, HEAD_DIM * (h + 1))
                q = q_ref[rows, lanes]
                k = k_ref[rows, lanes]
                rq = lax.rsqrt(jnp.mean(q * q, axis=-1, keepdims=True) + EPS)
                rk = lax.rsqrt(jnp.mean(k * k, axis=-1, keepdims=True) + EPS)
                qh = q * rq
                kh = k * rk
                dqn = dq_s[h, rows, :] * scale
                dkn = dk_s[h, rows, :]
                dqh = dqn * qg_ref[...]
                dkh = dkn * kg_ref[...]
                dq_ref[rows, lanes] = (rq * (dqh - qh * jnp.mean(dqh * qh, axis=-1, keepdims=True))).astype(BF16)
                dk_ref[rows, lanes] = (rk * (dkh - kh * jnp.mean(dkh * kh, axis=-1, keepdims=True))).astype(BF16)
                dv_ref[rows, lanes] = dv_s[h, rows, :].astype(BF16)
                new.append(carry[2 * h] + jnp.sum(dqn * qh, axis=0, keepdims=True))
                new.append(carry[2 * h + 1] + jnp.sum(dkn * kh, axis=0, keepdims=True))
            return tuple(new)

        zero = jnp.zeros((1, HEAD_DIM), F32)
        sums = lax.fori_loop(0, n_tiles, finish, (zero,) * 4)
        dqg_ref[...] += sums[0] + sums[2]
        dkg_ref[...] += sums[1] + sums[3]
        if n_comm:
            @pl.when(pl.program_id(0) == pairs - 1)
            def _():
                for cp in _scatter_copies(p_refs, lb_refs, kinds, *sems):
                    cp.wait()

    blk = lambda off: pl.BlockSpec((S, LANES), lambda p: (0, off + p))
    row64 = pl.BlockSpec((1, HEAD_DIM), lambda p: (0, 0))
    tri = pl.BlockSpec((2 * T, T), lambda p: (0, 0))
    n_sem = n_comm * (N_CHIPS - 1)
    out = pl.pallas_call(
        body, name="attn_bwd_scatter" if n_comm else "attn_bwd", grid=(pairs,),
        in_specs=[blk(q_blk), blk(k_blk), blk(v_blk), row64, row64,
                  pl.BlockSpec((None, S, LANES), lambda p: (p, 0, 0)), blk(0), tri, tri] + [ANY] * n_comm,
        out_specs=[blk(0), blk(0), blk(0), row64, row64] + [ANY] * n_comm,
        out_shape=[jax.ShapeDtypeStruct((S, ATTN_WIDTH), BF16)] * 3 + [jax.ShapeDtypeStruct((1, HEAD_DIM), F32)] * 2
        + _scatter_shapes(parts, kinds),
        scratch_shapes=[pltpu.VMEM((2, S, HEAD_DIM), BF16)] * 3 + [pltpu.VMEM((2, S, HEAD_DIM), F32)] * 3
        + ([pltpu.SemaphoreType.DMA((n_sem,))] * 2 if n_comm else []),
        compiler_params=pltpu.CompilerParams(dimension_semantics=("arbitrary",), vmem_limit_bytes=VMEM_LIMIT_BYTES,
                                             has_side_effects=bool(n_comm)),
    )(proj, proj, proj, qg, kg, ltot, dmix, prefix_incl, prefix_excl, *parts)
    return out[:5], list(out[5:])


def _shifted(win, n_rows):
    return [win if b == 0 else pltpu.roll(win, n_rows - b, 0) for b in range(SUBLANES)]


def _taps(variants, offsets, tm):
    return {o: variants[o % SUBLANES][(o // SUBLANES) * SUBLANES:(o // SUBLANES) * SUBLANES + tm, :] for o in offsets}


def _fold_rows(a):
    return jnp.sum(a.reshape(a.shape[0] // SUBLANES, SUBLANES, a.shape[1]), axis=0)


def _glu_conv_fwd(proj, w, bias, *, tm=256):
    S = proj.shape[0]
    CB = LANES
    a_blk, b_blk = 3 * ATTN_WIDTH // CB, (3 * ATTN_WIDTH + CONV_WIDTH) // CB
    n_rows = tm + CONV_PAD

    def body(a_ref, b_ref, w_ref, bias_ref, c1_ref, pad_s):
        pad_s[0:CONV_PAD, :] = jnp.zeros((CONV_PAD, CB), F32)

        def fill(i, carry):
            rows = pl.ds(pl.multiple_of(i * tm, tm), tm)
            pad_s[pl.ds(pl.multiple_of(CONV_PAD + i * tm, SUBLANES), tm), :] = a_ref[rows, :] * _sigmoid(b_ref[rows, :])
            return carry

        lax.fori_loop(0, S // tm, fill, 0)

        def conv(i, carry):
            r0 = pl.multiple_of(i * tm, tm)
            taps = _taps(_shifted(pad_s[pl.ds(r0, n_rows), :], n_rows), range(2, 2 + CONV_KERNEL), tm)
            acc = jnp.broadcast_to(bias_ref[...], (tm, CB))
            for k in range(CONV_KERNEL):
                acc = acc + w_ref[k:k + 1, :] * taps[k + 2]
            c1_ref[pl.ds(r0, tm), :] = acc
            return carry

        lax.fori_loop(0, S // tm, conv, 0)

    return pl.pallas_call(
        body, name="glu_conv_fwd", grid=(CONV_WIDTH // CB,),
        in_specs=[pl.BlockSpec((S, CB), lambda j: (0, a_blk + j)), pl.BlockSpec((S, CB), lambda j: (0, b_blk + j)),
                  pl.BlockSpec((CONV_KERNEL, CB), lambda j: (0, j)), pl.BlockSpec((1, CB), lambda j: (0, j))],
        out_specs=pl.BlockSpec((S, CB), lambda j: (0, j)),
        out_shape=jax.ShapeDtypeStruct((S, CONV_WIDTH), F32),
        scratch_shapes=[pltpu.VMEM((S + CONV_PAD, CB), F32)],
        compiler_params=_params("parallel"),
    )(proj, proj, w, bias)


def _glu_conv_bwd(proj, w, dc1, *, tm=256):
    S = proj.shape[0]
    CB = LANES
    a_blk, b_blk = 3 * ATTN_WIDTH // CB, (3 * ATTN_WIDTH + CONV_WIDTH) // CB
    n_rows = tm + CONV_PAD

    def body(a_ref, b_ref, w_ref, dc1_ref, da_ref, db_ref, dw_ref, dbias_ref, pad_s, dpad_s, dw_s):
        pad_s[0:CONV_PAD, :] = jnp.zeros((CONV_PAD, CB), F32)
        dpad_s[S:S + CONV_PAD, :] = jnp.zeros((CONV_PAD, CB), F32)
        dw_s[...] = jnp.zeros_like(dw_s)

        def fill(i, carry):
            rows = pl.ds(pl.multiple_of(i * tm, tm), tm)
            pad_s[pl.ds(pl.multiple_of(CONV_PAD + i * tm, SUBLANES), tm), :] = a_ref[rows, :] * _sigmoid(b_ref[rows, :])
            dpad_s[rows, :] = dc1_ref[rows, :]
            return carry

        lax.fori_loop(0, S // tm, fill, 0)

        def conv(i, carry):
            r0 = pl.multiple_of(i * tm, tm)
            rows = pl.ds(r0, tm)
            taps = _taps(_shifted(dpad_s[pl.ds(r0, n_rows), :], n_rows), range(CONV_KERNEL), tm)
            acc = jnp.zeros((tm, CB), F32)
            for k in range(CONV_KERNEL):
                acc = acc + w_ref[k:k + 1, :] * taps[CONV_KERNEL - 1 - k]
            a = a_ref[rows, :]
            sg = _sigmoid(b_ref[rows, :])
            da_ref[rows, :] = (acc * sg).astype(BF16)
            db_ref[rows, :] = (acc * a * sg * (1.0 - sg)).astype(BF16)
            d = taps[0]
            taps = _taps(_shifted(pad_s[pl.ds(r0, n_rows), :], n_rows), range(2, 2 + CONV_KERNEL), tm)
            for k in range(CONV_KERNEL):
                dw_s[SUBLANES * k:SUBLANES * (k + 1), :] += _fold_rows(d * taps[k + 2])
            dw_s[SUBLANES * CONV_KERNEL:SUBLANES * (CONV_KERNEL + 1), :] += _fold_rows(d)
            return carry

        lax.fori_loop(0, S // tm, conv, 0)
        for k in range(CONV_KERNEL):
            dw_ref[k:k + 1, :] = jnp.sum(dw_s[SUBLANES * k:SUBLANES * (k + 1), :], axis=0, keepdims=True)
        dbias_ref[...] = jnp.sum(dw_s[SUBLANES * CONV_KERNEL:SUBLANES * (CONV_KERNEL + 1), :], axis=0, keepdims=True)

    col = lambda off: pl.BlockSpec((S, CB), lambda j: (0, off + j))
    return pl.pallas_call(
        body, name="glu_conv_bwd", grid=(CONV_WIDTH // CB,),
        in_specs=[col(a_blk), col(b_blk), pl.BlockSpec((CONV_KERNEL, CB), lambda j: (0, j)), col(0)],
        out_specs=[col(0), col(0), pl.BlockSpec((CONV_KERNEL, CB), lambda j: (0, j)), pl.BlockSpec((1, CB), lambda j: (0, j))],
        out_shape=[jax.ShapeDtypeStruct((S, CONV_WIDTH), BF16)] * 2
        + [jax.ShapeDtypeStruct((CONV_KERNEL, CONV_WIDTH), F32), jax.ShapeDtypeStruct((1, CONV_WIDTH), F32)],
        scratch_shapes=[pltpu.VMEM((S + CONV_PAD, CB), F32), pltpu.VMEM((S + CONV_PAD, CB), F32),
                        pltpu.VMEM((SUBLANES * (CONV_KERNEL + 1), CB), F32)],
        compiler_params=_params("parallel"),
    )(proj, proj, w, dc1)


def _ln_stats(c1):
    mu = jnp.mean(c1, axis=-1, keepdims=True)
    xc = c1 - mu
    r = lax.rsqrt(jnp.mean(xc * xc, axis=-1, keepdims=True) + EPS)
    return xc * r, r


def _ln_silu_fwd(c1, g, b, *, tm=512):
    S, C = c1.shape

    def body(c1_ref, g_ref, b_ref, c_ref):
        yh, _ = _ln_stats(c1_ref[...])
        y = yh * g_ref[...] + b_ref[...]
        c_ref[...] = (y * _sigmoid(y)).astype(BF16)

    vec = pl.BlockSpec((1, C), lambda i: (0, 0))
    return pl.pallas_call(
        body, name="ln_silu_fwd", grid=(S // tm,),
        in_specs=[pl.BlockSpec((tm, C), lambda i: (i, 0)), vec, vec],
        out_specs=pl.BlockSpec((tm, C), lambda i: (i, 0)),
        out_shape=jax.ShapeDtypeStruct((S, C), BF16),
        compiler_params=_params("parallel"),
    )(c1, g, b)


def _ln_silu_bwd(c1, g, b, dmix, *, tm=512):
    S, C = c1.shape

    def body(c1_ref, g_ref, b_ref, dc_ref, dc1_ref, dg_ref, db_ref):
        yh, r = _ln_stats(c1_ref[...])
        y = yh * g_ref[...] + b_ref[...]
        sg = _sigmoid(y)
        dy = dc_ref[...] * (sg * (1.0 + y * (1.0 - sg)))
        dyh = dy * g_ref[...]
        dc1_ref[...] = r * (dyh - jnp.mean(dyh, axis=-1, keepdims=True)
                            - yh * jnp.mean(dyh * yh, axis=-1, keepdims=True))

        @pl.when(pl.program_id(0) == 0)
        def _():
            dg_ref[...] = jnp.zeros_like(dg_ref)
            db_ref[...] = jnp.zeros_like(db_ref)

        dg_ref[...] += jnp.sum(dy * yh, axis=0, keepdims=True)
        db_ref[...] += jnp.sum(dy, axis=0, keepdims=True)

    vec = pl.BlockSpec((1, C), lambda i: (0, 0))
    return pl.pallas_call(
        body, name="ln_silu_bwd", grid=(S // tm,),
        in_specs=[pl.BlockSpec((tm, C), lambda i: (i, 0)), vec, vec, pl.BlockSpec((tm, C), lambda i: (i, 1))],
        out_specs=[pl.BlockSpec((tm, C), lambda i: (i, 0)), vec, vec],
        out_shape=[jax.ShapeDtypeStruct((S, C), F32), jax.ShapeDtypeStruct((1, C), F32), jax.ShapeDtypeStruct((1, C), F32)],
        compiler_params=_params("arbitrary"),
    )(c1, g, b, dmix)


FFN_CB = 256


def _ffn_gate(pad_s, w_ref, bias_ref, r0, tm):
    n_rows = tm + FFN_PAD
    taps = _taps(_shifted(pad_s[pl.ds(r0, n_rows), :], n_rows), range(FFN_PAD - 2, FFN_PAD + 1), tm)
    g1 = bias_ref[...] + w_ref[0:1, :] * taps[6] + w_ref[1:2, :] * taps[7] + w_ref[2:3, :] * taps[8]
    return g1, taps


def _ffn_act_fwd(u, w, bias, *, tm=256):
    S = u.shape[0]
    CB = FFN_CB
    nb = D_FF // CB

    def body(g_ref, v_ref, w_ref, bias_ref, o_ref, pad_s):
        pad_s[0:FFN_PAD, :] = jnp.zeros((FFN_PAD, CB), F32)

        def fill(i, carry):
            pad_s[pl.ds(pl.multiple_of(FFN_PAD + i * tm, SUBLANES), tm), :] = g_ref[pl.ds(pl.multiple_of(i * tm, tm), tm), :]
            return carry

        lax.fori_loop(0, S // tm, fill, 0)

        def act(i, carry):
            r0 = pl.multiple_of(i * tm, tm)
            g1, _ = _ffn_gate(pad_s, w_ref, bias_ref, r0, tm)
            o_ref[pl.ds(r0, tm), :] = (g1 * _sigmoid(g1) * v_ref[pl.ds(r0, tm), :]).astype(BF16)
            return carry

        lax.fori_loop(0, S // tm, act, 0)

    return pl.pallas_call(
        body, name="ffn_act_fwd", grid=(nb,),
        in_specs=[pl.BlockSpec((S, CB), lambda j: (0, j)), pl.BlockSpec((S, CB), lambda j: (0, nb + j)),
                  pl.BlockSpec((FFN_KERNEL, CB), lambda j: (0, j)), pl.BlockSpec((1, CB), lambda j: (0, j))],
        out_specs=pl.BlockSpec((S, CB), lambda j: (0, j)),
        out_shape=jax.ShapeDtypeStruct((S, D_FF), BF16),
        scratch_shapes=[pltpu.VMEM((S + FFN_PAD, CB), F32)],
        compiler_params=_params("parallel"),
    )(u, u, w, bias)


def _ffn_act_bwd(u, w, bias, dact, *, tm=256):
    S = u.shape[0]
    CB = FFN_CB
    nb = D_FF // CB

    def body(g_ref, v_ref, w_ref, bias_ref, da_ref, dg_ref, dv_ref, dw_ref, dbias_ref, pad_s, dpad_s, dw_s):
        pad_s[0:FFN_PAD, :] = jnp.zeros((FFN_PAD, CB), F32)
        dpad_s[S:S + FFN_PAD, :] = jnp.zeros((FFN_PAD, CB), F32)
        dw_s[...] = jnp.zeros_like(dw_s)

        def fill(i, carry):
            pad_s[pl.ds(pl.multiple_of(FFN_PAD + i * tm, SUBLANES), tm), :] = g_ref[pl.ds(pl.multiple_of(i * tm, tm), tm), :]
            return carry

        lax.fori_loop(0, S // tm, fill, 0)

        def first(i, carry):
            r0 = pl.multiple_of(i * tm, tm)
            rows = pl.ds(r0, tm)
            g1, taps = _ffn_gate(pad_s, w_ref, bias_ref, r0, tm)
            sg = _sigmoid(g1)
            da = da_ref[rows, :].astype(F32)
            dv_ref[rows, :] = (da * g1 * sg).astype(BF16)
            dg1 = da * v_ref[rows, :] * (sg * (1.0 + g1 * (1.0 - sg)))
            dpad_s[rows, :] = dg1
            for k in range(FFN_KERNEL):
                dw_s[SUBLANES * k:SUBLANES * (k + 1), :] += _fold_rows(dg1 * taps[FFN_PAD - 2 + k])
            dw_s[SUBLANES * FFN_KERNEL:SUBLANES * (FFN_KERNEL + 1), :] += _fold_rows(dg1)
            return carry

        lax.fori_loop(0, S // tm, first, 0)

        def second(i, carry):
            r0 = pl.multiple_of(i * tm, tm)
            n_rows = tm + FFN_PAD
            taps = _taps(_shifted(dpad_s[pl.ds(r0, n_rows), :], n_rows), range(FFN_KERNEL), tm)
            dg_ref[pl.ds(r0, tm), :] = (w_ref[2:3, :] * taps[0] + w_ref[1:2, :] * taps[1] + w_ref[0:1, :] * taps[2]).astype(BF16)
            return carry

        lax.fori_loop(0, S // tm, second, 0)
        for k in range(FFN_KERNEL):
            dw_ref[k:k + 1, :] = jnp.sum(dw_s[SUBLANES * k:SUBLANES * (k + 1), :], axis=0, keepdims=True)
        dbias_ref[...] = jnp.sum(dw_s[SUBLANES * FFN_KERNEL:SUBLANES * (FFN_KERNEL + 1), :], axis=0, keepdims=True)

    col = lambda off: pl.BlockSpec((S, CB), lambda j: (0, off + j))
    wspec = pl.BlockSpec((FFN_KERNEL, CB), lambda j: (0, j))
    bspec = pl.BlockSpec((1, CB), lambda j: (0, j))
    return pl.pallas_call(
        body, name="ffn_act_bwd", grid=(nb,),
        in_specs=[col(0), col(nb), wspec, bspec, col(0)],
        out_specs=[col(0), col(0), wspec, bspec],
        out_shape=[jax.ShapeDtypeStruct((S, D_FF), BF16)] * 2
        + [jax.ShapeDtypeStruct((FFN_KERNEL, D_FF), F32), jax.ShapeDtypeStruct((1, D_FF), F32)],
        scratch_shapes=[pltpu.VMEM((S + FFN_PAD, CB), F32), pltpu.VMEM((S + FFN_PAD, CB), F32),
                        pltpu.VMEM((SUBLANES * (FFN_KERNEL + 1), CB), F32)],
        compiler_params=_params("parallel"),
    )(u, u, w, bias, dact)


def _loss_grad(y, target, *, tm=512):
    S, D = y.shape

    def body(y_ref, t_ref, dy_ref, l_ref):
        d = y_ref[...] - t_ref[...]
        dy_ref[...] = d * (1.0 / D)

        @pl.when(pl.program_id(0) == 0)
        def _():
            l_ref[...] = jnp.zeros_like(l_ref)

        l_ref[...] += 0.5 * jnp.sum(jnp.mean(d * d, axis=-1, keepdims=True), axis=0, keepdims=True)

    dy, l = pl.pallas_call(
        body, name="loss_grad", grid=(S // tm,),
        in_specs=[pl.BlockSpec((tm, D), lambda i: (i, 0))] * 2,
        out_specs=[pl.BlockSpec((tm, D), lambda i: (i, 0)), pl.BlockSpec((SUBLANES, LANES), lambda i: (0, 0))],
        out_shape=[jax.ShapeDtypeStruct((S, D), F32), jax.ShapeDtypeStruct((SUBLANES, LANES), F32)],
        compiler_params=_params("arbitrary"),
    )(y, target)
    return dy, l[0, 0]


def _row_tile(rows, cap=512):
    t = min(rows, cap)
    while rows % t or t % SUBLANES:
        t -= 1
    return t


def _adam_update(w, g, m, v):
    m1 = ADAM_B1 * m + (1.0 - ADAM_B1) * g
    v1 = ADAM_B2 * v + (1.0 - ADAM_B2) * (g * g)
    m_hat = m1 / (1.0 - ADAM_B1 ** ADAM_STEP)
    v_hat = v1 / (1.0 - ADAM_B2 ** ADAM_STEP)
    return -ADAM_LR * (m_hat / (jnp.sqrt(v_hat) + ADAM_EPS) + ADAM_WD * w), m1, v1


def _adamw(w, g, m, v):
    R, C = w.shape
    tr = _row_tile(R, 256)

    def body(w_ref, g_ref, m_ref, v_ref, d_ref, nm_ref, nv_ref):
        d_ref[...], nm_ref[...], nv_ref[...] = _adam_update(w_ref[...], g_ref[...], m_ref[...], v_ref[...])

    spec = pl.BlockSpec((tr, C), lambda i: (i, 0))
    return pl.pallas_call(
        body, name="adamw", grid=(R // tr,),
        in_specs=[spec] * 4, out_specs=[spec] * 3,
        out_shape=[jax.ShapeDtypeStruct((R, C), F32)] * 3,
        compiler_params=_params("parallel"),
    )(w, g, m, v)


def _adamw_layer(layer, w, g, m, v, prev):
    L, R, C = w.shape
    tr = _row_tile(R, 256)
    n_prev = 4 if prev else 0

    def body(*refs):
        w_ref, g_ref, m_ref, v_ref = refs[:4]
        go_ref, d_ref, nm_ref, nv_ref = refs[4 + n_prev:8 + n_prev]
        gv = g_ref[...]
        go_ref[...] = gv
        d_ref[...], nm_ref[...], nv_ref[...] = _adam_update(w_ref[...], gv, m_ref[...], v_ref[...])

    stacked = pl.BlockSpec((None, tr, C), lambda i: (layer, i, 0))
    return pl.pallas_call(
        body, name="adamw_layer", grid=(R // tr,),
        in_specs=[stacked, pl.BlockSpec((tr, C), lambda i: (i, 0)), stacked, stacked] + [ANY] * n_prev,
        out_specs=[stacked] * 4, out_shape=[jax.ShapeDtypeStruct((L, R, C), F32)] * 4,
        input_output_aliases={4 + j: j for j in range(n_prev)},
        compiler_params=_params("parallel"),
    )(w, g, m, v, *(prev or ()))


def _cast_into_full(w, layer, kind, chip):
    _, R, C = w.shape
    tr = _row_tile(R, 512)

    def body(chip_ref, w_ref, o_ref):
        o_ref[...] = w_ref[...].astype(BF16)

    if kind == "col":
        out_shape = jax.ShapeDtypeStruct((1, R, N_CHIPS * C), BF16)
        out_spec = pl.BlockSpec((None, tr, C), lambda i, chip_ref: (0, i, chip_ref[0]))
    else:
        out_shape = jax.ShapeDtypeStruct((1, N_CHIPS * R, C), BF16)
        out_spec = pl.BlockSpec((None, tr, C), lambda i, chip_ref: (0, chip_ref[0] * (R // tr) + i, 0))
    return pl.pallas_call(
        body, name="cast_into_full_" + kind,
        grid_spec=pltpu.PrefetchScalarGridSpec(
            num_scalar_prefetch=1, grid=(R // tr,),
            in_specs=[pl.BlockSpec((None, tr, C), lambda i, chip_ref: (layer, i, 0))], out_specs=out_spec),
        out_shape=out_shape, compiler_params=_params("parallel"),
    )(chip, w)


def _add_half(g4, la, c):
    L, _, H, W = g4.shape
    th = _row_tile(H, 256)

    def body(c_ref, g_ref, la_ref, o_ref):
        o_ref[...] = (g_ref[...].astype(F32) + la_ref[...].astype(F32)).astype(BF16)

    return pl.pallas_call(
        body, name="add_half",
        grid_spec=pltpu.PrefetchScalarGridSpec(
            num_scalar_prefetch=1, grid=(L, H // th),
            in_specs=[pl.BlockSpec((None, None, th, W), lambda l, i, c_ref: (l, c_ref[0], i, 0)),
                      pl.BlockSpec((None, th, W), lambda l, i, c_ref: (l, i, 0))],
            out_specs=pl.BlockSpec((None, th, W), lambda l, i, c_ref: (l, i, 0))),
        out_shape=jax.ShapeDtypeStruct((L, H, W), BF16),
        compiler_params=_params("parallel", "parallel"),
    )(c, g4, la)


def _add_parts(p, lb, place, kind):
    _, L, H, C = lb.shape
    th = _row_tile(H, 256)

    def body(s_ref, p_ref, lb_ref, o_ref):
        acc = p_ref[...].astype(F32)
        for k in range(N_CHIPS - 1):
            acc = acc + lb_ref[k].astype(F32)
        o_ref[...] = acc

    if kind == "col":
        p_spec = pl.BlockSpec((None, th, C), lambda l, i, s_ref: (l, i, s_ref[0]))
    else:
        p_spec = pl.BlockSpec((None, None, th, C), lambda l, i, s_ref: (l, s_ref[0], i, 0))
    return pl.pallas_call(
        body, name="add_parts_" + kind,
        grid_spec=pltpu.PrefetchScalarGridSpec(
            num_scalar_prefetch=1, grid=(L, H // th),
            in_specs=[p_spec, pl.BlockSpec((N_CHIPS - 1, None, th, C), lambda l, i, s_ref: (0, l, i, 0))],
            out_specs=pl.BlockSpec((None, None, th, C), lambda l, i, s_ref: (l, s_ref[1], i, 0))),
        out_shape=jax.ShapeDtypeStruct((L, 2, H, C), F32),
        compiler_params=_params("parallel", "parallel"),
    )(place, p, lb)


ANY = pl.BlockSpec(memory_space=pl.ANY)


def _place():
    x, y, c = lax.axis_index("x"), lax.axis_index("y"), lax.axis_index("c")
    chips = [(1 - x, y), (x, 1 - y), (1 - x, 1 - y)]
    return x, y, c, chips


def _comm_call(body, name, ins, out_shape, n_remote, n_local, aliases=None):
    scratch = [pltpu.SemaphoreType.DMA((n_remote,)), pltpu.SemaphoreType.DMA((n_remote,))]
    if n_local:
        scratch.append(pltpu.SemaphoreType.DMA((n_local,)))
    return pl.pallas_call(
        body, name=name, in_specs=[ANY] * len(ins), out_specs=[ANY] * len(out_shape), out_shape=out_shape,
        scratch_shapes=scratch, input_output_aliases=aliases or {},
        compiler_params=pltpu.CompilerParams(has_side_effects=True),
    )(*ins)


def _remote(src, dst, send, recv, k, to):
    return pltpu.make_async_remote_copy(src_ref=src, dst_ref=dst, send_sem=send.at[k], recv_sem=recv.at[k],
                                        device_id=to, device_id_type=MESH)


def _gather_weights(fulls, kinds):
    n = len(fulls)
    out_shape = [jax.ShapeDtypeStruct(f.shape, f.dtype) for f in fulls]

    def body(*refs):
        outs, (send, recv) = refs[n:2 * n], refs[2 * n:]
        first = _gather_copies(outs, kinds, send, recv, 0)
        for cp in first:
            cp.start()
        for cp in first:
            cp.wait()
        passed = _pass_on_copies(outs, kinds, send, recv, len(first))
        for cp in passed:
            cp.start()
        for cp in passed:
            cp.wait()

    return _comm_call(body, "gather_weights", fulls, out_shape, 2 * n * (N_CHIPS - 1), 0, {a: a for a in range(n)})


def _window(ref, kind, s, h):
    if kind == "col":
        H, C = ref.shape[1] // 2, ref.shape[2] // N_CHIPS
        return ref.at[:, pl.ds(pl.multiple_of(h * H, 16), H), pl.ds(pl.multiple_of(s * C, LANES), C)]
    R = ref.shape[1] // N_CHIPS
    return ref.at[:, pl.ds(pl.multiple_of(s * R + h * (R // 2), 16), R // 2), :]


def _gather_copies(outs, kinds, send, recv, sem0):
    x, y, c, chips = _place()
    me = 2 * x + y
    return [_remote(_window(o, kind, me, c), _window(o, kind, me, c), send, recv, sem0 + a * (N_CHIPS - 1) + k, (*chip, c))
            for a, (o, kind) in enumerate(zip(outs, kinds)) for k, chip in enumerate(chips)]


def _pass_on_copies(outs, kinds, send, recv, sem0):
    x, y, c, chips = _place()
    cps = []
    for a, (o, kind) in enumerate(zip(outs, kinds)):
        for k, chip in enumerate(chips):
            landed = _window(o, kind, 2 * chip[0] + chip[1], c)
            cps.append(_remote(landed, landed, send, recv, sem0 + a * (N_CHIPS - 1) + k, (x, y, 1 - c)))
    return cps


def _pass_on(fulls, kinds):
    n = len(fulls)
    out_shape = [jax.ShapeDtypeStruct(f.shape, f.dtype) for f in fulls]

    def body(*refs):
        outs, (send, recv) = refs[n:2 * n], refs[2 * n:]
        cps = _pass_on_copies(outs, kinds, send, recv, 0)
        for cp in cps:
            cp.start()
        for cp in cps:
            cp.wait()

    return _comm_call(body, "pass_on", fulls, out_shape, n * (N_CHIPS - 1), 0, {a: a for a in range(n)})


def _gather_small(shards):
    n = len(shards)
    out_shape = [jax.ShapeDtypeStruct((N_CHIPS,) + s.shape, s.dtype) for s in shards]

    def body(*refs):
        srcs, outs, (send, recv, loc) = refs[:n], refs[n:2 * n], refs[2 * n:]
        x, y, c, chips = _place()
        me = 2 * x + y
        remote, local = [], []
        for a in range(n):
            local.append(pltpu.make_async_copy(srcs[a], outs[a].at[me], loc.at[a]))
            for k, chip in enumerate(chips):
                remote.append(_remote(srcs[a], outs[a].at[me], send, recv, a * (N_CHIPS - 1) + k, (*chip, c)))
        for cp in local + remote:
            cp.start()
        for cp in remote + local:
            cp.wait()

    return _comm_call(body, "gather_small", shards, out_shape, n * (N_CHIPS - 1), n)


def _exchange_halves(g4s):
    n = len(g4s)
    out_shape = [jax.ShapeDtypeStruct((g.shape[0],) + g.shape[2:], g.dtype) for g in g4s]

    def body(*refs):
        gs, las, (send, recv) = refs[:n], refs[n:2 * n], refs[2 * n:]
        x, y, c, _ = _place()
        cps = [_remote(gs[a].at[:, 1 - c], las[a], send, recv, a, (x, y, 1 - c)) for a in range(n)]
        for cp in cps:
            cp.start()
        for cp in cps:
            cp.wait()

    return _comm_call(body, "exchange_halves", g4s, out_shape, n, 0)


def _scatter_partials(ps, kinds):
    n = len(ps)

    def body(*refs):
        srcs, lbs, (send, recv) = refs[:n], refs[n:2 * n], refs[2 * n:]
        cps = _scatter_copies(srcs, lbs, kinds, send, recv)
        for cp in cps:
            cp.start()
        for cp in cps:
            cp.wait()

    return _comm_call(body, "scatter_partials", ps, _scatter_shapes(ps, kinds), n * (N_CHIPS - 1), 0)


def _scatter_shapes(ps, kinds):
    out_shape = []
    for p, kind in zip(ps, kinds):
        L, H, C = (p.shape[0], p.shape[1], p.shape[2] // N_CHIPS) if kind == "col" else (p.shape[0], p.shape[2], p.shape[3])
        out_shape.append(jax.ShapeDtypeStruct((N_CHIPS - 1, L, H, C), p.dtype))
    return out_shape


def _scatter_copies(srcs, lbs, kinds, send, recv):
    x, y, c, chips = _place()
    cps = []
    for a, (src, lb, kind) in enumerate(zip(srcs, lbs, kinds)):
        C = lb.shape[3]
        for k, chip in enumerate(chips):
            s = 2 * chip[0] + chip[1]
            part = src.at[:, :, pl.ds(pl.multiple_of(s * C, LANES), C)] if kind == "col" else src.at[:, s]
            cps.append(_remote(part, lb.at[k], send, recv, a * (N_CHIPS - 1) + k, (*chip, c)))
    return cps


def _share_halves(g4s):
    n = len(g4s)
    out_shape = [jax.ShapeDtypeStruct(g.shape, g.dtype) for g in g4s]

    def body(*refs):
        outs, (send, recv) = refs[n:2 * n], refs[2 * n:]
        x, y, c, _ = _place()
        cps = [_remote(outs[a].at[:, c], outs[a].at[:, c], send, recv, a, (x, y, 1 - c)) for a in range(n)]
        for cp in cps:
            cp.start()
        for cp in cps:
            cp.wait()

    return _comm_call(body, "share_halves", g4s, out_shape, n, 0, {a: a for a in range(n)})


def _allreduce_small(part):
    N = part.shape[0]

    def body(p_ref, o_ref, buf, send, recv):
        x, y, c, _ = _place()
        me = 4 * x + 2 * y + c
        buf[me] = p_ref[...]
        cps = []
        for k in range(1, N_DEV):
            peer = (1 - x if k & 4 else x, 1 - y if k & 2 else y, 1 - c if k & 1 else c)
            cps.append(_remote(p_ref, buf.at[me], send, recv, k - 1, peer))
        for cp in cps:
            cp.start()
        for cp in cps:
            cp.wait()
        acc = buf[0]
        for i in range(1, N_DEV):
            acc = acc + buf[i]
        o_ref[...] = acc

    vmem = pl.BlockSpec(memory_space=pltpu.VMEM)
    return pl.pallas_call(
        body, name="allreduce_small", in_specs=[vmem], out_specs=vmem,
        out_shape=jax.ShapeDtypeStruct((N, LANES), F32),
        scratch_shapes=[pltpu.VMEM((N_DEV, N, LANES), F32), pltpu.SemaphoreType.DMA((N_DEV - 1,)),
                        pltpu.SemaphoreType.DMA((N_DEV - 1,))],
        compiler_params=pltpu.CompilerParams(has_side_effects=True, vmem_limit_bytes=VMEM_LIMIT_BYTES),
    )(part)


AFTER_ATTENTION = ("w_out", "w_up", "w_down")


def _layer_fwd(x, p, fulls=(), kinds=()):
    h1, proj = _norm_matmul(x, p["norm1_g"], p["w_in"])
    attn, ltot, filled = _attn_fwd(proj, p["q_norm_g"], p["k_norm_g"], fulls, kinds)
    if filled:
        filled = _pass_on(filled, kinds)
        p = dict(p, **{n: f[0] for n, f in zip(AFTER_ATTENTION, filled)})
    c1 = _glu_conv_fwd(proj, p["conv_dw_w"], p["conv_dw_b"])
    c = _ln_silu_fwd(c1, p["conv_ln_g"], p["conv_ln_b"])
    x_mid = _matmul_res([attn, c], p["w_out"], x)
    h2, u = _norm_matmul(x_mid, p["norm2_g"], p["w_up"])
    act = _ffn_act_fwd(u, p["ffn_dw_w"], p["ffn_dw_b"])
    x_out = _matmul_res([act], p["w_down"], x_mid)
    saved = dict(x=x, h1=h1, proj=proj, attn=attn, ltot=ltot, c1=c1, c=c, x_mid=x_mid, h2=h2, u=u, act=act)
    return x_out, saved, filled


def _chip_partials(gs, kinds, core):
    g4s = []
    for g, kind in zip(gs, kinds):
        if kind == "col":
            g4s.append(g.reshape(1, 2, g.shape[0] // 2, g.shape[1]))
        else:
            g4s.append(g.reshape(N_CHIPS, 2, g.shape[0] // N_CHIPS // 2, g.shape[1]))
    parts = [_add_half(g4, la, core) for g4, la in zip(g4s, _exchange_halves(g4s))]
    return [p if kind == "col" else p.reshape(1, N_CHIPS, p.shape[1], p.shape[2]) for p, kind in zip(parts, kinds)]


def _owned_sums(parts, landed, kinds, place):
    halves = [_add_parts(p, lb, place, kind) for p, lb, kind in zip(parts, landed, kinds)]
    return [g.reshape(2 * g.shape[2], g.shape[3]) for g in _share_halves(halves)]


def _layer_bwd(dx_out, s, p, comm=None):
    g = {}
    dact = _matmul_nt([dx_out], p["w_down"], BF16)
    g["w_down"] = _matmul_tn(s["act"], dx_out, tk=D_FF // 2, tn=D_MODEL)
    dgate, dval, g["ffn_dw_w"], g["ffn_dw_b"] = _ffn_act_bwd(s["u"], p["ffn_dw_w"], p["ffn_dw_b"], dact)
    dx_mid, g["norm2_g"] = _matmul_nt_rmsbwd([dgate, dval], p["w_up"], s["x_mid"], p["norm2_g"], dx_out)
    g["w_up"] = jnp.concatenate([_matmul_tn(s["h2"], d, tk=D_MODEL, tn=D_FF // 2) for d in (dgate, dval)], axis=1)
    dmix = _matmul_nt([dx_mid], p["w_out"], F32)
    g["w_out"] = jnp.concatenate([_matmul_tn(m, dx_mid, tk=ATTN_WIDTH, tn=D_MODEL) for m in (s["attn"], s["c"])], axis=0)
    dc1, g["conv_ln_g"], g["conv_ln_b"] = _ln_silu_bwd(s["c1"], p["conv_ln_g"], p["conv_ln_b"], dmix)
    da, db, g["conv_dw_w"], g["conv_dw_b"] = _glu_conv_bwd(s["proj"], p["conv_dw_w"], dc1)
    parts, kinds, sums = [], [], []
    if comm is not None:
        core, place, pending = comm
        kinds = [BIG_KIND[n] for n in AFTER_ATTENTION] + ["col"] * len(pending)
        parts = _chip_partials([g[n] for n in AFTER_ATTENTION] + list(pending), kinds, core)
    (dq, dk, dv, g["q_norm_g"], g["k_norm_g"]), landed = _attn_bwd(
        s["proj"], p["q_norm_g"], p["k_norm_g"], s["ltot"], dmix, parts, kinds)
    if comm is not None:
        sums = _owned_sums(parts, landed, kinds, place)
    pieces = [dq, dk, dv, da, db]
    dx, g["norm1_g"] = _matmul_nt_rmsbwd(pieces, p["w_in"], s["x"], p["norm1_g"], dx_mid)
    g["w_in"] = jnp.concatenate([_matmul_tn(s["h1"], d, tk=D_MODEL, tn=ATTN_WIDTH) for d in pieces], axis=1)
    return dx, g, sums


WEIGHTS = ("norm1_g", "w_in", "q_norm_g", "k_norm_g", "conv_dw_w", "conv_dw_b", "conv_ln_g", "conv_ln_b",
           "w_out", "norm2_g", "w_up", "ffn_dw_w", "ffn_dw_b", "w_down")
BIG = ("w_in", "w_out", "w_up", "w_down")
BIG_KIND = {"w_in": "col", "w_out": "row", "w_up": "col", "w_down": "row"}
SMALL_SHARDED = ("conv_dw_w", "ffn_dw_w")
REPLICATED = tuple(n for n in WEIGHTS if n not in BIG + SMALL_SHARDED)


def _pack(arrays):
    flat = jnp.concatenate([a.reshape(-1) for a in arrays])
    rows = -(-flat.shape[0] // (SUBLANES * LANES)) * SUBLANES
    return jnp.pad(flat, (0, rows * LANES - flat.shape[0])).reshape(rows, LANES)


def _unpack(packed, shapes):
    flat = packed.reshape(-1)
    out, off = [], 0
    for shape in shapes:
        size = 1
        for d in shape:
            size *= d
        out.append(flat[off:off + size].reshape(shape))
        off += size
    return out


def _unshard_last(stacked):
    n, L, K, C = stacked.shape
    return jnp.transpose(stacked, (1, 2, 0, 3)).reshape(L, K, n * C)


def kernel(x, norm1_g, w_in, q_norm_g, k_norm_g, conv_dw_w, conv_dw_b, conv_ln_g, conv_ln_b, w_out, norm2_g, w_up, ffn_dw_w, ffn_dw_b, w_down, loss_target, m_norm1_g, m_w_in, m_q_norm_g, m_k_norm_g, m_conv_dw_w, m_conv_dw_b, m_conv_ln_g, m_conv_ln_b, m_w_out, m_norm2_g, m_w_up, m_ffn_dw_w, m_ffn_dw_b, m_w_down, v_norm1_g, v_w_in, v_q_norm_g, v_k_norm_g, v_conv_dw_w, v_conv_dw_b, v_conv_ln_g, v_conv_ln_b, v_w_out, v_norm2_g, v_w_up, v_ffn_dw_w, v_ffn_dw_b, v_w_down):
    given = dict(locals())
    w = {n: given[n] for n in WEIGHTS}
    m = {n: given["m_" + n] for n in WEIGHTS}
    v = {n: given["v_" + n] for n in WEIGHTS}
    chip = 2 * lax.axis_index("x") + lax.axis_index("y")
    core = lax.axis_index("c")
    chip_arr = jnp.reshape(chip, (1,)).astype(jnp.int32)
    core_arr = jnp.reshape(core, (1,)).astype(jnp.int32)
    L = DEPTH

    place = jnp.concatenate([chip_arr, core_arr])

    full = [{n: _cast_into_full(w[n], l, BIG_KIND[n], chip_arr) for n in BIG} for l in range(L)]
    full[0]["w_in"] = _gather_weights([full[0]["w_in"]], ["col"])[0]
    small_full = {n: _unshard_last(stacked)
                  for n, stacked in zip(SMALL_SHARDED, _gather_small([w[n] for n in SMALL_SHARDED]))}
    params = []
    for l in range(L):
        p = {n: small_full[n][l] for n in SMALL_SHARDED}
        p.update({n: w[n][l][None] for n in REPLICATED})
        params.append(p)

    act = x[0]
    saved = []
    for l in range(L):
        group = [(l, n) for n in AFTER_ATTENTION] + ([(l + 1, "w_in")] if l + 1 < L else [])
        act, s, filled = _layer_fwd(act, dict(params[l], w_in=full[l]["w_in"][0]),
                                    [full[i][n] for i, n in group], [BIG_KIND[n] for _, n in group])
        for (i, n), f in zip(group, filled):
            full[i][n] = f
        saved.append(s)
    for l in range(L):
        params[l].update({n: full[l][n][0] for n in BIG})
    dx, loss_part = _loss_grad(act, loss_target[0])
    loss = lax.psum(loss_part, ("x", "y", "c"))

    grads = [None] * L
    summed = {}
    pending = []
    for l in reversed(range(L)):
        dx, grads[l], sums = _layer_bwd(dx, saved[l], params[l], (core_arr, place, pending))
        names = [(l, n) for n in AFTER_ATTENTION] + ([(l + 1, "w_in")] if pending else [])
        summed.update(zip(names, sums))
        pending = [grads[l]["w_in"]]
    parts = _chip_partials(pending, ["col"], core_arr)
    summed[0, "w_in"] = _owned_sums(parts, _scatter_partials(parts, ["col"]), ["col"], place)[0]

    grad, delta, new_m, new_v = {}, {}, {}, {}
    for n in BIG:
        out = None
        for l in range(L):
            out = _adamw_layer(l, w[n], summed[l, n], m[n], v[n], out)
        grad[n], delta[n], new_m[n], new_v[n] = out

    small = REPLICATED + SMALL_SHARDED
    small_grads = [jnp.stack([grads[l][n] for l in range(L)]) for n in small]
    small_sums = _unpack(_allreduce_small(_pack(small_grads)), [a.shape for a in small_grads])
    for n, g in zip(small, small_sums):
        if n in REPLICATED:
            grad[n] = g.reshape(w[n].shape)
        else:
            width = w[n].shape[2]
            grad[n] = lax.dynamic_slice_in_dim(g, chip * width, width, axis=2)
    shapes = [w[n].shape for n in small]
    packed = _adamw(*[_pack([src[n] for n in small]) for src in (w, grad, m, v)])
    for out, pk in zip((delta, new_m, new_v), packed):
        out.update(zip(small, _unpack(pk, shapes)))

    return (loss, dx[None], *[grad[n] for n in WEIGHTS], *[delta[n] for n in WEIGHTS],
            *[new_m[n] for n in WEIGHTS], *[new_v[n] for n in WEIGHTS])
```

```python
import functools

import jax
import jax.numpy as jnp
from jax import lax
from jax.experimental import pallas as pl
from jax.experimental.pallas import tpu as pltpu

F32 = jnp.float32
BF16 = jnp.bfloat16

DEPTH = 4
D_MODEL = 1024
HEADS = 8
HEAD_DIM = 64
ATTN_WIDTH = HEADS * HEAD_DIM
CONV_WIDTH = D_MODEL - ATTN_WIDTH
CONV_KERNEL = 31
D_FF = 2816
FFN_KERNEL = 3
EPS = 1e-6
ADAM_LR, ADAM_B1, ADAM_B2, ADAM_EPS, ADAM_WD, ADAM_STEP = 0.001, 0.9, 0.999, 1e-08, 0.01, 10

N_CHIPS = 4
N_DEV = 8
LANES = 128
SUBLANES = 8
VMEM_LIMIT_BYTES = 56 * 2**20
ATTN_TILE = 256
ATTN_HEADS_PER_STEP = 4
ATTN_BLOCK = ATTN_HEADS_PER_STEP * HEAD_DIM
CONV_PAD = 32
FFN_PAD = 8
MESH = pl.DeviceIdType.MESH


def _params(*sem):
    return pltpu.CompilerParams(dimension_semantics=sem if sem else None, vmem_limit_bytes=VMEM_LIMIT_BYTES)


def _dot(a, b):
    return jnp.dot(a, b, preferred_element_type=F32)


def _dot_nt(a, b):
    return lax.dot_general(a, b, (((1,), (1,)), ((), ())), preferred_element_type=F32)


def _dot_tn(a, b):
    return lax.dot_general(a, b, (((0,), (0,)), ((), ())), preferred_element_type=F32)


def _sigmoid(x):
    return 1.0 / (1.0 + jnp.exp(-x))


def _norm_matmul(x, g, w, *, tm=512, n_split=2):
    S, D = x.shape
    N = w.shape[1]
    tn = N // n_split

    def body(x_ref, g_ref, w_ref, h_ref, y_ref):
        @pl.when(pl.program_id(1) == 0)
        def _():
            xv = x_ref[...]
            r = lax.rsqrt(jnp.mean(xv * xv, axis=-1, keepdims=True) + EPS)
            h_ref[...] = (xv * r * g_ref[...]).astype(BF16)

        y_ref[...] = _dot(h_ref[...], w_ref[...])

    return pl.pallas_call(
        body, name="norm_matmul", grid=(S // tm, n_split),
        in_specs=[pl.BlockSpec((tm, D), lambda i, j: (i, 0)),
                  pl.BlockSpec((1, D), lambda i, j: (0, 0)),
                  pl.BlockSpec((D, tn), lambda i, j: (0, j))],
        out_specs=[pl.BlockSpec((tm, D), lambda i, j: (i, 0)),
                   pl.BlockSpec((tm, tn), lambda i, j: (i, j))],
        out_shape=[jax.ShapeDtypeStruct((S, D), BF16), jax.ShapeDtypeStruct((S, N), F32)],
        compiler_params=_params("parallel", "arbitrary"),
    )(x, g, w)


def _matmul_res(pieces, w, res, *, tm=512):
    S, N = res.shape
    K = w.shape[0]
    widths = [p.shape[1] for p in pieces]
    assert sum(widths) == K

    def body(*refs):
        p_refs, (w_ref, res_ref, o_ref) = refs[:len(pieces)], refs[len(pieces):]
        acc = res_ref[...]
        off = 0
        for p_ref, kp in zip(p_refs, widths):
            acc = acc + _dot(p_ref[...], w_ref[off:off + kp, :])
            off += kp
        o_ref[...] = acc

    return pl.pallas_call(
        body, name="matmul_res", grid=(S // tm,),
        in_specs=[pl.BlockSpec((tm, kp), lambda i: (i, 0)) for kp in widths]
        + [pl.BlockSpec((K, N), lambda i: (0, 0)), pl.BlockSpec((tm, N), lambda i: (i, 0))],
        out_specs=pl.BlockSpec((tm, N), lambda i: (i, 0)),
        out_shape=jax.ShapeDtypeStruct((S, N), F32),
        compiler_params=_params("parallel"),
    )(*pieces, w, res)


def _nt_sum(p_refs, widths, w_ref):
    acc = None
    off = 0
    for p_ref, n_p in zip(p_refs, widths):
        d = _dot_nt(p_ref[...].astype(BF16), w_ref[:, off:off + n_p])
        acc = d if acc is None else acc + d
        off += n_p
    return acc


def _matmul_nt(pieces, w, out_dtype, *, tm=512):
    S = pieces[0].shape[0]
    K, N = w.shape
    widths = [p.shape[1] for p in pieces]
    assert sum(widths) == N

    def body(*refs):
        p_refs, (w_ref, o_ref) = refs[:len(pieces)], refs[len(pieces):]
        o_ref[...] = _nt_sum(p_refs, widths, w_ref).astype(out_dtype)

    return pl.pallas_call(
        body, name="matmul_nt", grid=(S // tm,),
        in_specs=[pl.BlockSpec((tm, n_p), lambda i: (i, 0)) for n_p in widths]
        + [pl.BlockSpec((K, N), lambda i: (0, 0))],
        out_specs=pl.BlockSpec((tm, K), lambda i: (i, 0)),
        out_shape=jax.ShapeDtypeStruct((S, K), out_dtype),
        compiler_params=_params("parallel"),
    )(*pieces, w)


def _matmul_nt_rmsbwd(pieces, w, x, g, dres, *, tm=256):
    S, K = x.shape
    N = w.shape[1]
    widths = [p.shape[1] for p in pieces]
    assert sum(widths) == N

    def body(*refs):
        p_refs, (w_ref, x_ref, g_ref, dres_ref, dx_ref, dg_ref) = refs[:len(pieces)], refs[len(pieces):]
        dh = _nt_sum(p_refs, widths, w_ref)
        xv = x_ref[...]
        r = lax.rsqrt(jnp.mean(xv * xv, axis=-1, keepdims=True) + EPS)
        xh = xv * r
        dxh = dh * g_ref[...]
        dx_ref[...] = dres_ref[...] + r * (dxh - xh * jnp.mean(dxh * xh, axis=-1, keepdims=True))

        @pl.when(pl.program_id(0) == 0)
        def _():
            dg_ref[...] = jnp.zeros_like(dg_ref)

        dg_ref[...] += jnp.sum(dh * xh, axis=0, keepdims=True)

    return pl.pallas_call(
        body, name="matmul_nt_rmsbwd", grid=(S // tm,),
        in_specs=[pl.BlockSpec((tm, n_p), lambda i: (i, 0)) for n_p in widths]
        + [pl.BlockSpec((K, N), lambda i: (0, 0)), pl.BlockSpec((tm, K), lambda i: (i, 0)),
           pl.BlockSpec((1, K), lambda i: (0, 0)), pl.BlockSpec((tm, K), lambda i: (i, 0))],
        out_specs=[pl.BlockSpec((tm, K), lambda i: (i, 0)), pl.BlockSpec((1, K), lambda i: (0, 0))],
        out_shape=[jax.ShapeDtypeStruct((S, K), F32), jax.ShapeDtypeStruct((1, K), F32)],
        compiler_params=_params("arbitrary"),
    )(*pieces, w, x, g, dres)


def _matmul_tn(x, dy, *, tk, tn, ts=512):
    S, M = x.shape
    N = dy.shape[1]
    n_s = S // ts

    def body(x_ref, dy_ref, o_ref, acc_ref):
        s = pl.program_id(2)

        @pl.when(s == 0)
        def _():
            acc_ref[...] = jnp.zeros_like(acc_ref)

        acc_ref[...] += _dot_tn(x_ref[...].astype(BF16), dy_ref[...].astype(BF16))

        @pl.when(s == n_s - 1)
        def _():
            o_ref[...] = acc_ref[...].astype(BF16)

    return pl.pallas_call(
        body, name="matmul_tn", grid=(M // tk, N // tn, n_s),
        in_specs=[pl.BlockSpec((ts, tk), lambda i, j, s: (s, i)),
                  pl.BlockSpec((ts, tn), lambda i, j, s: (s, j))],
        out_specs=pl.BlockSpec((tk, tn), lambda i, j, s: (i, j)),
        out_shape=jax.ShapeDtypeStruct((M, N), BF16),
        scratch_shapes=[pltpu.VMEM((tk, tn), F32)],
        compiler_params=_params("parallel", "parallel", "arbitrary"),
    )(x, dy)


def _tri_consts():
    j = jnp.arange(ATTN_TILE)[:, None]
    s = jnp.arange(ATTN_TILE)[None, :]
    return (j > s).astype(BF16), (j <= s).astype(BF16), (j < s).astype(BF16)


SIGN_BIT = 0x80000000


def _log_terms(sn):
    minus_abs = lax.bitcast_convert_type(lax.bitcast_convert_type(sn, jnp.uint32) | jnp.uint32(SIGN_BIT), F32)
    lom = jnp.minimum(sn, 0.0) - jnp.log(1.0 + jnp.exp(minus_abs))
    return lom, lom - sn


def _causal_mask():
    t = lax.broadcasted_iota(jnp.int32, (ATTN_TILE, ATTN_TILE), 0)
    s = lax.broadcasted_iota(jnp.int32, (ATTN_TILE, ATTN_TILE), 1)
    return s < t


def _attn_prep(h, q_ref, k_ref, v_ref, qg_ref, kg_ref, qn_s, kn_s, vb_s, n_tiles):
    T = ATTN_TILE
    lanes = slice(HEAD_DIM * h, HEAD_DIM * (h + 1))
    scale = -(HEAD_DIM ** -0.5)

    def prep(i, carry):
        rows = pl.ds(pl.multiple_of(i * T, T), T)
        q = q_ref[rows, lanes]
        k = k_ref[rows, lanes]
        rq = lax.rsqrt(jnp.mean(q * q, axis=-1, keepdims=True) + EPS)
        rk = lax.rsqrt(jnp.mean(k * k, axis=-1, keepdims=True) + EPS)
        qn_s[h, rows, :] = (q * rq * qg_ref[...] * scale).astype(BF16)
        kn_s[h, rows, :] = (k * rk * kg_ref[...]).astype(BF16)
        vb_s[h, rows, :] = v_ref[rows, lanes].astype(BF16)
        return carry

    lax.fori_loop(0, n_tiles, prep, 0)


def _attn_fwd(proj, qg, kg, fulls=(), kinds=()):
    S = proj.shape[0]
    n_comm = len(fulls)
    T = ATTN_TILE
    n_tiles = S // T
    suffix, _, _ = _tri_consts()
    pairs = HEADS // ATTN_HEADS_PER_STEP
    q_blk, k_blk, v_blk = 0, ATTN_WIDTH // ATTN_BLOCK, 2 * ATTN_WIDTH // ATTN_BLOCK

    def body(*refs):
        q_ref, k_ref, v_ref, qg_ref, kg_ref, tri_ref = refs[:6]
        o_ref, lt_ref = refs[6 + n_comm:8 + n_comm]
        w_refs = refs[8 + n_comm:8 + 2 * n_comm]
        qn_s, kn_s, vb_s = refs[8 + 2 * n_comm:11 + 2 * n_comm]
        sems = refs[11 + 2 * n_comm:]
        if n_comm:
            @pl.when(pl.program_id(0) == 0)
            def _():
                for cp in _gather_copies(w_refs, kinds, *sems, 0):
                    cp.start()

        heads = range(ATTN_HEADS_PER_STEP)
        for h in heads:
            _attn_prep(h, q_ref, k_ref, v_ref, qg_ref, kg_ref, qn_s, kn_s, vb_s, n_tiles)

        def q_tile(qi, carry0):
            qrows = pl.ds(pl.multiple_of(qi * T, T), T)
            qt = [qn_s[h, qrows, :] for h in heads]

            def tile(kj, carry, diag):
                krows = pl.ds(pl.multiple_of(kj * T, T), T)
                s = [_dot_nt(qt[h], kn_s[h, krows, :]) for h in heads]
                terms = [_log_terms(s[h]) for h in heads]
                lom = [terms[h][0] for h in heads]
                if diag:
                    mask = _causal_mask()
                    lom = [jnp.where(mask, lom[h], 0.0) for h in heads]
                lom = [lom[h].astype(BF16) for h in heads]
                tail = [_dot(lom[h], tri_ref[...]) for h in heads]
                w = [jnp.exp(terms[h][1] + tail[h] + carry[h][1]) for h in heads]
                if diag:
                    w = [jnp.where(mask, w[h], 0.0) for h in heads]
                return tuple((carry[h][0] + _dot(w[h].astype(BF16), vb_s[h, krows, :]),
                              carry[h][1] + (tail[h][:, 0:1] + lom[h][:, 0:1].astype(F32))) for h in heads)

            init = tuple((jnp.zeros((T, HEAD_DIM), F32), jnp.zeros((T, 1), F32)) for h in heads)
            carry = lax.fori_loop(0, qi, lambda t, cr: tile(qi - 1 - t, cr, False), tile(qi, init, True))
            for h in heads:
                lanes = slice(HEAD_DIM * h, HEAD_DIM * (h + 1))
                o_ref[qrows, lanes] = carry[h][0].astype(BF16)
                lt_ref[qrows, lanes] = jnp.broadcast_to(carry[h][1], (T, HEAD_DIM))
            return carry0

        lax.fori_loop(0, n_tiles, q_tile, 0)
        if n_comm:
            @pl.when(pl.program_id(0) == pairs - 1)
            def _():
                for cp in _gather_copies(w_refs, kinds, *sems, 0):
                    cp.wait()

    n_sem = n_comm * (N_CHIPS - 1)
    out = pl.pallas_call(
        body, name="attn_fwd_gather" if n_comm else "attn_fwd", grid=(pairs,),
        in_specs=[pl.BlockSpec((S, ATTN_BLOCK), lambda p: (0, q_blk + p)),
                  pl.BlockSpec((S, ATTN_BLOCK), lambda p: (0, k_blk + p)),
                  pl.BlockSpec((S, ATTN_BLOCK), lambda p: (0, v_blk + p)),
                  pl.BlockSpec((1, HEAD_DIM), lambda p: (0, 0)),
                  pl.BlockSpec((1, HEAD_DIM), lambda p: (0, 0)),
                  pl.BlockSpec((T, T), lambda p: (0, 0))] + [ANY] * n_comm,
        out_specs=[pl.BlockSpec((S, ATTN_BLOCK), lambda p: (0, p)),
                   pl.BlockSpec((None, S, ATTN_BLOCK), lambda p: (p, 0, 0))] + [ANY] * n_comm,
        out_shape=[jax.ShapeDtypeStruct((S, ATTN_WIDTH), BF16),
                   jax.ShapeDtypeStruct((pairs, S, ATTN_BLOCK), F32)] + [jax.ShapeDtypeStruct(f.shape, f.dtype) for f in fulls],
        scratch_shapes=[pltpu.VMEM((ATTN_HEADS_PER_STEP, S, HEAD_DIM), BF16)] * 3
        + ([pltpu.SemaphoreType.DMA((n_sem,))] * 2 if n_comm else []),
        input_output_aliases={6 + a: 2 + a for a in range(n_comm)},
        compiler_params=pltpu.CompilerParams(dimension_semantics=("arbitrary",), vmem_limit_bytes=VMEM_LIMIT_BYTES,
                                             has_side_effects=bool(n_comm)),
    )(proj, proj, proj, qg, kg, suffix, *fulls)
    return out[0], out[1], list(out[2:])


def _attn_bwd(proj, qg, kg, ltot, dmix, parts=(), kinds=()):
    S = proj.shape[0]
    n_comm = len(parts)
    T = ATTN_TILE
    n_tiles = S // T
    _, prefix_incl, prefix_excl = _tri_consts()
    pairs = HEADS // ATTN_HEADS_PER_STEP
    q_blk, k_blk, v_blk = 0, ATTN_WIDTH // ATTN_BLOCK, 2 * ATTN_WIDTH // ATTN_BLOCK
    scale = HEAD_DIM ** -0.5

    def body(*refs):
        q_ref, k_ref, v_ref, qg_ref, kg_ref, lt_ref, do_ref, ti_ref, te_ref = refs[:9]
        p_refs = refs[9:9 + n_comm]
        dq_ref, dk_ref, dv_ref, dqg_ref, dkg_ref = refs[9 + n_comm:14 + n_comm]
        lb_refs = refs[14 + n_comm:14 + 2 * n_comm]
        qn_s, kn_s, vb_s, dq_s, dk_s, dv_s = refs[14 + 2 * n_comm:20 + 2 * n_comm]
        sems = refs[20 + 2 * n_comm:]

        @pl.when(pl.program_id(0) == 0)
        def _():
            dqg_ref[...] = jnp.zeros_like(dqg_ref)
            dkg_ref[...] = jnp.zeros_like(dkg_ref)
            for cp in _scatter_copies(p_refs, lb_refs, kinds, *sems) if n_comm else ():
                cp.start()

        heads = range(ATTN_HEADS_PER_STEP)
        for h in heads:
            _attn_prep(h, q_ref, k_ref, v_ref, qg_ref, kg_ref, qn_s, kn_s, vb_s, n_tiles)
        dk_s[...] = jnp.zeros_like(dk_s)
        dv_s[...] = jnp.zeros_like(dv_s)

        def q_tile(qi, carry0):
            qrows = pl.ds(pl.multiple_of(qi * T, T), T)
            qt = [qn_s[h, qrows, :] for h in heads]
            dob = [do_ref[qrows, HEAD_DIM * h:HEAD_DIM * (h + 1)].astype(BF16) for h in heads]
            lt = [lt_ref[qrows, HEAD_DIM * h:HEAD_DIM * h + 1] for h in heads]

            def tile(kj, carry, diag):
                krows = pl.ds(pl.multiple_of(kj * T, T), T)
                kt = [kn_s[h, krows, :] for h in heads]
                s = [_dot_nt(qt[h], kt[h]) for h in heads]
                dw = [_dot_nt(dob[h], vb_s[h, krows, :]) for h in heads]
                terms = [_log_terms(s[h]) for h in heads]
                lom = [terms[h][0] for h in heads]
                if diag:
                    mask = _causal_mask()
                    lom = [jnp.where(mask, lom[h], 0.0) for h in heads]
                pin = [_dot(lom[h].astype(BF16), ti_ref[...]) for h in heads]
                w = [jnp.exp(terms[h][1] + ((lt[h] - carry[h][1]) - pin[h])) for h in heads]
                if diag:
                    w = [jnp.where(mask, w[h], 0.0) for h in heads]
                e = [w[h] * dw[h] for h in heads]
                gex = [_dot(e[h].astype(BF16), te_ref[...]) for h in heads]
                dz = [e[h] - jnp.exp(terms[h][1]) * (e[h] + (carry[h][2] + gex[h])) for h in heads]
                if diag:
                    dz = [jnp.where(mask, dz[h], 0.0) for h in heads]
                new = []
                for h in heads:
                    dzb = dz[h].astype(BF16)
                    dk_s[h, krows, :] += _dot_tn(dzb, qt[h])
                    dv_s[h, krows, :] += _dot_tn(w[h].astype(BF16), dob[h])
                    new.append((carry[h][0] + _dot(dzb, kt[h]),
                                carry[h][1] + pin[h][:, T - 1:T],
                                carry[h][2] + gex[h][:, T - 1:T] + e[h][:, T - 1:T]))
                return tuple(new)

            zero = jnp.zeros((T, 1), F32)
            init = tuple((jnp.zeros((T, HEAD_DIM), F32), zero, zero) for h in heads)
            last = tile(qi, lax.fori_loop(0, qi, lambda kj, cr: tile(kj, cr, False), init), True)
            for h in heads:
                dq_s[h, qrows, :] = last[h][0]
            return carry0

        lax.fori_loop(0, n_tiles, q_tile, 0)

        def finish(i, carry):
            rows = pl.ds(pl.multiple_of(i * T, T), T)
            new = []
            for h in heads:
                lanes = slice(HEAD_DIM * h, HEAD_DIM * (h + 1))
                q = q_ref[rows, lanes]
                k = k_ref[rows, lanes]
                rq = lax.rsqrt(jnp.mean(q * q, axis=-1, keepdims=True) + EPS)
                rk = lax.rsqrt(jnp.mean(k * k, axis=-1, keepdims=True) + EPS)
                qh = q * rq
                kh = k * rk
                dqn = dq_s[h, rows, :] * scale
                dkn = -dk_s[h, rows, :]
                dqh = dqn * qg_ref[...]
                dkh = dkn * kg_ref[...]
                dq_ref[rows, lanes] = (rq * (dqh - qh * jnp.mean(dqh * qh, axis=-1, keepdims=True))).astype(BF16)
                dk_ref[rows, lanes] = (rk * (dkh - kh * jnp.mean(dkh * kh, axis=-1, keepdims=True))).astype(BF16)
                dv_ref[rows, lanes] = dv_s[h, rows, :].astype(BF16)
                new.append(carry[2 * h] + jnp.sum(dqn * qh, axis=0, keepdims=True))
                new.append(carry[2 * h + 1] + jnp.sum(dkn * kh, axis=0, keepdims=True))
            return tuple(new)

        zero = jnp.zeros((1, HEAD_DIM), F32)
        sums = lax.fori_loop(0, n_tiles, finish, (zero,) * (2 * len(heads)))
        dqg_ref[...] += sum(sums[0::2])
        dkg_ref[...] += sum(sums[1::2])
        if n_comm:
            @pl.when(pl.program_id(0) == pairs - 1)
            def _():
                for cp in _scatter_copies(p_refs, lb_refs, kinds, *sems):
                    cp.wait()

    blk = lambda off: pl.BlockSpec((S, ATTN_BLOCK), lambda p: (0, off + p))
    row64 = pl.BlockSpec((1, HEAD_DIM), lambda p: (0, 0))
    tri = pl.BlockSpec((T, T), lambda p: (0, 0))
    n_sem = n_comm * (N_CHIPS - 1)
    out = pl.pallas_call(
        body, name="attn_bwd_scatter" if n_comm else "attn_bwd", grid=(pairs,),
        in_specs=[blk(q_blk), blk(k_blk), blk(v_blk), row64, row64,
                  pl.BlockSpec((None, S, ATTN_BLOCK), lambda p: (p, 0, 0)), blk(0), tri, tri] + [ANY] * n_comm,
        out_specs=[blk(0), blk(0), blk(0), row64, row64] + [ANY] * n_comm,
        out_shape=[jax.ShapeDtypeStruct((S, ATTN_WIDTH), BF16)] * 3 + [jax.ShapeDtypeStruct((1, HEAD_DIM), F32)] * 2
        + _scatter_shapes(parts, kinds),
        scratch_shapes=[pltpu.VMEM((ATTN_HEADS_PER_STEP, S, HEAD_DIM), BF16)] * 3
        + [pltpu.VMEM((ATTN_HEADS_PER_STEP, S, HEAD_DIM), F32)] * 3
        + ([pltpu.SemaphoreType.DMA((n_sem,))] * 2 if n_comm else []),
        compiler_params=pltpu.CompilerParams(dimension_semantics=("arbitrary",), vmem_limit_bytes=VMEM_LIMIT_BYTES,
                                             has_side_effects=bool(n_comm)),
    )(proj, proj, proj, qg, kg, ltot, dmix, prefix_incl, prefix_excl, *parts)
    return out[:5], list(out[5:])


def _shifted(win, n_rows):
    return [win if b == 0 else pltpu.roll(win, n_rows - b, 0) for b in range(SUBLANES)]


def _taps(variants, offsets, tm):
    return {o: variants[o % SUBLANES][(o // SUBLANES) * SUBLANES:(o // SUBLANES) * SUBLANES + tm, :] for o in offsets}


def _fold_rows(a):
    return jnp.sum(a.reshape(a.shape[0] // SUBLANES, SUBLANES, a.shape[1]), axis=0)


def _glu_conv_fwd(proj, w, bias, *, tm=256):
    S = proj.shape[0]
    CB = LANES
    a_blk, b_blk = 3 * ATTN_WIDTH // CB, (3 * ATTN_WIDTH + CONV_WIDTH) // CB
    n_rows = tm + CONV_PAD

    def body(a_ref, b_ref, w_ref, bias_ref, c1_ref, pad_s):
        pad_s[0:CONV_PAD, :] = jnp.zeros((CONV_PAD, CB), F32)

        def fill(i, carry):
            rows = pl.ds(pl.multiple_of(i * tm, tm), tm)
            pad_s[pl.ds(pl.multiple_of(CONV_PAD + i * tm, SUBLANES), tm), :] = a_ref[rows, :] * _sigmoid(b_ref[rows, :])
            return carry

        lax.fori_loop(0, S // tm, fill, 0)

        def conv(i, carry):
            r0 = pl.multiple_of(i * tm, tm)
            taps = _taps(_shifted(pad_s[pl.ds(r0, n_rows), :], n_rows), range(2, 2 + CONV_KERNEL), tm)
            acc = jnp.broadcast_to(bias_ref[...], (tm, CB))
            for k in range(CONV_KERNEL):
                acc = acc + w_ref[k:k + 1, :] * taps[k + 2]
            c1_ref[pl.ds(r0, tm), :] = acc
            return carry

        lax.fori_loop(0, S // tm, conv, 0)

    return pl.pallas_call(
        body, name="glu_conv_fwd", grid=(CONV_WIDTH // CB,),
        in_specs=[pl.BlockSpec((S, CB), lambda j: (0, a_blk + j)), pl.BlockSpec((S, CB), lambda j: (0, b_blk + j)),
                  pl.BlockSpec((CONV_KERNEL, CB), lambda j: (0, j)), pl.BlockSpec((1, CB), lambda j: (0, j))],
        out_specs=pl.BlockSpec((S, CB), lambda j: (0, j)),
        out_shape=jax.ShapeDtypeStruct((S, CONV_WIDTH), F32),
        scratch_shapes=[pltpu.VMEM((S + CONV_PAD, CB), F32)],
        compiler_params=_params("parallel"),
    )(proj, proj, w, bias)


def _glu_conv_bwd(proj, w, dc1, *, tm=256):
    S = proj.shape[0]
    CB = LANES
    a_blk, b_blk = 3 * ATTN_WIDTH // CB, (3 * ATTN_WIDTH + CONV_WIDTH) // CB
    n_rows = tm + CONV_PAD

    def body(a_ref, b_ref, w_ref, dc1_ref, da_ref, db_ref, dw_ref, dbias_ref, pad_s, dpad_s, dw_s):
        pad_s[0:CONV_PAD, :] = jnp.zeros((CONV_PAD, CB), F32)
        dpad_s[S:S + CONV_PAD, :] = jnp.zeros((CONV_PAD, CB), F32)
        dw_s[...] = jnp.zeros_like(dw_s)

        def fill(i, carry):
            rows = pl.ds(pl.multiple_of(i * tm, tm), tm)
            pad_s[pl.ds(pl.multiple_of(CONV_PAD + i * tm, SUBLANES), tm), :] = a_ref[rows, :] * _sigmoid(b_ref[rows, :])
            dpad_s[rows, :] = dc1_ref[rows, :]
            return carry

        lax.fori_loop(0, S // tm, fill, 0)

        def conv(i, carry):
            r0 = pl.multiple_of(i * tm, tm)
            rows = pl.ds(r0, tm)
            taps = _taps(_shifted(dpad_s[pl.ds(r0, n_rows), :], n_rows), range(CONV_KERNEL), tm)
            acc = jnp.zeros((tm, CB), F32)
            for k in range(CONV_KERNEL):
                acc = acc + w_ref[k:k + 1, :] * taps[CONV_KERNEL - 1 - k]
            a = a_ref[rows, :]
            sg = _sigmoid(b_ref[rows, :])
            da_ref[rows, :] = (acc * sg).astype(BF16)
            db_ref[rows, :] = (acc * a * sg * (1.0 - sg)).astype(BF16)
            d = taps[0]
            taps = _taps(_shifted(pad_s[pl.ds(r0, n_rows), :], n_rows), range(2, 2 + CONV_KERNEL), tm)
            for k in range(CONV_KERNEL):
                dw_s[SUBLANES * k:SUBLANES * (k + 1), :] += _fold_rows(d * taps[k + 2])
            dw_s[SUBLANES * CONV_KERNEL:SUBLANES * (CONV_KERNEL + 1), :] += _fold_rows(d)
            return carry

        lax.fori_loop(0, S // tm, conv, 0)
        for k in range(CONV_KERNEL):
            dw_ref[k:k + 1, :] = jnp.sum(dw_s[SUBLANES * k:SUBLANES * (k + 1), :], axis=0, keepdims=True)
        dbias_ref[...] = jnp.sum(dw_s[SUBLANES * CONV_KERNEL:SUBLANES * (CONV_KERNEL + 1), :], axis=0, keepdims=True)

    col = lambda off: pl.BlockSpec((S, CB), lambda j: (0, off + j))
    return pl.pallas_call(
        body, name="glu_conv_bwd", grid=(CONV_WIDTH // CB,),
        in_specs=[col(a_blk), col(b_blk), pl.BlockSpec((CONV_KERNEL, CB), lambda j: (0, j)), col(0)],
        out_specs=[col(0), col(0), pl.BlockSpec((CONV_KERNEL, CB), lambda j: (0, j)), pl.BlockSpec((1, CB), lambda j: (0, j))],
        out_shape=[jax.ShapeDtypeStruct((S, CONV_WIDTH), BF16)] * 2
        + [jax.ShapeDtypeStruct((CONV_KERNEL, CONV_WIDTH), F32), jax.ShapeDtypeStruct((1, CONV_WIDTH), F32)],
        scratch_shapes=[pltpu.VMEM((S + CONV_PAD, CB), F32), pltpu.VMEM((S + CONV_PAD, CB), F32),
                        pltpu.VMEM((SUBLANES * (CONV_KERNEL + 1), CB), F32)],
        compiler_params=_params("parallel"),
    )(proj, proj, w, dc1)


def _ln_stats(c1):
    mu = jnp.mean(c1, axis=-1, keepdims=True)
    xc = c1 - mu
    r = lax.rsqrt(jnp.mean(xc * xc, axis=-1, keepdims=True) + EPS)
    return xc * r, r


def _ln_silu_fwd(c1, g, b, *, tm=512):
    S, C = c1.shape

    def body(c1_ref, g_ref, b_ref, c_ref):
        yh, _ = _ln_stats(c1_ref[...])
        y = yh * g_ref[...] + b_ref[...]
        c_ref[...] = (y * _sigmoid(y)).astype(BF16)

    vec = pl.BlockSpec((1, C), lambda i: (0, 0))
    return pl.pallas_call(
        body, name="ln_silu_fwd", grid=(S // tm,),
        in_specs=[pl.BlockSpec((tm, C), lambda i: (i, 0)), vec, vec],
        out_specs=pl.BlockSpec((tm, C), lambda i: (i, 0)),
        out_shape=jax.ShapeDtypeStruct((S, C), BF16),
        compiler_params=_params("parallel"),
    )(c1, g, b)


def _ln_silu_bwd(c1, g, b, dmix, *, tm=512):
    S, C = c1.shape

    def body(c1_ref, g_ref, b_ref, dc_ref, dc1_ref, dg_ref, db_ref):
        yh, r = _ln_stats(c1_ref[...])
        y = yh * g_ref[...] + b_ref[...]
        sg = _sigmoid(y)
        dy = dc_ref[...] * (sg * (1.0 + y * (1.0 - sg)))
        dyh = dy * g_ref[...]
        dc1_ref[...] = r * (dyh - jnp.mean(dyh, axis=-1, keepdims=True)
                            - yh * jnp.mean(dyh * yh, axis=-1, keepdims=True))

        @pl.when(pl.program_id(0) == 0)
        def _():
            dg_ref[...] = jnp.zeros_like(dg_ref)
            db_ref[...] = jnp.zeros_like(db_ref)

        dg_ref[...] += jnp.sum(dy * yh, axis=0, keepdims=True)
        db_ref[...] += jnp.sum(dy, axis=0, keepdims=True)

    vec = pl.BlockSpec((1, C), lambda i: (0, 0))
    return pl.pallas_call(
        body, name="ln_silu_bwd", grid=(S // tm,),
        in_specs=[pl.BlockSpec((tm, C), lambda i: (i, 0)), vec, vec, pl.BlockSpec((tm, C), lambda i: (i, 1))],
        out_specs=[pl.BlockSpec((tm, C), lambda i: (i, 0)), vec, vec],
        out_shape=[jax.ShapeDtypeStruct((S, C), F32), jax.ShapeDtypeStruct((1, C), F32), jax.ShapeDtypeStruct((1, C), F32)],
        compiler_params=_params("arbitrary"),
    )(c1, g, b, dmix)


FFN_CB = 256


def _ffn_gate(pad_s, w_ref, bias_ref, r0, tm):
    n_rows = tm + FFN_PAD
    taps = _taps(_shifted(pad_s[pl.ds(r0, n_rows), :], n_rows), range(FFN_PAD - 2, FFN_PAD + 1), tm)
    g1 = bias_ref[...] + w_ref[0:1, :] * taps[6] + w_ref[1:2, :] * taps[7] + w_ref[2:3, :] * taps[8]
    return g1, taps


def _ffn_act_fwd(u, w, bias, *, tm=256):
    S = u.shape[0]
    CB = FFN_CB
    nb = D_FF // CB

    def body(g_ref, v_ref, w_ref, bias_ref, o_ref, pad_s):
        pad_s[0:FFN_PAD, :] = jnp.zeros((FFN_PAD, CB), F32)

        def fill(i, carry):
            pad_s[pl.ds(pl.multiple_of(FFN_PAD + i * tm, SUBLANES), tm), :] = g_ref[pl.ds(pl.multiple_of(i * tm, tm), tm), :]
            return carry

        lax.fori_loop(0, S // tm, fill, 0)

        def act(i, carry):
            r0 = pl.multiple_of(i * tm, tm)
            g1, _ = _ffn_gate(pad_s, w_ref, bias_ref, r0, tm)
            o_ref[pl.ds(r0, tm), :] = (g1 * _sigmoid(g1) * v_ref[pl.ds(r0, tm), :]).astype(BF16)
            return carry

        lax.fori_loop(0, S // tm, act, 0)

    return pl.pallas_call(
        body, name="ffn_act_fwd", grid=(nb,),
        in_specs=[pl.BlockSpec((S, CB), lambda j: (0, j)), pl.BlockSpec((S, CB), lambda j: (0, nb + j)),
                  pl.BlockSpec((FFN_KERNEL, CB), lambda j: (0, j)), pl.BlockSpec((1, CB), lambda j: (0, j))],
        out_specs=pl.BlockSpec((S, CB), lambda j: (0, j)),
        out_shape=jax.ShapeDtypeStruct((S, D_FF), BF16),
        scratch_shapes=[pltpu.VMEM((S + FFN_PAD, CB), F32)],
        compiler_params=_params("parallel"),
    )(u, u, w, bias)


def _ffn_act_bwd(u, w, bias, dact, *, tm=256):
    S = u.shape[0]
    CB = FFN_CB
    nb = D_FF // CB

    def body(g_ref, v_ref, w_ref, bias_ref, da_ref, dg_ref, dv_ref, dw_ref, dbias_ref, pad_s, dpad_s, dw_s):
        pad_s[0:FFN_PAD, :] = jnp.zeros((FFN_PAD, CB), F32)
        dpad_s[S:S + FFN_PAD, :] = jnp.zeros((FFN_PAD, CB), F32)
        dw_s[...] = jnp.zeros_like(dw_s)

        def fill(i, carry):
            pad_s[pl.ds(pl.multiple_of(FFN_PAD + i * tm, SUBLANES), tm), :] = g_ref[pl.ds(pl.multiple_of(i * tm, tm), tm), :]
            return carry

        lax.fori_loop(0, S // tm, fill, 0)

        def first(i, carry):
            r0 = pl.multiple_of(i * tm, tm)
            rows = pl.ds(r0, tm)
            g1, taps = _ffn_gate(pad_s, w_ref, bias_ref, r0, tm)
            sg = _sigmoid(g1)
            da = da_ref[rows, :].astype(F32)
            dv_ref[rows, :] = (da * g1 * sg).astype(BF16)
            dg1 = da * v_ref[rows, :] * (sg * (1.0 + g1 * (1.0 - sg)))
            dpad_s[rows, :] = dg1
            for k in range(FFN_KERNEL):
                dw_s[SUBLANES * k:SUBLANES * (k + 1), :] += _fold_rows(dg1 * taps[FFN_PAD - 2 + k])
            dw_s[SUBLANES * FFN_KERNEL:SUBLANES * (FFN_KERNEL + 1), :] += _fold_rows(dg1)
            return carry

        lax.fori_loop(0, S // tm, first, 0)

        def second(i, carry):
            r0 = pl.multiple_of(i * tm, tm)
            n_rows = tm + FFN_PAD
            taps = _taps(_shifted(dpad_s[pl.ds(r0, n_rows), :], n_rows), range(FFN_KERNEL), tm)
            dg_ref[pl.ds(r0, tm), :] = (w_ref[2:3, :] * taps[0] + w_ref[1:2, :] * taps[1] + w_ref[0:1, :] * taps[2]).astype(BF16)
            return carry

        lax.fori_loop(0, S // tm, second, 0)
        for k in range(FFN_KERNEL):
            dw_ref[k:k + 1, :] = jnp.sum(dw_s[SUBLANES * k:SUBLANES * (k + 1), :], axis=0, keepdims=True)
        dbias_ref[...] = jnp.sum(dw_s[SUBLANES * FFN_KERNEL:SUBLANES * (FFN_KERNEL + 1), :], axis=0, keepdims=True)

    col = lambda off: pl.BlockSpec((S, CB), lambda j: (0, off + j))
    wspec = pl.BlockSpec((FFN_KERNEL, CB), lambda j: (0, j))
    bspec = pl.BlockSpec((1, CB), lambda j: (0, j))
    return pl.pallas_call(
        body, name="ffn_act_bwd", grid=(nb,),
        in_specs=[col(0), col(nb), wspec, bspec, col(0)],
        out_specs=[col(0), col(0), wspec, bspec],
        out_shape=[jax.ShapeDtypeStruct((S, D_FF), BF16)] * 2
        + [jax.ShapeDtypeStruct((FFN_KERNEL, D_FF), F32), jax.ShapeDtypeStruct((1, D_FF), F32)],
        scratch_shapes=[pltpu.VMEM((S + FFN_PAD, CB), F32), pltpu.VMEM((S + FFN_PAD, CB), F32),
                        pltpu.VMEM((SUBLANES * (FFN_KERNEL + 1), CB), F32)],
        compiler_params=_params("parallel"),
    )(u, u, w, bias, dact)


def _loss_grad(y, target, *, tm=512):
    S, D = y.shape

    def body(y_ref, t_ref, dy_ref, l_ref):
        d = y_ref[...] - t_ref[...]
        dy_ref[...] = d * (1.0 / D)

        @pl.when(pl.program_id(0) == 0)
        def _():
            l_ref[...] = jnp.zeros_like(l_ref)

        l_ref[...] += 0.5 * jnp.sum(jnp.mean(d * d, axis=-1, keepdims=True), axis=0, keepdims=True)

    dy, l = pl.pallas_call(
        body, name="loss_grad", grid=(S // tm,),
        in_specs=[pl.BlockSpec((tm, D), lambda i: (i, 0))] * 2,
        out_specs=[pl.BlockSpec((tm, D), lambda i: (i, 0)), pl.BlockSpec((SUBLANES, LANES), lambda i: (0, 0))],
        out_shape=[jax.ShapeDtypeStruct((S, D), F32), jax.ShapeDtypeStruct((SUBLANES, LANES), F32)],
        compiler_params=_params("arbitrary"),
    )(y, target)
    return dy, l[0, 0]


def _row_tile(rows, cap=512):
    t = min(rows, cap)
    while rows % t or t % SUBLANES:
        t -= 1
    return t


def _adam_update(w, g, m, v):
    m1 = ADAM_B1 * m + (1.0 - ADAM_B1) * g
    v1 = ADAM_B2 * v + (1.0 - ADAM_B2) * (g * g)
    m_hat = m1 / (1.0 - ADAM_B1 ** ADAM_STEP)
    v_hat = v1 / (1.0 - ADAM_B2 ** ADAM_STEP)
    return -ADAM_LR * (m_hat / (jnp.sqrt(v_hat) + ADAM_EPS) + ADAM_WD * w), m1, v1


def _adamw(w, g, m, v):
    R, C = w.shape
    tr = _row_tile(R, 256)

    def body(w_ref, g_ref, m_ref, v_ref, d_ref, nm_ref, nv_ref):
        d_ref[...], nm_ref[...], nv_ref[...] = _adam_update(w_ref[...], g_ref[...], m_ref[...], v_ref[...])

    spec = pl.BlockSpec((tr, C), lambda i: (i, 0))
    return pl.pallas_call(
        body, name="adamw", grid=(R // tr,),
        in_specs=[spec] * 4, out_specs=[spec] * 3,
        out_shape=[jax.ShapeDtypeStruct((R, C), F32)] * 3,
        compiler_params=_params("parallel"),
    )(w, g, m, v)


def _adamw_layer(layer, w, g, m, v, prev):
    L, R, C = w.shape
    tr = _row_tile(R, 256)
    n_prev = 4 if prev else 0

    def body(*refs):
        w_ref, g_ref, m_ref, v_ref = refs[:4]
        go_ref, d_ref, nm_ref, nv_ref = refs[4 + n_prev:8 + n_prev]
        gv = g_ref[...]
        go_ref[...] = gv
        d_ref[...], nm_ref[...], nv_ref[...] = _adam_update(w_ref[...], gv, m_ref[...], v_ref[...])

    stacked = pl.BlockSpec((None, tr, C), lambda i: (layer, i, 0))
    return pl.pallas_call(
        body, name="adamw_layer", grid=(R // tr,),
        in_specs=[stacked, pl.BlockSpec((tr, C), lambda i: (i, 0)), stacked, stacked] + [ANY] * n_prev,
        out_specs=[stacked] * 4, out_shape=[jax.ShapeDtypeStruct((L, R, C), F32)] * 4,
        input_output_aliases={4 + j: j for j in range(n_prev)},
        compiler_params=_params("parallel"),
    )(w, g, m, v, *(prev or ()))


def _cast_into_full(w, layer, kind, chip):
    _, R, C = w.shape
    tr = _row_tile(R, 512)

    def body(chip_ref, w_ref, o_ref):
        o_ref[...] = w_ref[...].astype(BF16)

    if kind == "col":
        out_shape = jax.ShapeDtypeStruct((1, R, N_CHIPS * C), BF16)
        out_spec = pl.BlockSpec((None, tr, C), lambda i, chip_ref: (0, i, chip_ref[0]))
    else:
        out_shape = jax.ShapeDtypeStruct((1, N_CHIPS * R, C), BF16)
        out_spec = pl.BlockSpec((None, tr, C), lambda i, chip_ref: (0, chip_ref[0] * (R // tr) + i, 0))
    return pl.pallas_call(
        body, name="cast_into_full_" + kind,
        grid_spec=pltpu.PrefetchScalarGridSpec(
            num_scalar_prefetch=1, grid=(R // tr,),
            in_specs=[pl.BlockSpec((None, tr, C), lambda i, chip_ref: (layer, i, 0))], out_specs=out_spec),
        out_shape=out_shape, compiler_params=_params("parallel"),
    )(chip, w)


def _add_half(g4, la, c):
    L, _, H, W = g4.shape
    th = _row_tile(H, 256)

    def body(c_ref, g_ref, la_ref, o_ref):
        o_ref[...] = (g_ref[...].astype(F32) + la_ref[...].astype(F32)).astype(BF16)

    return pl.pallas_call(
        body, name="add_half",
        grid_spec=pltpu.PrefetchScalarGridSpec(
            num_scalar_prefetch=1, grid=(L, H // th),
            in_specs=[pl.BlockSpec((None, None, th, W), lambda l, i, c_ref: (l, c_ref[0], i, 0)),
                      pl.BlockSpec((None, th, W), lambda l, i, c_ref: (l, i, 0))],
            out_specs=pl.BlockSpec((None, th, W), lambda l, i, c_ref: (l, i, 0))),
        out_shape=jax.ShapeDtypeStruct((L, H, W), BF16),
        compiler_params=_params("parallel", "parallel"),
    )(c, g4, la)


def _add_parts(p, lb, place, kind):
    _, L, H, C = lb.shape
    th = _row_tile(H, 256)

    def body(s_ref, p_ref, lb_ref, o_ref):
        acc = p_ref[...].astype(F32)
        for k in range(N_CHIPS - 1):
            acc = acc + lb_ref[k].astype(F32)
        o_ref[...] = acc

    if kind == "col":
        p_spec = pl.BlockSpec((None, th, C), lambda l, i, s_ref: (l, i, s_ref[0]))
    else:
        p_spec = pl.BlockSpec((None, None, th, C), lambda l, i, s_ref: (l, s_ref[0], i, 0))
    return pl.pallas_call(
        body, name="add_parts_" + kind,
        grid_spec=pltpu.PrefetchScalarGridSpec(
            num_scalar_prefetch=1, grid=(L, H // th),
            in_specs=[p_spec, pl.BlockSpec((N_CHIPS - 1, None, th, C), lambda l, i, s_ref: (0, l, i, 0))],
            out_specs=pl.BlockSpec((None, None, th, C), lambda l, i, s_ref: (l, s_ref[1], i, 0))),
        out_shape=jax.ShapeDtypeStruct((L, 2, H, C), F32),
        compiler_params=_params("parallel", "parallel"),
    )(place, p, lb)


ANY = pl.BlockSpec(memory_space=pl.ANY)


def _place():
    x, y, c = lax.axis_index("x"), lax.axis_index("y"), lax.axis_index("c")
    chips = [(1 - x, y), (x, 1 - y), (1 - x, 1 - y)]
    return x, y, c, chips


def _comm_call(body, name, ins, out_shape, n_remote, n_local, aliases=None):
    scratch = [pltpu.SemaphoreType.DMA((n_remote,)), pltpu.SemaphoreType.DMA((n_remote,))]
    if n_local:
        scratch.append(pltpu.SemaphoreType.DMA((n_local,)))
    return pl.pallas_call(
        body, name=name, in_specs=[ANY] * len(ins), out_specs=[ANY] * len(out_shape), out_shape=out_shape,
        scratch_shapes=scratch, input_output_aliases=aliases or {},
        compiler_params=pltpu.CompilerParams(has_side_effects=True),
    )(*ins)


def _remote(src, dst, send, recv, k, to):
    return pltpu.make_async_remote_copy(src_ref=src, dst_ref=dst, send_sem=send.at[k], recv_sem=recv.at[k],
                                        device_id=to, device_id_type=MESH)


def _gather_weights(fulls, kinds):
    n = len(fulls)
    out_shape = [jax.ShapeDtypeStruct(f.shape, f.dtype) for f in fulls]

    def body(*refs):
        outs, (send, recv) = refs[n:2 * n], refs[2 * n:]
        first = _gather_copies(outs, kinds, send, recv, 0)
        for cp in first:
            cp.start()
        for cp in first:
            cp.wait()
        passed = _pass_on_copies(outs, kinds, send, recv, len(first))
        for cp in passed:
            cp.start()
        for cp in passed:
            cp.wait()

    return _comm_call(body, "gather_weights", fulls, out_shape, 2 * n * (N_CHIPS - 1), 0, {a: a for a in range(n)})


def _window(ref, kind, s, h):
    if kind == "col":
        H, C = ref.shape[1] // 2, ref.shape[2] // N_CHIPS
        return ref.at[:, pl.ds(pl.multiple_of(h * H, 16), H), pl.ds(pl.multiple_of(s * C, LANES), C)]
    R = ref.shape[1] // N_CHIPS
    return ref.at[:, pl.ds(pl.multiple_of(s * R + h * (R // 2), 16), R // 2), :]


def _gather_copies(outs, kinds, send, recv, sem0):
    x, y, c, chips = _place()
    me = 2 * x + y
    return [_remote(_window(o, kind, me, c), _window(o, kind, me, c), send, recv, sem0 + a * (N_CHIPS - 1) + k, (*chip, c))
            for a, (o, kind) in enumerate(zip(outs, kinds)) for k, chip in enumerate(chips)]


def _pass_on_copies(outs, kinds, send, recv, sem0):
    x, y, c, chips = _place()
    cps = []
    for a, (o, kind) in enumerate(zip(outs, kinds)):
        for k, chip in enumerate(chips):
            landed = _window(o, kind, 2 * chip[0] + chip[1], c)
            cps.append(_remote(landed, landed, send, recv, sem0 + a * (N_CHIPS - 1) + k, (x, y, 1 - c)))
    return cps


def _pass_on(fulls, kinds):
    n = len(fulls)
    out_shape = [jax.ShapeDtypeStruct(f.shape, f.dtype) for f in fulls]

    def body(*refs):
        outs, (send, recv) = refs[n:2 * n], refs[2 * n:]
        cps = _pass_on_copies(outs, kinds, send, recv, 0)
        for cp in cps:
            cp.start()
        for cp in cps:
            cp.wait()

    return _comm_call(body, "pass_on", fulls, out_shape, n * (N_CHIPS - 1), 0, {a: a for a in range(n)})


def _gather_small(shards):
    n = len(shards)
    out_shape = [jax.ShapeDtypeStruct((N_CHIPS,) + s.shape, s.dtype) for s in shards]

    def body(*refs):
        srcs, outs, (send, recv, loc) = refs[:n], refs[n:2 * n], refs[2 * n:]
        x, y, c, chips = _place()
        me = 2 * x + y
        remote, local = [], []
        for a in range(n):
            local.append(pltpu.make_async_copy(srcs[a], outs[a].at[me], loc.at[a]))
            for k, chip in enumerate(chips):
                remote.append(_remote(srcs[a], outs[a].at[me], send, recv, a * (N_CHIPS - 1) + k, (*chip, c)))
        for cp in local + remote:
            cp.start()
        for cp in remote + local:
            cp.wait()

    return _comm_call(body, "gather_small", shards, out_shape, n * (N_CHIPS - 1), n)


def _exchange_halves(g4s):
    n = len(g4s)
    out_shape = [jax.ShapeDtypeStruct((g.shape[0],) + g.shape[2:], g.dtype) for g in g4s]

    def body(*refs):
        gs, las, (send, recv) = refs[:n], refs[n:2 * n], refs[2 * n:]
        x, y, c, _ = _place()
        cps = [_remote(gs[a].at[:, 1 - c], las[a], send, recv, a, (x, y, 1 - c)) for a in range(n)]
        for cp in cps:
            cp.start()
        for cp in cps:
            cp.wait()

    return _comm_call(body, "exchange_halves", g4s, out_shape, n, 0)


def _scatter_partials(ps, kinds):
    n = len(ps)

    def body(*refs):
        srcs, lbs, (send, recv) = refs[:n], refs[n:2 * n], refs[2 * n:]
        cps = _scatter_copies(srcs, lbs, kinds, send, recv)
        for cp in cps:
            cp.start()
        for cp in cps:
            cp.wait()

    return _comm_call(body, "scatter_partials", ps, _scatter_shapes(ps, kinds), n * (N_CHIPS - 1), 0)


def _scatter_shapes(ps, kinds):
    out_shape = []
    for p, kind in zip(ps, kinds):
        L, H, C = (p.shape[0], p.shape[1], p.shape[2] // N_CHIPS) if kind == "col" else (p.shape[0], p.shape[2], p.shape[3])
        out_shape.append(jax.ShapeDtypeStruct((N_CHIPS - 1, L, H, C), p.dtype))
    return out_shape


def _scatter_copies(srcs, lbs, kinds, send, recv):
    x, y, c, chips = _place()
    cps = []
    for a, (src, lb, kind) in enumerate(zip(srcs, lbs, kinds)):
        C = lb.shape[3]
        for k, chip in enumerate(chips):
            s = 2 * chip[0] + chip[1]
            part = src.at[:, :, pl.ds(pl.multiple_of(s * C, LANES), C)] if kind == "col" else src.at[:, s]
            cps.append(_remote(part, lb.at[k], send, recv, a * (N_CHIPS - 1) + k, (*chip, c)))
    return cps


def _share_halves(g4s):
    n = len(g4s)
    out_shape = [jax.ShapeDtypeStruct(g.shape, g.dtype) for g in g4s]

    def body(*refs):
        outs, (send, recv) = refs[n:2 * n], refs[2 * n:]
        x, y, c, _ = _place()
        cps = [_remote(outs[a].at[:, c], outs[a].at[:, c], send, recv, a, (x, y, 1 - c)) for a in range(n)]
        for cp in cps:
            cp.start()
        for cp in cps:
            cp.wait()

    return _comm_call(body, "share_halves", g4s, out_shape, n, 0, {a: a for a in range(n)})


def _allreduce_small(part):
    N = part.shape[0]

    def body(p_ref, o_ref, buf, send, recv):
        x, y, c, _ = _place()
        me = 4 * x + 2 * y + c
        buf[me] = p_ref[...]
        cps = []
        for k in range(1, N_DEV):
            peer = (1 - x if k & 4 else x, 1 - y if k & 2 else y, 1 - c if k & 1 else c)
            cps.append(_remote(p_ref, buf.at[me], send, recv, k - 1, peer))
        for cp in cps:
            cp.start()
        for cp in cps:
            cp.wait()
        acc = buf[0]
        for i in range(1, N_DEV):
            acc = acc + buf[i]
        o_ref[...] = acc

    vmem = pl.BlockSpec(memory_space=pltpu.VMEM)
    return pl.pallas_call(
        body, name="allreduce_small", in_specs=[vmem], out_specs=vmem,
        out_shape=jax.ShapeDtypeStruct((N, LANES), F32),
        scratch_shapes=[pltpu.VMEM((N_DEV, N, LANES), F32), pltpu.SemaphoreType.DMA((N_DEV - 1,)),
                        pltpu.SemaphoreType.DMA((N_DEV - 1,))],
        compiler_params=pltpu.CompilerParams(has_side_effects=True, vmem_limit_bytes=VMEM_LIMIT_BYTES),
    )(part)


AFTER_ATTENTION = ("w_out", "w_up", "w_down")


def _layer_fwd(x, p, fulls=(), kinds=()):
    h1, proj = _norm_matmul(x, p["norm1_g"], p["w_in"])
    attn, ltot, filled = _attn_fwd(proj, p["q_norm_g"], p["k_norm_g"], fulls, kinds)
    if filled:
        filled = _pass_on(filled, kinds)
        p = dict(p, **{n: f[0] for n, f in zip(AFTER_ATTENTION, filled)})
    c1 = _glu_conv_fwd(proj, p["conv_dw_w"], p["conv_dw_b"])
    c = _ln_silu_fwd(c1, p["conv_ln_g"], p["conv_ln_b"])
    x_mid = _matmul_res([attn, c], p["w_out"], x)
    h2, u = _norm_matmul(x_mid, p["norm2_g"], p["w_up"])
    act = _ffn_act_fwd(u, p["ffn_dw_w"], p["ffn_dw_b"])
    x_out = _matmul_res([act], p["w_down"], x_mid)
    saved = dict(x=x, h1=h1, proj=proj, attn=attn, ltot=ltot, c1=c1, c=c, x_mid=x_mid, h2=h2, u=u, act=act)
    return x_out, saved, filled


def _chip_partials(gs, kinds, core):
    g4s = []
    for g, kind in zip(gs, kinds):
        if kind == "col":
            g4s.append(g.reshape(1, 2, g.shape[0] // 2, g.shape[1]))
        else:
            g4s.append(g.reshape(N_CHIPS, 2, g.shape[0] // N_CHIPS // 2, g.shape[1]))
    parts = [_add_half(g4, la, core) for g4, la in zip(g4s, _exchange_halves(g4s))]
    return [p if kind == "col" else p.reshape(1, N_CHIPS, p.shape[1], p.shape[2]) for p, kind in zip(parts, kinds)]


def _owned_sums(parts, landed, kinds, place):
    halves = [_add_parts(p, lb, place, kind) for p, lb, kind in zip(parts, landed, kinds)]
    return [g.reshape(2 * g.shape[2], g.shape[3]) for g in _share_halves(halves)]


def _layer_bwd(dx_out, s, p, comm=None):
    g = {}
    dact = _matmul_nt([dx_out], p["w_down"], BF16)
    g["w_down"] = _matmul_tn(s["act"], dx_out, tk=D_FF // 2, tn=D_MODEL)
    dgate, dval, g["ffn_dw_w"], g["ffn_dw_b"] = _ffn_act_bwd(s["u"], p["ffn_dw_w"], p["ffn_dw_b"], dact)
    dx_mid, g["norm2_g"] = _matmul_nt_rmsbwd([dgate, dval], p["w_up"], s["x_mid"], p["norm2_g"], dx_out)
    g["w_up"] = jnp.concatenate([_matmul_tn(s["h2"], d, tk=D_MODEL, tn=D_FF // 2) for d in (dgate, dval)], axis=1)
    dmix = _matmul_nt([dx_mid], p["w_out"], F32)
    g["w_out"] = jnp.concatenate([_matmul_tn(m, dx_mid, tk=ATTN_WIDTH, tn=D_MODEL) for m in (s["attn"], s["c"])], axis=0)
    dc1, g["conv_ln_g"], g["conv_ln_b"] = _ln_silu_bwd(s["c1"], p["conv_ln_g"], p["conv_ln_b"], dmix)
    da, db, g["conv_dw_w"], g["conv_dw_b"] = _glu_conv_bwd(s["proj"], p["conv_dw_w"], dc1)
    parts, kinds, sums = [], [], []
    if comm is not None:
        core, place, pending = comm
        kinds = [BIG_KIND[n] for n in AFTER_ATTENTION] + ["col"] * len(pending)
        parts = _chip_partials([g[n] for n in AFTER_ATTENTION] + list(pending), kinds, core)
    (dq, dk, dv, g["q_norm_g"], g["k_norm_g"]), landed = _attn_bwd(
        s["proj"], p["q_norm_g"], p["k_norm_g"], s["ltot"], dmix, parts, kinds)
    if comm is not None:
        sums = _owned_sums(parts, landed, kinds, place)
    pieces = [dq, dk, dv, da, db]
    dx, g["norm1_g"] = _matmul_nt_rmsbwd(pieces, p["w_in"], s["x"], p["norm1_g"], dx_mid)
    g["w_in"] = jnp.concatenate([_matmul_tn(s["h1"], d, tk=D_MODEL, tn=ATTN_WIDTH) for d in pieces], axis=1)
    return dx, g, sums


WEIGHTS = ("norm1_g", "w_in", "q_norm_g", "k_norm_g", "conv_dw_w", "conv_dw_b", "conv_ln_g", "conv_ln_b",
           "w_out", "norm2_g", "w_up", "ffn_dw_w", "ffn_dw_b", "w_down")
BIG = ("w_in", "w_out", "w_up", "w_down")
BIG_KIND = {"w_in": "col", "w_out": "row", "w_up": "col", "w_down": "row"}
SMALL_SHARDED = ("conv_dw_w", "ffn_dw_w")
REPLICATED = tuple(n for n in WEIGHTS if n not in BIG + SMALL_SHARDED)


def _pack(arrays):
    flat = jnp.concatenate([a.reshape(-1) for a in arrays])
    rows = -(-flat.shape[0] // (SUBLANES * LANES)) * SUBLANES
    return jnp.pad(flat, (0, rows * LANES - flat.shape[0])).reshape(rows, LANES)


def _unpack(packed, shapes):
    flat = packed.reshape(-1)
    out, off = [], 0
    for shape in shapes:
        size = 1
        for d in shape:
            size *= d
        out.append(flat[off:off + size].reshape(shape))
        off += size
    return out


def _unshard_last(stacked):
    n, L, K, C = stacked.shape
    return jnp.transpose(stacked, (1, 2, 0, 3)).reshape(L, K, n * C)


def kernel(x, norm1_g, w_in, q_norm_g, k_norm_g, conv_dw_w, conv_dw_b, conv_ln_g, conv_ln_b, w_out, norm2_g, w_up, ffn_dw_w, ffn_dw_b, w_down, loss_target, m_norm1_g, m_w_in, m_q_norm_g, m_k_norm_g, m_conv_dw_w, m_conv_dw_b, m_conv_ln_g, m_conv_ln_b, m_w_out, m_norm2_g, m_w_up, m_ffn_dw_w, m_ffn_dw_b, m_w_down, v_norm1_g, v_w_in, v_q_norm_g, v_k_norm_g, v_conv_dw_w, v_conv_dw_b, v_conv_ln_g, v_conv_ln_b, v_w_out, v_norm2_g, v_w_up, v_ffn_dw_w, v_ffn_dw_b, v_w_down):
    given = dict(locals())
    w = {n: given[n] for n in WEIGHTS}
    m = {n: given["m_" + n] for n in WEIGHTS}
    v = {n: given["v_" + n] for n in WEIGHTS}
    chip = 2 * lax.axis_index("x") + lax.axis_index("y")
    core = lax.axis_index("c")
    chip_arr = jnp.reshape(chip, (1,)).astype(jnp.int32)
    core_arr = jnp.reshape(core, (1,)).astype(jnp.int32)
    L = DEPTH

    place = jnp.concatenate([chip_arr, core_arr])

    full = [{n: _cast_into_full(w[n], l, BIG_KIND[n], chip_arr) for n in BIG} for l in range(L)]
    full[0]["w_in"] = _gather_weights([full[0]["w_in"]], ["col"])[0]
    small_full = {n: _unshard_last(stacked)
                  for n, stacked in zip(SMALL_SHARDED, _gather_small([w[n] for n in SMALL_SHARDED]))}
    params = []
    for l in range(L):
        p = {n: small_full[n][l] for n in SMALL_SHARDED}
        p.update({n: w[n][l][None] for n in REPLICATED})
        params.append(p)

    act = x[0]
    saved = []
    for l in range(L):
        group = [(l, n) for n in AFTER_ATTENTION] + ([(l + 1, "w_in")] if l + 1 < L else [])
        act, s, filled = _layer_fwd(act, dict(params[l], w_in=full[l]["w_in"][0]),
                                    [full[i][n] for i, n in group], [BIG_KIND[n] for _, n in group])
        for (i, n), f in zip(group, filled):
            full[i][n] = f
        saved.append(s)
    for l in range(L):
        params[l].update({n: full[l][n][0] for n in BIG})
    dx, loss_part = _loss_grad(act, loss_target[0])
    loss = lax.psum(loss_part, ("x", "y", "c"))

    grads = [None] * L
    summed = {}
    pending = []
    for l in reversed(range(L)):
        dx, grads[l], sums = _layer_bwd(dx, saved[l], params[l], (core_arr, place, pending))
        names = [(l, n) for n in AFTER_ATTENTION] + ([(l + 1, "w_in")] if pending else [])
        summed.update(zip(names, sums))
        pending = [grads[l]["w_in"]]
    parts = _chip_partials(pending, ["col"], core_arr)
    summed[0, "w_in"] = _owned_sums(parts, _scatter_partials(parts, ["col"]), ["col"], place)[0]

    grad, delta, new_m, new_v = {}, {}, {}, {}
    for n in BIG:
        out = None
        for l in range(L):
            out = _adamw_layer(l, w[n], summed[l, n], m[n], v[n], out)
        grad[n], delta[n], new_m[n], new_v[n] = out

    small = REPLICATED + SMALL_SHARDED
    small_grads = [jnp.stack([grads[l][n] for l in range(L)]) for n in small]
    small_sums = _unpack(_allreduce_small(_pack(small_grads)), [a.shape for a in small_grads])
    for n, g in zip(small, small_sums):
        if n in REPLICATED:
            grad[n] = g.reshape(w[n].shape)
        else:
            width = w[n].shape[2]
            grad[n] = lax.dynamic_slice_in_dim(g, chip * width, width, axis=2)
    shapes = [w[n].shape for n in small]
    packed = _adamw(*[_pack([src[n] for n in small]) for src in (w, grad, m, v)])
    for out, pk in zip((delta, new_m, new_v), packed):
        out.update(zip(small, _unpack(pk, shapes)))

    return (loss, dx[None], *[grad[n] for n in WEIGHTS], *[delta[n] for n in WEIGHTS],
            *[new_m[n] for n in WEIGHTS], *[new_v[n] for n in WEIGHTS])
```

```python
import functools

import jax
import jax.numpy as jnp
from jax import lax
from jax.experimental import pallas as pl
from jax.experimental.pallas import tpu as pltpu

F32 = jnp.float32
BF16 = jnp.bfloat16

DEPTH = 4
D_MODEL = 1024
HEADS = 8
HEAD_DIM = 64
ATTN_WIDTH = HEADS * HEAD_DIM
CONV_WIDTH = D_MODEL - ATTN_WIDTH
CONV_KERNEL = 31
D_FF = 2816
FFN_KERNEL = 3
EPS = 1e-6
ADAM_LR, ADAM_B1, ADAM_B2, ADAM_EPS, ADAM_WD, ADAM_STEP = 0.001, 0.9, 0.999, 1e-08, 0.01, 10

N_CHIPS = 4
N_DEV = 8
LANES = 128
SUBLANES = 8
VMEM_LIMIT_BYTES = 56 * 2**20
ATTN_TILE = 256
ATTN_HEADS_PER_STEP = 4
ATTN_BLOCK = ATTN_HEADS_PER_STEP * HEAD_DIM
CONV_PAD = 32
FFN_PAD = 8
MESH = pl.DeviceIdType.MESH


def _params(*sem):
    return pltpu.CompilerParams(dimension_semantics=sem if sem else None, vmem_limit_bytes=VMEM_LIMIT_BYTES)


def _dot(a, b):
    return jnp.dot(a, b, preferred_element_type=F32)


def _dot_nt(a, b):
    return lax.dot_general(a, b, (((1,), (1,)), ((), ())), preferred_element_type=F32)


def _dot_tn(a, b):
    return lax.dot_general(a, b, (((0,), (0,)), ((), ())), preferred_element_type=F32)


def _sigmoid(x):
    return 1.0 / (1.0 + jnp.exp(-x))


def _norm_matmul(x, g, w, *, tm=256, n_split=1):
    S, D = x.shape
    N = w.shape[1]
    tn = N // n_split

    def body(x_ref, g_ref, w_ref, h_ref, y_ref):
        @pl.when(pl.program_id(1) == 0)
        def _():
            xv = x_ref[...]
            r = lax.rsqrt(jnp.mean(xv * xv, axis=-1, keepdims=True) + EPS)
            h_ref[...] = (xv * r * g_ref[...]).astype(BF16)

        y_ref[...] = _dot(h_ref[...], w_ref[...])

    return pl.pallas_call(
        body, name="norm_matmul", grid=(S // tm, n_split),
        in_specs=[pl.BlockSpec((tm, D), lambda i, j: (i, 0)),
                  pl.BlockSpec((1, D), lambda i, j: (0, 0)),
                  pl.BlockSpec((D, tn), lambda i, j: (0, j))],
        out_specs=[pl.BlockSpec((tm, D), lambda i, j: (i, 0)),
                   pl.BlockSpec((tm, tn), lambda i, j: (i, j))],
        out_shape=[jax.ShapeDtypeStruct((S, D), BF16), jax.ShapeDtypeStruct((S, N), F32)],
        compiler_params=_params("parallel", "arbitrary"),
    )(x, g, w)


def _matmul_res(pieces, w, res, *, tm=512):
    S, N = res.shape
    K = w.shape[0]
    widths = [p.shape[1] for p in pieces]
    assert sum(widths) == K

    def body(*refs):
        p_refs, (w_ref, res_ref, o_ref) = refs[:len(pieces)], refs[len(pieces):]
        acc = res_ref[...]
        off = 0
        for p_ref, kp in zip(p_refs, widths):
            acc = acc + _dot(p_ref[...], w_ref[off:off + kp, :])
            off += kp
        o_ref[...] = acc

    return pl.pallas_call(
        body, name="matmul_res", grid=(S // tm,),
        in_specs=[pl.BlockSpec((tm, kp), lambda i: (i, 0)) for kp in widths]
        + [pl.BlockSpec((K, N), lambda i: (0, 0)), pl.BlockSpec((tm, N), lambda i: (i, 0))],
        out_specs=pl.BlockSpec((tm, N), lambda i: (i, 0)),
        out_shape=jax.ShapeDtypeStruct((S, N), F32),
        compiler_params=_params("parallel"),
    )(*pieces, w, res)


def _nt_sum(p_refs, widths, w_ref):
    acc = None
    off = 0
    for p_ref, n_p in zip(p_refs, widths):
        d = _dot_nt(p_ref[...].astype(BF16), w_ref[:, off:off + n_p])
        acc = d if acc is None else acc + d
        off += n_p
    return acc


def _matmul_nt(pieces, w, out_dtype, *, tm=512):
    S = pieces[0].shape[0]
    K, N = w.shape
    widths = [p.shape[1] for p in pieces]
    assert sum(widths) == N

    def body(*refs):
        p_refs, (w_ref, o_ref) = refs[:len(pieces)], refs[len(pieces):]
        o_ref[...] = _nt_sum(p_refs, widths, w_ref).astype(out_dtype)

    return pl.pallas_call(
        body, name="matmul_nt", grid=(S // tm,),
        in_specs=[pl.BlockSpec((tm, n_p), lambda i: (i, 0)) for n_p in widths]
        + [pl.BlockSpec((K, N), lambda i: (0, 0))],
        out_specs=pl.BlockSpec((tm, K), lambda i: (i, 0)),
        out_shape=jax.ShapeDtypeStruct((S, K), out_dtype),
        compiler_params=_params("parallel"),
    )(*pieces, w)


def _matmul_nt_rmsbwd(pieces, w, x, g, dres, *, tm=256):
    S, K = x.shape
    N = w.shape[1]
    widths = [p.shape[1] for p in pieces]
    assert sum(widths) == N

    def body(*refs):
        p_refs, (w_ref, x_ref, g_ref, dres_ref, dx_ref, dg_ref) = refs[:len(pieces)], refs[len(pieces):]
        dh = _nt_sum(p_refs, widths, w_ref)
        xv = x_ref[...]
        r = lax.rsqrt(jnp.mean(xv * xv, axis=-1, keepdims=True) + EPS)
        xh = xv * r
        dxh = dh * g_ref[...]
        dx_ref[...] = dres_ref[...] + r * (dxh - xh * jnp.mean(dxh * xh, axis=-1, keepdims=True))

        @pl.when(pl.program_id(0) == 0)
        def _():
            dg_ref[...] = jnp.zeros_like(dg_ref)

        dg_ref[...] += jnp.sum(dh * xh, axis=0, keepdims=True)

    return pl.pallas_call(
        body, name="matmul_nt_rmsbwd", grid=(S // tm,),
        in_specs=[pl.BlockSpec((tm, n_p), lambda i: (i, 0)) for n_p in widths]
        + [pl.BlockSpec((K, N), lambda i: (0, 0)), pl.BlockSpec((tm, K), lambda i: (i, 0)),
           pl.BlockSpec((1, K), lambda i: (0, 0)), pl.BlockSpec((tm, K), lambda i: (i, 0))],
        out_specs=[pl.BlockSpec((tm, K), lambda i: (i, 0)), pl.BlockSpec((1, K), lambda i: (0, 0))],
        out_shape=[jax.ShapeDtypeStruct((S, K), F32), jax.ShapeDtypeStruct((1, K), F32)],
        compiler_params=_params("arbitrary"),
    )(*pieces, w, x, g, dres)


def _matmul_tn(xs, dys, *, tk, tn, ts=512):
    S = xs[0].shape[0]
    n_s = S // ts
    (mt,) = {x.shape[1] // tk for x in xs}
    (nt,) = {d.shape[1] // tn for d in dys}

    def body(*refs):
        x_refs, dy_refs, (o_ref, acc_ref) = refs[:len(xs)], refs[len(xs):len(xs) + len(dys)], refs[len(xs) + len(dys):]
        i, j, s = pl.program_id(0), pl.program_id(1), pl.program_id(2)

        @pl.when(s == 0)
        def _():
            acc_ref[...] = jnp.zeros_like(acc_ref)

        for a, x_ref in enumerate(x_refs):
            for b, dy_ref in enumerate(dy_refs):
                @pl.when((i // mt == a) & (j // nt == b))
                def _():
                    acc_ref[...] += _dot_tn(x_ref[...].astype(BF16), dy_ref[...].astype(BF16))

        @pl.when(s == n_s - 1)
        def _():
            o_ref[...] = acc_ref[...].astype(BF16)

    def x_map(a):
        return lambda i, j, s: (jnp.where(i // mt == a, s, 0), jnp.where(i // mt == a, i % mt, 0))

    def dy_map(b):
        return lambda i, j, s: (jnp.where(j // nt == b, s, 0), jnp.where(j // nt == b, j % nt, 0))

    return pl.pallas_call(
        body, name="matmul_tn", grid=(len(xs) * mt, len(dys) * nt, n_s),
        in_specs=[pl.BlockSpec((ts, tk), x_map(a)) for a in range(len(xs))]
        + [pl.BlockSpec((ts, tn), dy_map(b)) for b in range(len(dys))],
        out_specs=pl.BlockSpec((tk, tn), lambda i, j, s: (i, j)),
        out_shape=jax.ShapeDtypeStruct((len(xs) * mt * tk, len(dys) * nt * tn), BF16),
        scratch_shapes=[pltpu.VMEM((tk, tn), F32)],
        compiler_params=_params("parallel", "parallel", "arbitrary"),
    )(*xs, *dys)


def _tri_consts():
    j = jnp.arange(ATTN_TILE)[:, None]
    s = jnp.arange(ATTN_TILE)[None, :]
    return (j > s).astype(BF16), (j <= s).astype(BF16), (j < s).astype(BF16)


SIGN_BIT = 0x80000000


def _log_terms(sn):
    minus_abs = lax.bitcast_convert_type(lax.bitcast_convert_type(sn, jnp.uint32) | jnp.uint32(SIGN_BIT), F32)
    lom = jnp.minimum(sn, 0.0) - jnp.log(1.0 + jnp.exp(minus_abs))
    return lom, lom - sn


def _causal_mask():
    t = lax.broadcasted_iota(jnp.int32, (ATTN_TILE, ATTN_TILE), 0)
    s = lax.broadcasted_iota(jnp.int32, (ATTN_TILE, ATTN_TILE), 1)
    return s < t


def _attn_prep(h, q_ref, k_ref, v_ref, qg_ref, kg_ref, qn_s, kn_s, vb_s, n_tiles):
    T = ATTN_TILE
    lanes = slice(HEAD_DIM * h, HEAD_DIM * (h + 1))
    scale = -(HEAD_DIM ** -0.5)

    def prep(i, carry):
        rows = pl.ds(pl.multiple_of(i * T, T), T)
        q = q_ref[rows, lanes]
        k = k_ref[rows, lanes]
        rq = lax.rsqrt(jnp.mean(q * q, axis=-1, keepdims=True) + EPS)
        rk = lax.rsqrt(jnp.mean(k * k, axis=-1, keepdims=True) + EPS)
        qn_s[h, rows, :] = (q * rq * qg_ref[...] * scale).astype(BF16)
        kn_s[h, rows, :] = (k * rk * kg_ref[...]).astype(BF16)
        vb_s[h, rows, :] = v_ref[rows, lanes].astype(BF16)
        return carry

    lax.fori_loop(0, n_tiles, prep, 0)


def _attn_fwd(proj, qg, kg, fulls=(), kinds=()):
    S = proj.shape[0]
    n_comm = len(fulls)
    T = ATTN_TILE
    n_tiles = S // T
    suffix, _, _ = _tri_consts()
    pairs = HEADS // ATTN_HEADS_PER_STEP
    q_blk, k_blk, v_blk = 0, ATTN_WIDTH // ATTN_BLOCK, 2 * ATTN_WIDTH // ATTN_BLOCK

    def body(*refs):
        q_ref, k_ref, v_ref, qg_ref, kg_ref, tri_ref = refs[:6]
        o_ref, lt_ref = refs[6 + n_comm:8 + n_comm]
        w_refs = refs[8 + n_comm:8 + 2 * n_comm]
        qn_s, kn_s, vb_s = refs[8 + 2 * n_comm:11 + 2 * n_comm]
        sems = refs[11 + 2 * n_comm:]
        if n_comm:
            @pl.when(pl.program_id(0) == 0)
            def _():
                for cp in _gather_copies(w_refs, kinds, *sems, 0):
                    cp.start()

        heads = range(ATTN_HEADS_PER_STEP)
        for h in heads:
            _attn_prep(h, q_ref, k_ref, v_ref, qg_ref, kg_ref, qn_s, kn_s, vb_s, n_tiles)

        def q_tile(qi, carry0):
            qrows = pl.ds(pl.multiple_of(qi * T, T), T)
            qt = [qn_s[h, qrows, :] for h in heads]

            def tile(kj, carry, diag):
                krows = pl.ds(pl.multiple_of(kj * T, T), T)
                s = [_dot_nt(qt[h], kn_s[h, krows, :]) for h in heads]
                terms = [_log_terms(s[h]) for h in heads]
                lom = [terms[h][0] for h in heads]
                if diag:
                    mask = _causal_mask()
                    lom = [jnp.where(mask, lom[h], 0.0) for h in heads]
                lom = [lom[h].astype(BF16) for h in heads]
                tail = [_dot(lom[h], tri_ref[...]) for h in heads]
                w = [jnp.exp(terms[h][1] + tail[h] + carry[h][1]) for h in heads]
                if diag:
                    w = [jnp.where(mask, w[h], 0.0) for h in heads]
                return tuple((carry[h][0] + _dot(w[h].astype(BF16), vb_s[h, krows, :]),
                              carry[h][1] + (tail[h][:, 0:1] + lom[h][:, 0:1].astype(F32))) for h in heads)

            init = tuple((jnp.zeros((T, HEAD_DIM), F32), jnp.zeros((T, 1), F32)) for h in heads)
            carry = lax.fori_loop(0, qi, lambda t, cr: tile(qi - 1 - t, cr, False), tile(qi, init, True))
            for h in heads:
                lanes = slice(HEAD_DIM * h, HEAD_DIM * (h + 1))
                o_ref[qrows, lanes] = carry[h][0].astype(BF16)
                lt_ref[qrows, lanes] = jnp.broadcast_to(carry[h][1], (T, HEAD_DIM))
            return carry0

        lax.fori_loop(0, n_tiles, q_tile, 0)
        if n_comm:
            @pl.when(pl.program_id(0) == pairs - 1)
            def _():
                for cp in _gather_copies(w_refs, kinds, *sems, 0):
                    cp.wait()

    n_sem = n_comm * (N_CHIPS - 1)
    out = pl.pallas_call(
        body, name="attn_fwd_gather" if n_comm else "attn_fwd", grid=(pairs,),
        in_specs=[pl.BlockSpec((S, ATTN_BLOCK), lambda p: (0, q_blk + p)),
                  pl.BlockSpec((S, ATTN_BLOCK), lambda p: (0, k_blk + p)),
                  pl.BlockSpec((S, ATTN_BLOCK), lambda p: (0, v_blk + p)),
                  pl.BlockSpec((1, HEAD_DIM), lambda p: (0, 0)),
                  pl.BlockSpec((1, HEAD_DIM), lambda p: (0, 0)),
                  pl.BlockSpec((T, T), lambda p: (0, 0))] + [ANY] * n_comm,
        out_specs=[pl.BlockSpec((S, ATTN_BLOCK), lambda p: (0, p)),
                   pl.BlockSpec((None, S, ATTN_BLOCK), lambda p: (p, 0, 0))] + [ANY] * n_comm,
        out_shape=[jax.ShapeDtypeStruct((S, ATTN_WIDTH), BF16),
                   jax.ShapeDtypeStruct((pairs, S, ATTN_BLOCK), F32)] + [jax.ShapeDtypeStruct(f.shape, f.dtype) for f in fulls],
        scratch_shapes=[pltpu.VMEM((ATTN_HEADS_PER_STEP, S, HEAD_DIM), BF16)] * 3
        + ([pltpu.SemaphoreType.DMA((n_sem,))] * 2 if n_comm else []),
        input_output_aliases={6 + a: 2 + a for a in range(n_comm)},
        compiler_params=pltpu.CompilerParams(dimension_semantics=("arbitrary",), vmem_limit_bytes=VMEM_LIMIT_BYTES,
                                             has_side_effects=bool(n_comm)),
    )(proj, proj, proj, qg, kg, suffix, *fulls)
    return out[0], out[1], list(out[2:])


def _attn_bwd(proj, qg, kg, ltot, dmix, parts=(), kinds=()):
    S = proj.shape[0]
    n_comm = len(parts)
    T = ATTN_TILE
    n_tiles = S // T
    _, prefix_incl, prefix_excl = _tri_consts()
    pairs = HEADS // ATTN_HEADS_PER_STEP
    q_blk, k_blk, v_blk = 0, ATTN_WIDTH // ATTN_BLOCK, 2 * ATTN_WIDTH // ATTN_BLOCK
    scale = HEAD_DIM ** -0.5

    def body(*refs):
        q_ref, k_ref, v_ref, qg_ref, kg_ref, lt_ref, do_ref, ti_ref, te_ref = refs[:9]
        p_refs = refs[9:9 + n_comm]
        dq_ref, dk_ref, dv_ref, dqg_ref, dkg_ref = refs[9 + n_comm:14 + n_comm]
        lb_refs = refs[14 + n_comm:14 + 2 * n_comm]
        qn_s, kn_s, vb_s, dq_s, dk_s, dv_s = refs[14 + 2 * n_comm:20 + 2 * n_comm]
        sems = refs[20 + 2 * n_comm:]

        @pl.when(pl.program_id(0) == 0)
        def _():
            dqg_ref[...] = jnp.zeros_like(dqg_ref)
            dkg_ref[...] = jnp.zeros_like(dkg_ref)
            for cp in _scatter_copies(p_refs, lb_refs, kinds, *sems) if n_comm else ():
                cp.start()

        heads = range(ATTN_HEADS_PER_STEP)
        for h in heads:
            _attn_prep(h, q_ref, k_ref, v_ref, qg_ref, kg_ref, qn_s, kn_s, vb_s, n_tiles)
        dk_s[...] = jnp.zeros_like(dk_s)
        dv_s[...] = jnp.zeros_like(dv_s)

        def q_tile(qi, carry0):
            qrows = pl.ds(pl.multiple_of(qi * T, T), T)
            qt = [qn_s[h, qrows, :] for h in heads]
            dob = [do_ref[qrows, HEAD_DIM * h:HEAD_DIM * (h + 1)].astype(BF16) for h in heads]
            lt = [lt_ref[qrows, HEAD_DIM * h:HEAD_DIM * h + 1] for h in heads]

            def tile(kj, carry, diag):
                krows = pl.ds(pl.multiple_of(kj * T, T), T)
                kt = [kn_s[h, krows, :] for h in heads]
                s = [_dot_nt(qt[h], kt[h]) for h in heads]
                dw = [_dot_nt(dob[h], vb_s[h, krows, :]) for h in heads]
                terms = [_log_terms(s[h]) for h in heads]
                lom = [terms[h][0] for h in heads]
                if diag:
                    mask = _causal_mask()
                    lom = [jnp.where(mask, lom[h], 0.0) for h in heads]
                pin = [_dot(lom[h].astype(BF16), ti_ref[...]) for h in heads]
                w = [jnp.exp(terms[h][1] + ((lt[h] - carry[h][1]) - pin[h])) for h in heads]
                if diag:
                    w = [jnp.where(mask, w[h], 0.0) for h in heads]
                e = [w[h] * dw[h] for h in heads]
                gex = [_dot(e[h].astype(BF16), te_ref[...]) for h in heads]
                dz = [e[h] - jnp.exp(terms[h][1]) * (e[h] + (carry[h][2] + gex[h])) for h in heads]
                if diag:
                    dz = [jnp.where(mask, dz[h], 0.0) for h in heads]
                new = []
                for h in heads:
                    dzb = dz[h].astype(BF16)
                    dk_s[h, krows, :] += _dot_tn(dzb, qt[h])
                    dv_s[h, krows, :] += _dot_tn(w[h].astype(BF16), dob[h])
                    new.append((carry[h][0] + _dot(dzb, kt[h]),
                                carry[h][1] + pin[h][:, T - 1:T],
                                carry[h][2] + gex[h][:, T - 1:T] + e[h][:, T - 1:T]))
                return tuple(new)

            zero = jnp.zeros((T, 1), F32)
            init = tuple((jnp.zeros((T, HEAD_DIM), F32), zero, zero) for h in heads)
            last = tile(qi, lax.fori_loop(0, qi, lambda kj, cr: tile(kj, cr, False), init), True)
            for h in heads:
                dq_s[h, qrows, :] = last[h][0]
            return carry0

        lax.fori_loop(0, n_tiles, q_tile, 0)

        def finish(i, carry):
            rows = pl.ds(pl.multiple_of(i * T, T), T)
            new = []
            for h in heads:
                lanes = slice(HEAD_DIM * h, HEAD_DIM * (h + 1))
                q = q_ref[rows, lanes]
                k = k_ref[rows, lanes]
                rq = lax.rsqrt(jnp.mean(q * q, axis=-1, keepdims=True) + EPS)
                rk = lax.rsqrt(jnp.mean(k * k, axis=-1, keepdims=True) + EPS)
                qh = q * rq
                kh = k * rk
                dqn = dq_s[h, rows, :] * scale
                dkn = -dk_s[h, rows, :]
                dqh = dqn * qg_ref[...]
                dkh = dkn * kg_ref[...]
                dq_ref[rows, lanes] = (rq * (dqh - qh * jnp.mean(dqh * qh, axis=-1, keepdims=True))).astype(BF16)
                dk_ref[rows, lanes] = (rk * (dkh - kh * jnp.mean(dkh * kh, axis=-1, keepdims=True))).astype(BF16)
                dv_ref[rows, lanes] = dv_s[h, rows, :].astype(BF16)
                new.append(carry[2 * h] + jnp.sum(dqn * qh, axis=0, keepdims=True))
                new.append(carry[2 * h + 1] + jnp.sum(dkn * kh, axis=0, keepdims=True))
            return tuple(new)

        zero = jnp.zeros((1, HEAD_DIM), F32)
        sums = lax.fori_loop(0, n_tiles, finish, (zero,) * (2 * len(heads)))
        dqg_ref[...] += sum(sums[0::2])
        dkg_ref[...] += sum(sums[1::2])
        if n_comm:
            @pl.when(pl.program_id(0) == pairs - 1)
            def _():
                for cp in _scatter_copies(p_refs, lb_refs, kinds, *sems):
                    cp.wait()

    blk = lambda off: pl.BlockSpec((S, ATTN_BLOCK), lambda p: (0, off + p))
    row64 = pl.BlockSpec((1, HEAD_DIM), lambda p: (0, 0))
    tri = pl.BlockSpec((T, T), lambda p: (0, 0))
    n_sem = n_comm * (N_CHIPS - 1)
    out = pl.pallas_call(
        body, name="attn_bwd_scatter" if n_comm else "attn_bwd", grid=(pairs,),
        in_specs=[blk(q_blk), blk(k_blk), blk(v_blk), row64, row64,
                  pl.BlockSpec((None, S, ATTN_BLOCK), lambda p: (p, 0, 0)), blk(0), tri, tri] + [ANY] * n_comm,
        out_specs=[blk(0), blk(0), blk(0), row64, row64] + [ANY] * n_comm,
        out_shape=[jax.ShapeDtypeStruct((S, ATTN_WIDTH), BF16)] * 3 + [jax.ShapeDtypeStruct((1, HEAD_DIM), F32)] * 2
        + _scatter_shapes(parts, kinds),
        scratch_shapes=[pltpu.VMEM((ATTN_HEADS_PER_STEP, S, HEAD_DIM), BF16)] * 3
        + [pltpu.VMEM((ATTN_HEADS_PER_STEP, S, HEAD_DIM), F32)] * 3
        + ([pltpu.SemaphoreType.DMA((n_sem,))] * 2 if n_comm else []),
        compiler_params=pltpu.CompilerParams(dimension_semantics=("arbitrary",), vmem_limit_bytes=VMEM_LIMIT_BYTES,
                                             has_side_effects=bool(n_comm)),
    )(proj, proj, proj, qg, kg, ltot, dmix, prefix_incl, prefix_excl, *parts)
    return out[:5], list(out[5:])


def _shifted(win, n_rows):
    return [win if b == 0 else pltpu.roll(win, n_rows - b, 0) for b in range(SUBLANES)]


def _taps(variants, offsets, tm):
    return {o: variants[o % SUBLANES][(o // SUBLANES) * SUBLANES:(o // SUBLANES) * SUBLANES + tm, :] for o in offsets}


def _fold_rows(a):
    return jnp.sum(a.reshape(a.shape[0] // SUBLANES, SUBLANES, a.shape[1]), axis=0)


def _glu_conv_fwd(proj, w, bias, *, tm=256):
    S = proj.shape[0]
    CB = LANES
    a_blk, b_blk = 3 * ATTN_WIDTH // CB, (3 * ATTN_WIDTH + CONV_WIDTH) // CB
    n_rows = tm + CONV_PAD

    def body(a_ref, b_ref, w_ref, bias_ref, c1_ref, pad_s):
        pad_s[0:CONV_PAD, :] = jnp.zeros((CONV_PAD, CB), F32)

        def fill(i, carry):
            rows = pl.ds(pl.multiple_of(i * tm, tm), tm)
            pad_s[pl.ds(pl.multiple_of(CONV_PAD + i * tm, SUBLANES), tm), :] = a_ref[rows, :] * _sigmoid(b_ref[rows, :])
            return carry

        lax.fori_loop(0, S // tm, fill, 0)

        def conv(i, carry):
            r0 = pl.multiple_of(i * tm, tm)
            taps = _taps(_shifted(pad_s[pl.ds(r0, n_rows), :], n_rows), range(2, 2 + CONV_KERNEL), tm)
            acc = jnp.broadcast_to(bias_ref[...], (tm, CB))
            for k in range(CONV_KERNEL):
                acc = acc + w_ref[k:k + 1, :] * taps[k + 2]
            c1_ref[pl.ds(r0, tm), :] = acc
            return carry

        lax.fori_loop(0, S // tm, conv, 0)

    return pl.pallas_call(
        body, name="glu_conv_fwd", grid=(CONV_WIDTH // CB,),
        in_specs=[pl.BlockSpec((S, CB), lambda j: (0, a_blk + j)), pl.BlockSpec((S, CB), lambda j: (0, b_blk + j)),
                  pl.BlockSpec((CONV_KERNEL, CB), lambda j: (0, j)), pl.BlockSpec((1, CB), lambda j: (0, j))],
        out_specs=pl.BlockSpec((S, CB), lambda j: (0, j)),
        out_shape=jax.ShapeDtypeStruct((S, CONV_WIDTH), F32),
        scratch_shapes=[pltpu.VMEM((S + CONV_PAD, CB), F32)],
        compiler_params=_params("parallel"),
    )(proj, proj, w, bias)


def _glu_conv_bwd(proj, w, dc1, *, tm=256):
    S = proj.shape[0]
    CB = LANES
    a_blk, b_blk = 3 * ATTN_WIDTH // CB, (3 * ATTN_WIDTH + CONV_WIDTH) // CB
    n_rows = tm + CONV_PAD

    def body(a_ref, b_ref, w_ref, dc1_ref, da_ref, db_ref, dw_ref, dbias_ref, pad_s, dpad_s, dw_s):
        pad_s[0:CONV_PAD, :] = jnp.zeros((CONV_PAD, CB), F32)
        dpad_s[S:S + CONV_PAD, :] = jnp.zeros((CONV_PAD, CB), F32)
        dw_s[...] = jnp.zeros_like(dw_s)

        def fill(i, carry):
            rows = pl.ds(pl.multiple_of(i * tm, tm), tm)
            pad_s[pl.ds(pl.multiple_of(CONV_PAD + i * tm, SUBLANES), tm), :] = a_ref[rows, :] * _sigmoid(b_ref[rows, :])
            dpad_s[rows, :] = dc1_ref[rows, :]
            return carry

        lax.fori_loop(0, S // tm, fill, 0)

        def conv(i, carry):
            r0 = pl.multiple_of(i * tm, tm)
            rows = pl.ds(r0, tm)
            taps = _taps(_shifted(dpad_s[pl.ds(r0, n_rows), :], n_rows), range(CONV_KERNEL), tm)
            acc = jnp.zeros((tm, CB), F32)
            for k in range(CONV_KERNEL):
                acc = acc + w_ref[k:k + 1, :] * taps[CONV_KERNEL - 1 - k]
            a = a_ref[rows, :]
            sg = _sigmoid(b_ref[rows, :])
            da_ref[rows, :] = (acc * sg).astype(BF16)
            db_ref[rows, :] = (acc * a * sg * (1.0 - sg)).astype(BF16)
            d = taps[0]
            taps = _taps(_shifted(pad_s[pl.ds(r0, n_rows), :], n_rows), range(2, 2 + CONV_KERNEL), tm)
            for k in range(CONV_KERNEL):
                dw_s[SUBLANES * k:SUBLANES * (k + 1), :] += _fold_rows(d * taps[k + 2])
            dw_s[SUBLANES * CONV_KERNEL:SUBLANES * (CONV_KERNEL + 1), :] += _fold_rows(d)
            return carry

        lax.fori_loop(0, S // tm, conv, 0)
        for k in range(CONV_KERNEL):
            dw_ref[k:k + 1, :] = jnp.sum(dw_s[SUBLANES * k:SUBLANES * (k + 1), :], axis=0, keepdims=True)
        dbias_ref[...] = jnp.sum(dw_s[SUBLANES * CONV_KERNEL:SUBLANES * (CONV_KERNEL + 1), :], axis=0, keepdims=True)

    col = lambda off: pl.BlockSpec((S, CB), lambda j: (0, off + j))
    return pl.pallas_call(
        body, name="glu_conv_bwd", grid=(CONV_WIDTH // CB,),
        in_specs=[col(a_blk), col(b_blk), pl.BlockSpec((CONV_KERNEL, CB), lambda j: (0, j)), col(0)],
        out_specs=[col(0), col(0), pl.BlockSpec((CONV_KERNEL, CB), lambda j: (0, j)), pl.BlockSpec((1, CB), lambda j: (0, j))],
        out_shape=[jax.ShapeDtypeStruct((S, CONV_WIDTH), BF16)] * 2
        + [jax.ShapeDtypeStruct((CONV_KERNEL, CONV_WIDTH), F32), jax.ShapeDtypeStruct((1, CONV_WIDTH), F32)],
        scratch_shapes=[pltpu.VMEM((S + CONV_PAD, CB), F32), pltpu.VMEM((S + CONV_PAD, CB), F32),
                        pltpu.VMEM((SUBLANES * (CONV_KERNEL + 1), CB), F32)],
        compiler_params=_params("parallel"),
    )(proj, proj, w, dc1)


def _ln_stats(c1):
    mu = jnp.mean(c1, axis=-1, keepdims=True)
    xc = c1 - mu
    r = lax.rsqrt(jnp.mean(xc * xc, axis=-1, keepdims=True) + EPS)
    return xc * r, r


def _ln_silu_fwd(c1, g, b, *, tm=512):
    S, C = c1.shape

    def body(c1_ref, g_ref, b_ref, c_ref):
        yh, _ = _ln_stats(c1_ref[...])
        y = yh * g_ref[...] + b_ref[...]
        c_ref[...] = (y * _sigmoid(y)).astype(BF16)

    vec = pl.BlockSpec((1, C), lambda i: (0, 0))
    return pl.pallas_call(
        body, name="ln_silu_fwd", grid=(S // tm,),
        in_specs=[pl.BlockSpec((tm, C), lambda i: (i, 0)), vec, vec],
        out_specs=pl.BlockSpec((tm, C), lambda i: (i, 0)),
        out_shape=jax.ShapeDtypeStruct((S, C), BF16),
        compiler_params=_params("parallel"),
    )(c1, g, b)


def _ln_silu_bwd(c1, g, b, dmix, *, tm=512):
    S, C = c1.shape

    def body(c1_ref, g_ref, b_ref, dc_ref, dc1_ref, dg_ref, db_ref):
        yh, r = _ln_stats(c1_ref[...])
        y = yh * g_ref[...] + b_ref[...]
        sg = _sigmoid(y)
        dy = dc_ref[...] * (sg * (1.0 + y * (1.0 - sg)))
        dyh = dy * g_ref[...]
        dc1_ref[...] = r * (dyh - jnp.mean(dyh, axis=-1, keepdims=True)
                            - yh * jnp.mean(dyh * yh, axis=-1, keepdims=True))

        @pl.when(pl.program_id(0) == 0)
        def _():
            dg_ref[...] = jnp.zeros_like(dg_ref)
            db_ref[...] = jnp.zeros_like(db_ref)

        dg_ref[...] += jnp.sum(dy * yh, axis=0, keepdims=True)
        db_ref[...] += jnp.sum(dy, axis=0, keepdims=True)

    vec = pl.BlockSpec((1, C), lambda i: (0, 0))
    return pl.pallas_call(
        body, name="ln_silu_bwd", grid=(S // tm,),
        in_specs=[pl.BlockSpec((tm, C), lambda i: (i, 0)), vec, vec, pl.BlockSpec((tm, C), lambda i: (i, 1))],
        out_specs=[pl.BlockSpec((tm, C), lambda i: (i, 0)), vec, vec],
        out_shape=[jax.ShapeDtypeStruct((S, C), F32), jax.ShapeDtypeStruct((1, C), F32), jax.ShapeDtypeStruct((1, C), F32)],
        compiler_params=_params("arbitrary"),
    )(c1, g, b, dmix)


FFN_CB = 256


def _ffn_gate(pad_s, w_ref, bias_ref, r0, tm):
    n_rows = tm + FFN_PAD
    taps = _taps(_shifted(pad_s[pl.ds(r0, n_rows), :], n_rows), range(FFN_PAD - 2, FFN_PAD + 1), tm)
    g1 = bias_ref[...] + w_ref[0:1, :] * taps[6] + w_ref[1:2, :] * taps[7] + w_ref[2:3, :] * taps[8]
    return g1, taps


def _ffn_act_fwd(u, w, bias, *, tm=256):
    S = u.shape[0]
    CB = FFN_CB
    nb = D_FF // CB

    def body(g_ref, v_ref, w_ref, bias_ref, o_ref, pad_s):
        pad_s[0:FFN_PAD, :] = jnp.zeros((FFN_PAD, CB), F32)

        def fill(i, carry):
            pad_s[pl.ds(pl.multiple_of(FFN_PAD + i * tm, SUBLANES), tm), :] = g_ref[pl.ds(pl.multiple_of(i * tm, tm), tm), :]
            return carry

        lax.fori_loop(0, S // tm, fill, 0)

        def act(i, carry):
            r0 = pl.multiple_of(i * tm, tm)
            g1, _ = _ffn_gate(pad_s, w_ref, bias_ref, r0, tm)
            o_ref[pl.ds(r0, tm), :] = (g1 * _sigmoid(g1) * v_ref[pl.ds(r0, tm), :]).astype(BF16)
            return carry

        lax.fori_loop(0, S // tm, act, 0)

    return pl.pallas_call(
        body, name="ffn_act_fwd", grid=(nb,),
        in_specs=[pl.BlockSpec((S, CB), lambda j: (0, j)), pl.BlockSpec((S, CB), lambda j: (0, nb + j)),
                  pl.BlockSpec((FFN_KERNEL, CB), lambda j: (0, j)), pl.BlockSpec((1, CB), lambda j: (0, j))],
        out_specs=pl.BlockSpec((S, CB), lambda j: (0, j)),
        out_shape=jax.ShapeDtypeStruct((S, D_FF), BF16),
        scratch_shapes=[pltpu.VMEM((S + FFN_PAD, CB), F32)],
        compiler_params=_params("parallel"),
    )(u, u, w, bias)


def _ffn_act_bwd(u, w, bias, dact, *, tm=256):
    S = u.shape[0]
    CB = FFN_CB
    nb = D_FF // CB

    def body(g_ref, v_ref, w_ref, bias_ref, da_ref, dg_ref, dv_ref, dw_ref, dbias_ref, pad_s, dpad_s, dw_s):
        pad_s[0:FFN_PAD, :] = jnp.zeros((FFN_PAD, CB), F32)
        dpad_s[S:S + FFN_PAD, :] = jnp.zeros((FFN_PAD, CB), F32)
        dw_s[...] = jnp.zeros_like(dw_s)

        def fill(i, carry):
            pad_s[pl.ds(pl.multiple_of(FFN_PAD + i * tm, SUBLANES), tm), :] = g_ref[pl.ds(pl.multiple_of(i * tm, tm), tm), :]
            return carry

        lax.fori_loop(0, S // tm, fill, 0)

        def first(i, carry):
            r0 = pl.multiple_of(i * tm, tm)
            rows = pl.ds(r0, tm)
            g1, taps = _ffn_gate(pad_s, w_ref, bias_ref, r0, tm)
            sg = _sigmoid(g1)
            da = da_ref[rows, :].astype(F32)
            dv_ref[rows, :] = (da * g1 * sg).astype(BF16)
            dg1 = da * v_ref[rows, :] * (sg * (1.0 + g1 * (1.0 - sg)))
            dpad_s[rows, :] = dg1
            for k in range(FFN_KERNEL):
                dw_s[SUBLANES * k:SUBLANES * (k + 1), :] += _fold_rows(dg1 * taps[FFN_PAD - 2 + k])
            dw_s[SUBLANES * FFN_KERNEL:SUBLANES * (FFN_KERNEL + 1), :] += _fold_rows(dg1)
            return carry

        lax.fori_loop(0, S // tm, first, 0)

        def second(i, carry):
            r0 = pl.multiple_of(i * tm, tm)
            n_rows = tm + FFN_PAD
            taps = _taps(_shifted(dpad_s[pl.ds(r0, n_rows), :], n_rows), range(FFN_KERNEL), tm)
            dg_ref[pl.ds(r0, tm), :] = (w_ref[2:3, :] * taps[0] + w_ref[1:2, :] * taps[1] + w_ref[0:1, :] * taps[2]).astype(BF16)
            return carry

        lax.fori_loop(0, S // tm, second, 0)
        for k in range(FFN_KERNEL):
            dw_ref[k:k + 1, :] = jnp.sum(dw_s[SUBLANES * k:SUBLANES * (k + 1), :], axis=0, keepdims=True)
        dbias_ref[...] = jnp.sum(dw_s[SUBLANES * FFN_KERNEL:SUBLANES * (FFN_KERNEL + 1), :], axis=0, keepdims=True)

    col = lambda off: pl.BlockSpec((S, CB), lambda j: (0, off + j))
    wspec = pl.BlockSpec((FFN_KERNEL, CB), lambda j: (0, j))
    bspec = pl.BlockSpec((1, CB), lambda j: (0, j))
    return pl.pallas_call(
        body, name="ffn_act_bwd", grid=(nb,),
        in_specs=[col(0), col(nb), wspec, bspec, col(0)],
        out_specs=[col(0), col(0), wspec, bspec],
        out_shape=[jax.ShapeDtypeStruct((S, D_FF), BF16)] * 2
        + [jax.ShapeDtypeStruct((FFN_KERNEL, D_FF), F32), jax.ShapeDtypeStruct((1, D_FF), F32)],
        scratch_shapes=[pltpu.VMEM((S + FFN_PAD, CB), F32), pltpu.VMEM((S + FFN_PAD, CB), F32),
                        pltpu.VMEM((SUBLANES * (FFN_KERNEL + 1), CB), F32)],
        compiler_params=_params("parallel"),
    )(u, u, w, bias, dact)


def _loss_grad(y, target, *, tm=512):
    S, D = y.shape

    def body(y_ref, t_ref, dy_ref, l_ref):
        d = y_ref[...] - t_ref[...]
        dy_ref[...] = d * (1.0 / D)

        @pl.when(pl.program_id(0) == 0)
        def _():
            l_ref[...] = jnp.zeros_like(l_ref)

        l_ref[...] += 0.5 * jnp.sum(jnp.mean(d * d, axis=-1, keepdims=True), axis=0, keepdims=True)

    dy, l = pl.pallas_call(
        body, name="loss_grad", grid=(S // tm,),
        in_specs=[pl.BlockSpec((tm, D), lambda i: (i, 0))] * 2,
        out_specs=[pl.BlockSpec((tm, D), lambda i: (i, 0)), pl.BlockSpec((SUBLANES, LANES), lambda i: (0, 0))],
        out_shape=[jax.ShapeDtypeStruct((S, D), F32), jax.ShapeDtypeStruct((SUBLANES, LANES), F32)],
        compiler_params=_params("arbitrary"),
    )(y, target)
    return dy, l[0, 0]


def _row_tile(rows, cap=512):
    t = min(rows, cap)
    while rows % t or t % SUBLANES:
        t -= 1
    return t


def _adam_update(w, g, m, v):
    m1 = ADAM_B1 * m + (1.0 - ADAM_B1) * g
    v1 = ADAM_B2 * v + (1.0 - ADAM_B2) * (g * g)
    m_hat = m1 / (1.0 - ADAM_B1 ** ADAM_STEP)
    v_hat = v1 / (1.0 - ADAM_B2 ** ADAM_STEP)
    return -ADAM_LR * (m_hat / (jnp.sqrt(v_hat) + ADAM_EPS) + ADAM_WD * w), m1, v1


def _adamw(w, g, m, v):
    R, C = w.shape
    tr = _row_tile(R, 256)

    def body(w_ref, g_ref, m_ref, v_ref, d_ref, nm_ref, nv_ref):
        d_ref[...], nm_ref[...], nv_ref[...] = _adam_update(w_ref[...], g_ref[...], m_ref[...], v_ref[...])

    spec = pl.BlockSpec((tr, C), lambda i: (i, 0))
    return pl.pallas_call(
        body, name="adamw", grid=(R // tr,),
        in_specs=[spec] * 4, out_specs=[spec] * 3,
        out_shape=[jax.ShapeDtypeStruct((R, C), F32)] * 3,
        compiler_params=_params("parallel"),
    )(w, g, m, v)


def _adamw_layer(layer, w, g, m, v, prev):
    L, R, C = w.shape
    tr = _row_tile(R, 256)
    n_prev = 4 if prev else 0

    def body(*refs):
        w_ref, g_ref, m_ref, v_ref = refs[:4]
        go_ref, d_ref, nm_ref, nv_ref = refs[4 + n_prev:8 + n_prev]
        gv = g_ref[...]
        go_ref[...] = gv
        d_ref[...], nm_ref[...], nv_ref[...] = _adam_update(w_ref[...], gv, m_ref[...], v_ref[...])

    stacked = pl.BlockSpec((None, tr, C), lambda i: (layer, i, 0))
    return pl.pallas_call(
        body, name="adamw_layer", grid=(R // tr,),
        in_specs=[stacked, pl.BlockSpec((tr, C), lambda i: (i, 0)), stacked, stacked] + [ANY] * n_prev,
        out_specs=[stacked] * 4, out_shape=[jax.ShapeDtypeStruct((L, R, C), F32)] * 4,
        input_output_aliases={4 + j: j for j in range(n_prev)},
        compiler_params=_params("parallel"),
    )(w, g, m, v, *(prev or ()))


def _cast_into_full(w, layer, kind, chip):
    _, R, C = w.shape
    tr = _row_tile(R, 512)

    def body(chip_ref, w_ref, o_ref):
        o_ref[...] = w_ref[...].astype(BF16)

    if kind == "col":
        out_shape = jax.ShapeDtypeStruct((1, R, N_CHIPS * C), BF16)
        out_spec = pl.BlockSpec((None, tr, C), lambda i, chip_ref: (0, i, chip_ref[0]))
    else:
        out_shape = jax.ShapeDtypeStruct((1, N_CHIPS * R, C), BF16)
        out_spec = pl.BlockSpec((None, tr, C), lambda i, chip_ref: (0, chip_ref[0] * (R // tr) + i, 0))
    return pl.pallas_call(
        body, name="cast_into_full_" + kind,
        grid_spec=pltpu.PrefetchScalarGridSpec(
            num_scalar_prefetch=1, grid=(R // tr,),
            in_specs=[pl.BlockSpec((None, tr, C), lambda i, chip_ref: (layer, i, 0))], out_specs=out_spec),
        out_shape=out_shape, compiler_params=_params("parallel"),
    )(chip, w)


def _add_half(g4, la, c):
    L, _, H, W = g4.shape
    th = _row_tile(H, 256)

    def body(c_ref, g_ref, la_ref, o_ref):
        o_ref[...] = (g_ref[...].astype(F32) + la_ref[...].astype(F32)).astype(BF16)

    return pl.pallas_call(
        body, name="add_half",
        grid_spec=pltpu.PrefetchScalarGridSpec(
            num_scalar_prefetch=1, grid=(L, H // th),
            in_specs=[pl.BlockSpec((None, None, th, W), lambda l, i, c_ref: (l, c_ref[0], i, 0)),
                      pl.BlockSpec((None, th, W), lambda l, i, c_ref: (l, i, 0))],
            out_specs=pl.BlockSpec((None, th, W), lambda l, i, c_ref: (l, i, 0))),
        out_shape=jax.ShapeDtypeStruct((L, H, W), BF16),
        compiler_params=_params("parallel", "parallel"),
    )(c, g4, la)


def _add_parts(p, lb, place, kind):
    _, L, H, C = lb.shape
    th = _row_tile(H, 256)

    def body(s_ref, p_ref, lb_ref, o_ref):
        acc = p_ref[...].astype(F32)
        for k in range(N_CHIPS - 1):
            acc = acc + lb_ref[k].astype(F32)
        o_ref[...] = acc

    if kind == "col":
        p_spec = pl.BlockSpec((None, th, C), lambda l, i, s_ref: (l, i, s_ref[0]))
    else:
        p_spec = pl.BlockSpec((None, None, th, C), lambda l, i, s_ref: (l, s_ref[0], i, 0))
    return pl.pallas_call(
        body, name="add_parts_" + kind,
        grid_spec=pltpu.PrefetchScalarGridSpec(
            num_scalar_prefetch=1, grid=(L, H // th),
            in_specs=[p_spec, pl.BlockSpec((N_CHIPS - 1, None, th, C), lambda l, i, s_ref: (0, l, i, 0))],
            out_specs=pl.BlockSpec((None, None, th, C), lambda l, i, s_ref: (l, s_ref[1], i, 0))),
        out_shape=jax.ShapeDtypeStruct((L, 2, H, C), F32),
        compiler_params=_params("parallel", "parallel"),
    )(place, p, lb)


ANY = pl.BlockSpec(memory_space=pl.ANY)


def _place():
    x, y, c = lax.axis_index("x"), lax.axis_index("y"), lax.axis_index("c")
    chips = [(1 - x, y), (x, 1 - y), (1 - x, 1 - y)]
    return x, y, c, chips


def _comm_call(body, name, ins, out_shape, n_remote, n_local, aliases=None):
    scratch = [pltpu.SemaphoreType.DMA((n_remote,)), pltpu.SemaphoreType.DMA((n_remote,))]
    if n_local:
        scratch.append(pltpu.SemaphoreType.DMA((n_local,)))
    return pl.pallas_call(
        body, name=name, in_specs=[ANY] * len(ins), out_specs=[ANY] * len(out_shape), out_shape=out_shape,
        scratch_shapes=scratch, input_output_aliases=aliases or {},
        compiler_params=pltpu.CompilerParams(has_side_effects=True),
    )(*ins)


def _remote(src, dst, send, recv, k, to):
    return pltpu.make_async_remote_copy(src_ref=src, dst_ref=dst, send_sem=send.at[k], recv_sem=recv.at[k],
                                        device_id=to, device_id_type=MESH)


def _gather_weights(fulls, kinds):
    n = len(fulls)
    out_shape = [jax.ShapeDtypeStruct(f.shape, f.dtype) for f in fulls]

    def body(*refs):
        outs, (send, recv) = refs[n:2 * n], refs[2 * n:]
        first = _gather_copies(outs, kinds, send, recv, 0)
        for cp in first:
            cp.start()
        for cp in first:
            cp.wait()
        passed = _pass_on_copies(outs, kinds, send, recv, len(first))
        for cp in passed:
            cp.start()
        for cp in passed:
            cp.wait()

    return _comm_call(body, "gather_weights", fulls, out_shape, 2 * n * (N_CHIPS - 1), 0, {a: a for a in range(n)})


def _window(ref, kind, s, h):
    if kind == "col":
        H, C = ref.shape[1] // 2, ref.shape[2] // N_CHIPS
        return ref.at[:, pl.ds(pl.multiple_of(h * H, 16), H), pl.ds(pl.multiple_of(s * C, LANES), C)]
    R = ref.shape[1] // N_CHIPS
    return ref.at[:, pl.ds(pl.multiple_of(s * R + h * (R // 2), 16), R // 2), :]


def _gather_copies(outs, kinds, send, recv, sem0):
    x, y, c, chips = _place()
    me = 2 * x + y
    return [_remote(_window(o, kind, me, c), _window(o, kind, me, c), send, recv, sem0 + a * (N_CHIPS - 1) + k, (*chip, c))
            for a, (o, kind) in enumerate(zip(outs, kinds)) for k, chip in enumerate(chips)]


def _pass_on_copies(outs, kinds, send, recv, sem0):
    x, y, c, chips = _place()
    cps = []
    for a, (o, kind) in enumerate(zip(outs, kinds)):
        for k, chip in enumerate(chips):
            landed = _window(o, kind, 2 * chip[0] + chip[1], c)
            cps.append(_remote(landed, landed, send, recv, sem0 + a * (N_CHIPS - 1) + k, (x, y, 1 - c)))
    return cps


def _pass_on(fulls, kinds):
    n = len(fulls)
    out_shape = [jax.ShapeDtypeStruct(f.shape, f.dtype) for f in fulls]

    def body(*refs):
        outs, (send, recv) = refs[n:2 * n], refs[2 * n:]
        cps = _pass_on_copies(outs, kinds, send, recv, 0)
        for cp in cps:
            cp.start()
        for cp in cps:
            cp.wait()

    return _comm_call(body, "pass_on", fulls, out_shape, n * (N_CHIPS - 1), 0, {a: a for a in range(n)})


def _gather_small(shards):
    n = len(shards)
    out_shape = [jax.ShapeDtypeStruct((N_CHIPS,) + s.shape, s.dtype) for s in shards]

    def body(*refs):
        srcs, outs, (send, recv, loc) = refs[:n], refs[n:2 * n], refs[2 * n:]
        x, y, c, chips = _place()
        me = 2 * x + y
        remote, local = [], []
        for a in range(n):
            local.append(pltpu.make_async_copy(srcs[a], outs[a].at[me], loc.at[a]))
            for k, chip in enumerate(chips):
                remote.append(_remote(srcs[a], outs[a].at[me], send, recv, a * (N_CHIPS - 1) + k, (*chip, c)))
        for cp in local + remote:
            cp.start()
        for cp in remote + local:
            cp.wait()

    return _comm_call(body, "gather_small", shards, out_shape, n * (N_CHIPS - 1), n)


def _exchange_halves(g4s):
    n = len(g4s)
    out_shape = [jax.ShapeDtypeStruct((g.shape[0],) + g.shape[2:], g.dtype) for g in g4s]

    def body(*refs):
        gs, las, (send, recv) = refs[:n], refs[n:2 * n], refs[2 * n:]
        x, y, c, _ = _place()
        cps = [_remote(gs[a].at[:, 1 - c], las[a], send, recv, a, (x, y, 1 - c)) for a in range(n)]
        for cp in cps:
            cp.start()
        for cp in cps:
            cp.wait()

    return _comm_call(body, "exchange_halves", g4s, out_shape, n, 0)


def _scatter_partials(ps, kinds):
    n = len(ps)

    def body(*refs):
        srcs, lbs, (send, recv) = refs[:n], refs[n:2 * n], refs[2 * n:]
        cps = _scatter_copies(srcs, lbs, kinds, send, recv)
        for cp in cps:
            cp.start()
        for cp in cps:
            cp.wait()

    return _comm_call(body, "scatter_partials", ps, _scatter_shapes(ps, kinds), n * (N_CHIPS - 1), 0)


def _scatter_shapes(ps, kinds):
    out_shape = []
    for p, kind in zip(ps, kinds):
        L, H, C = (p.shape[0], p.shape[1], p.shape[2] // N_CHIPS) if kind == "col" else (p.shape[0], p.shape[2], p.shape[3])
        out_shape.append(jax.ShapeDtypeStruct((N_CHIPS - 1, L, H, C), p.dtype))
    return out_shape


def _scatter_copies(srcs, lbs, kinds, send, recv):
    x, y, c, chips = _place()
    cps = []
    for a, (src, lb, kind) in enumerate(zip(srcs, lbs, kinds)):
        C = lb.shape[3]
        for k, chip in enumerate(chips):
            s = 2 * chip[0] + chip[1]
            part = src.at[:, :, pl.ds(pl.multiple_of(s * C, LANES), C)] if kind == "col" else src.at[:, s]
            cps.append(_remote(part, lb.at[k], send, recv, a * (N_CHIPS - 1) + k, (*chip, c)))
    return cps


def _share_halves(g4s):
    n = len(g4s)
    out_shape = [jax.ShapeDtypeStruct(g.shape, g.dtype) for g in g4s]

    def body(*refs):
        outs, (send, recv) = refs[n:2 * n], refs[2 * n:]
        x, y, c, _ = _place()
        cps = [_remote(outs[a].at[:, c], outs[a].at[:, c], send, recv, a, (x, y, 1 - c)) for a in range(n)]
        for cp in cps:
            cp.start()
        for cp in cps:
            cp.wait()

    return _comm_call(body, "share_halves", g4s, out_shape, n, 0, {a: a for a in range(n)})


def _allreduce_small(part):
    N = part.shape[0]

    def body(p_ref, o_ref, buf, send, recv):
        x, y, c, _ = _place()
        me = 4 * x + 2 * y + c
        buf[me] = p_ref[...]
        cps = []
        for k in range(1, N_DEV):
            peer = (1 - x if k & 4 else x, 1 - y if k & 2 else y, 1 - c if k & 1 else c)
            cps.append(_remote(p_ref, buf.at[me], send, recv, k - 1, peer))
        for cp in cps:
            cp.start()
        for cp in cps:
            cp.wait()
        acc = buf[0]
        for i in range(1, N_DEV):
            acc = acc + buf[i]
        o_ref[...] = acc

    vmem = pl.BlockSpec(memory_space=pltpu.VMEM)
    return pl.pallas_call(
        body, name="allreduce_small", in_specs=[vmem], out_specs=vmem,
        out_shape=jax.ShapeDtypeStruct((N, LANES), F32),
        scratch_shapes=[pltpu.VMEM((N_DEV, N, LANES), F32), pltpu.SemaphoreType.DMA((N_DEV - 1,)),
                        pltpu.SemaphoreType.DMA((N_DEV - 1,))],
        compiler_params=pltpu.CompilerParams(has_side_effects=True, vmem_limit_bytes=VMEM_LIMIT_BYTES),
    )(part)


AFTER_ATTENTION = ("w_out", "w_up", "w_down")


def _layer_fwd(x, p, fulls=(), kinds=()):
    h1, proj = _norm_matmul(x, p["norm1_g"], p["w_in"])
    attn, ltot, filled = _attn_fwd(proj, p["q_norm_g"], p["k_norm_g"], fulls, kinds)
    if filled:
        filled = _pass_on(filled, kinds)
        p = dict(p, **{n: f[0] for n, f in zip(AFTER_ATTENTION, filled)})
    c1 = _glu_conv_fwd(proj, p["conv_dw_w"], p["conv_dw_b"])
    c = _ln_silu_fwd(c1, p["conv_ln_g"], p["conv_ln_b"])
    x_mid = _matmul_res([attn, c], p["w_out"], x)
    h2, u = _norm_matmul(x_mid, p["norm2_g"], p["w_up"])
    act = _ffn_act_fwd(u, p["ffn_dw_w"], p["ffn_dw_b"])
    x_out = _matmul_res([act], p["w_down"], x_mid)
    saved = dict(x=x, h1=h1, proj=proj, attn=attn, ltot=ltot, c1=c1, c=c, x_mid=x_mid, h2=h2, u=u, act=act)
    return x_out, saved, filled


def _chip_partials(gs, kinds, core):
    g4s = []
    for g, kind in zip(gs, kinds):
        if kind == "col":
            g4s.append(g.reshape(1, 2, g.shape[0] // 2, g.shape[1]))
        else:
            g4s.append(g.reshape(N_CHIPS, 2, g.shape[0] // N_CHIPS // 2, g.shape[1]))
    parts = [_add_half(g4, la, core) for g4, la in zip(g4s, _exchange_halves(g4s))]
    return [p if kind == "col" else p.reshape(1, N_CHIPS, p.shape[1], p.shape[2]) for p, kind in zip(parts, kinds)]


def _owned_sums(parts, landed, kinds, place):
    halves = [_add_parts(p, lb, place, kind) for p, lb, kind in zip(parts, landed, kinds)]
    return [g.reshape(2 * g.shape[2], g.shape[3]) for g in _share_halves(halves)]


def _layer_bwd(dx_out, s, p, comm=None):
    g = {}
    dact = _matmul_nt([dx_out], p["w_down"], BF16)
    g["w_down"] = _matmul_tn([s["act"]], [dx_out], tk=D_FF // 2, tn=D_MODEL)
    dgate, dval, g["ffn_dw_w"], g["ffn_dw_b"] = _ffn_act_bwd(s["u"], p["ffn_dw_w"], p["ffn_dw_b"], dact)
    dx_mid, g["norm2_g"] = _matmul_nt_rmsbwd([dgate, dval], p["w_up"], s["x_mid"], p["norm2_g"], dx_out)
    g["w_up"] = _matmul_tn([s["h2"]], [dgate, dval], tk=D_MODEL, tn=D_FF // 2)
    dmix = _matmul_nt([dx_mid], p["w_out"], F32)
    g["w_out"] = _matmul_tn([s["attn"], s["c"]], [dx_mid], tk=ATTN_WIDTH, tn=D_MODEL)
    dc1, g["conv_ln_g"], g["conv_ln_b"] = _ln_silu_bwd(s["c1"], p["conv_ln_g"], p["conv_ln_b"], dmix)
    da, db, g["conv_dw_w"], g["conv_dw_b"] = _glu_conv_bwd(s["proj"], p["conv_dw_w"], dc1)
    parts, kinds, sums = [], [], []
    if comm is not None:
        core, place, pending = comm
        kinds = [BIG_KIND[n] for n in AFTER_ATTENTION] + ["col"] * len(pending)
        parts = _chip_partials([g[n] for n in AFTER_ATTENTION] + list(pending), kinds, core)
    (dq, dk, dv, g["q_norm_g"], g["k_norm_g"]), landed = _attn_bwd(
        s["proj"], p["q_norm_g"], p["k_norm_g"], s["ltot"], dmix, parts, kinds)
    if comm is not None:
        sums = _owned_sums(parts, landed, kinds, place)
    pieces = [dq, dk, dv, da, db]
    dx, g["norm1_g"] = _matmul_nt_rmsbwd(pieces, p["w_in"], s["x"], p["norm1_g"], dx_mid)
    g["w_in"] = _matmul_tn([s["h1"]], pieces, tk=D_MODEL, tn=ATTN_WIDTH)
    return dx, g, sums


WEIGHTS = ("norm1_g", "w_in", "q_norm_g", "k_norm_g", "conv_dw_w", "conv_dw_b", "conv_ln_g", "conv_ln_b",
           "w_out", "norm2_g", "w_up", "ffn_dw_w", "ffn_dw_b", "w_down")
BIG = ("w_in", "w_out", "w_up", "w_down")
BIG_KIND = {"w_in": "col", "w_out": "row", "w_up": "col", "w_down": "row"}
SMALL_SHARDED = ("conv_dw_w", "ffn_dw_w")
REPLICATED = tuple(n for n in WEIGHTS if n not in BIG + SMALL_SHARDED)


def _pack(arrays):
    flat = jnp.concatenate([a.reshape(-1) for a in arrays])
    rows = -(-flat.shape[0] // (SUBLANES * LANES)) * SUBLANES
    return jnp.pad(flat, (0, rows * LANES - flat.shape[0])).reshape(rows, LANES)


def _unpack(packed, shapes):
    flat = packed.reshape(-1)
    out, off = [], 0
    for shape in shapes:
        size = 1
        for d in shape:
            size *= d
        out.append(flat[off:off + size].reshape(shape))
        off += size
    return out


def _unshard_last(stacked):
    n, L, K, C = stacked.shape
    return jnp.transpose(stacked, (1, 2, 0, 3)).reshape(L, K, n * C)


def kernel(x, norm1_g, w_in, q_norm_g, k_norm_g, conv_dw_w, conv_dw_b, conv_ln_g, conv_ln_b, w_out, norm2_g, w_up, ffn_dw_w, ffn_dw_b, w_down, loss_target, m_norm1_g, m_w_in, m_q_norm_g, m_k_norm_g, m_conv_dw_w, m_conv_dw_b, m_conv_ln_g, m_conv_ln_b, m_w_out, m_norm2_g, m_w_up, m_ffn_dw_w, m_ffn_dw_b, m_w_down, v_norm1_g, v_w_in, v_q_norm_g, v_k_norm_g, v_conv_dw_w, v_conv_dw_b, v_conv_ln_g, v_conv_ln_b, v_w_out, v_norm2_g, v_w_up, v_ffn_dw_w, v_ffn_dw_b, v_w_down):
    given = dict(locals())
    w = {n: given[n] for n in WEIGHTS}
    m = {n: given["m_" + n] for n in WEIGHTS}
    v = {n: given["v_" + n] for n in WEIGHTS}
    chip = 2 * lax.axis_index("x") + lax.axis_index("y")
    core = lax.axis_index("c")
    chip_arr = jnp.reshape(chip, (1,)).astype(jnp.int32)
    core_arr = jnp.reshape(core, (1,)).astype(jnp.int32)
    L = DEPTH

    place = jnp.concatenate([chip_arr, core_arr])

    full = [{n: _cast_into_full(w[n], l, BIG_KIND[n], chip_arr) for n in BIG} for l in range(L)]
    full[0]["w_in"] = _gather_weights([full[0]["w_in"]], ["col"])[0]
    small_full = {n: _unshard_last(stacked)
                  for n, stacked in zip(SMALL_SHARDED, _gather_small([w[n] for n in SMALL_SHARDED]))}
    params = []
    for l in range(L):
        p = {n: small_full[n][l] for n in SMALL_SHARDED}
        p.update({n: w[n][l][None] for n in REPLICATED})
        params.append(p)

    act = x[0]
    saved = []
    for l in range(L):
        group = [(l, n) for n in AFTER_ATTENTION] + ([(l + 1, "w_in")] if l + 1 < L else [])
        act, s, filled = _layer_fwd(act, dict(params[l], w_in=full[l]["w_in"][0]),
                                    [full[i][n] for i, n in group], [BIG_KIND[n] for _, n in group])
        for (i, n), f in zip(group, filled):
            full[i][n] = f
        saved.append(s)
    for l in range(L):
        params[l].update({n: full[l][n][0] for n in BIG})
    dx, loss_part = _loss_grad(act, loss_target[0])
    loss = lax.psum(loss_part, ("x", "y", "c"))

    grads = [None] * L
    summed = {}
    pending = []
    for l in reversed(range(L)):
        dx, grads[l], sums = _layer_bwd(dx, saved[l], params[l], (core_arr, place, pending))
        names = [(l, n) for n in AFTER_ATTENTION] + ([(l + 1, "w_in")] if pending else [])
        summed.update(zip(names, sums))
        pending = [grads[l]["w_in"]]
    parts = _chip_partials(pending, ["col"], core_arr)
    summed[0, "w_in"] = _owned_sums(parts, _scatter_partials(parts, ["col"]), ["col"], place)[0]

    grad, delta, new_m, new_v = {}, {}, {}, {}
    for n in BIG:
        out = None
        for l in range(L):
            out = _adamw_layer(l, w[n], summed[l, n], m[n], v[n], out)
        grad[n], delta[n], new_m[n], new_v[n] = out

    small = REPLICATED + SMALL_SHARDED
    small_grads = [jnp.stack([grads[l][n] for l in range(L)]) for n in small]
    small_sums = _unpack(_allreduce_small(_pack(small_grads)), [a.shape for a in small_grads])
    for n, g in zip(small, small_sums):
        if n in REPLICATED:
            grad[n] = g.reshape(w[n].shape)
        else:
            width = w[n].shape[2]
            grad[n] = lax.dynamic_slice_in_dim(g, chip * width, width, axis=2)
    shapes = [w[n].shape for n in small]
    packed = _adamw(*[_pack([src[n] for n in small]) for src in (w, grad, m, v)])
    for out, pk in zip((delta, new_m, new_v), packed):
        out.update(zip(small, _unpack(pk, shapes)))

    return (loss, dx[None], *[grad[n] for n in WEIGHTS], *[delta[n] for n in WEIGHTS],
            *[new_m[n] for n in WEIGHTS], *[new_v[n] for n in WEIGHTS])
```

```python
import functools

import jax
import jax.numpy as jnp
from jax import lax
from jax.experimental import pallas as pl
from jax.experimental.pallas import tpu as pltpu

F32 = jnp.float32
BF16 = jnp.bfloat16

DEPTH = 4
D_MODEL = 1024
HEADS = 8
HEAD_DIM = 64
ATTN_WIDTH = HEADS * HEAD_DIM
CONV_WIDTH = D_MODEL - ATTN_WIDTH
CONV_KERNEL = 31
D_FF = 2816
FFN_KERNEL = 3
EPS = 1e-6
ADAM_LR, ADAM_B1, ADAM_B2, ADAM_EPS, ADAM_WD, ADAM_STEP = 0.001, 0.9, 0.999, 1e-08, 0.01, 10

N_CHIPS = 4
N_DEV = 8
LANES = 128
SUBLANES = 8
VMEM_LIMIT_BYTES = 56 * 2**20
ATTN_TILE = 256
ATTN_HEADS_PER_STEP = 4
ATTN_BLOCK = ATTN_HEADS_PER_STEP * HEAD_DIM
CONV_PAD = 32
FFN_PAD = 8
MESH = pl.DeviceIdType.MESH


def _params(*sem):
    return pltpu.CompilerParams(dimension_semantics=sem if sem else None, vmem_limit_bytes=VMEM_LIMIT_BYTES)


class _Guest:
    def __init__(self, name, ins, out_shape, aliases, copies, n_sem):
        self.name, self.ins, self.out_shape, self.aliases, self.copies, self.n_sem = name, ins, out_shape, aliases, copies, n_sem


def _hosted_call(body, guest, *, name, grid, in_specs, out_specs, out_shape, scratch_shapes, operands):
    if guest is None:
        out = pl.pallas_call(body, name=name, grid=grid, in_specs=in_specs, out_specs=out_specs, out_shape=out_shape,
                             scratch_shapes=scratch_shapes, compiler_params=_params("arbitrary"))(*operands)
        return list(out), []
    n_in, n_out, n_scr = len(in_specs), len(out_specs), len(scratch_shapes)
    gi, go = len(guest.ins), len(guest.out_shape)

    def hosting(*refs):
        ins, g_in = refs[:n_in], refs[n_in:n_in + gi]
        outs, g_out = refs[n_in + gi:n_in + gi + n_out], refs[n_in + gi + n_out:n_in + gi + n_out + go]
        scratch, (send, recv) = refs[n_in + gi + n_out + go:-2], refs[-2:]

        @pl.when(pl.program_id(0) == 0)
        def _():
            for cp in guest.copies(g_in, g_out, send, recv):
                cp.start()

        body(*ins, *outs, *scratch)

        @pl.when(pl.program_id(0) == grid[0] - 1)
        def _():
            for cp in guest.copies(g_in, g_out, send, recv):
                cp.wait()

    out = pl.pallas_call(
        hosting, name=name + "_" + guest.name, grid=grid,
        in_specs=list(in_specs) + [ANY] * gi, out_specs=list(out_specs) + [ANY] * go,
        out_shape=list(out_shape) + list(guest.out_shape),
        scratch_shapes=list(scratch_shapes) + [pltpu.SemaphoreType.DMA((guest.n_sem,))] * 2,
        input_output_aliases={n_in + a: n_out + b for a, b in guest.aliases.items()},
        compiler_params=pltpu.CompilerParams(dimension_semantics=("arbitrary",), vmem_limit_bytes=VMEM_LIMIT_BYTES,
                                             has_side_effects=True),
    )(*operands, *guest.ins)
    return list(out[:n_out]), list(out[n_out:])


def _dot(a, b):
    return jnp.dot(a, b, preferred_element_type=F32)


def _dot_nt(a, b):
    return lax.dot_general(a, b, (((1,), (1,)), ((), ())), preferred_element_type=F32)


def _dot_tn(a, b):
    return lax.dot_general(a, b, (((0,), (0,)), ((), ())), preferred_element_type=F32)


def _sigmoid(x):
    return 1.0 / (1.0 + jnp.exp(-x))


def _norm_matmul(x, g, w, *, tm=256):
    S, D = x.shape
    N = w.shape[1]

    def body(x_ref, g_ref, w_ref, ht_ref, y_ref):
        xv = x_ref[...]
        h = xv * lax.rsqrt(jnp.mean(xv * xv, axis=-1, keepdims=True) + EPS) * g_ref[...]
        ht_ref[...] = h.T.astype(BF16)
        y_ref[...] = _dot(h.astype(BF16), w_ref[...])

    return pl.pallas_call(
        body, name="norm_matmul", grid=(S // tm,),
        in_specs=[pl.BlockSpec((tm, D), lambda i: (i, 0)),
                  pl.BlockSpec((1, D), lambda i: (0, 0)),
                  pl.BlockSpec((D, N), lambda i: (0, 0))],
        out_specs=[pl.BlockSpec((D, tm), lambda i: (0, i)),
                   pl.BlockSpec((tm, N), lambda i: (i, 0))],
        out_shape=[jax.ShapeDtypeStruct((D, S), BF16), jax.ShapeDtypeStruct((S, N), F32)],
        compiler_params=_params("parallel"),
    )(x, g, w)


def _matmul_res(pieces, w, res, *, tm=512):
    S, N = res.shape
    K = w.shape[0]
    widths = [p.shape[1] for p in pieces]
    assert sum(widths) == K

    def body(*refs):
        p_refs, (w_ref, res_ref, o_ref) = refs[:len(pieces)], refs[len(pieces):]
        acc = res_ref[...]
        off = 0
        for p_ref, kp in zip(p_refs, widths):
            acc = acc + _dot(p_ref[...], w_ref[off:off + kp, :])
            off += kp
        o_ref[...] = acc

    return pl.pallas_call(
        body, name="matmul_res", grid=(S // tm,),
        in_specs=[pl.BlockSpec((tm, kp), lambda i: (i, 0)) for kp in widths]
        + [pl.BlockSpec((K, N), lambda i: (0, 0)), pl.BlockSpec((tm, N), lambda i: (i, 0))],
        out_specs=pl.BlockSpec((tm, N), lambda i: (i, 0)),
        out_shape=jax.ShapeDtypeStruct((S, N), F32),
        compiler_params=_params("parallel"),
    )(*pieces, w, res)


def _nt_sum(p_refs, widths, w_ref):
    acc = None
    off = 0
    for p_ref, n_p in zip(p_refs, widths):
        d = _dot_nt(p_ref[...].astype(BF16), w_ref[:, off:off + n_p])
        acc = d if acc is None else acc + d
        off += n_p
    return acc


def _matmul_nt(pieces, w, out_dtype, *, tm=512):
    S = pieces[0].shape[0]
    K, N = w.shape
    widths = [p.shape[1] for p in pieces]
    assert sum(widths) == N

    def body(*refs):
        p_refs, (w_ref, o_ref) = refs[:len(pieces)], refs[len(pieces):]
        o_ref[...] = _nt_sum(p_refs, widths, w_ref).astype(out_dtype)

    return pl.pallas_call(
        body, name="matmul_nt", grid=(S // tm,),
        in_specs=[pl.BlockSpec((tm, n_p), lambda i: (i, 0)) for n_p in widths]
        + [pl.BlockSpec((K, N), lambda i: (0, 0))],
        out_specs=pl.BlockSpec((tm, K), lambda i: (i, 0)),
        out_shape=jax.ShapeDtypeStruct((S, K), out_dtype),
        compiler_params=_params("parallel"),
    )(*pieces, w)


def _matmul_nt_rmsbwd(pieces, w, x, g, dres, guest=None, *, tm=256):
    S, K = x.shape
    N = w.shape[1]
    widths = [p.shape[1] for p in pieces]
    assert sum(widths) == N

    def body(*refs):
        p_refs, (w_ref, x_ref, g_ref, dres_ref, dx_ref, dg_ref) = refs[:len(pieces)], refs[len(pieces):]
        dh = _nt_sum(p_refs, widths, w_ref)
        xv = x_ref[...]
        r = lax.rsqrt(jnp.mean(xv * xv, axis=-1, keepdims=True) + EPS)
        xh = xv * r
        dxh = dh * g_ref[...]
        dx_ref[...] = dres_ref[...] + r * (dxh - xh * jnp.mean(dxh * xh, axis=-1, keepdims=True))

        @pl.when(pl.program_id(0) == 0)
        def _():
            dg_ref[...] = jnp.zeros_like(dg_ref)

        dg_ref[...] += jnp.sum(dh * xh, axis=0, keepdims=True)

    (dx, dg), guest_out = _hosted_call(
        body, guest, name="matmul_nt_rmsbwd", grid=(S // tm,),
        in_specs=[pl.BlockSpec((tm, n_p), lambda i: (i, 0)) for n_p in widths]
        + [pl.BlockSpec((K, N), lambda i: (0, 0)), pl.BlockSpec((tm, K), lambda i: (i, 0)),
           pl.BlockSpec((1, K), lambda i: (0, 0)), pl.BlockSpec((tm, K), lambda i: (i, 0))],
        out_specs=[pl.BlockSpec((tm, K), lambda i: (i, 0)), pl.BlockSpec((1, K), lambda i: (0, 0))],
        out_shape=[jax.ShapeDtypeStruct((S, K), F32), jax.ShapeDtypeStruct((1, K), F32)],
        scratch_shapes=[], operands=[*pieces, w, x, g, dres])
    return dx, dg, guest_out


def _matmul_tn(xts, dys, *, tk, tn, ts=1024):
    S = dys[0].shape[0]
    xs = xts
    n_s = S // ts
    (mt,) = {x.shape[0] // tk for x in xts}
    (nt,) = {d.shape[1] // tn for d in dys}

    def body(*refs):
        x_refs, dy_refs, (o_ref, acc_ref) = refs[:len(xs)], refs[len(xs):len(xs) + len(dys)], refs[len(xs) + len(dys):]
        i, j, s = pl.program_id(0), pl.program_id(1), pl.program_id(2)

        @pl.when(s == 0)
        def _():
            acc_ref[...] = jnp.zeros_like(acc_ref)

        for a, x_ref in enumerate(x_refs):
            for b, dy_ref in enumerate(dy_refs):
                @pl.when((i // mt == a) & (j // nt == b))
                def _():
                    acc_ref[...] += _dot(x_ref[...], dy_ref[...].astype(BF16))

        @pl.when(s == n_s - 1)
        def _():
            o_ref[...] = acc_ref[...].astype(BF16)

    def x_map(a):
        return lambda i, j, s: (jnp.where(i // mt == a, i % mt, 0), jnp.where(i // mt == a, s, 0))

    def dy_map(b):
        return lambda i, j, s: (jnp.where(j // nt == b, s, 0), jnp.where(j // nt == b, j % nt, 0))

    return pl.pallas_call(
        body, name="matmul_tn", grid=(len(xs) * mt, len(dys) * nt, n_s),
        in_specs=[pl.BlockSpec((tk, ts), x_map(a)) for a in range(len(xs))]
        + [pl.BlockSpec((ts, tn), dy_map(b)) for b in range(len(dys))],
        out_specs=pl.BlockSpec((tk, tn), lambda i, j, s: (i, j)),
        out_shape=jax.ShapeDtypeStruct((len(xs) * mt * tk, len(dys) * nt * tn), BF16),
        scratch_shapes=[pltpu.VMEM((tk, tn), F32)],
        compiler_params=_params("parallel", "parallel", "arbitrary"),
    )(*xs, *dys)


def _tri_consts():
    j = jnp.arange(ATTN_TILE)[:, None]
    s = jnp.arange(ATTN_TILE)[None, :]
    return (j > s).astype(BF16), (j <= s).astype(BF16), (j < s).astype(BF16)


SIGN_BIT = 0x80000000


def _log_terms(sn):
    minus_abs = lax.bitcast_convert_type(lax.bitcast_convert_type(sn, jnp.uint32) | jnp.uint32(SIGN_BIT), F32)
    lom = jnp.minimum(sn, 0.0) - jnp.log(1.0 + jnp.exp(minus_abs))
    return lom, lom - sn


def _causal_mask():
    t = lax.broadcasted_iota(jnp.int32, (ATTN_TILE, ATTN_TILE), 0)
    s = lax.broadcasted_iota(jnp.int32, (ATTN_TILE, ATTN_TILE), 1)
    return s < t


def _attn_prep(h, q_ref, k_ref, v_ref, qg_ref, kg_ref, qn_s, kn_s, vb_s, n_tiles):
    T = ATTN_TILE
    lanes = slice(HEAD_DIM * h, HEAD_DIM * (h + 1))
    scale = -(HEAD_DIM ** -0.5)

    def prep(i, carry):
        rows = pl.ds(pl.multiple_of(i * T, T), T)
        q = q_ref[rows, lanes]
        k = k_ref[rows, lanes]
        rq = lax.rsqrt(jnp.mean(q * q, axis=-1, keepdims=True) + EPS)
        rk = lax.rsqrt(jnp.mean(k * k, axis=-1, keepdims=True) + EPS)
        qn_s[h, rows, :] = (q * rq * qg_ref[...] * scale).astype(BF16)
        kn_s[h, rows, :] = (k * rk * kg_ref[...]).astype(BF16)
        vb_s[h, rows, :] = v_ref[rows, lanes].astype(BF16)
        return carry

    lax.fori_loop(0, n_tiles, prep, 0)


def _attn_fwd(proj, qg, kg, fulls=(), kinds=()):
    S = proj.shape[0]
    n_comm = len(fulls)
    T = ATTN_TILE
    n_tiles = S // T
    suffix, _, _ = _tri_consts()
    pairs = HEADS // ATTN_HEADS_PER_STEP
    q_blk, k_blk, v_blk = 0, ATTN_WIDTH // ATTN_BLOCK, 2 * ATTN_WIDTH // ATTN_BLOCK

    def body(*refs):
        q_ref, k_ref, v_ref, qg_ref, kg_ref, tri_ref = refs[:6]
        o_ref, lt_ref, ot_ref = refs[6 + n_comm:9 + n_comm]
        w_refs = refs[9 + n_comm:9 + 2 * n_comm]
        qn_s, kn_s, vb_s = refs[9 + 2 * n_comm:12 + 2 * n_comm]
        sems = refs[12 + 2 * n_comm:]
        if n_comm:
            @pl.when(pl.program_id(0) == 0)
            def _():
                for cp in _gather_copies(w_refs, kinds, *sems, 0):
                    cp.start()

        heads = range(ATTN_HEADS_PER_STEP)
        for h in heads:
            _attn_prep(h, q_ref, k_ref, v_ref, qg_ref, kg_ref, qn_s, kn_s, vb_s, n_tiles)

        def q_tile(qi, carry0):
            qrows = pl.ds(pl.multiple_of(qi * T, T), T)
            qt = [qn_s[h, qrows, :] for h in heads]

            def tile(kj, carry, diag):
                krows = pl.ds(pl.multiple_of(kj * T, T), T)
                s = [_dot_nt(qt[h], kn_s[h, krows, :]) for h in heads]
                terms = [_log_terms(s[h]) for h in heads]
                lom = [terms[h][0] for h in heads]
                if diag:
                    mask = _causal_mask()
                    lom = [jnp.where(mask, lom[h], 0.0) for h in heads]
                lom = [lom[h].astype(BF16) for h in heads]
                tail = [_dot(lom[h], tri_ref[...]) for h in heads]
                w = [jnp.exp(terms[h][1] + tail[h] + carry[h][1]) for h in heads]
                if diag:
                    w = [jnp.where(mask, w[h], 0.0) for h in heads]
                return tuple((carry[h][0] + _dot(w[h].astype(BF16), vb_s[h, krows, :]),
                              carry[h][1] + (tail[h][:, 0:1] + lom[h][:, 0:1].astype(F32))) for h in heads)

            init = tuple((jnp.zeros((T, HEAD_DIM), F32), jnp.zeros((T, 1), F32)) for h in heads)
            carry = lax.fori_loop(0, qi, lambda t, cr: tile(qi - 1 - t, cr, False), tile(qi, init, True))
            o = jnp.concatenate([carry[h][0] for h in heads], axis=1)
            o_ref[qrows, :] = o.astype(BF16)
            ot_ref[:, qrows] = o.T.astype(BF16)
            for h in heads:
                lt_ref[qrows, HEAD_DIM * h:HEAD_DIM * (h + 1)] = jnp.broadcast_to(carry[h][1], (T, HEAD_DIM))
            return carry0

        lax.fori_loop(0, n_tiles, q_tile, 0)
        if n_comm:
            @pl.when(pl.program_id(0) == pairs - 1)
            def _():
                for cp in _gather_copies(w_refs, kinds, *sems, 0):
                    cp.wait()

    n_sem = n_comm * (N_CHIPS - 1)
    out = pl.pallas_call(
        body, name="attn_fwd_gather" if n_comm else "attn_fwd", grid=(pairs,),
        in_specs=[pl.BlockSpec((S, ATTN_BLOCK), lambda p: (0, q_blk + p)),
                  pl.BlockSpec((S, ATTN_BLOCK), lambda p: (0, k_blk + p)),
                  pl.BlockSpec((S, ATTN_BLOCK), lambda p: (0, v_blk + p)),
                  pl.BlockSpec((1, HEAD_DIM), lambda p: (0, 0)),
                  pl.BlockSpec((1, HEAD_DIM), lambda p: (0, 0)),
                  pl.BlockSpec((T, T), lambda p: (0, 0))] + [ANY] * n_comm,
        out_specs=[pl.BlockSpec((S, ATTN_BLOCK), lambda p: (0, p)),
                   pl.BlockSpec((None, S, ATTN_BLOCK), lambda p: (p, 0, 0)),
                   pl.BlockSpec((ATTN_BLOCK, S), lambda p: (p, 0))] + [ANY] * n_comm,
        out_shape=[jax.ShapeDtypeStruct((S, ATTN_WIDTH), BF16),
                   jax.ShapeDtypeStruct((pairs, S, ATTN_BLOCK), F32),
                   jax.ShapeDtypeStruct((ATTN_WIDTH, S), BF16)] + [jax.ShapeDtypeStruct(f.shape, f.dtype) for f in fulls],
        scratch_shapes=[pltpu.VMEM((ATTN_HEADS_PER_STEP, S, HEAD_DIM), BF16)] * 3
        + ([pltpu.SemaphoreType.DMA((n_sem,))] * 2 if n_comm else []),
        input_output_aliases={6 + a: 3 + a for a in range(n_comm)},
        compiler_params=pltpu.CompilerParams(dimension_semantics=("arbitrary",), vmem_limit_bytes=VMEM_LIMIT_BYTES,
                                             has_side_effects=bool(n_comm)),
    )(proj, proj, proj, qg, kg, suffix, *fulls)
    return out[0], out[1], out[2], list(out[3:])


def _attn_bwd(proj, qg, kg, ltot, dmix, parts=(), kinds=()):
    S = proj.shape[0]
    n_comm = len(parts)
    T = ATTN_TILE
    n_tiles = S // T
    _, prefix_incl, prefix_excl = _tri_consts()
    pairs = HEADS // ATTN_HEADS_PER_STEP
    q_blk, k_blk, v_blk = 0, ATTN_WIDTH // ATTN_BLOCK, 2 * ATTN_WIDTH // ATTN_BLOCK
    scale = HEAD_DIM ** -0.5

    def body(*refs):
        q_ref, k_ref, v_ref, qg_ref, kg_ref, lt_ref, do_ref, ti_ref, te_ref = refs[:9]
        p_refs = refs[9:9 + n_comm]
        dq_ref, dk_ref, dv_ref, dqg_ref, dkg_ref = refs[9 + n_comm:14 + n_comm]
        lb_refs = refs[14 + n_comm:14 + 2 * n_comm]
        qn_s, kn_s, vb_s, dq_s, dk_s, dv_s = refs[14 + 2 * n_comm:20 + 2 * n_comm]
        sems = refs[20 + 2 * n_comm:]

        @pl.when(pl.program_id(0) == 0)
        def _():
            dqg_ref[...] = jnp.zeros_like(dqg_ref)
            dkg_ref[...] = jnp.zeros_like(dkg_ref)
            for cp in _scatter_copies(p_refs, lb_refs, kinds, *sems) if n_comm else ():
                cp.start()

        heads = range(ATTN_HEADS_PER_STEP)
        for h in heads:
            _attn_prep(h, q_ref, k_ref, v_ref, qg_ref, kg_ref, qn_s, kn_s, vb_s, n_tiles)
        dk_s[...] = jnp.zeros_like(dk_s)
        dv_s[...] = jnp.zeros_like(dv_s)

        def q_tile(qi, carry0):
            qrows = pl.ds(pl.multiple_of(qi * T, T), T)
            qt = [qn_s[h, qrows, :] for h in heads]
            dob = [do_ref[qrows, HEAD_DIM * h:HEAD_DIM * (h + 1)].astype(BF16) for h in heads]
            lt = [lt_ref[qrows, HEAD_DIM * h:HEAD_DIM * h + 1] for h in heads]

            def tile(kj, carry, diag):
                krows = pl.ds(pl.multiple_of(kj * T, T), T)
                kt = [kn_s[h, krows, :] for h in heads]
                s = [_dot_nt(qt[h], kt[h]) for h in heads]
                dw = [_dot_nt(dob[h], vb_s[h, krows, :]) for h in heads]
                terms = [_log_terms(s[h]) for h in heads]
                lom = [terms[h][0] for h in heads]
                if diag:
                    mask = _causal_mask()
                    lom = [jnp.where(mask, lom[h], 0.0) for h in heads]
                pin = [_dot(lom[h].astype(BF16), ti_ref[...]) for h in heads]
                w = [jnp.exp(terms[h][1] + ((lt[h] - carry[h][1]) - pin[h])) for h in heads]
                if diag:
                    w = [jnp.where(mask, w[h], 0.0) for h in heads]
                e = [w[h] * dw[h] for h in heads]
                gex = [_dot(e[h].astype(BF16), te_ref[...]) for h in heads]
                dz = [e[h] - jnp.exp(terms[h][1]) * (e[h] + (carry[h][2] + gex[h])) for h in heads]
                if diag:
                    dz = [jnp.where(mask, dz[h], 0.0) for h in heads]
                new = []
                for h in heads:
                    dzb = dz[h].astype(BF16)
                    dk_s[h, krows, :] += _dot_tn(dzb, qt[h])
                    dv_s[h, krows, :] += _dot_tn(w[h].astype(BF16), dob[h])
                    new.append((carry[h][0] + _dot(dzb, kt[h]),
                                carry[h][1] + pin[h][:, T - 1:T],
                                carry[h][2] + gex[h][:, T - 1:T] + e[h][:, T - 1:T]))
                return tuple(new)

            zero = jnp.zeros((T, 1), F32)
            init = tuple((jnp.zeros((T, HEAD_DIM), F32), zero, zero) for h in heads)
            last = tile(qi, lax.fori_loop(0, qi, lambda kj, cr: tile(kj, cr, False), init), True)
            for h in heads:
                dq_s[h, qrows, :] = last[h][0]
            return carry0

        lax.fori_loop(0, n_tiles, q_tile, 0)

        def finish(i, carry):
            rows = pl.ds(pl.multiple_of(i * T, T), T)
            new = []
            for h in heads:
                lanes = slice(HEAD_DIM * h, HEAD_DIM * (h + 1))
                q = q_ref[rows, lanes]
                k = k_ref[rows, lanes]
                rq = lax.rsqrt(jnp.mean(q * q, axis=-1, keepdims=True) + EPS)
                rk = lax.rsqrt(jnp.mean(k * k, axis=-1, keepdims=True) + EPS)
                qh = q * rq
                kh = k * rk
                dqn = dq_s[h, rows, :] * scale
                dkn = -dk_s[h, rows, :]
                dqh = dqn * qg_ref[...]
                dkh = dkn * kg_ref[...]
                dq_ref[rows, lanes] = (rq * (dqh - qh * jnp.mean(dqh * qh, axis=-1, keepdims=True))).astype(BF16)
                dk_ref[rows, lanes] = (rk * (dkh - kh * jnp.mean(dkh * kh, axis=-1, keepdims=True))).astype(BF16)
                dv_ref[rows, lanes] = dv_s[h, rows, :].astype(BF16)
                new.append(carry[2 * h] + jnp.sum(dqn * qh, axis=0, keepdims=True))
                new.append(carry[2 * h + 1] + jnp.sum(dkn * kh, axis=0, keepdims=True))
            return tuple(new)

        zero = jnp.zeros((1, HEAD_DIM), F32)
        sums = lax.fori_loop(0, n_tiles, finish, (zero,) * (2 * len(heads)))
        dqg_ref[...] += sum(sums[0::2])
        dkg_ref[...] += sum(sums[1::2])
        if n_comm:
            @pl.when(pl.program_id(0) == pairs - 1)
            def _():
                for cp in _scatter_copies(p_refs, lb_refs, kinds, *sems):
                    cp.wait()

    blk = lambda off: pl.BlockSpec((S, ATTN_BLOCK), lambda p: (0, off + p))
    row64 = pl.BlockSpec((1, HEAD_DIM), lambda p: (0, 0))
    tri = pl.BlockSpec((T, T), lambda p: (0, 0))
    n_sem = n_comm * (N_CHIPS - 1)
    out = pl.pallas_call(
        body, name="attn_bwd_scatter" if n_comm else "attn_bwd", grid=(pairs,),
        in_specs=[blk(q_blk), blk(k_blk), blk(v_blk), row64, row64,
                  pl.BlockSpec((None, S, ATTN_BLOCK), lambda p: (p, 0, 0)), blk(0), tri, tri] + [ANY] * n_comm,
        out_specs=[blk(0), blk(0), blk(0), row64, row64] + [ANY] * n_comm,
        out_shape=[jax.ShapeDtypeStruct((S, ATTN_WIDTH), BF16)] * 3 + [jax.ShapeDtypeStruct((1, HEAD_DIM), F32)] * 2
        + _scatter_shapes(parts, kinds),
        scratch_shapes=[pltpu.VMEM((ATTN_HEADS_PER_STEP, S, HEAD_DIM), BF16)] * 3
        + [pltpu.VMEM((ATTN_HEADS_PER_STEP, S, HEAD_DIM), F32)] * 3
        + ([pltpu.SemaphoreType.DMA((n_sem,))] * 2 if n_comm else []),
        compiler_params=pltpu.CompilerParams(dimension_semantics=("arbitrary",), vmem_limit_bytes=VMEM_LIMIT_BYTES,
                                             has_side_effects=bool(n_comm)),
    )(proj, proj, proj, qg, kg, ltot, dmix, prefix_incl, prefix_excl, *parts)
    return out[:5], list(out[5:])


def _shifted(win, n_rows):
    return [win if b == 0 else pltpu.roll(win, n_rows - b, 0) for b in range(SUBLANES)]


def _taps(variants, offsets, tm):
    return {o: variants[o % SUBLANES][(o // SUBLANES) * SUBLANES:(o // SUBLANES) * SUBLANES + tm, :] for o in offsets}


def _fold_rows(a):
    return jnp.sum(a.reshape(a.shape[0] // SUBLANES, SUBLANES, a.shape[1]), axis=0)


def _glu_conv_fwd(proj, w, bias, guest=None, *, tm=256):
    S = proj.shape[0]
    CB = LANES
    a_blk, b_blk = 3 * ATTN_WIDTH // CB, (3 * ATTN_WIDTH + CONV_WIDTH) // CB
    n_rows = tm + CONV_PAD

    def body(a_ref, b_ref, w_ref, bias_ref, c1_ref, pad_s):
        pad_s[0:CONV_PAD, :] = jnp.zeros((CONV_PAD, CB), F32)

        def fill(i, carry):
            rows = pl.ds(pl.multiple_of(i * tm, tm), tm)
            pad_s[pl.ds(pl.multiple_of(CONV_PAD + i * tm, SUBLANES), tm), :] = a_ref[rows, :] * _sigmoid(b_ref[rows, :])
            return carry

        lax.fori_loop(0, S // tm, fill, 0)

        def conv(i, carry):
            r0 = pl.multiple_of(i * tm, tm)
            taps = _taps(_shifted(pad_s[pl.ds(r0, n_rows), :], n_rows), range(2, 2 + CONV_KERNEL), tm)
            acc = jnp.broadcast_to(bias_ref[...], (tm, CB))
            for k in range(CONV_KERNEL):
                acc = acc + w_ref[k:k + 1, :] * taps[k + 2]
            c1_ref[pl.ds(r0, tm), :] = acc
            return carry

        lax.fori_loop(0, S // tm, conv, 0)

    (c1,), guest_out = _hosted_call(
        body, guest, name="glu_conv_fwd", grid=(CONV_WIDTH // CB,),
        in_specs=[pl.BlockSpec((S, CB), lambda j: (0, a_blk + j)), pl.BlockSpec((S, CB), lambda j: (0, b_blk + j)),
                  pl.BlockSpec((CONV_KERNEL, CB), lambda j: (0, j)), pl.BlockSpec((1, CB), lambda j: (0, j))],
        out_specs=[pl.BlockSpec((S, CB), lambda j: (0, j))],
        out_shape=[jax.ShapeDtypeStruct((S, CONV_WIDTH), F32)],
        scratch_shapes=[pltpu.VMEM((S + CONV_PAD, CB), F32)], operands=[proj, proj, w, bias])
    return c1, guest_out


def _glu_conv_bwd(proj, w, dc1, guest=None, *, tm=256):
    S = proj.shape[0]
    CB = LANES
    a_blk, b_blk = 3 * ATTN_WIDTH // CB, (3 * ATTN_WIDTH + CONV_WIDTH) // CB
    n_rows = tm + CONV_PAD

    def body(a_ref, b_ref, w_ref, dc1_ref, da_ref, db_ref, dw_ref, dbias_ref, pad_s, dpad_s, dw_s):
        pad_s[0:CONV_PAD, :] = jnp.zeros((CONV_PAD, CB), F32)
        dpad_s[S:S + CONV_PAD, :] = jnp.zeros((CONV_PAD, CB), F32)
        dw_s[...] = jnp.zeros_like(dw_s)

        def fill(i, carry):
            rows = pl.ds(pl.multiple_of(i * tm, tm), tm)
            pad_s[pl.ds(pl.multiple_of(CONV_PAD + i * tm, SUBLANES), tm), :] = a_ref[rows, :] * _sigmoid(b_ref[rows, :])
            dpad_s[rows, :] = dc1_ref[rows, :]
            return carry

        lax.fori_loop(0, S // tm, fill, 0)

        def conv(i, carry):
            r0 = pl.multiple_of(i * tm, tm)
            rows = pl.ds(r0, tm)
            taps = _taps(_shifted(dpad_s[pl.ds(r0, n_rows), :], n_rows), range(CONV_KERNEL), tm)
            acc = jnp.zeros((tm, CB), F32)
            for k in range(CONV_KERNEL):
                acc = acc + w_ref[k:k + 1, :] * taps[CONV_KERNEL - 1 - k]
            a = a_ref[rows, :]
            sg = _sigmoid(b_ref[rows, :])
            da_ref[rows, :] = (acc * sg).astype(BF16)
            db_ref[rows, :] = (acc * a * sg * (1.0 - sg)).astype(BF16)
            d = taps[0]
            taps = _taps(_shifted(pad_s[pl.ds(r0, n_rows), :], n_rows), range(2, 2 + CONV_KERNEL), tm)
            for k in range(CONV_KERNEL):
                dw_s[SUBLANES * k:SUBLANES * (k + 1), :] += _fold_rows(d * taps[k + 2])
            dw_s[SUBLANES * CONV_KERNEL:SUBLANES * (CONV_KERNEL + 1), :] += _fold_rows(d)
            return carry

        lax.fori_loop(0, S // tm, conv, 0)
        for k in range(CONV_KERNEL):
            dw_ref[k:k + 1, :] = jnp.sum(dw_s[SUBLANES * k:SUBLANES * (k + 1), :], axis=0, keepdims=True)
        dbias_ref[...] = jnp.sum(dw_s[SUBLANES * CONV_KERNEL:SUBLANES * (CONV_KERNEL + 1), :], axis=0, keepdims=True)

    col = lambda off: pl.BlockSpec((S, CB), lambda j: (0, off + j))
    out, guest_out = _hosted_call(
        body, guest, name="glu_conv_bwd", grid=(CONV_WIDTH // CB,),
        in_specs=[col(a_blk), col(b_blk), pl.BlockSpec((CONV_KERNEL, CB), lambda j: (0, j)), col(0)],
        out_specs=[col(0), col(0), pl.BlockSpec((CONV_KERNEL, CB), lambda j: (0, j)), pl.BlockSpec((1, CB), lambda j: (0, j))],
        out_shape=[jax.ShapeDtypeStruct((S, CONV_WIDTH), BF16)] * 2
        + [jax.ShapeDtypeStruct((CONV_KERNEL, CONV_WIDTH), F32), jax.ShapeDtypeStruct((1, CONV_WIDTH), F32)],
        scratch_shapes=[pltpu.VMEM((S + CONV_PAD, CB), F32), pltpu.VMEM((S + CONV_PAD, CB), F32),
                        pltpu.VMEM((SUBLANES * (CONV_KERNEL + 1), CB), F32)], operands=[proj, proj, w, dc1])
    return (*out, guest_out)


def _ln_stats(c1):
    mu = jnp.mean(c1, axis=-1, keepdims=True)
    xc = c1 - mu
    r = lax.rsqrt(jnp.mean(xc * xc, axis=-1, keepdims=True) + EPS)
    return xc * r, r


def _ln_silu_fwd(c1, g, b, *, tm=512):
    S, C = c1.shape

    def body(c1_ref, g_ref, b_ref, c_ref, ct_ref):
        yh, _ = _ln_stats(c1_ref[...])
        y = yh * g_ref[...] + b_ref[...]
        c = y * _sigmoid(y)
        c_ref[...] = c.astype(BF16)
        ct_ref[...] = c.T.astype(BF16)

    vec = pl.BlockSpec((1, C), lambda i: (0, 0))
    return pl.pallas_call(
        body, name="ln_silu_fwd", grid=(S // tm,),
        in_specs=[pl.BlockSpec((tm, C), lambda i: (i, 0)), vec, vec],
        out_specs=[pl.BlockSpec((tm, C), lambda i: (i, 0)), pl.BlockSpec((C, tm), lambda i: (0, i))],
        out_shape=[jax.ShapeDtypeStruct((S, C), BF16), jax.ShapeDtypeStruct((C, S), BF16)],
        compiler_params=_params("parallel"),
    )(c1, g, b)


def _ln_silu_bwd(c1, g, b, dmix, *, tm=512):
    S, C = c1.shape

    def body(c1_ref, g_ref, b_ref, dc_ref, dc1_ref, dg_ref, db_ref):
        yh, r = _ln_stats(c1_ref[...])
        y = yh * g_ref[...] + b_ref[...]
        sg = _sigmoid(y)
        dy = dc_ref[...] * (sg * (1.0 + y * (1.0 - sg)))
        dyh = dy * g_ref[...]
        dc1_ref[...] = r * (dyh - jnp.mean(dyh, axis=-1, keepdims=True)
                            - yh * jnp.mean(dyh * yh, axis=-1, keepdims=True))

        @pl.when(pl.program_id(0) == 0)
        def _():
            dg_ref[...] = jnp.zeros_like(dg_ref)
            db_ref[...] = jnp.zeros_like(db_ref)

        dg_ref[...] += jnp.sum(dy * yh, axis=0, keepdims=True)
        db_ref[...] += jnp.sum(dy, axis=0, keepdims=True)

    vec = pl.BlockSpec((1, C), lambda i: (0, 0))
    return pl.pallas_call(
        body, name="ln_silu_bwd", grid=(S // tm,),
        in_specs=[pl.BlockSpec((tm, C), lambda i: (i, 0)), vec, vec, pl.BlockSpec((tm, C), lambda i: (i, 1))],
        out_specs=[pl.BlockSpec((tm, C), lambda i: (i, 0)), vec, vec],
        out_shape=[jax.ShapeDtypeStruct((S, C), F32), jax.ShapeDtypeStruct((1, C), F32), jax.ShapeDtypeStruct((1, C), F32)],
        compiler_params=_params("arbitrary"),
    )(c1, g, b, dmix)


FFN_CB = 256


def _ffn_gate(pad_s, w_ref, bias_ref, r0, tm):
    n_rows = tm + FFN_PAD
    taps = _taps(_shifted(pad_s[pl.ds(r0, n_rows), :], n_rows), range(FFN_PAD - 2, FFN_PAD + 1), tm)
    g1 = bias_ref[...] + w_ref[0:1, :] * taps[6] + w_ref[1:2, :] * taps[7] + w_ref[2:3, :] * taps[8]
    return g1, taps


def _ffn_act_fwd(u, w, bias, *, tm=256):
    S = u.shape[0]
    CB = FFN_CB
    nb = D_FF // CB

    def body(g_ref, v_ref, w_ref, bias_ref, o_ref, ot_ref, pad_s):
        pad_s[0:FFN_PAD, :] = jnp.zeros((FFN_PAD, CB), F32)

        def fill(i, carry):
            pad_s[pl.ds(pl.multiple_of(FFN_PAD + i * tm, SUBLANES), tm), :] = g_ref[pl.ds(pl.multiple_of(i * tm, tm), tm), :]
            return carry

        lax.fori_loop(0, S // tm, fill, 0)

        def act(i, carry):
            r0 = pl.multiple_of(i * tm, tm)
            g1, _ = _ffn_gate(pad_s, w_ref, bias_ref, r0, tm)
            a = g1 * _sigmoid(g1) * v_ref[pl.ds(r0, tm), :]
            o_ref[pl.ds(r0, tm), :] = a.astype(BF16)
            ot_ref[:, pl.ds(r0, tm)] = a.T.astype(BF16)
            return carry

        lax.fori_loop(0, S // tm, act, 0)

    return pl.pallas_call(
        body, name="ffn_act_fwd", grid=(nb,),
        in_specs=[pl.BlockSpec((S, CB), lambda j: (0, j)), pl.BlockSpec((S, CB), lambda j: (0, nb + j)),
                  pl.BlockSpec((FFN_KERNEL, CB), lambda j: (0, j)), pl.BlockSpec((1, CB), lambda j: (0, j))],
        out_specs=[pl.BlockSpec((S, CB), lambda j: (0, j)), pl.BlockSpec((CB, S), lambda j: (j, 0))],
        out_shape=[jax.ShapeDtypeStruct((S, D_FF), BF16), jax.ShapeDtypeStruct((D_FF, S), BF16)],
        scratch_shapes=[pltpu.VMEM((S + FFN_PAD, CB), F32)],
        compiler_params=_params("parallel"),
    )(u, u, w, bias)


def _ffn_act_bwd(u, w, bias, dact, *, tm=256):
    S = u.shape[0]
    CB = FFN_CB
    nb = D_FF // CB

    def body(g_ref, v_ref, w_ref, bias_ref, da_ref, dg_ref, dv_ref, dw_ref, dbias_ref, pad_s, dpad_s, dw_s):
        pad_s[0:FFN_PAD, :] = jnp.zeros((FFN_PAD, CB), F32)
        dpad_s[S:S + FFN_PAD, :] = jnp.zeros((FFN_PAD, CB), F32)
        dw_s[...] = jnp.zeros_like(dw_s)

        def fill(i, carry):
            pad_s[pl.ds(pl.multiple_of(FFN_PAD + i * tm, SUBLANES), tm), :] = g_ref[pl.ds(pl.multiple_of(i * tm, tm), tm), :]
            return carry

        lax.fori_loop(0, S // tm, fill, 0)

        def first(i, carry):
            r0 = pl.multiple_of(i * tm, tm)
            rows = pl.ds(r0, tm)
            g1, taps = _ffn_gate(pad_s, w_ref, bias_ref, r0, tm)
            sg = _sigmoid(g1)
            da = da_ref[rows, :].astype(F32)
            dv_ref[rows, :] = (da * g1 * sg).astype(BF16)
            dg1 = da * v_ref[rows, :] * (sg * (1.0 + g1 * (1.0 - sg)))
            dpad_s[rows, :] = dg1
            for k in range(FFN_KERNEL):
                dw_s[SUBLANES * k:SUBLANES * (k + 1), :] += _fold_rows(dg1 * taps[FFN_PAD - 2 + k])
            dw_s[SUBLANES * FFN_KERNEL:SUBLANES * (FFN_KERNEL + 1), :] += _fold_rows(dg1)
            return carry

        lax.fori_loop(0, S // tm, first, 0)

        def second(i, carry):
            r0 = pl.multiple_of(i * tm, tm)
            n_rows = tm + FFN_PAD
            taps = _taps(_shifted(dpad_s[pl.ds(r0, n_rows), :], n_rows), range(FFN_KERNEL), tm)
            dg_ref[pl.ds(r0, tm), :] = (w_ref[2:3, :] * taps[0] + w_ref[1:2, :] * taps[1] + w_ref[0:1, :] * taps[2]).astype(BF16)
            return carry

        lax.fori_loop(0, S // tm, second, 0)
        for k in range(FFN_KERNEL):
            dw_ref[k:k + 1, :] = jnp.sum(dw_s[SUBLANES * k:SUBLANES * (k + 1), :], axis=0, keepdims=True)
        dbias_ref[...] = jnp.sum(dw_s[SUBLANES * FFN_KERNEL:SUBLANES * (FFN_KERNEL + 1), :], axis=0, keepdims=True)

    col = lambda off: pl.BlockSpec((S, CB), lambda j: (0, off + j))
    wspec = pl.BlockSpec((FFN_KERNEL, CB), lambda j: (0, j))
    bspec = pl.BlockSpec((1, CB), lambda j: (0, j))
    return pl.pallas_call(
        body, name="ffn_act_bwd", grid=(nb,),
        in_specs=[col(0), col(nb), wspec, bspec, col(0)],
        out_specs=[col(0), col(0), wspec, bspec],
        out_shape=[jax.ShapeDtypeStruct((S, D_FF), BF16)] * 2
        + [jax.ShapeDtypeStruct((FFN_KERNEL, D_FF), F32), jax.ShapeDtypeStruct((1, D_FF), F32)],
        scratch_shapes=[pltpu.VMEM((S + FFN_PAD, CB), F32), pltpu.VMEM((S + FFN_PAD, CB), F32),
                        pltpu.VMEM((SUBLANES * (FFN_KERNEL + 1), CB), F32)],
        compiler_params=_params("parallel"),
    )(u, u, w, bias, dact)


def _loss_grad(y, target, *, tm=512):
    S, D = y.shape

    def body(y_ref, t_ref, dy_ref, l_ref):
        d = y_ref[...] - t_ref[...]
        dy_ref[...] = d * (1.0 / D)

        @pl.when(pl.program_id(0) == 0)
        def _():
            l_ref[...] = jnp.zeros_like(l_ref)

        l_ref[...] += 0.5 * jnp.sum(jnp.mean(d * d, axis=-1, keepdims=True), axis=0, keepdims=True)

    dy, l = pl.pallas_call(
        body, name="loss_grad", grid=(S // tm,),
        in_specs=[pl.BlockSpec((tm, D), lambda i: (i, 0))] * 2,
        out_specs=[pl.BlockSpec((tm, D), lambda i: (i, 0)), pl.BlockSpec((SUBLANES, LANES), lambda i: (0, 0))],
        out_shape=[jax.ShapeDtypeStruct((S, D), F32), jax.ShapeDtypeStruct((SUBLANES, LANES), F32)],
        compiler_params=_params("arbitrary"),
    )(y, target)
    return dy, l[0, 0]


def _row_tile(rows, cap=512):
    t = min(rows, cap)
    while rows % t or t % SUBLANES:
        t -= 1
    return t


def _adam_update(w, g, m, v):
    m1 = ADAM_B1 * m + (1.0 - ADAM_B1) * g
    v1 = ADAM_B2 * v + (1.0 - ADAM_B2) * (g * g)
    m_hat = m1 / (1.0 - ADAM_B1 ** ADAM_STEP)
    v_hat = v1 / (1.0 - ADAM_B2 ** ADAM_STEP)
    return -ADAM_LR * (m_hat / (jnp.sqrt(v_hat) + ADAM_EPS) + ADAM_WD * w), m1, v1


def _adamw(w, g, m, v):
    R, C = w.shape
    tr = _row_tile(R, 256)

    def body(w_ref, g_ref, m_ref, v_ref, d_ref, nm_ref, nv_ref):
        d_ref[...], nm_ref[...], nv_ref[...] = _adam_update(w_ref[...], g_ref[...], m_ref[...], v_ref[...])

    spec = pl.BlockSpec((tr, C), lambda i: (i, 0))
    return pl.pallas_call(
        body, name="adamw", grid=(R // tr,),
        in_specs=[spec] * 4, out_specs=[spec] * 3,
        out_shape=[jax.ShapeDtypeStruct((R, C), F32)] * 3,
        compiler_params=_params("parallel"),
    )(w, g, m, v)


def _adamw_layer(layer, w, g, m, v, prev):
    L, R, C = w.shape
    tr = _row_tile(R, 256)
    n_prev = 4 if prev else 0

    def body(*refs):
        w_ref, g_ref, m_ref, v_ref = refs[:4]
        go_ref, d_ref, nm_ref, nv_ref = refs[4 + n_prev:8 + n_prev]
        gv = g_ref[...]
        go_ref[...] = gv
        d_ref[...], nm_ref[...], nv_ref[...] = _adam_update(w_ref[...], gv, m_ref[...], v_ref[...])

    stacked = pl.BlockSpec((None, tr, C), lambda i: (layer, i, 0))
    return pl.pallas_call(
        body, name="adamw_layer", grid=(R // tr,),
        in_specs=[stacked, pl.BlockSpec((tr, C), lambda i: (i, 0)), stacked, stacked] + [ANY] * n_prev,
        out_specs=[stacked] * 4, out_shape=[jax.ShapeDtypeStruct((L, R, C), F32)] * 4,
        input_output_aliases={4 + j: j for j in range(n_prev)},
        compiler_params=_params("parallel"),
    )(w, g, m, v, *(prev or ()))


def _cast_into_full(w, layer, kind, chip):
    _, R, C = w.shape
    tr = _row_tile(R, 512)

    def body(chip_ref, w_ref, o_ref):
        o_ref[...] = w_ref[...].astype(BF16)

    if kind == "col":
        out_shape = jax.ShapeDtypeStruct((1, R, N_CHIPS * C), BF16)
        out_spec = pl.BlockSpec((None, tr, C), lambda i, chip_ref: (0, i, chip_ref[0]))
    else:
        out_shape = jax.ShapeDtypeStruct((1, N_CHIPS * R, C), BF16)
        out_spec = pl.BlockSpec((None, tr, C), lambda i, chip_ref: (0, chip_ref[0] * (R // tr) + i, 0))
    return pl.pallas_call(
        body, name="cast_into_full_" + kind,
        grid_spec=pltpu.PrefetchScalarGridSpec(
            num_scalar_prefetch=1, grid=(R // tr,),
            in_specs=[pl.BlockSpec((None, tr, C), lambda i, chip_ref: (layer, i, 0))], out_specs=out_spec),
        out_shape=out_shape, compiler_params=_params("parallel"),
    )(chip, w)


def _add_half(g4, la, c):
    L, _, H, W = g4.shape
    th = _row_tile(H, 256)

    def body(c_ref, g_ref, la_ref, o_ref):
        o_ref[...] = (g_ref[...].astype(F32) + la_ref[...].astype(F32)).astype(BF16)

    return pl.pallas_call(
        body, name="add_half",
        grid_spec=pltpu.PrefetchScalarGridSpec(
            num_scalar_prefetch=1, grid=(L, H // th),
            in_specs=[pl.BlockSpec((None, None, th, W), lambda l, i, c_ref: (l, c_ref[0], i, 0)),
                      pl.BlockSpec((None, th, W), lambda l, i, c_ref: (l, i, 0))],
            out_specs=pl.BlockSpec((None, th, W), lambda l, i, c_ref: (l, i, 0))),
        out_shape=jax.ShapeDtypeStruct((L, H, W), BF16),
        compiler_params=_params("parallel", "parallel"),
    )(c, g4, la)


def _add_parts(p, lb, place, kind):
    _, L, H, C = lb.shape
    th = _row_tile(H, 256)

    def body(s_ref, p_ref, lb_ref, o_ref):
        acc = p_ref[...].astype(F32)
        for k in range(N_CHIPS - 1):
            acc = acc + lb_ref[k].astype(F32)
        o_ref[...] = acc

    if kind == "col":
        p_spec = pl.BlockSpec((None, th, C), lambda l, i, s_ref: (l, i, s_ref[0]))
    else:
        p_spec = pl.BlockSpec((None, None, th, C), lambda l, i, s_ref: (l, s_ref[0], i, 0))
    return pl.pallas_call(
        body, name="add_parts_" + kind,
        grid_spec=pltpu.PrefetchScalarGridSpec(
            num_scalar_prefetch=1, grid=(L, H // th),
            in_specs=[p_spec, pl.BlockSpec((N_CHIPS - 1, None, th, C), lambda l, i, s_ref: (0, l, i, 0))],
            out_specs=pl.BlockSpec((None, None, th, C), lambda l, i, s_ref: (l, s_ref[1], i, 0))),
        out_shape=jax.ShapeDtypeStruct((L, 2, H, C), F32),
        compiler_params=_params("parallel", "parallel"),
    )(place, p, lb)


ANY = pl.BlockSpec(memory_space=pl.ANY)


def _place():
    x, y, c = lax.axis_index("x"), lax.axis_index("y"), lax.axis_index("c")
    chips = [(1 - x, y), (x, 1 - y), (1 - x, 1 - y)]
    return x, y, c, chips


def _comm_call(body, name, ins, out_shape, n_remote, n_local, aliases=None):
    scratch = [pltpu.SemaphoreType.DMA((n_remote,)), pltpu.SemaphoreType.DMA((n_remote,))]
    if n_local:
        scratch.append(pltpu.SemaphoreType.DMA((n_local,)))
    return pl.pallas_call(
        body, name=name, in_specs=[ANY] * len(ins), out_specs=[ANY] * len(out_shape), out_shape=out_shape,
        scratch_shapes=scratch, input_output_aliases=aliases or {},
        compiler_params=pltpu.CompilerParams(has_side_effects=True),
    )(*ins)


def _remote(src, dst, send, recv, k, to):
    return pltpu.make_async_remote_copy(src_ref=src, dst_ref=dst, send_sem=send.at[k], recv_sem=recv.at[k],
                                        device_id=to, device_id_type=MESH)


def _gather_weights(fulls, kinds):
    n = len(fulls)
    out_shape = [jax.ShapeDtypeStruct(f.shape, f.dtype) for f in fulls]

    def body(*refs):
        outs, (send, recv) = refs[n:2 * n], refs[2 * n:]
        first = _gather_copies(outs, kinds, send, recv, 0)
        for cp in first:
            cp.start()
        for cp in first:
            cp.wait()
        passed = _pass_on_copies(outs, kinds, send, recv, len(first))
        for cp in passed:
            cp.start()
        for cp in passed:
            cp.wait()

    return _comm_call(body, "gather_weights", fulls, out_shape, 2 * n * (N_CHIPS - 1), 0, {a: a for a in range(n)})


def _window(ref, kind, s, h):
    if kind == "col":
        H, C = ref.shape[1] // 2, ref.shape[2] // N_CHIPS
        return ref.at[:, pl.ds(pl.multiple_of(h * H, 16), H), pl.ds(pl.multiple_of(s * C, LANES), C)]
    R = ref.shape[1] // N_CHIPS
    return ref.at[:, pl.ds(pl.multiple_of(s * R + h * (R // 2), 16), R // 2), :]


def _gather_copies(outs, kinds, send, recv, sem0):
    x, y, c, chips = _place()
    me = 2 * x + y
    return [_remote(_window(o, kind, me, c), _window(o, kind, me, c), send, recv, sem0 + a * (N_CHIPS - 1) + k, (*chip, c))
            for a, (o, kind) in enumerate(zip(outs, kinds)) for k, chip in enumerate(chips)]


def _pass_on_copies(outs, kinds, send, recv, sem0):
    x, y, c, chips = _place()
    cps = []
    for a, (o, kind) in enumerate(zip(outs, kinds)):
        for k, chip in enumerate(chips):
            landed = _window(o, kind, 2 * chip[0] + chip[1], c)
            cps.append(_remote(landed, landed, send, recv, sem0 + a * (N_CHIPS - 1) + k, (x, y, 1 - c)))
    return cps


def _pass_on(fulls, kinds):
    n = len(fulls)
    out_shape = [jax.ShapeDtypeStruct(f.shape, f.dtype) for f in fulls]

    def body(*refs):
        outs, (send, recv) = refs[n:2 * n], refs[2 * n:]
        cps = _pass_on_copies(outs, kinds, send, recv, 0)
        for cp in cps:
            cp.start()
        for cp in cps:
            cp.wait()

    return _comm_call(body, "pass_on", fulls, out_shape, n * (N_CHIPS - 1), 0, {a: a for a in range(n)})


def _gather_small(shards):
    n = len(shards)
    out_shape = [jax.ShapeDtypeStruct((N_CHIPS,) + s.shape, s.dtype) for s in shards]

    def body(*refs):
        srcs, outs, (send, recv, loc) = refs[:n], refs[n:2 * n], refs[2 * n:]
        x, y, c, chips = _place()
        me = 2 * x + y
        remote, local = [], []
        for a in range(n):
            local.append(pltpu.make_async_copy(srcs[a], outs[a].at[me], loc.at[a]))
            for k, chip in enumerate(chips):
                remote.append(_remote(srcs[a], outs[a].at[me], send, recv, a * (N_CHIPS - 1) + k, (*chip, c)))
        for cp in local + remote:
            cp.start()
        for cp in remote + local:
            cp.wait()

    return _comm_call(body, "gather_small", shards, out_shape, n * (N_CHIPS - 1), n)


def _exchange_halves(g4s):
    n = len(g4s)
    out_shape = [jax.ShapeDtypeStruct((g.shape[0],) + g.shape[2:], g.dtype) for g in g4s]

    def body(*refs):
        gs, las, (send, recv) = refs[:n], refs[n:2 * n], refs[2 * n:]
        x, y, c, _ = _place()
        cps = [_remote(gs[a].at[:, 1 - c], las[a], send, recv, a, (x, y, 1 - c)) for a in range(n)]
        for cp in cps:
            cp.start()
        for cp in cps:
            cp.wait()

    return _comm_call(body, "exchange_halves", g4s, out_shape, n, 0)


def _scatter_partials(ps, kinds):
    n = len(ps)

    def body(*refs):
        srcs, lbs, (send, recv) = refs[:n], refs[n:2 * n], refs[2 * n:]
        cps = _scatter_copies(srcs, lbs, kinds, send, recv)
        for cp in cps:
            cp.start()
        for cp in cps:
            cp.wait()

    return _comm_call(body, "scatter_partials", ps, _scatter_shapes(ps, kinds), n * (N_CHIPS - 1), 0)


def _scatter_shapes(ps, kinds):
    out_shape = []
    for p, kind in zip(ps, kinds):
        L, H, C = (p.shape[0], p.shape[1], p.shape[2] // N_CHIPS) if kind == "col" else (p.shape[0], p.shape[2], p.shape[3])
        out_shape.append(jax.ShapeDtypeStruct((N_CHIPS - 1, L, H, C), p.dtype))
    return out_shape


def _scatter_copies(srcs, lbs, kinds, send, recv):
    x, y, c, chips = _place()
    cps = []
    for a, (src, lb, kind) in enumerate(zip(srcs, lbs, kinds)):
        C = lb.shape[3]
        for k, chip in enumerate(chips):
            s = 2 * chip[0] + chip[1]
            part = src.at[:, :, pl.ds(pl.multiple_of(s * C, LANES), C)] if kind == "col" else src.at[:, s]
            cps.append(_remote(part, lb.at[k], send, recv, a * (N_CHIPS - 1) + k, (*chip, c)))
    return cps


def _share_halves(g4s):
    n = len(g4s)
    out_shape = [jax.ShapeDtypeStruct(g.shape, g.dtype) for g in g4s]

    def body(*refs):
        outs, (send, recv) = refs[n:2 * n], refs[2 * n:]
        x, y, c, _ = _place()
        cps = [_remote(outs[a].at[:, c], outs[a].at[:, c], send, recv, a, (x, y, 1 - c)) for a in range(n)]
        for cp in cps:
            cp.start()
        for cp in cps:
            cp.wait()

    return _comm_call(body, "share_halves", g4s, out_shape, n, 0, {a: a for a in range(n)})


def _allreduce_small(part):
    N = part.shape[0]

    def body(p_ref, o_ref, buf, send, recv):
        x, y, c, _ = _place()
        me = 4 * x + 2 * y + c
        buf[me] = p_ref[...]
        cps = []
        for k in range(1, N_DEV):
            peer = (1 - x if k & 4 else x, 1 - y if k & 2 else y, 1 - c if k & 1 else c)
            cps.append(_remote(p_ref, buf.at[me], send, recv, k - 1, peer))
        for cp in cps:
            cp.start()
        for cp in cps:
            cp.wait()
        acc = buf[0]
        for i in range(1, N_DEV):
            acc = acc + buf[i]
        o_ref[...] = acc

    vmem = pl.BlockSpec(memory_space=pltpu.VMEM)
    return pl.pallas_call(
        body, name="allreduce_small", in_specs=[vmem], out_specs=vmem,
        out_shape=jax.ShapeDtypeStruct((N, LANES), F32),
        scratch_shapes=[pltpu.VMEM((N_DEV, N, LANES), F32), pltpu.SemaphoreType.DMA((N_DEV - 1,)),
                        pltpu.SemaphoreType.DMA((N_DEV - 1,))],
        compiler_params=pltpu.CompilerParams(has_side_effects=True, vmem_limit_bytes=VMEM_LIMIT_BYTES),
    )(part)


AFTER_ATTENTION = ("w_out", "w_up", "w_down")


def _layer_fwd(x, p, fulls=(), kinds=()):
    h1t, proj = _norm_matmul(x, p["norm1_g"], p["w_in"])
    attn, ltot, attnt, filled = _attn_fwd(proj, p["q_norm_g"], p["k_norm_g"], fulls, kinds)
    c1, passed = _glu_conv_fwd(proj, p["conv_dw_w"], p["conv_dw_b"], _pass_on_guest(filled, kinds) if filled else None)
    if filled:
        filled = passed
        p = dict(p, **{n: f[0] for n, f in zip(AFTER_ATTENTION, filled)})
    c, ct = _ln_silu_fwd(c1, p["conv_ln_g"], p["conv_ln_b"])
    x_mid = _matmul_res([attn, c], p["w_out"], x)
    h2t, u = _norm_matmul(x_mid, p["norm2_g"], p["w_up"])
    act, actt = _ffn_act_fwd(u, p["ffn_dw_w"], p["ffn_dw_b"])
    x_out = _matmul_res([act], p["w_down"], x_mid)
    saved = dict(x=x, h1t=h1t, proj=proj, attnt=attnt, ltot=ltot, c1=c1, ct=ct, x_mid=x_mid, h2t=h2t, u=u, actt=actt)
    return x_out, saved, filled


def _chip_partials(gs, kinds, core):
    g4s = _halves_view(gs, kinds)
    return _add_halves(g4s, _exchange_halves(g4s), kinds, core)


def _halves_view(gs, kinds):
    return [g.reshape(1, 2, g.shape[0] // 2, g.shape[1]) if kind == "col"
            else g.reshape(N_CHIPS, 2, g.shape[0] // N_CHIPS // 2, g.shape[1]) for g, kind in zip(gs, kinds)]


def _add_halves(g4s, received, kinds, core):
    parts = [_add_half(g4, la, core) for g4, la in zip(g4s, received)]
    return [p if kind == "col" else p.reshape(1, N_CHIPS, p.shape[1], p.shape[2]) for p, kind in zip(parts, kinds)]


def _pass_on_guest(fulls, kinds):
    n = len(fulls)
    return _Guest("pass_on", list(fulls), [jax.ShapeDtypeStruct(f.shape, f.dtype) for f in fulls], {a: a for a in range(n)},
                  lambda ins, outs, send, recv: _pass_on_copies(outs, kinds, send, recv, 0), n * (N_CHIPS - 1))


def _exchange_guest(g4s):
    def copies(ins, outs, send, recv):
        x, y, c, _ = _place()
        return [_remote(g.at[:, 1 - c], la, send, recv, a, (x, y, 1 - c)) for a, (g, la) in enumerate(zip(ins, outs))]

    out_shape = [jax.ShapeDtypeStruct((g.shape[0],) + g.shape[2:], g.dtype) for g in g4s]
    return _Guest("exchange", list(g4s), out_shape, {}, copies, len(g4s))


def _share_guest(g4s):
    def copies(ins, outs, send, recv):
        x, y, c, _ = _place()
        return [_remote(o.at[:, c], o.at[:, c], send, recv, a, (x, y, 1 - c)) for a, o in enumerate(outs)]

    n = len(g4s)
    return _Guest("share", list(g4s), [jax.ShapeDtypeStruct(g.shape, g.dtype) for g in g4s], {a: a for a in range(n)}, copies, n)


def _owned_sums(parts, landed, kinds, place):
    halves = [_add_parts(p, lb, place, kind) for p, lb, kind in zip(parts, landed, kinds)]
    return [g.reshape(2 * g.shape[2], g.shape[3]) for g in _share_halves(halves)]


def _layer_bwd(dx_out, s, p, comm=None):
    g = {}
    dact = _matmul_nt([dx_out], p["w_down"], BF16)
    g["w_down"] = _matmul_tn([s["actt"]], [dx_out], tk=D_FF // 2, tn=D_MODEL)
    dgate, dval, g["ffn_dw_w"], g["ffn_dw_b"] = _ffn_act_bwd(s["u"], p["ffn_dw_w"], p["ffn_dw_b"], dact)
    dx_mid, g["norm2_g"], _ = _matmul_nt_rmsbwd([dgate, dval], p["w_up"], s["x_mid"], p["norm2_g"], dx_out)
    g["w_up"] = _matmul_tn([s["h2t"]], [dgate, dval], tk=D_MODEL, tn=D_FF // 2)
    dmix = _matmul_nt([dx_mid], p["w_out"], F32)
    g["w_out"] = _matmul_tn([s["attnt"], s["ct"]], [dx_mid], tk=ATTN_WIDTH, tn=D_MODEL)
    dc1, g["conv_ln_g"], g["conv_ln_b"] = _ln_silu_bwd(s["c1"], p["conv_ln_g"], p["conv_ln_b"], dmix)
    parts, kinds, exchange, share = [], [], None, None
    if comm is not None:
        core, place, pending = comm
        kinds = [BIG_KIND[n] for n in AFTER_ATTENTION] + ["col"] * len(pending)
        g4s = _halves_view([g[n] for n in AFTER_ATTENTION] + list(pending), kinds)
        exchange = _exchange_guest(g4s)
    da, db, g["conv_dw_w"], g["conv_dw_b"], received = _glu_conv_bwd(s["proj"], p["conv_dw_w"], dc1, exchange)
    if comm is not None:
        parts = _add_halves(g4s, received, kinds, core)
    (dq, dk, dv, g["q_norm_g"], g["k_norm_g"]), landed = _attn_bwd(
        s["proj"], p["q_norm_g"], p["k_norm_g"], s["ltot"], dmix, parts, kinds)
    if comm is not None:
        share = _share_guest([_add_parts(pt, lb, place, kind) for pt, lb, kind in zip(parts, landed, kinds)])
    pieces = [dq, dk, dv, da, db]
    dx, g["norm1_g"], shared = _matmul_nt_rmsbwd(pieces, p["w_in"], s["x"], p["norm1_g"], dx_mid, share)
    g["w_in"] = _matmul_tn([s["h1t"]], pieces, tk=D_MODEL, tn=ATTN_WIDTH)
    return dx, g, [a.reshape(2 * a.shape[2], a.shape[3]) for a in shared]


WEIGHTS = ("norm1_g", "w_in", "q_norm_g", "k_norm_g", "conv_dw_w", "conv_dw_b", "conv_ln_g", "conv_ln_b",
           "w_out", "norm2_g", "w_up", "ffn_dw_w", "ffn_dw_b", "w_down")
BIG = ("w_in", "w_out", "w_up", "w_down")
BIG_KIND = {"w_in": "col", "w_out": "row", "w_up": "col", "w_down": "row"}
SMALL_SHARDED = ("conv_dw_w", "ffn_dw_w")
REPLICATED = tuple(n for n in WEIGHTS if n not in BIG + SMALL_SHARDED)


def _pack(arrays):
    flat = jnp.concatenate([a.reshape(-1) for a in arrays])
    rows = -(-flat.shape[0] // (SUBLANES * LANES)) * SUBLANES
    return jnp.pad(flat, (0, rows * LANES - flat.shape[0])).reshape(rows, LANES)


def _unpack(packed, shapes):
    flat = packed.reshape(-1)
    out, off = [], 0
    for shape in shapes:
        size = 1
        for d in shape:
            size *= d
        out.append(flat[off:off + size].reshape(shape))
        off += size
    return out


def _unshard_last(stacked):
    n, L, K, C = stacked.shape
    return jnp.transpose(stacked, (1, 2, 0, 3)).reshape(L, K, n * C)


def kernel(x, norm1_g, w_in, q_norm_g, k_norm_g, conv_dw_w, conv_dw_b, conv_ln_g, conv_ln_b, w_out, norm2_g, w_up, ffn_dw_w, ffn_dw_b, w_down, loss_target, m_norm1_g, m_w_in, m_q_norm_g, m_k_norm_g, m_conv_dw_w, m_conv_dw_b, m_conv_ln_g, m_conv_ln_b, m_w_out, m_norm2_g, m_w_up, m_ffn_dw_w, m_ffn_dw_b, m_w_down, v_norm1_g, v_w_in, v_q_norm_g, v_k_norm_g, v_conv_dw_w, v_conv_dw_b, v_conv_ln_g, v_conv_ln_b, v_w_out, v_norm2_g, v_w_up, v_ffn_dw_w, v_ffn_dw_b, v_w_down):
    given = dict(locals())
    w = {n: given[n] for n in WEIGHTS}
    m = {n: given["m_" + n] for n in WEIGHTS}
    v = {n: given["v_" + n] for n in WEIGHTS}
    chip = 2 * lax.axis_index("x") + lax.axis_index("y")
    core = lax.axis_index("c")
    chip_arr = jnp.reshape(chip, (1,)).astype(jnp.int32)
    core_arr = jnp.reshape(core, (1,)).astype(jnp.int32)
    L = DEPTH

    place = jnp.concatenate([chip_arr, core_arr])

    full = [{n: _cast_into_full(w[n], l, BIG_KIND[n], chip_arr) for n in BIG} for l in range(L)]
    full[0]["w_in"] = _gather_weights([full[0]["w_in"]], ["col"])[0]
    small_full = {n: _unshard_last(stacked)
                  for n, stacked in zip(SMALL_SHARDED, _gather_small([w[n] for n in SMALL_SHARDED]))}
    params = []
    for l in range(L):
        p = {n: small_full[n][l] for n in SMALL_SHARDED}
        p.update({n: w[n][l][None] for n in REPLICATED})
        params.append(p)

    act = x[0]
    saved = []
    for l in range(L):
        group = [(l, n) for n in AFTER_ATTENTION] + ([(l + 1, "w_in")] if l + 1 < L else [])
        act, s, filled = _layer_fwd(act, dict(params[l], w_in=full[l]["w_in"][0]),
                                    [full[i][n] for i, n in group], [BIG_KIND[n] for _, n in group])
        for (i, n), f in zip(group, filled):
            full[i][n] = f
        saved.append(s)
    for l in range(L):
        params[l].update({n: full[l][n][0] for n in BIG})
    dx, loss_part = _loss_grad(act, loss_target[0])
    loss = lax.psum(loss_part, ("x", "y", "c"))

    grads = [None] * L
    summed = {}
    pending = []
    for l in reversed(range(L)):
        dx, grads[l], sums = _layer_bwd(dx, saved[l], params[l], (core_arr, place, pending))
        names = [(l, n) for n in AFTER_ATTENTION] + ([(l + 1, "w_in")] if pending else [])
        summed.update(zip(names, sums))
        pending = [grads[l]["w_in"]]
    parts = _chip_partials(pending, ["col"], core_arr)
    summed[0, "w_in"] = _owned_sums(parts, _scatter_partials(parts, ["col"]), ["col"], place)[0]

    grad, delta, new_m, new_v = {}, {}, {}, {}
    for n in BIG:
        out = None
        for l in range(L):
            out = _adamw_layer(l, w[n], summed[l, n], m[n], v[n], out)
        grad[n], delta[n], new_m[n], new_v[n] = out

    small = REPLICATED + SMALL_SHARDED
    small_grads = [jnp.stack([grads[l][n] for l in range(L)]) for n in small]
    small_sums = _unpack(_allreduce_small(_pack(small_grads)), [a.shape for a in small_grads])
    for n, g in zip(small, small_sums):
        if n in REPLICATED:
            grad[n] = g.reshape(w[n].shape)
        else:
            width = w[n].shape[2]
            grad[n] = lax.dynamic_slice_in_dim(g, chip * width, width, axis=2)
    shapes = [w[n].shape for n in small]
    packed = _adamw(*[_pack([src[n] for n in small]) for src in (w, grad, m, v)])
    for out, pk in zip((delta, new_m, new_v), packed):
        out.update(zip(small, _unpack(pk, shapes)))

    return (loss, dx[None], *[grad[n] for n in WEIGHTS], *[delta[n] for n in WEIGHTS],
            *[new_m[n] for n in WEIGHTS], *[new_v[n] for n in WEIGHTS])
```

```python
import functools

import jax
import jax.numpy as jnp
from jax import lax
from jax.experimental import pallas as pl
from jax.experimental.pallas import tpu as pltpu

F32 = jnp.float32
BF16 = jnp.bfloat16

DEPTH = 4
D_MODEL = 1024
HEADS = 8
HEAD_DIM = 64
ATTN_WIDTH = HEADS * HEAD_DIM
CONV_WIDTH = D_MODEL - ATTN_WIDTH
CONV_KERNEL = 31
D_FF = 2816
FFN_KERNEL = 3
EPS = 1e-6
ADAM_LR, ADAM_B1, ADAM_B2, ADAM_EPS, ADAM_WD, ADAM_STEP = 0.001, 0.9, 0.999, 1e-08, 0.01, 10

N_CHIPS = 4
N_DEV = 8
LANES = 128
SUBLANES = 8
VMEM_LIMIT_BYTES = 56 * 2**20
ATTN_TILE = 256
ATTN_HEADS_PER_STEP = 4
ATTN_BLOCK = ATTN_HEADS_PER_STEP * HEAD_DIM
CONV_PAD = 32
FFN_PAD = 8
MESH = pl.DeviceIdType.MESH


def _params(*sem):
    return pltpu.CompilerParams(dimension_semantics=sem if sem else None, vmem_limit_bytes=VMEM_LIMIT_BYTES)


class _Guest:
    def __init__(self, name, ins, out_shape, aliases, copies, n_sem):
        self.name, self.ins, self.out_shape, self.aliases, self.copies, self.n_sem = name, ins, out_shape, aliases, copies, n_sem


def _hosted_call(body, guest, *, name, grid, in_specs, out_specs, out_shape, scratch_shapes, operands):
    if guest is None:
        out = pl.pallas_call(body, name=name, grid=grid, in_specs=in_specs, out_specs=out_specs, out_shape=out_shape,
                             scratch_shapes=scratch_shapes, compiler_params=_params("arbitrary"))(*operands)
        return list(out), []
    n_in, n_out, n_scr = len(in_specs), len(out_specs), len(scratch_shapes)
    gi, go = len(guest.ins), len(guest.out_shape)

    def hosting(*refs):
        ins, g_in = refs[:n_in], refs[n_in:n_in + gi]
        outs, g_out = refs[n_in + gi:n_in + gi + n_out], refs[n_in + gi + n_out:n_in + gi + n_out + go]
        scratch, (send, recv) = refs[n_in + gi + n_out + go:-2], refs[-2:]

        @pl.when(pl.program_id(0) == 0)
        def _():
            for cp in guest.copies(g_in, g_out, send, recv):
                cp.start()

        body(*ins, *outs, *scratch)

        @pl.when(pl.program_id(0) == grid[0] - 1)
        def _():
            for cp in guest.copies(g_in, g_out, send, recv):
                cp.wait()

    out = pl.pallas_call(
        hosting, name=name + "_" + guest.name, grid=grid,
        in_specs=list(in_specs) + [ANY] * gi, out_specs=list(out_specs) + [ANY] * go,
        out_shape=list(out_shape) + list(guest.out_shape),
        scratch_shapes=list(scratch_shapes) + [pltpu.SemaphoreType.DMA((guest.n_sem,))] * 2,
        input_output_aliases={n_in + a: n_out + b for a, b in guest.aliases.items()},
        compiler_params=pltpu.CompilerParams(dimension_semantics=("arbitrary",), vmem_limit_bytes=VMEM_LIMIT_BYTES,
                                             has_side_effects=True),
    )(*operands, *guest.ins)
    return list(out[:n_out]), list(out[n_out:])


def _dot(a, b):
    return jnp.dot(a, b, preferred_element_type=F32)


def _dot_nt(a, b):
    return lax.dot_general(a, b, (((1,), (1,)), ((), ())), preferred_element_type=F32)


def _dot_tn(a, b):
    return lax.dot_general(a, b, (((0,), (0,)), ((), ())), preferred_element_type=F32)


def _sigmoid(x):
    return 1.0 / (1.0 + jnp.exp(-x))


def _norm_matmul(x, g, w, *, tm=256):
    S, D = x.shape
    N = w.shape[1]

    def body(x_ref, g_ref, w_ref, ht_ref, y_ref):
        xv = x_ref[...]
        h = xv * lax.rsqrt(jnp.mean(xv * xv, axis=-1, keepdims=True) + EPS) * g_ref[...]
        ht_ref[...] = h.T.astype(BF16)
        y_ref[...] = _dot(h.astype(BF16), w_ref[...])

    return pl.pallas_call(
        body, name="norm_matmul", grid=(S // tm,),
        in_specs=[pl.BlockSpec((tm, D), lambda i: (i, 0)),
                  pl.BlockSpec((1, D), lambda i: (0, 0)),
                  pl.BlockSpec((D, N), lambda i: (0, 0))],
        out_specs=[pl.BlockSpec((D, tm), lambda i: (0, i)),
                   pl.BlockSpec((tm, N), lambda i: (i, 0))],
        out_shape=[jax.ShapeDtypeStruct((D, S), BF16), jax.ShapeDtypeStruct((S, N), F32)],
        compiler_params=_params("parallel"),
    )(x, g, w)


def _matmul_res(pieces, w, res, *, tm=512):
    S, N = res.shape
    K = w.shape[0]
    widths = [p.shape[1] for p in pieces]
    assert sum(widths) == K

    def body(*refs):
        p_refs, (w_ref, res_ref, o_ref) = refs[:len(pieces)], refs[len(pieces):]
        acc = res_ref[...]
        off = 0
        for p_ref, kp in zip(p_refs, widths):
            acc = acc + _dot(p_ref[...], w_ref[off:off + kp, :])
            off += kp
        o_ref[...] = acc

    return pl.pallas_call(
        body, name="matmul_res", grid=(S // tm,),
        in_specs=[pl.BlockSpec((tm, kp), lambda i: (i, 0)) for kp in widths]
        + [pl.BlockSpec((K, N), lambda i: (0, 0)), pl.BlockSpec((tm, N), lambda i: (i, 0))],
        out_specs=pl.BlockSpec((tm, N), lambda i: (i, 0)),
        out_shape=jax.ShapeDtypeStruct((S, N), F32),
        compiler_params=_params("parallel"),
    )(*pieces, w, res)


def _nt_sum(p_refs, widths, w_ref):
    acc = None
    off = 0
    for p_ref, n_p in zip(p_refs, widths):
        d = _dot_nt(p_ref[...].astype(BF16), w_ref[:, off:off + n_p])
        acc = d if acc is None else acc + d
        off += n_p
    return acc


def _matmul_nt(pieces, w, out_dtype, *, tm=512):
    S = pieces[0].shape[0]
    K, N = w.shape
    widths = [p.shape[1] for p in pieces]
    assert sum(widths) == N

    def body(*refs):
        p_refs, (w_ref, o_ref) = refs[:len(pieces)], refs[len(pieces):]
        o_ref[...] = _nt_sum(p_refs, widths, w_ref).astype(out_dtype)

    return pl.pallas_call(
        body, name="matmul_nt", grid=(S // tm,),
        in_specs=[pl.BlockSpec((tm, n_p), lambda i: (i, 0)) for n_p in widths]
        + [pl.BlockSpec((K, N), lambda i: (0, 0))],
        out_specs=pl.BlockSpec((tm, K), lambda i: (i, 0)),
        out_shape=jax.ShapeDtypeStruct((S, K), out_dtype),
        compiler_params=_params("parallel"),
    )(*pieces, w)


def _matmul_nt_rmsbwd(pieces, w, x, g, dres, guest=None, *, tm=256):
    S, K = x.shape
    N = w.shape[1]
    widths = [p.shape[1] for p in pieces]
    assert sum(widths) == N

    def body(*refs):
        p_refs, (w_ref, x_ref, g_ref, dres_ref, dx_ref, dg_ref) = refs[:len(pieces)], refs[len(pieces):]
        dh = _nt_sum(p_refs, widths, w_ref)
        xv = x_ref[...]
        r = lax.rsqrt(jnp.mean(xv * xv, axis=-1, keepdims=True) + EPS)
        xh = xv * r
        dxh = dh * g_ref[...]
        dx_ref[...] = dres_ref[...] + r * (dxh - xh * jnp.mean(dxh * xh, axis=-1, keepdims=True))

        @pl.when(pl.program_id(0) == 0)
        def _():
            dg_ref[...] = jnp.zeros_like(dg_ref)

        dg_ref[...] += jnp.sum(dh * xh, axis=0, keepdims=True)

    (dx, dg), guest_out = _hosted_call(
        body, guest, name="matmul_nt_rmsbwd", grid=(S // tm,),
        in_specs=[pl.BlockSpec((tm, n_p), lambda i: (i, 0)) for n_p in widths]
        + [pl.BlockSpec((K, N), lambda i: (0, 0)), pl.BlockSpec((tm, K), lambda i: (i, 0)),
           pl.BlockSpec((1, K), lambda i: (0, 0)), pl.BlockSpec((tm, K), lambda i: (i, 0))],
        out_specs=[pl.BlockSpec((tm, K), lambda i: (i, 0)), pl.BlockSpec((1, K), lambda i: (0, 0))],
        out_shape=[jax.ShapeDtypeStruct((S, K), F32), jax.ShapeDtypeStruct((1, K), F32)],
        scratch_shapes=[], operands=[*pieces, w, x, g, dres])
    return dx, dg, guest_out


def _matmul_tn(xts, dys, *, tk, tn, ts=1024):
    S = dys[0].shape[0]
    xs = xts
    n_s = S // ts
    (mt,) = {x.shape[0] // tk for x in xts}
    (nt,) = {d.shape[1] // tn for d in dys}

    def body(*refs):
        x_refs, dy_refs, (o_ref, acc_ref) = refs[:len(xs)], refs[len(xs):len(xs) + len(dys)], refs[len(xs) + len(dys):]
        i, j, s = pl.program_id(0), pl.program_id(1), pl.program_id(2)

        @pl.when(s == 0)
        def _():
            acc_ref[...] = jnp.zeros_like(acc_ref)

        for a, x_ref in enumerate(x_refs):
            for b, dy_ref in enumerate(dy_refs):
                @pl.when((i // mt == a) & (j // nt == b))
                def _():
                    acc_ref[...] += _dot(x_ref[...], dy_ref[...].astype(BF16))

        @pl.when(s == n_s - 1)
        def _():
            o_ref[...] = acc_ref[...].astype(BF16)

    def x_map(a):
        return lambda i, j, s: (jnp.where(i // mt == a, i % mt, 0), jnp.where(i // mt == a, s, 0))

    def dy_map(b):
        return lambda i, j, s: (jnp.where(j // nt == b, s, 0), jnp.where(j // nt == b, j % nt, 0))

    return pl.pallas_call(
        body, name="matmul_tn", grid=(len(xs) * mt, len(dys) * nt, n_s),
        in_specs=[pl.BlockSpec((tk, ts), x_map(a)) for a in range(len(xs))]
        + [pl.BlockSpec((ts, tn), dy_map(b)) for b in range(len(dys))],
        out_specs=pl.BlockSpec((tk, tn), lambda i, j, s: (i, j)),
        out_shape=jax.ShapeDtypeStruct((len(xs) * mt * tk, len(dys) * nt * tn), BF16),
        scratch_shapes=[pltpu.VMEM((tk, tn), F32)],
        compiler_params=_params("parallel", "parallel", "arbitrary"),
    )(*xs, *dys)


def _tri_consts():
    j = jnp.arange(ATTN_TILE)[:, None]
    s = jnp.arange(ATTN_TILE)[None, :]
    return (j > s).astype(BF16), (j <= s).astype(BF16), (j < s).astype(BF16)


SIGN_BIT = 0x80000000
WEIGHT_IS_ZERO = -104.0


def _log_terms(sn):
    minus_abs = lax.bitcast_convert_type(lax.bitcast_convert_type(sn, jnp.uint32) | jnp.uint32(SIGN_BIT), F32)
    lom = jnp.minimum(sn, 0.0) - jnp.log(1.0 + jnp.exp(minus_abs))
    return lom, lom - sn


def _causal_mask():
    t = lax.broadcasted_iota(jnp.int32, (ATTN_TILE, ATTN_TILE), 0)
    s = lax.broadcasted_iota(jnp.int32, (ATTN_TILE, ATTN_TILE), 1)
    return s < t


def _attn_prep(h, q_ref, k_ref, v_ref, qg_ref, kg_ref, qn_s, kn_s, vb_s, n_tiles):
    T = ATTN_TILE
    lanes = slice(HEAD_DIM * h, HEAD_DIM * (h + 1))
    scale = -(HEAD_DIM ** -0.5)

    def prep(i, carry):
        rows = pl.ds(pl.multiple_of(i * T, T), T)
        q = q_ref[rows, lanes]
        k = k_ref[rows, lanes]
        rq = lax.rsqrt(jnp.mean(q * q, axis=-1, keepdims=True) + EPS)
        rk = lax.rsqrt(jnp.mean(k * k, axis=-1, keepdims=True) + EPS)
        qn_s[h, rows, :] = (q * rq * qg_ref[...] * scale).astype(BF16)
        kn_s[h, rows, :] = (k * rk * kg_ref[...]).astype(BF16)
        vb_s[h, rows, :] = v_ref[rows, lanes].astype(BF16)
        return carry

    lax.fori_loop(0, n_tiles, prep, 0)


def _attn_fwd(proj, qg, kg, fulls=(), kinds=()):
    S = proj.shape[0]
    n_comm = len(fulls)
    T = ATTN_TILE
    n_tiles = S // T
    suffix, _, _ = _tri_consts()
    pairs = HEADS // ATTN_HEADS_PER_STEP
    q_blk, k_blk, v_blk = 0, ATTN_WIDTH // ATTN_BLOCK, 2 * ATTN_WIDTH // ATTN_BLOCK

    def body(*refs):
        q_ref, k_ref, v_ref, qg_ref, kg_ref, tri_ref = refs[:6]
        o_ref, lt_ref, ot_ref, start_ref = refs[6 + n_comm:10 + n_comm]
        w_refs = refs[10 + n_comm:10 + 2 * n_comm]
        qn_s, kn_s, vb_s = refs[10 + 2 * n_comm:13 + 2 * n_comm]
        sems = refs[13 + 2 * n_comm:]
        step_id = pl.program_id(0)
        if n_comm:
            @pl.when(pl.program_id(0) == 0)
            def _():
                for cp in _gather_copies(w_refs, kinds, *sems, 0):
                    cp.start()

        heads = range(ATTN_HEADS_PER_STEP)
        for h in heads:
            _attn_prep(h, q_ref, k_ref, v_ref, qg_ref, kg_ref, qn_s, kn_s, vb_s, n_tiles)

        def q_tile(qi, carry0):
            qrows = pl.ds(pl.multiple_of(qi * T, T), T)
            qt = [qn_s[h, qrows, :] for h in heads]

            def tile(kj, carry, diag):
                krows = pl.ds(pl.multiple_of(kj * T, T), T)
                s = [_dot_nt(qt[h], kn_s[h, krows, :]) for h in heads]
                terms = [_log_terms(s[h]) for h in heads]
                lom = [terms[h][0] for h in heads]
                if diag:
                    mask = _causal_mask()
                    lom = [jnp.where(mask, lom[h], 0.0) for h in heads]
                lom = [lom[h].astype(BF16) for h in heads]
                tail = [_dot(lom[h], tri_ref[...]) for h in heads]
                w = [jnp.exp(terms[h][1] + tail[h] + carry[h][1]) for h in heads]
                if diag:
                    w = [jnp.where(mask, w[h], 0.0) for h in heads]
                return tuple((carry[h][0] + _dot(w[h].astype(BF16), vb_s[h, krows, :]),
                              carry[h][1] + (tail[h][:, 0:1] + lom[h][:, 0:1].astype(F32))) for h in heads)

            init = tuple((jnp.zeros((T, HEAD_DIM), F32), jnp.zeros((T, 1), F32)) for h in heads)
            def alive(cr):
                worst = cr[0][1]
                for h in heads[1:]:
                    worst = jnp.maximum(worst, cr[h][1])
                return jnp.max(worst) >= WEIGHT_IS_ZERO

            def step(state):
                t, _, cr = state
                cr = tile(qi - 1 - t, cr, False)
                return t + 1, alive(cr), cr

            first = tile(qi, init, True)
            swept, _, carry = lax.while_loop(lambda st: (st[0] < qi) & st[1], step, (jnp.int32(0), alive(first), first))
            start_ref[step_id, qi] = (qi - swept).astype(F32)
            o = jnp.concatenate([carry[h][0] for h in heads], axis=1)
            o_ref[qrows, :] = o.astype(BF16)
            ot_ref[:, qrows] = o.T.astype(BF16)
            for h in heads:
                lt_ref[qrows, HEAD_DIM * h:HEAD_DIM * (h + 1)] = jnp.broadcast_to(carry[h][1], (T, HEAD_DIM))
            return carry0

        lax.fori_loop(0, n_tiles, q_tile, 0)
        if n_comm:
            @pl.when(pl.program_id(0) == pairs - 1)
            def _():
                for cp in _gather_copies(w_refs, kinds, *sems, 0):
                    cp.wait()

    n_sem = n_comm * (N_CHIPS - 1)
    out = pl.pallas_call(
        body, name="attn_fwd_gather" if n_comm else "attn_fwd", grid=(pairs,),
        in_specs=[pl.BlockSpec((S, ATTN_BLOCK), lambda p: (0, q_blk + p)),
                  pl.BlockSpec((S, ATTN_BLOCK), lambda p: (0, k_blk + p)),
                  pl.BlockSpec((S, ATTN_BLOCK), lambda p: (0, v_blk + p)),
                  pl.BlockSpec((1, HEAD_DIM), lambda p: (0, 0)),
                  pl.BlockSpec((1, HEAD_DIM), lambda p: (0, 0)),
                  pl.BlockSpec((T, T), lambda p: (0, 0))] + [ANY] * n_comm,
        out_specs=[pl.BlockSpec((S, ATTN_BLOCK), lambda p: (0, p)),
                   pl.BlockSpec((None, S, ATTN_BLOCK), lambda p: (p, 0, 0)),
                   pl.BlockSpec((ATTN_BLOCK, S), lambda p: (p, 0)),
                   pl.BlockSpec(memory_space=pltpu.SMEM)] + [ANY] * n_comm,
        out_shape=[jax.ShapeDtypeStruct((S, ATTN_WIDTH), BF16),
                   jax.ShapeDtypeStruct((pairs, S, ATTN_BLOCK), F32),
                   jax.ShapeDtypeStruct((ATTN_WIDTH, S), BF16),
                   jax.ShapeDtypeStruct((pairs, n_tiles), F32)] + [jax.ShapeDtypeStruct(f.shape, f.dtype) for f in fulls],
        scratch_shapes=[pltpu.VMEM((ATTN_HEADS_PER_STEP, S, HEAD_DIM), BF16)] * 3
        + ([pltpu.SemaphoreType.DMA((n_sem,))] * 2 if n_comm else []),
        input_output_aliases={6 + a: 4 + a for a in range(n_comm)},
        compiler_params=pltpu.CompilerParams(dimension_semantics=("arbitrary",), vmem_limit_bytes=VMEM_LIMIT_BYTES,
                                             has_side_effects=bool(n_comm)),
    )(proj, proj, proj, qg, kg, suffix, *fulls)
    return out[0], (out[1], out[3]), out[2], list(out[4:])


def _attn_bwd(proj, qg, kg, ltot, dmix, parts=(), kinds=()):
    S = proj.shape[0]
    n_comm = len(parts)
    T = ATTN_TILE
    n_tiles = S // T
    _, prefix_incl, prefix_excl = _tri_consts()
    pairs = HEADS // ATTN_HEADS_PER_STEP
    q_blk, k_blk, v_blk = 0, ATTN_WIDTH // ATTN_BLOCK, 2 * ATTN_WIDTH // ATTN_BLOCK
    scale = HEAD_DIM ** -0.5

    def body(*refs):
        q_ref, k_ref, v_ref, qg_ref, kg_ref, lt_ref, do_ref, ti_ref, te_ref, start_ref = refs[:10]
        p_refs = refs[10:10 + n_comm]
        dq_ref, dk_ref, dv_ref, dqg_ref, dkg_ref = refs[10 + n_comm:15 + n_comm]
        lb_refs = refs[15 + n_comm:15 + 2 * n_comm]
        qn_s, kn_s, vb_s, dq_s, dk_s, dv_s = refs[15 + 2 * n_comm:21 + 2 * n_comm]
        sems = refs[21 + 2 * n_comm:]
        step_id = pl.program_id(0)

        @pl.when(pl.program_id(0) == 0)
        def _():
            dqg_ref[...] = jnp.zeros_like(dqg_ref)
            dkg_ref[...] = jnp.zeros_like(dkg_ref)
            for cp in _scatter_copies(p_refs, lb_refs, kinds, *sems) if n_comm else ():
                cp.start()

        heads = range(ATTN_HEADS_PER_STEP)
        for h in heads:
            _attn_prep(h, q_ref, k_ref, v_ref, qg_ref, kg_ref, qn_s, kn_s, vb_s, n_tiles)
        dk_s[...] = jnp.zeros_like(dk_s)
        dv_s[...] = jnp.zeros_like(dv_s)

        def q_tile(qi, carry0):
            qrows = pl.ds(pl.multiple_of(qi * T, T), T)
            qt = [qn_s[h, qrows, :] for h in heads]
            dob = [do_ref[qrows, HEAD_DIM * h:HEAD_DIM * (h + 1)].astype(BF16) for h in heads]
            lt = [lt_ref[qrows, HEAD_DIM * h:HEAD_DIM * h + 1] for h in heads]

            def tile(kj, carry, diag):
                krows = pl.ds(pl.multiple_of(kj * T, T), T)
                kt = [kn_s[h, krows, :] for h in heads]
                s = [_dot_nt(qt[h], kt[h]) for h in heads]
                dw = [_dot_nt(dob[h], vb_s[h, krows, :]) for h in heads]
                terms = [_log_terms(s[h]) for h in heads]
                lom = [terms[h][0] for h in heads]
                if diag:
                    mask = _causal_mask()
                    lom = [jnp.where(mask, lom[h], 0.0) for h in heads]
                pin = [_dot(lom[h].astype(BF16), ti_ref[...]) for h in heads]
                w = [jnp.exp(terms[h][1] + ((lt[h] - carry[h][1]) - pin[h])) for h in heads]
                if diag:
                    w = [jnp.where(mask, w[h], 0.0) for h in heads]
                e = [w[h] * dw[h] for h in heads]
                gex = [_dot(e[h].astype(BF16), te_ref[...]) for h in heads]
                dz = [e[h] - jnp.exp(terms[h][1]) * (e[h] + (carry[h][2] + gex[h])) for h in heads]
                if diag:
                    dz = [jnp.where(mask, dz[h], 0.0) for h in heads]
                new = []
                for h in heads:
                    dzb = dz[h].astype(BF16)
                    dk_s[h, krows, :] += _dot_tn(dzb, qt[h])
                    dv_s[h, krows, :] += _dot_tn(w[h].astype(BF16), dob[h])
                    new.append((carry[h][0] + _dot(dzb, kt[h]),
                                carry[h][1] + pin[h][:, T - 1:T],
                                carry[h][2] + gex[h][:, T - 1:T] + e[h][:, T - 1:T]))
                return tuple(new)

            zero = jnp.zeros((T, 1), F32)
            init = tuple((jnp.zeros((T, HEAD_DIM), F32), zero, zero) for h in heads)
            first = jnp.clip(start_ref[step_id, qi].astype(jnp.int32), 0, qi)
            last = tile(qi, lax.fori_loop(first, qi, lambda kj, cr: tile(kj, cr, False), init), True)
            for h in heads:
                dq_s[h, qrows, :] = last[h][0]
            return carry0

        lax.fori_loop(0, n_tiles, q_tile, 0)

        def finish(i, carry):
            rows = pl.ds(pl.multiple_of(i * T, T), T)
            new = []
            for h in heads:
                lanes = slice(HEAD_DIM * h, HEAD_DIM * (h + 1))
                q = q_ref[rows, lanes]
                k = k_ref[rows, lanes]
                rq = lax.rsqrt(jnp.mean(q * q, axis=-1, keepdims=True) + EPS)
                rk = lax.rsqrt(jnp.mean(k * k, axis=-1, keepdims=True) + EPS)
                qh = q * rq
                kh = k * rk
                dqn = dq_s[h, rows, :] * scale
                dkn = -dk_s[h, rows, :]
                dqh = dqn * qg_ref[...]
                dkh = dkn * kg_ref[...]
                dq_ref[rows, lanes] = (rq * (dqh - qh * jnp.mean(dqh * qh, axis=-1, keepdims=True))).astype(BF16)
                dk_ref[rows, lanes] = (rk * (dkh - kh * jnp.mean(dkh * kh, axis=-1, keepdims=True))).astype(BF16)
                dv_ref[rows, lanes] = dv_s[h, rows, :].astype(BF16)
                new.append(carry[2 * h] + jnp.sum(dqn * qh, axis=0, keepdims=True))
                new.append(carry[2 * h + 1] + jnp.sum(dkn * kh, axis=0, keepdims=True))
            return tuple(new)

        zero = jnp.zeros((1, HEAD_DIM), F32)
        sums = lax.fori_loop(0, n_tiles, finish, (zero,) * (2 * len(heads)))
        dqg_ref[...] += sum(sums[0::2])
        dkg_ref[...] += sum(sums[1::2])
        if n_comm:
            @pl.when(pl.program_id(0) == pairs - 1)
            def _():
                for cp in _scatter_copies(p_refs, lb_refs, kinds, *sems):
                    cp.wait()

    blk = lambda off: pl.BlockSpec((S, ATTN_BLOCK), lambda p: (0, off + p))
    row64 = pl.BlockSpec((1, HEAD_DIM), lambda p: (0, 0))
    tri = pl.BlockSpec((T, T), lambda p: (0, 0))
    n_sem = n_comm * (N_CHIPS - 1)
    out = pl.pallas_call(
        body, name="attn_bwd_scatter" if n_comm else "attn_bwd", grid=(pairs,),
        in_specs=[blk(q_blk), blk(k_blk), blk(v_blk), row64, row64,
                  pl.BlockSpec((None, S, ATTN_BLOCK), lambda p: (p, 0, 0)), blk(0), tri, tri,
                  pl.BlockSpec(memory_space=pltpu.SMEM)] + [ANY] * n_comm,
        out_specs=[blk(0), blk(0), blk(0), row64, row64] + [ANY] * n_comm,
        out_shape=[jax.ShapeDtypeStruct((S, ATTN_WIDTH), BF16)] * 3 + [jax.ShapeDtypeStruct((1, HEAD_DIM), F32)] * 2
        + _scatter_shapes(parts, kinds),
        scratch_shapes=[pltpu.VMEM((ATTN_HEADS_PER_STEP, S, HEAD_DIM), BF16)] * 3
        + [pltpu.VMEM((ATTN_HEADS_PER_STEP, S, HEAD_DIM), F32)] * 3
        + ([pltpu.SemaphoreType.DMA((n_sem,))] * 2 if n_comm else []),
        compiler_params=pltpu.CompilerParams(dimension_semantics=("arbitrary",), vmem_limit_bytes=VMEM_LIMIT_BYTES,
                                             has_side_effects=bool(n_comm)),
    )(proj, proj, proj, qg, kg, ltot[0], dmix, prefix_incl, prefix_excl, ltot[1], *parts)
    return out[:5], list(out[5:])


def _shifted(win, n_rows):
    return [win if b == 0 else pltpu.roll(win, n_rows - b, 0) for b in range(SUBLANES)]


def _taps(variants, offsets, tm):
    return {o: variants[o % SUBLANES][(o // SUBLANES) * SUBLANES:(o // SUBLANES) * SUBLANES + tm, :] for o in offsets}


def _fold_rows(a):
    return jnp.sum(a.reshape(a.shape[0] // SUBLANES, SUBLANES, a.shape[1]), axis=0)


def _glu_conv_fwd(proj, w, bias, guest=None, *, tm=256):
    S = proj.shape[0]
    CB = LANES
    a_blk, b_blk = 3 * ATTN_WIDTH // CB, (3 * ATTN_WIDTH + CONV_WIDTH) // CB
    n_rows = tm + CONV_PAD

    def body(a_ref, b_ref, w_ref, bias_ref, c1_ref, pad_s):
        pad_s[0:CONV_PAD, :] = jnp.zeros((CONV_PAD, CB), F32)

        def fill(i, carry):
            rows = pl.ds(pl.multiple_of(i * tm, tm), tm)
            pad_s[pl.ds(pl.multiple_of(CONV_PAD + i * tm, SUBLANES), tm), :] = a_ref[rows, :] * _sigmoid(b_ref[rows, :])
            return carry

        lax.fori_loop(0, S // tm, fill, 0)

        def conv(i, carry):
            r0 = pl.multiple_of(i * tm, tm)
            taps = _taps(_shifted(pad_s[pl.ds(r0, n_rows), :], n_rows), range(2, 2 + CONV_KERNEL), tm)
            acc = jnp.broadcast_to(bias_ref[...], (tm, CB))
            for k in range(CONV_KERNEL):
                acc = acc + w_ref[k:k + 1, :] * taps[k + 2]
            c1_ref[pl.ds(r0, tm), :] = acc
            return carry

        lax.fori_loop(0, S // tm, conv, 0)

    (c1,), guest_out = _hosted_call(
        body, guest, name="glu_conv_fwd", grid=(CONV_WIDTH // CB,),
        in_specs=[pl.BlockSpec((S, CB), lambda j: (0, a_blk + j)), pl.BlockSpec((S, CB), lambda j: (0, b_blk + j)),
                  pl.BlockSpec((CONV_KERNEL, CB), lambda j: (0, j)), pl.BlockSpec((1, CB), lambda j: (0, j))],
        out_specs=[pl.BlockSpec((S, CB), lambda j: (0, j))],
        out_shape=[jax.ShapeDtypeStruct((S, CONV_WIDTH), F32)],
        scratch_shapes=[pltpu.VMEM((S + CONV_PAD, CB), F32)], operands=[proj, proj, w, bias])
    return c1, guest_out


def _glu_conv_bwd(proj, w, dc1, guest=None, *, tm=256):
    S = proj.shape[0]
    CB = LANES
    a_blk, b_blk = 3 * ATTN_WIDTH // CB, (3 * ATTN_WIDTH + CONV_WIDTH) // CB
    n_rows = tm + CONV_PAD

    def body(a_ref, b_ref, w_ref, dc1_ref, da_ref, db_ref, dw_ref, dbias_ref, pad_s, dpad_s, dw_s):
        pad_s[0:CONV_PAD, :] = jnp.zeros((CONV_PAD, CB), F32)
        dpad_s[S:S + CONV_PAD, :] = jnp.zeros((CONV_PAD, CB), F32)
        dw_s[...] = jnp.zeros_like(dw_s)

        def fill(i, carry):
            rows = pl.ds(pl.multiple_of(i * tm, tm), tm)
            pad_s[pl.ds(pl.multiple_of(CONV_PAD + i * tm, SUBLANES), tm), :] = a_ref[rows, :] * _sigmoid(b_ref[rows, :])
            dpad_s[rows, :] = dc1_ref[rows, :]
            return carry

        lax.fori_loop(0, S // tm, fill, 0)

        def conv(i, carry):
            r0 = pl.multiple_of(i * tm, tm)
            rows = pl.ds(r0, tm)
            taps = _taps(_shifted(dpad_s[pl.ds(r0, n_rows), :], n_rows), range(CONV_KERNEL), tm)
            acc = jnp.zeros((tm, CB), F32)
            for k in range(CONV_KERNEL):
                acc = acc + w_ref[k:k + 1, :] * taps[CONV_KERNEL - 1 - k]
            a = a_ref[rows, :]
            sg = _sigmoid(b_ref[rows, :])
            da_ref[rows, :] = (acc * sg).astype(BF16)
            db_ref[rows, :] = (acc * a * sg * (1.0 - sg)).astype(BF16)
            d = taps[0]
            taps = _taps(_shifted(pad_s[pl.ds(r0, n_rows), :], n_rows), range(2, 2 + CONV_KERNEL), tm)
            for k in range(CONV_KERNEL):
                dw_s[SUBLANES * k:SUBLANES * (k + 1), :] += _fold_rows(d * taps[k + 2])
            dw_s[SUBLANES * CONV_KERNEL:SUBLANES * (CONV_KERNEL + 1), :] += _fold_rows(d)
            return carry

        lax.fori_loop(0, S // tm, conv, 0)
        for k in range(CONV_KERNEL):
            dw_ref[k:k + 1, :] = jnp.sum(dw_s[SUBLANES * k:SUBLANES * (k + 1), :], axis=0, keepdims=True)
        dbias_ref[...] = jnp.sum(dw_s[SUBLANES * CONV_KERNEL:SUBLANES * (CONV_KERNEL + 1), :], axis=0, keepdims=True)

    col = lambda off: pl.BlockSpec((S, CB), lambda j: (0, off + j))
    out, guest_out = _hosted_call(
        body, guest, name="glu_conv_bwd", grid=(CONV_WIDTH // CB,),
        in_specs=[col(a_blk), col(b_blk), pl.BlockSpec((CONV_KERNEL, CB), lambda j: (0, j)), col(0)],
        out_specs=[col(0), col(0), pl.BlockSpec((CONV_KERNEL, CB), lambda j: (0, j)), pl.BlockSpec((1, CB), lambda j: (0, j))],
        out_shape=[jax.ShapeDtypeStruct((S, CONV_WIDTH), BF16)] * 2
        + [jax.ShapeDtypeStruct((CONV_KERNEL, CONV_WIDTH), F32), jax.ShapeDtypeStruct((1, CONV_WIDTH), F32)],
        scratch_shapes=[pltpu.VMEM((S + CONV_PAD, CB), F32), pltpu.VMEM((S + CONV_PAD, CB), F32),
                        pltpu.VMEM((SUBLANES * (CONV_KERNEL + 1), CB), F32)], operands=[proj, proj, w, dc1])
    return (*out, guest_out)


def _ln_stats(c1):
    mu = jnp.mean(c1, axis=-1, keepdims=True)
    xc = c1 - mu
    r = lax.rsqrt(jnp.mean(xc * xc, axis=-1, keepdims=True) + EPS)
    return xc * r, r


def _ln_silu_fwd(c1, g, b, *, tm=512):
    S, C = c1.shape

    def body(c1_ref, g_ref, b_ref, c_ref, ct_ref):
        yh, _ = _ln_stats(c1_ref[...])
        y = yh * g_ref[...] + b_ref[...]
        c = y * _sigmoid(y)
        c_ref[...] = c.astype(BF16)
        ct_ref[...] = c.T.astype(BF16)

    vec = pl.BlockSpec((1, C), lambda i: (0, 0))
    return pl.pallas_call(
        body, name="ln_silu_fwd", grid=(S // tm,),
        in_specs=[pl.BlockSpec((tm, C), lambda i: (i, 0)), vec, vec],
        out_specs=[pl.BlockSpec((tm, C), lambda i: (i, 0)), pl.BlockSpec((C, tm), lambda i: (0, i))],
        out_shape=[jax.ShapeDtypeStruct((S, C), BF16), jax.ShapeDtypeStruct((C, S), BF16)],
        compiler_params=_params("parallel"),
    )(c1, g, b)


def _ln_silu_bwd(c1, g, b, dmix, *, tm=512):
    S, C = c1.shape

    def body(c1_ref, g_ref, b_ref, dc_ref, dc1_ref, dg_ref, db_ref):
        yh, r = _ln_stats(c1_ref[...])
        y = yh * g_ref[...] + b_ref[...]
        sg = _sigmoid(y)
        dy = dc_ref[...] * (sg * (1.0 + y * (1.0 - sg)))
        dyh = dy * g_ref[...]
        dc1_ref[...] = r * (dyh - jnp.mean(dyh, axis=-1, keepdims=True)
                            - yh * jnp.mean(dyh * yh, axis=-1, keepdims=True))

        @pl.when(pl.program_id(0) == 0)
        def _():
            dg_ref[...] = jnp.zeros_like(dg_ref)
            db_ref[...] = jnp.zeros_like(db_ref)

        dg_ref[...] += jnp.sum(dy * yh, axis=0, keepdims=True)
        db_ref[...] += jnp.sum(dy, axis=0, keepdims=True)

    vec = pl.BlockSpec((1, C), lambda i: (0, 0))
    return pl.pallas_call(
        body, name="ln_silu_bwd", grid=(S // tm,),
        in_specs=[pl.BlockSpec((tm, C), lambda i: (i, 0)), vec, vec, pl.BlockSpec((tm, C), lambda i: (i, 1))],
        out_specs=[pl.BlockSpec((tm, C), lambda i: (i, 0)), vec, vec],
        out_shape=[jax.ShapeDtypeStruct((S, C), F32), jax.ShapeDtypeStruct((1, C), F32), jax.ShapeDtypeStruct((1, C), F32)],
        compiler_params=_params("arbitrary"),
    )(c1, g, b, dmix)


FFN_CB = 256


def _ffn_gate(pad_s, w_ref, bias_ref, r0, tm):
    n_rows = tm + FFN_PAD
    taps = _taps(_shifted(pad_s[pl.ds(r0, n_rows), :], n_rows), range(FFN_PAD - 2, FFN_PAD + 1), tm)
    g1 = bias_ref[...] + w_ref[0:1, :] * taps[6] + w_ref[1:2, :] * taps[7] + w_ref[2:3, :] * taps[8]
    return g1, taps


def _ffn_act_fwd(u, w, bias, *, tm=256):
    S = u.shape[0]
    CB = FFN_CB
    nb = D_FF // CB

    def body(g_ref, v_ref, w_ref, bias_ref, o_ref, ot_ref, pad_s):
        pad_s[0:FFN_PAD, :] = jnp.zeros((FFN_PAD, CB), F32)

        def fill(i, carry):
            pad_s[pl.ds(pl.multiple_of(FFN_PAD + i * tm, SUBLANES), tm), :] = g_ref[pl.ds(pl.multiple_of(i * tm, tm), tm), :]
            return carry

        lax.fori_loop(0, S // tm, fill, 0)

        def act(i, carry):
            r0 = pl.multiple_of(i * tm, tm)
            g1, _ = _ffn_gate(pad_s, w_ref, bias_ref, r0, tm)
            a = g1 * _sigmoid(g1) * v_ref[pl.ds(r0, tm), :]
            o_ref[pl.ds(r0, tm), :] = a.astype(BF16)
            ot_ref[:, pl.ds(r0, tm)] = a.T.astype(BF16)
            return carry

        lax.fori_loop(0, S // tm, act, 0)

    return pl.pallas_call(
        body, name="ffn_act_fwd", grid=(nb,),
        in_specs=[pl.BlockSpec((S, CB), lambda j: (0, j)), pl.BlockSpec((S, CB), lambda j: (0, nb + j)),
                  pl.BlockSpec((FFN_KERNEL, CB), lambda j: (0, j)), pl.BlockSpec((1, CB), lambda j: (0, j))],
        out_specs=[pl.BlockSpec((S, CB), lambda j: (0, j)), pl.BlockSpec((CB, S), lambda j: (j, 0))],
        out_shape=[jax.ShapeDtypeStruct((S, D_FF), BF16), jax.ShapeDtypeStruct((D_FF, S), BF16)],
        scratch_shapes=[pltpu.VMEM((S + FFN_PAD, CB), F32)],
        compiler_params=_params("parallel"),
    )(u, u, w, bias)


def _ffn_act_bwd(u, w, bias, dact, *, tm=256):
    S = u.shape[0]
    CB = FFN_CB
    nb = D_FF // CB

    def body(g_ref, v_ref, w_ref, bias_ref, da_ref, dg_ref, dv_ref, dw_ref, dbias_ref, pad_s, dpad_s, dw_s):
        pad_s[0:FFN_PAD, :] = jnp.zeros((FFN_PAD, CB), F32)
        dpad_s[S:S + FFN_PAD, :] = jnp.zeros((FFN_PAD, CB), F32)
        dw_s[...] = jnp.zeros_like(dw_s)

        def fill(i, carry):
            pad_s[pl.ds(pl.multiple_of(FFN_PAD + i * tm, SUBLANES), tm), :] = g_ref[pl.ds(pl.multiple_of(i * tm, tm), tm), :]
            return carry

        lax.fori_loop(0, S // tm, fill, 0)

        def first(i, carry):
            r0 = pl.multiple_of(i * tm, tm)
            rows = pl.ds(r0, tm)
            g1, taps = _ffn_gate(pad_s, w_ref, bias_ref, r0, tm)
            sg = _sigmoid(g1)
            da = da_ref[rows, :].astype(F32)
            dv_ref[rows, :] = (da * g1 * sg).astype(BF16)
            dg1 = da * v_ref[rows, :] * (sg * (1.0 + g1 * (1.0 - sg)))
            dpad_s[rows, :] = dg1
            for k in range(FFN_KERNEL):
                dw_s[SUBLANES * k:SUBLANES * (k + 1), :] += _fold_rows(dg1 * taps[FFN_PAD - 2 + k])
            dw_s[SUBLANES * FFN_KERNEL:SUBLANES * (FFN_KERNEL + 1), :] += _fold_rows(dg1)
            return carry

        lax.fori_loop(0, S // tm, first, 0)

        def second(i, carry):
            r0 = pl.multiple_of(i * tm, tm)
            n_rows = tm + FFN_PAD
            taps = _taps(_shifted(dpad_s[pl.ds(r0, n_rows), :], n_rows), range(FFN_KERNEL), tm)
            dg_ref[pl.ds(r0, tm), :] = (w_ref[2:3, :] * taps[0] + w_ref[1:2, :] * taps[1] + w_ref[0:1, :] * taps[2]).astype(BF16)
            return carry

        lax.fori_loop(0, S // tm, second, 0)
        for k in range(FFN_KERNEL):
            dw_ref[k:k + 1, :] = jnp.sum(dw_s[SUBLANES * k:SUBLANES * (k + 1), :], axis=0, keepdims=True)
        dbias_ref[...] = jnp.sum(dw_s[SUBLANES * FFN_KERNEL:SUBLANES * (FFN_KERNEL + 1), :], axis=0, keepdims=True)

    col = lambda off: pl.BlockSpec((S, CB), lambda j: (0, off + j))
    wspec = pl.BlockSpec((FFN_KERNEL, CB), lambda j: (0, j))
    bspec = pl.BlockSpec((1, CB), lambda j: (0, j))
    return pl.pallas_call(
        body, name="ffn_act_bwd", grid=(nb,),
        in_specs=[col(0), col(nb), wspec, bspec, col(0)],
        out_specs=[col(0), col(0), wspec, bspec],
        out_shape=[jax.ShapeDtypeStruct((S, D_FF), BF16)] * 2
        + [jax.ShapeDtypeStruct((FFN_KERNEL, D_FF), F32), jax.ShapeDtypeStruct((1, D_FF), F32)],
        scratch_shapes=[pltpu.VMEM((S + FFN_PAD, CB), F32), pltpu.VMEM((S + FFN_PAD, CB), F32),
                        pltpu.VMEM((SUBLANES * (FFN_KERNEL + 1), CB), F32)],
        compiler_params=_params("parallel"),
    )(u, u, w, bias, dact)


def _loss_grad(y, target, *, tm=512):
    S, D = y.shape

    def body(y_ref, t_ref, dy_ref, l_ref):
        d = y_ref[...] - t_ref[...]
        dy_ref[...] = d * (1.0 / D)

        @pl.when(pl.program_id(0) == 0)
        def _():
            l_ref[...] = jnp.zeros_like(l_ref)

        l_ref[...] += 0.5 * jnp.sum(jnp.mean(d * d, axis=-1, keepdims=True), axis=0, keepdims=True)

    dy, l = pl.pallas_call(
        body, name="loss_grad", grid=(S // tm,),
        in_specs=[pl.BlockSpec((tm, D), lambda i: (i, 0))] * 2,
        out_specs=[pl.BlockSpec((tm, D), lambda i: (i, 0)), pl.BlockSpec((SUBLANES, LANES), lambda i: (0, 0))],
        out_shape=[jax.ShapeDtypeStruct((S, D), F32), jax.ShapeDtypeStruct((SUBLANES, LANES), F32)],
        compiler_params=_params("arbitrary"),
    )(y, target)
    return dy, l[0, 0]


def _row_tile(rows, cap=512):
    t = min(rows, cap)
    while rows % t or t % SUBLANES:
        t -= 1
    return t


def _adam_update(w, g, m, v):
    m1 = ADAM_B1 * m + (1.0 - ADAM_B1) * g
    v1 = ADAM_B2 * v + (1.0 - ADAM_B2) * (g * g)
    m_hat = m1 / (1.0 - ADAM_B1 ** ADAM_STEP)
    v_hat = v1 / (1.0 - ADAM_B2 ** ADAM_STEP)
    return -ADAM_LR * (m_hat / (jnp.sqrt(v_hat) + ADAM_EPS) + ADAM_WD * w), m1, v1


def _adamw(w, g, m, v):
    R, C = w.shape
    tr = _row_tile(R, 256)

    def body(w_ref, g_ref, m_ref, v_ref, d_ref, nm_ref, nv_ref):
        d_ref[...], nm_ref[...], nv_ref[...] = _adam_update(w_ref[...], g_ref[...], m_ref[...], v_ref[...])

    spec = pl.BlockSpec((tr, C), lambda i: (i, 0))
    return pl.pallas_call(
        body, name="adamw", grid=(R // tr,),
        in_specs=[spec] * 4, out_specs=[spec] * 3,
        out_shape=[jax.ShapeDtypeStruct((R, C), F32)] * 3,
        compiler_params=_params("parallel"),
    )(w, g, m, v)


def _adamw_layer(layer, w, g, m, v, prev):
    L, R, C = w.shape
    tr = _row_tile(R, 256)
    n_prev = 4 if prev else 0

    def body(*refs):
        w_ref, g_ref, m_ref, v_ref = refs[:4]
        go_ref, d_ref, nm_ref, nv_ref = refs[4 + n_prev:8 + n_prev]
        gv = g_ref[...]
        go_ref[...] = gv
        d_ref[...], nm_ref[...], nv_ref[...] = _adam_update(w_ref[...], gv, m_ref[...], v_ref[...])

    stacked = pl.BlockSpec((None, tr, C), lambda i: (layer, i, 0))
    return pl.pallas_call(
        body, name="adamw_layer", grid=(R // tr,),
        in_specs=[stacked, pl.BlockSpec((tr, C), lambda i: (i, 0)), stacked, stacked] + [ANY] * n_prev,
        out_specs=[stacked] * 4, out_shape=[jax.ShapeDtypeStruct((L, R, C), F32)] * 4,
        input_output_aliases={4 + j: j for j in range(n_prev)},
        compiler_params=_params("parallel"),
    )(w, g, m, v, *(prev or ()))


def _cast_into_full(w, layer, kind, chip):
    _, R, C = w.shape
    tr = _row_tile(R, 512)

    def body(chip_ref, w_ref, o_ref):
        o_ref[...] = w_ref[...].astype(BF16)

    if kind == "col":
        out_shape = jax.ShapeDtypeStruct((1, R, N_CHIPS * C), BF16)
        out_spec = pl.BlockSpec((None, tr, C), lambda i, chip_ref: (0, i, chip_ref[0]))
    else:
        out_shape = jax.ShapeDtypeStruct((1, N_CHIPS * R, C), BF16)
        out_spec = pl.BlockSpec((None, tr, C), lambda i, chip_ref: (0, chip_ref[0] * (R // tr) + i, 0))
    return pl.pallas_call(
        body, name="cast_into_full_" + kind,
        grid_spec=pltpu.PrefetchScalarGridSpec(
            num_scalar_prefetch=1, grid=(R // tr,),
            in_specs=[pl.BlockSpec((None, tr, C), lambda i, chip_ref: (layer, i, 0))], out_specs=out_spec),
        out_shape=out_shape, compiler_params=_params("parallel"),
    )(chip, w)


def _add_half(g4, la, c):
    L, _, H, W = g4.shape
    th = _row_tile(H, 256)

    def body(c_ref, g_ref, la_ref, o_ref):
        o_ref[...] = (g_ref[...].astype(F32) + la_ref[...].astype(F32)).astype(BF16)

    return pl.pallas_call(
        body, name="add_half",
        grid_spec=pltpu.PrefetchScalarGridSpec(
            num_scalar_prefetch=1, grid=(L, H // th),
            in_specs=[pl.BlockSpec((None, None, th, W), lambda l, i, c_ref: (l, c_ref[0], i, 0)),
                      pl.BlockSpec((None, th, W), lambda l, i, c_ref: (l, i, 0))],
            out_specs=pl.BlockSpec((None, th, W), lambda l, i, c_ref: (l, i, 0))),
        out_shape=jax.ShapeDtypeStruct((L, H, W), BF16),
        compiler_params=_params("parallel", "parallel"),
    )(c, g4, la)


def _add_parts(p, lb, place, kind):
    _, L, H, C = lb.shape
    th = _row_tile(H, 256)

    def body(s_ref, p_ref, lb_ref, o_ref):
        acc = p_ref[...].astype(F32)
        for k in range(N_CHIPS - 1):
            acc = acc + lb_ref[k].astype(F32)
        o_ref[...] = acc

    if kind == "col":
        p_spec = pl.BlockSpec((None, th, C), lambda l, i, s_ref: (l, i, s_ref[0]))
    else:
        p_spec = pl.BlockSpec((None, None, th, C), lambda l, i, s_ref: (l, s_ref[0], i, 0))
    return pl.pallas_call(
        body, name="add_parts_" + kind,
        grid_spec=pltpu.PrefetchScalarGridSpec(
            num_scalar_prefetch=1, grid=(L, H // th),
            in_specs=[p_spec, pl.BlockSpec((N_CHIPS - 1, None, th, C), lambda l, i, s_ref: (0, l, i, 0))],
            out_specs=pl.BlockSpec((None, None, th, C), lambda l, i, s_ref: (l, s_ref[1], i, 0))),
        out_shape=jax.ShapeDtypeStruct((L, 2, H, C), F32),
        compiler_params=_params("parallel", "parallel"),
    )(place, p, lb)


ANY = pl.BlockSpec(memory_space=pl.ANY)


def _place():
    x, y, c = lax.axis_index("x"), lax.axis_index("y"), lax.axis_index("c")
    chips = [(1 - x, y), (x, 1 - y), (1 - x, 1 - y)]
    return x, y, c, chips


def _comm_call(body, name, ins, out_shape, n_remote, n_local, aliases=None):
    scratch = [pltpu.SemaphoreType.DMA((n_remote,)), pltpu.SemaphoreType.DMA((n_remote,))]
    if n_local:
        scratch.append(pltpu.SemaphoreType.DMA((n_local,)))
    return pl.pallas_call(
        body, name=name, in_specs=[ANY] * len(ins), out_specs=[ANY] * len(out_shape), out_shape=out_shape,
        scratch_shapes=scratch, input_output_aliases=aliases or {},
        compiler_params=pltpu.CompilerParams(has_side_effects=True),
    )(*ins)


def _remote(src, dst, send, recv, k, to):
    return pltpu.make_async_remote_copy(src_ref=src, dst_ref=dst, send_sem=send.at[k], recv_sem=recv.at[k],
                                        device_id=to, device_id_type=MESH)


def _gather_weights(fulls, kinds):
    n = len(fulls)
    out_shape = [jax.ShapeDtypeStruct(f.shape, f.dtype) for f in fulls]

    def body(*refs):
        outs, (send, recv) = refs[n:2 * n], refs[2 * n:]
        first = _gather_copies(outs, kinds, send, recv, 0)
        for cp in first:
            cp.start()
        for cp in first:
            cp.wait()
        passed = _pass_on_copies(outs, kinds, send, recv, len(first))
        for cp in passed:
            cp.start()
        for cp in passed:
            cp.wait()

    return _comm_call(body, "gather_weights", fulls, out_shape, 2 * n * (N_CHIPS - 1), 0, {a: a for a in range(n)})


def _window(ref, kind, s, h):
    if kind == "col":
        H, C = ref.shape[1] // 2, ref.shape[2] // N_CHIPS
        return ref.at[:, pl.ds(pl.multiple_of(h * H, 16), H), pl.ds(pl.multiple_of(s * C, LANES), C)]
    R = ref.shape[1] // N_CHIPS
    return ref.at[:, pl.ds(pl.multiple_of(s * R + h * (R // 2), 16), R // 2), :]


def _gather_copies(outs, kinds, send, recv, sem0):
    x, y, c, chips = _place()
    me = 2 * x + y
    return [_remote(_window(o, kind, me, c), _window(o, kind, me, c), send, recv, sem0 + a * (N_CHIPS - 1) + k, (*chip, c))
            for a, (o, kind) in enumerate(zip(outs, kinds)) for k, chip in enumerate(chips)]


def _pass_on_copies(outs, kinds, send, recv, sem0):
    x, y, c, chips = _place()
    cps = []
    for a, (o, kind) in enumerate(zip(outs, kinds)):
        for k, chip in enumerate(chips):
            landed = _window(o, kind, 2 * chip[0] + chip[1], c)
            cps.append(_remote(landed, landed, send, recv, sem0 + a * (N_CHIPS - 1) + k, (x, y, 1 - c)))
    return cps


def _pass_on(fulls, kinds):
    n = len(fulls)
    out_shape = [jax.ShapeDtypeStruct(f.shape, f.dtype) for f in fulls]

    def body(*refs):
        outs, (send, recv) = refs[n:2 * n], refs[2 * n:]
        cps = _pass_on_copies(outs, kinds, send, recv, 0)
        for cp in cps:
            cp.start()
        for cp in cps:
            cp.wait()

    return _comm_call(body, "pass_on", fulls, out_shape, n * (N_CHIPS - 1), 0, {a: a for a in range(n)})


def _gather_small(shards):
    n = len(shards)
    out_shape = [jax.ShapeDtypeStruct((N_CHIPS,) + s.shape, s.dtype) for s in shards]

    def body(*refs):
        srcs, outs, (send, recv, loc) = refs[:n], refs[n:2 * n], refs[2 * n:]
        x, y, c, chips = _place()
        me = 2 * x + y
        remote, local = [], []
        for a in range(n):
            local.append(pltpu.make_async_copy(srcs[a], outs[a].at[me], loc.at[a]))
            for k, chip in enumerate(chips):
                remote.append(_remote(srcs[a], outs[a].at[me], send, recv, a * (N_CHIPS - 1) + k, (*chip, c)))
        for cp in local + remote:
            cp.start()
        for cp in remote + local:
            cp.wait()

    return _comm_call(body, "gather_small", shards, out_shape, n * (N_CHIPS - 1), n)


def _exchange_halves(g4s):
    n = len(g4s)
    out_shape = [jax.ShapeDtypeStruct((g.shape[0],) + g.shape[2:], g.dtype) for g in g4s]

    def body(*refs):
        gs, las, (send, recv) = refs[:n], refs[n:2 * n], refs[2 * n:]
        x, y, c, _ = _place()
        cps = [_remote(gs[a].at[:, 1 - c], las[a], send, recv, a, (x, y, 1 - c)) for a in range(n)]
        for cp in cps:
            cp.start()
        for cp in cps:
            cp.wait()

    return _comm_call(body, "exchange_halves", g4s, out_shape, n, 0)


def _scatter_partials(ps, kinds):
    n = len(ps)

    def body(*refs):
        srcs, lbs, (send, recv) = refs[:n], refs[n:2 * n], refs[2 * n:]
        cps = _scatter_copies(srcs, lbs, kinds, send, recv)
        for cp in cps:
            cp.start()
        for cp in cps:
            cp.wait()

    return _comm_call(body, "scatter_partials", ps, _scatter_shapes(ps, kinds), n * (N_CHIPS - 1), 0)


def _scatter_shapes(ps, kinds):
    out_shape = []
    for p, kind in zip(ps, kinds):
        L, H, C = (p.shape[0], p.shape[1], p.shape[2] // N_CHIPS) if kind == "col" else (p.shape[0], p.shape[2], p.shape[3])
        out_shape.append(jax.ShapeDtypeStruct((N_CHIPS - 1, L, H, C), p.dtype))
    return out_shape


def _scatter_copies(srcs, lbs, kinds, send, recv):
    x, y, c, chips = _place()
    cps = []
    for a, (src, lb, kind) in enumerate(zip(srcs, lbs, kinds)):
        C = lb.shape[3]
        for k, chip in enumerate(chips):
            s = 2 * chip[0] + chip[1]
            part = src.at[:, :, pl.ds(pl.multiple_of(s * C, LANES), C)] if kind == "col" else src.at[:, s]
            cps.append(_remote(part, lb.at[k], send, recv, a * (N_CHIPS - 1) + k, (*chip, c)))
    return cps


def _share_halves(g4s):
    n = len(g4s)
    out_shape = [jax.ShapeDtypeStruct(g.shape, g.dtype) for g in g4s]

    def body(*refs):
        outs, (send, recv) = refs[n:2 * n], refs[2 * n:]
        x, y, c, _ = _place()
        cps = [_remote(outs[a].at[:, c], outs[a].at[:, c], send, recv, a, (x, y, 1 - c)) for a in range(n)]
        for cp in cps:
            cp.start()
        for cp in cps:
            cp.wait()

    return _comm_call(body, "share_halves", g4s, out_shape, n, 0, {a: a for a in range(n)})


def _allreduce_small(part):
    N = part.shape[0]

    def body(p_ref, o_ref, buf, send, recv):
        x, y, c, _ = _place()
        me = 4 * x + 2 * y + c
        buf[me] = p_ref[...]
        cps = []
        for k in range(1, N_DEV):
            peer = (1 - x if k & 4 else x, 1 - y if k & 2 else y, 1 - c if k & 1 else c)
            cps.append(_remote(p_ref, buf.at[me], send, recv, k - 1, peer))
        for cp in cps:
            cp.start()
        for cp in cps:
            cp.wait()
        acc = buf[0]
        for i in range(1, N_DEV):
            acc = acc + buf[i]
        o_ref[...] = acc

    vmem = pl.BlockSpec(memory_space=pltpu.VMEM)
    return pl.pallas_call(
        body, name="allreduce_small", in_specs=[vmem], out_specs=vmem,
        out_shape=jax.ShapeDtypeStruct((N, LANES), F32),
        scratch_shapes=[pltpu.VMEM((N_DEV, N, LANES), F32), pltpu.SemaphoreType.DMA((N_DEV - 1,)),
                        pltpu.SemaphoreType.DMA((N_DEV - 1,))],
        compiler_params=pltpu.CompilerParams(has_side_effects=True, vmem_limit_bytes=VMEM_LIMIT_BYTES),
    )(part)


AFTER_ATTENTION = ("w_out", "w_up", "w_down")


def _layer_fwd(x, p, fulls=(), kinds=()):
    h1t, proj = _norm_matmul(x, p["norm1_g"], p["w_in"])
    attn, ltot, attnt, filled = _attn_fwd(proj, p["q_norm_g"], p["k_norm_g"], fulls, kinds)
    c1, passed = _glu_conv_fwd(proj, p["conv_dw_w"], p["conv_dw_b"], _pass_on_guest(filled, kinds) if filled else None)
    if filled:
        filled = passed
        p = dict(p, **{n: f[0] for n, f in zip(AFTER_ATTENTION, filled)})
    c, ct = _ln_silu_fwd(c1, p["conv_ln_g"], p["conv_ln_b"])
    x_mid = _matmul_res([attn, c], p["w_out"], x)
    h2t, u = _norm_matmul(x_mid, p["norm2_g"], p["w_up"])
    act, actt = _ffn_act_fwd(u, p["ffn_dw_w"], p["ffn_dw_b"])
    x_out = _matmul_res([act], p["w_down"], x_mid)
    saved = dict(x=x, h1t=h1t, proj=proj, attnt=attnt, ltot=ltot, c1=c1, ct=ct, x_mid=x_mid, h2t=h2t, u=u, actt=actt)
    return x_out, saved, filled


def _chip_partials(gs, kinds, core):
    g4s = _halves_view(gs, kinds)
    return _add_halves(g4s, _exchange_halves(g4s), kinds, core)


def _halves_view(gs, kinds):
    return [g.reshape(1, 2, g.shape[0] // 2, g.shape[1]) if kind == "col"
            else g.reshape(N_CHIPS, 2, g.shape[0] // N_CHIPS // 2, g.shape[1]) for g, kind in zip(gs, kinds)]


def _add_halves(g4s, received, kinds, core):
    parts = [_add_half(g4, la, core) for g4, la in zip(g4s, received)]
    return [p if kind == "col" else p.reshape(1, N_CHIPS, p.shape[1], p.shape[2]) for p, kind in zip(parts, kinds)]


def _pass_on_guest(fulls, kinds):
    n = len(fulls)
    return _Guest("pass_on", list(fulls), [jax.ShapeDtypeStruct(f.shape, f.dtype) for f in fulls], {a: a for a in range(n)},
                  lambda ins, outs, send, recv: _pass_on_copies(outs, kinds, send, recv, 0), n * (N_CHIPS - 1))


def _exchange_guest(g4s):
    def copies(ins, outs, send, recv):
        x, y, c, _ = _place()
        return [_remote(g.at[:, 1 - c], la, send, recv, a, (x, y, 1 - c)) for a, (g, la) in enumerate(zip(ins, outs))]

    out_shape = [jax.ShapeDtypeStruct((g.shape[0],) + g.shape[2:], g.dtype) for g in g4s]
    return _Guest("exchange", list(g4s), out_shape, {}, copies, len(g4s))


def _share_guest(g4s):
    def copies(ins, outs, send, recv):
        x, y, c, _ = _place()
        return [_remote(o.at[:, c], o.at[:, c], send, recv, a, (x, y, 1 - c)) for a, o in enumerate(outs)]

    n = len(g4s)
    return _Guest("share", list(g4s), [jax.ShapeDtypeStruct(g.shape, g.dtype) for g in g4s], {a: a for a in range(n)}, copies, n)


def _owned_sums(parts, landed, kinds, place):
    halves = [_add_parts(p, lb, place, kind) for p, lb, kind in zip(parts, landed, kinds)]
    return [g.reshape(2 * g.shape[2], g.shape[3]) for g in _share_halves(halves)]


def _layer_bwd(dx_out, s, p, comm=None):
    g = {}
    dact = _matmul_nt([dx_out], p["w_down"], BF16)
    g["w_down"] = _matmul_tn([s["actt"]], [dx_out], tk=D_FF // 2, tn=D_MODEL)
    dgate, dval, g["ffn_dw_w"], g["ffn_dw_b"] = _ffn_act_bwd(s["u"], p["ffn_dw_w"], p["ffn_dw_b"], dact)
    dx_mid, g["norm2_g"], _ = _matmul_nt_rmsbwd([dgate, dval], p["w_up"], s["x_mid"], p["norm2_g"], dx_out)
    g["w_up"] = _matmul_tn([s["h2t"]], [dgate, dval], tk=D_MODEL, tn=D_FF // 2)
    dmix = _matmul_nt([dx_mid], p["w_out"], F32)
    g["w_out"] = _matmul_tn([s["attnt"], s["ct"]], [dx_mid], tk=ATTN_WIDTH, tn=D_MODEL)
    dc1, g["conv_ln_g"], g["conv_ln_b"] = _ln_silu_bwd(s["c1"], p["conv_ln_g"], p["conv_ln_b"], dmix)
    parts, kinds, exchange, share = [], [], None, None
    if comm is not None:
        core, place, pending = comm
        kinds = [BIG_KIND[n] for n in AFTER_ATTENTION] + ["col"] * len(pending)
        g4s = _halves_view([g[n] for n in AFTER_ATTENTION] + list(pending), kinds)
        exchange = _exchange_guest(g4s)
    da, db, g["conv_dw_w"], g["conv_dw_b"], received = _glu_conv_bwd(s["proj"], p["conv_dw_w"], dc1, exchange)
    if comm is not None:
        parts = _add_halves(g4s, received, kinds, core)
    (dq, dk, dv, g["q_norm_g"], g["k_norm_g"]), landed = _attn_bwd(
        s["proj"], p["q_norm_g"], p["k_norm_g"], s["ltot"], dmix, parts, kinds)
    if comm is not None:
        share = _share_guest([_add_parts(pt, lb, place, kind) for pt, lb, kind in zip(parts, landed, kinds)])
    pieces = [dq, dk, dv, da, db]
    dx, g["norm1_g"], shared = _matmul_nt_rmsbwd(pieces, p["w_in"], s["x"], p["norm1_g"], dx_mid, share)
    g["w_in"] = _matmul_tn([s["h1t"]], pieces, tk=D_MODEL, tn=ATTN_WIDTH)
    return dx, g, [a.reshape(2 * a.shape[2], a.shape[3]) for a in shared]


WEIGHTS = ("norm1_g", "w_in", "q_norm_g", "k_norm_g", "conv_dw_w", "conv_dw_b", "conv_ln_g", "conv_ln_b",
           "w_out", "norm2_g", "w_up", "ffn_dw_w", "ffn_dw_b", "w_down")
BIG = ("w_in", "w_out", "w_up", "w_down")
BIG_KIND = {"w_in": "col", "w_out": "row", "w_up": "col", "w_down": "row"}
SMALL_SHARDED = ("conv_dw_w", "ffn_dw_w")
REPLICATED = tuple(n for n in WEIGHTS if n not in BIG + SMALL_SHARDED)


def _pack(arrays):
    flat = jnp.concatenate([a.reshape(-1) for a in arrays])
    rows = -(-flat.shape[0] // (SUBLANES * LANES)) * SUBLANES
    return jnp.pad(flat, (0, rows * LANES - flat.shape[0])).reshape(rows, LANES)


def _unpack(packed, shapes):
    flat = packed.reshape(-1)
    out, off = [], 0
    for shape in shapes:
        size = 1
        for d in shape:
            size *= d
        out.append(flat[off:off + size].reshape(shape))
        off += size
    return out


def _unshard_last(stacked):
    n, L, K, C = stacked.shape
    return jnp.transpose(stacked, (1, 2, 0, 3)).reshape(L, K, n * C)


def kernel(x, norm1_g, w_in, q_norm_g, k_norm_g, conv_dw_w, conv_dw_b, conv_ln_g, conv_ln_b, w_out, norm2_g, w_up, ffn_dw_w, ffn_dw_b, w_down, loss_target, m_norm1_g, m_w_in, m_q_norm_g, m_k_norm_g, m_conv_dw_w, m_conv_dw_b, m_conv_ln_g, m_conv_ln_b, m_w_out, m_norm2_g, m_w_up, m_ffn_dw_w, m_ffn_dw_b, m_w_down, v_norm1_g, v_w_in, v_q_norm_g, v_k_norm_g, v_conv_dw_w, v_conv_dw_b, v_conv_ln_g, v_conv_ln_b, v_w_out, v_norm2_g, v_w_up, v_ffn_dw_w, v_ffn_dw_b, v_w_down):
    given = dict(locals())
    w = {n: given[n] for n in WEIGHTS}
    m = {n: given["m_" + n] for n in WEIGHTS}
    v = {n: given["v_" + n] for n in WEIGHTS}
    chip = 2 * lax.axis_index("x") + lax.axis_index("y")
    core = lax.axis_index("c")
    chip_arr = jnp.reshape(chip, (1,)).astype(jnp.int32)
    core_arr = jnp.reshape(core, (1,)).astype(jnp.int32)
    L = DEPTH

    place = jnp.concatenate([chip_arr, core_arr])

    full = [{n: _cast_into_full(w[n], l, BIG_KIND[n], chip_arr) for n in BIG} for l in range(L)]
    full[0]["w_in"] = _gather_weights([full[0]["w_in"]], ["col"])[0]
    small_full = {n: _unshard_last(stacked)
                  for n, stacked in zip(SMALL_SHARDED, _gather_small([w[n] for n in SMALL_SHARDED]))}
    params = []
    for l in range(L):
        p = {n: small_full[n][l] for n in SMALL_SHARDED}
        p.update({n: w[n][l][None] for n in REPLICATED})
        params.append(p)

    act = x[0]
    saved = []
    for l in range(L):
        group = [(l, n) for n in AFTER_ATTENTION] + ([(l + 1, "w_in")] if l + 1 < L else [])
        act, s, filled = _layer_fwd(act, dict(params[l], w_in=full[l]["w_in"][0]),
                                    [full[i][n] for i, n in group], [BIG_KIND[n] for _, n in group])
        for (i, n), f in zip(group, filled):
            full[i][n] = f
        saved.append(s)
    for l in range(L):
        params[l].update({n: full[l][n][0] for n in BIG})
    dx, loss_part = _loss_grad(act, loss_target[0])
    loss = lax.psum(loss_part, ("x", "y", "c"))

    grads = [None] * L
    summed = {}
    pending = []
    for l in reversed(range(L)):
        dx, grads[l], sums = _layer_bwd(dx, saved[l], params[l], (core_arr, place, pending))
        names = [(l, n) for n in AFTER_ATTENTION] + ([(l + 1, "w_in")] if pending else [])
        summed.update(zip(names, sums))
        pending = [grads[l]["w_in"]]
    parts = _chip_partials(pending, ["col"], core_arr)
    summed[0, "w_in"] = _owned_sums(parts, _scatter_partials(parts, ["col"]), ["col"], place)[0]

    grad, delta, new_m, new_v = {}, {}, {}, {}
    for n in BIG:
        out = None
        for l in range(L):
            out = _adamw_layer(l, w[n], summed[l, n], m[n], v[n], out)
        grad[n], delta[n], new_m[n], new_v[n] = out

    small = REPLICATED + SMALL_SHARDED
    small_grads = [jnp.stack([grads[l][n] for l in range(L)]) for n in small]
    small_sums = _unpack(_allreduce_small(_pack(small_grads)), [a.shape for a in small_grads])
    for n, g in zip(small, small_sums):
        if n in REPLICATED:
            grad[n] = g.reshape(w[n].shape)
        else:
            width = w[n].shape[2]
            grad[n] = lax.dynamic_slice_in_dim(g, chip * width, width, axis=2)
    shapes = [w[n].shape for n in small]
    packed = _adamw(*[_pack([src[n] for n in small]) for src in (w, grad, m, v)])
    for out, pk in zip((delta, new_m, new_v), packed):
        out.update(zip(small, _unpack(pk, shapes)))

    return (loss, dx[None], *[grad[n] for n in WEIGHTS], *[delta[n] for n in WEIGHTS],
            *[new_m[n] for n in WEIGHTS], *[new_v[n] for n in WEIGHTS])
```

```python
import functools

import jax
import jax.numpy as jnp
from jax import lax
from jax.experimental import pallas as pl
from jax.experimental.pallas import tpu as pltpu

F32 = jnp.float32
BF16 = jnp.bfloat16

DEPTH = 4
D_MODEL = 1024
HEADS = 8
HEAD_DIM = 64
ATTN_WIDTH = HEADS * HEAD_DIM
CONV_WIDTH = D_MODEL - ATTN_WIDTH
CONV_KERNEL = 31
D_FF = 2816
FFN_KERNEL = 3
EPS = 1e-6
ADAM_LR, ADAM_B1, ADAM_B2, ADAM_EPS, ADAM_WD, ADAM_STEP = 0.001, 0.9, 0.999, 1e-08, 0.01, 10

N_CHIPS = 4
N_DEV = 8
LANES = 128
SUBLANES = 8
VMEM_LIMIT_BYTES = 56 * 2**20
ATTN_TILE = 256
ATTN_HEADS_PER_STEP = 4
ATTN_BLOCK = ATTN_HEADS_PER_STEP * HEAD_DIM
CONV_PAD = 32
FFN_PAD = 8
MESH = pl.DeviceIdType.MESH


def _params(*sem):
    return pltpu.CompilerParams(dimension_semantics=sem if sem else None, vmem_limit_bytes=VMEM_LIMIT_BYTES)


class _Guest:
    def __init__(self, name, ins, out_shape, aliases, copies, n_sem):
        self.name, self.ins, self.out_shape, self.aliases, self.copies, self.n_sem = name, ins, out_shape, aliases, copies, n_sem


def _hosted_call(body, guest, *, name, grid, in_specs, out_specs, out_shape, scratch_shapes, operands):
    if guest is None:
        out = pl.pallas_call(body, name=name, grid=grid, in_specs=in_specs, out_specs=out_specs, out_shape=out_shape,
                             scratch_shapes=scratch_shapes, compiler_params=_params("arbitrary"))(*operands)
        return list(out), []
    n_in, n_out, n_scr = len(in_specs), len(out_specs), len(scratch_shapes)
    gi, go = len(guest.ins), len(guest.out_shape)

    def hosting(*refs):
        ins, g_in = refs[:n_in], refs[n_in:n_in + gi]
        outs, g_out = refs[n_in + gi:n_in + gi + n_out], refs[n_in + gi + n_out:n_in + gi + n_out + go]
        scratch, (send, recv) = refs[n_in + gi + n_out + go:-2], refs[-2:]

        @pl.when(pl.program_id(0) == 0)
        def _():
            for cp in guest.copies(g_in, g_out, send, recv):
                cp.start()

        body(*ins, *outs, *scratch)

        @pl.when(pl.program_id(0) == grid[0] - 1)
        def _():
            for cp in guest.copies(g_in, g_out, send, recv):
                cp.wait()

    out = pl.pallas_call(
        hosting, name=name + "_" + guest.name, grid=grid,
        in_specs=list(in_specs) + [ANY] * gi, out_specs=list(out_specs) + [ANY] * go,
        out_shape=list(out_shape) + list(guest.out_shape),
        scratch_shapes=list(scratch_shapes) + [pltpu.SemaphoreType.DMA((guest.n_sem,))] * 2,
        input_output_aliases={n_in + a: n_out + b for a, b in guest.aliases.items()},
        compiler_params=pltpu.CompilerParams(dimension_semantics=("arbitrary",), vmem_limit_bytes=VMEM_LIMIT_BYTES,
                                             has_side_effects=True),
    )(*operands, *guest.ins)
    return list(out[:n_out]), list(out[n_out:])


def _dot(a, b):
    return jnp.dot(a, b, preferred_element_type=F32)


def _dot_nt(a, b):
    return lax.dot_general(a, b, (((1,), (1,)), ((), ())), preferred_element_type=F32)


def _dot_tn(a, b):
    return lax.dot_general(a, b, (((0,), (0,)), ((), ())), preferred_element_type=F32)


def _sigmoid(x):
    return 1.0 / (1.0 + jnp.exp(-x))


def _norm_matmul(x, g, w, guest=None, *, tm=256):
    S, D = x.shape
    N = w.shape[1]

    def body(x_ref, g_ref, w_ref, ht_ref, y_ref):
        xv = x_ref[...]
        h = xv * lax.rsqrt(jnp.mean(xv * xv, axis=-1, keepdims=True) + EPS) * g_ref[...]
        ht_ref[...] = h.T.astype(BF16)
        y_ref[...] = _dot(h.astype(BF16), w_ref[...])

    (ht, y), guest_out = _hosted_call(
        body, guest, name="norm_matmul", grid=(S // tm,),
        in_specs=[pl.BlockSpec((tm, D), lambda i: (i, 0)),
                  pl.BlockSpec((1, D), lambda i: (0, 0)),
                  pl.BlockSpec((D, N), lambda i: (0, 0))],
        out_specs=[pl.BlockSpec((D, tm), lambda i: (0, i)),
                   pl.BlockSpec((tm, N), lambda i: (i, 0))],
        out_shape=[jax.ShapeDtypeStruct((D, S), BF16), jax.ShapeDtypeStruct((S, N), F32)],
        scratch_shapes=[], operands=[x, g, w])
    return ht, y, guest_out


def _matmul_res(pieces, w, res, guest=None, *, tm=512):
    S, N = res.shape
    K = w.shape[0]
    widths = [p.shape[1] for p in pieces]
    assert sum(widths) == K

    def body(*refs):
        p_refs, (w_ref, res_ref, o_ref) = refs[:len(pieces)], refs[len(pieces):]
        acc = res_ref[...]
        off = 0
        for p_ref, kp in zip(p_refs, widths):
            acc = acc + _dot(p_ref[...], w_ref[off:off + kp, :])
            off += kp
        o_ref[...] = acc

    (out,), guest_out = _hosted_call(
        body, guest, name="matmul_res", grid=(S // tm,),
        in_specs=[pl.BlockSpec((tm, kp), lambda i: (i, 0)) for kp in widths]
        + [pl.BlockSpec((K, N), lambda i: (0, 0)), pl.BlockSpec((tm, N), lambda i: (i, 0))],
        out_specs=[pl.BlockSpec((tm, N), lambda i: (i, 0))],
        out_shape=[jax.ShapeDtypeStruct((S, N), F32)],
        scratch_shapes=[], operands=[*pieces, w, res])
    return out, guest_out


def _nt_sum(p_refs, widths, w_ref):
    acc = None
    off = 0
    for p_ref, n_p in zip(p_refs, widths):
        d = _dot_nt(p_ref[...].astype(BF16), w_ref[:, off:off + n_p])
        acc = d if acc is None else acc + d
        off += n_p
    return acc


def _matmul_nt(pieces, w, out_dtype, guest=None, *, tm=512):
    S = pieces[0].shape[0]
    K, N = w.shape
    widths = [p.shape[1] for p in pieces]
    assert sum(widths) == N

    def body(*refs):
        p_refs, (w_ref, o_ref) = refs[:len(pieces)], refs[len(pieces):]
        o_ref[...] = _nt_sum(p_refs, widths, w_ref).astype(out_dtype)

    (out,), guest_out = _hosted_call(
        body, guest, name="matmul_nt", grid=(S // tm,),
        in_specs=[pl.BlockSpec((tm, n_p), lambda i: (i, 0)) for n_p in widths]
        + [pl.BlockSpec((K, N), lambda i: (0, 0))],
        out_specs=[pl.BlockSpec((tm, K), lambda i: (i, 0))],
        out_shape=[jax.ShapeDtypeStruct((S, K), out_dtype)],
        scratch_shapes=[], operands=[*pieces, w])
    return out, guest_out


def _matmul_nt_rmsbwd(pieces, w, x, g, dres, guest=None, *, tm=256):
    S, K = x.shape
    N = w.shape[1]
    widths = [p.shape[1] for p in pieces]
    assert sum(widths) == N

    def body(*refs):
        p_refs, (w_ref, x_ref, g_ref, dres_ref, dx_ref, dg_ref) = refs[:len(pieces)], refs[len(pieces):]
        dh = _nt_sum(p_refs, widths, w_ref)
        xv = x_ref[...]
        r = lax.rsqrt(jnp.mean(xv * xv, axis=-1, keepdims=True) + EPS)
        xh = xv * r
        dxh = dh * g_ref[...]
        dx_ref[...] = dres_ref[...] + r * (dxh - xh * jnp.mean(dxh * xh, axis=-1, keepdims=True))

        @pl.when(pl.program_id(0) == 0)
        def _():
            dg_ref[...] = jnp.zeros_like(dg_ref)

        dg_ref[...] += jnp.sum(dh * xh, axis=0, keepdims=True)

    (dx, dg), guest_out = _hosted_call(
        body, guest, name="matmul_nt_rmsbwd", grid=(S // tm,),
        in_specs=[pl.BlockSpec((tm, n_p), lambda i: (i, 0)) for n_p in widths]
        + [pl.BlockSpec((K, N), lambda i: (0, 0)), pl.BlockSpec((tm, K), lambda i: (i, 0)),
           pl.BlockSpec((1, K), lambda i: (0, 0)), pl.BlockSpec((tm, K), lambda i: (i, 0))],
        out_specs=[pl.BlockSpec((tm, K), lambda i: (i, 0)), pl.BlockSpec((1, K), lambda i: (0, 0))],
        out_shape=[jax.ShapeDtypeStruct((S, K), F32), jax.ShapeDtypeStruct((1, K), F32)],
        scratch_shapes=[], operands=[*pieces, w, x, g, dres])
    return dx, dg, guest_out


def _matmul_tn(xts, dys, *, tk, tn, ts=1024):
    S = dys[0].shape[0]
    xs = xts
    n_s = S // ts
    (mt,) = {x.shape[0] // tk for x in xts}
    (nt,) = {d.shape[1] // tn for d in dys}

    def body(*refs):
        x_refs, dy_refs, (o_ref, acc_ref) = refs[:len(xs)], refs[len(xs):len(xs) + len(dys)], refs[len(xs) + len(dys):]
        i, j, s = pl.program_id(0), pl.program_id(1), pl.program_id(2)

        @pl.when(s == 0)
        def _():
            acc_ref[...] = jnp.zeros_like(acc_ref)

        for a, x_ref in enumerate(x_refs):
            for b, dy_ref in enumerate(dy_refs):
                @pl.when((i // mt == a) & (j // nt == b))
                def _():
                    acc_ref[...] += _dot(x_ref[...], dy_ref[...].astype(BF16))

        @pl.when(s == n_s - 1)
        def _():
            o_ref[...] = acc_ref[...].astype(BF16)

    def x_map(a):
        return lambda i, j, s: (jnp.where(i // mt == a, i % mt, 0), jnp.where(i // mt == a, s, 0))

    def dy_map(b):
        return lambda i, j, s: (jnp.where(j // nt == b, s, 0), jnp.where(j // nt == b, j % nt, 0))

    return pl.pallas_call(
        body, name="matmul_tn", grid=(len(xs) * mt, len(dys) * nt, n_s),
        in_specs=[pl.BlockSpec((tk, ts), x_map(a)) for a in range(len(xs))]
        + [pl.BlockSpec((ts, tn), dy_map(b)) for b in range(len(dys))],
        out_specs=pl.BlockSpec((tk, tn), lambda i, j, s: (i, j)),
        out_shape=jax.ShapeDtypeStruct((len(xs) * mt * tk, len(dys) * nt * tn), BF16),
        scratch_shapes=[pltpu.VMEM((tk, tn), F32)],
        compiler_params=_params("parallel", "parallel", "arbitrary"),
    )(*xs, *dys)


def _tri_consts():
    j = jnp.arange(ATTN_TILE)[:, None]
    s = jnp.arange(ATTN_TILE)[None, :]
    return (j > s).astype(BF16), (j <= s).astype(BF16), (j < s).astype(BF16)


SIGN_BIT = 0x80000000
WEIGHT_IS_ZERO = -104.0


def _log_terms(sn):
    minus_abs = lax.bitcast_convert_type(lax.bitcast_convert_type(sn, jnp.uint32) | jnp.uint32(SIGN_BIT), F32)
    lom = jnp.minimum(sn, 0.0) - jnp.log(1.0 + jnp.exp(minus_abs))
    return lom, lom - sn


def _causal_mask():
    t = lax.broadcasted_iota(jnp.int32, (ATTN_TILE, ATTN_TILE), 0)
    s = lax.broadcasted_iota(jnp.int32, (ATTN_TILE, ATTN_TILE), 1)
    return s < t


def _attn_prep(h, q_ref, k_ref, v_ref, qg_ref, kg_ref, qn_s, kn_s, vb_s, n_tiles):
    T = ATTN_TILE
    lanes = slice(HEAD_DIM * h, HEAD_DIM * (h + 1))
    scale = -(HEAD_DIM ** -0.5)

    def prep(i, carry):
        rows = pl.ds(pl.multiple_of(i * T, T), T)
        q = q_ref[rows, lanes]
        k = k_ref[rows, lanes]
        rq = lax.rsqrt(jnp.mean(q * q, axis=-1, keepdims=True) + EPS)
        rk = lax.rsqrt(jnp.mean(k * k, axis=-1, keepdims=True) + EPS)
        qn_s[h, rows, :] = (q * rq * qg_ref[...] * scale).astype(BF16)
        kn_s[h, rows, :] = (k * rk * kg_ref[...]).astype(BF16)
        vb_s[h, rows, :] = v_ref[rows, lanes].astype(BF16)
        return carry

    lax.fori_loop(0, n_tiles, prep, 0)


def _attn_fwd(proj, qg, kg, fulls=(), kinds=()):
    S = proj.shape[0]
    n_comm = len(fulls)
    T = ATTN_TILE
    n_tiles = S // T
    suffix, _, _ = _tri_consts()
    pairs = HEADS // ATTN_HEADS_PER_STEP
    q_blk, k_blk, v_blk = 0, ATTN_WIDTH // ATTN_BLOCK, 2 * ATTN_WIDTH // ATTN_BLOCK

    def body(*refs):
        q_ref, k_ref, v_ref, qg_ref, kg_ref, tri_ref = refs[:6]
        o_ref, lt_ref, ot_ref, start_ref = refs[6 + n_comm:10 + n_comm]
        w_refs = refs[10 + n_comm:10 + 2 * n_comm]
        qn_s, kn_s, vb_s = refs[10 + 2 * n_comm:13 + 2 * n_comm]
        sems = refs[13 + 2 * n_comm:]
        step_id = pl.program_id(0)
        if n_comm:
            @pl.when(pl.program_id(0) == 0)
            def _():
                for cp in _gather_copies(w_refs, kinds, *sems, 0):
                    cp.start()

        heads = range(ATTN_HEADS_PER_STEP)
        for h in heads:
            _attn_prep(h, q_ref, k_ref, v_ref, qg_ref, kg_ref, qn_s, kn_s, vb_s, n_tiles)

        def q_tile(qi, carry0):
            qrows = pl.ds(pl.multiple_of(qi * T, T), T)
            qt = [qn_s[h, qrows, :] for h in heads]

            def tile(kj, carry, diag):
                krows = pl.ds(pl.multiple_of(kj * T, T), T)
                s = [_dot_nt(qt[h], kn_s[h, krows, :]) for h in heads]
                terms = [_log_terms(s[h]) for h in heads]
                lom = [terms[h][0] for h in heads]
                if diag:
                    mask = _causal_mask()
                    lom = [jnp.where(mask, lom[h], 0.0) for h in heads]
                lom = [lom[h].astype(BF16) for h in heads]
                tail = [_dot(lom[h], tri_ref[...]) for h in heads]
                w = [jnp.exp(terms[h][1] + tail[h] + carry[h][1]) for h in heads]
                if diag:
                    w = [jnp.where(mask, w[h], 0.0) for h in heads]
                return tuple((carry[h][0] + _dot(w[h].astype(BF16), vb_s[h, krows, :]),
                              carry[h][1] + (tail[h][:, 0:1] + lom[h][:, 0:1].astype(F32))) for h in heads)

            init = tuple((jnp.zeros((T, HEAD_DIM), F32), jnp.zeros((T, 1), F32)) for h in heads)
            def alive(cr):
                worst = cr[0][1]
                for h in heads[1:]:
                    worst = jnp.maximum(worst, cr[h][1])
                return jnp.max(worst) >= WEIGHT_IS_ZERO

            def step(state):
                t, _, cr = state
                cr = tile(qi - 1 - t, cr, False)
                return t + 1, alive(cr), cr

            first = tile(qi, init, True)
            swept, _, carry = lax.while_loop(lambda st: (st[0] < qi) & st[1], step, (jnp.int32(0), alive(first), first))
            start_ref[step_id, qi] = (qi - swept).astype(F32)
            o = jnp.concatenate([carry[h][0] for h in heads], axis=1)
            o_ref[qrows, :] = o.astype(BF16)
            ot_ref[:, qrows] = o.T.astype(BF16)
            for h in heads:
                lt_ref[qrows, HEAD_DIM * h:HEAD_DIM * (h + 1)] = jnp.broadcast_to(carry[h][1], (T, HEAD_DIM))
            return carry0

        lax.fori_loop(0, n_tiles, q_tile, 0)
        if n_comm:
            @pl.when(pl.program_id(0) == pairs - 1)
            def _():
                for cp in _gather_copies(w_refs, kinds, *sems, 0):
                    cp.wait()

    n_sem = n_comm * (N_CHIPS - 1)
    out = pl.pallas_call(
        body, name="attn_fwd_gather" if n_comm else "attn_fwd", grid=(pairs,),
        in_specs=[pl.BlockSpec((S, ATTN_BLOCK), lambda p: (0, q_blk + p)),
                  pl.BlockSpec((S, ATTN_BLOCK), lambda p: (0, k_blk + p)),
                  pl.BlockSpec((S, ATTN_BLOCK), lambda p: (0, v_blk + p)),
                  pl.BlockSpec((1, HEAD_DIM), lambda p: (0, 0)),
                  pl.BlockSpec((1, HEAD_DIM), lambda p: (0, 0)),
                  pl.BlockSpec((T, T), lambda p: (0, 0))] + [ANY] * n_comm,
        out_specs=[pl.BlockSpec((S, ATTN_BLOCK), lambda p: (0, p)),
                   pl.BlockSpec((None, S, ATTN_BLOCK), lambda p: (p, 0, 0)),
                   pl.BlockSpec((ATTN_BLOCK, S), lambda p: (p, 0)),
                   pl.BlockSpec(memory_space=pltpu.SMEM)] + [ANY] * n_comm,
        out_shape=[jax.ShapeDtypeStruct((S, ATTN_WIDTH), BF16),
                   jax.ShapeDtypeStruct((pairs, S, ATTN_BLOCK), F32),
                   jax.ShapeDtypeStruct((ATTN_WIDTH, S), BF16),
                   jax.ShapeDtypeStruct((pairs, n_tiles), F32)] + [jax.ShapeDtypeStruct(f.shape, f.dtype) for f in fulls],
        scratch_shapes=[pltpu.VMEM((ATTN_HEADS_PER_STEP, S, HEAD_DIM), BF16)] * 3
        + ([pltpu.SemaphoreType.DMA((n_sem,))] * 2 if n_comm else []),
        input_output_aliases={6 + a: 4 + a for a in range(n_comm)},
        compiler_params=pltpu.CompilerParams(dimension_semantics=("arbitrary",), vmem_limit_bytes=VMEM_LIMIT_BYTES,
                                             has_side_effects=bool(n_comm)),
    )(proj, proj, proj, qg, kg, suffix, *fulls)
    return out[0], (out[1], out[3]), out[2], list(out[4:])


def _attn_bwd(proj, qg, kg, ltot, dmix, parts=(), kinds=()):
    S = proj.shape[0]
    n_comm = len(parts)
    T = ATTN_TILE
    n_tiles = S // T
    _, prefix_incl, prefix_excl = _tri_consts()
    pairs = HEADS // ATTN_HEADS_PER_STEP
    q_blk, k_blk, v_blk = 0, ATTN_WIDTH // ATTN_BLOCK, 2 * ATTN_WIDTH // ATTN_BLOCK
    scale = HEAD_DIM ** -0.5

    def body(*refs):
        q_ref, k_ref, v_ref, qg_ref, kg_ref, lt_ref, do_ref, ti_ref, te_ref, start_ref = refs[:10]
        p_refs = refs[10:10 + n_comm]
        dq_ref, dk_ref, dv_ref, dqg_ref, dkg_ref = refs[10 + n_comm:15 + n_comm]
        lb_refs = refs[15 + n_comm:15 + 2 * n_comm]
        qn_s, kn_s, vb_s, dq_s, dk_s, dv_s = refs[15 + 2 * n_comm:21 + 2 * n_comm]
        sems = refs[21 + 2 * n_comm:]
        step_id = pl.program_id(0)

        @pl.when(pl.program_id(0) == 0)
        def _():
            dqg_ref[...] = jnp.zeros_like(dqg_ref)
            dkg_ref[...] = jnp.zeros_like(dkg_ref)
            for cp in _scatter_copies(p_refs, lb_refs, kinds, *sems) if n_comm else ():
                cp.start()

        heads = range(ATTN_HEADS_PER_STEP)
        for h in heads:
            _attn_prep(h, q_ref, k_ref, v_ref, qg_ref, kg_ref, qn_s, kn_s, vb_s, n_tiles)
        dk_s[...] = jnp.zeros_like(dk_s)
        dv_s[...] = jnp.zeros_like(dv_s)

        def q_tile(qi, carry0):
            qrows = pl.ds(pl.multiple_of(qi * T, T), T)
            qt = [qn_s[h, qrows, :] for h in heads]
            dob = [do_ref[qrows, HEAD_DIM * h:HEAD_DIM * (h + 1)].astype(BF16) for h in heads]
            lt = [lt_ref[qrows, HEAD_DIM * h:HEAD_DIM * h + 1] for h in heads]

            def tile(kj, carry, diag):
                krows = pl.ds(pl.multiple_of(kj * T, T), T)
                kt = [kn_s[h, krows, :] for h in heads]
                s = [_dot_nt(qt[h], kt[h]) for h in heads]
                dw = [_dot_nt(dob[h], vb_s[h, krows, :]) for h in heads]
                terms = [_log_terms(s[h]) for h in heads]
                lom = [terms[h][0] for h in heads]
                if diag:
                    mask = _causal_mask()
                    lom = [jnp.where(mask, lom[h], 0.0) for h in heads]
                pin = [_dot(lom[h].astype(BF16), ti_ref[...]) for h in heads]
                w = [jnp.exp(terms[h][1] + ((lt[h] - carry[h][1]) - pin[h])) for h in heads]
                if diag:
                    w = [jnp.where(mask, w[h], 0.0) for h in heads]
                e = [w[h] * dw[h] for h in heads]
                gex = [_dot(e[h].astype(BF16), te_ref[...]) for h in heads]
                dz = [e[h] - jnp.exp(terms[h][1]) * (e[h] + (carry[h][2] + gex[h])) for h in heads]
                if diag:
                    dz = [jnp.where(mask, dz[h], 0.0) for h in heads]
                new = []
                for h in heads:
                    dzb = dz[h].astype(BF16)
                    dk_s[h, krows, :] += _dot_tn(dzb, qt[h])
                    dv_s[h, krows, :] += _dot_tn(w[h].astype(BF16), dob[h])
                    new.append((carry[h][0] + _dot(dzb, kt[h]),
                                carry[h][1] + pin[h][:, T - 1:T],
                                carry[h][2] + gex[h][:, T - 1:T] + e[h][:, T - 1:T]))
                return tuple(new)

            zero = jnp.zeros((T, 1), F32)
            init = tuple((jnp.zeros((T, HEAD_DIM), F32), zero, zero) for h in heads)
            first = jnp.clip(start_ref[step_id, qi].astype(jnp.int32), 0, qi)
            last = tile(qi, lax.fori_loop(first, qi, lambda kj, cr: tile(kj, cr, False), init), True)
            for h in heads:
                dq_s[h, qrows, :] = last[h][0]
            return carry0

        lax.fori_loop(0, n_tiles, q_tile, 0)

        def finish(i, carry):
            rows = pl.ds(pl.multiple_of(i * T, T), T)
            new = []
            for h in heads:
                lanes = slice(HEAD_DIM * h, HEAD_DIM * (h + 1))
                q = q_ref[rows, lanes]
                k = k_ref[rows, lanes]
                rq = lax.rsqrt(jnp.mean(q * q, axis=-1, keepdims=True) + EPS)
                rk = lax.rsqrt(jnp.mean(k * k, axis=-1, keepdims=True) + EPS)
                qh = q * rq
                kh = k * rk
                dqn = dq_s[h, rows, :] * scale
                dkn = -dk_s[h, rows, :]
                dqh = dqn * qg_ref[...]
                dkh = dkn * kg_ref[...]
                dq_ref[rows, lanes] = (rq * (dqh - qh * jnp.mean(dqh * qh, axis=-1, keepdims=True))).astype(BF16)
                dk_ref[rows, lanes] = (rk * (dkh - kh * jnp.mean(dkh * kh, axis=-1, keepdims=True))).astype(BF16)
                dv_ref[rows, lanes] = dv_s[h, rows, :].astype(BF16)
                new.append(carry[2 * h] + jnp.sum(dqn * qh, axis=0, keepdims=True))
                new.append(carry[2 * h + 1] + jnp.sum(dkn * kh, axis=0, keepdims=True))
            return tuple(new)

        zero = jnp.zeros((1, HEAD_DIM), F32)
        sums = lax.fori_loop(0, n_tiles, finish, (zero,) * (2 * len(heads)))
        dqg_ref[...] += sum(sums[0::2])
        dkg_ref[...] += sum(sums[1::2])
        if n_comm:
            @pl.when(pl.program_id(0) == pairs - 1)
            def _():
                for cp in _scatter_copies(p_refs, lb_refs, kinds, *sems):
                    cp.wait()

    blk = lambda off: pl.BlockSpec((S, ATTN_BLOCK), lambda p: (0, off + p))
    row64 = pl.BlockSpec((1, HEAD_DIM), lambda p: (0, 0))
    tri = pl.BlockSpec((T, T), lambda p: (0, 0))
    n_sem = n_comm * (N_CHIPS - 1)
    out = pl.pallas_call(
        body, name="attn_bwd_scatter" if n_comm else "attn_bwd", grid=(pairs,),
        in_specs=[blk(q_blk), blk(k_blk), blk(v_blk), row64, row64,
                  pl.BlockSpec((None, S, ATTN_BLOCK), lambda p: (p, 0, 0)), blk(0), tri, tri,
                  pl.BlockSpec(memory_space=pltpu.SMEM)] + [ANY] * n_comm,
        out_specs=[blk(0), blk(0), blk(0), row64, row64] + [ANY] * n_comm,
        out_shape=[jax.ShapeDtypeStruct((S, ATTN_WIDTH), BF16)] * 3 + [jax.ShapeDtypeStruct((1, HEAD_DIM), F32)] * 2
        + _scatter_shapes(parts, kinds),
        scratch_shapes=[pltpu.VMEM((ATTN_HEADS_PER_STEP, S, HEAD_DIM), BF16)] * 3
        + [pltpu.VMEM((ATTN_HEADS_PER_STEP, S, HEAD_DIM), F32)] * 3
        + ([pltpu.SemaphoreType.DMA((n_sem,))] * 2 if n_comm else []),
        compiler_params=pltpu.CompilerParams(dimension_semantics=("arbitrary",), vmem_limit_bytes=VMEM_LIMIT_BYTES,
                                             has_side_effects=bool(n_comm)),
    )(proj, proj, proj, qg, kg, ltot[0], dmix, prefix_incl, prefix_excl, ltot[1], *parts)
    return out[:5], list(out[5:])


def _shifted(win, n_rows):
    return [win if b == 0 else pltpu.roll(win, n_rows - b, 0) for b in range(SUBLANES)]


def _taps(variants, offsets, tm):
    return {o: variants[o % SUBLANES][(o // SUBLANES) * SUBLANES:(o // SUBLANES) * SUBLANES + tm, :] for o in offsets}


def _fold_rows(a):
    return jnp.sum(a.reshape(a.shape[0] // SUBLANES, SUBLANES, a.shape[1]), axis=0)


def _glu_conv_fwd(proj, w, bias, guest=None, *, tm=256):
    S = proj.shape[0]
    CB = LANES
    a_blk, b_blk = 3 * ATTN_WIDTH // CB, (3 * ATTN_WIDTH + CONV_WIDTH) // CB
    n_rows = tm + CONV_PAD

    def body(a_ref, b_ref, w_ref, bias_ref, c1_ref, pad_s):
        pad_s[0:CONV_PAD, :] = jnp.zeros((CONV_PAD, CB), F32)

        def fill(i, carry):
            rows = pl.ds(pl.multiple_of(i * tm, tm), tm)
            pad_s[pl.ds(pl.multiple_of(CONV_PAD + i * tm, SUBLANES), tm), :] = a_ref[rows, :] * _sigmoid(b_ref[rows, :])
            return carry

        lax.fori_loop(0, S // tm, fill, 0)

        def conv(i, carry):
            r0 = pl.multiple_of(i * tm, tm)
            taps = _taps(_shifted(pad_s[pl.ds(r0, n_rows), :], n_rows), range(2, 2 + CONV_KERNEL), tm)
            acc = jnp.broadcast_to(bias_ref[...], (tm, CB))
            for k in range(CONV_KERNEL):
                acc = acc + w_ref[k:k + 1, :] * taps[k + 2]
            c1_ref[pl.ds(r0, tm), :] = acc
            return carry

        lax.fori_loop(0, S // tm, conv, 0)

    (c1,), guest_out = _hosted_call(
        body, guest, name="glu_conv_fwd", grid=(CONV_WIDTH // CB,),
        in_specs=[pl.BlockSpec((S, CB), lambda j: (0, a_blk + j)), pl.BlockSpec((S, CB), lambda j: (0, b_blk + j)),
                  pl.BlockSpec((CONV_KERNEL, CB), lambda j: (0, j)), pl.BlockSpec((1, CB), lambda j: (0, j))],
        out_specs=[pl.BlockSpec((S, CB), lambda j: (0, j))],
        out_shape=[jax.ShapeDtypeStruct((S, CONV_WIDTH), F32)],
        scratch_shapes=[pltpu.VMEM((S + CONV_PAD, CB), F32)], operands=[proj, proj, w, bias])
    return c1, guest_out


def _glu_conv_bwd(proj, w, dc1, guest=None, *, tm=256):
    S = proj.shape[0]
    CB = LANES
    a_blk, b_blk = 3 * ATTN_WIDTH // CB, (3 * ATTN_WIDTH + CONV_WIDTH) // CB
    n_rows = tm + CONV_PAD

    def body(a_ref, b_ref, w_ref, dc1_ref, da_ref, db_ref, dw_ref, dbias_ref, pad_s, dpad_s, dw_s):
        pad_s[0:CONV_PAD, :] = jnp.zeros((CONV_PAD, CB), F32)
        dpad_s[S:S + CONV_PAD, :] = jnp.zeros((CONV_PAD, CB), F32)
        dw_s[...] = jnp.zeros_like(dw_s)

        def fill(i, carry):
            rows = pl.ds(pl.multiple_of(i * tm, tm), tm)
            pad_s[pl.ds(pl.multiple_of(CONV_PAD + i * tm, SUBLANES), tm), :] = a_ref[rows, :] * _sigmoid(b_ref[rows, :])
            dpad_s[rows, :] = dc1_ref[rows, :]
            return carry

        lax.fori_loop(0, S // tm, fill, 0)

        def conv(i, carry):
            r0 = pl.multiple_of(i * tm, tm)
            rows = pl.ds(r0, tm)
            taps = _taps(_shifted(dpad_s[pl.ds(r0, n_rows), :], n_rows), range(CONV_KERNEL), tm)
            acc = jnp.zeros((tm, CB), F32)
            for k in range(CONV_KERNEL):
                acc = acc + w_ref[k:k + 1, :] * taps[CONV_KERNEL - 1 - k]
            a = a_ref[rows, :]
            sg = _sigmoid(b_ref[rows, :])
            da_ref[rows, :] = (acc * sg).astype(BF16)
            db_ref[rows, :] = (acc * a * sg * (1.0 - sg)).astype(BF16)
            d = taps[0]
            taps = _taps(_shifted(pad_s[pl.ds(r0, n_rows), :], n_rows), range(2, 2 + CONV_KERNEL), tm)
            for k in range(CONV_KERNEL):
                dw_s[SUBLANES * k:SUBLANES * (k + 1), :] += _fold_rows(d * taps[k + 2])
            dw_s[SUBLANES * CONV_KERNEL:SUBLANES * (CONV_KERNEL + 1), :] += _fold_rows(d)
            return carry

        lax.fori_loop(0, S // tm, conv, 0)
        for k in range(CONV_KERNEL):
            dw_ref[k:k + 1, :] = jnp.sum(dw_s[SUBLANES * k:SUBLANES * (k + 1), :], axis=0, keepdims=True)
        dbias_ref[...] = jnp.sum(dw_s[SUBLANES * CONV_KERNEL:SUBLANES * (CONV_KERNEL + 1), :], axis=0, keepdims=True)

    col = lambda off: pl.BlockSpec((S, CB), lambda j: (0, off + j))
    out, guest_out = _hosted_call(
        body, guest, name="glu_conv_bwd", grid=(CONV_WIDTH // CB,),
        in_specs=[col(a_blk), col(b_blk), pl.BlockSpec((CONV_KERNEL, CB), lambda j: (0, j)), col(0)],
        out_specs=[col(0), col(0), pl.BlockSpec((CONV_KERNEL, CB), lambda j: (0, j)), pl.BlockSpec((1, CB), lambda j: (0, j))],
        out_shape=[jax.ShapeDtypeStruct((S, CONV_WIDTH), BF16)] * 2
        + [jax.ShapeDtypeStruct((CONV_KERNEL, CONV_WIDTH), F32), jax.ShapeDtypeStruct((1, CONV_WIDTH), F32)],
        scratch_shapes=[pltpu.VMEM((S + CONV_PAD, CB), F32), pltpu.VMEM((S + CONV_PAD, CB), F32),
                        pltpu.VMEM((SUBLANES * (CONV_KERNEL + 1), CB), F32)], operands=[proj, proj, w, dc1])
    return (*out, guest_out)


def _ln_stats(c1):
    mu = jnp.mean(c1, axis=-1, keepdims=True)
    xc = c1 - mu
    r = lax.rsqrt(jnp.mean(xc * xc, axis=-1, keepdims=True) + EPS)
    return xc * r, r


def _ln_silu_fwd(c1, g, b, *, tm=512):
    S, C = c1.shape

    def body(c1_ref, g_ref, b_ref, c_ref, ct_ref):
        yh, _ = _ln_stats(c1_ref[...])
        y = yh * g_ref[...] + b_ref[...]
        c = y * _sigmoid(y)
        c_ref[...] = c.astype(BF16)
        ct_ref[...] = c.T.astype(BF16)

    vec = pl.BlockSpec((1, C), lambda i: (0, 0))
    return pl.pallas_call(
        body, name="ln_silu_fwd", grid=(S // tm,),
        in_specs=[pl.BlockSpec((tm, C), lambda i: (i, 0)), vec, vec],
        out_specs=[pl.BlockSpec((tm, C), lambda i: (i, 0)), pl.BlockSpec((C, tm), lambda i: (0, i))],
        out_shape=[jax.ShapeDtypeStruct((S, C), BF16), jax.ShapeDtypeStruct((C, S), BF16)],
        compiler_params=_params("parallel"),
    )(c1, g, b)


def _ln_silu_bwd(c1, g, b, dmix, *, tm=512):
    S, C = c1.shape

    def body(c1_ref, g_ref, b_ref, dc_ref, dc1_ref, dg_ref, db_ref):
        yh, r = _ln_stats(c1_ref[...])
        y = yh * g_ref[...] + b_ref[...]
        sg = _sigmoid(y)
        dy = dc_ref[...] * (sg * (1.0 + y * (1.0 - sg)))
        dyh = dy * g_ref[...]
        dc1_ref[...] = r * (dyh - jnp.mean(dyh, axis=-1, keepdims=True)
                            - yh * jnp.mean(dyh * yh, axis=-1, keepdims=True))

        @pl.when(pl.program_id(0) == 0)
        def _():
            dg_ref[...] = jnp.zeros_like(dg_ref)
            db_ref[...] = jnp.zeros_like(db_ref)

        dg_ref[...] += jnp.sum(dy * yh, axis=0, keepdims=True)
        db_ref[...] += jnp.sum(dy, axis=0, keepdims=True)

    vec = pl.BlockSpec((1, C), lambda i: (0, 0))
    return pl.pallas_call(
        body, name="ln_silu_bwd", grid=(S // tm,),
        in_specs=[pl.BlockSpec((tm, C), lambda i: (i, 0)), vec, vec, pl.BlockSpec((tm, C), lambda i: (i, 1))],
        out_specs=[pl.BlockSpec((tm, C), lambda i: (i, 0)), vec, vec],
        out_shape=[jax.ShapeDtypeStruct((S, C), F32), jax.ShapeDtypeStruct((1, C), F32), jax.ShapeDtypeStruct((1, C), F32)],
        compiler_params=_params("arbitrary"),
    )(c1, g, b, dmix)


FFN_CB = 256


def _ffn_gate(pad_s, w_ref, bias_ref, r0, tm):
    n_rows = tm + FFN_PAD
    taps = _taps(_shifted(pad_s[pl.ds(r0, n_rows), :], n_rows), range(FFN_PAD - 2, FFN_PAD + 1), tm)
    g1 = bias_ref[...] + w_ref[0:1, :] * taps[6] + w_ref[1:2, :] * taps[7] + w_ref[2:3, :] * taps[8]
    return g1, taps


def _ffn_act_fwd(u, w, bias, guest=None, *, tm=256):
    S = u.shape[0]
    CB = FFN_CB
    nb = D_FF // CB

    def body(g_ref, v_ref, w_ref, bias_ref, o_ref, ot_ref, pad_s):
        pad_s[0:FFN_PAD, :] = jnp.zeros((FFN_PAD, CB), F32)

        def fill(i, carry):
            pad_s[pl.ds(pl.multiple_of(FFN_PAD + i * tm, SUBLANES), tm), :] = g_ref[pl.ds(pl.multiple_of(i * tm, tm), tm), :]
            return carry

        lax.fori_loop(0, S // tm, fill, 0)

        def act(i, carry):
            r0 = pl.multiple_of(i * tm, tm)
            g1, _ = _ffn_gate(pad_s, w_ref, bias_ref, r0, tm)
            a = g1 * _sigmoid(g1) * v_ref[pl.ds(r0, tm), :]
            o_ref[pl.ds(r0, tm), :] = a.astype(BF16)
            ot_ref[:, pl.ds(r0, tm)] = a.T.astype(BF16)
            return carry

        lax.fori_loop(0, S // tm, act, 0)

    (act, actt), guest_out = _hosted_call(
        body, guest, name="ffn_act_fwd", grid=(nb,),
        in_specs=[pl.BlockSpec((S, CB), lambda j: (0, j)), pl.BlockSpec((S, CB), lambda j: (0, nb + j)),
                  pl.BlockSpec((FFN_KERNEL, CB), lambda j: (0, j)), pl.BlockSpec((1, CB), lambda j: (0, j))],
        out_specs=[pl.BlockSpec((S, CB), lambda j: (0, j)), pl.BlockSpec((CB, S), lambda j: (j, 0))],
        out_shape=[jax.ShapeDtypeStruct((S, D_FF), BF16), jax.ShapeDtypeStruct((D_FF, S), BF16)],
        scratch_shapes=[pltpu.VMEM((S + FFN_PAD, CB), F32)], operands=[u, u, w, bias])
    return act, actt, guest_out


def _ffn_act_bwd(u, w, bias, dact, guest=None, *, tm=256):
    S = u.shape[0]
    CB = FFN_CB
    nb = D_FF // CB

    def body(g_ref, v_ref, w_ref, bias_ref, da_ref, dg_ref, dv_ref, dw_ref, dbias_ref, pad_s, dpad_s, dw_s):
        pad_s[0:FFN_PAD, :] = jnp.zeros((FFN_PAD, CB), F32)
        dpad_s[S:S + FFN_PAD, :] = jnp.zeros((FFN_PAD, CB), F32)
        dw_s[...] = jnp.zeros_like(dw_s)

        def fill(i, carry):
            pad_s[pl.ds(pl.multiple_of(FFN_PAD + i * tm, SUBLANES), tm), :] = g_ref[pl.ds(pl.multiple_of(i * tm, tm), tm), :]
            return carry

        lax.fori_loop(0, S // tm, fill, 0)

        def first(i, carry):
            r0 = pl.multiple_of(i * tm, tm)
            rows = pl.ds(r0, tm)
            g1, taps = _ffn_gate(pad_s, w_ref, bias_ref, r0, tm)
            sg = _sigmoid(g1)
            da = da_ref[rows, :].astype(F32)
            dv_ref[rows, :] = (da * g1 * sg).astype(BF16)
            dg1 = da * v_ref[rows, :] * (sg * (1.0 + g1 * (1.0 - sg)))
            dpad_s[rows, :] = dg1
            for k in range(FFN_KERNEL):
                dw_s[SUBLANES * k:SUBLANES * (k + 1), :] += _fold_rows(dg1 * taps[FFN_PAD - 2 + k])
            dw_s[SUBLANES * FFN_KERNEL:SUBLANES * (FFN_KERNEL + 1), :] += _fold_rows(dg1)
            return carry

        lax.fori_loop(0, S // tm, first, 0)

        def second(i, carry):
            r0 = pl.multiple_of(i * tm, tm)
            n_rows = tm + FFN_PAD
            taps = _taps(_shifted(dpad_s[pl.ds(r0, n_rows), :], n_rows), range(FFN_KERNEL), tm)
            dg_ref[pl.ds(r0, tm), :] = (w_ref[2:3, :] * taps[0] + w_ref[1:2, :] * taps[1] + w_ref[0:1, :] * taps[2]).astype(BF16)
            return carry

        lax.fori_loop(0, S // tm, second, 0)
        for k in range(FFN_KERNEL):
            dw_ref[k:k + 1, :] = jnp.sum(dw_s[SUBLANES * k:SUBLANES * (k + 1), :], axis=0, keepdims=True)
        dbias_ref[...] = jnp.sum(dw_s[SUBLANES * FFN_KERNEL:SUBLANES * (FFN_KERNEL + 1), :], axis=0, keepdims=True)

    col = lambda off: pl.BlockSpec((S, CB), lambda j: (0, off + j))
    wspec = pl.BlockSpec((FFN_KERNEL, CB), lambda j: (0, j))
    bspec = pl.BlockSpec((1, CB), lambda j: (0, j))
    out, guest_out = _hosted_call(
        body, guest, name="ffn_act_bwd", grid=(nb,),
        in_specs=[col(0), col(nb), wspec, bspec, col(0)],
        out_specs=[col(0), col(0), wspec, bspec],
        out_shape=[jax.ShapeDtypeStruct((S, D_FF), BF16)] * 2
        + [jax.ShapeDtypeStruct((FFN_KERNEL, D_FF), F32), jax.ShapeDtypeStruct((1, D_FF), F32)],
        scratch_shapes=[pltpu.VMEM((S + FFN_PAD, CB), F32), pltpu.VMEM((S + FFN_PAD, CB), F32),
                        pltpu.VMEM((SUBLANES * (FFN_KERNEL + 1), CB), F32)], operands=[u, u, w, bias, dact])
    return (*out, guest_out)


def _loss_grad(y, target, *, tm=512):
    S, D = y.shape

    def body(y_ref, t_ref, dy_ref, l_ref):
        d = y_ref[...] - t_ref[...]
        dy_ref[...] = d * (1.0 / D)

        @pl.when(pl.program_id(0) == 0)
        def _():
            l_ref[...] = jnp.zeros_like(l_ref)

        l_ref[...] += 0.5 * jnp.sum(jnp.mean(d * d, axis=-1, keepdims=True), axis=0, keepdims=True)

    dy, l = pl.pallas_call(
        body, name="loss_grad", grid=(S // tm,),
        in_specs=[pl.BlockSpec((tm, D), lambda i: (i, 0))] * 2,
        out_specs=[pl.BlockSpec((tm, D), lambda i: (i, 0)), pl.BlockSpec((SUBLANES, LANES), lambda i: (0, 0))],
        out_shape=[jax.ShapeDtypeStruct((S, D), F32), jax.ShapeDtypeStruct((SUBLANES, LANES), F32)],
        compiler_params=_params("arbitrary"),
    )(y, target)
    return dy, l[0, 0]


def _row_tile(rows, cap=512):
    t = min(rows, cap)
    while rows % t or t % SUBLANES:
        t -= 1
    return t


def _adam_update(w, g, m, v):
    m1 = ADAM_B1 * m + (1.0 - ADAM_B1) * g
    v1 = ADAM_B2 * v + (1.0 - ADAM_B2) * (g * g)
    m_hat = m1 / (1.0 - ADAM_B1 ** ADAM_STEP)
    v_hat = v1 / (1.0 - ADAM_B2 ** ADAM_STEP)
    return -ADAM_LR * (m_hat / (jnp.sqrt(v_hat) + ADAM_EPS) + ADAM_WD * w), m1, v1


def _adamw(w, g, m, v):
    R, C = w.shape
    tr = _row_tile(R, 256)

    def body(w_ref, g_ref, m_ref, v_ref, d_ref, nm_ref, nv_ref):
        d_ref[...], nm_ref[...], nv_ref[...] = _adam_update(w_ref[...], g_ref[...], m_ref[...], v_ref[...])

    spec = pl.BlockSpec((tr, C), lambda i: (i, 0))
    return pl.pallas_call(
        body, name="adamw", grid=(R // tr,),
        in_specs=[spec] * 4, out_specs=[spec] * 3,
        out_shape=[jax.ShapeDtypeStruct((R, C), F32)] * 3,
        compiler_params=_params("parallel"),
    )(w, g, m, v)


def _adamw_layer(layer, w, g, m, v, prev):
    L, R, C = w.shape
    tr = _row_tile(R, 256)
    n_prev = 4 if prev else 0

    def body(*refs):
        w_ref, g_ref, m_ref, v_ref = refs[:4]
        go_ref, d_ref, nm_ref, nv_ref = refs[4 + n_prev:8 + n_prev]
        gv = g_ref[...]
        go_ref[...] = gv
        d_ref[...], nm_ref[...], nv_ref[...] = _adam_update(w_ref[...], gv, m_ref[...], v_ref[...])

    stacked = pl.BlockSpec((None, tr, C), lambda i: (layer, i, 0))
    return pl.pallas_call(
        body, name="adamw_layer", grid=(R // tr,),
        in_specs=[stacked, pl.BlockSpec((tr, C), lambda i: (i, 0)), stacked, stacked] + [ANY] * n_prev,
        out_specs=[stacked] * 4, out_shape=[jax.ShapeDtypeStruct((L, R, C), F32)] * 4,
        input_output_aliases={4 + j: j for j in range(n_prev)},
        compiler_params=_params("parallel"),
    )(w, g, m, v, *(prev or ()))


def _cast_into_full(w, layer, kind, chip):
    _, R, C = w.shape
    tr = _row_tile(R, 512)

    def body(chip_ref, w_ref, o_ref):
        o_ref[...] = w_ref[...].astype(BF16)

    if kind == "col":
        out_shape = jax.ShapeDtypeStruct((1, R, N_CHIPS * C), BF16)
        out_spec = pl.BlockSpec((None, tr, C), lambda i, chip_ref: (0, i, chip_ref[0]))
    else:
        out_shape = jax.ShapeDtypeStruct((1, N_CHIPS * R, C), BF16)
        out_spec = pl.BlockSpec((None, tr, C), lambda i, chip_ref: (0, chip_ref[0] * (R // tr) + i, 0))
    return pl.pallas_call(
        body, name="cast_into_full_" + kind,
        grid_spec=pltpu.PrefetchScalarGridSpec(
            num_scalar_prefetch=1, grid=(R // tr,),
            in_specs=[pl.BlockSpec((None, tr, C), lambda i, chip_ref: (layer, i, 0))], out_specs=out_spec),
        out_shape=out_shape, compiler_params=_params("parallel"),
    )(chip, w)


def _add_half(g4, la, c):
    L, _, H, W = g4.shape
    th = _row_tile(H, 256)

    def body(c_ref, g_ref, la_ref, o_ref):
        o_ref[...] = (g_ref[...].astype(F32) + la_ref[...].astype(F32)).astype(BF16)

    return pl.pallas_call(
        body, name="add_half",
        grid_spec=pltpu.PrefetchScalarGridSpec(
            num_scalar_prefetch=1, grid=(L, H // th),
            in_specs=[pl.BlockSpec((None, None, th, W), lambda l, i, c_ref: (l, c_ref[0], i, 0)),
                      pl.BlockSpec((None, th, W), lambda l, i, c_ref: (l, i, 0))],
            out_specs=pl.BlockSpec((None, th, W), lambda l, i, c_ref: (l, i, 0))),
        out_shape=jax.ShapeDtypeStruct((L, H, W), BF16),
        compiler_params=_params("parallel", "parallel"),
    )(c, g4, la)


def _add_parts(p, lb, place, kind):
    _, L, H, C = lb.shape
    th = _row_tile(H, 256)

    def body(s_ref, p_ref, lb_ref, o_ref):
        acc = p_ref[...].astype(F32)
        for k in range(N_CHIPS - 1):
            acc = acc + lb_ref[k].astype(F32)
        o_ref[...] = acc

    if kind == "col":
        p_spec = pl.BlockSpec((None, th, C), lambda l, i, s_ref: (l, i, s_ref[0]))
    else:
        p_spec = pl.BlockSpec((None, None, th, C), lambda l, i, s_ref: (l, s_ref[0], i, 0))
    return pl.pallas_call(
        body, name="add_parts_" + kind,
        grid_spec=pltpu.PrefetchScalarGridSpec(
            num_scalar_prefetch=1, grid=(L, H // th),
            in_specs=[p_spec, pl.BlockSpec((N_CHIPS - 1, None, th, C), lambda l, i, s_ref: (0, l, i, 0))],
            out_specs=pl.BlockSpec((None, None, th, C), lambda l, i, s_ref: (l, s_ref[1], i, 0))),
        out_shape=jax.ShapeDtypeStruct((L, 2, H, C), F32),
        compiler_params=_params("parallel", "parallel"),
    )(place, p, lb)


ANY = pl.BlockSpec(memory_space=pl.ANY)


def _place():
    x, y, c = lax.axis_index("x"), lax.axis_index("y"), lax.axis_index("c")
    chips = [(1 - x, y), (x, 1 - y), (1 - x, 1 - y)]
    return x, y, c, chips


def _comm_call(body, name, ins, out_shape, n_remote, n_local, aliases=None):
    scratch = [pltpu.SemaphoreType.DMA((n_remote,)), pltpu.SemaphoreType.DMA((n_remote,))]
    if n_local:
        scratch.append(pltpu.SemaphoreType.DMA((n_local,)))
    return pl.pallas_call(
        body, name=name, in_specs=[ANY] * len(ins), out_specs=[ANY] * len(out_shape), out_shape=out_shape,
        scratch_shapes=scratch, input_output_aliases=aliases or {},
        compiler_params=pltpu.CompilerParams(has_side_effects=True),
    )(*ins)


def _remote(src, dst, send, recv, k, to):
    return pltpu.make_async_remote_copy(src_ref=src, dst_ref=dst, send_sem=send.at[k], recv_sem=recv.at[k],
                                        device_id=to, device_id_type=MESH)


def _gather_weights(fulls, kinds):
    n = len(fulls)
    out_shape = [jax.ShapeDtypeStruct(f.shape, f.dtype) for f in fulls]

    def body(*refs):
        outs, (send, recv) = refs[n:2 * n], refs[2 * n:]
        first = _gather_copies(outs, kinds, send, recv, 0)
        for cp in first:
            cp.start()
        for cp in first:
            cp.wait()
        passed = _pass_on_copies(outs, kinds, send, recv, len(first))
        for cp in passed:
            cp.start()
        for cp in passed:
            cp.wait()

    return _comm_call(body, "gather_weights", fulls, out_shape, 2 * n * (N_CHIPS - 1), 0, {a: a for a in range(n)})


def _window(ref, kind, s, h):
    if kind == "col":
        H, C = ref.shape[1] // 2, ref.shape[2] // N_CHIPS
        return ref.at[:, pl.ds(pl.multiple_of(h * H, 16), H), pl.ds(pl.multiple_of(s * C, LANES), C)]
    R = ref.shape[1] // N_CHIPS
    return ref.at[:, pl.ds(pl.multiple_of(s * R + h * (R // 2), 16), R // 2), :]


def _gather_copies(outs, kinds, send, recv, sem0):
    x, y, c, chips = _place()
    me = 2 * x + y
    return [_remote(_window(o, kind, me, c), _window(o, kind, me, c), send, recv, sem0 + a * (N_CHIPS - 1) + k, (*chip, c))
            for a, (o, kind) in enumerate(zip(outs, kinds)) for k, chip in enumerate(chips)]


def _pass_on_copies(outs, kinds, send, recv, sem0):
    x, y, c, chips = _place()
    cps = []
    for a, (o, kind) in enumerate(zip(outs, kinds)):
        for k, chip in enumerate(chips):
            landed = _window(o, kind, 2 * chip[0] + chip[1], c)
            cps.append(_remote(landed, landed, send, recv, sem0 + a * (N_CHIPS - 1) + k, (x, y, 1 - c)))
    return cps


def _pass_on(fulls, kinds):
    n = len(fulls)
    out_shape = [jax.ShapeDtypeStruct(f.shape, f.dtype) for f in fulls]

    def body(*refs):
        outs, (send, recv) = refs[n:2 * n], refs[2 * n:]
        cps = _pass_on_copies(outs, kinds, send, recv, 0)
        for cp in cps:
            cp.start()
        for cp in cps:
            cp.wait()

    return _comm_call(body, "pass_on", fulls, out_shape, n * (N_CHIPS - 1), 0, {a: a for a in range(n)})


def _gather_small(shards):
    n = len(shards)
    out_shape = [jax.ShapeDtypeStruct((N_CHIPS,) + s.shape, s.dtype) for s in shards]

    def body(*refs):
        srcs, outs, (send, recv, loc) = refs[:n], refs[n:2 * n], refs[2 * n:]
        x, y, c, chips = _place()
        me = 2 * x + y
        remote, local = [], []
        for a in range(n):
            local.append(pltpu.make_async_copy(srcs[a], outs[a].at[me], loc.at[a]))
            for k, chip in enumerate(chips):
                remote.append(_remote(srcs[a], outs[a].at[me], send, recv, a * (N_CHIPS - 1) + k, (*chip, c)))
        for cp in local + remote:
            cp.start()
        for cp in remote + local:
            cp.wait()

    return _comm_call(body, "gather_small", shards, out_shape, n * (N_CHIPS - 1), n)


def _exchange_halves(g4s):
    n = len(g4s)
    out_shape = [jax.ShapeDtypeStruct((g.shape[0],) + g.shape[2:], g.dtype) for g in g4s]

    def body(*refs):
        gs, las, (send, recv) = refs[:n], refs[n:2 * n], refs[2 * n:]
        x, y, c, _ = _place()
        cps = [_remote(gs[a].at[:, 1 - c], las[a], send, recv, a, (x, y, 1 - c)) for a in range(n)]
        for cp in cps:
            cp.start()
        for cp in cps:
            cp.wait()

    return _comm_call(body, "exchange_halves", g4s, out_shape, n, 0)


def _scatter_partials(ps, kinds):
    n = len(ps)

    def body(*refs):
        srcs, lbs, (send, recv) = refs[:n], refs[n:2 * n], refs[2 * n:]
        cps = _scatter_copies(srcs, lbs, kinds, send, recv)
        for cp in cps:
            cp.start()
        for cp in cps:
            cp.wait()

    return _comm_call(body, "scatter_partials", ps, _scatter_shapes(ps, kinds), n * (N_CHIPS - 1), 0)


def _scatter_shapes(ps, kinds):
    out_shape = []
    for p, kind in zip(ps, kinds):
        L, H, C = (p.shape[0], p.shape[1], p.shape[2] // N_CHIPS) if kind == "col" else (p.shape[0], p.shape[2], p.shape[3])
        out_shape.append(jax.ShapeDtypeStruct((N_CHIPS - 1, L, H, C), p.dtype))
    return out_shape


def _scatter_copies(srcs, lbs, kinds, send, recv):
    x, y, c, chips = _place()
    cps = []
    for a, (src, lb, kind) in enumerate(zip(srcs, lbs, kinds)):
        C = lb.shape[3]
        for k, chip in enumerate(chips):
            s = 2 * chip[0] + chip[1]
            part = src.at[:, :, pl.ds(pl.multiple_of(s * C, LANES), C)] if kind == "col" else src.at[:, s]
            cps.append(_remote(part, lb.at[k], send, recv, a * (N_CHIPS - 1) + k, (*chip, c)))
    return cps


def _share_halves(g4s):
    n = len(g4s)
    out_shape = [jax.ShapeDtypeStruct(g.shape, g.dtype) for g in g4s]

    def body(*refs):
        outs, (send, recv) = refs[n:2 * n], refs[2 * n:]
        x, y, c, _ = _place()
        cps = [_remote(outs[a].at[:, c], outs[a].at[:, c], send, recv, a, (x, y, 1 - c)) for a in range(n)]
        for cp in cps:
            cp.start()
        for cp in cps:
            cp.wait()

    return _comm_call(body, "share_halves", g4s, out_shape, n, 0, {a: a for a in range(n)})


def _allreduce_small(part):
    N = part.shape[0]

    def body(p_ref, o_ref, buf, send, recv):
        x, y, c, _ = _place()
        me = 4 * x + 2 * y + c
        buf[me] = p_ref[...]
        cps = []
        for k in range(1, N_DEV):
            peer = (1 - x if k & 4 else x, 1 - y if k & 2 else y, 1 - c if k & 1 else c)
            cps.append(_remote(p_ref, buf.at[me], send, recv, k - 1, peer))
        for cp in cps:
            cp.start()
        for cp in cps:
            cp.wait()
        acc = buf[0]
        for i in range(1, N_DEV):
            acc = acc + buf[i]
        o_ref[...] = acc

    vmem = pl.BlockSpec(memory_space=pltpu.VMEM)
    return pl.pallas_call(
        body, name="allreduce_small", in_specs=[vmem], out_specs=vmem,
        out_shape=jax.ShapeDtypeStruct((N, LANES), F32),
        scratch_shapes=[pltpu.VMEM((N_DEV, N, LANES), F32), pltpu.SemaphoreType.DMA((N_DEV - 1,)),
                        pltpu.SemaphoreType.DMA((N_DEV - 1,))],
        compiler_params=pltpu.CompilerParams(has_side_effects=True, vmem_limit_bytes=VMEM_LIMIT_BYTES),
    )(part)


AFTER_ATTENTION = ("w_out", "w_up", "w_down")


def _layer_fwd(x, p, full=None, l=0):
    comm = full is not None

    def guest(first, second):
        first, second = ([k for k in ks if comm and k[0] < len(full)] for ks in (first, second))
        if not first + second:
            return None, []
        return _gather_guest([full[i][n] for i, n in first], [BIG_KIND[n] for _, n in first],
                             [full[i][n] for i, n in second], [BIG_KIND[n] for _, n in second]), first + second

    def done(keys, bufs):
        for (i, n), buf in zip(keys, bufs):
            full[i][n] = buf

    def weight(n):
        return full[l][n][0] if comm else p[n]

    in_sweep = ([(l, "w_up")] if l else [(l, n) for n in AFTER_ATTENTION]) if comm else []
    g, keys = guest([], [(l, "w_out")] if l else [])
    h1t, proj, out = _norm_matmul(x, p["norm1_g"], weight("w_in"), g)
    done(keys, out)
    attn, ltot, attnt, out = _attn_fwd(proj, p["q_norm_g"], p["k_norm_g"],
                                       [full[i][n] for i, n in in_sweep], [BIG_KIND[n] for _, n in in_sweep])
    done(in_sweep, out)
    g, keys = guest([], in_sweep)
    c1, out = _glu_conv_fwd(proj, p["conv_dw_w"], p["conv_dw_b"], g)
    done(keys, out)
    c, ct = _ln_silu_fwd(c1, p["conv_ln_g"], p["conv_ln_b"])
    x_mid, _ = _matmul_res([attn, c], weight("w_out"), x)
    g, keys = guest([(l + 1, "w_down")], [])
    h2t, u, out = _norm_matmul(x_mid, p["norm2_g"], weight("w_up"), g)
    done(keys, out)
    g, keys = guest([(l + 1, "w_in")], [(l + 1, "w_down")])
    act, actt, out = _ffn_act_fwd(u, p["ffn_dw_w"], p["ffn_dw_b"], g)
    done(keys, out)
    g, keys = guest([(l + 1, "w_out")], [(l + 1, "w_in")])
    x_out, out = _matmul_res([act], weight("w_down"), x_mid, g)
    done(keys, out)
    saved = dict(x=x, h1t=h1t, proj=proj, attnt=attnt, ltot=ltot, c1=c1, ct=ct, x_mid=x_mid, h2t=h2t, u=u, actt=actt)
    return x_out, saved


def _chip_partials(gs, kinds, core):
    g4s = _halves_view(gs, kinds)
    return _add_halves(g4s, _exchange_halves(g4s), kinds, core)


def _halves_view(gs, kinds):
    return [g.reshape(1, 2, g.shape[0] // 2, g.shape[1]) if kind == "col"
            else g.reshape(N_CHIPS, 2, g.shape[0] // N_CHIPS // 2, g.shape[1]) for g, kind in zip(gs, kinds)]


def _add_halves(g4s, received, kinds, core):
    parts = [_add_half(g4, la, core) for g4, la in zip(g4s, received)]
    return [p if kind == "col" else p.reshape(1, N_CHIPS, p.shape[1], p.shape[2]) for p, kind in zip(parts, kinds)]


def _gather_guest(first, first_kinds, second, second_kinds):
    bufs = list(first) + list(second)
    n1 = len(first)

    def copies(ins, outs, send, recv):
        return (_gather_copies(outs[:n1], first_kinds, send, recv, 0)
                + _pass_on_copies(outs[n1:], second_kinds, send, recv, n1 * (N_CHIPS - 1)))

    return _Guest("gather", bufs, [jax.ShapeDtypeStruct(f.shape, f.dtype) for f in bufs],
                  {a: a for a in range(len(bufs))}, copies, len(bufs) * (N_CHIPS - 1))


def _scatter_guest(parts, kinds):
    return _Guest("scatter", list(parts), _scatter_shapes(parts, kinds), {},
                  lambda ins, outs, send, recv: _scatter_copies(ins, outs, kinds, send, recv), len(parts) * (N_CHIPS - 1))


def _exchange_guest(g4s):
    def copies(ins, outs, send, recv):
        x, y, c, _ = _place()
        return [_remote(g.at[:, 1 - c], la, send, recv, a, (x, y, 1 - c)) for a, (g, la) in enumerate(zip(ins, outs))]

    out_shape = [jax.ShapeDtypeStruct((g.shape[0],) + g.shape[2:], g.dtype) for g in g4s]
    return _Guest("exchange", list(g4s), out_shape, {}, copies, len(g4s))


def _share_guest(g4s):
    def copies(ins, outs, send, recv):
        x, y, c, _ = _place()
        return [_remote(o.at[:, c], o.at[:, c], send, recv, a, (x, y, 1 - c)) for a, o in enumerate(outs)]

    n = len(g4s)
    return _Guest("share", list(g4s), [jax.ShapeDtypeStruct(g.shape, g.dtype) for g in g4s], {a: a for a in range(n)}, copies, n)


def _owned_sums(parts, landed, kinds, place):
    halves = [_add_parts(p, lb, place, kind) for p, lb, kind in zip(parts, landed, kinds)]
    return [g.reshape(2 * g.shape[2], g.shape[3]) for g in _share_halves(halves)]


def _layer_bwd(dx_out, s, p, comm=None):
    g = {}
    core, place, pending = comm if comm is not None else (None, None, [])
    exchange = scatter = share = None
    if pending:
        g4s_in = _halves_view(pending, ["col"])
        exchange = _exchange_guest(g4s_in)
    dact, received = _matmul_nt([dx_out], p["w_down"], BF16, exchange)
    if pending:
        parts_in = _add_halves(g4s_in, received, ["col"], core)
        scatter = _scatter_guest(parts_in, ["col"])
    g["w_down"] = _matmul_tn([s["actt"]], [dx_out], tk=D_FF // 2, tn=D_MODEL)
    dgate, dval, g["ffn_dw_w"], g["ffn_dw_b"], landed = _ffn_act_bwd(s["u"], p["ffn_dw_w"], p["ffn_dw_b"], dact, scatter)
    if pending:
        share = _share_guest([_add_parts(parts_in[0], landed[0], place, "col")])
    dx_mid, g["norm2_g"], shared_in = _matmul_nt_rmsbwd([dgate, dval], p["w_up"], s["x_mid"], p["norm2_g"], dx_out, share)
    g["w_up"] = _matmul_tn([s["h2t"]], [dgate, dval], tk=D_MODEL, tn=D_FF // 2)
    dmix, _ = _matmul_nt([dx_mid], p["w_out"], F32)
    g["w_out"] = _matmul_tn([s["attnt"], s["ct"]], [dx_mid], tk=ATTN_WIDTH, tn=D_MODEL)
    dc1, g["conv_ln_g"], g["conv_ln_b"] = _ln_silu_bwd(s["c1"], p["conv_ln_g"], p["conv_ln_b"], dmix)
    parts, kinds, exchange, share = [], [], None, None
    if comm is not None:
        kinds = [BIG_KIND[n] for n in AFTER_ATTENTION]
        g4s = _halves_view([g[n] for n in AFTER_ATTENTION], kinds)
        exchange = _exchange_guest(g4s)
    da, db, g["conv_dw_w"], g["conv_dw_b"], received = _glu_conv_bwd(s["proj"], p["conv_dw_w"], dc1, exchange)
    if comm is not None:
        parts = _add_halves(g4s, received, kinds, core)
    (dq, dk, dv, g["q_norm_g"], g["k_norm_g"]), landed = _attn_bwd(
        s["proj"], p["q_norm_g"], p["k_norm_g"], s["ltot"], dmix, parts, kinds)
    if comm is not None:
        share = _share_guest([_add_parts(pt, lb, place, kind) for pt, lb, kind in zip(parts, landed, kinds)])
    pieces = [dq, dk, dv, da, db]
    dx, g["norm1_g"], shared = _matmul_nt_rmsbwd(pieces, p["w_in"], s["x"], p["norm1_g"], dx_mid, share)
    g["w_in"] = _matmul_tn([s["h1t"]], pieces, tk=D_MODEL, tn=ATTN_WIDTH)
    return dx, g, [a.reshape(2 * a.shape[2], a.shape[3]) for a in list(shared) + list(shared_in)]


WEIGHTS = ("norm1_g", "w_in", "q_norm_g", "k_norm_g", "conv_dw_w", "conv_dw_b", "conv_ln_g", "conv_ln_b",
           "w_out", "norm2_g", "w_up", "ffn_dw_w", "ffn_dw_b", "w_down")
BIG = ("w_in", "w_out", "w_up", "w_down")
BIG_KIND = {"w_in": "col", "w_out": "row", "w_up": "col", "w_down": "row"}
SMALL_SHARDED = ("conv_dw_w", "ffn_dw_w")
REPLICATED = tuple(n for n in WEIGHTS if n not in BIG + SMALL_SHARDED)


def _pack(arrays):
    flat = jnp.concatenate([a.reshape(-1) for a in arrays])
    rows = -(-flat.shape[0] // (SUBLANES * LANES)) * SUBLANES
    return jnp.pad(flat, (0, rows * LANES - flat.shape[0])).reshape(rows, LANES)


def _unpack(packed, shapes):
    flat = packed.reshape(-1)
    out, off = [], 0
    for shape in shapes:
        size = 1
        for d in shape:
            size *= d
        out.append(flat[off:off + size].reshape(shape))
        off += size
    return out


def _unshard_last(stacked):
    n, L, K, C = stacked.shape
    return jnp.transpose(stacked, (1, 2, 0, 3)).reshape(L, K, n * C)


def kernel(x, norm1_g, w_in, q_norm_g, k_norm_g, conv_dw_w, conv_dw_b, conv_ln_g, conv_ln_b, w_out, norm2_g, w_up, ffn_dw_w, ffn_dw_b, w_down, loss_target, m_norm1_g, m_w_in, m_q_norm_g, m_k_norm_g, m_conv_dw_w, m_conv_dw_b, m_conv_ln_g, m_conv_ln_b, m_w_out, m_norm2_g, m_w_up, m_ffn_dw_w, m_ffn_dw_b, m_w_down, v_norm1_g, v_w_in, v_q_norm_g, v_k_norm_g, v_conv_dw_w, v_conv_dw_b, v_conv_ln_g, v_conv_ln_b, v_w_out, v_norm2_g, v_w_up, v_ffn_dw_w, v_ffn_dw_b, v_w_down):
    given = dict(locals())
    w = {n: given[n] for n in WEIGHTS}
    m = {n: given["m_" + n] for n in WEIGHTS}
    v = {n: given["v_" + n] for n in WEIGHTS}
    chip = 2 * lax.axis_index("x") + lax.axis_index("y")
    core = lax.axis_index("c")
    chip_arr = jnp.reshape(chip, (1,)).astype(jnp.int32)
    core_arr = jnp.reshape(core, (1,)).astype(jnp.int32)
    L = DEPTH

    place = jnp.concatenate([chip_arr, core_arr])

    full = [{n: _cast_into_full(w[n], l, BIG_KIND[n], chip_arr) for n in BIG} for l in range(L)]
    full[0]["w_in"] = _gather_weights([full[0]["w_in"]], ["col"])[0]
    small_full = {n: _unshard_last(stacked)
                  for n, stacked in zip(SMALL_SHARDED, _gather_small([w[n] for n in SMALL_SHARDED]))}
    params = []
    for l in range(L):
        p = {n: small_full[n][l] for n in SMALL_SHARDED}
        p.update({n: w[n][l][None] for n in REPLICATED})
        params.append(p)

    act = x[0]
    saved = []
    for l in range(L):
        act, s = _layer_fwd(act, params[l], full, l)
        saved.append(s)
    for l in range(L):
        params[l].update({n: full[l][n][0] for n in BIG})
    dx, loss_part = _loss_grad(act, loss_target[0])
    loss = lax.psum(loss_part, ("x", "y", "c"))

    grads = [None] * L
    summed = {}
    pending = []
    for l in reversed(range(L)):
        dx, grads[l], sums = _layer_bwd(dx, saved[l], params[l], (core_arr, place, pending))
        names = [(l, n) for n in AFTER_ATTENTION] + ([(l + 1, "w_in")] if pending else [])
        summed.update(zip(names, sums))
        pending = [grads[l]["w_in"]]
    parts = _chip_partials(pending, ["col"], core_arr)
    summed[0, "w_in"] = _owned_sums(parts, _scatter_partials(parts, ["col"]), ["col"], place)[0]

    grad, delta, new_m, new_v = {}, {}, {}, {}
    for n in BIG:
        out = None
        for l in range(L):
            out = _adamw_layer(l, w[n], summed[l, n], m[n], v[n], out)
        grad[n], delta[n], new_m[n], new_v[n] = out

    small = REPLICATED + SMALL_SHARDED
    small_grads = [jnp.stack([grads[l][n] for l in range(L)]) for n in small]
    small_sums = _unpack(_allreduce_small(_pack(small_grads)), [a.shape for a in small_grads])
    for n, g in zip(small, small_sums):
        if n in REPLICATED:
            grad[n] = g.reshape(w[n].shape)
        else:
            width = w[n].shape[2]
            grad[n] = lax.dynamic_slice_in_dim(g, chip * width, width, axis=2)
    shapes = [w[n].shape for n in small]
    packed = _adamw(*[_pack([src[n] for n in small]) for src in (w, grad, m, v)])
    for out, pk in zip((delta, new_m, new_v), packed):
        out.update(zip(small, _unpack(pk, shapes)))

    return (loss, dx[None], *[grad[n] for n in WEIGHTS], *[delta[n] for n in WEIGHTS],
            *[new_m[n] for n in WEIGHTS], *[new_v[n] for n in WEIGHTS])
```

```python
import functools

import jax
import jax.numpy as jnp
from jax import lax
from jax.experimental import pallas as pl
from jax.experimental.pallas import tpu as pltpu

F32 = jnp.float32
BF16 = jnp.bfloat16

DEPTH = 4
D_MODEL = 1024
HEADS = 8
HEAD_DIM = 64
ATTN_WIDTH = HEADS * HEAD_DIM
CONV_WIDTH = D_MODEL - ATTN_WIDTH
CONV_KERNEL = 31
D_FF = 2816
FFN_KERNEL = 3
EPS = 1e-6
ADAM_LR, ADAM_B1, ADAM_B2, ADAM_EPS, ADAM_WD, ADAM_STEP = 0.001, 0.9, 0.999, 1e-08, 0.01, 10

N_CHIPS = 4
N_DEV = 8
LANES = 128
SUBLANES = 8
VMEM_LIMIT_BYTES = 56 * 2**20
ATTN_TILE = 256
ATTN_HEADS_PER_STEP = 4
ATTN_BLOCK = ATTN_HEADS_PER_STEP * HEAD_DIM
CONV_PAD = 32
FFN_PAD = 8
MESH = pl.DeviceIdType.MESH


def _params(*sem):
    return pltpu.CompilerParams(dimension_semantics=sem if sem else None, vmem_limit_bytes=VMEM_LIMIT_BYTES)


class _Guest:
    def __init__(self, name, ins, out_shape, aliases, copies, n_sem):
        self.name, self.ins, self.out_shape, self.aliases, self.copies, self.n_sem = name, ins, out_shape, aliases, copies, n_sem


def _hosted_call(body, guest, *, name, grid, in_specs, out_specs, out_shape, scratch_shapes, operands):
    if guest is None:
        out = pl.pallas_call(body, name=name, grid=grid, in_specs=in_specs, out_specs=out_specs, out_shape=out_shape,
                             scratch_shapes=scratch_shapes, compiler_params=_params("arbitrary"))(*operands)
        return list(out), []
    n_in, n_out, n_scr = len(in_specs), len(out_specs), len(scratch_shapes)
    gi, go = len(guest.ins), len(guest.out_shape)

    def hosting(*refs):
        ins, g_in = refs[:n_in], refs[n_in:n_in + gi]
        outs, g_out = refs[n_in + gi:n_in + gi + n_out], refs[n_in + gi + n_out:n_in + gi + n_out + go]
        scratch, (send, recv) = refs[n_in + gi + n_out + go:-2], refs[-2:]

        @pl.when(pl.program_id(0) == 0)
        def _():
            for cp in guest.copies(g_in, g_out, send, recv):
                cp.start()

        body(*ins, *outs, *scratch)

        @pl.when(pl.program_id(0) == grid[0] - 1)
        def _():
            for cp in guest.copies(g_in, g_out, send, recv):
                cp.wait()

    out = pl.pallas_call(
        hosting, name=name + "_" + guest.name, grid=grid,
        in_specs=list(in_specs) + [ANY] * gi, out_specs=list(out_specs) + [ANY] * go,
        out_shape=list(out_shape) + list(guest.out_shape),
        scratch_shapes=list(scratch_shapes) + [pltpu.SemaphoreType.DMA((guest.n_sem,))] * 2,
        input_output_aliases={n_in + a: n_out + b for a, b in guest.aliases.items()},
        compiler_params=pltpu.CompilerParams(dimension_semantics=("arbitrary",), vmem_limit_bytes=VMEM_LIMIT_BYTES,
                                             has_side_effects=True),
    )(*operands, *guest.ins)
    return list(out[:n_out]), list(out[n_out:])


def _dot(a, b):
    return jnp.dot(a, b, preferred_element_type=F32)


def _dot_nt(a, b):
    return lax.dot_general(a, b, (((1,), (1,)), ((), ())), preferred_element_type=F32)


def _dot_tn(a, b):
    return lax.dot_general(a, b, (((0,), (0,)), ((), ())), preferred_element_type=F32)


def _sigmoid(x):
    return 1.0 / (1.0 + jnp.exp(-x))


def _norm_matmul(x, g, w, guest=None, *, out_dtype=F32, tm=256):
    S, D = x.shape
    N = w.shape[1]

    def body(x_ref, g_ref, w_ref, ht_ref, y_ref):
        xv = x_ref[...]
        h = xv * lax.rsqrt(jnp.mean(xv * xv, axis=-1, keepdims=True) + EPS) * g_ref[...]
        ht_ref[...] = h.T.astype(BF16)
        y_ref[...] = _dot(h.astype(BF16), w_ref[...]).astype(out_dtype)

    (ht, y), guest_out = _hosted_call(
        body, guest, name="norm_matmul", grid=(S // tm,),
        in_specs=[pl.BlockSpec((tm, D), lambda i: (i, 0)),
                  pl.BlockSpec((1, D), lambda i: (0, 0)),
                  pl.BlockSpec((D, N), lambda i: (0, 0))],
        out_specs=[pl.BlockSpec((D, tm), lambda i: (0, i)),
                   pl.BlockSpec((tm, N), lambda i: (i, 0))],
        out_shape=[jax.ShapeDtypeStruct((D, S), BF16), jax.ShapeDtypeStruct((S, N), out_dtype)],
        scratch_shapes=[], operands=[x, g, w])
    return ht, y, guest_out


def _matmul_res(pieces, w, res, guest=None, *, tm=512):
    S, N = res.shape
    K = w.shape[0]
    widths = [p.shape[1] for p in pieces]
    assert sum(widths) == K

    def body(*refs):
        p_refs, (w_ref, res_ref, o_ref) = refs[:len(pieces)], refs[len(pieces):]
        acc = res_ref[...]
        off = 0
        for p_ref, kp in zip(p_refs, widths):
            acc = acc + _dot(p_ref[...], w_ref[off:off + kp, :])
            off += kp
        o_ref[...] = acc

    (out,), guest_out = _hosted_call(
        body, guest, name="matmul_res", grid=(S // tm,),
        in_specs=[pl.BlockSpec((tm, kp), lambda i: (i, 0)) for kp in widths]
        + [pl.BlockSpec((K, N), lambda i: (0, 0)), pl.BlockSpec((tm, N), lambda i: (i, 0))],
        out_specs=[pl.BlockSpec((tm, N), lambda i: (i, 0))],
        out_shape=[jax.ShapeDtypeStruct((S, N), F32)],
        scratch_shapes=[], operands=[*pieces, w, res])
    return out, guest_out


def _nt_sum(p_refs, widths, w_ref):
    acc = None
    off = 0
    for p_ref, n_p in zip(p_refs, widths):
        d = _dot_nt(p_ref[...].astype(BF16), w_ref[:, off:off + n_p])
        acc = d if acc is None else acc + d
        off += n_p
    return acc


def _matmul_nt(pieces, w, out_dtype, guest=None, *, tm=512):
    S = pieces[0].shape[0]
    K, N = w.shape
    widths = [p.shape[1] for p in pieces]
    assert sum(widths) == N

    def body(*refs):
        p_refs, (w_ref, o_ref) = refs[:len(pieces)], refs[len(pieces):]
        o_ref[...] = _nt_sum(p_refs, widths, w_ref).astype(out_dtype)

    (out,), guest_out = _hosted_call(
        body, guest, name="matmul_nt", grid=(S // tm,),
        in_specs=[pl.BlockSpec((tm, n_p), lambda i: (i, 0)) for n_p in widths]
        + [pl.BlockSpec((K, N), lambda i: (0, 0))],
        out_specs=[pl.BlockSpec((tm, K), lambda i: (i, 0))],
        out_shape=[jax.ShapeDtypeStruct((S, K), out_dtype)],
        scratch_shapes=[], operands=[*pieces, w])
    return out, guest_out


def _matmul_nt_rmsbwd(pieces, w, x, g, dres, guest=None, *, tm=256):
    S, K = x.shape
    N = w.shape[1]
    widths = [p.shape[1] for p in pieces]
    assert sum(widths) == N

    def body(*refs):
        p_refs, (w_ref, x_ref, g_ref, dres_ref, dx_ref, dg_ref) = refs[:len(pieces)], refs[len(pieces):]
        dh = _nt_sum(p_refs, widths, w_ref)
        xv = x_ref[...]
        r = lax.rsqrt(jnp.mean(xv * xv, axis=-1, keepdims=True) + EPS)
        xh = xv * r
        dxh = dh * g_ref[...]
        dx_ref[...] = dres_ref[...] + r * (dxh - xh * jnp.mean(dxh * xh, axis=-1, keepdims=True))

        @pl.when(pl.program_id(0) == 0)
        def _():
            dg_ref[...] = jnp.zeros_like(dg_ref)

        dg_ref[...] += jnp.sum(dh * xh, axis=0, keepdims=True)

    (dx, dg), guest_out = _hosted_call(
        body, guest, name="matmul_nt_rmsbwd", grid=(S // tm,),
        in_specs=[pl.BlockSpec((tm, n_p), lambda i: (i, 0)) for n_p in widths]
        + [pl.BlockSpec((K, N), lambda i: (0, 0)), pl.BlockSpec((tm, K), lambda i: (i, 0)),
           pl.BlockSpec((1, K), lambda i: (0, 0)), pl.BlockSpec((tm, K), lambda i: (i, 0))],
        out_specs=[pl.BlockSpec((tm, K), lambda i: (i, 0)), pl.BlockSpec((1, K), lambda i: (0, 0))],
        out_shape=[jax.ShapeDtypeStruct((S, K), F32), jax.ShapeDtypeStruct((1, K), F32)],
        scratch_shapes=[], operands=[*pieces, w, x, g, dres])
    return dx, dg, guest_out


def _matmul_tn(xts, dys, *, tk, tn, ts=1024):
    S = dys[0].shape[0]
    xs = xts
    n_s = S // ts
    (mt,) = {x.shape[0] // tk for x in xts}
    (nt,) = {d.shape[1] // tn for d in dys}

    def body(*refs):
        x_refs, dy_refs, (o_ref, acc_ref) = refs[:len(xs)], refs[len(xs):len(xs) + len(dys)], refs[len(xs) + len(dys):]
        i, j, s = pl.program_id(0), pl.program_id(1), pl.program_id(2)

        @pl.when(s == 0)
        def _():
            acc_ref[...] = jnp.zeros_like(acc_ref)

        for a, x_ref in enumerate(x_refs):
            for b, dy_ref in enumerate(dy_refs):
                @pl.when((i // mt == a) & (j // nt == b))
                def _():
                    acc_ref[...] += _dot(x_ref[...], dy_ref[...].astype(BF16))

        @pl.when(s == n_s - 1)
        def _():
            o_ref[...] = acc_ref[...].astype(BF16)

    def x_map(a):
        return lambda i, j, s: (jnp.where(i // mt == a, i % mt, 0), jnp.where(i // mt == a, s, 0))

    def dy_map(b):
        return lambda i, j, s: (jnp.where(j // nt == b, s, 0), jnp.where(j // nt == b, j % nt, 0))

    return pl.pallas_call(
        body, name="matmul_tn", grid=(len(xs) * mt, len(dys) * nt, n_s),
        in_specs=[pl.BlockSpec((tk, ts), x_map(a)) for a in range(len(xs))]
        + [pl.BlockSpec((ts, tn), dy_map(b)) for b in range(len(dys))],
        out_specs=pl.BlockSpec((tk, tn), lambda i, j, s: (i, j)),
        out_shape=jax.ShapeDtypeStruct((len(xs) * mt * tk, len(dys) * nt * tn), BF16),
        scratch_shapes=[pltpu.VMEM((tk, tn), F32)],
        compiler_params=_params("parallel", "parallel", "arbitrary"),
    )(*xs, *dys)


def _tri_consts():
    j = jnp.arange(ATTN_TILE)[:, None]
    s = jnp.arange(ATTN_TILE)[None, :]
    return (j > s).astype(BF16), (j <= s).astype(BF16), (j < s).astype(BF16)


SIGN_BIT = 0x80000000
WEIGHT_IS_ZERO = -104.0


def _log_terms(sn):
    minus_abs = lax.bitcast_convert_type(lax.bitcast_convert_type(sn, jnp.uint32) | jnp.uint32(SIGN_BIT), F32)
    lom = jnp.minimum(sn, 0.0) - jnp.log(1.0 + jnp.exp(minus_abs))
    return lom, lom - sn


def _causal_mask():
    t = lax.broadcasted_iota(jnp.int32, (ATTN_TILE, ATTN_TILE), 0)
    s = lax.broadcasted_iota(jnp.int32, (ATTN_TILE, ATTN_TILE), 1)
    return s < t


def _attn_prep(h, q_ref, k_ref, v_ref, qg_ref, kg_ref, qn_s, kn_s, vb_s, n_tiles):
    T = ATTN_TILE
    lanes = slice(HEAD_DIM * h, HEAD_DIM * (h + 1))
    scale = -(HEAD_DIM ** -0.5)

    def prep(i, carry):
        rows = pl.ds(pl.multiple_of(i * T, T), T)
        q = q_ref[rows, lanes]
        k = k_ref[rows, lanes]
        rq = lax.rsqrt(jnp.mean(q * q, axis=-1, keepdims=True) + EPS)
        rk = lax.rsqrt(jnp.mean(k * k, axis=-1, keepdims=True) + EPS)
        qn_s[h, rows, :] = (q * rq * qg_ref[...] * scale).astype(BF16)
        kn_s[h, rows, :] = (k * rk * kg_ref[...]).astype(BF16)
        vb_s[h, rows, :] = v_ref[rows, lanes].astype(BF16)
        return carry

    lax.fori_loop(0, n_tiles, prep, 0)


def _attn_fwd(proj, qg, kg, fulls=(), kinds=()):
    S = proj.shape[0]
    n_comm = len(fulls)
    T = ATTN_TILE
    n_tiles = S // T
    suffix, _, _ = _tri_consts()
    pairs = HEADS // ATTN_HEADS_PER_STEP
    q_blk, k_blk, v_blk = 0, ATTN_WIDTH // ATTN_BLOCK, 2 * ATTN_WIDTH // ATTN_BLOCK

    def body(*refs):
        q_ref, k_ref, v_ref, qg_ref, kg_ref, tri_ref = refs[:6]
        o_ref, lt_ref, ot_ref, start_ref = refs[6 + n_comm:10 + n_comm]
        w_refs = refs[10 + n_comm:10 + 2 * n_comm]
        qn_s, kn_s, vb_s = refs[10 + 2 * n_comm:13 + 2 * n_comm]
        sems = refs[13 + 2 * n_comm:]
        step_id = pl.program_id(0)
        if n_comm:
            @pl.when(pl.program_id(0) == 0)
            def _():
                for cp in _gather_copies(w_refs, kinds, *sems, 0):
                    cp.start()

        heads = range(ATTN_HEADS_PER_STEP)
        for h in heads:
            _attn_prep(h, q_ref, k_ref, v_ref, qg_ref, kg_ref, qn_s, kn_s, vb_s, n_tiles)

        def q_tile(qi, carry0):
            qrows = pl.ds(pl.multiple_of(qi * T, T), T)
            qt = [qn_s[h, qrows, :] for h in heads]

            def tile(kj, carry, diag):
                krows = pl.ds(pl.multiple_of(kj * T, T), T)
                s = [_dot_nt(qt[h], kn_s[h, krows, :]) for h in heads]
                terms = [_log_terms(s[h]) for h in heads]
                lom = [terms[h][0] for h in heads]
                if diag:
                    mask = _causal_mask()
                    lom = [jnp.where(mask, lom[h], 0.0) for h in heads]
                lom = [lom[h].astype(BF16) for h in heads]
                tail = [_dot(lom[h], tri_ref[...]) for h in heads]
                w = [jnp.exp(terms[h][1] + tail[h] + carry[h][1]) for h in heads]
                if diag:
                    w = [jnp.where(mask, w[h], 0.0) for h in heads]
                return tuple((carry[h][0] + _dot(w[h].astype(BF16), vb_s[h, krows, :]),
                              carry[h][1] + (tail[h][:, 0:1] + lom[h][:, 0:1].astype(F32))) for h in heads)

            init = tuple((jnp.zeros((T, HEAD_DIM), F32), jnp.zeros((T, 1), F32)) for h in heads)
            def alive(cr):
                worst = cr[0][1]
                for h in heads[1:]:
                    worst = jnp.maximum(worst, cr[h][1])
                return jnp.max(worst) >= WEIGHT_IS_ZERO

            def step(state):
                t, _, cr = state
                cr = tile(qi - 1 - t, cr, False)
                return t + 1, alive(cr), cr

            first = tile(qi, init, True)
            swept, _, carry = lax.while_loop(lambda st: (st[0] < qi) & st[1], step, (jnp.int32(0), alive(first), first))
            start_ref[step_id, qi] = (qi - swept).astype(F32)
            o = jnp.concatenate([carry[h][0] for h in heads], axis=1)
            o_ref[qrows, :] = o.astype(BF16)
            ot_ref[:, qrows] = o.T.astype(BF16)
            for h in heads:
                lt_ref[qrows, HEAD_DIM * h:HEAD_DIM * (h + 1)] = jnp.broadcast_to(carry[h][1], (T, HEAD_DIM))
            return carry0

        lax.fori_loop(0, n_tiles, q_tile, 0)
        if n_comm:
            @pl.when(pl.program_id(0) == pairs - 1)
            def _():
                for cp in _gather_copies(w_refs, kinds, *sems, 0):
                    cp.wait()

    n_sem = n_comm * (N_CHIPS - 1)
    out = pl.pallas_call(
        body, name="attn_fwd_gather" if n_comm else "attn_fwd", grid=(pairs,),
        in_specs=[pl.BlockSpec((S, ATTN_BLOCK), lambda p: (0, q_blk + p)),
                  pl.BlockSpec((S, ATTN_BLOCK), lambda p: (0, k_blk + p)),
                  pl.BlockSpec((S, ATTN_BLOCK), lambda p: (0, v_blk + p)),
                  pl.BlockSpec((1, HEAD_DIM), lambda p: (0, 0)),
                  pl.BlockSpec((1, HEAD_DIM), lambda p: (0, 0)),
                  pl.BlockSpec((T, T), lambda p: (0, 0))] + [ANY] * n_comm,
        out_specs=[pl.BlockSpec((S, ATTN_BLOCK), lambda p: (0, p)),
                   pl.BlockSpec((None, S, ATTN_BLOCK), lambda p: (p, 0, 0)),
                   pl.BlockSpec((ATTN_BLOCK, S), lambda p: (p, 0)),
                   pl.BlockSpec(memory_space=pltpu.SMEM)] + [ANY] * n_comm,
        out_shape=[jax.ShapeDtypeStruct((S, ATTN_WIDTH), BF16),
                   jax.ShapeDtypeStruct((pairs, S, ATTN_BLOCK), F32),
                   jax.ShapeDtypeStruct((ATTN_WIDTH, S), BF16),
                   jax.ShapeDtypeStruct((pairs, n_tiles), F32)] + [jax.ShapeDtypeStruct(f.shape, f.dtype) for f in fulls],
        scratch_shapes=[pltpu.VMEM((ATTN_HEADS_PER_STEP, S, HEAD_DIM), BF16)] * 3
        + ([pltpu.SemaphoreType.DMA((n_sem,))] * 2 if n_comm else []),
        input_output_aliases={6 + a: 4 + a for a in range(n_comm)},
        compiler_params=pltpu.CompilerParams(dimension_semantics=("arbitrary",), vmem_limit_bytes=VMEM_LIMIT_BYTES,
                                             has_side_effects=bool(n_comm)),
    )(proj, proj, proj, qg, kg, suffix, *fulls)
    return out[0], (out[1], out[3]), out[2], list(out[4:])


def _attn_bwd(proj, qg, kg, ltot, dmix, parts=(), kinds=()):
    S = proj.shape[0]
    n_comm = len(parts)
    T = ATTN_TILE
    n_tiles = S // T
    _, prefix_incl, prefix_excl = _tri_consts()
    pairs = HEADS // ATTN_HEADS_PER_STEP
    q_blk, k_blk, v_blk = 0, ATTN_WIDTH // ATTN_BLOCK, 2 * ATTN_WIDTH // ATTN_BLOCK
    scale = HEAD_DIM ** -0.5

    def body(*refs):
        q_ref, k_ref, v_ref, qg_ref, kg_ref, lt_ref, do_ref, ti_ref, te_ref, start_ref = refs[:10]
        p_refs = refs[10:10 + n_comm]
        dq_ref, dk_ref, dv_ref, dqg_ref, dkg_ref = refs[10 + n_comm:15 + n_comm]
        lb_refs = refs[15 + n_comm:15 + 2 * n_comm]
        qn_s, kn_s, vb_s, dq_s, dk_s, dv_s = refs[15 + 2 * n_comm:21 + 2 * n_comm]
        sems = refs[21 + 2 * n_comm:]
        step_id = pl.program_id(0)

        @pl.when(pl.program_id(0) == 0)
        def _():
            dqg_ref[...] = jnp.zeros_like(dqg_ref)
            dkg_ref[...] = jnp.zeros_like(dkg_ref)
            for cp in _scatter_copies(p_refs, lb_refs, kinds, *sems) if n_comm else ():
                cp.start()

        heads = range(ATTN_HEADS_PER_STEP)
        for h in heads:
            _attn_prep(h, q_ref, k_ref, v_ref, qg_ref, kg_ref, qn_s, kn_s, vb_s, n_tiles)
        dk_s[...] = jnp.zeros_like(dk_s)
        dv_s[...] = jnp.zeros_like(dv_s)

        def q_tile(qi, carry0):
            qrows = pl.ds(pl.multiple_of(qi * T, T), T)
            qt = [qn_s[h, qrows, :] for h in heads]
            dob = [do_ref[qrows, HEAD_DIM * h:HEAD_DIM * (h + 1)].astype(BF16) for h in heads]
            lt = [lt_ref[qrows, HEAD_DIM * h:HEAD_DIM * h + 1] for h in heads]

            def tile(kj, carry, diag):
                krows = pl.ds(pl.multiple_of(kj * T, T), T)
                kt = [kn_s[h, krows, :] for h in heads]
                s = [_dot_nt(qt[h], kt[h]) for h in heads]
                dw = [_dot_nt(dob[h], vb_s[h, krows, :]) for h in heads]
                terms = [_log_terms(s[h]) for h in heads]
                lom = [terms[h][0] for h in heads]
                if diag:
                    mask = _causal_mask()
                    lom = [jnp.where(mask, lom[h], 0.0) for h in heads]
                pin = [_dot(lom[h].astype(BF16), ti_ref[...]) for h in heads]
                w = [jnp.exp(terms[h][1] + ((lt[h] - carry[h][1]) - pin[h])) for h in heads]
                if diag:
                    w = [jnp.where(mask, w[h], 0.0) for h in heads]
                e = [w[h] * dw[h] for h in heads]
                gex = [_dot(e[h].astype(BF16), te_ref[...]) for h in heads]
                dz = [e[h] - jnp.exp(terms[h][1]) * (e[h] + (carry[h][2] + gex[h])) for h in heads]
                if diag:
                    dz = [jnp.where(mask, dz[h], 0.0) for h in heads]
                new = []
                for h in heads:
                    dzb = dz[h].astype(BF16)
                    dk_s[h, krows, :] += _dot_tn(dzb, qt[h])
                    dv_s[h, krows, :] += _dot_tn(w[h].astype(BF16), dob[h])
                    new.append((carry[h][0] + _dot(dzb, kt[h]),
                                carry[h][1] + pin[h][:, T - 1:T],
                                carry[h][2] + gex[h][:, T - 1:T] + e[h][:, T - 1:T]))
                return tuple(new)

            zero = jnp.zeros((T, 1), F32)
            init = tuple((jnp.zeros((T, HEAD_DIM), F32), zero, zero) for h in heads)
            first = jnp.clip(start_ref[step_id, qi].astype(jnp.int32), 0, qi)
            last = tile(qi, lax.fori_loop(first, qi, lambda kj, cr: tile(kj, cr, False), init), True)
            for h in heads:
                dq_s[h, qrows, :] = last[h][0]
            return carry0

        lax.fori_loop(0, n_tiles, q_tile, 0)

        def finish(i, carry):
            rows = pl.ds(pl.multiple_of(i * T, T), T)
            new = []
            for h in heads:
                lanes = slice(HEAD_DIM * h, HEAD_DIM * (h + 1))
                q = q_ref[rows, lanes]
                k = k_ref[rows, lanes]
                rq = lax.rsqrt(jnp.mean(q * q, axis=-1, keepdims=True) + EPS)
                rk = lax.rsqrt(jnp.mean(k * k, axis=-1, keepdims=True) + EPS)
                qh = q * rq
                kh = k * rk
                dqn = dq_s[h, rows, :] * scale
                dkn = -dk_s[h, rows, :]
                dqh = dqn * qg_ref[...]
                dkh = dkn * kg_ref[...]
                dq_ref[rows, lanes] = (rq * (dqh - qh * jnp.mean(dqh * qh, axis=-1, keepdims=True))).astype(BF16)
                dk_ref[rows, lanes] = (rk * (dkh - kh * jnp.mean(dkh * kh, axis=-1, keepdims=True))).astype(BF16)
                dv_ref[rows, lanes] = dv_s[h, rows, :].astype(BF16)
                new.append(carry[2 * h] + jnp.sum(dqn * qh, axis=0, keepdims=True))
                new.append(carry[2 * h + 1] + jnp.sum(dkn * kh, axis=0, keepdims=True))
            return tuple(new)

        zero = jnp.zeros((1, HEAD_DIM), F32)
        sums = lax.fori_loop(0, n_tiles, finish, (zero,) * (2 * len(heads)))
        dqg_ref[...] += sum(sums[0::2])
        dkg_ref[...] += sum(sums[1::2])
        if n_comm:
            @pl.when(pl.program_id(0) == pairs - 1)
            def _():
                for cp in _scatter_copies(p_refs, lb_refs, kinds, *sems):
                    cp.wait()

    blk = lambda off: pl.BlockSpec((S, ATTN_BLOCK), lambda p: (0, off + p))
    row64 = pl.BlockSpec((1, HEAD_DIM), lambda p: (0, 0))
    tri = pl.BlockSpec((T, T), lambda p: (0, 0))
    n_sem = n_comm * (N_CHIPS - 1)
    out = pl.pallas_call(
        body, name="attn_bwd_scatter" if n_comm else "attn_bwd", grid=(pairs,),
        in_specs=[blk(q_blk), blk(k_blk), blk(v_blk), row64, row64,
                  pl.BlockSpec((None, S, ATTN_BLOCK), lambda p: (p, 0, 0)), blk(0), tri, tri,
                  pl.BlockSpec(memory_space=pltpu.SMEM)] + [ANY] * n_comm,
        out_specs=[blk(0), blk(0), blk(0), row64, row64] + [ANY] * n_comm,
        out_shape=[jax.ShapeDtypeStruct((S, ATTN_WIDTH), BF16)] * 3 + [jax.ShapeDtypeStruct((1, HEAD_DIM), F32)] * 2
        + _scatter_shapes(parts, kinds),
        scratch_shapes=[pltpu.VMEM((ATTN_HEADS_PER_STEP, S, HEAD_DIM), BF16)] * 3
        + [pltpu.VMEM((ATTN_HEADS_PER_STEP, S, HEAD_DIM), F32)] * 3
        + ([pltpu.SemaphoreType.DMA((n_sem,))] * 2 if n_comm else []),
        compiler_params=pltpu.CompilerParams(dimension_semantics=("arbitrary",), vmem_limit_bytes=VMEM_LIMIT_BYTES,
                                             has_side_effects=bool(n_comm)),
    )(proj, proj, proj, qg, kg, ltot[0], dmix, prefix_incl, prefix_excl, ltot[1], *parts)
    return out[:5], list(out[5:])


def _shifted(win, n_rows):
    return [win if b == 0 else pltpu.roll(win, n_rows - b, 0) for b in range(SUBLANES)]


def _taps(variants, offsets, tm):
    return {o: variants[o % SUBLANES][(o // SUBLANES) * SUBLANES:(o // SUBLANES) * SUBLANES + tm, :] for o in offsets}


def _fold_rows(a):
    return jnp.sum(a.reshape(a.shape[0] // SUBLANES, SUBLANES, a.shape[1]), axis=0)


def _glu_conv_fwd(proj, w, bias, guest=None, *, tm=256):
    S = proj.shape[0]
    CB = LANES
    a_blk, b_blk = 3 * ATTN_WIDTH // CB, (3 * ATTN_WIDTH + CONV_WIDTH) // CB
    n_rows = tm + CONV_PAD

    def body(a_ref, b_ref, w_ref, bias_ref, c1_ref, pad_s):
        pad_s[0:CONV_PAD, :] = jnp.zeros((CONV_PAD, CB), F32)

        def fill(i, carry):
            rows = pl.ds(pl.multiple_of(i * tm, tm), tm)
            pad_s[pl.ds(pl.multiple_of(CONV_PAD + i * tm, SUBLANES), tm), :] = a_ref[rows, :] * _sigmoid(b_ref[rows, :])
            return carry

        lax.fori_loop(0, S // tm, fill, 0)

        def conv(i, carry):
            r0 = pl.multiple_of(i * tm, tm)
            taps = _taps(_shifted(pad_s[pl.ds(r0, n_rows), :], n_rows), range(2, 2 + CONV_KERNEL), tm)
            acc = jnp.broadcast_to(bias_ref[...], (tm, CB))
            for k in range(CONV_KERNEL):
                acc = acc + w_ref[k:k + 1, :] * taps[k + 2]
            c1_ref[pl.ds(r0, tm), :] = acc
            return carry

        lax.fori_loop(0, S // tm, conv, 0)

    (c1,), guest_out = _hosted_call(
        body, guest, name="glu_conv_fwd", grid=(CONV_WIDTH // CB,),
        in_specs=[pl.BlockSpec((S, CB), lambda j: (0, a_blk + j)), pl.BlockSpec((S, CB), lambda j: (0, b_blk + j)),
                  pl.BlockSpec((CONV_KERNEL, CB), lambda j: (0, j)), pl.BlockSpec((1, CB), lambda j: (0, j))],
        out_specs=[pl.BlockSpec((S, CB), lambda j: (0, j))],
        out_shape=[jax.ShapeDtypeStruct((S, CONV_WIDTH), F32)],
        scratch_shapes=[pltpu.VMEM((S + CONV_PAD, CB), F32)], operands=[proj, proj, w, bias])
    return c1, guest_out


def _glu_conv_bwd(proj, w, dc1, guest=None, *, tm=256):
    S = proj.shape[0]
    CB = LANES
    a_blk, b_blk = 3 * ATTN_WIDTH // CB, (3 * ATTN_WIDTH + CONV_WIDTH) // CB
    n_rows = tm + CONV_PAD

    def body(a_ref, b_ref, w_ref, dc1_ref, da_ref, db_ref, dw_ref, dbias_ref, pad_s, dpad_s, dw_s):
        pad_s[0:CONV_PAD, :] = jnp.zeros((CONV_PAD, CB), F32)
        dpad_s[S:S + CONV_PAD, :] = jnp.zeros((CONV_PAD, CB), F32)
        dw_s[...] = jnp.zeros_like(dw_s)

        def fill(i, carry):
            rows = pl.ds(pl.multiple_of(i * tm, tm), tm)
            pad_s[pl.ds(pl.multiple_of(CONV_PAD + i * tm, SUBLANES), tm), :] = a_ref[rows, :] * _sigmoid(b_ref[rows, :])
            dpad_s[rows, :] = dc1_ref[rows, :]
            return carry

        lax.fori_loop(0, S // tm, fill, 0)

        def conv(i, carry):
            r0 = pl.multiple_of(i * tm, tm)
            rows = pl.ds(r0, tm)
            taps = _taps(_shifted(dpad_s[pl.ds(r0, n_rows), :], n_rows), range(CONV_KERNEL), tm)
            acc = jnp.zeros((tm, CB), F32)
            for k in range(CONV_KERNEL):
                acc = acc + w_ref[k:k + 1, :] * taps[CONV_KERNEL - 1 - k]
            a = a_ref[rows, :]
            sg = _sigmoid(b_ref[rows, :])
            da_ref[rows, :] = (acc * sg).astype(BF16)
            db_ref[rows, :] = (acc * a * sg * (1.0 - sg)).astype(BF16)
            d = taps[0]
            taps = _taps(_shifted(pad_s[pl.ds(r0, n_rows), :], n_rows), range(2, 2 + CONV_KERNEL), tm)
            for k in range(CONV_KERNEL):
                dw_s[SUBLANES * k:SUBLANES * (k + 1), :] += _fold_rows(d * taps[k + 2])
            dw_s[SUBLANES * CONV_KERNEL:SUBLANES * (CONV_KERNEL + 1), :] += _fold_rows(d)
            return carry

        lax.fori_loop(0, S // tm, conv, 0)
        for k in range(CONV_KERNEL):
            dw_ref[k:k + 1, :] = jnp.sum(dw_s[SUBLANES * k:SUBLANES * (k + 1), :], axis=0, keepdims=True)
        dbias_ref[...] = jnp.sum(dw_s[SUBLANES * CONV_KERNEL:SUBLANES * (CONV_KERNEL + 1), :], axis=0, keepdims=True)

    col = lambda off: pl.BlockSpec((S, CB), lambda j: (0, off + j))
    out, guest_out = _hosted_call(
        body, guest, name="glu_conv_bwd", grid=(CONV_WIDTH // CB,),
        in_specs=[col(a_blk), col(b_blk), pl.BlockSpec((CONV_KERNEL, CB), lambda j: (0, j)), col(0)],
        out_specs=[col(0), col(0), pl.BlockSpec((CONV_KERNEL, CB), lambda j: (0, j)), pl.BlockSpec((1, CB), lambda j: (0, j))],
        out_shape=[jax.ShapeDtypeStruct((S, CONV_WIDTH), BF16)] * 2
        + [jax.ShapeDtypeStruct((CONV_KERNEL, CONV_WIDTH), F32), jax.ShapeDtypeStruct((1, CONV_WIDTH), F32)],
        scratch_shapes=[pltpu.VMEM((S + CONV_PAD, CB), F32), pltpu.VMEM((S + CONV_PAD, CB), F32),
                        pltpu.VMEM((SUBLANES * (CONV_KERNEL + 1), CB), F32)], operands=[proj, proj, w, dc1])
    return (*out, guest_out)


def _ln_stats(c1):
    mu = jnp.mean(c1, axis=-1, keepdims=True)
    xc = c1 - mu
    r = lax.rsqrt(jnp.mean(xc * xc, axis=-1, keepdims=True) + EPS)
    return xc * r, r


def _ln_silu_fwd(c1, g, b, *, tm=512):
    S, C = c1.shape

    def body(c1_ref, g_ref, b_ref, c_ref, ct_ref):
        yh, _ = _ln_stats(c1_ref[...])
        y = yh * g_ref[...] + b_ref[...]
        c = y * _sigmoid(y)
        c_ref[...] = c.astype(BF16)
        ct_ref[...] = c.T.astype(BF16)

    vec = pl.BlockSpec((1, C), lambda i: (0, 0))
    return pl.pallas_call(
        body, name="ln_silu_fwd", grid=(S // tm,),
        in_specs=[pl.BlockSpec((tm, C), lambda i: (i, 0)), vec, vec],
        out_specs=[pl.BlockSpec((tm, C), lambda i: (i, 0)), pl.BlockSpec((C, tm), lambda i: (0, i))],
        out_shape=[jax.ShapeDtypeStruct((S, C), BF16), jax.ShapeDtypeStruct((C, S), BF16)],
        compiler_params=_params("parallel"),
    )(c1, g, b)


def _ln_silu_bwd(c1, g, b, dmix, *, tm=512):
    S, C = c1.shape

    def body(c1_ref, g_ref, b_ref, dc_ref, dc1_ref, dg_ref, db_ref):
        yh, r = _ln_stats(c1_ref[...])
        y = yh * g_ref[...] + b_ref[...]
        sg = _sigmoid(y)
        dy = dc_ref[...] * (sg * (1.0 + y * (1.0 - sg)))
        dyh = dy * g_ref[...]
        dc1_ref[...] = r * (dyh - jnp.mean(dyh, axis=-1, keepdims=True)
                            - yh * jnp.mean(dyh * yh, axis=-1, keepdims=True))

        @pl.when(pl.program_id(0) == 0)
        def _():
            dg_ref[...] = jnp.zeros_like(dg_ref)
            db_ref[...] = jnp.zeros_like(db_ref)

        dg_ref[...] += jnp.sum(dy * yh, axis=0, keepdims=True)
        db_ref[...] += jnp.sum(dy, axis=0, keepdims=True)

    vec = pl.BlockSpec((1, C), lambda i: (0, 0))
    return pl.pallas_call(
        body, name="ln_silu_bwd", grid=(S // tm,),
        in_specs=[pl.BlockSpec((tm, C), lambda i: (i, 0)), vec, vec, pl.BlockSpec((tm, C), lambda i: (i, 1))],
        out_specs=[pl.BlockSpec((tm, C), lambda i: (i, 0)), vec, vec],
        out_shape=[jax.ShapeDtypeStruct((S, C), F32), jax.ShapeDtypeStruct((1, C), F32), jax.ShapeDtypeStruct((1, C), F32)],
        compiler_params=_params("arbitrary"),
    )(c1, g, b, dmix)


FFN_CB = 256


def _ffn_gate(pad_s, w_ref, bias_ref, r0, tm):
    n_rows = tm + FFN_PAD
    taps = _taps(_shifted(pad_s[pl.ds(r0, n_rows), :], n_rows), range(FFN_PAD - 2, FFN_PAD + 1), tm)
    g1 = bias_ref[...] + w_ref[0:1, :] * taps[6] + w_ref[1:2, :] * taps[7] + w_ref[2:3, :] * taps[8]
    return g1, taps


def _ffn_act_fwd(u, w, bias, guest=None, *, tm=256):
    S = u.shape[0]
    CB = FFN_CB
    nb = D_FF // CB

    def body(g_ref, v_ref, w_ref, bias_ref, o_ref, ot_ref, pad_s):
        pad_s[0:FFN_PAD, :] = jnp.zeros((FFN_PAD, CB), F32)

        def fill(i, carry):
            pad_s[pl.ds(pl.multiple_of(FFN_PAD + i * tm, SUBLANES), tm), :] = g_ref[pl.ds(pl.multiple_of(i * tm, tm), tm), :].astype(F32)
            return carry

        lax.fori_loop(0, S // tm, fill, 0)

        def act(i, carry):
            r0 = pl.multiple_of(i * tm, tm)
            g1, _ = _ffn_gate(pad_s, w_ref, bias_ref, r0, tm)
            a = g1 * _sigmoid(g1) * v_ref[pl.ds(r0, tm), :].astype(F32)
            o_ref[pl.ds(r0, tm), :] = a.astype(BF16)
            ot_ref[:, pl.ds(r0, tm)] = a.T.astype(BF16)
            return carry

        lax.fori_loop(0, S // tm, act, 0)

    (act, actt), guest_out = _hosted_call(
        body, guest, name="ffn_act_fwd", grid=(nb,),
        in_specs=[pl.BlockSpec((S, CB), lambda j: (0, j)), pl.BlockSpec((S, CB), lambda j: (0, nb + j)),
                  pl.BlockSpec((FFN_KERNEL, CB), lambda j: (0, j)), pl.BlockSpec((1, CB), lambda j: (0, j))],
        out_specs=[pl.BlockSpec((S, CB), lambda j: (0, j)), pl.BlockSpec((CB, S), lambda j: (j, 0))],
        out_shape=[jax.ShapeDtypeStruct((S, D_FF), BF16), jax.ShapeDtypeStruct((D_FF, S), BF16)],
        scratch_shapes=[pltpu.VMEM((S + FFN_PAD, CB), F32)], operands=[u, u, w, bias])
    return act, actt, guest_out


def _ffn_act_bwd(u, w, bias, dact, guest=None, *, tm=256):
    S = u.shape[0]
    CB = FFN_CB
    nb = D_FF // CB

    def body(g_ref, v_ref, w_ref, bias_ref, da_ref, dg_ref, dv_ref, dw_ref, dbias_ref, pad_s, dpad_s, dw_s):
        pad_s[0:FFN_PAD, :] = jnp.zeros((FFN_PAD, CB), F32)
        dpad_s[S:S + FFN_PAD, :] = jnp.zeros((FFN_PAD, CB), F32)
        dw_s[...] = jnp.zeros_like(dw_s)

        def fill(i, carry):
            pad_s[pl.ds(pl.multiple_of(FFN_PAD + i * tm, SUBLANES), tm), :] = g_ref[pl.ds(pl.multiple_of(i * tm, tm), tm), :].astype(F32)
            return carry

        lax.fori_loop(0, S // tm, fill, 0)

        def first(i, carry):
            r0 = pl.multiple_of(i * tm, tm)
            rows = pl.ds(r0, tm)
            g1, taps = _ffn_gate(pad_s, w_ref, bias_ref, r0, tm)
            sg = _sigmoid(g1)
            da = da_ref[rows, :].astype(F32)
            dv_ref[rows, :] = (da * g1 * sg).astype(BF16)
            dg1 = da * v_ref[rows, :].astype(F32) * (sg * (1.0 + g1 * (1.0 - sg)))
            dpad_s[rows, :] = dg1
            for k in range(FFN_KERNEL):
                dw_s[SUBLANES * k:SUBLANES * (k + 1), :] += _fold_rows(dg1 * taps[FFN_PAD - 2 + k])
            dw_s[SUBLANES * FFN_KERNEL:SUBLANES * (FFN_KERNEL + 1), :] += _fold_rows(dg1)
            return carry

        lax.fori_loop(0, S // tm, first, 0)

        def second(i, carry):
            r0 = pl.multiple_of(i * tm, tm)
            n_rows = tm + FFN_PAD
            taps = _taps(_shifted(dpad_s[pl.ds(r0, n_rows), :], n_rows), range(FFN_KERNEL), tm)
            dg_ref[pl.ds(r0, tm), :] = (w_ref[2:3, :] * taps[0] + w_ref[1:2, :] * taps[1] + w_ref[0:1, :] * taps[2]).astype(BF16)
            return carry

        lax.fori_loop(0, S // tm, second, 0)
        for k in range(FFN_KERNEL):
            dw_ref[k:k + 1, :] = jnp.sum(dw_s[SUBLANES * k:SUBLANES * (k + 1), :], axis=0, keepdims=True)
        dbias_ref[...] = jnp.sum(dw_s[SUBLANES * FFN_KERNEL:SUBLANES * (FFN_KERNEL + 1), :], axis=0, keepdims=True)

    col = lambda off: pl.BlockSpec((S, CB), lambda j: (0, off + j))
    wspec = pl.BlockSpec((FFN_KERNEL, CB), lambda j: (0, j))
    bspec = pl.BlockSpec((1, CB), lambda j: (0, j))
    out, guest_out = _hosted_call(
        body, guest, name="ffn_act_bwd", grid=(nb,),
        in_specs=[col(0), col(nb), wspec, bspec, col(0)],
        out_specs=[col(0), col(0), wspec, bspec],
        out_shape=[jax.ShapeDtypeStruct((S, D_FF), BF16)] * 2
        + [jax.ShapeDtypeStruct((FFN_KERNEL, D_FF), F32), jax.ShapeDtypeStruct((1, D_FF), F32)],
        scratch_shapes=[pltpu.VMEM((S + FFN_PAD, CB), F32), pltpu.VMEM((S + FFN_PAD, CB), F32),
                        pltpu.VMEM((SUBLANES * (FFN_KERNEL + 1), CB), F32)], operands=[u, u, w, bias, dact])
    return (*out, guest_out)


def _loss_grad(y, target, *, tm=512):
    S, D = y.shape

    def body(y_ref, t_ref, dy_ref, l_ref):
        d = y_ref[...] - t_ref[...]
        dy_ref[...] = d * (1.0 / D)

        @pl.when(pl.program_id(0) == 0)
        def _():
            l_ref[...] = jnp.zeros_like(l_ref)

        l_ref[...] += 0.5 * jnp.sum(jnp.mean(d * d, axis=-1, keepdims=True), axis=0, keepdims=True)

    dy, l = pl.pallas_call(
        body, name="loss_grad", grid=(S // tm,),
        in_specs=[pl.BlockSpec((tm, D), lambda i: (i, 0))] * 2,
        out_specs=[pl.BlockSpec((tm, D), lambda i: (i, 0)), pl.BlockSpec((SUBLANES, LANES), lambda i: (0, 0))],
        out_shape=[jax.ShapeDtypeStruct((S, D), F32), jax.ShapeDtypeStruct((SUBLANES, LANES), F32)],
        compiler_params=_params("arbitrary"),
    )(y, target)
    return dy, l[0, 0]


def _row_tile(rows, cap=512):
    t = min(rows, cap)
    while rows % t or t % SUBLANES:
        t -= 1
    return t


def _adam_update(w, g, m, v):
    m1 = ADAM_B1 * m + (1.0 - ADAM_B1) * g
    v1 = ADAM_B2 * v + (1.0 - ADAM_B2) * (g * g)
    m_hat = m1 / (1.0 - ADAM_B1 ** ADAM_STEP)
    v_hat = v1 / (1.0 - ADAM_B2 ** ADAM_STEP)
    return -ADAM_LR * (m_hat / (jnp.sqrt(v_hat) + ADAM_EPS) + ADAM_WD * w), m1, v1


def _adamw(w, g, m, v):
    R, C = w.shape
    tr = _row_tile(R, 256)

    def body(w_ref, g_ref, m_ref, v_ref, d_ref, nm_ref, nv_ref):
        d_ref[...], nm_ref[...], nv_ref[...] = _adam_update(w_ref[...], g_ref[...], m_ref[...], v_ref[...])

    spec = pl.BlockSpec((tr, C), lambda i: (i, 0))
    return pl.pallas_call(
        body, name="adamw", grid=(R // tr,),
        in_specs=[spec] * 4, out_specs=[spec] * 3,
        out_shape=[jax.ShapeDtypeStruct((R, C), F32)] * 3,
        compiler_params=_params("parallel"),
    )(w, g, m, v)


def _adamw_layer(layer, w, g, m, v, prev):
    L, R, C = w.shape
    tr = _row_tile(R, 256)
    n_prev = 4 if prev else 0

    def body(*refs):
        w_ref, g_ref, m_ref, v_ref = refs[:4]
        go_ref, d_ref, nm_ref, nv_ref = refs[4 + n_prev:8 + n_prev]
        gv = g_ref[...]
        go_ref[...] = gv
        d_ref[...], nm_ref[...], nv_ref[...] = _adam_update(w_ref[...], gv, m_ref[...], v_ref[...])

    stacked = pl.BlockSpec((None, tr, C), lambda i: (layer, i, 0))
    return pl.pallas_call(
        body, name="adamw_layer", grid=(R // tr,),
        in_specs=[stacked, pl.BlockSpec((tr, C), lambda i: (i, 0)), stacked, stacked] + [ANY] * n_prev,
        out_specs=[stacked] * 4, out_shape=[jax.ShapeDtypeStruct((L, R, C), F32)] * 4,
        input_output_aliases={4 + j: j for j in range(n_prev)},
        compiler_params=_params("parallel"),
    )(w, g, m, v, *(prev or ()))


def _cast_into_full(w, layer, kind, chip):
    _, R, C = w.shape
    tr = _row_tile(R, 512)

    def body(chip_ref, w_ref, o_ref):
        o_ref[...] = w_ref[...].astype(BF16)

    if kind == "col":
        out_shape = jax.ShapeDtypeStruct((1, R, N_CHIPS * C), BF16)
        out_spec = pl.BlockSpec((None, tr, C), lambda i, chip_ref: (0, i, chip_ref[0]))
    else:
        out_shape = jax.ShapeDtypeStruct((1, N_CHIPS * R, C), BF16)
        out_spec = pl.BlockSpec((None, tr, C), lambda i, chip_ref: (0, chip_ref[0] * (R // tr) + i, 0))
    return pl.pallas_call(
        body, name="cast_into_full_" + kind,
        grid_spec=pltpu.PrefetchScalarGridSpec(
            num_scalar_prefetch=1, grid=(R // tr,),
            in_specs=[pl.BlockSpec((None, tr, C), lambda i, chip_ref: (layer, i, 0))], out_specs=out_spec),
        out_shape=out_shape, compiler_params=_params("parallel"),
    )(chip, w)


def _add_half(g4, la, c):
    L, _, H, W = g4.shape
    th = _row_tile(H, 256)

    def body(c_ref, g_ref, la_ref, o_ref):
        o_ref[...] = (g_ref[...].astype(F32) + la_ref[...].astype(F32)).astype(BF16)

    return pl.pallas_call(
        body, name="add_half",
        grid_spec=pltpu.PrefetchScalarGridSpec(
            num_scalar_prefetch=1, grid=(L, H // th),
            in_specs=[pl.BlockSpec((None, None, th, W), lambda l, i, c_ref: (l, c_ref[0], i, 0)),
                      pl.BlockSpec((None, th, W), lambda l, i, c_ref: (l, i, 0))],
            out_specs=pl.BlockSpec((None, th, W), lambda l, i, c_ref: (l, i, 0))),
        out_shape=jax.ShapeDtypeStruct((L, H, W), BF16),
        compiler_params=_params("parallel", "parallel"),
    )(c, g4, la)


def _add_parts(p, lb, place, kind):
    _, L, H, C = lb.shape
    th = _row_tile(H, 256)

    def body(s_ref, p_ref, lb_ref, o_ref):
        acc = p_ref[...].astype(F32)
        for k in range(N_CHIPS - 1):
            acc = acc + lb_ref[k].astype(F32)
        o_ref[...] = acc

    if kind == "col":
        p_spec = pl.BlockSpec((None, th, C), lambda l, i, s_ref: (l, i, s_ref[0]))
    else:
        p_spec = pl.BlockSpec((None, None, th, C), lambda l, i, s_ref: (l, s_ref[0], i, 0))
    return pl.pallas_call(
        body, name="add_parts_" + kind,
        grid_spec=pltpu.PrefetchScalarGridSpec(
            num_scalar_prefetch=1, grid=(L, H // th),
            in_specs=[p_spec, pl.BlockSpec((N_CHIPS - 1, None, th, C), lambda l, i, s_ref: (0, l, i, 0))],
            out_specs=pl.BlockSpec((None, None, th, C), lambda l, i, s_ref: (l, s_ref[1], i, 0))),
        out_shape=jax.ShapeDtypeStruct((L, 2, H, C), F32),
        compiler_params=_params("parallel", "parallel"),
    )(place, p, lb)


ANY = pl.BlockSpec(memory_space=pl.ANY)


def _place():
    x, y, c = lax.axis_index("x"), lax.axis_index("y"), lax.axis_index("c")
    chips = [(1 - x, y), (x, 1 - y), (1 - x, 1 - y)]
    return x, y, c, chips


def _comm_call(body, name, ins, out_shape, n_remote, n_local, aliases=None):
    scratch = [pltpu.SemaphoreType.DMA((n_remote,)), pltpu.SemaphoreType.DMA((n_remote,))]
    if n_local:
        scratch.append(pltpu.SemaphoreType.DMA((n_local,)))
    return pl.pallas_call(
        body, name=name, in_specs=[ANY] * len(ins), out_specs=[ANY] * len(out_shape), out_shape=out_shape,
        scratch_shapes=scratch, input_output_aliases=aliases or {},
        compiler_params=pltpu.CompilerParams(has_side_effects=True),
    )(*ins)


def _remote(src, dst, send, recv, k, to):
    return pltpu.make_async_remote_copy(src_ref=src, dst_ref=dst, send_sem=send.at[k], recv_sem=recv.at[k],
                                        device_id=to, device_id_type=MESH)


def _gather_weights(fulls, kinds):
    n = len(fulls)
    out_shape = [jax.ShapeDtypeStruct(f.shape, f.dtype) for f in fulls]

    def body(*refs):
        outs, (send, recv) = refs[n:2 * n], refs[2 * n:]
        first = _gather_copies(outs, kinds, send, recv, 0)
        for cp in first:
            cp.start()
        for cp in first:
            cp.wait()
        passed = _pass_on_copies(outs, kinds, send, recv, len(first))
        for cp in passed:
            cp.start()
        for cp in passed:
            cp.wait()

    return _comm_call(body, "gather_weights", fulls, out_shape, 2 * n * (N_CHIPS - 1), 0, {a: a for a in range(n)})


def _window(ref, kind, s, h):
    if kind == "col":
        H, C = ref.shape[1] // 2, ref.shape[2] // N_CHIPS
        return ref.at[:, pl.ds(pl.multiple_of(h * H, 16), H), pl.ds(pl.multiple_of(s * C, LANES), C)]
    R = ref.shape[1] // N_CHIPS
    return ref.at[:, pl.ds(pl.multiple_of(s * R + h * (R // 2), 16), R // 2), :]


def _gather_copies(outs, kinds, send, recv, sem0):
    x, y, c, chips = _place()
    me = 2 * x + y
    return [_remote(_window(o, kind, me, c), _window(o, kind, me, c), send, recv, sem0 + a * (N_CHIPS - 1) + k, (*chip, c))
            for a, (o, kind) in enumerate(zip(outs, kinds)) for k, chip in enumerate(chips)]


def _pass_on_copies(outs, kinds, send, recv, sem0):
    x, y, c, chips = _place()
    cps = []
    for a, (o, kind) in enumerate(zip(outs, kinds)):
        for k, chip in enumerate(chips):
            landed = _window(o, kind, 2 * chip[0] + chip[1], c)
            cps.append(_remote(landed, landed, send, recv, sem0 + a * (N_CHIPS - 1) + k, (x, y, 1 - c)))
    return cps


def _pass_on(fulls, kinds):
    n = len(fulls)
    out_shape = [jax.ShapeDtypeStruct(f.shape, f.dtype) for f in fulls]

    def body(*refs):
        outs, (send, recv) = refs[n:2 * n], refs[2 * n:]
        cps = _pass_on_copies(outs, kinds, send, recv, 0)
        for cp in cps:
            cp.start()
        for cp in cps:
            cp.wait()

    return _comm_call(body, "pass_on", fulls, out_shape, n * (N_CHIPS - 1), 0, {a: a for a in range(n)})


def _gather_small(shards):
    n = len(shards)
    out_shape = [jax.ShapeDtypeStruct((N_CHIPS,) + s.shape, s.dtype) for s in shards]

    def body(*refs):
        srcs, outs, (send, recv, loc) = refs[:n], refs[n:2 * n], refs[2 * n:]
        x, y, c, chips = _place()
        me = 2 * x + y
        remote, local = [], []
        for a in range(n):
            local.append(pltpu.make_async_copy(srcs[a], outs[a].at[me], loc.at[a]))
            for k, chip in enumerate(chips):
                remote.append(_remote(srcs[a], outs[a].at[me], send, recv, a * (N_CHIPS - 1) + k, (*chip, c)))
        for cp in local + remote:
            cp.start()
        for cp in remote + local:
            cp.wait()

    return _comm_call(body, "gather_small", shards, out_shape, n * (N_CHIPS - 1), n)


def _exchange_halves(g4s):
    n = len(g4s)
    out_shape = [jax.ShapeDtypeStruct((g.shape[0],) + g.shape[2:], g.dtype) for g in g4s]

    def body(*refs):
        gs, las, (send, recv) = refs[:n], refs[n:2 * n], refs[2 * n:]
        x, y, c, _ = _place()
        cps = [_remote(gs[a].at[:, 1 - c], las[a], send, recv, a, (x, y, 1 - c)) for a in range(n)]
        for cp in cps:
            cp.start()
        for cp in cps:
            cp.wait()

    return _comm_call(body, "exchange_halves", g4s, out_shape, n, 0)


def _scatter_partials(ps, kinds):
    n = len(ps)

    def body(*refs):
        srcs, lbs, (send, recv) = refs[:n], refs[n:2 * n], refs[2 * n:]
        cps = _scatter_copies(srcs, lbs, kinds, send, recv)
        for cp in cps:
            cp.start()
        for cp in cps:
            cp.wait()

    return _comm_call(body, "scatter_partials", ps, _scatter_shapes(ps, kinds), n * (N_CHIPS - 1), 0)


def _scatter_shapes(ps, kinds):
    out_shape = []
    for p, kind in zip(ps, kinds):
        L, H, C = (p.shape[0], p.shape[1], p.shape[2] // N_CHIPS) if kind == "col" else (p.shape[0], p.shape[2], p.shape[3])
        out_shape.append(jax.ShapeDtypeStruct((N_CHIPS - 1, L, H, C), p.dtype))
    return out_shape


def _scatter_copies(srcs, lbs, kinds, send, recv, sem0=0):
    x, y, c, chips = _place()
    cps = []
    for a, (src, lb, kind) in enumerate(zip(srcs, lbs, kinds)):
        C = lb.shape[3]
        for k, chip in enumerate(chips):
            s = 2 * chip[0] + chip[1]
            part = src.at[:, :, pl.ds(pl.multiple_of(s * C, LANES), C)] if kind == "col" else src.at[:, s]
            cps.append(_remote(part, lb.at[k], send, recv, sem0 + a * (N_CHIPS - 1) + k, (*chip, c)))
    return cps


def _share_halves(g4s):
    n = len(g4s)
    out_shape = [jax.ShapeDtypeStruct(g.shape, g.dtype) for g in g4s]

    def body(*refs):
        outs, (send, recv) = refs[n:2 * n], refs[2 * n:]
        x, y, c, _ = _place()
        cps = [_remote(outs[a].at[:, c], outs[a].at[:, c], send, recv, a, (x, y, 1 - c)) for a in range(n)]
        for cp in cps:
            cp.start()
        for cp in cps:
            cp.wait()

    return _comm_call(body, "share_halves", g4s, out_shape, n, 0, {a: a for a in range(n)})


def _allreduce_small(part):
    N = part.shape[0]

    def body(p_ref, o_ref, buf, send, recv):
        x, y, c, _ = _place()
        me = 4 * x + 2 * y + c
        buf[me] = p_ref[...]
        cps = []
        for k in range(1, N_DEV):
            peer = (1 - x if k & 4 else x, 1 - y if k & 2 else y, 1 - c if k & 1 else c)
            cps.append(_remote(p_ref, buf.at[me], send, recv, k - 1, peer))
        for cp in cps:
            cp.start()
        for cp in cps:
            cp.wait()
        acc = buf[0]
        for i in range(1, N_DEV):
            acc = acc + buf[i]
        o_ref[...] = acc

    vmem = pl.BlockSpec(memory_space=pltpu.VMEM)
    return pl.pallas_call(
        body, name="allreduce_small", in_specs=[vmem], out_specs=vmem,
        out_shape=jax.ShapeDtypeStruct((N, LANES), F32),
        scratch_shapes=[pltpu.VMEM((N_DEV, N, LANES), F32), pltpu.SemaphoreType.DMA((N_DEV - 1,)),
                        pltpu.SemaphoreType.DMA((N_DEV - 1,))],
        compiler_params=pltpu.CompilerParams(has_side_effects=True, vmem_limit_bytes=VMEM_LIMIT_BYTES),
    )(part)


AFTER_ATTENTION = ("w_out", "w_up", "w_down")


def _layer_fwd(x, p, full=None, l=0):
    comm = full is not None

    def guest(first, second):
        first, second = ([k for k in ks if comm and k[0] < len(full)] for ks in (first, second))
        if not first + second:
            return None, []
        return _gather_guest([full[i][n] for i, n in first], [BIG_KIND[n] for _, n in first],
                             [full[i][n] for i, n in second], [BIG_KIND[n] for _, n in second]), first + second

    def done(keys, bufs):
        for (i, n), buf in zip(keys, bufs):
            full[i][n] = buf

    def weight(n):
        return full[l][n][0] if comm else p[n]

    in_sweep = ([(l, "w_up")] if l else [(l, n) for n in AFTER_ATTENTION]) if comm else []
    g, keys = guest([], [(l, "w_out")] if l else [])
    h1t, proj, out = _norm_matmul(x, p["norm1_g"], weight("w_in"), g)
    done(keys, out)
    attn, ltot, attnt, out = _attn_fwd(proj, p["q_norm_g"], p["k_norm_g"],
                                       [full[i][n] for i, n in in_sweep], [BIG_KIND[n] for _, n in in_sweep])
    done(in_sweep, out)
    g, keys = guest([], in_sweep)
    c1, out = _glu_conv_fwd(proj, p["conv_dw_w"], p["conv_dw_b"], g)
    done(keys, out)
    c, ct = _ln_silu_fwd(c1, p["conv_ln_g"], p["conv_ln_b"])
    x_mid, _ = _matmul_res([attn, c], weight("w_out"), x)
    g, keys = guest([(l + 1, "w_down")], [])
    h2t, u, out = _norm_matmul(x_mid, p["norm2_g"], weight("w_up"), g, out_dtype=BF16)
    done(keys, out)
    g, keys = guest([(l + 1, "w_in")], [(l + 1, "w_down")])
    act, actt, out = _ffn_act_fwd(u, p["ffn_dw_w"], p["ffn_dw_b"], g)
    done(keys, out)
    g, keys = guest([(l + 1, "w_out")], [(l + 1, "w_in")])
    x_out, out = _matmul_res([act], weight("w_down"), x_mid, g)
    done(keys, out)
    saved = dict(x=x, h1t=h1t, proj=proj, attnt=attnt, ltot=ltot, c1=c1, ct=ct, x_mid=x_mid, h2t=h2t, u=u, actt=actt)
    return x_out, saved


def _chip_partials(gs, kinds, core):
    g4s = _halves_view(gs, kinds)
    return _add_halves(g4s, _exchange_halves(g4s), kinds, core)


def _halves_view(gs, kinds):
    return [g.reshape(1, 2, g.shape[0] // 2, g.shape[1]) if kind == "col"
            else g.reshape(N_CHIPS, 2, g.shape[0] // N_CHIPS // 2, g.shape[1]) for g, kind in zip(gs, kinds)]


def _add_halves(g4s, received, kinds, core):
    parts = [_add_half(g4, la, core) for g4, la in zip(g4s, received)]
    return [p if kind == "col" else p.reshape(1, N_CHIPS, p.shape[1], p.shape[2]) for p, kind in zip(parts, kinds)]


def _gather_guest(first, first_kinds, second, second_kinds):
    bufs = list(first) + list(second)
    n1 = len(first)

    def copies(ins, outs, send, recv, sem0=0):
        return (_gather_copies(outs[:n1], first_kinds, send, recv, sem0)
                + _pass_on_copies(outs[n1:], second_kinds, send, recv, sem0 + n1 * (N_CHIPS - 1)))

    return _Guest("gather", bufs, [jax.ShapeDtypeStruct(f.shape, f.dtype) for f in bufs],
                  {a: a for a in range(len(bufs))}, copies, len(bufs) * (N_CHIPS - 1))


def _scatter_guest(parts, kinds):
    return _Guest("scatter", list(parts), _scatter_shapes(parts, kinds), {},
                  lambda ins, outs, send, recv, sem0=0: _scatter_copies(ins, outs, kinds, send, recv, sem0),
                  len(parts) * (N_CHIPS - 1))


def _exchange_guest(g4s):
    def copies(ins, outs, send, recv, sem0=0):
        x, y, c, _ = _place()
        return [_remote(g.at[:, 1 - c], la, send, recv, sem0 + a, (x, y, 1 - c)) for a, (g, la) in enumerate(zip(ins, outs))]

    out_shape = [jax.ShapeDtypeStruct((g.shape[0],) + g.shape[2:], g.dtype) for g in g4s]
    return _Guest("exchange", list(g4s), out_shape, {}, copies, len(g4s))


def _share_guest(g4s):
    def copies(ins, outs, send, recv, sem0=0):
        x, y, c, _ = _place()
        return [_remote(o.at[:, c], o.at[:, c], send, recv, sem0 + a, (x, y, 1 - c)) for a, o in enumerate(outs)]

    n = len(g4s)
    return _Guest("share", list(g4s), [jax.ShapeDtypeStruct(g.shape, g.dtype) for g in g4s], {a: a for a in range(n)}, copies, n)


def _join(first, second):
    if first is None or second is None:
        return first or second
    n_in, n_out = len(first.ins), len(first.out_shape)

    def copies(ins, outs, send, recv, sem0=0):
        return (first.copies(ins[:n_in], outs[:n_out], send, recv, sem0)
                + second.copies(ins[n_in:], outs[n_out:], send, recv, sem0 + first.n_sem))

    aliases = dict(first.aliases, **{n_in + a: n_out + b for a, b in second.aliases.items()})
    return _Guest(first.name + "_" + second.name, first.ins + second.ins, first.out_shape + second.out_shape,
                  aliases, copies, first.n_sem + second.n_sem)


def _owned_sums(parts, landed, kinds, place):
    halves = [_add_parts(p, lb, place, kind) for p, lb, kind in zip(parts, landed, kinds)]
    return [g.reshape(2 * g.shape[2], g.shape[3]) for g in _share_halves(halves)]


def _layer_bwd(dx_out, s, p, comm=None):
    g = {}
    on = comm is not None
    core, place, pending = comm if on else (None, None, [])
    n_p = len(pending)
    finished = lambda arrays: [a.reshape(2 * a.shape[2], a.shape[3]) for a in arrays]

    g["w_down"] = _matmul_tn([s["actt"]], [dx_out], tk=D_FF // 2, tn=D_MODEL)
    if on:
        kinds_e = ["col"] * n_p + ["row"]
        g4s_e = _halves_view(list(pending) + [g["w_down"]], kinds_e)
    dact, received = _matmul_nt([dx_out], p["w_down"], BF16, _exchange_guest(g4s_e) if on else None)
    if on:
        parts_e = _add_halves(g4s_e, received, kinds_e, core)
    dgate, dval, g["ffn_dw_w"], g["ffn_dw_b"], landed_p = _ffn_act_bwd(
        s["u"], p["ffn_dw_w"], p["ffn_dw_b"], dact, _scatter_guest(parts_e[:n_p], kinds_e[:n_p]) if n_p else None)
    guest = None
    if on:
        halves_p = [_add_parts(pt, lb, place, "col") for pt, lb in zip(parts_e[:n_p], landed_p)]
        guest = _join(_share_guest(halves_p) if n_p else None, _scatter_guest(parts_e[n_p:], ["row"]))
    dx_mid, g["norm2_g"], out = _matmul_nt_rmsbwd([dgate, dval], p["w_up"], s["x_mid"], p["norm2_g"], dx_out, guest)
    shared_p, landed_d = out[:n_p], out[n_p:]
    g["w_up"] = _matmul_tn([s["h2t"]], [dgate, dval], tk=D_MODEL, tn=D_FF // 2)
    guest = _share_guest([_add_parts(parts_e[n_p], landed_d[0], place, "row")]) if on else None
    dmix, shared_d = _matmul_nt([dx_mid], p["w_out"], F32, guest)
    g["w_out"] = _matmul_tn([s["attnt"], s["ct"]], [dx_mid], tk=ATTN_WIDTH, tn=D_MODEL)
    dc1, g["conv_ln_g"], g["conv_ln_b"] = _ln_silu_bwd(s["c1"], p["conv_ln_g"], p["conv_ln_b"], dmix)

    late = ("w_out", "w_up")
    parts, kinds = [], []
    if on:
        kinds = [BIG_KIND[n] for n in late]
        g4s = _halves_view([g[n] for n in late], kinds)
    da, db, g["conv_dw_w"], g["conv_dw_b"], received = _glu_conv_bwd(
        s["proj"], p["conv_dw_w"], dc1, _exchange_guest(g4s) if on else None)
    if on:
        parts = _add_halves(g4s, received, kinds, core)
    (dq, dk, dv, g["q_norm_g"], g["k_norm_g"]), landed = _attn_bwd(
        s["proj"], p["q_norm_g"], p["k_norm_g"], s["ltot"], dmix, parts, kinds)
    guest = _share_guest([_add_parts(pt, lb, place, k) for pt, lb, k in zip(parts, landed, kinds)]) if on else None
    pieces = [dq, dk, dv, da, db]
    dx, g["norm1_g"], shared = _matmul_nt_rmsbwd(pieces, p["w_in"], s["x"], p["norm1_g"], dx_mid, guest)
    g["w_in"] = _matmul_tn([s["h1t"]], pieces, tk=D_MODEL, tn=ATTN_WIDTH)
    sums = {}
    if on:
        sums = dict(zip(late, finished(shared)), w_down=finished(shared_d)[0], pending=finished(shared_p))
    return dx, g, sums


WEIGHTS = ("norm1_g", "w_in", "q_norm_g", "k_norm_g", "conv_dw_w", "conv_dw_b", "conv_ln_g", "conv_ln_b",
           "w_out", "norm2_g", "w_up", "ffn_dw_w", "ffn_dw_b", "w_down")
BIG = ("w_in", "w_out", "w_up", "w_down")
BIG_KIND = {"w_in": "col", "w_out": "row", "w_up": "col", "w_down": "row"}
SMALL_SHARDED = ("conv_dw_w", "ffn_dw_w")
REPLICATED = tuple(n for n in WEIGHTS if n not in BIG + SMALL_SHARDED)


def _pack(arrays):
    flat = jnp.concatenate([a.reshape(-1) for a in arrays])
    rows = -(-flat.shape[0] // (SUBLANES * LANES)) * SUBLANES
    return jnp.pad(flat, (0, rows * LANES - flat.shape[0])).reshape(rows, LANES)


def _unpack(packed, shapes):
    flat = packed.reshape(-1)
    out, off = [], 0
    for shape in shapes:
        size = 1
        for d in shape:
            size *= d
        out.append(flat[off:off + size].reshape(shape))
        off += size
    return out


def _unshard_last(stacked):
    n, L, K, C = stacked.shape
    return jnp.transpose(stacked, (1, 2, 0, 3)).reshape(L, K, n * C)


def kernel(x, norm1_g, w_in, q_norm_g, k_norm_g, conv_dw_w, conv_dw_b, conv_ln_g, conv_ln_b, w_out, norm2_g, w_up, ffn_dw_w, ffn_dw_b, w_down, loss_target, m_norm1_g, m_w_in, m_q_norm_g, m_k_norm_g, m_conv_dw_w, m_conv_dw_b, m_conv_ln_g, m_conv_ln_b, m_w_out, m_norm2_g, m_w_up, m_ffn_dw_w, m_ffn_dw_b, m_w_down, v_norm1_g, v_w_in, v_q_norm_g, v_k_norm_g, v_conv_dw_w, v_conv_dw_b, v_conv_ln_g, v_conv_ln_b, v_w_out, v_norm2_g, v_w_up, v_ffn_dw_w, v_ffn_dw_b, v_w_down):
    given = dict(locals())
    w = {n: given[n] for n in WEIGHTS}
    m = {n: given["m_" + n] for n in WEIGHTS}
    v = {n: given["v_" + n] for n in WEIGHTS}
    chip = 2 * lax.axis_index("x") + lax.axis_index("y")
    core = lax.axis_index("c")
    chip_arr = jnp.reshape(chip, (1,)).astype(jnp.int32)
    core_arr = jnp.reshape(core, (1,)).astype(jnp.int32)
    L = DEPTH

    place = jnp.concatenate([chip_arr, core_arr])

    full = [{n: _cast_into_full(w[n], l, BIG_KIND[n], chip_arr) for n in BIG} for l in range(L)]
    full[0]["w_in"] = _gather_weights([full[0]["w_in"]], ["col"])[0]
    small_full = {n: _unshard_last(stacked)
                  for n, stacked in zip(SMALL_SHARDED, _gather_small([w[n] for n in SMALL_SHARDED]))}
    params = []
    for l in range(L):
        p = {n: small_full[n][l] for n in SMALL_SHARDED}
        p.update({n: w[n][l][None] for n in REPLICATED})
        params.append(p)

    act = x[0]
    saved = []
    for l in range(L):
        act, s = _layer_fwd(act, params[l], full, l)
        saved.append(s)
    for l in range(L):
        params[l].update({n: full[l][n][0] for n in BIG})
    dx, loss_part = _loss_grad(act, loss_target[0])
    loss = lax.psum(loss_part, ("x", "y", "c"))

    grads = [None] * L
    summed = {}
    pending = []
    for l in reversed(range(L)):
        dx, grads[l], sums = _layer_bwd(dx, saved[l], params[l], (core_arr, place, pending))
        summed.update({(l, n): sums[n] for n in AFTER_ATTENTION})
        if pending:
            summed[l + 1, "w_in"] = sums["pending"][0]
        pending = [grads[l]["w_in"]]
    parts = _chip_partials(pending, ["col"], core_arr)
    summed[0, "w_in"] = _owned_sums(parts, _scatter_partials(parts, ["col"]), ["col"], place)[0]

    grad, delta, new_m, new_v = {}, {}, {}, {}
    for n in BIG:
        out = None
        for l in range(L):
            out = _adamw_layer(l, w[n], summed[l, n], m[n], v[n], out)
        grad[n], delta[n], new_m[n], new_v[n] = out

    small = REPLICATED + SMALL_SHARDED
    small_grads = [jnp.stack([grads[l][n] for l in range(L)]) for n in small]
    small_sums = _unpack(_allreduce_small(_pack(small_grads)), [a.shape for a in small_grads])
    for n, g in zip(small, small_sums):
        if n in REPLICATED:
            grad[n] = g.reshape(w[n].shape)
        else:
            width = w[n].shape[2]
            grad[n] = lax.dynamic_slice_in_dim(g, chip * width, width, axis=2)
    shapes = [w[n].shape for n in small]
    packed = _adamw(*[_pack([src[n] for n in small]) for src in (w, grad, m, v)])
    for out, pk in zip((delta, new_m, new_v), packed):
        out.update(zip(small, _unpack(pk, shapes)))

    return (loss, dx[None], *[grad[n] for n in WEIGHTS], *[delta[n] for n in WEIGHTS],
            *[new_m[n] for n in WEIGHTS], *[new_v[n] for n in WEIGHTS])
```

```python
import jax
import jax.numpy as jnp
from jax import lax
from jax.experimental import pallas as pl
from jax.experimental.pallas import tpu as pltpu

F32 = jnp.float32
BF16 = jnp.bfloat16

DEPTH = 4
D_MODEL = 1024
HEADS = 8
HEAD_DIM = 64
ATTN_WIDTH = HEADS * HEAD_DIM
CONV_WIDTH = D_MODEL - ATTN_WIDTH
CONV_KERNEL = 31
D_FF = 2816
FFN_KERNEL = 3
EPS = 1e-6
ADAM_LR, ADAM_B1, ADAM_B2, ADAM_EPS, ADAM_WD, ADAM_STEP = 0.001, 0.9, 0.999, 1e-08, 0.01, 10

N_CHIPS = 4
N_DEV = 8
LANES = 128
SUBLANES = 8
VMEM_LIMIT_BYTES = 56 * 2**20
ATTN_TILE = 256
ATTN_HEADS_PER_STEP = 4
ATTN_BLOCK = ATTN_HEADS_PER_STEP * HEAD_DIM
CONV_PAD = 32
FFN_PAD = 8
MESH = pl.DeviceIdType.MESH


def _params(*sem):
    return pltpu.CompilerParams(dimension_semantics=sem if sem else None, vmem_limit_bytes=VMEM_LIMIT_BYTES)


class _Guest:
    def __init__(self, name, ins, out_shape, aliases, copies, n_sem):
        self.name, self.ins, self.out_shape, self.aliases, self.copies, self.n_sem = name, ins, out_shape, aliases, copies, n_sem


def _hosted_call(body, guest, *, name, grid, in_specs, out_specs, out_shape, scratch_shapes, operands):
    if guest is None:
        out = pl.pallas_call(body, name=name, grid=grid, in_specs=in_specs, out_specs=out_specs, out_shape=out_shape,
                             scratch_shapes=scratch_shapes, compiler_params=_params("arbitrary"))(*operands)
        return list(out), []
    n_in, n_out, n_scr = len(in_specs), len(out_specs), len(scratch_shapes)
    gi, go = len(guest.ins), len(guest.out_shape)

    def hosting(*refs):
        ins, g_in = refs[:n_in], refs[n_in:n_in + gi]
        outs, g_out = refs[n_in + gi:n_in + gi + n_out], refs[n_in + gi + n_out:n_in + gi + n_out + go]
        scratch, (send, recv) = refs[n_in + gi + n_out + go:-2], refs[-2:]

        @pl.when(pl.program_id(0) == 0)
        def _():
            for cp in guest.copies(g_in, g_out, send, recv):
                cp.start()

        body(*ins, *outs, *scratch)

        @pl.when(pl.program_id(0) == grid[0] - 1)
        def _():
            for cp in guest.copies(g_in, g_out, send, recv):
                cp.wait()

    out = pl.pallas_call(
        hosting, name=name + "_" + guest.name, grid=grid,
        in_specs=list(in_specs) + [ANY] * gi, out_specs=list(out_specs) + [ANY] * go,
        out_shape=list(out_shape) + list(guest.out_shape),
        scratch_shapes=list(scratch_shapes) + [pltpu.SemaphoreType.DMA((guest.n_sem,))] * 2,
        input_output_aliases={n_in + a: n_out + b for a, b in guest.aliases.items()},
        compiler_params=pltpu.CompilerParams(dimension_semantics=("arbitrary",), vmem_limit_bytes=VMEM_LIMIT_BYTES,
                                             has_side_effects=True),
    )(*operands, *guest.ins)
    return list(out[:n_out]), list(out[n_out:])


def _dot(a, b):
    return jnp.dot(a, b, preferred_element_type=F32)


def _dot_nt(a, b):
    return lax.dot_general(a, b, (((1,), (1,)), ((), ())), preferred_element_type=F32)


def _dot_tn(a, b):
    return lax.dot_general(a, b, (((0,), (0,)), ((), ())), preferred_element_type=F32)


def _sigmoid(x):
    return 1.0 / (1.0 + jnp.exp(-x))


def _norm_matmul(x, g, w, guest=None, *, out_dtype=F32, tm=256):
    S, D = x.shape
    N = w.shape[1]

    def body(x_ref, g_ref, w_ref, ht_ref, y_ref):
        xv = x_ref[...]
        h = xv * lax.rsqrt(jnp.mean(xv * xv, axis=-1, keepdims=True) + EPS) * g_ref[...]
        ht_ref[...] = h.T.astype(BF16)
        y_ref[...] = _dot(h.astype(BF16), w_ref[...]).astype(out_dtype)

    (ht, y), guest_out = _hosted_call(
        body, guest, name="norm_matmul", grid=(S // tm,),
        in_specs=[pl.BlockSpec((tm, D), lambda i: (i, 0)),
                  pl.BlockSpec((1, D), lambda i: (0, 0)),
                  pl.BlockSpec((D, N), lambda i: (0, 0))],
        out_specs=[pl.BlockSpec((D, tm), lambda i: (0, i)),
                   pl.BlockSpec((tm, N), lambda i: (i, 0))],
        out_shape=[jax.ShapeDtypeStruct((D, S), BF16), jax.ShapeDtypeStruct((S, N), out_dtype)],
        scratch_shapes=[], operands=[x, g, w])
    return ht, y, guest_out


def _matmul_res(pieces, w, res, guest=None, *, tm=512):
    S, N = res.shape
    K = w.shape[0]
    widths = [p.shape[1] for p in pieces]
    assert sum(widths) == K

    def body(*refs):
        p_refs, (w_ref, res_ref, o_ref) = refs[:len(pieces)], refs[len(pieces):]
        acc = res_ref[...]
        off = 0
        for p_ref, kp in zip(p_refs, widths):
            acc = acc + _dot(p_ref[...], w_ref[off:off + kp, :])
            off += kp
        o_ref[...] = acc

    (out,), guest_out = _hosted_call(
        body, guest, name="matmul_res", grid=(S // tm,),
        in_specs=[pl.BlockSpec((tm, kp), lambda i: (i, 0)) for kp in widths]
        + [pl.BlockSpec((K, N), lambda i: (0, 0)), pl.BlockSpec((tm, N), lambda i: (i, 0))],
        out_specs=[pl.BlockSpec((tm, N), lambda i: (i, 0))],
        out_shape=[jax.ShapeDtypeStruct((S, N), F32)],
        scratch_shapes=[], operands=[*pieces, w, res])
    return out, guest_out


def _nt_sum(p_refs, widths, w_ref):
    acc = None
    off = 0
    for p_ref, n_p in zip(p_refs, widths):
        d = _dot_nt(p_ref[...].astype(BF16), w_ref[:, off:off + n_p])
        acc = d if acc is None else acc + d
        off += n_p
    return acc


def _matmul_nt(pieces, w, out_dtype, guest=None, *, tm=512):
    S = pieces[0].shape[0]
    K, N = w.shape
    widths = [p.shape[1] for p in pieces]
    assert sum(widths) == N

    def body(*refs):
        p_refs, (w_ref, o_ref) = refs[:len(pieces)], refs[len(pieces):]
        o_ref[...] = _nt_sum(p_refs, widths, w_ref).astype(out_dtype)

    (out,), guest_out = _hosted_call(
        body, guest, name="matmul_nt", grid=(S // tm,),
        in_specs=[pl.BlockSpec((tm, n_p), lambda i: (i, 0)) for n_p in widths]
        + [pl.BlockSpec((K, N), lambda i: (0, 0))],
        out_specs=[pl.BlockSpec((tm, K), lambda i: (i, 0))],
        out_shape=[jax.ShapeDtypeStruct((S, K), out_dtype)],
        scratch_shapes=[], operands=[*pieces, w])
    return out, guest_out


def _matmul_nt_rmsbwd(pieces, w, x, g, dres, guest=None, *, tm=256):
    S, K = x.shape
    N = w.shape[1]
    widths = [p.shape[1] for p in pieces]
    assert sum(widths) == N

    def body(*refs):
        p_refs, (w_ref, x_ref, g_ref, dres_ref, dx_ref, dg_ref) = refs[:len(pieces)], refs[len(pieces):]
        dh = _nt_sum(p_refs, widths, w_ref)
        xv = x_ref[...]
        r = lax.rsqrt(jnp.mean(xv * xv, axis=-1, keepdims=True) + EPS)
        xh = xv * r
        dxh = dh * g_ref[...]
        dx_ref[...] = dres_ref[...] + r * (dxh - xh * jnp.mean(dxh * xh, axis=-1, keepdims=True))

        @pl.when(pl.program_id(0) == 0)
        def _():
            dg_ref[...] = jnp.zeros_like(dg_ref)

        dg_ref[...] += jnp.sum(dh * xh, axis=0, keepdims=True)

    (dx, dg), guest_out = _hosted_call(
        body, guest, name="matmul_nt_rmsbwd", grid=(S // tm,),
        in_specs=[pl.BlockSpec((tm, n_p), lambda i: (i, 0)) for n_p in widths]
        + [pl.BlockSpec((K, N), lambda i: (0, 0)), pl.BlockSpec((tm, K), lambda i: (i, 0)),
           pl.BlockSpec((1, K), lambda i: (0, 0)), pl.BlockSpec((tm, K), lambda i: (i, 0))],
        out_specs=[pl.BlockSpec((tm, K), lambda i: (i, 0)), pl.BlockSpec((1, K), lambda i: (0, 0))],
        out_shape=[jax.ShapeDtypeStruct((S, K), F32), jax.ShapeDtypeStruct((1, K), F32)],
        scratch_shapes=[], operands=[*pieces, w, x, g, dres])
    return dx, dg, guest_out


def _matmul_tn(xts, dys, *, tk, tn, ts=1024):
    S = dys[0].shape[0]
    xs = xts
    n_s = S // ts
    (mt,) = {x.shape[0] // tk for x in xts}
    (nt,) = {d.shape[1] // tn for d in dys}

    def body(*refs):
        x_refs, dy_refs, (o_ref, acc_ref) = refs[:len(xs)], refs[len(xs):len(xs) + len(dys)], refs[len(xs) + len(dys):]
        i, j, s = pl.program_id(0), pl.program_id(1), pl.program_id(2)

        @pl.when(s == 0)
        def _():
            acc_ref[...] = jnp.zeros_like(acc_ref)

        for a, x_ref in enumerate(x_refs):
            for b, dy_ref in enumerate(dy_refs):
                @pl.when((i // mt == a) & (j // nt == b))
                def _():
                    acc_ref[...] += _dot(x_ref[...], dy_ref[...].astype(BF16))

        @pl.when(s == n_s - 1)
        def _():
            o_ref[...] = acc_ref[...].astype(BF16)

    def x_map(a):
        return lambda i, j, s: (jnp.where(i // mt == a, i % mt, 0), jnp.where(i // mt == a, s, 0))

    def dy_map(b):
        return lambda i, j, s: (jnp.where(j // nt == b, s, 0), jnp.where(j // nt == b, j % nt, 0))

    return pl.pallas_call(
        body, name="matmul_tn", grid=(len(xs) * mt, len(dys) * nt, n_s),
        in_specs=[pl.BlockSpec((tk, ts), x_map(a)) for a in range(len(xs))]
        + [pl.BlockSpec((ts, tn), dy_map(b)) for b in range(len(dys))],
        out_specs=pl.BlockSpec((tk, tn), lambda i, j, s: (i, j)),
        out_shape=jax.ShapeDtypeStruct((len(xs) * mt * tk, len(dys) * nt * tn), BF16),
        scratch_shapes=[pltpu.VMEM((tk, tn), F32)],
        compiler_params=_params("parallel", "parallel", "arbitrary"),
    )(*xs, *dys)


def _tri_consts():
    j = jnp.arange(ATTN_TILE)[:, None]
    s = jnp.arange(ATTN_TILE)[None, :]
    return (j > s).astype(BF16), (j <= s).astype(BF16), (j < s).astype(BF16)


SIGN_BIT = 0x80000000
WEIGHT_IS_ZERO = -104.0


def _log_terms(sn):
    minus_abs = lax.bitcast_convert_type(lax.bitcast_convert_type(sn, jnp.uint32) | jnp.uint32(SIGN_BIT), F32)
    lom = jnp.minimum(sn, 0.0) - jnp.log(1.0 + jnp.exp(minus_abs))
    return lom, lom - sn


def _causal_mask():
    t = lax.broadcasted_iota(jnp.int32, (ATTN_TILE, ATTN_TILE), 0)
    s = lax.broadcasted_iota(jnp.int32, (ATTN_TILE, ATTN_TILE), 1)
    return s < t


def _attn_prep(h, q_ref, k_ref, v_ref, qg_ref, kg_ref, qn_s, kn_s, vb_s, n_tiles):
    T = ATTN_TILE
    lanes = slice(HEAD_DIM * h, HEAD_DIM * (h + 1))
    scale = -(HEAD_DIM ** -0.5)

    def prep(i, carry):
        rows = pl.ds(pl.multiple_of(i * T, T), T)
        q = q_ref[rows, lanes]
        k = k_ref[rows, lanes]
        rq = lax.rsqrt(jnp.mean(q * q, axis=-1, keepdims=True) + EPS)
        rk = lax.rsqrt(jnp.mean(k * k, axis=-1, keepdims=True) + EPS)
        qn_s[h, rows, :] = (q * rq * qg_ref[...] * scale).astype(BF16)
        kn_s[h, rows, :] = (k * rk * kg_ref[...]).astype(BF16)
        vb_s[h, rows, :] = v_ref[rows, lanes].astype(BF16)
        return carry

    lax.fori_loop(0, n_tiles, prep, 0)


def _attn_fwd(proj, qg, kg, fulls=(), kinds=()):
    S = proj.shape[0]
    n_comm = len(fulls)
    T = ATTN_TILE
    n_tiles = S // T
    suffix, _, _ = _tri_consts()
    pairs = HEADS // ATTN_HEADS_PER_STEP
    q_blk, k_blk, v_blk = 0, ATTN_WIDTH // ATTN_BLOCK, 2 * ATTN_WIDTH // ATTN_BLOCK

    def body(*refs):
        q_ref, k_ref, v_ref, qg_ref, kg_ref, tri_ref = refs[:6]
        o_ref, lt_ref, ot_ref, start_ref, qn_s, kn_s, vb_s = refs[6 + n_comm:13 + n_comm]
        w_refs = refs[13 + n_comm:13 + 2 * n_comm]
        sems = refs[13 + 2 * n_comm:]
        step_id = pl.program_id(0)
        if n_comm:
            @pl.when(pl.program_id(0) == 0)
            def _():
                for cp in _gather_copies(w_refs, kinds, *sems, 0):
                    cp.start()

        heads = range(ATTN_HEADS_PER_STEP)
        for h in heads:
            _attn_prep(h, q_ref, k_ref, v_ref, qg_ref, kg_ref, qn_s, kn_s, vb_s, n_tiles)

        def q_tile(qi, carry0):
            qrows = pl.ds(pl.multiple_of(qi * T, T), T)
            qt = [qn_s[h, qrows, :] for h in heads]

            def tile(kj, carry, diag):
                krows = pl.ds(pl.multiple_of(kj * T, T), T)
                s = [_dot_nt(qt[h], kn_s[h, krows, :]) for h in heads]
                terms = [_log_terms(s[h]) for h in heads]
                lom = [terms[h][0] for h in heads]
                if diag:
                    mask = _causal_mask()
                    lom = [jnp.where(mask, lom[h], 0.0) for h in heads]
                lom = [lom[h].astype(BF16) for h in heads]
                tail = [_dot(lom[h], tri_ref[...]) for h in heads]
                w = [jnp.exp(terms[h][1] + tail[h] + carry[h][1]) for h in heads]
                if diag:
                    w = [jnp.where(mask, w[h], 0.0) for h in heads]
                return tuple((carry[h][0] + _dot(w[h].astype(BF16), vb_s[h, krows, :]),
                              carry[h][1] + (tail[h][:, 0:1] + lom[h][:, 0:1].astype(F32))) for h in heads)

            init = tuple((jnp.zeros((T, HEAD_DIM), F32), jnp.zeros((T, 1), F32)) for h in heads)
            def alive(cr):
                worst = cr[0][1]
                for h in heads[1:]:
                    worst = jnp.maximum(worst, cr[h][1])
                return jnp.max(worst) >= WEIGHT_IS_ZERO

            def step(state):
                t, _, cr = state
                cr = tile(qi - 1 - t, cr, False)
                return t + 1, alive(cr), cr

            first = tile(qi, init, True)
            swept, _, carry = lax.while_loop(lambda st: (st[0] < qi) & st[1], step, (jnp.int32(0), alive(first), first))
            start_ref[step_id, qi] = (qi - swept).astype(F32)
            o = jnp.concatenate([carry[h][0] for h in heads], axis=1)
            o_ref[qrows, :] = o.astype(BF16)
            ot_ref[:, qrows] = o.T.astype(BF16)
            for h in heads:
                lt_ref[qrows, HEAD_DIM * h:HEAD_DIM * (h + 1)] = jnp.broadcast_to(carry[h][1], (T, HEAD_DIM))
            return carry0

        lax.fori_loop(0, n_tiles, q_tile, 0)
        if n_comm:
            @pl.when(pl.program_id(0) == pairs - 1)
            def _():
                for cp in _gather_copies(w_refs, kinds, *sems, 0):
                    cp.wait()

    n_sem = n_comm * (N_CHIPS - 1)
    per_head = pl.BlockSpec((ATTN_HEADS_PER_STEP, S, HEAD_DIM), lambda p: (p, 0, 0))
    out = pl.pallas_call(
        body, name="attn_fwd_gather" if n_comm else "attn_fwd", grid=(pairs,),
        in_specs=[pl.BlockSpec((S, ATTN_BLOCK), lambda p: (0, q_blk + p)),
                  pl.BlockSpec((S, ATTN_BLOCK), lambda p: (0, k_blk + p)),
                  pl.BlockSpec((S, ATTN_BLOCK), lambda p: (0, v_blk + p)),
                  pl.BlockSpec((1, HEAD_DIM), lambda p: (0, 0)),
                  pl.BlockSpec((1, HEAD_DIM), lambda p: (0, 0)),
                  pl.BlockSpec((T, T), lambda p: (0, 0))] + [ANY] * n_comm,
        out_specs=[pl.BlockSpec((S, ATTN_BLOCK), lambda p: (0, p)),
                   pl.BlockSpec((None, S, ATTN_BLOCK), lambda p: (p, 0, 0)),
                   pl.BlockSpec((ATTN_BLOCK, S), lambda p: (p, 0)),
                   pl.BlockSpec(memory_space=pltpu.SMEM)] + [per_head] * 3 + [ANY] * n_comm,
        out_shape=[jax.ShapeDtypeStruct((S, ATTN_WIDTH), BF16),
                   jax.ShapeDtypeStruct((pairs, S, ATTN_BLOCK), F32),
                   jax.ShapeDtypeStruct((ATTN_WIDTH, S), BF16),
                   jax.ShapeDtypeStruct((pairs, n_tiles), F32)] + [jax.ShapeDtypeStruct((HEADS, S, HEAD_DIM), BF16)] * 3
        + [jax.ShapeDtypeStruct(f.shape, f.dtype) for f in fulls],
        scratch_shapes=[pltpu.SemaphoreType.DMA((n_sem,))] * 2 if n_comm else [],
        input_output_aliases={6 + a: 7 + a for a in range(n_comm)},
        compiler_params=pltpu.CompilerParams(dimension_semantics=("arbitrary",), vmem_limit_bytes=VMEM_LIMIT_BYTES,
                                             has_side_effects=bool(n_comm)),
    )(proj, proj, proj, qg, kg, suffix, *fulls)
    return out[0], (out[1], out[3], out[4], out[5], out[6]), out[2], list(out[7:])


def _attn_bwd(proj, qg, kg, ltot, dmix, parts=(), kinds=()):
    S = proj.shape[0]
    n_comm = len(parts)
    T = ATTN_TILE
    n_tiles = S // T
    _, prefix_incl, prefix_excl = _tri_consts()
    pairs = HEADS // ATTN_HEADS_PER_STEP
    q_blk, k_blk, v_blk = 0, ATTN_WIDTH // ATTN_BLOCK, 2 * ATTN_WIDTH // ATTN_BLOCK
    scale = HEAD_DIM ** -0.5

    def body(*refs):
        q_ref, k_ref, qg_ref, kg_ref, lt_ref, do_ref, ti_ref, te_ref, start_ref, qn_s, kn_s, vb_s = refs[:12]
        p_refs = refs[12:12 + n_comm]
        dq_ref, dk_ref, dv_ref, dqg_ref, dkg_ref = refs[12 + n_comm:17 + n_comm]
        lb_refs = refs[17 + n_comm:17 + 2 * n_comm]
        dq_s, dk_s, dv_s = refs[17 + 2 * n_comm:20 + 2 * n_comm]
        sems = refs[20 + 2 * n_comm:]
        step_id = pl.program_id(0)

        @pl.when(pl.program_id(0) == 0)
        def _():
            dqg_ref[...] = jnp.zeros_like(dqg_ref)
            dkg_ref[...] = jnp.zeros_like(dkg_ref)
            for cp in _scatter_copies(p_refs, lb_refs, kinds, *sems) if n_comm else ():
                cp.start()

        heads = range(ATTN_HEADS_PER_STEP)
        dk_s[...] = jnp.zeros_like(dk_s)
        dv_s[...] = jnp.zeros_like(dv_s)

        def q_tile(qi, carry0):
            qrows = pl.ds(pl.multiple_of(qi * T, T), T)
            qt = [qn_s[h, qrows, :] for h in heads]
            dob = [do_ref[qrows, HEAD_DIM * h:HEAD_DIM * (h + 1)].astype(BF16) for h in heads]
            lt = [lt_ref[qrows, HEAD_DIM * h:HEAD_DIM * h + 1] for h in heads]

            def tile(kj, carry, diag):
                krows = pl.ds(pl.multiple_of(kj * T, T), T)
                kt = [kn_s[h, krows, :] for h in heads]
                s = [_dot_nt(qt[h], kt[h]) for h in heads]
                dw = [_dot_nt(dob[h], vb_s[h, krows, :]) for h in heads]
                terms = [_log_terms(s[h]) for h in heads]
                lom = [terms[h][0] for h in heads]
                if diag:
                    mask = _causal_mask()
                    lom = [jnp.where(mask, lom[h], 0.0) for h in heads]
                pin = [_dot(lom[h].astype(BF16), ti_ref[...]) for h in heads]
                w = [jnp.exp(terms[h][1] + ((lt[h] - carry[h][1]) - pin[h])) for h in heads]
                if diag:
                    w = [jnp.where(mask, w[h], 0.0) for h in heads]
                e = [w[h] * dw[h] for h in heads]
                gex = [_dot(e[h].astype(BF16), te_ref[...]) for h in heads]
                dz = [e[h] - jnp.exp(terms[h][1]) * (e[h] + (carry[h][2] + gex[h])) for h in heads]
                if diag:
                    dz = [jnp.where(mask, dz[h], 0.0) for h in heads]
                new = []
                for h in heads:
                    dzb = dz[h].astype(BF16)
                    dk_s[h, krows, :] += _dot_tn(dzb, qt[h])
                    dv_s[h, krows, :] += _dot_tn(w[h].astype(BF16), dob[h])
                    new.append((carry[h][0] + _dot(dzb, kt[h]),
                                carry[h][1] + pin[h][:, T - 1:T],
                                carry[h][2] + gex[h][:, T - 1:T] + e[h][:, T - 1:T]))
                return tuple(new)

            zero = jnp.zeros((T, 1), F32)
            init = tuple((jnp.zeros((T, HEAD_DIM), F32), zero, zero) for h in heads)
            first = jnp.clip(start_ref[step_id, qi].astype(jnp.int32), 0, qi)
            last = tile(qi, lax.fori_loop(first, qi, lambda kj, cr: tile(kj, cr, False), init), True)
            for h in heads:
                dq_s[h, qrows, :] = last[h][0]
            return carry0

        lax.fori_loop(0, n_tiles, q_tile, 0)

        def finish(i, carry):
            rows = pl.ds(pl.multiple_of(i * T, T), T)
            new = []
            for h in heads:
                lanes = slice(HEAD_DIM * h, HEAD_DIM * (h + 1))
                q = q_ref[rows, lanes]
                k = k_ref[rows, lanes]
                rq = lax.rsqrt(jnp.mean(q * q, axis=-1, keepdims=True) + EPS)
                rk = lax.rsqrt(jnp.mean(k * k, axis=-1, keepdims=True) + EPS)
                qh = q * rq
                kh = k * rk
                dqn = dq_s[h, rows, :] * scale
                dkn = -dk_s[h, rows, :]
                dqh = dqn * qg_ref[...]
                dkh = dkn * kg_ref[...]
                dq_ref[rows, lanes] = (rq * (dqh - qh * jnp.mean(dqh * qh, axis=-1, keepdims=True))).astype(BF16)
                dk_ref[rows, lanes] = (rk * (dkh - kh * jnp.mean(dkh * kh, axis=-1, keepdims=True))).astype(BF16)
                dv_ref[rows, lanes] = dv_s[h, rows, :].astype(BF16)
                new.append(carry[2 * h] + jnp.sum(dqn * qh, axis=0, keepdims=True))
                new.append(carry[2 * h + 1] + jnp.sum(dkn * kh, axis=0, keepdims=True))
            return tuple(new)

        zero = jnp.zeros((1, HEAD_DIM), F32)
        sums = lax.fori_loop(0, n_tiles, finish, (zero,) * (2 * len(heads)))
        dqg_ref[...] += sum(sums[0::2])
        dkg_ref[...] += sum(sums[1::2])
        if n_comm:
            @pl.when(pl.program_id(0) == pairs - 1)
            def _():
                for cp in _scatter_copies(p_refs, lb_refs, kinds, *sems):
                    cp.wait()

    blk = lambda off: pl.BlockSpec((S, ATTN_BLOCK), lambda p: (0, off + p))
    row64 = pl.BlockSpec((1, HEAD_DIM), lambda p: (0, 0))
    tri = pl.BlockSpec((T, T), lambda p: (0, 0))
    n_sem = n_comm * (N_CHIPS - 1)
    out = pl.pallas_call(
        body, name="attn_bwd_scatter" if n_comm else "attn_bwd", grid=(pairs,),
        in_specs=[blk(q_blk), blk(k_blk), row64, row64,
                  pl.BlockSpec((None, S, ATTN_BLOCK), lambda p: (p, 0, 0)), blk(0), tri, tri,
                  pl.BlockSpec(memory_space=pltpu.SMEM)]
        + [pl.BlockSpec((ATTN_HEADS_PER_STEP, S, HEAD_DIM), lambda p: (p, 0, 0))] * 3 + [ANY] * n_comm,
        out_specs=[blk(0), blk(0), blk(0), row64, row64] + [ANY] * n_comm,
        out_shape=[jax.ShapeDtypeStruct((S, ATTN_WIDTH), BF16)] * 3 + [jax.ShapeDtypeStruct((1, HEAD_DIM), F32)] * 2
        + _scatter_shapes(parts, kinds),
        scratch_shapes=[pltpu.VMEM((ATTN_HEADS_PER_STEP, S, HEAD_DIM), F32)] * 3
        + ([pltpu.SemaphoreType.DMA((n_sem,))] * 2 if n_comm else []),
        compiler_params=pltpu.CompilerParams(dimension_semantics=("arbitrary",), vmem_limit_bytes=VMEM_LIMIT_BYTES,
                                             has_side_effects=bool(n_comm)),
    )(proj, proj, qg, kg, ltot[0], dmix, prefix_incl, prefix_excl, *ltot[1:], *parts)
    return out[:5], list(out[5:])


def _shifted(win, n_rows):
    return [win if b == 0 else pltpu.roll(win, n_rows - b, 0) for b in range(SUBLANES)]


def _taps(variants, offsets, tm):
    return {o: variants[o % SUBLANES][(o // SUBLANES) * SUBLANES:(o // SUBLANES) * SUBLANES + tm, :] for o in offsets}


def _fold_rows(a):
    return jnp.sum(a.reshape(a.shape[0] // SUBLANES, SUBLANES, a.shape[1]), axis=0)


def _glu_conv_fwd(proj, w, bias, guest=None, *, tm=256):
    S = proj.shape[0]
    CB = LANES
    a_blk, b_blk = 3 * ATTN_WIDTH // CB, (3 * ATTN_WIDTH + CONV_WIDTH) // CB
    n_rows = tm + CONV_PAD

    def body(a_ref, b_ref, w_ref, bias_ref, c1_ref, pad_s):
        pad_s[0:CONV_PAD, :] = jnp.zeros((CONV_PAD, CB), F32)

        def fill(i, carry):
            rows = pl.ds(pl.multiple_of(i * tm, tm), tm)
            pad_s[pl.ds(pl.multiple_of(CONV_PAD + i * tm, SUBLANES), tm), :] = a_ref[rows, :] * _sigmoid(b_ref[rows, :])
            return carry

        lax.fori_loop(0, S // tm, fill, 0)

        def conv(i, carry):
            r0 = pl.multiple_of(i * tm, tm)
            taps = _taps(_shifted(pad_s[pl.ds(r0, n_rows), :], n_rows), range(2, 2 + CONV_KERNEL), tm)
            acc = jnp.broadcast_to(bias_ref[...], (tm, CB))
            for k in range(CONV_KERNEL):
                acc = acc + w_ref[k:k + 1, :] * taps[k + 2]
            c1_ref[pl.ds(r0, tm), :] = acc
            return carry

        lax.fori_loop(0, S // tm, conv, 0)

    (c1,), guest_out = _hosted_call(
        body, guest, name="glu_conv_fwd", grid=(CONV_WIDTH // CB,),
        in_specs=[pl.BlockSpec((S, CB), lambda j: (0, a_blk + j)), pl.BlockSpec((S, CB), lambda j: (0, b_blk + j)),
                  pl.BlockSpec((CONV_KERNEL, CB), lambda j: (0, j)), pl.BlockSpec((1, CB), lambda j: (0, j))],
        out_specs=[pl.BlockSpec((S, CB), lambda j: (0, j))],
        out_shape=[jax.ShapeDtypeStruct((S, CONV_WIDTH), F32)],
        scratch_shapes=[pltpu.VMEM((S + CONV_PAD, CB), F32)], operands=[proj, proj, w, bias])
    return c1, guest_out


def _glu_conv_bwd(proj, w, dc1, guest=None, *, tm=256):
    S = proj.shape[0]
    CB = LANES
    a_blk, b_blk = 3 * ATTN_WIDTH // CB, (3 * ATTN_WIDTH + CONV_WIDTH) // CB
    n_rows = tm + CONV_PAD

    def body(a_ref, b_ref, w_ref, dc1_ref, da_ref, db_ref, dw_ref, dbias_ref, pad_s, dpad_s, dw_s):
        pad_s[0:CONV_PAD, :] = jnp.zeros((CONV_PAD, CB), F32)
        dpad_s[S:S + CONV_PAD, :] = jnp.zeros((CONV_PAD, CB), F32)
        dw_s[...] = jnp.zeros_like(dw_s)

        def fill(i, carry):
            rows = pl.ds(pl.multiple_of(i * tm, tm), tm)
            pad_s[pl.ds(pl.multiple_of(CONV_PAD + i * tm, SUBLANES), tm), :] = a_ref[rows, :] * _sigmoid(b_ref[rows, :])
            dpad_s[rows, :] = dc1_ref[rows, :]
            return carry

        lax.fori_loop(0, S // tm, fill, 0)

        def conv(i, carry):
            r0 = pl.multiple_of(i * tm, tm)
            rows = pl.ds(r0, tm)
            taps = _taps(_shifted(dpad_s[pl.ds(r0, n_rows), :], n_rows), range(CONV_KERNEL), tm)
            acc = jnp.zeros((tm, CB), F32)
            for k in range(CONV_KERNEL):
                acc = acc + w_ref[k:k + 1, :] * taps[CONV_KERNEL - 1 - k]
            a = a_ref[rows, :]
            sg = _sigmoid(b_ref[rows, :])
            da_ref[rows, :] = (acc * sg).astype(BF16)
            db_ref[rows, :] = (acc * a * sg * (1.0 - sg)).astype(BF16)
            d = taps[0]
            taps = _taps(_shifted(pad_s[pl.ds(r0, n_rows), :], n_rows), range(2, 2 + CONV_KERNEL), tm)
            for k in range(CONV_KERNEL):
                dw_s[SUBLANES * k:SUBLANES * (k + 1), :] += _fold_rows(d * taps[k + 2])
            dw_s[SUBLANES * CONV_KERNEL:SUBLANES * (CONV_KERNEL + 1), :] += _fold_rows(d)
            return carry

        lax.fori_loop(0, S // tm, conv, 0)
        for k in range(CONV_KERNEL):
            dw_ref[k:k + 1, :] = jnp.sum(dw_s[SUBLANES * k:SUBLANES * (k + 1), :], axis=0, keepdims=True)
        dbias_ref[...] = jnp.sum(dw_s[SUBLANES * CONV_KERNEL:SUBLANES * (CONV_KERNEL + 1), :], axis=0, keepdims=True)

    col = lambda off: pl.BlockSpec((S, CB), lambda j: (0, off + j))
    out, guest_out = _hosted_call(
        body, guest, name="glu_conv_bwd", grid=(CONV_WIDTH // CB,),
        in_specs=[col(a_blk), col(b_blk), pl.BlockSpec((CONV_KERNEL, CB), lambda j: (0, j)), col(0)],
        out_specs=[col(0), col(0), pl.BlockSpec((CONV_KERNEL, CB), lambda j: (0, j)), pl.BlockSpec((1, CB), lambda j: (0, j))],
        out_shape=[jax.ShapeDtypeStruct((S, CONV_WIDTH), BF16)] * 2
        + [jax.ShapeDtypeStruct((CONV_KERNEL, CONV_WIDTH), F32), jax.ShapeDtypeStruct((1, CONV_WIDTH), F32)],
        scratch_shapes=[pltpu.VMEM((S + CONV_PAD, CB), F32), pltpu.VMEM((S + CONV_PAD, CB), F32),
                        pltpu.VMEM((SUBLANES * (CONV_KERNEL + 1), CB), F32)], operands=[proj, proj, w, dc1])
    return (*out, guest_out)


def _ln_stats(c1):
    mu = jnp.mean(c1, axis=-1, keepdims=True)
    xc = c1 - mu
    r = lax.rsqrt(jnp.mean(xc * xc, axis=-1, keepdims=True) + EPS)
    return xc * r, r


def _ln_silu_fwd(c1, g, b, *, tm=512):
    S, C = c1.shape

    def body(c1_ref, g_ref, b_ref, c_ref, ct_ref):
        yh, _ = _ln_stats(c1_ref[...])
        y = yh * g_ref[...] + b_ref[...]
        c = y * _sigmoid(y)
        c_ref[...] = c.astype(BF16)
        ct_ref[...] = c.T.astype(BF16)

    vec = pl.BlockSpec((1, C), lambda i: (0, 0))
    return pl.pallas_call(
        body, name="ln_silu_fwd", grid=(S // tm,),
        in_specs=[pl.BlockSpec((tm, C), lambda i: (i, 0)), vec, vec],
        out_specs=[pl.BlockSpec((tm, C), lambda i: (i, 0)), pl.BlockSpec((C, tm), lambda i: (0, i))],
        out_shape=[jax.ShapeDtypeStruct((S, C), BF16), jax.ShapeDtypeStruct((C, S), BF16)],
        compiler_params=_params("parallel"),
    )(c1, g, b)


def _ln_silu_bwd(c1, g, b, dmix, *, tm=512):
    S, C = c1.shape

    def body(c1_ref, g_ref, b_ref, dc_ref, dc1_ref, dg_ref, db_ref):
        yh, r = _ln_stats(c1_ref[...])
        y = yh * g_ref[...] + b_ref[...]
        sg = _sigmoid(y)
        dy = dc_ref[...] * (sg * (1.0 + y * (1.0 - sg)))
        dyh = dy * g_ref[...]
        dc1_ref[...] = r * (dyh - jnp.mean(dyh, axis=-1, keepdims=True)
                            - yh * jnp.mean(dyh * yh, axis=-1, keepdims=True))

        @pl.when(pl.program_id(0) == 0)
        def _():
            dg_ref[...] = jnp.zeros_like(dg_ref)
            db_ref[...] = jnp.zeros_like(db_ref)

        dg_ref[...] += jnp.sum(dy * yh, axis=0, keepdims=True)
        db_ref[...] += jnp.sum(dy, axis=0, keepdims=True)

    vec = pl.BlockSpec((1, C), lambda i: (0, 0))
    return pl.pallas_call(
        body, name="ln_silu_bwd", grid=(S // tm,),
        in_specs=[pl.BlockSpec((tm, C), lambda i: (i, 0)), vec, vec, pl.BlockSpec((tm, C), lambda i: (i, 1))],
        out_specs=[pl.BlockSpec((tm, C), lambda i: (i, 0)), vec, vec],
        out_shape=[jax.ShapeDtypeStruct((S, C), F32), jax.ShapeDtypeStruct((1, C), F32), jax.ShapeDtypeStruct((1, C), F32)],
        compiler_params=_params("arbitrary"),
    )(c1, g, b, dmix)


FFN_CB = 256


def _ffn_gate(pad_s, w_ref, bias_ref, r0, tm):
    n_rows = tm + FFN_PAD
    taps = _taps(_shifted(pad_s[pl.ds(r0, n_rows), :], n_rows), range(FFN_PAD - 2, FFN_PAD + 1), tm)
    g1 = bias_ref[...] + w_ref[0:1, :] * taps[6] + w_ref[1:2, :] * taps[7] + w_ref[2:3, :] * taps[8]
    return g1, taps


def _ffn_act_fwd(u, w, bias, guest=None, *, tm=256):
    S = u.shape[0]
    CB = FFN_CB
    nb = D_FF // CB

    def body(g_ref, v_ref, w_ref, bias_ref, o_ref, ot_ref, pad_s):
        pad_s[0:FFN_PAD, :] = jnp.zeros((FFN_PAD, CB), F32)

        def fill(i, carry):
            pad_s[pl.ds(pl.multiple_of(FFN_PAD + i * tm, SUBLANES), tm), :] = g_ref[pl.ds(pl.multiple_of(i * tm, tm), tm), :].astype(F32)
            return carry

        lax.fori_loop(0, S // tm, fill, 0)

        def act(i, carry):
            r0 = pl.multiple_of(i * tm, tm)
            g1, _ = _ffn_gate(pad_s, w_ref, bias_ref, r0, tm)
            a = g1 * _sigmoid(g1) * v_ref[pl.ds(r0, tm), :].astype(F32)
            o_ref[pl.ds(r0, tm), :] = a.astype(BF16)
            ot_ref[:, pl.ds(r0, tm)] = a.T.astype(BF16)
            return carry

        lax.fori_loop(0, S // tm, act, 0)

    (act, actt), guest_out = _hosted_call(
        body, guest, name="ffn_act_fwd", grid=(nb,),
        in_specs=[pl.BlockSpec((S, CB), lambda j: (0, j)), pl.BlockSpec((S, CB), lambda j: (0, nb + j)),
                  pl.BlockSpec((FFN_KERNEL, CB), lambda j: (0, j)), pl.BlockSpec((1, CB), lambda j: (0, j))],
        out_specs=[pl.BlockSpec((S, CB), lambda j: (0, j)), pl.BlockSpec((CB, S), lambda j: (j, 0))],
        out_shape=[jax.ShapeDtypeStruct((S, D_FF), BF16), jax.ShapeDtypeStruct((D_FF, S), BF16)],
        scratch_shapes=[pltpu.VMEM((S + FFN_PAD, CB), F32)], operands=[u, u, w, bias])
    return act, actt, guest_out


def _ffn_act_bwd(u, w, bias, dact, guest=None, *, tm=256):
    S = u.shape[0]
    CB = FFN_CB
    nb = D_FF // CB

    def body(g_ref, v_ref, w_ref, bias_ref, da_ref, dg_ref, dv_ref, dw_ref, dbias_ref, pad_s, dpad_s, dw_s):
        pad_s[0:FFN_PAD, :] = jnp.zeros((FFN_PAD, CB), F32)
        dpad_s[S:S + FFN_PAD, :] = jnp.zeros((FFN_PAD, CB), F32)
        dw_s[...] = jnp.zeros_like(dw_s)

        def fill(i, carry):
            pad_s[pl.ds(pl.multiple_of(FFN_PAD + i * tm, SUBLANES), tm), :] = g_ref[pl.ds(pl.multiple_of(i * tm, tm), tm), :].astype(F32)
            return carry

        lax.fori_loop(0, S // tm, fill, 0)

        def first(i, carry):
            r0 = pl.multiple_of(i * tm, tm)
            rows = pl.ds(r0, tm)
            g1, taps = _ffn_gate(pad_s, w_ref, bias_ref, r0, tm)
            sg = _sigmoid(g1)
            da = da_ref[rows, :].astype(F32)
            dv_ref[rows, :] = (da * g1 * sg).astype(BF16)
            dg1 = da * v_ref[rows, :].astype(F32) * (sg * (1.0 + g1 * (1.0 - sg)))
            dpad_s[rows, :] = dg1
            for k in range(FFN_KERNEL):
                dw_s[SUBLANES * k:SUBLANES * (k + 1), :] += _fold_rows(dg1 * taps[FFN_PAD - 2 + k])
            dw_s[SUBLANES * FFN_KERNEL:SUBLANES * (FFN_KERNEL + 1), :] += _fold_rows(dg1)
            return carry

        lax.fori_loop(0, S // tm, first, 0)

        def second(i, carry):
            r0 = pl.multiple_of(i * tm, tm)
            n_rows = tm + FFN_PAD
            taps = _taps(_shifted(dpad_s[pl.ds(r0, n_rows), :], n_rows), range(FFN_KERNEL), tm)
            dg_ref[pl.ds(r0, tm), :] = (w_ref[2:3, :] * taps[0] + w_ref[1:2, :] * taps[1] + w_ref[0:1, :] * taps[2]).astype(BF16)
            return carry

        lax.fori_loop(0, S // tm, second, 0)
        for k in range(FFN_KERNEL):
            dw_ref[k:k + 1, :] = jnp.sum(dw_s[SUBLANES * k:SUBLANES * (k + 1), :], axis=0, keepdims=True)
        dbias_ref[...] = jnp.sum(dw_s[SUBLANES * FFN_KERNEL:SUBLANES * (FFN_KERNEL + 1), :], axis=0, keepdims=True)

    col = lambda off: pl.BlockSpec((S, CB), lambda j: (0, off + j))
    wspec = pl.BlockSpec((FFN_KERNEL, CB), lambda j: (0, j))
    bspec = pl.BlockSpec((1, CB), lambda j: (0, j))
    out, guest_out = _hosted_call(
        body, guest, name="ffn_act_bwd", grid=(nb,),
        in_specs=[col(0), col(nb), wspec, bspec, col(0)],
        out_specs=[col(0), col(0), wspec, bspec],
        out_shape=[jax.ShapeDtypeStruct((S, D_FF), BF16)] * 2
        + [jax.ShapeDtypeStruct((FFN_KERNEL, D_FF), F32), jax.ShapeDtypeStruct((1, D_FF), F32)],
        scratch_shapes=[pltpu.VMEM((S + FFN_PAD, CB), F32), pltpu.VMEM((S + FFN_PAD, CB), F32),
                        pltpu.VMEM((SUBLANES * (FFN_KERNEL + 1), CB), F32)], operands=[u, u, w, bias, dact])
    return (*out, guest_out)


def _loss_grad(y, target, *, tm=512):
    S, D = y.shape

    def body(y_ref, t_ref, dy_ref, l_ref):
        d = y_ref[...] - t_ref[...]
        dy_ref[...] = d * (1.0 / D)

        @pl.when(pl.program_id(0) == 0)
        def _():
            l_ref[...] = jnp.zeros_like(l_ref)

        l_ref[...] += 0.5 * jnp.sum(jnp.mean(d * d, axis=-1, keepdims=True), axis=0, keepdims=True)

    dy, l = pl.pallas_call(
        body, name="loss_grad", grid=(S // tm,),
        in_specs=[pl.BlockSpec((tm, D), lambda i: (i, 0))] * 2,
        out_specs=[pl.BlockSpec((tm, D), lambda i: (i, 0)), pl.BlockSpec((SUBLANES, LANES), lambda i: (0, 0))],
        out_shape=[jax.ShapeDtypeStruct((S, D), F32), jax.ShapeDtypeStruct((SUBLANES, LANES), F32)],
        compiler_params=_params("arbitrary"),
    )(y, target)
    return dy, l[0, 0]


def _row_tile(rows, cap=512):
    t = min(rows, cap)
    while rows % t or t % SUBLANES:
        t -= 1
    return t


def _adam_update(w, g, m, v):
    m1 = ADAM_B1 * m + (1.0 - ADAM_B1) * g
    v1 = ADAM_B2 * v + (1.0 - ADAM_B2) * (g * g)
    m_hat = m1 / (1.0 - ADAM_B1 ** ADAM_STEP)
    v_hat = v1 / (1.0 - ADAM_B2 ** ADAM_STEP)
    return -ADAM_LR * (m_hat / (jnp.sqrt(v_hat) + ADAM_EPS) + ADAM_WD * w), m1, v1


def _adamw(w, g, m, v):
    R, C = w.shape
    tr = _row_tile(R, 256)

    def body(w_ref, g_ref, m_ref, v_ref, d_ref, nm_ref, nv_ref):
        d_ref[...], nm_ref[...], nv_ref[...] = _adam_update(w_ref[...], g_ref[...], m_ref[...], v_ref[...])

    spec = pl.BlockSpec((tr, C), lambda i: (i, 0))
    return pl.pallas_call(
        body, name="adamw", grid=(R // tr,),
        in_specs=[spec] * 4, out_specs=[spec] * 3,
        out_shape=[jax.ShapeDtypeStruct((R, C), F32)] * 3,
        compiler_params=_params("parallel"),
    )(w, g, m, v)


def _adamw_layer(layer, w, g, m, v, prev):
    L, R, C = w.shape
    tr = _row_tile(R, 256)
    n_prev = 4 if prev else 0

    def body(*refs):
        w_ref, g_ref, m_ref, v_ref = refs[:4]
        go_ref, d_ref, nm_ref, nv_ref = refs[4 + n_prev:8 + n_prev]
        gv = g_ref[...]
        go_ref[...] = gv
        d_ref[...], nm_ref[...], nv_ref[...] = _adam_update(w_ref[...], gv, m_ref[...], v_ref[...])

    stacked = pl.BlockSpec((None, tr, C), lambda i: (layer, i, 0))
    return pl.pallas_call(
        body, name="adamw_layer", grid=(R // tr,),
        in_specs=[stacked, pl.BlockSpec((tr, C), lambda i: (i, 0)), stacked, stacked] + [ANY] * n_prev,
        out_specs=[stacked] * 4, out_shape=[jax.ShapeDtypeStruct((L, R, C), F32)] * 4,
        input_output_aliases={4 + j: j for j in range(n_prev)},
        compiler_params=_params("parallel"),
    )(w, g, m, v, *(prev or ()))


def _cast_into_full(w, layer, kind, chip):
    _, R, C = w.shape
    tr = _row_tile(R, 512)

    def body(chip_ref, w_ref, o_ref):
        o_ref[...] = w_ref[...].astype(BF16)

    if kind == "col":
        out_shape = jax.ShapeDtypeStruct((1, R, N_CHIPS * C), BF16)
        out_spec = pl.BlockSpec((None, tr, C), lambda i, chip_ref: (0, i, chip_ref[0]))
    else:
        out_shape = jax.ShapeDtypeStruct((1, N_CHIPS * R, C), BF16)
        out_spec = pl.BlockSpec((None, tr, C), lambda i, chip_ref: (0, chip_ref[0] * (R // tr) + i, 0))
    return pl.pallas_call(
        body, name="cast_into_full_" + kind,
        grid_spec=pltpu.PrefetchScalarGridSpec(
            num_scalar_prefetch=1, grid=(R // tr,),
            in_specs=[pl.BlockSpec((None, tr, C), lambda i, chip_ref: (layer, i, 0))], out_specs=out_spec),
        out_shape=out_shape, compiler_params=_params("parallel"),
    )(chip, w)


ADD_HALF_STEPS = 8
ADD_PARTS_STEPS = 2


def _add_half(g4s, las, c):
    n = len(g4s)

    def body(c_ref, *refs):
        for g_ref, la_ref, o_ref in zip(refs[:n], refs[n:2 * n], refs[2 * n:]):
            o_ref[...] = (g_ref[...].astype(F32) + la_ref[...].astype(F32)).astype(BF16)

    g_specs, la_specs, out_shape = [], [], []
    for g4 in g4s:
        L, _, H, W = g4.shape
        th = L * H // ADD_HALF_STEPS
        per = H // th
        assert th % 16 == 0 and per * th == H
        g_specs.append(pl.BlockSpec((None, None, th, W), lambda i, c_ref, per=per: (i // per, c_ref[0], i % per, 0)))
        la_specs.append(pl.BlockSpec((None, th, W), lambda i, c_ref, per=per: (i // per, i % per, 0)))
        out_shape.append(jax.ShapeDtypeStruct((L, H, W), BF16))
    return pl.pallas_call(
        body, name="add_half",
        grid_spec=pltpu.PrefetchScalarGridSpec(num_scalar_prefetch=1, grid=(ADD_HALF_STEPS,),
                                               in_specs=g_specs + la_specs, out_specs=la_specs),
        out_shape=out_shape, compiler_params=_params("parallel"),
    )(c, *g4s, *las)


def _add_parts(ps, lbs, place, kinds):
    n = len(ps)

    def body(s_ref, *refs):
        for p_ref, lb_ref, o_ref in zip(refs[:n], refs[n:2 * n], refs[2 * n:]):
            acc = p_ref[...].astype(F32)
            for k in range(N_CHIPS - 1):
                acc = acc + lb_ref[k].astype(F32)
            o_ref[...] = acc

    p_specs, lb_specs, out_specs, out_shape = [], [], [], []
    for lb, kind in zip(lbs, kinds):
        _, L, H, C = lb.shape
        th = H // ADD_PARTS_STEPS
        assert L == 1 and th % 16 == 0
        if kind == "col":
            p_specs.append(pl.BlockSpec((None, th, C), lambda i, s_ref: (0, i, s_ref[0])))
        else:
            p_specs.append(pl.BlockSpec((None, None, th, C), lambda i, s_ref: (0, s_ref[0], i, 0)))
        lb_specs.append(pl.BlockSpec((N_CHIPS - 1, None, th, C), lambda i, s_ref: (0, 0, i, 0)))
        out_specs.append(pl.BlockSpec((None, None, th, C), lambda i, s_ref: (0, s_ref[1], i, 0)))
        out_shape.append(jax.ShapeDtypeStruct((L, 2, H, C), F32))
    return pl.pallas_call(
        body, name="add_parts",
        grid_spec=pltpu.PrefetchScalarGridSpec(num_scalar_prefetch=1, grid=(ADD_PARTS_STEPS,),
                                               in_specs=p_specs + lb_specs, out_specs=out_specs),
        out_shape=out_shape, compiler_params=_params("parallel"),
    )(place, *ps, *lbs)


ANY = pl.BlockSpec(memory_space=pl.ANY)


def _place():
    x, y, c = lax.axis_index("x"), lax.axis_index("y"), lax.axis_index("c")
    chips = [(1 - x, y), (x, 1 - y), (1 - x, 1 - y)]
    return x, y, c, chips


def _comm_call(body, name, ins, out_shape, n_remote, n_local, aliases=None):
    scratch = [pltpu.SemaphoreType.DMA((n_remote,)), pltpu.SemaphoreType.DMA((n_remote,))]
    if n_local:
        scratch.append(pltpu.SemaphoreType.DMA((n_local,)))
    return pl.pallas_call(
        body, name=name, in_specs=[ANY] * len(ins), out_specs=[ANY] * len(out_shape), out_shape=out_shape,
        scratch_shapes=scratch, input_output_aliases=aliases or {},
        compiler_params=pltpu.CompilerParams(has_side_effects=True),
    )(*ins)


def _remote(src, dst, send, recv, k, to):
    return pltpu.make_async_remote_copy(src_ref=src, dst_ref=dst, send_sem=send.at[k], recv_sem=recv.at[k],
                                        device_id=to, device_id_type=MESH)


def _gather_weights(fulls, kinds):
    n = len(fulls)
    out_shape = [jax.ShapeDtypeStruct(f.shape, f.dtype) for f in fulls]

    def body(*refs):
        outs, (send, recv) = refs[n:2 * n], refs[2 * n:]
        first = _gather_copies(outs, kinds, send, recv, 0)
        for cp in first:
            cp.start()
        for cp in first:
            cp.wait()
        passed = _pass_on_copies(outs, kinds, send, recv, len(first))
        for cp in passed:
            cp.start()
        for cp in passed:
            cp.wait()

    return _comm_call(body, "gather_weights", fulls, out_shape, 2 * n * (N_CHIPS - 1), 0, {a: a for a in range(n)})


def _window(ref, kind, s, h):
    if kind == "col":
        H, C = ref.shape[1] // 2, ref.shape[2] // N_CHIPS
        return ref.at[:, pl.ds(pl.multiple_of(h * H, 16), H), pl.ds(pl.multiple_of(s * C, LANES), C)]
    R = ref.shape[1] // N_CHIPS
    return ref.at[:, pl.ds(pl.multiple_of(s * R + h * (R // 2), 16), R // 2), :]


def _gather_copies(outs, kinds, send, recv, sem0):
    x, y, c, chips = _place()
    me = 2 * x + y
    return [_remote(_window(o, kind, me, c), _window(o, kind, me, c), send, recv, sem0 + a * (N_CHIPS - 1) + k, (*chip, c))
            for a, (o, kind) in enumerate(zip(outs, kinds)) for k, chip in enumerate(chips)]


def _pass_on_copies(outs, kinds, send, recv, sem0):
    x, y, c, chips = _place()
    cps = []
    for a, (o, kind) in enumerate(zip(outs, kinds)):
        for k, chip in enumerate(chips):
            landed = _window(o, kind, 2 * chip[0] + chip[1], c)
            cps.append(_remote(landed, landed, send, recv, sem0 + a * (N_CHIPS - 1) + k, (x, y, 1 - c)))
    return cps


def _gather_small(shards):
    n = len(shards)
    out_shape = [jax.ShapeDtypeStruct((N_CHIPS,) + s.shape, s.dtype) for s in shards]

    def body(*refs):
        srcs, outs, (send, recv, loc) = refs[:n], refs[n:2 * n], refs[2 * n:]
        x, y, c, chips = _place()
        me = 2 * x + y
        remote, local = [], []
        for a in range(n):
            local.append(pltpu.make_async_copy(srcs[a], outs[a].at[me], loc.at[a]))
            for k, chip in enumerate(chips):
                remote.append(_remote(srcs[a], outs[a].at[me], send, recv, a * (N_CHIPS - 1) + k, (*chip, c)))
        for cp in local + remote:
            cp.start()
        for cp in remote + local:
            cp.wait()

    return _comm_call(body, "gather_small", shards, out_shape, n * (N_CHIPS - 1), n)


def _exchange_halves(g4s):
    n = len(g4s)
    out_shape = [jax.ShapeDtypeStruct((g.shape[0],) + g.shape[2:], g.dtype) for g in g4s]

    def body(*refs):
        gs, las, (send, recv) = refs[:n], refs[n:2 * n], refs[2 * n:]
        x, y, c, _ = _place()
        cps = [_remote(gs[a].at[:, 1 - c], las[a], send, recv, a, (x, y, 1 - c)) for a in range(n)]
        for cp in cps:
            cp.start()
        for cp in cps:
            cp.wait()

    return _comm_call(body, "exchange_halves", g4s, out_shape, n, 0)


def _scatter_partials(ps, kinds):
    n = len(ps)

    def body(*refs):
        srcs, lbs, (send, recv) = refs[:n], refs[n:2 * n], refs[2 * n:]
        cps = _scatter_copies(srcs, lbs, kinds, send, recv)
        for cp in cps:
            cp.start()
        for cp in cps:
            cp.wait()

    return _comm_call(body, "scatter_partials", ps, _scatter_shapes(ps, kinds), n * (N_CHIPS - 1), 0)


def _scatter_shapes(ps, kinds):
    out_shape = []
    for p, kind in zip(ps, kinds):
        L, H, C = (p.shape[0], p.shape[1], p.shape[2] // N_CHIPS) if kind == "col" else (p.shape[0], p.shape[2], p.shape[3])
        out_shape.append(jax.ShapeDtypeStruct((N_CHIPS - 1, L, H, C), p.dtype))
    return out_shape


def _scatter_copies(srcs, lbs, kinds, send, recv, sem0=0):
    x, y, c, chips = _place()
    cps = []
    for a, (src, lb, kind) in enumerate(zip(srcs, lbs, kinds)):
        C = lb.shape[3]
        for k, chip in enumerate(chips):
            s = 2 * chip[0] + chip[1]
            part = src.at[:, :, pl.ds(pl.multiple_of(s * C, LANES), C)] if kind == "col" else src.at[:, s]
            cps.append(_remote(part, lb.at[k], send, recv, sem0 + a * (N_CHIPS - 1) + k, (*chip, c)))
    return cps


def _share_halves(g4s):
    n = len(g4s)
    out_shape = [jax.ShapeDtypeStruct(g.shape, g.dtype) for g in g4s]

    def body(*refs):
        outs, (send, recv) = refs[n:2 * n], refs[2 * n:]
        x, y, c, _ = _place()
        cps = [_remote(outs[a].at[:, c], outs[a].at[:, c], send, recv, a, (x, y, 1 - c)) for a in range(n)]
        for cp in cps:
            cp.start()
        for cp in cps:
            cp.wait()

    return _comm_call(body, "share_halves", g4s, out_shape, n, 0, {a: a for a in range(n)})


def _allreduce_small(part):
    N = part.shape[0]

    def body(p_ref, o_ref, buf, send, recv):
        x, y, c, _ = _place()
        me = 4 * x + 2 * y + c
        buf[me] = p_ref[...]
        cps = []
        for k in range(1, N_DEV):
            peer = (1 - x if k & 4 else x, 1 - y if k & 2 else y, 1 - c if k & 1 else c)
            cps.append(_remote(p_ref, buf.at[me], send, recv, k - 1, peer))
        for cp in cps:
            cp.start()
        for cp in cps:
            cp.wait()
        acc = buf[0]
        for i in range(1, N_DEV):
            acc = acc + buf[i]
        o_ref[...] = acc

    vmem = pl.BlockSpec(memory_space=pltpu.VMEM)
    return pl.pallas_call(
        body, name="allreduce_small", in_specs=[vmem], out_specs=vmem,
        out_shape=jax.ShapeDtypeStruct((N, LANES), F32),
        scratch_shapes=[pltpu.VMEM((N_DEV, N, LANES), F32), pltpu.SemaphoreType.DMA((N_DEV - 1,)),
                        pltpu.SemaphoreType.DMA((N_DEV - 1,))],
        compiler_params=pltpu.CompilerParams(has_side_effects=True, vmem_limit_bytes=VMEM_LIMIT_BYTES),
    )(part)


AFTER_ATTENTION = ("w_out", "w_up", "w_down")


def _layer_fwd(x, p, full=None, l=0):
    comm = full is not None

    def guest(first, second):
        first, second = ([k for k in ks if comm and k[0] < len(full)] for ks in (first, second))
        if not first + second:
            return None, []
        return _gather_guest([full[i][n] for i, n in first], [BIG_KIND[n] for _, n in first],
                             [full[i][n] for i, n in second], [BIG_KIND[n] for _, n in second]), first + second

    def done(keys, bufs):
        for (i, n), buf in zip(keys, bufs):
            full[i][n] = buf

    def weight(n):
        return full[l][n][0] if comm else p[n]

    in_sweep = ([(l, "w_up")] if l else [(l, n) for n in AFTER_ATTENTION]) if comm else []
    g, keys = guest([], [(l, "w_out")] if l else [])
    h1t, proj, out = _norm_matmul(x, p["norm1_g"], weight("w_in"), g)
    done(keys, out)
    attn, ltot, attnt, out = _attn_fwd(proj, p["q_norm_g"], p["k_norm_g"],
                                       [full[i][n] for i, n in in_sweep], [BIG_KIND[n] for _, n in in_sweep])
    done(in_sweep, out)
    g, keys = guest([], in_sweep)
    c1, out = _glu_conv_fwd(proj, p["conv_dw_w"], p["conv_dw_b"], g)
    done(keys, out)
    c, ct = _ln_silu_fwd(c1, p["conv_ln_g"], p["conv_ln_b"])
    x_mid, _ = _matmul_res([attn, c], weight("w_out"), x)
    g, keys = guest([(l + 1, "w_down")], [])
    h2t, u, out = _norm_matmul(x_mid, p["norm2_g"], weight("w_up"), g)
    done(keys, out)
    g, keys = guest([(l + 1, "w_in")], [(l + 1, "w_down")])
    act, actt, out = _ffn_act_fwd(u, p["ffn_dw_w"], p["ffn_dw_b"], g)
    done(keys, out)
    g, keys = guest([(l + 1, "w_out")], [(l + 1, "w_in")])
    x_out, out = _matmul_res([act], weight("w_down"), x_mid, g)
    done(keys, out)
    saved = dict(x=x, h1t=h1t, proj=proj, attnt=attnt, ltot=ltot, c1=c1, ct=ct, x_mid=x_mid, h2t=h2t, u=u, actt=actt)
    return x_out, saved


def _chip_partials(gs, kinds, core):
    g4s = _halves_view(gs, kinds)
    return _add_halves(g4s, _exchange_halves(g4s), kinds, core)


def _halves_view(gs, kinds):
    return [g.reshape(1, 2, g.shape[0] // 2, g.shape[1]) if kind == "col"
            else g.reshape(N_CHIPS, 2, g.shape[0] // N_CHIPS // 2, g.shape[1]) for g, kind in zip(gs, kinds)]


def _add_halves(g4s, received, kinds, core):
    parts = _add_half(g4s, received, core)
    return [p if kind == "col" else p.reshape(1, N_CHIPS, p.shape[1], p.shape[2]) for p, kind in zip(parts, kinds)]


def _gather_guest(first, first_kinds, second, second_kinds):
    bufs = list(first) + list(second)
    n1 = len(first)

    def copies(ins, outs, send, recv, sem0=0):
        return (_gather_copies(outs[:n1], first_kinds, send, recv, sem0)
                + _pass_on_copies(outs[n1:], second_kinds, send, recv, sem0 + n1 * (N_CHIPS - 1)))

    return _Guest("gather", bufs, [jax.ShapeDtypeStruct(f.shape, f.dtype) for f in bufs],
                  {a: a for a in range(len(bufs))}, copies, len(bufs) * (N_CHIPS - 1))


def _scatter_guest(parts, kinds):
    return _Guest("scatter", list(parts), _scatter_shapes(parts, kinds), {},
                  lambda ins, outs, send, recv, sem0=0: _scatter_copies(ins, outs, kinds, send, recv, sem0),
                  len(parts) * (N_CHIPS - 1))


def _exchange_guest(g4s):
    def copies(ins, outs, send, recv, sem0=0):
        x, y, c, _ = _place()
        return [_remote(g.at[:, 1 - c], la, send, recv, sem0 + a, (x, y, 1 - c)) for a, (g, la) in enumerate(zip(ins, outs))]

    out_shape = [jax.ShapeDtypeStruct((g.shape[0],) + g.shape[2:], g.dtype) for g in g4s]
    return _Guest("exchange", list(g4s), out_shape, {}, copies, len(g4s))


def _share_guest(g4s):
    def copies(ins, outs, send, recv, sem0=0):
        x, y, c, _ = _place()
        return [_remote(o.at[:, c], o.at[:, c], send, recv, sem0 + a, (x, y, 1 - c)) for a, o in enumerate(outs)]

    n = len(g4s)
    return _Guest("share", list(g4s), [jax.ShapeDtypeStruct(g.shape, g.dtype) for g in g4s], {a: a for a in range(n)}, copies, n)


def _owned_sums(parts, landed, kinds, place):
    halves = _add_parts(parts, landed, place, kinds)
    return [g.reshape(2 * g.shape[2], g.shape[3]) for g in _share_halves(halves)]


def _layer_bwd(dx_out, s, p, comm=None):
    g = {}
    on = comm is not None
    core, place, pending = comm if on else (None, None, [])
    n_p = len(pending)
    finished = lambda arrays: [a.reshape(2 * a.shape[2], a.shape[3]) for a in arrays]

    if n_p:
        g4s_p = _halves_view(pending, ["col"] * n_p)
    dact, received = _matmul_nt([dx_out], p["w_down"], BF16, _exchange_guest(g4s_p) if n_p else None)
    if n_p:
        parts_p = _add_halves(g4s_p, received, ["col"] * n_p, core)
    g["w_down"] = _matmul_tn([s["actt"]], [dx_out], tk=D_FF // 2, tn=D_MODEL)
    dgate, dval, g["ffn_dw_w"], g["ffn_dw_b"], landed_p = _ffn_act_bwd(
        s["u"], p["ffn_dw_w"], p["ffn_dw_b"], dact, _scatter_guest(parts_p, ["col"] * n_p) if n_p else None)
    guest = _share_guest(_add_parts(parts_p, landed_p, place, ["col"] * n_p)) if n_p else None
    dx_mid, g["norm2_g"], shared_p = _matmul_nt_rmsbwd([dgate, dval], p["w_up"], s["x_mid"], p["norm2_g"], dx_out, guest)
    g["w_up"] = _matmul_tn([s["h2t"]], [dgate, dval], tk=D_MODEL, tn=D_FF // 2)
    dmix, _ = _matmul_nt([dx_mid], p["w_out"], F32)
    g["w_out"] = _matmul_tn([s["attnt"], s["ct"]], [dx_mid], tk=ATTN_WIDTH, tn=D_MODEL)
    dc1, g["conv_ln_g"], g["conv_ln_b"] = _ln_silu_bwd(s["c1"], p["conv_ln_g"], p["conv_ln_b"], dmix)

    late = AFTER_ATTENTION
    parts, kinds = [], []
    if on:
        kinds = [BIG_KIND[n] for n in late]
        g4s = _halves_view([g[n] for n in late], kinds)
    da, db, g["conv_dw_w"], g["conv_dw_b"], received = _glu_conv_bwd(
        s["proj"], p["conv_dw_w"], dc1, _exchange_guest(g4s) if on else None)
    if on:
        parts = _add_halves(g4s, received, kinds, core)
    (dq, dk, dv, g["q_norm_g"], g["k_norm_g"]), landed = _attn_bwd(
        s["proj"], p["q_norm_g"], p["k_norm_g"], s["ltot"], dmix, parts, kinds)
    guest = _share_guest(_add_parts(parts, landed, place, kinds)) if on else None
    pieces = [dq, dk, dv, da, db]
    dx, g["norm1_g"], shared = _matmul_nt_rmsbwd(pieces, p["w_in"], s["x"], p["norm1_g"], dx_mid, guest)
    g["w_in"] = _matmul_tn([s["h1t"]], pieces, tk=D_MODEL, tn=ATTN_WIDTH)
    sums = {}
    if on:
        sums = dict(zip(late, finished(shared)), pending=finished(shared_p))
    return dx, g, sums


WEIGHTS = ("norm1_g", "w_in", "q_norm_g", "k_norm_g", "conv_dw_w", "conv_dw_b", "conv_ln_g", "conv_ln_b",
           "w_out", "norm2_g", "w_up", "ffn_dw_w", "ffn_dw_b", "w_down")
BIG = ("w_in", "w_out", "w_up", "w_down")
BIG_KIND = {"w_in": "col", "w_out": "row", "w_up": "col", "w_down": "row"}
SMALL_SHARDED = ("conv_dw_w", "ffn_dw_w")
REPLICATED = tuple(n for n in WEIGHTS if n not in BIG + SMALL_SHARDED)


def _pack(arrays):
    flat = jnp.concatenate([a.reshape(-1) for a in arrays])
    rows = -(-flat.shape[0] // (SUBLANES * LANES)) * SUBLANES
    return jnp.pad(flat, (0, rows * LANES - flat.shape[0])).reshape(rows, LANES)


def _unpack(packed, shapes):
    flat = packed.reshape(-1)
    out, off = [], 0
    for shape in shapes:
        size = 1
        for d in shape:
            size *= d
        out.append(flat[off:off + size].reshape(shape))
        off += size
    return out


def _unshard_last(stacked):
    n, L, K, C = stacked.shape
    return jnp.transpose(stacked, (1, 2, 0, 3)).reshape(L, K, n * C)


def kernel(x, norm1_g, w_in, q_norm_g, k_norm_g, conv_dw_w, conv_dw_b, conv_ln_g, conv_ln_b, w_out, norm2_g, w_up, ffn_dw_w, ffn_dw_b, w_down, loss_target, m_norm1_g, m_w_in, m_q_norm_g, m_k_norm_g, m_conv_dw_w, m_conv_dw_b, m_conv_ln_g, m_conv_ln_b, m_w_out, m_norm2_g, m_w_up, m_ffn_dw_w, m_ffn_dw_b, m_w_down, v_norm1_g, v_w_in, v_q_norm_g, v_k_norm_g, v_conv_dw_w, v_conv_dw_b, v_conv_ln_g, v_conv_ln_b, v_w_out, v_norm2_g, v_w_up, v_ffn_dw_w, v_ffn_dw_b, v_w_down):
    given = dict(locals())
    w = {n: given[n] for n in WEIGHTS}
    m = {n: given["m_" + n] for n in WEIGHTS}
    v = {n: given["v_" + n] for n in WEIGHTS}
    chip = 2 * lax.axis_index("x") + lax.axis_index("y")
    core = lax.axis_index("c")
    chip_arr = jnp.reshape(chip, (1,)).astype(jnp.int32)
    core_arr = jnp.reshape(core, (1,)).astype(jnp.int32)
    L = DEPTH

    place = jnp.concatenate([chip_arr, core_arr])

    full = [{n: _cast_into_full(w[n], l, BIG_KIND[n], chip_arr) for n in BIG} for l in range(L)]
    full[0]["w_in"] = _gather_weights([full[0]["w_in"]], ["col"])[0]
    small_full = {n: _unshard_last(stacked)
                  for n, stacked in zip(SMALL_SHARDED, _gather_small([w[n] for n in SMALL_SHARDED]))}
    params = []
    for l in range(L):
        p = {n: small_full[n][l] for n in SMALL_SHARDED}
        p.update({n: w[n][l][None] for n in REPLICATED})
        params.append(p)

    act = x[0]
    saved = []
    for l in range(L):
        act, s = _layer_fwd(act, params[l], full, l)
        saved.append(s)
    for l in range(L):
        params[l].update({n: full[l][n][0] for n in BIG})
    dx, loss_part = _loss_grad(act, loss_target[0])
    loss = lax.psum(loss_part, ("x", "y", "c"))

    grads = [None] * L
    summed = {}
    pending = []
    for l in reversed(range(L)):
        dx, grads[l], sums = _layer_bwd(dx, saved[l], params[l], (core_arr, place, pending))
        summed.update({(l, n): sums[n] for n in AFTER_ATTENTION})
        if pending:
            summed[l + 1, "w_in"] = sums["pending"][0]
        pending = [grads[l]["w_in"]]
    parts = _chip_partials(pending, ["col"], core_arr)
    summed[0, "w_in"] = _owned_sums(parts, _scatter_partials(parts, ["col"]), ["col"], place)[0]

    grad, delta, new_m, new_v = {}, {}, {}, {}
    for n in BIG:
        out = None
        for l in range(L):
            out = _adamw_layer(l, w[n], summed[l, n], m[n], v[n], out)
        grad[n], delta[n], new_m[n], new_v[n] = out

    small = REPLICATED + SMALL_SHARDED
    small_grads = [jnp.stack([grads[l][n] for l in range(L)]) for n in small]
    small_sums = _unpack(_allreduce_small(_pack(small_grads)), [a.shape for a in small_grads])
    for n, g in zip(small, small_sums):
        if n in REPLICATED:
            grad[n] = g.reshape(w[n].shape)
        else:
            width = w[n].shape[2]
            grad[n] = lax.dynamic_slice_in_dim(g, chip * width, width, axis=2)
    shapes = [w[n].shape for n in small]
    packed = _adamw(*[_pack([src[n] for n in small]) for src in (w, grad, m, v)])
    for out, pk in zip((delta, new_m, new_v), packed):
        out.update(zip(small, _unpack(pk, shapes)))

    return (loss, dx[None], *[grad[n] for n in WEIGHTS], *[delta[n] for n in WEIGHTS],
            *[new_m[n] for n in WEIGHTS], *[new_v[n] for n in WEIGHTS])
```

```python
import jax
import jax.numpy as jnp
from jax import lax
from jax.experimental import pallas as pl
from jax.experimental.pallas import tpu as pltpu

F32 = jnp.float32
BF16 = jnp.bfloat16

DEPTH = 4
D_MODEL = 1024
HEADS = 8
HEAD_DIM = 64
ATTN_WIDTH = HEADS * HEAD_DIM
CONV_WIDTH = D_MODEL - ATTN_WIDTH
CONV_KERNEL = 31
D_FF = 2816
FFN_KERNEL = 3
EPS = 1e-6
ADAM_LR, ADAM_B1, ADAM_B2, ADAM_EPS, ADAM_WD, ADAM_STEP = 0.001, 0.9, 0.999, 1e-08, 0.01, 10

N_CHIPS = 4
N_DEV = 8
LANES = 128
SUBLANES = 8
VMEM_LIMIT_BYTES = 56 * 2**20
ATTN_TILE = 256
ATTN_HEADS_PER_STEP = 4
ATTN_BLOCK = ATTN_HEADS_PER_STEP * HEAD_DIM
CONV_PAD = 32
FFN_PAD = 8
MESH = pl.DeviceIdType.MESH


def _params(*sem):
    return pltpu.CompilerParams(dimension_semantics=sem if sem else None, vmem_limit_bytes=VMEM_LIMIT_BYTES)


class _Guest:
    def __init__(self, name, ins, out_shape, aliases, copies, n_sem):
        self.name, self.ins, self.out_shape, self.aliases, self.copies, self.n_sem = name, ins, out_shape, aliases, copies, n_sem


def _hosted_call(body, guest, *, name, grid, in_specs, out_specs, out_shape, scratch_shapes, operands):
    if guest is None:
        out = pl.pallas_call(body, name=name, grid=grid, in_specs=in_specs, out_specs=out_specs, out_shape=out_shape,
                             scratch_shapes=scratch_shapes, compiler_params=_params("arbitrary"))(*operands)
        return list(out), []
    n_in, n_out, n_scr = len(in_specs), len(out_specs), len(scratch_shapes)
    gi, go = len(guest.ins), len(guest.out_shape)

    def hosting(*refs):
        ins, g_in = refs[:n_in], refs[n_in:n_in + gi]
        outs, g_out = refs[n_in + gi:n_in + gi + n_out], refs[n_in + gi + n_out:n_in + gi + n_out + go]
        scratch, (send, recv) = refs[n_in + gi + n_out + go:-2], refs[-2:]

        @pl.when(pl.program_id(0) == 0)
        def _():
            for cp in guest.copies(g_in, g_out, send, recv):
                cp.start()

        body(*ins, *outs, *scratch)

        @pl.when(pl.program_id(0) == grid[0] - 1)
        def _():
            for cp in guest.copies(g_in, g_out, send, recv):
                cp.wait()

    out = pl.pallas_call(
        hosting, name=name + "_" + guest.name, grid=grid,
        in_specs=list(in_specs) + [ANY] * gi, out_specs=list(out_specs) + [ANY] * go,
        out_shape=list(out_shape) + list(guest.out_shape),
        scratch_shapes=list(scratch_shapes) + [pltpu.SemaphoreType.DMA((guest.n_sem,))] * 2,
        input_output_aliases={n_in + a: n_out + b for a, b in guest.aliases.items()},
        compiler_params=pltpu.CompilerParams(dimension_semantics=("arbitrary",), vmem_limit_bytes=VMEM_LIMIT_BYTES,
                                             has_side_effects=True),
    )(*operands, *guest.ins)
    return list(out[:n_out]), list(out[n_out:])


def _dot(a, b):
    return jnp.dot(a, b, preferred_element_type=F32)


def _dot_nt(a, b):
    return lax.dot_general(a, b, (((1,), (1,)), ((), ())), preferred_element_type=F32)


def _dot_tn(a, b):
    return lax.dot_general(a, b, (((0,), (0,)), ((), ())), preferred_element_type=F32)


def _sigmoid(x):
    return 1.0 / (1.0 + jnp.exp(-x))


def _norm_matmul(x, g, w, guest=None, *, out_dtype=F32, tm=256):
    S, D = x.shape
    N = w.shape[1]

    def body(x_ref, g_ref, w_ref, ht_ref, y_ref):
        xv = x_ref[...]
        h = xv * lax.rsqrt(jnp.mean(xv * xv, axis=-1, keepdims=True) + EPS) * g_ref[...]
        ht_ref[...] = h.T.astype(BF16)
        y_ref[...] = _dot(h.astype(BF16), w_ref[...]).astype(out_dtype)

    (ht, y), guest_out = _hosted_call(
        body, guest, name="norm_matmul", grid=(S // tm,),
        in_specs=[pl.BlockSpec((tm, D), lambda i: (i, 0)),
                  pl.BlockSpec((1, D), lambda i: (0, 0)),
                  pl.BlockSpec((D, N), lambda i: (0, 0))],
        out_specs=[pl.BlockSpec((D, tm), lambda i: (0, i)),
                   pl.BlockSpec((tm, N), lambda i: (i, 0))],
        out_shape=[jax.ShapeDtypeStruct((D, S), BF16), jax.ShapeDtypeStruct((S, N), out_dtype)],
        scratch_shapes=[], operands=[x, g, w])
    return ht, y, guest_out


def _matmul_res(pieces, w, res, guest=None, *, tm=512):
    S, N = res.shape
    K = w.shape[0]
    widths = [p.shape[1] for p in pieces]
    assert sum(widths) == K

    def body(*refs):
        p_refs, (w_ref, res_ref, o_ref) = refs[:len(pieces)], refs[len(pieces):]
        acc = res_ref[...]
        off = 0
        for p_ref, kp in zip(p_refs, widths):
            acc = acc + _dot(p_ref[...], w_ref[off:off + kp, :])
            off += kp
        o_ref[...] = acc

    (out,), guest_out = _hosted_call(
        body, guest, name="matmul_res", grid=(S // tm,),
        in_specs=[pl.BlockSpec((tm, kp), lambda i: (i, 0)) for kp in widths]
        + [pl.BlockSpec((K, N), lambda i: (0, 0)), pl.BlockSpec((tm, N), lambda i: (i, 0))],
        out_specs=[pl.BlockSpec((tm, N), lambda i: (i, 0))],
        out_shape=[jax.ShapeDtypeStruct((S, N), F32)],
        scratch_shapes=[], operands=[*pieces, w, res])
    return out, guest_out


def _nt_sum(p_refs, widths, w_ref):
    acc = None
    off = 0
    for p_ref, n_p in zip(p_refs, widths):
        d = _dot_nt(p_ref[...].astype(BF16), w_ref[:, off:off + n_p])
        acc = d if acc is None else acc + d
        off += n_p
    return acc


def _matmul_nt(pieces, w, out_dtype, guest=None, *, tm=512):
    S = pieces[0].shape[0]
    K, N = w.shape
    widths = [p.shape[1] for p in pieces]
    assert sum(widths) == N

    def body(*refs):
        p_refs, (w_ref, o_ref) = refs[:len(pieces)], refs[len(pieces):]
        o_ref[...] = _nt_sum(p_refs, widths, w_ref).astype(out_dtype)

    (out,), guest_out = _hosted_call(
        body, guest, name="matmul_nt", grid=(S // tm,),
        in_specs=[pl.BlockSpec((tm, n_p), lambda i: (i, 0)) for n_p in widths]
        + [pl.BlockSpec((K, N), lambda i: (0, 0))],
        out_specs=[pl.BlockSpec((tm, K), lambda i: (i, 0))],
        out_shape=[jax.ShapeDtypeStruct((S, K), out_dtype)],
        scratch_shapes=[], operands=[*pieces, w])
    return out, guest_out


def _matmul_nt_rmsbwd(pieces, w, x, g, dres, guest=None, *, tm=256):
    S, K = x.shape
    N = w.shape[1]
    widths = [p.shape[1] for p in pieces]
    assert sum(widths) == N

    def body(*refs):
        p_refs, (w_ref, x_ref, g_ref, dres_ref, dx_ref, dg_ref) = refs[:len(pieces)], refs[len(pieces):]
        dh = _nt_sum(p_refs, widths, w_ref)
        xv = x_ref[...]
        r = lax.rsqrt(jnp.mean(xv * xv, axis=-1, keepdims=True) + EPS)
        xh = xv * r
        dxh = dh * g_ref[...]
        dx_ref[...] = dres_ref[...] + r * (dxh - xh * jnp.mean(dxh * xh, axis=-1, keepdims=True))

        @pl.when(pl.program_id(0) == 0)
        def _():
            dg_ref[...] = jnp.zeros_like(dg_ref)

        dg_ref[...] += jnp.sum(dh * xh, axis=0, keepdims=True)

    (dx, dg), guest_out = _hosted_call(
        body, guest, name="matmul_nt_rmsbwd", grid=(S // tm,),
        in_specs=[pl.BlockSpec((tm, n_p), lambda i: (i, 0)) for n_p in widths]
        + [pl.BlockSpec((K, N), lambda i: (0, 0)), pl.BlockSpec((tm, K), lambda i: (i, 0)),
           pl.BlockSpec((1, K), lambda i: (0, 0)), pl.BlockSpec((tm, K), lambda i: (i, 0))],
        out_specs=[pl.BlockSpec((tm, K), lambda i: (i, 0)), pl.BlockSpec((1, K), lambda i: (0, 0))],
        out_shape=[jax.ShapeDtypeStruct((S, K), F32), jax.ShapeDtypeStruct((1, K), F32)],
        scratch_shapes=[], operands=[*pieces, w, x, g, dres])
    return dx, dg, guest_out


def _matmul_tn(xts, dys, *, tk, tn, ts=1024):
    S = dys[0].shape[0]
    xs = xts
    n_s = S // ts
    (mt,) = {x.shape[0] // tk for x in xts}
    (nt,) = {d.shape[1] // tn for d in dys}

    def body(*refs):
        x_refs, dy_refs, (o_ref, acc_ref) = refs[:len(xs)], refs[len(xs):len(xs) + len(dys)], refs[len(xs) + len(dys):]
        i, j, s = pl.program_id(0), pl.program_id(1), pl.program_id(2)

        @pl.when(s == 0)
        def _():
            acc_ref[...] = jnp.zeros_like(acc_ref)

        for a, x_ref in enumerate(x_refs):
            for b, dy_ref in enumerate(dy_refs):
                @pl.when((i // mt == a) & (j // nt == b))
                def _():
                    acc_ref[...] += _dot(x_ref[...], dy_ref[...].astype(BF16))

        @pl.when(s == n_s - 1)
        def _():
            o_ref[...] = acc_ref[...].astype(BF16)

    def x_map(a):
        return lambda i, j, s: (jnp.where(i // mt == a, i % mt, 0), jnp.where(i // mt == a, s, 0))

    def dy_map(b):
        return lambda i, j, s: (jnp.where(j // nt == b, s, 0), jnp.where(j // nt == b, j % nt, 0))

    return pl.pallas_call(
        body, name="matmul_tn", grid=(len(xs) * mt, len(dys) * nt, n_s),
        in_specs=[pl.BlockSpec((tk, ts), x_map(a)) for a in range(len(xs))]
        + [pl.BlockSpec((ts, tn), dy_map(b)) for b in range(len(dys))],
        out_specs=pl.BlockSpec((tk, tn), lambda i, j, s: (i, j)),
        out_shape=jax.ShapeDtypeStruct((len(xs) * mt * tk, len(dys) * nt * tn), BF16),
        scratch_shapes=[pltpu.VMEM((tk, tn), F32)],
        compiler_params=_params("parallel", "parallel", "arbitrary"),
    )(*xs, *dys)


def _tri_consts():
    j = jnp.arange(ATTN_TILE)[:, None]
    s = jnp.arange(ATTN_TILE)[None, :]
    return (j > s).astype(BF16), (j <= s).astype(BF16), (j < s).astype(BF16)


SIGN_BIT = 0x80000000
WEIGHT_IS_ZERO = -104.0


def _log_terms(sn):
    minus_abs = lax.bitcast_convert_type(lax.bitcast_convert_type(sn, jnp.uint32) | jnp.uint32(SIGN_BIT), F32)
    lom = jnp.minimum(sn, 0.0) - jnp.log(1.0 + jnp.exp(minus_abs))
    return lom, lom - sn


def _causal_mask():
    t = lax.broadcasted_iota(jnp.int32, (ATTN_TILE, ATTN_TILE), 0)
    s = lax.broadcasted_iota(jnp.int32, (ATTN_TILE, ATTN_TILE), 1)
    return s < t


def _attn_prep(h, q_ref, k_ref, v_ref, qg_ref, kg_ref, qn_s, kn_s, vb_s, n_tiles):
    T = ATTN_TILE
    lanes = slice(HEAD_DIM * h, HEAD_DIM * (h + 1))
    scale = -(HEAD_DIM ** -0.5)

    def prep(i, carry):
        rows = pl.ds(pl.multiple_of(i * T, T), T)
        q = q_ref[rows, lanes]
        k = k_ref[rows, lanes]
        rq = lax.rsqrt(jnp.mean(q * q, axis=-1, keepdims=True) + EPS)
        rk = lax.rsqrt(jnp.mean(k * k, axis=-1, keepdims=True) + EPS)
        qn_s[h, rows, :] = (q * rq * qg_ref[...] * scale).astype(BF16)
        kn_s[h, rows, :] = (k * rk * kg_ref[...]).astype(BF16)
        vb_s[h, rows, :] = v_ref[rows, lanes].astype(BF16)
        return carry

    lax.fori_loop(0, n_tiles, prep, 0)


def _attn_fwd(proj, qg, kg, fulls=(), kinds=()):
    S = proj.shape[0]
    n_comm = len(fulls)
    T = ATTN_TILE
    n_tiles = S // T
    suffix, _, _ = _tri_consts()
    pairs = HEADS // ATTN_HEADS_PER_STEP
    q_blk, k_blk, v_blk = 0, ATTN_WIDTH // ATTN_BLOCK, 2 * ATTN_WIDTH // ATTN_BLOCK

    def body(*refs):
        q_ref, k_ref, v_ref, qg_ref, kg_ref, tri_ref = refs[:6]
        o_ref, lt_ref, ot_ref, start_ref, qn_s, kn_s, vb_s = refs[6 + n_comm:13 + n_comm]
        w_refs = refs[13 + n_comm:13 + 2 * n_comm]
        sems = refs[13 + 2 * n_comm:]
        step_id = pl.program_id(0)
        if n_comm:
            @pl.when(pl.program_id(0) == 0)
            def _():
                for cp in _gather_copies(w_refs, kinds, *sems, 0):
                    cp.start()

        heads = range(ATTN_HEADS_PER_STEP)
        for h in heads:
            _attn_prep(h, q_ref, k_ref, v_ref, qg_ref, kg_ref, qn_s, kn_s, vb_s, n_tiles)

        def q_tile(qi, carry0):
            qrows = pl.ds(pl.multiple_of(qi * T, T), T)
            qt = [qn_s[h, qrows, :] for h in heads]

            def tile(kj, carry, diag):
                krows = pl.ds(pl.multiple_of(kj * T, T), T)
                s = [_dot_nt(qt[h], kn_s[h, krows, :]) for h in heads]
                terms = [_log_terms(s[h]) for h in heads]
                lom = [terms[h][0] for h in heads]
                if diag:
                    mask = _causal_mask()
                    lom = [jnp.where(mask, lom[h], 0.0) for h in heads]
                lom = [lom[h].astype(BF16) for h in heads]
                tail = [_dot(lom[h], tri_ref[...]) for h in heads]
                w = [jnp.exp(terms[h][1] + tail[h] + carry[h][1]) for h in heads]
                if diag:
                    w = [jnp.where(mask, w[h], 0.0) for h in heads]
                return tuple((carry[h][0] + _dot(w[h].astype(BF16), vb_s[h, krows, :]),
                              carry[h][1] + (tail[h][:, 0:1] + lom[h][:, 0:1].astype(F32))) for h in heads)

            init = tuple((jnp.zeros((T, HEAD_DIM), F32), jnp.zeros((T, 1), F32)) for h in heads)
            def alive(cr):
                worst = cr[0][1]
                for h in heads[1:]:
                    worst = jnp.maximum(worst, cr[h][1])
                return jnp.max(worst) >= WEIGHT_IS_ZERO

            def step(state):
                t, _, cr = state
                cr = tile(qi - 1 - t, cr, False)
                return t + 1, alive(cr), cr

            first = tile(qi, init, True)
            swept, _, carry = lax.while_loop(lambda st: (st[0] < qi) & st[1], step, (jnp.int32(0), alive(first), first))
            start_ref[step_id, qi] = (qi - swept).astype(F32)
            o = jnp.concatenate([carry[h][0] for h in heads], axis=1)
            o_ref[qrows, :] = o.astype(BF16)
            ot_ref[:, qrows] = o.T.astype(BF16)
            for h in heads:
                lt_ref[qrows, HEAD_DIM * h:HEAD_DIM * (h + 1)] = jnp.broadcast_to(carry[h][1], (T, HEAD_DIM))
            return carry0

        lax.fori_loop(0, n_tiles, q_tile, 0)
        if n_comm:
            @pl.when(pl.program_id(0) == pairs - 1)
            def _():
                for cp in _gather_copies(w_refs, kinds, *sems, 0):
                    cp.wait()

    n_sem = n_comm * (N_CHIPS - 1)
    per_head = pl.BlockSpec((ATTN_HEADS_PER_STEP, S, HEAD_DIM), lambda p: (p, 0, 0))
    out = pl.pallas_call(
        body, name="attn_fwd_gather" if n_comm else "attn_fwd", grid=(pairs,),
        in_specs=[pl.BlockSpec((S, ATTN_BLOCK), lambda p: (0, q_blk + p)),
                  pl.BlockSpec((S, ATTN_BLOCK), lambda p: (0, k_blk + p)),
                  pl.BlockSpec((S, ATTN_BLOCK), lambda p: (0, v_blk + p)),
                  pl.BlockSpec((1, HEAD_DIM), lambda p: (0, 0)),
                  pl.BlockSpec((1, HEAD_DIM), lambda p: (0, 0)),
                  pl.BlockSpec((T, T), lambda p: (0, 0))] + [ANY] * n_comm,
        out_specs=[pl.BlockSpec((S, ATTN_BLOCK), lambda p: (0, p)),
                   pl.BlockSpec((None, S, ATTN_BLOCK), lambda p: (p, 0, 0)),
                   pl.BlockSpec((ATTN_BLOCK, S), lambda p: (p, 0)),
                   pl.BlockSpec(memory_space=pltpu.SMEM)] + [per_head] * 3 + [ANY] * n_comm,
        out_shape=[jax.ShapeDtypeStruct((S, ATTN_WIDTH), BF16),
                   jax.ShapeDtypeStruct((pairs, S, ATTN_BLOCK), F32),
                   jax.ShapeDtypeStruct((ATTN_WIDTH, S), BF16),
                   jax.ShapeDtypeStruct((pairs, n_tiles), F32)] + [jax.ShapeDtypeStruct((HEADS, S, HEAD_DIM), BF16)] * 3
        + [jax.ShapeDtypeStruct(f.shape, f.dtype) for f in fulls],
        scratch_shapes=[pltpu.SemaphoreType.DMA((n_sem,))] * 2 if n_comm else [],
        input_output_aliases={6 + a: 7 + a for a in range(n_comm)},
        compiler_params=pltpu.CompilerParams(dimension_semantics=("arbitrary",), vmem_limit_bytes=VMEM_LIMIT_BYTES,
                                             has_side_effects=bool(n_comm)),
    )(proj, proj, proj, qg, kg, suffix, *fulls)
    return out[0], (out[1], out[3], out[4], out[5], out[6]), out[2], list(out[7:])


def _attn_bwd(proj, qg, kg, ltot, dmix, parts=(), kinds=()):
    S = proj.shape[0]
    n_comm = len(parts)
    T = ATTN_TILE
    n_tiles = S // T
    _, prefix_incl, prefix_excl = _tri_consts()
    pairs = HEADS // ATTN_HEADS_PER_STEP
    q_blk, k_blk, v_blk = 0, ATTN_WIDTH // ATTN_BLOCK, 2 * ATTN_WIDTH // ATTN_BLOCK
    scale = HEAD_DIM ** -0.5

    def body(*refs):
        q_ref, k_ref, qg_ref, kg_ref, lt_ref, do_ref, ti_ref, te_ref, start_ref, qn_s, kn_s, vb_s = refs[:12]
        p_refs = refs[12:12 + n_comm]
        dq_ref, dk_ref, dv_ref, dqg_ref, dkg_ref = refs[12 + n_comm:17 + n_comm]
        lb_refs = refs[17 + n_comm:17 + 2 * n_comm]
        dq_s, dk_s, dv_s = refs[17 + 2 * n_comm:20 + 2 * n_comm]
        sems = refs[20 + 2 * n_comm:]
        step_id = pl.program_id(0)

        @pl.when(pl.program_id(0) == 0)
        def _():
            dqg_ref[...] = jnp.zeros_like(dqg_ref)
            dkg_ref[...] = jnp.zeros_like(dkg_ref)
            for cp in _scatter_copies(p_refs, lb_refs, kinds, *sems) if n_comm else ():
                cp.start()

        heads = range(ATTN_HEADS_PER_STEP)
        dk_s[...] = jnp.zeros_like(dk_s)
        dv_s[...] = jnp.zeros_like(dv_s)

        def q_tile(qi, carry0):
            qrows = pl.ds(pl.multiple_of(qi * T, T), T)
            qt = [qn_s[h, qrows, :] for h in heads]
            dob = [do_ref[qrows, HEAD_DIM * h:HEAD_DIM * (h + 1)].astype(BF16) for h in heads]
            lt = [lt_ref[qrows, HEAD_DIM * h:HEAD_DIM * h + 1] for h in heads]

            def tile(kj, carry, diag):
                krows = pl.ds(pl.multiple_of(kj * T, T), T)
                kt = [kn_s[h, krows, :] for h in heads]
                s = [_dot_nt(qt[h], kt[h]) for h in heads]
                dw = [_dot_nt(dob[h], vb_s[h, krows, :]) for h in heads]
                terms = [_log_terms(s[h]) for h in heads]
                lom = [terms[h][0] for h in heads]
                if diag:
                    mask = _causal_mask()
                    lom = [jnp.where(mask, lom[h], 0.0) for h in heads]
                pin = [_dot(lom[h].astype(BF16), ti_ref[...]) for h in heads]
                w = [jnp.exp(terms[h][1] + ((lt[h] - carry[h][1]) - pin[h])) for h in heads]
                if diag:
                    w = [jnp.where(mask, w[h], 0.0) for h in heads]
                e = [w[h] * dw[h] for h in heads]
                gex = [_dot(e[h].astype(BF16), te_ref[...]) for h in heads]
                dz = [e[h] - jnp.exp(terms[h][1]) * (e[h] + (carry[h][2] + gex[h])) for h in heads]
                if diag:
                    dz = [jnp.where(mask, dz[h], 0.0) for h in heads]
                new = []
                for h in heads:
                    dzb = dz[h].astype(BF16)
                    dk_s[h, krows, :] += _dot_tn(dzb, qt[h])
                    dv_s[h, krows, :] += _dot_tn(w[h].astype(BF16), dob[h])
                    new.append((carry[h][0] + _dot(dzb, kt[h]),
                                carry[h][1] + pin[h][:, T - 1:T],
                                carry[h][2] + gex[h][:, T - 1:T] + e[h][:, T - 1:T]))
                return tuple(new)

            zero = jnp.zeros((T, 1), F32)
            init = tuple((jnp.zeros((T, HEAD_DIM), F32), zero, zero) for h in heads)
            first = jnp.clip(start_ref[step_id, qi].astype(jnp.int32), 0, qi)
            last = tile(qi, lax.fori_loop(first, qi, lambda kj, cr: tile(kj, cr, False), init), True)
            for h in heads:
                dq_s[h, qrows, :] = last[h][0]
            return carry0

        lax.fori_loop(0, n_tiles, q_tile, 0)

        def finish(i, carry):
            rows = pl.ds(pl.multiple_of(i * T, T), T)
            new = []
            for h in heads:
                lanes = slice(HEAD_DIM * h, HEAD_DIM * (h + 1))
                q = q_ref[rows, lanes]
                k = k_ref[rows, lanes]
                rq = lax.rsqrt(jnp.mean(q * q, axis=-1, keepdims=True) + EPS)
                rk = lax.rsqrt(jnp.mean(k * k, axis=-1, keepdims=True) + EPS)
                qh = q * rq
                kh = k * rk
                dqn = dq_s[h, rows, :] * scale
                dkn = -dk_s[h, rows, :]
                dqh = dqn * qg_ref[...]
                dkh = dkn * kg_ref[...]
                dq_ref[rows, lanes] = (rq * (dqh - qh * jnp.mean(dqh * qh, axis=-1, keepdims=True))).astype(BF16)
                dk_ref[rows, lanes] = (rk * (dkh - kh * jnp.mean(dkh * kh, axis=-1, keepdims=True))).astype(BF16)
                dv_ref[rows, lanes] = dv_s[h, rows, :].astype(BF16)
                new.append(carry[2 * h] + jnp.sum(dqn * qh, axis=0, keepdims=True))
                new.append(carry[2 * h + 1] + jnp.sum(dkn * kh, axis=0, keepdims=True))
            return tuple(new)

        zero = jnp.zeros((1, HEAD_DIM), F32)
        sums = lax.fori_loop(0, n_tiles, finish, (zero,) * (2 * len(heads)))
        dqg_ref[...] += sum(sums[0::2])
        dkg_ref[...] += sum(sums[1::2])
        if n_comm:
            @pl.when(pl.program_id(0) == pairs - 1)
            def _():
                for cp in _scatter_copies(p_refs, lb_refs, kinds, *sems):
                    cp.wait()

    blk = lambda off: pl.BlockSpec((S, ATTN_BLOCK), lambda p: (0, off + p))
    row64 = pl.BlockSpec((1, HEAD_DIM), lambda p: (0, 0))
    tri = pl.BlockSpec((T, T), lambda p: (0, 0))
    n_sem = n_comm * (N_CHIPS - 1)
    out = pl.pallas_call(
        body, name="attn_bwd_scatter" if n_comm else "attn_bwd", grid=(pairs,),
        in_specs=[blk(q_blk), blk(k_blk), row64, row64,
                  pl.BlockSpec((None, S, ATTN_BLOCK), lambda p: (p, 0, 0)), blk(0), tri, tri,
                  pl.BlockSpec(memory_space=pltpu.SMEM)]
        + [pl.BlockSpec((ATTN_HEADS_PER_STEP, S, HEAD_DIM), lambda p: (p, 0, 0))] * 3 + [ANY] * n_comm,
        out_specs=[blk(0), blk(0), blk(0), row64, row64] + [ANY] * n_comm,
        out_shape=[jax.ShapeDtypeStruct((S, ATTN_WIDTH), BF16)] * 3 + [jax.ShapeDtypeStruct((1, HEAD_DIM), F32)] * 2
        + _scatter_shapes(parts, kinds),
        scratch_shapes=[pltpu.VMEM((ATTN_HEADS_PER_STEP, S, HEAD_DIM), F32)] * 3
        + ([pltpu.SemaphoreType.DMA((n_sem,))] * 2 if n_comm else []),
        compiler_params=pltpu.CompilerParams(dimension_semantics=("arbitrary",), vmem_limit_bytes=VMEM_LIMIT_BYTES,
                                             has_side_effects=bool(n_comm)),
    )(proj, proj, qg, kg, ltot[0], dmix, prefix_incl, prefix_excl, *ltot[1:], *parts)
    return out[:5], list(out[5:])


def _shifted(win, n_rows):
    return [win if b == 0 else pltpu.roll(win, n_rows - b, 0) for b in range(SUBLANES)]


def _taps(variants, offsets, tm):
    return {o: variants[o % SUBLANES][(o // SUBLANES) * SUBLANES:(o // SUBLANES) * SUBLANES + tm, :] for o in offsets}


def _fold_rows(a):
    return jnp.sum(a.reshape(a.shape[0] // SUBLANES, SUBLANES, a.shape[1]), axis=0)


def _glu_conv_fwd(proj, w, bias, guest=None, *, tm=256):
    S = proj.shape[0]
    CB = LANES
    a_blk, b_blk = 3 * ATTN_WIDTH // CB, (3 * ATTN_WIDTH + CONV_WIDTH) // CB
    n_rows = tm + CONV_PAD

    def body(a_ref, b_ref, w_ref, bias_ref, c1_ref, pad_s):
        pad_s[0:CONV_PAD, :] = jnp.zeros((CONV_PAD, CB), F32)

        def fill(i, carry):
            rows = pl.ds(pl.multiple_of(i * tm, tm), tm)
            pad_s[pl.ds(pl.multiple_of(CONV_PAD + i * tm, SUBLANES), tm), :] = a_ref[rows, :] * _sigmoid(b_ref[rows, :])
            return carry

        lax.fori_loop(0, S // tm, fill, 0)

        def conv(i, carry):
            r0 = pl.multiple_of(i * tm, tm)
            taps = _taps(_shifted(pad_s[pl.ds(r0, n_rows), :], n_rows), range(2, 2 + CONV_KERNEL), tm)
            acc = jnp.broadcast_to(bias_ref[...], (tm, CB))
            for k in range(CONV_KERNEL):
                acc = acc + w_ref[k:k + 1, :] * taps[k + 2]
            c1_ref[pl.ds(r0, tm), :] = acc
            return carry

        lax.fori_loop(0, S // tm, conv, 0)

    (c1,), guest_out = _hosted_call(
        body, guest, name="glu_conv_fwd", grid=(CONV_WIDTH // CB,),
        in_specs=[pl.BlockSpec((S, CB), lambda j: (0, a_blk + j)), pl.BlockSpec((S, CB), lambda j: (0, b_blk + j)),
                  pl.BlockSpec((CONV_KERNEL, CB), lambda j: (0, j)), pl.BlockSpec((1, CB), lambda j: (0, j))],
        out_specs=[pl.BlockSpec((S, CB), lambda j: (0, j))],
        out_shape=[jax.ShapeDtypeStruct((S, CONV_WIDTH), F32)],
        scratch_shapes=[pltpu.VMEM((S + CONV_PAD, CB), F32)], operands=[proj, proj, w, bias])
    return c1, guest_out


def _glu_conv_bwd(proj, w, dc1, guest=None, *, tm=256):
    S = proj.shape[0]
    CB = LANES
    a_blk, b_blk = 3 * ATTN_WIDTH // CB, (3 * ATTN_WIDTH + CONV_WIDTH) // CB
    n_rows = tm + CONV_PAD

    def body(a_ref, b_ref, w_ref, dc1_ref, da_ref, db_ref, dw_ref, dbias_ref, pad_s, dpad_s, dw_s):
        pad_s[0:CONV_PAD, :] = jnp.zeros((CONV_PAD, CB), F32)
        dpad_s[S:S + CONV_PAD, :] = jnp.zeros((CONV_PAD, CB), F32)
        dw_s[...] = jnp.zeros_like(dw_s)

        def fill(i, carry):
            rows = pl.ds(pl.multiple_of(i * tm, tm), tm)
            pad_s[pl.ds(pl.multiple_of(CONV_PAD + i * tm, SUBLANES), tm), :] = a_ref[rows, :] * _sigmoid(b_ref[rows, :])
            dpad_s[rows, :] = dc1_ref[rows, :]
            return carry

        lax.fori_loop(0, S // tm, fill, 0)

        def conv(i, carry):
            r0 = pl.multiple_of(i * tm, tm)
            rows = pl.ds(r0, tm)
            taps = _taps(_shifted(dpad_s[pl.ds(r0, n_rows), :], n_rows), range(CONV_KERNEL), tm)
            acc = jnp.zeros((tm, CB), F32)
            for k in range(CONV_KERNEL):
                acc = acc + w_ref[k:k + 1, :] * taps[CONV_KERNEL - 1 - k]
            a = a_ref[rows, :]
            sg = _sigmoid(b_ref[rows, :])
            da_ref[rows, :] = (acc * sg).astype(BF16)
            db_ref[rows, :] = (acc * a * sg * (1.0 - sg)).astype(BF16)
            d = taps[0]
            taps = _taps(_shifted(pad_s[pl.ds(r0, n_rows), :], n_rows), range(2, 2 + CONV_KERNEL), tm)
            for k in range(CONV_KERNEL):
                dw_s[SUBLANES * k:SUBLANES * (k + 1), :] += _fold_rows(d * taps[k + 2])
            dw_s[SUBLANES * CONV_KERNEL:SUBLANES * (CONV_KERNEL + 1), :] += _fold_rows(d)
            return carry

        lax.fori_loop(0, S // tm, conv, 0)
        for k in range(CONV_KERNEL):
            dw_ref[k:k + 1, :] = jnp.sum(dw_s[SUBLANES * k:SUBLANES * (k + 1), :], axis=0, keepdims=True)
        dbias_ref[...] = jnp.sum(dw_s[SUBLANES * CONV_KERNEL:SUBLANES * (CONV_KERNEL + 1), :], axis=0, keepdims=True)

    col = lambda off: pl.BlockSpec((S, CB), lambda j: (0, off + j))
    out, guest_out = _hosted_call(
        body, guest, name="glu_conv_bwd", grid=(CONV_WIDTH // CB,),
        in_specs=[col(a_blk), col(b_blk), pl.BlockSpec((CONV_KERNEL, CB), lambda j: (0, j)), col(0)],
        out_specs=[col(0), col(0), pl.BlockSpec((CONV_KERNEL, CB), lambda j: (0, j)), pl.BlockSpec((1, CB), lambda j: (0, j))],
        out_shape=[jax.ShapeDtypeStruct((S, CONV_WIDTH), BF16)] * 2
        + [jax.ShapeDtypeStruct((CONV_KERNEL, CONV_WIDTH), F32), jax.ShapeDtypeStruct((1, CONV_WIDTH), F32)],
        scratch_shapes=[pltpu.VMEM((S + CONV_PAD, CB), F32), pltpu.VMEM((S + CONV_PAD, CB), F32),
                        pltpu.VMEM((SUBLANES * (CONV_KERNEL + 1), CB), F32)], operands=[proj, proj, w, dc1])
    return (*out, guest_out)


def _ln_stats(c1):
    mu = jnp.mean(c1, axis=-1, keepdims=True)
    xc = c1 - mu
    r = lax.rsqrt(jnp.mean(xc * xc, axis=-1, keepdims=True) + EPS)
    return xc * r, r


def _ln_silu_fwd(c1, g, b, *, tm=512):
    S, C = c1.shape

    def body(c1_ref, g_ref, b_ref, c_ref, ct_ref):
        yh, _ = _ln_stats(c1_ref[...])
        y = yh * g_ref[...] + b_ref[...]
        c = y * _sigmoid(y)
        c_ref[...] = c.astype(BF16)
        ct_ref[...] = c.T.astype(BF16)

    vec = pl.BlockSpec((1, C), lambda i: (0, 0))
    return pl.pallas_call(
        body, name="ln_silu_fwd", grid=(S // tm,),
        in_specs=[pl.BlockSpec((tm, C), lambda i: (i, 0)), vec, vec],
        out_specs=[pl.BlockSpec((tm, C), lambda i: (i, 0)), pl.BlockSpec((C, tm), lambda i: (0, i))],
        out_shape=[jax.ShapeDtypeStruct((S, C), BF16), jax.ShapeDtypeStruct((C, S), BF16)],
        compiler_params=_params("parallel"),
    )(c1, g, b)


def _ln_silu_bwd(c1, g, b, dmix, *, tm=512):
    S, C = c1.shape

    def body(c1_ref, g_ref, b_ref, dc_ref, dc1_ref, dg_ref, db_ref):
        yh, r = _ln_stats(c1_ref[...])
        y = yh * g_ref[...] + b_ref[...]
        sg = _sigmoid(y)
        dy = dc_ref[...] * (sg * (1.0 + y * (1.0 - sg)))
        dyh = dy * g_ref[...]
        dc1_ref[...] = r * (dyh - jnp.mean(dyh, axis=-1, keepdims=True)
                            - yh * jnp.mean(dyh * yh, axis=-1, keepdims=True))

        @pl.when(pl.program_id(0) == 0)
        def _():
            dg_ref[...] = jnp.zeros_like(dg_ref)
            db_ref[...] = jnp.zeros_like(db_ref)

        dg_ref[...] += jnp.sum(dy * yh, axis=0, keepdims=True)
        db_ref[...] += jnp.sum(dy, axis=0, keepdims=True)

    vec = pl.BlockSpec((1, C), lambda i: (0, 0))
    return pl.pallas_call(
        body, name="ln_silu_bwd", grid=(S // tm,),
        in_specs=[pl.BlockSpec((tm, C), lambda i: (i, 0)), vec, vec, pl.BlockSpec((tm, C), lambda i: (i, 1))],
        out_specs=[pl.BlockSpec((tm, C), lambda i: (i, 0)), vec, vec],
        out_shape=[jax.ShapeDtypeStruct((S, C), F32), jax.ShapeDtypeStruct((1, C), F32), jax.ShapeDtypeStruct((1, C), F32)],
        compiler_params=_params("arbitrary"),
    )(c1, g, b, dmix)


FFN_CB = 256


def _ffn_gate(pad_s, w_ref, bias_ref, r0, tm):
    n_rows = tm + FFN_PAD
    taps = _taps(_shifted(pad_s[pl.ds(r0, n_rows), :], n_rows), range(FFN_PAD - 2, FFN_PAD + 1), tm)
    g1 = bias_ref[...] + w_ref[0:1, :] * taps[6] + w_ref[1:2, :] * taps[7] + w_ref[2:3, :] * taps[8]
    return g1, taps


def _ffn_act_fwd(u, w, bias, guest=None, *, tm=256):
    S = u.shape[0]
    CB = FFN_CB
    nb = D_FF // CB

    def body(g_ref, v_ref, w_ref, bias_ref, o_ref, ot_ref, pad_s):
        pad_s[0:FFN_PAD, :] = jnp.zeros((FFN_PAD, CB), F32)

        def fill(i, carry):
            pad_s[pl.ds(pl.multiple_of(FFN_PAD + i * tm, SUBLANES), tm), :] = g_ref[pl.ds(pl.multiple_of(i * tm, tm), tm), :].astype(F32)
            return carry

        lax.fori_loop(0, S // tm, fill, 0)

        def act(i, carry):
            r0 = pl.multiple_of(i * tm, tm)
            g1, _ = _ffn_gate(pad_s, w_ref, bias_ref, r0, tm)
            a = g1 * _sigmoid(g1) * v_ref[pl.ds(r0, tm), :].astype(F32)
            o_ref[pl.ds(r0, tm), :] = a.astype(BF16)
            ot_ref[:, pl.ds(r0, tm)] = a.T.astype(BF16)
            return carry

        lax.fori_loop(0, S // tm, act, 0)

    (act, actt), guest_out = _hosted_call(
        body, guest, name="ffn_act_fwd", grid=(nb,),
        in_specs=[pl.BlockSpec((S, CB), lambda j: (0, j)), pl.BlockSpec((S, CB), lambda j: (0, nb + j)),
                  pl.BlockSpec((FFN_KERNEL, CB), lambda j: (0, j)), pl.BlockSpec((1, CB), lambda j: (0, j))],
        out_specs=[pl.BlockSpec((S, CB), lambda j: (0, j)), pl.BlockSpec((CB, S), lambda j: (j, 0))],
        out_shape=[jax.ShapeDtypeStruct((S, D_FF), BF16), jax.ShapeDtypeStruct((D_FF, S), BF16)],
        scratch_shapes=[pltpu.VMEM((S + FFN_PAD, CB), F32)], operands=[u, u, w, bias])
    return act, actt, guest_out


def _ffn_act_bwd(u, w, bias, dact, guest=None, *, tm=256):
    S = u.shape[0]
    CB = FFN_CB
    nb = D_FF // CB

    def body(g_ref, v_ref, w_ref, bias_ref, da_ref, dg_ref, dv_ref, dw_ref, dbias_ref, pad_s, dpad_s, dw_s):
        pad_s[0:FFN_PAD, :] = jnp.zeros((FFN_PAD, CB), F32)
        dpad_s[S:S + FFN_PAD, :] = jnp.zeros((FFN_PAD, CB), F32)
        dw_s[...] = jnp.zeros_like(dw_s)

        def fill(i, carry):
            pad_s[pl.ds(pl.multiple_of(FFN_PAD + i * tm, SUBLANES), tm), :] = g_ref[pl.ds(pl.multiple_of(i * tm, tm), tm), :].astype(F32)
            return carry

        lax.fori_loop(0, S // tm, fill, 0)

        def first(i, carry):
            r0 = pl.multiple_of(i * tm, tm)
            rows = pl.ds(r0, tm)
            g1, taps = _ffn_gate(pad_s, w_ref, bias_ref, r0, tm)
            sg = _sigmoid(g1)
            da = da_ref[rows, :].astype(F32)
            dv_ref[rows, :] = (da * g1 * sg).astype(BF16)
            dg1 = da * v_ref[rows, :].astype(F32) * (sg * (1.0 + g1 * (1.0 - sg)))
            dpad_s[rows, :] = dg1
            for k in range(FFN_KERNEL):
                dw_s[SUBLANES * k:SUBLANES * (k + 1), :] += _fold_rows(dg1 * taps[FFN_PAD - 2 + k])
            dw_s[SUBLANES * FFN_KERNEL:SUBLANES * (FFN_KERNEL + 1), :] += _fold_rows(dg1)
            return carry

        lax.fori_loop(0, S // tm, first, 0)

        def second(i, carry):
            r0 = pl.multiple_of(i * tm, tm)
            n_rows = tm + FFN_PAD
            taps = _taps(_shifted(dpad_s[pl.ds(r0, n_rows), :], n_rows), range(FFN_KERNEL), tm)
            dg_ref[pl.ds(r0, tm), :] = (w_ref[2:3, :] * taps[0] + w_ref[1:2, :] * taps[1] + w_ref[0:1, :] * taps[2]).astype(BF16)
            return carry

        lax.fori_loop(0, S // tm, second, 0)
        for k in range(FFN_KERNEL):
            dw_ref[k:k + 1, :] = jnp.sum(dw_s[SUBLANES * k:SUBLANES * (k + 1), :], axis=0, keepdims=True)
        dbias_ref[...] = jnp.sum(dw_s[SUBLANES * FFN_KERNEL:SUBLANES * (FFN_KERNEL + 1), :], axis=0, keepdims=True)

    col = lambda off: pl.BlockSpec((S, CB), lambda j: (0, off + j))
    wspec = pl.BlockSpec((FFN_KERNEL, CB), lambda j: (0, j))
    bspec = pl.BlockSpec((1, CB), lambda j: (0, j))
    out, guest_out = _hosted_call(
        body, guest, name="ffn_act_bwd", grid=(nb,),
        in_specs=[col(0), col(nb), wspec, bspec, col(0)],
        out_specs=[col(0), col(0), wspec, bspec],
        out_shape=[jax.ShapeDtypeStruct((S, D_FF), BF16)] * 2
        + [jax.ShapeDtypeStruct((FFN_KERNEL, D_FF), F32), jax.ShapeDtypeStruct((1, D_FF), F32)],
        scratch_shapes=[pltpu.VMEM((S + FFN_PAD, CB), F32), pltpu.VMEM((S + FFN_PAD, CB), F32),
                        pltpu.VMEM((SUBLANES * (FFN_KERNEL + 1), CB), F32)], operands=[u, u, w, bias, dact])
    return (*out, guest_out)


def _loss_grad(y, target, *, tm=512):
    S, D = y.shape

    def body(y_ref, t_ref, dy_ref, l_ref):
        d = y_ref[...] - t_ref[...]
        dy_ref[...] = d * (1.0 / D)

        @pl.when(pl.program_id(0) == 0)
        def _():
            l_ref[...] = jnp.zeros_like(l_ref)

        l_ref[...] += 0.5 * jnp.sum(jnp.mean(d * d, axis=-1, keepdims=True), axis=0, keepdims=True)

    dy, l = pl.pallas_call(
        body, name="loss_grad", grid=(S // tm,),
        in_specs=[pl.BlockSpec((tm, D), lambda i: (i, 0))] * 2,
        out_specs=[pl.BlockSpec((tm, D), lambda i: (i, 0)), pl.BlockSpec((SUBLANES, LANES), lambda i: (0, 0))],
        out_shape=[jax.ShapeDtypeStruct((S, D), F32), jax.ShapeDtypeStruct((SUBLANES, LANES), F32)],
        compiler_params=_params("arbitrary"),
    )(y, target)
    return dy, l[0, 0]


def _row_tile(rows, cap=512):
    t = min(rows, cap)
    while rows % t or t % SUBLANES:
        t -= 1
    return t


def _adam_update(w, g, m, v):
    m1 = ADAM_B1 * m + (1.0 - ADAM_B1) * g
    v1 = ADAM_B2 * v + (1.0 - ADAM_B2) * (g * g)
    m_hat = m1 / (1.0 - ADAM_B1 ** ADAM_STEP)
    v_hat = v1 / (1.0 - ADAM_B2 ** ADAM_STEP)
    return -ADAM_LR * (m_hat / (jnp.sqrt(v_hat) + ADAM_EPS) + ADAM_WD * w), m1, v1


def _adamw(w, g, m, v):
    R, C = w.shape
    tr = _row_tile(R, 256)

    def body(w_ref, g_ref, m_ref, v_ref, d_ref, nm_ref, nv_ref):
        d_ref[...], nm_ref[...], nv_ref[...] = _adam_update(w_ref[...], g_ref[...], m_ref[...], v_ref[...])

    spec = pl.BlockSpec((tr, C), lambda i: (i, 0))
    return pl.pallas_call(
        body, name="adamw", grid=(R // tr,),
        in_specs=[spec] * 4, out_specs=[spec] * 3,
        out_shape=[jax.ShapeDtypeStruct((R, C), F32)] * 3,
        compiler_params=_params("parallel"),
    )(w, g, m, v)


def _adamw_stacked(w, gs, m, v):
    L, R, C = w.shape
    tr = _row_tile(R, 256)

    def body(w_ref, m_ref, v_ref, *rest):
        g_refs, (go_ref, d_ref, nm_ref, nv_ref) = rest[:L], rest[L:]
        for a, g_ref in enumerate(g_refs):
            @pl.when(pl.program_id(0) == a)
            def _():
                gv = g_ref[...]
                go_ref[...] = gv
                d_ref[...], nm_ref[...], nv_ref[...] = _adam_update(w_ref[...], gv, m_ref[...], v_ref[...])

    stacked = pl.BlockSpec((None, tr, C), lambda l, i: (l, i, 0))
    return pl.pallas_call(
        body, name="adamw_stacked", grid=(L, R // tr),
        in_specs=[stacked] * 3 + [pl.BlockSpec((tr, C), lambda l, i, a=a: (jnp.where(l == a, i, 0), 0)) for a in range(L)],
        out_specs=[stacked] * 4, out_shape=[jax.ShapeDtypeStruct((L, R, C), F32)] * 4,
        compiler_params=_params("parallel", "parallel"),
    )(w, m, v, *gs)


CAST_STEPS = 4


def _cast_into_full(ws, layers, kinds, chip):
    n = len(ws)

    def body(chip_ref, *refs):
        for w_ref, o_ref in zip(refs[:n], refs[n:]):
            o_ref[...] = w_ref[...].astype(BF16)

    in_specs, out_specs, out_shape = [], [], []
    for w, layer, kind in zip(ws, layers, kinds):
        _, R, C = w.shape
        tr = R // CAST_STEPS
        assert tr % 16 == 0
        in_specs.append(pl.BlockSpec((None, tr, C), lambda i, chip_ref, layer=layer: (layer, i, 0)))
        if kind == "col":
            out_shape.append(jax.ShapeDtypeStruct((1, R, N_CHIPS * C), BF16))
            out_specs.append(pl.BlockSpec((None, tr, C), lambda i, chip_ref: (0, i, chip_ref[0])))
        else:
            out_shape.append(jax.ShapeDtypeStruct((1, N_CHIPS * R, C), BF16))
            out_specs.append(pl.BlockSpec((None, tr, C), lambda i, chip_ref: (0, chip_ref[0] * CAST_STEPS + i, 0)))
    return pl.pallas_call(
        body, name="cast_into_full",
        grid_spec=pltpu.PrefetchScalarGridSpec(num_scalar_prefetch=1, grid=(CAST_STEPS,),
                                               in_specs=in_specs, out_specs=out_specs),
        out_shape=out_shape, compiler_params=_params("parallel"),
    )(chip, *ws)


ADD_HALF_STEPS = 8
ADD_PARTS_STEPS = 2


def _add_half(g4s, las, c):
    n = len(g4s)

    def body(c_ref, *refs):
        for g_ref, la_ref, o_ref in zip(refs[:n], refs[n:2 * n], refs[2 * n:]):
            o_ref[...] = (g_ref[...].astype(F32) + la_ref[...].astype(F32)).astype(BF16)

    g_specs, la_specs, out_shape = [], [], []
    for g4 in g4s:
        L, _, H, W = g4.shape
        th = L * H // ADD_HALF_STEPS
        per = H // th
        assert th % 16 == 0 and per * th == H
        g_specs.append(pl.BlockSpec((None, None, th, W), lambda i, c_ref, per=per: (i // per, c_ref[0], i % per, 0)))
        la_specs.append(pl.BlockSpec((None, th, W), lambda i, c_ref, per=per: (i // per, i % per, 0)))
        out_shape.append(jax.ShapeDtypeStruct((L, H, W), BF16))
    return pl.pallas_call(
        body, name="add_half",
        grid_spec=pltpu.PrefetchScalarGridSpec(num_scalar_prefetch=1, grid=(ADD_HALF_STEPS,),
                                               in_specs=g_specs + la_specs, out_specs=la_specs),
        out_shape=out_shape, compiler_params=_params("parallel"),
    )(c, *g4s, *las)


def _add_parts(ps, lbs, place, kinds):
    n = len(ps)

    def body(s_ref, *refs):
        for p_ref, lb_ref, o_ref in zip(refs[:n], refs[n:2 * n], refs[2 * n:]):
            acc = p_ref[...].astype(F32)
            for k in range(N_CHIPS - 1):
                acc = acc + lb_ref[k].astype(F32)
            o_ref[...] = acc

    p_specs, lb_specs, out_specs, out_shape = [], [], [], []
    for lb, kind in zip(lbs, kinds):
        _, L, H, C = lb.shape
        th = H // ADD_PARTS_STEPS
        assert L == 1 and th % 16 == 0
        if kind == "col":
            p_specs.append(pl.BlockSpec((None, th, C), lambda i, s_ref: (0, i, s_ref[0])))
        else:
            p_specs.append(pl.BlockSpec((None, None, th, C), lambda i, s_ref: (0, s_ref[0], i, 0)))
        lb_specs.append(pl.BlockSpec((N_CHIPS - 1, None, th, C), lambda i, s_ref: (0, 0, i, 0)))
        out_specs.append(pl.BlockSpec((None, None, th, C), lambda i, s_ref: (0, s_ref[1], i, 0)))
        out_shape.append(jax.ShapeDtypeStruct((L, 2, H, C), F32))
    return pl.pallas_call(
        body, name="add_parts",
        grid_spec=pltpu.PrefetchScalarGridSpec(num_scalar_prefetch=1, grid=(ADD_PARTS_STEPS,),
                                               in_specs=p_specs + lb_specs, out_specs=out_specs),
        out_shape=out_shape, compiler_params=_params("parallel"),
    )(place, *ps, *lbs)


ANY = pl.BlockSpec(memory_space=pl.ANY)


def _place():
    x, y, c = lax.axis_index("x"), lax.axis_index("y"), lax.axis_index("c")
    chips = [(1 - x, y), (x, 1 - y), (1 - x, 1 - y)]
    return x, y, c, chips


def _comm_call(body, name, ins, out_shape, n_remote, n_local, aliases=None):
    scratch = [pltpu.SemaphoreType.DMA((n_remote,)), pltpu.SemaphoreType.DMA((n_remote,))]
    if n_local:
        scratch.append(pltpu.SemaphoreType.DMA((n_local,)))
    return pl.pallas_call(
        body, name=name, in_specs=[ANY] * len(ins), out_specs=[ANY] * len(out_shape), out_shape=out_shape,
        scratch_shapes=scratch, input_output_aliases=aliases or {},
        compiler_params=pltpu.CompilerParams(has_side_effects=True),
    )(*ins)


def _remote(src, dst, send, recv, k, to):
    return pltpu.make_async_remote_copy(src_ref=src, dst_ref=dst, send_sem=send.at[k], recv_sem=recv.at[k],
                                        device_id=to, device_id_type=MESH)


def _gather_weights(fulls, kinds):
    n = len(fulls)
    out_shape = [jax.ShapeDtypeStruct(f.shape, f.dtype) for f in fulls]

    def body(*refs):
        outs, (send, recv) = refs[n:2 * n], refs[2 * n:]
        first = _gather_copies(outs, kinds, send, recv, 0)
        for cp in first:
            cp.start()
        for cp in first:
            cp.wait()
        passed = _pass_on_copies(outs, kinds, send, recv, len(first))
        for cp in passed:
            cp.start()
        for cp in passed:
            cp.wait()

    return _comm_call(body, "gather_weights", fulls, out_shape, 2 * n * (N_CHIPS - 1), 0, {a: a for a in range(n)})


def _window(ref, kind, s, h):
    if kind == "col":
        H, C = ref.shape[1] // 2, ref.shape[2] // N_CHIPS
        return ref.at[:, pl.ds(pl.multiple_of(h * H, 16), H), pl.ds(pl.multiple_of(s * C, LANES), C)]
    R = ref.shape[1] // N_CHIPS
    return ref.at[:, pl.ds(pl.multiple_of(s * R + h * (R // 2), 16), R // 2), :]


def _gather_copies(outs, kinds, send, recv, sem0):
    x, y, c, chips = _place()
    me = 2 * x + y
    return [_remote(_window(o, kind, me, c), _window(o, kind, me, c), send, recv, sem0 + a * (N_CHIPS - 1) + k, (*chip, c))
            for a, (o, kind) in enumerate(zip(outs, kinds)) for k, chip in enumerate(chips)]


def _pass_on_copies(outs, kinds, send, recv, sem0):
    x, y, c, chips = _place()
    cps = []
    for a, (o, kind) in enumerate(zip(outs, kinds)):
        for k, chip in enumerate(chips):
            landed = _window(o, kind, 2 * chip[0] + chip[1], c)
            cps.append(_remote(landed, landed, send, recv, sem0 + a * (N_CHIPS - 1) + k, (x, y, 1 - c)))
    return cps


def _gather_small(shards):
    n = len(shards)
    out_shape = [jax.ShapeDtypeStruct((N_CHIPS,) + s.shape, s.dtype) for s in shards]

    def body(*refs):
        srcs, outs, (send, recv, loc) = refs[:n], refs[n:2 * n], refs[2 * n:]
        x, y, c, chips = _place()
        me = 2 * x + y
        remote, local = [], []
        for a in range(n):
            local.append(pltpu.make_async_copy(srcs[a], outs[a].at[me], loc.at[a]))
            for k, chip in enumerate(chips):
                remote.append(_remote(srcs[a], outs[a].at[me], send, recv, a * (N_CHIPS - 1) + k, (*chip, c)))
        for cp in local + remote:
            cp.start()
        for cp in remote + local:
            cp.wait()

    return _comm_call(body, "gather_small", shards, out_shape, n * (N_CHIPS - 1), n)


def _exchange_halves(g4s):
    n = len(g4s)
    out_shape = [jax.ShapeDtypeStruct((g.shape[0],) + g.shape[2:], g.dtype) for g in g4s]

    def body(*refs):
        gs, las, (send, recv) = refs[:n], refs[n:2 * n], refs[2 * n:]
        x, y, c, _ = _place()
        cps = [_remote(gs[a].at[:, 1 - c], las[a], send, recv, a, (x, y, 1 - c)) for a in range(n)]
        for cp in cps:
            cp.start()
        for cp in cps:
            cp.wait()

    return _comm_call(body, "exchange_halves", g4s, out_shape, n, 0)


def _scatter_partials(ps, kinds):
    n = len(ps)

    def body(*refs):
        srcs, lbs, (send, recv) = refs[:n], refs[n:2 * n], refs[2 * n:]
        cps = _scatter_copies(srcs, lbs, kinds, send, recv)
        for cp in cps:
            cp.start()
        for cp in cps:
            cp.wait()

    return _comm_call(body, "scatter_partials", ps, _scatter_shapes(ps, kinds), n * (N_CHIPS - 1), 0)


def _scatter_shapes(ps, kinds):
    out_shape = []
    for p, kind in zip(ps, kinds):
        L, H, C = (p.shape[0], p.shape[1], p.shape[2] // N_CHIPS) if kind == "col" else (p.shape[0], p.shape[2], p.shape[3])
        out_shape.append(jax.ShapeDtypeStruct((N_CHIPS - 1, L, H, C), p.dtype))
    return out_shape


def _scatter_copies(srcs, lbs, kinds, send, recv, sem0=0):
    x, y, c, chips = _place()
    cps = []
    for a, (src, lb, kind) in enumerate(zip(srcs, lbs, kinds)):
        C = lb.shape[3]
        for k, chip in enumerate(chips):
            s = 2 * chip[0] + chip[1]
            part = src.at[:, :, pl.ds(pl.multiple_of(s * C, LANES), C)] if kind == "col" else src.at[:, s]
            cps.append(_remote(part, lb.at[k], send, recv, sem0 + a * (N_CHIPS - 1) + k, (*chip, c)))
    return cps


def _share_halves(g4s):
    n = len(g4s)
    out_shape = [jax.ShapeDtypeStruct(g.shape, g.dtype) for g in g4s]

    def body(*refs):
        outs, (send, recv) = refs[n:2 * n], refs[2 * n:]
        x, y, c, _ = _place()
        cps = [_remote(outs[a].at[:, c], outs[a].at[:, c], send, recv, a, (x, y, 1 - c)) for a in range(n)]
        for cp in cps:
            cp.start()
        for cp in cps:
            cp.wait()

    return _comm_call(body, "share_halves", g4s, out_shape, n, 0, {a: a for a in range(n)})


def _allreduce_small(part):
    N = part.shape[0]

    def body(p_ref, o_ref, buf, send, recv):
        x, y, c, _ = _place()
        me = 4 * x + 2 * y + c
        buf[me] = p_ref[...]
        cps = []
        for k in range(1, N_DEV):
            peer = (1 - x if k & 4 else x, 1 - y if k & 2 else y, 1 - c if k & 1 else c)
            cps.append(_remote(p_ref, buf.at[me], send, recv, k - 1, peer))
        for cp in cps:
            cp.start()
        for cp in cps:
            cp.wait()
        acc = buf[0]
        for i in range(1, N_DEV):
            acc = acc + buf[i]
        o_ref[...] = acc

    vmem = pl.BlockSpec(memory_space=pltpu.VMEM)
    return pl.pallas_call(
        body, name="allreduce_small", in_specs=[vmem], out_specs=vmem,
        out_shape=jax.ShapeDtypeStruct((N, LANES), F32),
        scratch_shapes=[pltpu.VMEM((N_DEV, N, LANES), F32), pltpu.SemaphoreType.DMA((N_DEV - 1,)),
                        pltpu.SemaphoreType.DMA((N_DEV - 1,))],
        compiler_params=pltpu.CompilerParams(has_side_effects=True, vmem_limit_bytes=VMEM_LIMIT_BYTES),
    )(part)


AFTER_ATTENTION = ("w_out", "w_up", "w_down")


def _layer_fwd(x, p, full=None, l=0):
    comm = full is not None

    def guest(first, second):
        first, second = ([k for k in ks if comm and k[0] < len(full)] for ks in (first, second))
        if not first + second:
            return None, []
        return _gather_guest([full[i][n] for i, n in first], [BIG_KIND[n] for _, n in first],
                             [full[i][n] for i, n in second], [BIG_KIND[n] for _, n in second]), first + second

    def done(keys, bufs):
        for (i, n), buf in zip(keys, bufs):
            full[i][n] = buf

    def weight(n):
        return full[l][n][0] if comm else p[n]

    in_sweep = ([(l, "w_up")] if l else [(l, n) for n in AFTER_ATTENTION]) if comm else []
    g, keys = guest([], [(l, "w_out")] if l else [])
    h1t, proj, out = _norm_matmul(x, p["norm1_g"], weight("w_in"), g)
    done(keys, out)
    attn, ltot, attnt, out = _attn_fwd(proj, p["q_norm_g"], p["k_norm_g"],
                                       [full[i][n] for i, n in in_sweep], [BIG_KIND[n] for _, n in in_sweep])
    done(in_sweep, out)
    g, keys = guest([], in_sweep)
    c1, out = _glu_conv_fwd(proj, p["conv_dw_w"], p["conv_dw_b"], g)
    done(keys, out)
    c, ct = _ln_silu_fwd(c1, p["conv_ln_g"], p["conv_ln_b"])
    x_mid, _ = _matmul_res([attn, c], weight("w_out"), x)
    g, keys = guest([(l + 1, "w_down")], [])
    h2t, u, out = _norm_matmul(x_mid, p["norm2_g"], weight("w_up"), g)
    done(keys, out)
    g, keys = guest([(l + 1, "w_in")], [(l + 1, "w_down")])
    act, actt, out = _ffn_act_fwd(u, p["ffn_dw_w"], p["ffn_dw_b"], g)
    done(keys, out)
    g, keys = guest([(l + 1, "w_out")], [(l + 1, "w_in")])
    x_out, out = _matmul_res([act], weight("w_down"), x_mid, g)
    done(keys, out)
    saved = dict(x=x, h1t=h1t, proj=proj, attnt=attnt, ltot=ltot, c1=c1, ct=ct, x_mid=x_mid, h2t=h2t, u=u, actt=actt)
    return x_out, saved


def _chip_partials(gs, kinds, core):
    g4s = _halves_view(gs, kinds)
    return _add_halves(g4s, _exchange_halves(g4s), kinds, core)


def _halves_view(gs, kinds):
    return [g.reshape(1, 2, g.shape[0] // 2, g.shape[1]) if kind == "col"
            else g.reshape(N_CHIPS, 2, g.shape[0] // N_CHIPS // 2, g.shape[1]) for g, kind in zip(gs, kinds)]


def _add_halves(g4s, received, kinds, core):
    parts = _add_half(g4s, received, core)
    return [p if kind == "col" else p.reshape(1, N_CHIPS, p.shape[1], p.shape[2]) for p, kind in zip(parts, kinds)]


def _gather_guest(first, first_kinds, second, second_kinds):
    bufs = list(first) + list(second)
    n1 = len(first)

    def copies(ins, outs, send, recv, sem0=0):
        return (_gather_copies(outs[:n1], first_kinds, send, recv, sem0)
                + _pass_on_copies(outs[n1:], second_kinds, send, recv, sem0 + n1 * (N_CHIPS - 1)))

    return _Guest("gather", bufs, [jax.ShapeDtypeStruct(f.shape, f.dtype) for f in bufs],
                  {a: a for a in range(len(bufs))}, copies, len(bufs) * (N_CHIPS - 1))


def _scatter_guest(parts, kinds):
    return _Guest("scatter", list(parts), _scatter_shapes(parts, kinds), {},
                  lambda ins, outs, send, recv, sem0=0: _scatter_copies(ins, outs, kinds, send, recv, sem0),
                  len(parts) * (N_CHIPS - 1))


def _exchange_guest(g4s):
    def copies(ins, outs, send, recv, sem0=0):
        x, y, c, _ = _place()
        return [_remote(g.at[:, 1 - c], la, send, recv, sem0 + a, (x, y, 1 - c)) for a, (g, la) in enumerate(zip(ins, outs))]

    out_shape = [jax.ShapeDtypeStruct((g.shape[0],) + g.shape[2:], g.dtype) for g in g4s]
    return _Guest("exchange", list(g4s), out_shape, {}, copies, len(g4s))


def _share_guest(g4s):
    def copies(ins, outs, send, recv, sem0=0):
        x, y, c, _ = _place()
        return [_remote(o.at[:, c], o.at[:, c], send, recv, sem0 + a, (x, y, 1 - c)) for a, o in enumerate(outs)]

    n = len(g4s)
    return _Guest("share", list(g4s), [jax.ShapeDtypeStruct(g.shape, g.dtype) for g in g4s], {a: a for a in range(n)}, copies, n)


def _owned_sums(parts, landed, kinds, place):
    halves = _add_parts(parts, landed, place, kinds)
    return [g.reshape(2 * g.shape[2], g.shape[3]) for g in _share_halves(halves)]


def _layer_bwd(dx_out, s, p, comm=None):
    g = {}
    on = comm is not None
    core, place, pending = comm if on else (None, None, [])
    n_p = len(pending)
    finished = lambda arrays: [a.reshape(2 * a.shape[2], a.shape[3]) for a in arrays]

    if n_p:
        g4s_p = _halves_view(pending, ["col"] * n_p)
    dact, received = _matmul_nt([dx_out], p["w_down"], BF16, _exchange_guest(g4s_p) if n_p else None)
    if n_p:
        parts_p = _add_halves(g4s_p, received, ["col"] * n_p, core)
    g["w_down"] = _matmul_tn([s["actt"]], [dx_out], tk=D_FF // 2, tn=D_MODEL)
    dgate, dval, g["ffn_dw_w"], g["ffn_dw_b"], landed_p = _ffn_act_bwd(
        s["u"], p["ffn_dw_w"], p["ffn_dw_b"], dact, _scatter_guest(parts_p, ["col"] * n_p) if n_p else None)
    guest = _share_guest(_add_parts(parts_p, landed_p, place, ["col"] * n_p)) if n_p else None
    dx_mid, g["norm2_g"], shared_p = _matmul_nt_rmsbwd([dgate, dval], p["w_up"], s["x_mid"], p["norm2_g"], dx_out, guest)
    g["w_up"] = _matmul_tn([s["h2t"]], [dgate, dval], tk=D_MODEL, tn=D_FF // 2)
    dmix, _ = _matmul_nt([dx_mid], p["w_out"], F32)
    g["w_out"] = _matmul_tn([s["attnt"], s["ct"]], [dx_mid], tk=ATTN_WIDTH, tn=D_MODEL)
    dc1, g["conv_ln_g"], g["conv_ln_b"] = _ln_silu_bwd(s["c1"], p["conv_ln_g"], p["conv_ln_b"], dmix)

    late = AFTER_ATTENTION
    parts, kinds = [], []
    if on:
        kinds = [BIG_KIND[n] for n in late]
        g4s = _halves_view([g[n] for n in late], kinds)
    da, db, g["conv_dw_w"], g["conv_dw_b"], received = _glu_conv_bwd(
        s["proj"], p["conv_dw_w"], dc1, _exchange_guest(g4s) if on else None)
    if on:
        parts = _add_halves(g4s, received, kinds, core)
    (dq, dk, dv, g["q_norm_g"], g["k_norm_g"]), landed = _attn_bwd(
        s["proj"], p["q_norm_g"], p["k_norm_g"], s["ltot"], dmix, parts, kinds)
    guest = _share_guest(_add_parts(parts, landed, place, kinds)) if on else None
    pieces = [dq, dk, dv, da, db]
    dx, g["norm1_g"], shared = _matmul_nt_rmsbwd(pieces, p["w_in"], s["x"], p["norm1_g"], dx_mid, guest)
    g["w_in"] = _matmul_tn([s["h1t"]], pieces, tk=D_MODEL, tn=ATTN_WIDTH)
    sums = {}
    if on:
        sums = dict(zip(late, finished(shared)), pending=finished(shared_p))
    return dx, g, sums


WEIGHTS = ("norm1_g", "w_in", "q_norm_g", "k_norm_g", "conv_dw_w", "conv_dw_b", "conv_ln_g", "conv_ln_b",
           "w_out", "norm2_g", "w_up", "ffn_dw_w", "ffn_dw_b", "w_down")
BIG = ("w_in", "w_out", "w_up", "w_down")
BIG_KIND = {"w_in": "col", "w_out": "row", "w_up": "col", "w_down": "row"}
SMALL_SHARDED = ("conv_dw_w", "ffn_dw_w")
REPLICATED = tuple(n for n in WEIGHTS if n not in BIG + SMALL_SHARDED)


def _pack(arrays):
    flat = jnp.concatenate([a.reshape(-1) for a in arrays])
    rows = -(-flat.shape[0] // (SUBLANES * LANES)) * SUBLANES
    return jnp.pad(flat, (0, rows * LANES - flat.shape[0])).reshape(rows, LANES)


def _unpack(packed, shapes):
    flat = packed.reshape(-1)
    out, off = [], 0
    for shape in shapes:
        size = 1
        for d in shape:
            size *= d
        out.append(flat[off:off + size].reshape(shape))
        off += size
    return out


def _unshard_last(stacked):
    n, L, K, C = stacked.shape
    return jnp.transpose(stacked, (1, 2, 0, 3)).reshape(L, K, n * C)


def kernel(x, norm1_g, w_in, q_norm_g, k_norm_g, conv_dw_w, conv_dw_b, conv_ln_g, conv_ln_b, w_out, norm2_g, w_up, ffn_dw_w, ffn_dw_b, w_down, loss_target, m_norm1_g, m_w_in, m_q_norm_g, m_k_norm_g, m_conv_dw_w, m_conv_dw_b, m_conv_ln_g, m_conv_ln_b, m_w_out, m_norm2_g, m_w_up, m_ffn_dw_w, m_ffn_dw_b, m_w_down, v_norm1_g, v_w_in, v_q_norm_g, v_k_norm_g, v_conv_dw_w, v_conv_dw_b, v_conv_ln_g, v_conv_ln_b, v_w_out, v_norm2_g, v_w_up, v_ffn_dw_w, v_ffn_dw_b, v_w_down):
    given = dict(locals())
    w = {n: given[n] for n in WEIGHTS}
    m = {n: given["m_" + n] for n in WEIGHTS}
    v = {n: given["v_" + n] for n in WEIGHTS}
    chip = 2 * lax.axis_index("x") + lax.axis_index("y")
    core = lax.axis_index("c")
    chip_arr = jnp.reshape(chip, (1,)).astype(jnp.int32)
    core_arr = jnp.reshape(core, (1,)).astype(jnp.int32)
    L = DEPTH

    place = jnp.concatenate([chip_arr, core_arr])

    first = _cast_into_full([w["w_in"]], [0], ["col"], chip_arr)
    rest = [(l, n) for l in range(L) for n in BIG if (l, n) != (0, "w_in")]
    cast = dict(zip(rest, _cast_into_full([w[n] for _, n in rest], [l for l, _ in rest],
                                          [BIG_KIND[n] for _, n in rest], chip_arr)))
    cast[0, "w_in"] = first[0]
    full = [{n: cast[l, n] for n in BIG} for l in range(L)]
    full[0]["w_in"] = _gather_weights([full[0]["w_in"]], ["col"])[0]
    small_full = {n: _unshard_last(stacked)
                  for n, stacked in zip(SMALL_SHARDED, _gather_small([w[n] for n in SMALL_SHARDED]))}
    params = []
    for l in range(L):
        p = {n: small_full[n][l] for n in SMALL_SHARDED}
        p.update({n: w[n][l][None] for n in REPLICATED})
        params.append(p)

    act = x[0]
    saved = []
    for l in range(L):
        act, s = _layer_fwd(act, params[l], full, l)
        saved.append(s)
    for l in range(L):
        params[l].update({n: full[l][n][0] for n in BIG})
    dx, loss_part = _loss_grad(act, loss_target[0])
    loss = lax.psum(loss_part, ("x", "y", "c"))

    grads = [None] * L
    summed = {}
    pending = []
    for l in reversed(range(L)):
        dx, grads[l], sums = _layer_bwd(dx, saved[l], params[l], (core_arr, place, pending))
        summed.update({(l, n): sums[n] for n in AFTER_ATTENTION})
        if pending:
            summed[l + 1, "w_in"] = sums["pending"][0]
        pending = [grads[l]["w_in"]]
    parts = _chip_partials(pending, ["col"], core_arr)
    summed[0, "w_in"] = _owned_sums(parts, _scatter_partials(parts, ["col"]), ["col"], place)[0]

    grad, delta, new_m, new_v = {}, {}, {}, {}
    for n in BIG:
        grad[n], delta[n], new_m[n], new_v[n] = _adamw_stacked(w[n], [summed[l, n] for l in range(L)], m[n], v[n])

    small = REPLICATED + SMALL_SHARDED
    small_grads = [jnp.stack([grads[l][n] for l in range(L)]) for n in small]
    small_sums = _unpack(_allreduce_small(_pack(small_grads)), [a.shape for a in small_grads])
    for n, g in zip(small, small_sums):
        if n in REPLICATED:
            grad[n] = g.reshape(w[n].shape)
        else:
            width = w[n].shape[2]
            grad[n] = lax.dynamic_slice_in_dim(g, chip * width, width, axis=2)
    shapes = [w[n].shape for n in small]
    packed = _adamw(*[_pack([src[n] for n in small]) for src in (w, grad, m, v)])
    for out, pk in zip((delta, new_m, new_v), packed):
        out.update(zip(small, _unpack(pk, shapes)))

    return (loss, dx[None], *[grad[n] for n in WEIGHTS], *[delta[n] for n in WEIGHTS],
            *[new_m[n] for n in WEIGHTS], *[new_v[n] for n in WEIGHTS])
```

```python
import jax
import jax.numpy as jnp
from jax import lax
from jax.experimental import pallas as pl
from jax.experimental.pallas import tpu as pltpu

F32 = jnp.float32
BF16 = jnp.bfloat16

DEPTH = 4
D_MODEL = 1024
HEADS = 8
HEAD_DIM = 64
ATTN_WIDTH = HEADS * HEAD_DIM
CONV_WIDTH = D_MODEL - ATTN_WIDTH
CONV_KERNEL = 31
D_FF = 2816
FFN_KERNEL = 3
EPS = 1e-6
ADAM_LR, ADAM_B1, ADAM_B2, ADAM_EPS, ADAM_WD, ADAM_STEP = 0.001, 0.9, 0.999, 1e-08, 0.01, 10

N_CHIPS = 4
N_DEV = 8
LANES = 128
SUBLANES = 8
VMEM_LIMIT_BYTES = 56 * 2**20
ATTN_TILE = 256
ATTN_HEADS_PER_STEP = 4
ATTN_BLOCK = ATTN_HEADS_PER_STEP * HEAD_DIM
CONV_PAD = 32
FFN_PAD = 8
MESH = pl.DeviceIdType.MESH


def _params(*sem):
    return pltpu.CompilerParams(dimension_semantics=sem if sem else None, vmem_limit_bytes=VMEM_LIMIT_BYTES)


class _Guest:
    def __init__(self, name, ins, out_shape, aliases, copies, n_sem):
        self.name, self.ins, self.out_shape, self.aliases, self.copies, self.n_sem = name, ins, out_shape, aliases, copies, n_sem


def _hosted_call(body, guest, *, name, grid, in_specs, out_specs, out_shape, scratch_shapes, operands):
    if guest is None:
        out = pl.pallas_call(body, name=name, grid=grid, in_specs=in_specs, out_specs=out_specs, out_shape=out_shape,
                             scratch_shapes=scratch_shapes, compiler_params=_params("arbitrary"))(*operands)
        return list(out), []
    n_in, n_out, n_scr = len(in_specs), len(out_specs), len(scratch_shapes)
    gi, go = len(guest.ins), len(guest.out_shape)

    def hosting(*refs):
        ins, g_in = refs[:n_in], refs[n_in:n_in + gi]
        outs, g_out = refs[n_in + gi:n_in + gi + n_out], refs[n_in + gi + n_out:n_in + gi + n_out + go]
        scratch, (send, recv) = refs[n_in + gi + n_out + go:-2], refs[-2:]

        @pl.when(pl.program_id(0) == 0)
        def _():
            for cp in guest.copies(g_in, g_out, send, recv):
                cp.start()

        body(*ins, *outs, *scratch)

        @pl.when(pl.program_id(0) == grid[0] - 1)
        def _():
            for cp in guest.copies(g_in, g_out, send, recv):
                cp.wait()

    out = pl.pallas_call(
        hosting, name=name + "_" + guest.name, grid=grid,
        in_specs=list(in_specs) + [ANY] * gi, out_specs=list(out_specs) + [ANY] * go,
        out_shape=list(out_shape) + list(guest.out_shape),
        scratch_shapes=list(scratch_shapes) + [pltpu.SemaphoreType.DMA((guest.n_sem,))] * 2,
        input_output_aliases={n_in + a: n_out + b for a, b in guest.aliases.items()},
        compiler_params=pltpu.CompilerParams(dimension_semantics=("arbitrary",), vmem_limit_bytes=VMEM_LIMIT_BYTES,
                                             has_side_effects=True),
    )(*operands, *guest.ins)
    return list(out[:n_out]), list(out[n_out:])


def _dot(a, b):
    return jnp.dot(a, b, preferred_element_type=F32)


def _dot_nt(a, b):
    return lax.dot_general(a, b, (((1,), (1,)), ((), ())), preferred_element_type=F32)


def _dot_tn(a, b):
    return lax.dot_general(a, b, (((0,), (0,)), ((), ())), preferred_element_type=F32)


def _sigmoid(x):
    return 1.0 / (1.0 + jnp.exp(-x))


def _norm_matmul(x, g, w, guest=None, *, out_dtype=F32, tm=256):
    S, D = x.shape
    N = w.shape[1]

    def body(x_ref, g_ref, w_ref, ht_ref, y_ref):
        xv = x_ref[...]
        h = xv * lax.rsqrt(jnp.mean(xv * xv, axis=-1, keepdims=True) + EPS) * g_ref[...]
        ht_ref[...] = h.T.astype(BF16)
        y_ref[...] = _dot(h.astype(BF16), w_ref[...]).astype(out_dtype)

    (ht, y), guest_out = _hosted_call(
        body, guest, name="norm_matmul", grid=(S // tm,),
        in_specs=[pl.BlockSpec((tm, D), lambda i: (i, 0)),
                  pl.BlockSpec((1, D), lambda i: (0, 0)),
                  pl.BlockSpec((D, N), lambda i: (0, 0))],
        out_specs=[pl.BlockSpec((D, tm), lambda i: (0, i)),
                   pl.BlockSpec((tm, N), lambda i: (i, 0))],
        out_shape=[jax.ShapeDtypeStruct((D, S), BF16), jax.ShapeDtypeStruct((S, N), out_dtype)],
        scratch_shapes=[], operands=[x, g, w])
    return ht, y, guest_out


def _matmul_res(pieces, w, res, guest=None, *, tm=512):
    S, N = res.shape
    K = w.shape[0]
    widths = [p.shape[1] for p in pieces]
    assert sum(widths) == K

    def body(*refs):
        p_refs, (w_ref, res_ref, o_ref) = refs[:len(pieces)], refs[len(pieces):]
        acc = res_ref[...]
        off = 0
        for p_ref, kp in zip(p_refs, widths):
            acc = acc + _dot(p_ref[...], w_ref[off:off + kp, :])
            off += kp
        o_ref[...] = acc

    (out,), guest_out = _hosted_call(
        body, guest, name="matmul_res", grid=(S // tm,),
        in_specs=[pl.BlockSpec((tm, kp), lambda i: (i, 0)) for kp in widths]
        + [pl.BlockSpec((K, N), lambda i: (0, 0)), pl.BlockSpec((tm, N), lambda i: (i, 0))],
        out_specs=[pl.BlockSpec((tm, N), lambda i: (i, 0))],
        out_shape=[jax.ShapeDtypeStruct((S, N), F32)],
        scratch_shapes=[], operands=[*pieces, w, res])
    return out, guest_out


def _nt_sum(p_refs, widths, w_ref):
    acc = None
    off = 0
    for p_ref, n_p in zip(p_refs, widths):
        d = _dot_nt(p_ref[...].astype(BF16), w_ref[:, off:off + n_p])
        acc = d if acc is None else acc + d
        off += n_p
    return acc


def _matmul_nt(pieces, w, out_dtype, guest=None, *, tm=512):
    S = pieces[0].shape[0]
    K, N = w.shape
    widths = [p.shape[1] for p in pieces]
    assert sum(widths) == N

    def body(*refs):
        p_refs, (w_ref, o_ref) = refs[:len(pieces)], refs[len(pieces):]
        o_ref[...] = _nt_sum(p_refs, widths, w_ref).astype(out_dtype)

    (out,), guest_out = _hosted_call(
        body, guest, name="matmul_nt", grid=(S // tm,),
        in_specs=[pl.BlockSpec((tm, n_p), lambda i: (i, 0)) for n_p in widths]
        + [pl.BlockSpec((K, N), lambda i: (0, 0))],
        out_specs=[pl.BlockSpec((tm, K), lambda i: (i, 0))],
        out_shape=[jax.ShapeDtypeStruct((S, K), out_dtype)],
        scratch_shapes=[], operands=[*pieces, w])
    return out, guest_out


def _matmul_nt_rmsbwd(pieces, w, x, g, dres, guest=None, *, tm=256):
    S, K = x.shape
    N = w.shape[1]
    widths = [p.shape[1] for p in pieces]
    assert sum(widths) == N

    def body(*refs):
        p_refs, (w_ref, x_ref, g_ref, dres_ref, dx_ref, dg_ref) = refs[:len(pieces)], refs[len(pieces):]
        dh = _nt_sum(p_refs, widths, w_ref)
        xv = x_ref[...]
        r = lax.rsqrt(jnp.mean(xv * xv, axis=-1, keepdims=True) + EPS)
        xh = xv * r
        dxh = dh * g_ref[...]
        dx_ref[...] = dres_ref[...] + r * (dxh - xh * jnp.mean(dxh * xh, axis=-1, keepdims=True))

        @pl.when(pl.program_id(0) == 0)
        def _():
            dg_ref[...] = jnp.zeros_like(dg_ref)

        dg_ref[...] += jnp.sum(dh * xh, axis=0, keepdims=True)

    (dx, dg), guest_out = _hosted_call(
        body, guest, name="matmul_nt_rmsbwd", grid=(S // tm,),
        in_specs=[pl.BlockSpec((tm, n_p), lambda i: (i, 0)) for n_p in widths]
        + [pl.BlockSpec((K, N), lambda i: (0, 0)), pl.BlockSpec((tm, K), lambda i: (i, 0)),
           pl.BlockSpec((1, K), lambda i: (0, 0)), pl.BlockSpec((tm, K), lambda i: (i, 0))],
        out_specs=[pl.BlockSpec((tm, K), lambda i: (i, 0)), pl.BlockSpec((1, K), lambda i: (0, 0))],
        out_shape=[jax.ShapeDtypeStruct((S, K), F32), jax.ShapeDtypeStruct((1, K), F32)],
        scratch_shapes=[], operands=[*pieces, w, x, g, dres])
    return dx, dg, guest_out


def _matmul_tn(xts, dys, *, tk, tn, ts=1024):
    S = dys[0].shape[0]
    xs = xts
    n_s = S // ts
    (mt,) = {x.shape[0] // tk for x in xts}
    (nt,) = {d.shape[1] // tn for d in dys}

    def body(*refs):
        x_refs, dy_refs, (o_ref, acc_ref) = refs[:len(xs)], refs[len(xs):len(xs) + len(dys)], refs[len(xs) + len(dys):]
        i, j, s = pl.program_id(0), pl.program_id(1), pl.program_id(2)

        @pl.when(s == 0)
        def _():
            acc_ref[...] = jnp.zeros_like(acc_ref)

        for a, x_ref in enumerate(x_refs):
            for b, dy_ref in enumerate(dy_refs):
                @pl.when((i // mt == a) & (j // nt == b))
                def _():
                    acc_ref[...] += _dot(x_ref[...], dy_ref[...].astype(BF16))

        @pl.when(s == n_s - 1)
        def _():
            o_ref[...] = acc_ref[...].astype(BF16)

    def x_map(a):
        return lambda i, j, s: (jnp.where(i // mt == a, i % mt, 0), jnp.where(i // mt == a, s, 0))

    def dy_map(b):
        return lambda i, j, s: (jnp.where(j // nt == b, s, 0), jnp.where(j // nt == b, j % nt, 0))

    return pl.pallas_call(
        body, name="matmul_tn", grid=(len(xs) * mt, len(dys) * nt, n_s),
        in_specs=[pl.BlockSpec((tk, ts), x_map(a)) for a in range(len(xs))]
        + [pl.BlockSpec((ts, tn), dy_map(b)) for b in range(len(dys))],
        out_specs=pl.BlockSpec((tk, tn), lambda i, j, s: (i, j)),
        out_shape=jax.ShapeDtypeStruct((len(xs) * mt * tk, len(dys) * nt * tn), BF16),
        scratch_shapes=[pltpu.VMEM((tk, tn), F32)],
        compiler_params=_params("parallel", "parallel", "arbitrary"),
    )(*xs, *dys)


def _tri_consts():
    j = jnp.arange(ATTN_TILE)[:, None]
    s = jnp.arange(ATTN_TILE)[None, :]
    return (j > s).astype(BF16), (j <= s).astype(BF16), (j < s).astype(BF16)


SIGN_BIT = 0x80000000
WEIGHT_IS_ZERO = -104.0


def _log_terms(sn):
    minus_abs = lax.bitcast_convert_type(lax.bitcast_convert_type(sn, jnp.uint32) | jnp.uint32(SIGN_BIT), F32)
    lom = jnp.minimum(sn, 0.0) - jnp.log(1.0 + jnp.exp(minus_abs))
    return lom, lom - sn


def _causal_mask():
    t = lax.broadcasted_iota(jnp.int32, (ATTN_TILE, ATTN_TILE), 0)
    s = lax.broadcasted_iota(jnp.int32, (ATTN_TILE, ATTN_TILE), 1)
    return s < t


def _attn_prep(h, q_ref, k_ref, v_ref, qg_ref, kg_ref, qn_s, kn_s, vb_s, n_tiles):
    T = ATTN_TILE
    lanes = slice(HEAD_DIM * h, HEAD_DIM * (h + 1))
    scale = -(HEAD_DIM ** -0.5)

    def prep(i, carry):
        rows = pl.ds(pl.multiple_of(i * T, T), T)
        q = q_ref[rows, lanes]
        k = k_ref[rows, lanes]
        rq = lax.rsqrt(jnp.mean(q * q, axis=-1, keepdims=True) + EPS)
        rk = lax.rsqrt(jnp.mean(k * k, axis=-1, keepdims=True) + EPS)
        qn_s[h, rows, :] = (q * rq * qg_ref[...] * scale).astype(BF16)
        kn_s[h, rows, :] = (k * rk * kg_ref[...]).astype(BF16)
        vb_s[h, rows, :] = v_ref[rows, lanes].astype(BF16)
        return carry

    lax.fori_loop(0, n_tiles, prep, 0)


def _attn_fwd(proj, qg, kg, fulls=(), kinds=()):
    S = proj.shape[0]
    n_comm = len(fulls)
    T = ATTN_TILE
    n_tiles = S // T
    suffix, _, _ = _tri_consts()
    pairs = HEADS // ATTN_HEADS_PER_STEP
    q_blk, k_blk, v_blk = 0, ATTN_WIDTH // ATTN_BLOCK, 2 * ATTN_WIDTH // ATTN_BLOCK

    def body(*refs):
        q_ref, k_ref, v_ref, qg_ref, kg_ref, tri_ref = refs[:6]
        o_ref, lt_ref, ot_ref, start_ref, qn_s, kn_s, vb_s = refs[6 + n_comm:13 + n_comm]
        w_refs = refs[13 + n_comm:13 + 2 * n_comm]
        sems = refs[13 + 2 * n_comm:]
        step_id = pl.program_id(0)
        if n_comm:
            @pl.when(pl.program_id(0) == 0)
            def _():
                for cp in _gather_copies(w_refs, kinds, *sems, 0):
                    cp.start()

        heads = range(ATTN_HEADS_PER_STEP)
        for h in heads:
            _attn_prep(h, q_ref, k_ref, v_ref, qg_ref, kg_ref, qn_s, kn_s, vb_s, n_tiles)

        def q_tile(qi, carry0):
            qrows = pl.ds(pl.multiple_of(qi * T, T), T)
            qt = [qn_s[h, qrows, :] for h in heads]

            def tile(kj, carry, diag):
                krows = pl.ds(pl.multiple_of(kj * T, T), T)
                s = [_dot_nt(qt[h], kn_s[h, krows, :]) for h in heads]
                terms = [_log_terms(s[h]) for h in heads]
                lom = [terms[h][0] for h in heads]
                if diag:
                    mask = _causal_mask()
                    lom = [jnp.where(mask, lom[h], 0.0) for h in heads]
                lom = [lom[h].astype(BF16) for h in heads]
                tail = [_dot(lom[h], tri_ref[...]) for h in heads]
                w = [jnp.exp(terms[h][1] + tail[h] + carry[h][1]) for h in heads]
                if diag:
                    w = [jnp.where(mask, w[h], 0.0) for h in heads]
                return tuple((carry[h][0] + _dot(w[h].astype(BF16), vb_s[h, krows, :]),
                              carry[h][1] + (tail[h][:, 0:1] + lom[h][:, 0:1].astype(F32))) for h in heads)

            init = tuple((jnp.zeros((T, HEAD_DIM), F32), jnp.zeros((T, 1), F32)) for h in heads)
            def alive(cr):
                worst = cr[0][1]
                for h in heads[1:]:
                    worst = jnp.maximum(worst, cr[h][1])
                return jnp.max(worst) >= WEIGHT_IS_ZERO

            def step(state):
                t, _, cr = state
                cr = tile(qi - 1 - t, cr, False)
                return t + 1, alive(cr), cr

            first = tile(qi, init, True)
            swept, _, carry = lax.while_loop(lambda st: (st[0] < qi) & st[1], step, (jnp.int32(0), alive(first), first))
            start_ref[step_id, qi] = (qi - swept).astype(F32)
            o = jnp.concatenate([carry[h][0] for h in heads], axis=1)
            o_ref[qrows, :] = o.astype(BF16)
            ot_ref[:, qrows] = o.T.astype(BF16)
            for h in heads:
                lt_ref[qrows, HEAD_DIM * h:HEAD_DIM * (h + 1)] = jnp.broadcast_to(carry[h][1], (T, HEAD_DIM))
            return carry0

        lax.fori_loop(0, n_tiles, q_tile, 0)
        if n_comm:
            @pl.when(pl.program_id(0) == pairs - 1)
            def _():
                for cp in _gather_copies(w_refs, kinds, *sems, 0):
                    cp.wait()

    n_sem = n_comm * (N_CHIPS - 1)
    per_head = pl.BlockSpec((ATTN_HEADS_PER_STEP, S, HEAD_DIM), lambda p: (p, 0, 0))
    out = pl.pallas_call(
        body, name="attn_fwd_gather" if n_comm else "attn_fwd", grid=(pairs,),
        in_specs=[pl.BlockSpec((S, ATTN_BLOCK), lambda p: (0, q_blk + p)),
                  pl.BlockSpec((S, ATTN_BLOCK), lambda p: (0, k_blk + p)),
                  pl.BlockSpec((S, ATTN_BLOCK), lambda p: (0, v_blk + p)),
                  pl.BlockSpec((1, HEAD_DIM), lambda p: (0, 0)),
                  pl.BlockSpec((1, HEAD_DIM), lambda p: (0, 0)),
                  pl.BlockSpec((T, T), lambda p: (0, 0))] + [ANY] * n_comm,
        out_specs=[pl.BlockSpec((S, ATTN_BLOCK), lambda p: (0, p)),
                   pl.BlockSpec((None, S, ATTN_BLOCK), lambda p: (p, 0, 0)),
                   pl.BlockSpec((ATTN_BLOCK, S), lambda p: (p, 0)),
                   pl.BlockSpec(memory_space=pltpu.SMEM)] + [per_head] * 3 + [ANY] * n_comm,
        out_shape=[jax.ShapeDtypeStruct((S, ATTN_WIDTH), BF16),
                   jax.ShapeDtypeStruct((pairs, S, ATTN_BLOCK), F32),
                   jax.ShapeDtypeStruct((ATTN_WIDTH, S), BF16),
                   jax.ShapeDtypeStruct((pairs, n_tiles), F32)] + [jax.ShapeDtypeStruct((HEADS, S, HEAD_DIM), BF16)] * 3
        + [jax.ShapeDtypeStruct(f.shape, f.dtype) for f in fulls],
        scratch_shapes=[pltpu.SemaphoreType.DMA((n_sem,))] * 2 if n_comm else [],
        input_output_aliases={6 + a: 7 + a for a in range(n_comm)},
        compiler_params=pltpu.CompilerParams(dimension_semantics=("arbitrary",), vmem_limit_bytes=VMEM_LIMIT_BYTES,
                                             has_side_effects=bool(n_comm)),
    )(proj, proj, proj, qg, kg, suffix, *fulls)
    return out[0], (out[1], out[3], out[4], out[5], out[6]), out[2], list(out[7:])


def _attn_bwd(proj, qg, kg, ltot, dmix, parts=(), kinds=()):
    S = proj.shape[0]
    n_comm = len(parts)
    T = ATTN_TILE
    n_tiles = S // T
    _, prefix_incl, prefix_excl = _tri_consts()
    pairs = HEADS // ATTN_HEADS_PER_STEP
    q_blk, k_blk, v_blk = 0, ATTN_WIDTH // ATTN_BLOCK, 2 * ATTN_WIDTH // ATTN_BLOCK
    scale = HEAD_DIM ** -0.5

    def body(*refs):
        q_ref, k_ref, qg_ref, kg_ref, lt_ref, do_ref, ti_ref, te_ref, start_ref, qn_s, kn_s, vb_s = refs[:12]
        p_refs = refs[12:12 + n_comm]
        dq_ref, dk_ref, dv_ref, dqg_ref, dkg_ref = refs[12 + n_comm:17 + n_comm]
        lb_refs = refs[17 + n_comm:17 + 2 * n_comm]
        dq_s, dk_s, dv_s = refs[17 + 2 * n_comm:20 + 2 * n_comm]
        sems = refs[20 + 2 * n_comm:]
        step_id = pl.program_id(0)

        @pl.when(pl.program_id(0) == 0)
        def _():
            dqg_ref[...] = jnp.zeros_like(dqg_ref)
            dkg_ref[...] = jnp.zeros_like(dkg_ref)
            for cp in _scatter_copies(p_refs, lb_refs, kinds, *sems) if n_comm else ():
                cp.start()

        heads = range(ATTN_HEADS_PER_STEP)
        dk_s[...] = jnp.zeros_like(dk_s)
        dv_s[...] = jnp.zeros_like(dv_s)

        def q_tile(qi, carry0):
            qrows = pl.ds(pl.multiple_of(qi * T, T), T)
            qt = [qn_s[h, qrows, :] for h in heads]
            dob = [do_ref[qrows, HEAD_DIM * h:HEAD_DIM * (h + 1)].astype(BF16) for h in heads]
            lt = [lt_ref[qrows, HEAD_DIM * h:HEAD_DIM * h + 1] for h in heads]

            def tile(kj, carry, diag):
                krows = pl.ds(pl.multiple_of(kj * T, T), T)
                kt = [kn_s[h, krows, :] for h in heads]
                s = [_dot_nt(qt[h], kt[h]) for h in heads]
                dw = [_dot_nt(dob[h], vb_s[h, krows, :]) for h in heads]
                terms = [_log_terms(s[h]) for h in heads]
                lom = [terms[h][0] for h in heads]
                if diag:
                    mask = _causal_mask()
                    lom = [jnp.where(mask, lom[h], 0.0) for h in heads]
                pin = [_dot(lom[h].astype(BF16), ti_ref[...]) for h in heads]
                w = [jnp.exp(terms[h][1] + ((lt[h] - carry[h][1]) - pin[h])) for h in heads]
                if diag:
                    w = [jnp.where(mask, w[h], 0.0) for h in heads]
                e = [w[h] * dw[h] for h in heads]
                gex = [_dot(e[h].astype(BF16), te_ref[...]) for h in heads]
                dz = [e[h] - jnp.exp(terms[h][1]) * (e[h] + (carry[h][2] + gex[h])) for h in heads]
                if diag:
                    dz = [jnp.where(mask, dz[h], 0.0) for h in heads]
                new = []
                for h in heads:
                    dzb = dz[h].astype(BF16)
                    dk_s[h, krows, :] += _dot_tn(dzb, qt[h])
                    dv_s[h, krows, :] += _dot_tn(w[h].astype(BF16), dob[h])
                    new.append((carry[h][0] + _dot(dzb, kt[h]),
                                carry[h][1] + pin[h][:, T - 1:T],
                                carry[h][2] + gex[h][:, T - 1:T] + e[h][:, T - 1:T]))
                return tuple(new)

            zero = jnp.zeros((T, 1), F32)
            init = tuple((jnp.zeros((T, HEAD_DIM), F32), zero, zero) for h in heads)
            first = jnp.clip(start_ref[step_id, qi].astype(jnp.int32), 0, qi)
            last = tile(qi, lax.fori_loop(first, qi, lambda kj, cr: tile(kj, cr, False), init), True)
            for h in heads:
                dq_s[h, qrows, :] = last[h][0]
            return carry0

        lax.fori_loop(0, n_tiles, q_tile, 0)

        def finish(i, carry):
            rows = pl.ds(pl.multiple_of(i * T, T), T)
            new = []
            for h in heads:
                lanes = slice(HEAD_DIM * h, HEAD_DIM * (h + 1))
                q = q_ref[rows, lanes]
                k = k_ref[rows, lanes]
                rq = lax.rsqrt(jnp.mean(q * q, axis=-1, keepdims=True) + EPS)
                rk = lax.rsqrt(jnp.mean(k * k, axis=-1, keepdims=True) + EPS)
                qh = q * rq
                kh = k * rk
                dqn = dq_s[h, rows, :] * scale
                dkn = -dk_s[h, rows, :]
                dqh = dqn * qg_ref[...]
                dkh = dkn * kg_ref[...]
                dq_ref[rows, lanes] = (rq * (dqh - qh * jnp.mean(dqh * qh, axis=-1, keepdims=True))).astype(BF16)
                dk_ref[rows, lanes] = (rk * (dkh - kh * jnp.mean(dkh * kh, axis=-1, keepdims=True))).astype(BF16)
                dv_ref[rows, lanes] = dv_s[h, rows, :].astype(BF16)
                new.append(carry[2 * h] + jnp.sum(dqn * qh, axis=0, keepdims=True))
                new.append(carry[2 * h + 1] + jnp.sum(dkn * kh, axis=0, keepdims=True))
            return tuple(new)

        zero = jnp.zeros((1, HEAD_DIM), F32)
        sums = lax.fori_loop(0, n_tiles, finish, (zero,) * (2 * len(heads)))
        dqg_ref[...] += sum(sums[0::2])
        dkg_ref[...] += sum(sums[1::2])
        if n_comm:
            @pl.when(pl.program_id(0) == pairs - 1)
            def _():
                for cp in _scatter_copies(p_refs, lb_refs, kinds, *sems):
                    cp.wait()

    blk = lambda off: pl.BlockSpec((S, ATTN_BLOCK), lambda p: (0, off + p))
    row64 = pl.BlockSpec((1, HEAD_DIM), lambda p: (0, 0))
    tri = pl.BlockSpec((T, T), lambda p: (0, 0))
    n_sem = n_comm * (N_CHIPS - 1)
    out = pl.pallas_call(
        body, name="attn_bwd_scatter" if n_comm else "attn_bwd", grid=(pairs,),
        in_specs=[blk(q_blk), blk(k_blk), row64, row64,
                  pl.BlockSpec((None, S, ATTN_BLOCK), lambda p: (p, 0, 0)), blk(0), tri, tri,
                  pl.BlockSpec(memory_space=pltpu.SMEM)]
        + [pl.BlockSpec((ATTN_HEADS_PER_STEP, S, HEAD_DIM), lambda p: (p, 0, 0))] * 3 + [ANY] * n_comm,
        out_specs=[blk(0), blk(0), blk(0), row64, row64] + [ANY] * n_comm,
        out_shape=[jax.ShapeDtypeStruct((S, ATTN_WIDTH), BF16)] * 3 + [jax.ShapeDtypeStruct((1, HEAD_DIM), F32)] * 2
        + _scatter_shapes(parts, kinds),
        scratch_shapes=[pltpu.VMEM((ATTN_HEADS_PER_STEP, S, HEAD_DIM), F32)] * 3
        + ([pltpu.SemaphoreType.DMA((n_sem,))] * 2 if n_comm else []),
        compiler_params=pltpu.CompilerParams(dimension_semantics=("arbitrary",), vmem_limit_bytes=VMEM_LIMIT_BYTES,
                                             has_side_effects=bool(n_comm)),
    )(proj, proj, qg, kg, ltot[0], dmix, prefix_incl, prefix_excl, *ltot[1:], *parts)
    return out[:5], list(out[5:])


def _shifted(win, n_rows):
    return [win if b == 0 else pltpu.roll(win, n_rows - b, 0) for b in range(SUBLANES)]


def _taps(variants, offsets, tm):
    return {o: variants[o % SUBLANES][(o // SUBLANES) * SUBLANES:(o // SUBLANES) * SUBLANES + tm, :] for o in offsets}


def _fold_rows(a):
    return jnp.sum(a.reshape(a.shape[0] // SUBLANES, SUBLANES, a.shape[1]), axis=0)


def _glu_conv_fwd(proj, w, bias, guest=None, *, tm=256):
    S = proj.shape[0]
    CB = LANES
    a_blk, b_blk = 3 * ATTN_WIDTH // CB, (3 * ATTN_WIDTH + CONV_WIDTH) // CB
    n_rows = tm + CONV_PAD

    def body(a_ref, b_ref, w_ref, bias_ref, c1_ref, pad_s):
        pad_s[0:CONV_PAD, :] = jnp.zeros((CONV_PAD, CB), F32)

        def fill(i, carry):
            rows = pl.ds(pl.multiple_of(i * tm, tm), tm)
            pad_s[pl.ds(pl.multiple_of(CONV_PAD + i * tm, SUBLANES), tm), :] = a_ref[rows, :] * _sigmoid(b_ref[rows, :])
            return carry

        lax.fori_loop(0, S // tm, fill, 0)

        def conv(i, carry):
            r0 = pl.multiple_of(i * tm, tm)
            taps = _taps(_shifted(pad_s[pl.ds(r0, n_rows), :], n_rows), range(2, 2 + CONV_KERNEL), tm)
            acc = jnp.broadcast_to(bias_ref[...], (tm, CB))
            for k in range(CONV_KERNEL):
                acc = acc + w_ref[k:k + 1, :] * taps[k + 2]
            c1_ref[pl.ds(r0, tm), :] = acc
            return carry

        lax.fori_loop(0, S // tm, conv, 0)

    (c1,), guest_out = _hosted_call(
        body, guest, name="glu_conv_fwd", grid=(CONV_WIDTH // CB,),
        in_specs=[pl.BlockSpec((S, CB), lambda j: (0, a_blk + j)), pl.BlockSpec((S, CB), lambda j: (0, b_blk + j)),
                  pl.BlockSpec((CONV_KERNEL, CB), lambda j: (0, j)), pl.BlockSpec((1, CB), lambda j: (0, j))],
        out_specs=[pl.BlockSpec((S, CB), lambda j: (0, j))],
        out_shape=[jax.ShapeDtypeStruct((S, CONV_WIDTH), F32)],
        scratch_shapes=[pltpu.VMEM((S + CONV_PAD, CB), F32)], operands=[proj, proj, w, bias])
    return c1, guest_out


def _glu_conv_bwd(proj, w, dc1, guest=None, *, tm=256):
    S = proj.shape[0]
    CB = LANES
    a_blk, b_blk = 3 * ATTN_WIDTH // CB, (3 * ATTN_WIDTH + CONV_WIDTH) // CB
    n_rows = tm + CONV_PAD

    def body(a_ref, b_ref, w_ref, dc1_ref, da_ref, db_ref, dw_ref, dbias_ref, pad_s, dpad_s, dw_s):
        pad_s[0:CONV_PAD, :] = jnp.zeros((CONV_PAD, CB), F32)
        dpad_s[S:S + CONV_PAD, :] = jnp.zeros((CONV_PAD, CB), F32)
        dw_s[...] = jnp.zeros_like(dw_s)

        def fill(i, carry):
            rows = pl.ds(pl.multiple_of(i * tm, tm), tm)
            pad_s[pl.ds(pl.multiple_of(CONV_PAD + i * tm, SUBLANES), tm), :] = a_ref[rows, :] * _sigmoid(b_ref[rows, :])
            dpad_s[rows, :] = dc1_ref[rows, :]
            return carry

        lax.fori_loop(0, S // tm, fill, 0)

        def conv(i, carry):
            r0 = pl.multiple_of(i * tm, tm)
            rows = pl.ds(r0, tm)
            taps = _taps(_shifted(dpad_s[pl.ds(r0, n_rows), :], n_rows), range(CONV_KERNEL), tm)
            acc = jnp.zeros((tm, CB), F32)
            for k in range(CONV_KERNEL):
                acc = acc + w_ref[k:k + 1, :] * taps[CONV_KERNEL - 1 - k]
            a = a_ref[rows, :]
            sg = _sigmoid(b_ref[rows, :])
            da_ref[rows, :] = (acc * sg).astype(BF16)
            db_ref[rows, :] = (acc * a * sg * (1.0 - sg)).astype(BF16)
            d = taps[0]
            taps = _taps(_shifted(pad_s[pl.ds(r0, n_rows), :], n_rows), range(2, 2 + CONV_KERNEL), tm)
            for k in range(CONV_KERNEL):
                dw_s[SUBLANES * k:SUBLANES * (k + 1), :] += _fold_rows(d * taps[k + 2])
            dw_s[SUBLANES * CONV_KERNEL:SUBLANES * (CONV_KERNEL + 1), :] += _fold_rows(d)
            return carry

        lax.fori_loop(0, S // tm, conv, 0)
        for k in range(CONV_KERNEL):
            dw_ref[k:k + 1, :] = jnp.sum(dw_s[SUBLANES * k:SUBLANES * (k + 1), :], axis=0, keepdims=True)
        dbias_ref[...] = jnp.sum(dw_s[SUBLANES * CONV_KERNEL:SUBLANES * (CONV_KERNEL + 1), :], axis=0, keepdims=True)

    col = lambda off: pl.BlockSpec((S, CB), lambda j: (0, off + j))
    out, guest_out = _hosted_call(
        body, guest, name="glu_conv_bwd", grid=(CONV_WIDTH // CB,),
        in_specs=[col(a_blk), col(b_blk), pl.BlockSpec((CONV_KERNEL, CB), lambda j: (0, j)), col(0)],
        out_specs=[col(0), col(0), pl.BlockSpec((CONV_KERNEL, CB), lambda j: (0, j)), pl.BlockSpec((1, CB), lambda j: (0, j))],
        out_shape=[jax.ShapeDtypeStruct((S, CONV_WIDTH), BF16)] * 2
        + [jax.ShapeDtypeStruct((CONV_KERNEL, CONV_WIDTH), F32), jax.ShapeDtypeStruct((1, CONV_WIDTH), F32)],
        scratch_shapes=[pltpu.VMEM((S + CONV_PAD, CB), F32), pltpu.VMEM((S + CONV_PAD, CB), F32),
                        pltpu.VMEM((SUBLANES * (CONV_KERNEL + 1), CB), F32)], operands=[proj, proj, w, dc1])
    return (*out, guest_out)


def _ln_stats(c1):
    mu = jnp.mean(c1, axis=-1, keepdims=True)
    xc = c1 - mu
    r = lax.rsqrt(jnp.mean(xc * xc, axis=-1, keepdims=True) + EPS)
    return xc * r, r


def _ln_silu_fwd(c1, g, b, *, tm=512):
    S, C = c1.shape

    def body(c1_ref, g_ref, b_ref, c_ref, ct_ref):
        yh, _ = _ln_stats(c1_ref[...])
        y = yh * g_ref[...] + b_ref[...]
        c = y * _sigmoid(y)
        c_ref[...] = c.astype(BF16)
        ct_ref[...] = c.T.astype(BF16)

    vec = pl.BlockSpec((1, C), lambda i: (0, 0))
    return pl.pallas_call(
        body, name="ln_silu_fwd", grid=(S // tm,),
        in_specs=[pl.BlockSpec((tm, C), lambda i: (i, 0)), vec, vec],
        out_specs=[pl.BlockSpec((tm, C), lambda i: (i, 0)), pl.BlockSpec((C, tm), lambda i: (0, i))],
        out_shape=[jax.ShapeDtypeStruct((S, C), BF16), jax.ShapeDtypeStruct((C, S), BF16)],
        compiler_params=_params("parallel"),
    )(c1, g, b)


def _ln_silu_bwd(c1, g, b, dmix, *, tm=512):
    S, C = c1.shape

    def body(c1_ref, g_ref, b_ref, dc_ref, dc1_ref, dg_ref, db_ref):
        yh, r = _ln_stats(c1_ref[...])
        y = yh * g_ref[...] + b_ref[...]
        sg = _sigmoid(y)
        dy = dc_ref[...] * (sg * (1.0 + y * (1.0 - sg)))
        dyh = dy * g_ref[...]
        dc1_ref[...] = r * (dyh - jnp.mean(dyh, axis=-1, keepdims=True)
                            - yh * jnp.mean(dyh * yh, axis=-1, keepdims=True))

        @pl.when(pl.program_id(0) == 0)
        def _():
            dg_ref[...] = jnp.zeros_like(dg_ref)
            db_ref[...] = jnp.zeros_like(db_ref)

        dg_ref[...] += jnp.sum(dy * yh, axis=0, keepdims=True)
        db_ref[...] += jnp.sum(dy, axis=0, keepdims=True)

    vec = pl.BlockSpec((1, C), lambda i: (0, 0))
    return pl.pallas_call(
        body, name="ln_silu_bwd", grid=(S // tm,),
        in_specs=[pl.BlockSpec((tm, C), lambda i: (i, 0)), vec, vec, pl.BlockSpec((tm, C), lambda i: (i, 1))],
        out_specs=[pl.BlockSpec((tm, C), lambda i: (i, 0)), vec, vec],
        out_shape=[jax.ShapeDtypeStruct((S, C), F32), jax.ShapeDtypeStruct((1, C), F32), jax.ShapeDtypeStruct((1, C), F32)],
        compiler_params=_params("arbitrary"),
    )(c1, g, b, dmix)


FFN_CB = 256


def _ffn_gate(pad_s, w_ref, bias_ref, r0, tm):
    n_rows = tm + FFN_PAD
    taps = _taps(_shifted(pad_s[pl.ds(r0, n_rows), :], n_rows), range(FFN_PAD - 2, FFN_PAD + 1), tm)
    g1 = bias_ref[...] + w_ref[0:1, :] * taps[6] + w_ref[1:2, :] * taps[7] + w_ref[2:3, :] * taps[8]
    return g1, taps


def _ffn_act_fwd(u, w, bias, guest=None, *, tm=256):
    S = u.shape[0]
    CB = FFN_CB
    nb = D_FF // CB

    def body(g_ref, v_ref, w_ref, bias_ref, o_ref, ot_ref, pad_s):
        pad_s[0:FFN_PAD, :] = jnp.zeros((FFN_PAD, CB), F32)

        def fill(i, carry):
            pad_s[pl.ds(pl.multiple_of(FFN_PAD + i * tm, SUBLANES), tm), :] = g_ref[pl.ds(pl.multiple_of(i * tm, tm), tm), :].astype(F32)
            return carry

        lax.fori_loop(0, S // tm, fill, 0)

        def act(i, carry):
            r0 = pl.multiple_of(i * tm, tm)
            g1, _ = _ffn_gate(pad_s, w_ref, bias_ref, r0, tm)
            a = g1 * _sigmoid(g1) * v_ref[pl.ds(r0, tm), :].astype(F32)
            o_ref[pl.ds(r0, tm), :] = a.astype(BF16)
            ot_ref[:, pl.ds(r0, tm)] = a.T.astype(BF16)
            return carry

        lax.fori_loop(0, S // tm, act, 0)

    (act, actt), guest_out = _hosted_call(
        body, guest, name="ffn_act_fwd", grid=(nb,),
        in_specs=[pl.BlockSpec((S, CB), lambda j: (0, j)), pl.BlockSpec((S, CB), lambda j: (0, nb + j)),
                  pl.BlockSpec((FFN_KERNEL, CB), lambda j: (0, j)), pl.BlockSpec((1, CB), lambda j: (0, j))],
        out_specs=[pl.BlockSpec((S, CB), lambda j: (0, j)), pl.BlockSpec((CB, S), lambda j: (j, 0))],
        out_shape=[jax.ShapeDtypeStruct((S, D_FF), BF16), jax.ShapeDtypeStruct((D_FF, S), BF16)],
        scratch_shapes=[pltpu.VMEM((S + FFN_PAD, CB), F32)], operands=[u, u, w, bias])
    return act, actt, guest_out


def _ffn_act_bwd(u, w, bias, dact, guest=None, *, tm=256):
    S = u.shape[0]
    CB = FFN_CB
    nb = D_FF // CB

    def body(g_ref, v_ref, w_ref, bias_ref, da_ref, dg_ref, dv_ref, dw_ref, dbias_ref, pad_s, dpad_s, dw_s):
        pad_s[0:FFN_PAD, :] = jnp.zeros((FFN_PAD, CB), F32)
        dpad_s[S:S + FFN_PAD, :] = jnp.zeros((FFN_PAD, CB), F32)
        dw_s[...] = jnp.zeros_like(dw_s)

        def fill(i, carry):
            pad_s[pl.ds(pl.multiple_of(FFN_PAD + i * tm, SUBLANES), tm), :] = g_ref[pl.ds(pl.multiple_of(i * tm, tm), tm), :].astype(F32)
            return carry

        lax.fori_loop(0, S // tm, fill, 0)

        def first(i, carry):
            r0 = pl.multiple_of(i * tm, tm)
            rows = pl.ds(r0, tm)
            g1, taps = _ffn_gate(pad_s, w_ref, bias_ref, r0, tm)
            sg = _sigmoid(g1)
            da = da_ref[rows, :].astype(F32)
            dv_ref[rows, :] = (da * g1 * sg).astype(BF16)
            dg1 = da * v_ref[rows, :].astype(F32) * (sg * (1.0 + g1 * (1.0 - sg)))
            dpad_s[rows, :] = dg1
            for k in range(FFN_KERNEL):
                dw_s[SUBLANES * k:SUBLANES * (k + 1), :] += _fold_rows(dg1 * taps[FFN_PAD - 2 + k])
            dw_s[SUBLANES * FFN_KERNEL:SUBLANES * (FFN_KERNEL + 1), :] += _fold_rows(dg1)
            return carry

        lax.fori_loop(0, S // tm, first, 0)

        def second(i, carry):
            r0 = pl.multiple_of(i * tm, tm)
            n_rows = tm + FFN_PAD
            taps = _taps(_shifted(dpad_s[pl.ds(r0, n_rows), :], n_rows), range(FFN_KERNEL), tm)
            dg_ref[pl.ds(r0, tm), :] = (w_ref[2:3, :] * taps[0] + w_ref[1:2, :] * taps[1] + w_ref[0:1, :] * taps[2]).astype(BF16)
            return carry

        lax.fori_loop(0, S // tm, second, 0)
        for k in range(FFN_KERNEL):
            dw_ref[k:k + 1, :] = jnp.sum(dw_s[SUBLANES * k:SUBLANES * (k + 1), :], axis=0, keepdims=True)
        dbias_ref[...] = jnp.sum(dw_s[SUBLANES * FFN_KERNEL:SUBLANES * (FFN_KERNEL + 1), :], axis=0, keepdims=True)

    col = lambda off: pl.BlockSpec((S, CB), lambda j: (0, off + j))
    wspec = pl.BlockSpec((FFN_KERNEL, CB), lambda j: (0, j))
    bspec = pl.BlockSpec((1, CB), lambda j: (0, j))
    out, guest_out = _hosted_call(
        body, guest, name="ffn_act_bwd", grid=(nb,),
        in_specs=[col(0), col(nb), wspec, bspec, col(0)],
        out_specs=[col(0), col(0), wspec, bspec],
        out_shape=[jax.ShapeDtypeStruct((S, D_FF), BF16)] * 2
        + [jax.ShapeDtypeStruct((FFN_KERNEL, D_FF), F32), jax.ShapeDtypeStruct((1, D_FF), F32)],
        scratch_shapes=[pltpu.VMEM((S + FFN_PAD, CB), F32), pltpu.VMEM((S + FFN_PAD, CB), F32),
                        pltpu.VMEM((SUBLANES * (FFN_KERNEL + 1), CB), F32)], operands=[u, u, w, bias, dact])
    return (*out, guest_out)


def _loss_grad(y, target, *, tm=512):
    S, D = y.shape

    def body(y_ref, t_ref, dy_ref, l_ref):
        d = y_ref[...] - t_ref[...]
        dy_ref[...] = d * (1.0 / D)

        @pl.when(pl.program_id(0) == 0)
        def _():
            l_ref[...] = jnp.zeros_like(l_ref)

        l_ref[...] += 0.5 * jnp.sum(jnp.mean(d * d, axis=-1, keepdims=True), axis=0, keepdims=True)

    dy, l = pl.pallas_call(
        body, name="loss_grad", grid=(S // tm,),
        in_specs=[pl.BlockSpec((tm, D), lambda i: (i, 0))] * 2,
        out_specs=[pl.BlockSpec((tm, D), lambda i: (i, 0)), pl.BlockSpec((SUBLANES, LANES), lambda i: (0, 0))],
        out_shape=[jax.ShapeDtypeStruct((S, D), F32), jax.ShapeDtypeStruct((SUBLANES, LANES), F32)],
        compiler_params=_params("arbitrary"),
    )(y, target)
    return dy, l[0, 0]


def _row_tile(rows, cap=512):
    t = min(rows, cap)
    while rows % t or t % SUBLANES:
        t -= 1
    return t


def _adam_update(w, g, m, v):
    m1 = ADAM_B1 * m + (1.0 - ADAM_B1) * g
    v1 = ADAM_B2 * v + (1.0 - ADAM_B2) * (g * g)
    m_hat = m1 / (1.0 - ADAM_B1 ** ADAM_STEP)
    v_hat = v1 / (1.0 - ADAM_B2 ** ADAM_STEP)
    return -ADAM_LR * (m_hat / (jnp.sqrt(v_hat) + ADAM_EPS) + ADAM_WD * w), m1, v1


def _adamw_small(w, g, m, v):
    def body(w_ref, g_ref, m_ref, v_ref, d_ref, nm_ref, nv_ref):
        d_ref[...], nm_ref[...], nv_ref[...] = _adam_update(w_ref[...], g_ref[...], m_ref[...], v_ref[...])

    vmem = pl.BlockSpec(memory_space=pltpu.VMEM)
    return pl.pallas_call(body, name="adamw_small", in_specs=[vmem] * 4, out_specs=[vmem] * 3,
                          out_shape=[jax.ShapeDtypeStruct(w.shape, F32)] * 3)(w, g, m, v)


def _adamw_stacked(w, gs, m, v):
    L, R, C = w.shape
    tr = _row_tile(R, 256)

    def body(w_ref, m_ref, v_ref, *rest):
        g_refs, (go_ref, d_ref, nm_ref, nv_ref) = rest[:L], rest[L:]
        for a, g_ref in enumerate(g_refs):
            @pl.when(pl.program_id(0) == a)
            def _():
                gv = g_ref[...]
                go_ref[...] = gv
                d_ref[...], nm_ref[...], nv_ref[...] = _adam_update(w_ref[...], gv, m_ref[...], v_ref[...])

    stacked = pl.BlockSpec((None, tr, C), lambda l, i: (l, i, 0))
    return pl.pallas_call(
        body, name="adamw_stacked", grid=(L, R // tr),
        in_specs=[stacked] * 3 + [pl.BlockSpec((tr, C), lambda l, i, a=a: (jnp.where(l == a, i, 0), 0)) for a in range(L)],
        out_specs=[stacked] * 4, out_shape=[jax.ShapeDtypeStruct((L, R, C), F32)] * 4,
        compiler_params=_params("parallel", "parallel"),
    )(w, m, v, *gs)


CAST_STEPS = 4


def _cast_into_full(ws, layers, kinds, chip):
    n = len(ws)

    def body(chip_ref, *refs):
        for w_ref, o_ref in zip(refs[:n], refs[n:]):
            o_ref[...] = w_ref[...].astype(BF16)

    in_specs, out_specs, out_shape = [], [], []
    for w, layer, kind in zip(ws, layers, kinds):
        _, R, C = w.shape
        tr = R // CAST_STEPS
        assert tr % 16 == 0
        in_specs.append(pl.BlockSpec((None, tr, C), lambda i, chip_ref, layer=layer: (layer, i, 0)))
        if kind == "col":
            out_shape.append(jax.ShapeDtypeStruct((1, R, N_CHIPS * C), BF16))
            out_specs.append(pl.BlockSpec((None, tr, C), lambda i, chip_ref: (0, i, chip_ref[0])))
        else:
            out_shape.append(jax.ShapeDtypeStruct((1, N_CHIPS * R, C), BF16))
            out_specs.append(pl.BlockSpec((None, tr, C), lambda i, chip_ref: (0, chip_ref[0] * CAST_STEPS + i, 0)))
    return pl.pallas_call(
        body, name="cast_into_full",
        grid_spec=pltpu.PrefetchScalarGridSpec(num_scalar_prefetch=1, grid=(CAST_STEPS,),
                                               in_specs=in_specs, out_specs=out_specs),
        out_shape=out_shape, compiler_params=_params("parallel"),
    )(chip, *ws)


ADD_HALF_STEPS = 8
ADD_PARTS_STEPS = 2


def _add_half(g4s, las, c):
    n = len(g4s)

    def body(c_ref, *refs):
        for g_ref, la_ref, o_ref in zip(refs[:n], refs[n:2 * n], refs[2 * n:]):
            o_ref[...] = (g_ref[...].astype(F32) + la_ref[...].astype(F32)).astype(BF16)

    g_specs, la_specs, out_shape = [], [], []
    for g4 in g4s:
        L, _, H, W = g4.shape
        th = L * H // ADD_HALF_STEPS
        per = H // th
        assert th % 16 == 0 and per * th == H
        g_specs.append(pl.BlockSpec((None, None, th, W), lambda i, c_ref, per=per: (i // per, c_ref[0], i % per, 0)))
        la_specs.append(pl.BlockSpec((None, th, W), lambda i, c_ref, per=per: (i // per, i % per, 0)))
        out_shape.append(jax.ShapeDtypeStruct((L, H, W), BF16))
    return pl.pallas_call(
        body, name="add_half",
        grid_spec=pltpu.PrefetchScalarGridSpec(num_scalar_prefetch=1, grid=(ADD_HALF_STEPS,),
                                               in_specs=g_specs + la_specs, out_specs=la_specs),
        out_shape=out_shape, compiler_params=_params("parallel"),
    )(c, *g4s, *las)


def _add_parts(ps, lbs, place, kinds):
    n = len(ps)

    def body(s_ref, *refs):
        for p_ref, lb_ref, o_ref in zip(refs[:n], refs[n:2 * n], refs[2 * n:]):
            acc = p_ref[...].astype(F32)
            for k in range(N_CHIPS - 1):
                acc = acc + lb_ref[k].astype(F32)
            o_ref[...] = acc

    p_specs, lb_specs, out_specs, out_shape = [], [], [], []
    for lb, kind in zip(lbs, kinds):
        _, L, H, C = lb.shape
        th = H // ADD_PARTS_STEPS
        assert L == 1 and th % 16 == 0
        if kind == "col":
            p_specs.append(pl.BlockSpec((None, th, C), lambda i, s_ref: (0, i, s_ref[0])))
        else:
            p_specs.append(pl.BlockSpec((None, None, th, C), lambda i, s_ref: (0, s_ref[0], i, 0)))
        lb_specs.append(pl.BlockSpec((N_CHIPS - 1, None, th, C), lambda i, s_ref: (0, 0, i, 0)))
        out_specs.append(pl.BlockSpec((None, None, th, C), lambda i, s_ref: (0, s_ref[1], i, 0)))
        out_shape.append(jax.ShapeDtypeStruct((L, 2, H, C), F32))
    return pl.pallas_call(
        body, name="add_parts",
        grid_spec=pltpu.PrefetchScalarGridSpec(num_scalar_prefetch=1, grid=(ADD_PARTS_STEPS,),
                                               in_specs=p_specs + lb_specs, out_specs=out_specs),
        out_shape=out_shape, compiler_params=_params("parallel"),
    )(place, *ps, *lbs)


ANY = pl.BlockSpec(memory_space=pl.ANY)


def _place():
    x, y, c = lax.axis_index("x"), lax.axis_index("y"), lax.axis_index("c")
    chips = [(1 - x, y), (x, 1 - y), (1 - x, 1 - y)]
    return x, y, c, chips


def _comm_call(body, name, ins, out_shape, n_remote, n_local, aliases=None):
    scratch = [pltpu.SemaphoreType.DMA((n_remote,)), pltpu.SemaphoreType.DMA((n_remote,))]
    if n_local:
        scratch.append(pltpu.SemaphoreType.DMA((n_local,)))
    return pl.pallas_call(
        body, name=name, in_specs=[ANY] * len(ins), out_specs=[ANY] * len(out_shape), out_shape=out_shape,
        scratch_shapes=scratch, input_output_aliases=aliases or {},
        compiler_params=pltpu.CompilerParams(has_side_effects=True),
    )(*ins)


def _remote(src, dst, send, recv, k, to):
    return pltpu.make_async_remote_copy(src_ref=src, dst_ref=dst, send_sem=send.at[k], recv_sem=recv.at[k],
                                        device_id=to, device_id_type=MESH)


def _gather_weights(fulls, kinds):
    n = len(fulls)
    out_shape = [jax.ShapeDtypeStruct(f.shape, f.dtype) for f in fulls]

    def body(*refs):
        outs, (send, recv) = refs[n:2 * n], refs[2 * n:]
        first = _gather_copies(outs, kinds, send, recv, 0)
        for cp in first:
            cp.start()
        for cp in first:
            cp.wait()
        passed = _pass_on_copies(outs, kinds, send, recv, len(first))
        for cp in passed:
            cp.start()
        for cp in passed:
            cp.wait()

    return _comm_call(body, "gather_weights", fulls, out_shape, 2 * n * (N_CHIPS - 1), 0, {a: a for a in range(n)})


def _window(ref, kind, s, h):
    if kind == "col":
        H, C = ref.shape[1] // 2, ref.shape[2] // N_CHIPS
        return ref.at[:, pl.ds(pl.multiple_of(h * H, 16), H), pl.ds(pl.multiple_of(s * C, LANES), C)]
    R = ref.shape[1] // N_CHIPS
    return ref.at[:, pl.ds(pl.multiple_of(s * R + h * (R // 2), 16), R // 2), :]


def _gather_copies(outs, kinds, send, recv, sem0):
    x, y, c, chips = _place()
    me = 2 * x + y
    return [_remote(_window(o, kind, me, c), _window(o, kind, me, c), send, recv, sem0 + a * (N_CHIPS - 1) + k, (*chip, c))
            for a, (o, kind) in enumerate(zip(outs, kinds)) for k, chip in enumerate(chips)]


def _pass_on_copies(outs, kinds, send, recv, sem0):
    x, y, c, chips = _place()
    cps = []
    for a, (o, kind) in enumerate(zip(outs, kinds)):
        for k, chip in enumerate(chips):
            landed = _window(o, kind, 2 * chip[0] + chip[1], c)
            cps.append(_remote(landed, landed, send, recv, sem0 + a * (N_CHIPS - 1) + k, (x, y, 1 - c)))
    return cps


def _gather_small(shards):
    n = len(shards)
    out_shape = [jax.ShapeDtypeStruct((N_CHIPS,) + s.shape, s.dtype) for s in shards]

    def body(*refs):
        srcs, outs, (send, recv, loc) = refs[:n], refs[n:2 * n], refs[2 * n:]
        x, y, c, chips = _place()
        me = 2 * x + y
        remote, local = [], []
        for a in range(n):
            local.append(pltpu.make_async_copy(srcs[a], outs[a].at[me], loc.at[a]))
            for k, chip in enumerate(chips):
                remote.append(_remote(srcs[a], outs[a].at[me], send, recv, a * (N_CHIPS - 1) + k, (*chip, c)))
        for cp in local + remote:
            cp.start()
        for cp in remote + local:
            cp.wait()

    return _comm_call(body, "gather_small", shards, out_shape, n * (N_CHIPS - 1), n)


def _exchange_halves(g4s):
    n = len(g4s)
    out_shape = [jax.ShapeDtypeStruct((g.shape[0],) + g.shape[2:], g.dtype) for g in g4s]

    def body(*refs):
        gs, las, (send, recv) = refs[:n], refs[n:2 * n], refs[2 * n:]
        x, y, c, _ = _place()
        cps = [_remote(gs[a].at[:, 1 - c], las[a], send, recv, a, (x, y, 1 - c)) for a in range(n)]
        for cp in cps:
            cp.start()
        for cp in cps:
            cp.wait()

    return _comm_call(body, "exchange_halves", g4s, out_shape, n, 0)


def _scatter_partials(ps, kinds):
    n = len(ps)

    def body(*refs):
        srcs, lbs, (send, recv) = refs[:n], refs[n:2 * n], refs[2 * n:]
        cps = _scatter_copies(srcs, lbs, kinds, send, recv)
        for cp in cps:
            cp.start()
        for cp in cps:
            cp.wait()

    return _comm_call(body, "scatter_partials", ps, _scatter_shapes(ps, kinds), n * (N_CHIPS - 1), 0)


def _scatter_shapes(ps, kinds):
    out_shape = []
    for p, kind in zip(ps, kinds):
        L, H, C = (p.shape[0], p.shape[1], p.shape[2] // N_CHIPS) if kind == "col" else (p.shape[0], p.shape[2], p.shape[3])
        out_shape.append(jax.ShapeDtypeStruct((N_CHIPS - 1, L, H, C), p.dtype))
    return out_shape


def _scatter_copies(srcs, lbs, kinds, send, recv, sem0=0):
    x, y, c, chips = _place()
    cps = []
    for a, (src, lb, kind) in enumerate(zip(srcs, lbs, kinds)):
        C = lb.shape[3]
        for k, chip in enumerate(chips):
            s = 2 * chip[0] + chip[1]
            part = src.at[:, :, pl.ds(pl.multiple_of(s * C, LANES), C)] if kind == "col" else src.at[:, s]
            cps.append(_remote(part, lb.at[k], send, recv, sem0 + a * (N_CHIPS - 1) + k, (*chip, c)))
    return cps


def _share_halves(g4s):
    n = len(g4s)
    out_shape = [jax.ShapeDtypeStruct(g.shape, g.dtype) for g in g4s]

    def body(*refs):
        outs, (send, recv) = refs[n:2 * n], refs[2 * n:]
        x, y, c, _ = _place()
        cps = [_remote(outs[a].at[:, c], outs[a].at[:, c], send, recv, a, (x, y, 1 - c)) for a in range(n)]
        for cp in cps:
            cp.start()
        for cp in cps:
            cp.wait()

    return _comm_call(body, "share_halves", g4s, out_shape, n, 0, {a: a for a in range(n)})


def _allreduce_small(part):
    N = part.shape[0]

    def body(p_ref, o_ref, buf, send, recv):
        x, y, c, _ = _place()
        me = 4 * x + 2 * y + c
        buf[me] = p_ref[...]
        cps = []
        for k in range(1, N_DEV):
            peer = (1 - x if k & 4 else x, 1 - y if k & 2 else y, 1 - c if k & 1 else c)
            cps.append(_remote(p_ref, buf.at[me], send, recv, k - 1, peer))
        for cp in cps:
            cp.start()
        for cp in cps:
            cp.wait()
        acc = buf[0]
        for i in range(1, N_DEV):
            acc = acc + buf[i]
        o_ref[...] = acc

    vmem = pl.BlockSpec(memory_space=pltpu.VMEM)
    return pl.pallas_call(
        body, name="allreduce_small", in_specs=[vmem], out_specs=vmem,
        out_shape=jax.ShapeDtypeStruct((N, LANES), F32),
        scratch_shapes=[pltpu.VMEM((N_DEV, N, LANES), F32), pltpu.SemaphoreType.DMA((N_DEV - 1,)),
                        pltpu.SemaphoreType.DMA((N_DEV - 1,))],
        compiler_params=pltpu.CompilerParams(has_side_effects=True, vmem_limit_bytes=VMEM_LIMIT_BYTES),
    )(part)


AFTER_ATTENTION = ("w_out", "w_up", "w_down")


def _layer_fwd(x, p, full=None, l=0):
    comm = full is not None

    def guest(first, second):
        first, second = ([k for k in ks if comm and k[0] < len(full)] for ks in (first, second))
        if not first + second:
            return None, []
        return _gather_guest([full[i][n] for i, n in first], [BIG_KIND[n] for _, n in first],
                             [full[i][n] for i, n in second], [BIG_KIND[n] for _, n in second]), first + second

    def done(keys, bufs):
        for (i, n), buf in zip(keys, bufs):
            full[i][n] = buf

    def weight(n):
        return full[l][n][0] if comm else p[n]

    in_sweep = ([(l, "w_up")] if l else [(l, n) for n in AFTER_ATTENTION]) if comm else []
    g, keys = guest([], [(l, "w_out")] if l else [])
    h1t, proj, out = _norm_matmul(x, p["norm1_g"], weight("w_in"), g)
    done(keys, out)
    attn, ltot, attnt, out = _attn_fwd(proj, p["q_norm_g"], p["k_norm_g"],
                                       [full[i][n] for i, n in in_sweep], [BIG_KIND[n] for _, n in in_sweep])
    done(in_sweep, out)
    g, keys = guest([], in_sweep)
    c1, out = _glu_conv_fwd(proj, p["conv_dw_w"], p["conv_dw_b"], g)
    done(keys, out)
    c, ct = _ln_silu_fwd(c1, p["conv_ln_g"], p["conv_ln_b"])
    x_mid, _ = _matmul_res([attn, c], weight("w_out"), x)
    g, keys = guest([(l + 1, "w_down")], [])
    h2t, u, out = _norm_matmul(x_mid, p["norm2_g"], weight("w_up"), g)
    done(keys, out)
    g, keys = guest([(l + 1, "w_in")], [(l + 1, "w_down")])
    act, actt, out = _ffn_act_fwd(u, p["ffn_dw_w"], p["ffn_dw_b"], g)
    done(keys, out)
    g, keys = guest([(l + 1, "w_out")], [(l + 1, "w_in")])
    x_out, out = _matmul_res([act], weight("w_down"), x_mid, g)
    done(keys, out)
    saved = dict(x=x, h1t=h1t, proj=proj, attnt=attnt, ltot=ltot, c1=c1, ct=ct, x_mid=x_mid, h2t=h2t, u=u, actt=actt)
    return x_out, saved


def _chip_partials(gs, kinds, core):
    g4s = _halves_view(gs, kinds)
    return _add_halves(g4s, _exchange_halves(g4s), kinds, core)


def _halves_view(gs, kinds):
    return [g.reshape(1, 2, g.shape[0] // 2, g.shape[1]) if kind == "col"
            else g.reshape(N_CHIPS, 2, g.shape[0] // N_CHIPS // 2, g.shape[1]) for g, kind in zip(gs, kinds)]


def _add_halves(g4s, received, kinds, core):
    parts = _add_half(g4s, received, core)
    return [p if kind == "col" else p.reshape(1, N_CHIPS, p.shape[1], p.shape[2]) for p, kind in zip(parts, kinds)]


def _gather_guest(first, first_kinds, second, second_kinds):
    bufs = list(first) + list(second)
    n1 = len(first)

    def copies(ins, outs, send, recv, sem0=0):
        return (_gather_copies(outs[:n1], first_kinds, send, recv, sem0)
                + _pass_on_copies(outs[n1:], second_kinds, send, recv, sem0 + n1 * (N_CHIPS - 1)))

    return _Guest("gather", bufs, [jax.ShapeDtypeStruct(f.shape, f.dtype) for f in bufs],
                  {a: a for a in range(len(bufs))}, copies, len(bufs) * (N_CHIPS - 1))


def _scatter_guest(parts, kinds):
    return _Guest("scatter", list(parts), _scatter_shapes(parts, kinds), {},
                  lambda ins, outs, send, recv, sem0=0: _scatter_copies(ins, outs, kinds, send, recv, sem0),
                  len(parts) * (N_CHIPS - 1))


def _exchange_guest(g4s):
    def copies(ins, outs, send, recv, sem0=0):
        x, y, c, _ = _place()
        return [_remote(g.at[:, 1 - c], la, send, recv, sem0 + a, (x, y, 1 - c)) for a, (g, la) in enumerate(zip(ins, outs))]

    out_shape = [jax.ShapeDtypeStruct((g.shape[0],) + g.shape[2:], g.dtype) for g in g4s]
    return _Guest("exchange", list(g4s), out_shape, {}, copies, len(g4s))


def _share_guest(g4s):
    def copies(ins, outs, send, recv, sem0=0):
        x, y, c, _ = _place()
        return [_remote(o.at[:, c], o.at[:, c], send, recv, sem0 + a, (x, y, 1 - c)) for a, o in enumerate(outs)]

    n = len(g4s)
    return _Guest("share", list(g4s), [jax.ShapeDtypeStruct(g.shape, g.dtype) for g in g4s], {a: a for a in range(n)}, copies, n)


def _owned_sums(parts, landed, kinds, place):
    halves = _add_parts(parts, landed, place, kinds)
    return [g.reshape(2 * g.shape[2], g.shape[3]) for g in _share_halves(halves)]


def _layer_bwd(dx_out, s, p, comm=None):
    g = {}
    on = comm is not None
    core, place, pending = comm if on else (None, None, [])
    n_p = len(pending)
    finished = lambda arrays: [a.reshape(2 * a.shape[2], a.shape[3]) for a in arrays]

    if n_p:
        g4s_p = _halves_view(pending, ["col"] * n_p)
    dact, received = _matmul_nt([dx_out], p["w_down"], BF16, _exchange_guest(g4s_p) if n_p else None)
    if n_p:
        parts_p = _add_halves(g4s_p, received, ["col"] * n_p, core)
    g["w_down"] = _matmul_tn([s["actt"]], [dx_out], tk=D_FF // 2, tn=D_MODEL)
    dgate, dval, g["ffn_dw_w"], g["ffn_dw_b"], landed_p = _ffn_act_bwd(
        s["u"], p["ffn_dw_w"], p["ffn_dw_b"], dact, _scatter_guest(parts_p, ["col"] * n_p) if n_p else None)
    guest = _share_guest(_add_parts(parts_p, landed_p, place, ["col"] * n_p)) if n_p else None
    dx_mid, g["norm2_g"], shared_p = _matmul_nt_rmsbwd([dgate, dval], p["w_up"], s["x_mid"], p["norm2_g"], dx_out, guest)
    g["w_up"] = _matmul_tn([s["h2t"]], [dgate, dval], tk=D_MODEL, tn=D_FF // 2)
    dmix, _ = _matmul_nt([dx_mid], p["w_out"], F32)
    g["w_out"] = _matmul_tn([s["attnt"], s["ct"]], [dx_mid], tk=ATTN_WIDTH, tn=D_MODEL)
    dc1, g["conv_ln_g"], g["conv_ln_b"] = _ln_silu_bwd(s["c1"], p["conv_ln_g"], p["conv_ln_b"], dmix)

    late = AFTER_ATTENTION
    parts, kinds = [], []
    if on:
        kinds = [BIG_KIND[n] for n in late]
        g4s = _halves_view([g[n] for n in late], kinds)
    da, db, g["conv_dw_w"], g["conv_dw_b"], received = _glu_conv_bwd(
        s["proj"], p["conv_dw_w"], dc1, _exchange_guest(g4s) if on else None)
    if on:
        parts = _add_halves(g4s, received, kinds, core)
    (dq, dk, dv, g["q_norm_g"], g["k_norm_g"]), landed = _attn_bwd(
        s["proj"], p["q_norm_g"], p["k_norm_g"], s["ltot"], dmix, parts, kinds)
    guest = _share_guest(_add_parts(parts, landed, place, kinds)) if on else None
    pieces = [dq, dk, dv, da, db]
    dx, g["norm1_g"], shared = _matmul_nt_rmsbwd(pieces, p["w_in"], s["x"], p["norm1_g"], dx_mid, guest)
    g["w_in"] = _matmul_tn([s["h1t"]], pieces, tk=D_MODEL, tn=ATTN_WIDTH)
    sums = {}
    if on:
        sums = dict(zip(late, finished(shared)), pending=finished(shared_p))
    return dx, g, sums


WEIGHTS = ("norm1_g", "w_in", "q_norm_g", "k_norm_g", "conv_dw_w", "conv_dw_b", "conv_ln_g", "conv_ln_b",
           "w_out", "norm2_g", "w_up", "ffn_dw_w", "ffn_dw_b", "w_down")
BIG = ("w_in", "w_out", "w_up", "w_down")
BIG_KIND = {"w_in": "col", "w_out": "row", "w_up": "col", "w_down": "row"}
SMALL_SHARDED = ("conv_dw_w", "ffn_dw_w")
REPLICATED = tuple(n for n in WEIGHTS if n not in BIG + SMALL_SHARDED)


def _pack(arrays):
    flat = jnp.concatenate([a.reshape(-1) for a in arrays])
    rows = -(-flat.shape[0] // (SUBLANES * LANES)) * SUBLANES
    return jnp.pad(flat, (0, rows * LANES - flat.shape[0])).reshape(rows, LANES)


def _unpack(packed, shapes):
    flat = packed.reshape(-1)
    out, off = [], 0
    for shape in shapes:
        size = 1
        for d in shape:
            size *= d
        out.append(flat[off:off + size].reshape(shape))
        off += size
    return out


def _unshard_last(stacked):
    n, L, K, C = stacked.shape
    return jnp.transpose(stacked, (1, 2, 0, 3)).reshape(L, K, n * C)


def kernel(x, norm1_g, w_in, q_norm_g, k_norm_g, conv_dw_w, conv_dw_b, conv_ln_g, conv_ln_b, w_out, norm2_g, w_up, ffn_dw_w, ffn_dw_b, w_down, loss_target, m_norm1_g, m_w_in, m_q_norm_g, m_k_norm_g, m_conv_dw_w, m_conv_dw_b, m_conv_ln_g, m_conv_ln_b, m_w_out, m_norm2_g, m_w_up, m_ffn_dw_w, m_ffn_dw_b, m_w_down, v_norm1_g, v_w_in, v_q_norm_g, v_k_norm_g, v_conv_dw_w, v_conv_dw_b, v_conv_ln_g, v_conv_ln_b, v_w_out, v_norm2_g, v_w_up, v_ffn_dw_w, v_ffn_dw_b, v_w_down):
    given = dict(locals())
    w = {n: given[n] for n in WEIGHTS}
    m = {n: given["m_" + n] for n in WEIGHTS}
    v = {n: given["v_" + n] for n in WEIGHTS}
    chip = 2 * lax.axis_index("x") + lax.axis_index("y")
    core = lax.axis_index("c")
    chip_arr = jnp.reshape(chip, (1,)).astype(jnp.int32)
    core_arr = jnp.reshape(core, (1,)).astype(jnp.int32)
    L = DEPTH

    place = jnp.concatenate([chip_arr, core_arr])

    first = _cast_into_full([w["w_in"]], [0], ["col"], chip_arr)
    rest = [(l, n) for l in range(L) for n in BIG if (l, n) != (0, "w_in")]
    cast = dict(zip(rest, _cast_into_full([w[n] for _, n in rest], [l for l, _ in rest],
                                          [BIG_KIND[n] for _, n in rest], chip_arr)))
    cast[0, "w_in"] = first[0]
    full = [{n: cast[l, n] for n in BIG} for l in range(L)]
    full[0]["w_in"] = _gather_weights([full[0]["w_in"]], ["col"])[0]
    small_full = {n: _unshard_last(stacked)
                  for n, stacked in zip(SMALL_SHARDED, _gather_small([w[n] for n in SMALL_SHARDED]))}
    params = []
    for l in range(L):
        p = {n: small_full[n][l] for n in SMALL_SHARDED}
        p.update({n: w[n][l][None] for n in REPLICATED})
        params.append(p)

    act = x[0]
    saved = []
    for l in range(L):
        act, s = _layer_fwd(act, params[l], full, l)
        saved.append(s)
    for l in range(L):
        params[l].update({n: full[l][n][0] for n in BIG})
    dx, loss_part = _loss_grad(act, loss_target[0])
    loss = lax.psum(loss_part, ("x", "y", "c"))

    grads = [None] * L
    summed = {}
    pending = []
    for l in reversed(range(L)):
        dx, grads[l], sums = _layer_bwd(dx, saved[l], params[l], (core_arr, place, pending))
        summed.update({(l, n): sums[n] for n in AFTER_ATTENTION})
        if pending:
            summed[l + 1, "w_in"] = sums["pending"][0]
        pending = [grads[l]["w_in"]]
    parts = _chip_partials(pending, ["col"], core_arr)
    summed[0, "w_in"] = _owned_sums(parts, _scatter_partials(parts, ["col"]), ["col"], place)[0]

    grad, delta, new_m, new_v = {}, {}, {}, {}
    for n in BIG:
        grad[n], delta[n], new_m[n], new_v[n] = _adamw_stacked(w[n], [summed[l, n] for l in range(L)], m[n], v[n])

    small = REPLICATED + SMALL_SHARDED
    small_grads = [jnp.stack([grads[l][n] for l in range(L)]) for n in small]
    small_sums = _unpack(_allreduce_small(_pack(small_grads)), [a.shape for a in small_grads])
    for n, g in zip(small, small_sums):
        if n in REPLICATED:
            grad[n] = g.reshape(w[n].shape)
        else:
            width = w[n].shape[2]
            grad[n] = lax.dynamic_slice_in_dim(g, chip * width, width, axis=2)
    for n in small:
        delta[n], new_m[n], new_v[n] = _adamw_small(w[n], grad[n], m[n], v[n])

    return (loss, dx[None], *[grad[n] for n in WEIGHTS], *[delta[n] for n in WEIGHTS],
            *[new_m[n] for n in WEIGHTS], *[new_v[n] for n in WEIGHTS])
```

```python
import jax
import jax.numpy as jnp
from jax import lax
from jax.experimental import pallas as pl
from jax.experimental.pallas import tpu as pltpu

F32 = jnp.float32
BF16 = jnp.bfloat16

DEPTH = 4
D_MODEL = 1024
HEADS = 8
HEAD_DIM = 64
ATTN_WIDTH = HEADS * HEAD_DIM
CONV_WIDTH = D_MODEL - ATTN_WIDTH
CONV_KERNEL = 31
D_FF = 2816
FFN_KERNEL = 3
EPS = 1e-6
ADAM_LR, ADAM_B1, ADAM_B2, ADAM_EPS, ADAM_WD, ADAM_STEP = 0.001, 0.9, 0.999, 1e-08, 0.01, 10

N_CHIPS = 4
N_DEV = 8
LANES = 128
SUBLANES = 8
VMEM_LIMIT_BYTES = 56 * 2**20
ATTN_TILE = 256
ATTN_HEADS_PER_STEP = 4
ATTN_BLOCK = ATTN_HEADS_PER_STEP * HEAD_DIM
CONV_PAD = 32
FFN_PAD = 8
MESH = pl.DeviceIdType.MESH


def _params(*sem):
    return pltpu.CompilerParams(dimension_semantics=sem if sem else None, vmem_limit_bytes=VMEM_LIMIT_BYTES)


class _Guest:
    def __init__(self, name, ins, out_shape, aliases, copies, n_sem):
        self.name, self.ins, self.out_shape, self.aliases, self.copies, self.n_sem = name, ins, out_shape, aliases, copies, n_sem


def _hosted_call(body, guest, *, name, grid, in_specs, out_specs, out_shape, scratch_shapes, operands):
    if guest is None:
        out = pl.pallas_call(body, name=name, grid=grid, in_specs=in_specs, out_specs=out_specs, out_shape=out_shape,
                             scratch_shapes=scratch_shapes, compiler_params=_params("arbitrary"))(*operands)
        return list(out), []
    n_in, n_out, n_scr = len(in_specs), len(out_specs), len(scratch_shapes)
    gi, go = len(guest.ins), len(guest.out_shape)

    def hosting(*refs):
        ins, g_in = refs[:n_in], refs[n_in:n_in + gi]
        outs, g_out = refs[n_in + gi:n_in + gi + n_out], refs[n_in + gi + n_out:n_in + gi + n_out + go]
        scratch, (send, recv) = refs[n_in + gi + n_out + go:-2], refs[-2:]

        @pl.when(pl.program_id(0) == 0)
        def _():
            for cp in guest.copies(g_in, g_out, send, recv):
                cp.start()

        body(*ins, *outs, *scratch)

        @pl.when(pl.program_id(0) == grid[0] - 1)
        def _():
            for cp in guest.copies(g_in, g_out, send, recv):
                cp.wait()

    out = pl.pallas_call(
        hosting, name=name + "_" + guest.name, grid=grid,
        in_specs=list(in_specs) + [ANY] * gi, out_specs=list(out_specs) + [ANY] * go,
        out_shape=list(out_shape) + list(guest.out_shape),
        scratch_shapes=list(scratch_shapes) + [pltpu.SemaphoreType.DMA((guest.n_sem,))] * 2,
        input_output_aliases={n_in + a: n_out + b for a, b in guest.aliases.items()},
        compiler_params=pltpu.CompilerParams(dimension_semantics=("arbitrary",), vmem_limit_bytes=VMEM_LIMIT_BYTES,
                                             has_side_effects=True),
    )(*operands, *guest.ins)
    return list(out[:n_out]), list(out[n_out:])


def _dot(a, b):
    return jnp.dot(a, b, preferred_element_type=F32)


def _dot_nt(a, b):
    return lax.dot_general(a, b, (((1,), (1,)), ((), ())), preferred_element_type=F32)


def _dot_tn(a, b):
    return lax.dot_general(a, b, (((0,), (0,)), ((), ())), preferred_element_type=F32)


def _sigmoid(x):
    return 1.0 / (1.0 + jnp.exp(-x))


def _norm_matmul(x, g, w, guest=None, *, out_dtype=F32, tm=256):
    S, D = x.shape
    N = w.shape[1]

    def body(x_ref, g_ref, w_ref, ht_ref, y_ref):
        xv = x_ref[...]
        h = xv * lax.rsqrt(jnp.mean(xv * xv, axis=-1, keepdims=True) + EPS) * g_ref[...]
        ht_ref[...] = h.T.astype(BF16)
        y_ref[...] = _dot(h.astype(BF16), w_ref[...]).astype(out_dtype)

    (ht, y), guest_out = _hosted_call(
        body, guest, name="norm_matmul", grid=(S // tm,),
        in_specs=[pl.BlockSpec((tm, D), lambda i: (i, 0)),
                  pl.BlockSpec((1, D), lambda i: (0, 0)),
                  pl.BlockSpec((D, N), lambda i: (0, 0))],
        out_specs=[pl.BlockSpec((D, tm), lambda i: (0, i)),
                   pl.BlockSpec((tm, N), lambda i: (i, 0))],
        out_shape=[jax.ShapeDtypeStruct((D, S), BF16), jax.ShapeDtypeStruct((S, N), out_dtype)],
        scratch_shapes=[], operands=[x, g, w])
    return ht, y, guest_out


def _matmul_res(pieces, w, res, guest=None, *, tm=512):
    S, N = res.shape
    K = w.shape[0]
    widths = [p.shape[1] for p in pieces]
    assert sum(widths) == K

    def body(*refs):
        p_refs, (w_ref, res_ref, o_ref) = refs[:len(pieces)], refs[len(pieces):]
        acc = res_ref[...]
        off = 0
        for p_ref, kp in zip(p_refs, widths):
            acc = acc + _dot(p_ref[...], w_ref[off:off + kp, :])
            off += kp
        o_ref[...] = acc

    (out,), guest_out = _hosted_call(
        body, guest, name="matmul_res", grid=(S // tm,),
        in_specs=[pl.BlockSpec((tm, kp), lambda i: (i, 0)) for kp in widths]
        + [pl.BlockSpec((K, N), lambda i: (0, 0)), pl.BlockSpec((tm, N), lambda i: (i, 0))],
        out_specs=[pl.BlockSpec((tm, N), lambda i: (i, 0))],
        out_shape=[jax.ShapeDtypeStruct((S, N), F32)],
        scratch_shapes=[], operands=[*pieces, w, res])
    return out, guest_out


def _nt_sum(p_refs, widths, w_ref):
    acc = None
    off = 0
    for p_ref, n_p in zip(p_refs, widths):
        d = _dot_nt(p_ref[...].astype(BF16), w_ref[:, off:off + n_p])
        acc = d if acc is None else acc + d
        off += n_p
    return acc


def _matmul_nt(pieces, w, out_dtype, guest=None, *, tm=512):
    S = pieces[0].shape[0]
    K, N = w.shape
    widths = [p.shape[1] for p in pieces]
    assert sum(widths) == N

    def body(*refs):
        p_refs, (w_ref, o_ref) = refs[:len(pieces)], refs[len(pieces):]
        o_ref[...] = _nt_sum(p_refs, widths, w_ref).astype(out_dtype)

    (out,), guest_out = _hosted_call(
        body, guest, name="matmul_nt", grid=(S // tm,),
        in_specs=[pl.BlockSpec((tm, n_p), lambda i: (i, 0)) for n_p in widths]
        + [pl.BlockSpec((K, N), lambda i: (0, 0))],
        out_specs=[pl.BlockSpec((tm, K), lambda i: (i, 0))],
        out_shape=[jax.ShapeDtypeStruct((S, K), out_dtype)],
        scratch_shapes=[], operands=[*pieces, w])
    return out, guest_out


def _matmul_nt_rmsbwd(pieces, w, x, g, dres, guest=None, *, tm=256):
    S, K = x.shape
    N = w.shape[1]
    widths = [p.shape[1] for p in pieces]
    assert sum(widths) == N

    def body(*refs):
        p_refs, (w_ref, x_ref, g_ref, dres_ref, dx_ref, dg_ref) = refs[:len(pieces)], refs[len(pieces):]
        dh = _nt_sum(p_refs, widths, w_ref)
        xv = x_ref[...]
        r = lax.rsqrt(jnp.mean(xv * xv, axis=-1, keepdims=True) + EPS)
        xh = xv * r
        dxh = dh * g_ref[...]
        dx_ref[...] = dres_ref[...] + r * (dxh - xh * jnp.mean(dxh * xh, axis=-1, keepdims=True))

        @pl.when(pl.program_id(0) == 0)
        def _():
            dg_ref[...] = jnp.zeros_like(dg_ref)

        dg_ref[...] += jnp.sum(dh * xh, axis=0, keepdims=True)

    (dx, dg), guest_out = _hosted_call(
        body, guest, name="matmul_nt_rmsbwd", grid=(S // tm,),
        in_specs=[pl.BlockSpec((tm, n_p), lambda i: (i, 0)) for n_p in widths]
        + [pl.BlockSpec((K, N), lambda i: (0, 0)), pl.BlockSpec((tm, K), lambda i: (i, 0)),
           pl.BlockSpec((1, K), lambda i: (0, 0)), pl.BlockSpec((tm, K), lambda i: (i, 0))],
        out_specs=[pl.BlockSpec((tm, K), lambda i: (i, 0)), pl.BlockSpec((1, K), lambda i: (0, 0))],
        out_shape=[jax.ShapeDtypeStruct((S, K), F32), jax.ShapeDtypeStruct((1, K), F32)],
        scratch_shapes=[], operands=[*pieces, w, x, g, dres])
    return dx, dg, guest_out


def _matmul_tn(xts, dys, *, tk, tn, ts=1024):
    S = dys[0].shape[0]
    xs = xts
    n_s = S // ts
    (mt,) = {x.shape[0] // tk for x in xts}
    (nt,) = {d.shape[1] // tn for d in dys}

    def body(*refs):
        x_refs, dy_refs, (o_ref, acc_ref) = refs[:len(xs)], refs[len(xs):len(xs) + len(dys)], refs[len(xs) + len(dys):]
        i, j, s = pl.program_id(0), pl.program_id(1), pl.program_id(2)

        @pl.when(s == 0)
        def _():
            acc_ref[...] = jnp.zeros_like(acc_ref)

        for a, x_ref in enumerate(x_refs):
            for b, dy_ref in enumerate(dy_refs):
                @pl.when((i // mt == a) & (j // nt == b))
                def _():
                    acc_ref[...] += _dot(x_ref[...], dy_ref[...].astype(BF16))

        @pl.when(s == n_s - 1)
        def _():
            o_ref[...] = acc_ref[...].astype(BF16)

    def x_map(a):
        return lambda i, j, s: (jnp.where(i // mt == a, i % mt, 0), jnp.where(i // mt == a, s, 0))

    def dy_map(b):
        return lambda i, j, s: (jnp.where(j // nt == b, s, 0), jnp.where(j // nt == b, j % nt, 0))

    return pl.pallas_call(
        body, name="matmul_tn", grid=(len(xs) * mt, len(dys) * nt, n_s),
        in_specs=[pl.BlockSpec((tk, ts), x_map(a)) for a in range(len(xs))]
        + [pl.BlockSpec((ts, tn), dy_map(b)) for b in range(len(dys))],
        out_specs=pl.BlockSpec((tk, tn), lambda i, j, s: (i, j)),
        out_shape=jax.ShapeDtypeStruct((len(xs) * mt * tk, len(dys) * nt * tn), BF16),
        scratch_shapes=[pltpu.VMEM((tk, tn), F32)],
        compiler_params=_params("parallel", "parallel", "arbitrary"),
    )(*xs, *dys)


def _tri_consts():
    j = jnp.arange(ATTN_TILE)[:, None]
    s = jnp.arange(ATTN_TILE)[None, :]
    return (j > s).astype(BF16), (j <= s).astype(BF16), (j < s).astype(BF16)


SIGN_BIT = 0x80000000
WEIGHT_IS_ZERO = -104.0


def _log_terms(sn):
    minus_abs = lax.bitcast_convert_type(lax.bitcast_convert_type(sn, jnp.uint32) | jnp.uint32(SIGN_BIT), F32)
    lom = jnp.minimum(sn, 0.0) - jnp.log(1.0 + jnp.exp(minus_abs))
    return lom, lom - sn


def _causal_mask():
    t = lax.broadcasted_iota(jnp.int32, (ATTN_TILE, ATTN_TILE), 0)
    s = lax.broadcasted_iota(jnp.int32, (ATTN_TILE, ATTN_TILE), 1)
    return s < t


def _attn_prep(h, q_ref, k_ref, v_ref, qg_ref, kg_ref, qn_s, kn_s, vb_s, n_tiles):
    T = ATTN_TILE
    lanes = slice(HEAD_DIM * h, HEAD_DIM * (h + 1))
    scale = -(HEAD_DIM ** -0.5)

    def prep(i, carry):
        rows = pl.ds(pl.multiple_of(i * T, T), T)
        q = q_ref[rows, lanes]
        k = k_ref[rows, lanes]
        rq = lax.rsqrt(jnp.mean(q * q, axis=-1, keepdims=True) + EPS)
        rk = lax.rsqrt(jnp.mean(k * k, axis=-1, keepdims=True) + EPS)
        qn_s[h, rows, :] = (q * rq * qg_ref[...] * scale).astype(BF16)
        kn_s[h, rows, :] = (k * rk * kg_ref[...]).astype(BF16)
        vb_s[h, rows, :] = v_ref[rows, lanes].astype(BF16)
        return carry

    lax.fori_loop(0, n_tiles, prep, 0)


def _attn_fwd(proj, qg, kg, fulls=(), kinds=()):
    S = proj.shape[0]
    n_comm = len(fulls)
    T = ATTN_TILE
    n_tiles = S // T
    suffix, _, _ = _tri_consts()
    pairs = HEADS // ATTN_HEADS_PER_STEP
    q_blk, k_blk, v_blk = 0, ATTN_WIDTH // ATTN_BLOCK, 2 * ATTN_WIDTH // ATTN_BLOCK

    def body(*refs):
        q_ref, k_ref, v_ref, qg_ref, kg_ref, tri_ref = refs[:6]
        o_ref, lt_ref, ot_ref, start_ref, qn_s, kn_s, vb_s = refs[6 + n_comm:13 + n_comm]
        w_refs = refs[13 + n_comm:13 + 2 * n_comm]
        sems = refs[13 + 2 * n_comm:]
        step_id = pl.program_id(0)
        if n_comm:
            @pl.when(pl.program_id(0) == 0)
            def _():
                for cp in _gather_copies(w_refs, kinds, *sems, 0):
                    cp.start()

        heads = range(ATTN_HEADS_PER_STEP)
        for h in heads:
            _attn_prep(h, q_ref, k_ref, v_ref, qg_ref, kg_ref, qn_s, kn_s, vb_s, n_tiles)

        def q_tile(qi, carry0):
            qrows = pl.ds(pl.multiple_of(qi * T, T), T)
            qt = [qn_s[h, qrows, :] for h in heads]

            def tile(kj, carry, diag):
                krows = pl.ds(pl.multiple_of(kj * T, T), T)
                s = [_dot_nt(qt[h], kn_s[h, krows, :]) for h in heads]
                terms = [_log_terms(s[h]) for h in heads]
                lom = [terms[h][0] for h in heads]
                if diag:
                    mask = _causal_mask()
                    lom = [jnp.where(mask, lom[h], 0.0) for h in heads]
                lom = [lom[h].astype(BF16) for h in heads]
                tail = [_dot(lom[h], tri_ref[...]) for h in heads]
                w = [jnp.exp(terms[h][1] + tail[h] + carry[h][1]) for h in heads]
                if diag:
                    w = [jnp.where(mask, w[h], 0.0) for h in heads]
                return tuple((carry[h][0] + _dot(w[h].astype(BF16), vb_s[h, krows, :]),
                              carry[h][1] + (tail[h][:, 0:1] + lom[h][:, 0:1].astype(F32))) for h in heads)

            init = tuple((jnp.zeros((T, HEAD_DIM), F32), jnp.zeros((T, 1), F32)) for h in heads)
            def alive(cr):
                worst = cr[0][1]
                for h in heads[1:]:
                    worst = jnp.maximum(worst, cr[h][1])
                return jnp.max(worst) >= WEIGHT_IS_ZERO

            def step(state):
                t, _, cr = state
                cr = tile(qi - 1 - t, cr, False)
                return t + 1, alive(cr), cr

            first = tile(qi, init, True)
            swept, _, carry = lax.while_loop(lambda st: (st[0] < qi) & st[1], step, (jnp.int32(0), alive(first), first))
            start_ref[step_id, qi] = (qi - swept).astype(F32)
            o = jnp.concatenate([carry[h][0] for h in heads], axis=1)
            o_ref[qrows, :] = o.astype(BF16)
            ot_ref[:, qrows] = o.T.astype(BF16)
            for h in heads:
                lt_ref[qrows, HEAD_DIM * h:HEAD_DIM * (h + 1)] = jnp.broadcast_to(carry[h][1], (T, HEAD_DIM))
            return carry0

        lax.fori_loop(0, n_tiles, q_tile, 0)
        if n_comm:
            @pl.when(pl.program_id(0) == pairs - 1)
            def _():
                for cp in _gather_copies(w_refs, kinds, *sems, 0):
                    cp.wait()

    n_sem = n_comm * (N_CHIPS - 1)
    per_head = pl.BlockSpec((ATTN_HEADS_PER_STEP, S, HEAD_DIM), lambda p: (p, 0, 0))
    out = pl.pallas_call(
        body, name="attn_fwd_gather" if n_comm else "attn_fwd", grid=(pairs,),
        in_specs=[pl.BlockSpec((S, ATTN_BLOCK), lambda p: (0, q_blk + p)),
                  pl.BlockSpec((S, ATTN_BLOCK), lambda p: (0, k_blk + p)),
                  pl.BlockSpec((S, ATTN_BLOCK), lambda p: (0, v_blk + p)),
                  pl.BlockSpec((1, HEAD_DIM), lambda p: (0, 0)),
                  pl.BlockSpec((1, HEAD_DIM), lambda p: (0, 0)),
                  pl.BlockSpec((T, T), lambda p: (0, 0))] + [ANY] * n_comm,
        out_specs=[pl.BlockSpec((S, ATTN_BLOCK), lambda p: (0, p)),
                   pl.BlockSpec((None, S, ATTN_BLOCK), lambda p: (p, 0, 0)),
                   pl.BlockSpec((ATTN_BLOCK, S), lambda p: (p, 0)),
                   pl.BlockSpec(memory_space=pltpu.SMEM)] + [per_head] * 3 + [ANY] * n_comm,
        out_shape=[jax.ShapeDtypeStruct((S, ATTN_WIDTH), BF16),
                   jax.ShapeDtypeStruct((pairs, S, ATTN_BLOCK), F32),
                   jax.ShapeDtypeStruct((ATTN_WIDTH, S), BF16),
                   jax.ShapeDtypeStruct((pairs, n_tiles), F32)] + [jax.ShapeDtypeStruct((HEADS, S, HEAD_DIM), BF16)] * 3
        + [jax.ShapeDtypeStruct(f.shape, f.dtype) for f in fulls],
        scratch_shapes=[pltpu.SemaphoreType.DMA((n_sem,))] * 2 if n_comm else [],
        input_output_aliases={6 + a: 7 + a for a in range(n_comm)},
        compiler_params=pltpu.CompilerParams(dimension_semantics=("arbitrary",), vmem_limit_bytes=VMEM_LIMIT_BYTES,
                                             has_side_effects=bool(n_comm)),
    )(proj, proj, proj, qg, kg, suffix, *fulls)
    return out[0], (out[1], out[3], out[4], out[5], out[6]), out[2], list(out[7:])


def _attn_bwd(proj, qg, kg, ltot, dmix, parts=(), kinds=()):
    S = proj.shape[0]
    n_comm = len(parts)
    T = ATTN_TILE
    n_tiles = S // T
    _, prefix_incl, prefix_excl = _tri_consts()
    pairs = HEADS // ATTN_HEADS_PER_STEP
    q_blk, k_blk, v_blk = 0, ATTN_WIDTH // ATTN_BLOCK, 2 * ATTN_WIDTH // ATTN_BLOCK
    scale = HEAD_DIM ** -0.5

    def body(*refs):
        q_ref, k_ref, qg_ref, kg_ref, lt_ref, do_ref, ti_ref, te_ref, start_ref, qn_s, kn_s, vb_s = refs[:12]
        p_refs = refs[12:12 + n_comm]
        dq_ref, dk_ref, dv_ref, dqg_ref, dkg_ref = refs[12 + n_comm:17 + n_comm]
        lb_refs = refs[17 + n_comm:17 + 2 * n_comm]
        dq_s, dk_s, dv_s = refs[17 + 2 * n_comm:20 + 2 * n_comm]
        sems = refs[20 + 2 * n_comm:]
        step_id = pl.program_id(0)

        @pl.when(pl.program_id(0) == 0)
        def _():
            dqg_ref[...] = jnp.zeros_like(dqg_ref)
            dkg_ref[...] = jnp.zeros_like(dkg_ref)
            for cp in _scatter_copies(p_refs, lb_refs, kinds, *sems) if n_comm else ():
                cp.start()

        heads = range(ATTN_HEADS_PER_STEP)
        dk_s[...] = jnp.zeros_like(dk_s)
        dv_s[...] = jnp.zeros_like(dv_s)

        def q_tile(qi, carry0):
            qrows = pl.ds(pl.multiple_of(qi * T, T), T)
            qt = [qn_s[h, qrows, :] for h in heads]
            dob = [do_ref[qrows, HEAD_DIM * h:HEAD_DIM * (h + 1)].astype(BF16) for h in heads]
            lt = [lt_ref[qrows, HEAD_DIM * h:HEAD_DIM * h + 1] for h in heads]

            def tile(kj, carry, diag):
                krows = pl.ds(pl.multiple_of(kj * T, T), T)
                kt = [kn_s[h, krows, :] for h in heads]
                s = [_dot_nt(qt[h], kt[h]) for h in heads]
                dw = [_dot_nt(dob[h], vb_s[h, krows, :]) for h in heads]
                terms = [_log_terms(s[h]) for h in heads]
                lom = [terms[h][0] for h in heads]
                if diag:
                    mask = _causal_mask()
                    lom = [jnp.where(mask, lom[h], 0.0) for h in heads]
                pin = [_dot(lom[h].astype(BF16), ti_ref[...]) for h in heads]
                w = [jnp.exp(terms[h][1] + ((lt[h] - carry[h][1]) - pin[h])) for h in heads]
                if diag:
                    w = [jnp.where(mask, w[h], 0.0) for h in heads]
                e = [w[h] * dw[h] for h in heads]
                gex = [_dot(e[h].astype(BF16), te_ref[...]) for h in heads]
                dz = [e[h] - jnp.exp(terms[h][1]) * (e[h] + (carry[h][2] + gex[h])) for h in heads]
                if diag:
                    dz = [jnp.where(mask, dz[h], 0.0) for h in heads]
                new = []
                for h in heads:
                    dzb = dz[h].astype(BF16)
                    dk_s[h, krows, :] += _dot_tn(dzb, qt[h])
                    dv_s[h, krows, :] += _dot_tn(w[h].astype(BF16), dob[h])
                    new.append((carry[h][0] + _dot(dzb, kt[h]),
                                carry[h][1] + pin[h][:, T - 1:T],
                                carry[h][2] + gex[h][:, T - 1:T] + e[h][:, T - 1:T]))
                return tuple(new)

            zero = jnp.zeros((T, 1), F32)
            init = tuple((jnp.zeros((T, HEAD_DIM), F32), zero, zero) for h in heads)
            first = jnp.clip(start_ref[step_id, qi].astype(jnp.int32), 0, qi)
            last = tile(qi, lax.fori_loop(first, qi, lambda kj, cr: tile(kj, cr, False), init), True)
            for h in heads:
                dq_s[h, qrows, :] = last[h][0]
            return carry0

        lax.fori_loop(0, n_tiles, q_tile, 0)

        def finish(i, carry):
            rows = pl.ds(pl.multiple_of(i * T, T), T)
            new = []
            for h in heads:
                lanes = slice(HEAD_DIM * h, HEAD_DIM * (h + 1))
                q = q_ref[rows, lanes]
                k = k_ref[rows, lanes]
                rq = lax.rsqrt(jnp.mean(q * q, axis=-1, keepdims=True) + EPS)
                rk = lax.rsqrt(jnp.mean(k * k, axis=-1, keepdims=True) + EPS)
                qh = q * rq
                kh = k * rk
                dqn = dq_s[h, rows, :] * scale
                dkn = -dk_s[h, rows, :]
                dqh = dqn * qg_ref[...]
                dkh = dkn * kg_ref[...]
                dq_ref[rows, lanes] = (rq * (dqh - qh * jnp.mean(dqh * qh, axis=-1, keepdims=True))).astype(BF16)
                dk_ref[rows, lanes] = (rk * (dkh - kh * jnp.mean(dkh * kh, axis=-1, keepdims=True))).astype(BF16)
                dv_ref[rows, lanes] = dv_s[h, rows, :].astype(BF16)
                new.append(carry[2 * h] + jnp.sum(dqn * qh, axis=0, keepdims=True))
                new.append(carry[2 * h + 1] + jnp.sum(dkn * kh, axis=0, keepdims=True))
            return tuple(new)

        zero = jnp.zeros((1, HEAD_DIM), F32)
        sums = lax.fori_loop(0, n_tiles, finish, (zero,) * (2 * len(heads)))
        dqg_ref[...] += sum(sums[0::2])
        dkg_ref[...] += sum(sums[1::2])
        if n_comm:
            @pl.when(pl.program_id(0) == pairs - 1)
            def _():
                for cp in _scatter_copies(p_refs, lb_refs, kinds, *sems):
                    cp.wait()

    blk = lambda off: pl.BlockSpec((S, ATTN_BLOCK), lambda p: (0, off + p))
    row64 = pl.BlockSpec((1, HEAD_DIM), lambda p: (0, 0))
    tri = pl.BlockSpec((T, T), lambda p: (0, 0))
    n_sem = n_comm * (N_CHIPS - 1)
    out = pl.pallas_call(
        body, name="attn_bwd_scatter" if n_comm else "attn_bwd", grid=(pairs,),
        in_specs=[blk(q_blk), blk(k_blk), row64, row64,
                  pl.BlockSpec((None, S, ATTN_BLOCK), lambda p: (p, 0, 0)), blk(0), tri, tri,
                  pl.BlockSpec(memory_space=pltpu.SMEM)]
        + [pl.BlockSpec((ATTN_HEADS_PER_STEP, S, HEAD_DIM), lambda p: (p, 0, 0))] * 3 + [ANY] * n_comm,
        out_specs=[blk(0), blk(0), blk(0), row64, row64] + [ANY] * n_comm,
        out_shape=[jax.ShapeDtypeStruct((S, ATTN_WIDTH), BF16)] * 3 + [jax.ShapeDtypeStruct((1, HEAD_DIM), F32)] * 2
        + _scatter_shapes(parts, kinds),
        scratch_shapes=[pltpu.VMEM((ATTN_HEADS_PER_STEP, S, HEAD_DIM), F32)] * 3
        + ([pltpu.SemaphoreType.DMA((n_sem,))] * 2 if n_comm else []),
        compiler_params=pltpu.CompilerParams(dimension_semantics=("arbitrary",), vmem_limit_bytes=VMEM_LIMIT_BYTES,
                                             has_side_effects=bool(n_comm)),
    )(proj, proj, qg, kg, ltot[0], dmix, prefix_incl, prefix_excl, *ltot[1:], *parts)
    return out[:5], list(out[5:])


def _shifted(win, n_rows):
    return [win if b == 0 else pltpu.roll(win, n_rows - b, 0) for b in range(SUBLANES)]


def _taps(variants, offsets, tm):
    return {o: variants[o % SUBLANES][(o // SUBLANES) * SUBLANES:(o // SUBLANES) * SUBLANES + tm, :] for o in offsets}


def _fold_rows(a):
    return jnp.sum(a.reshape(a.shape[0] // SUBLANES, SUBLANES, a.shape[1]), axis=0)


def _glu_conv_fwd(proj, w, bias, guest=None, *, tm=256):
    S = proj.shape[0]
    CB = LANES
    a_blk, b_blk = 3 * ATTN_WIDTH // CB, (3 * ATTN_WIDTH + CONV_WIDTH) // CB
    n_rows = tm + CONV_PAD

    def body(a_ref, b_ref, w_ref, bias_ref, c1_ref, pad_s):
        pad_s[0:CONV_PAD, :] = jnp.zeros((CONV_PAD, CB), F32)

        def fill(i, carry):
            rows = pl.ds(pl.multiple_of(i * tm, tm), tm)
            pad_s[pl.ds(pl.multiple_of(CONV_PAD + i * tm, SUBLANES), tm), :] = a_ref[rows, :] * _sigmoid(b_ref[rows, :])
            return carry

        lax.fori_loop(0, S // tm, fill, 0)

        def conv(i, carry):
            r0 = pl.multiple_of(i * tm, tm)
            taps = _taps(_shifted(pad_s[pl.ds(r0, n_rows), :], n_rows), range(2, 2 + CONV_KERNEL), tm)
            acc = jnp.broadcast_to(bias_ref[...], (tm, CB))
            for k in range(CONV_KERNEL):
                acc = acc + w_ref[k:k + 1, :] * taps[k + 2]
            c1_ref[pl.ds(r0, tm), :] = acc
            return carry

        lax.fori_loop(0, S // tm, conv, 0)

    (c1,), guest_out = _hosted_call(
        body, guest, name="glu_conv_fwd", grid=(CONV_WIDTH // CB,),
        in_specs=[pl.BlockSpec((S, CB), lambda j: (0, a_blk + j)), pl.BlockSpec((S, CB), lambda j: (0, b_blk + j)),
                  pl.BlockSpec((CONV_KERNEL, CB), lambda j: (0, j)), pl.BlockSpec((1, CB), lambda j: (0, j))],
        out_specs=[pl.BlockSpec((S, CB), lambda j: (0, j))],
        out_shape=[jax.ShapeDtypeStruct((S, CONV_WIDTH), F32)],
        scratch_shapes=[pltpu.VMEM((S + CONV_PAD, CB), F32)], operands=[proj, proj, w, bias])
    return c1, guest_out


def _glu_conv_bwd(proj, w, dc1, guest=None, *, tm=256):
    S = proj.shape[0]
    CB = LANES
    a_blk, b_blk = 3 * ATTN_WIDTH // CB, (3 * ATTN_WIDTH + CONV_WIDTH) // CB
    n_rows = tm + CONV_PAD

    def body(a_ref, b_ref, w_ref, dc1_ref, da_ref, db_ref, dw_ref, dbias_ref, pad_s, dpad_s, dw_s):
        pad_s[0:CONV_PAD, :] = jnp.zeros((CONV_PAD, CB), F32)
        dpad_s[S:S + CONV_PAD, :] = jnp.zeros((CONV_PAD, CB), F32)
        dw_s[...] = jnp.zeros_like(dw_s)

        def fill(i, carry):
            rows = pl.ds(pl.multiple_of(i * tm, tm), tm)
            pad_s[pl.ds(pl.multiple_of(CONV_PAD + i * tm, SUBLANES), tm), :] = a_ref[rows, :] * _sigmoid(b_ref[rows, :])
            dpad_s[rows, :] = dc1_ref[rows, :]
            return carry

        lax.fori_loop(0, S // tm, fill, 0)

        def conv(i, carry):
            r0 = pl.multiple_of(i * tm, tm)
            rows = pl.ds(r0, tm)
            taps = _taps(_shifted(dpad_s[pl.ds(r0, n_rows), :], n_rows), range(CONV_KERNEL), tm)
            acc = jnp.zeros((tm, CB), F32)
            for k in range(CONV_KERNEL):
                acc = acc + w_ref[k:k + 1, :] * taps[CONV_KERNEL - 1 - k]
            a = a_ref[rows, :]
            sg = _sigmoid(b_ref[rows, :])
            da_ref[rows, :] = (acc * sg).astype(BF16)
            db_ref[rows, :] = (acc * a * sg * (1.0 - sg)).astype(BF16)
            d = taps[0]
            taps = _taps(_shifted(pad_s[pl.ds(r0, n_rows), :], n_rows), range(2, 2 + CONV_KERNEL), tm)
            for k in range(CONV_KERNEL):
                dw_s[SUBLANES * k:SUBLANES * (k + 1), :] += _fold_rows(d * taps[k + 2])
            dw_s[SUBLANES * CONV_KERNEL:SUBLANES * (CONV_KERNEL + 1), :] += _fold_rows(d)
            return carry

        lax.fori_loop(0, S // tm, conv, 0)
        for k in range(CONV_KERNEL):
            dw_ref[k:k + 1, :] = jnp.sum(dw_s[SUBLANES * k:SUBLANES * (k + 1), :], axis=0, keepdims=True)
        dbias_ref[...] = jnp.sum(dw_s[SUBLANES * CONV_KERNEL:SUBLANES * (CONV_KERNEL + 1), :], axis=0, keepdims=True)

    col = lambda off: pl.BlockSpec((S, CB), lambda j: (0, off + j))
    out, guest_out = _hosted_call(
        body, guest, name="glu_conv_bwd", grid=(CONV_WIDTH // CB,),
        in_specs=[col(a_blk), col(b_blk), pl.BlockSpec((CONV_KERNEL, CB), lambda j: (0, j)), col(0)],
        out_specs=[col(0), col(0), pl.BlockSpec((CONV_KERNEL, CB), lambda j: (0, j)), pl.BlockSpec((1, CB), lambda j: (0, j))],
        out_shape=[jax.ShapeDtypeStruct((S, CONV_WIDTH), BF16)] * 2
        + [jax.ShapeDtypeStruct((CONV_KERNEL, CONV_WIDTH), F32), jax.ShapeDtypeStruct((1, CONV_WIDTH), F32)],
        scratch_shapes=[pltpu.VMEM((S + CONV_PAD, CB), F32), pltpu.VMEM((S + CONV_PAD, CB), F32),
                        pltpu.VMEM((SUBLANES * (CONV_KERNEL + 1), CB), F32)], operands=[proj, proj, w, dc1])
    return (*out, guest_out)


def _ln_stats(c1):
    mu = jnp.mean(c1, axis=-1, keepdims=True)
    xc = c1 - mu
    r = lax.rsqrt(jnp.mean(xc * xc, axis=-1, keepdims=True) + EPS)
    return xc * r, r


def _ln_silu_fwd(c1, g, b, *, tm=512):
    S, C = c1.shape

    def body(c1_ref, g_ref, b_ref, c_ref, ct_ref):
        yh, _ = _ln_stats(c1_ref[...])
        y = yh * g_ref[...] + b_ref[...]
        c = y * _sigmoid(y)
        c_ref[...] = c.astype(BF16)
        ct_ref[...] = c.T.astype(BF16)

    vec = pl.BlockSpec((1, C), lambda i: (0, 0))
    return pl.pallas_call(
        body, name="ln_silu_fwd", grid=(S // tm,),
        in_specs=[pl.BlockSpec((tm, C), lambda i: (i, 0)), vec, vec],
        out_specs=[pl.BlockSpec((tm, C), lambda i: (i, 0)), pl.BlockSpec((C, tm), lambda i: (0, i))],
        out_shape=[jax.ShapeDtypeStruct((S, C), BF16), jax.ShapeDtypeStruct((C, S), BF16)],
        compiler_params=_params("parallel"),
    )(c1, g, b)


def _ln_silu_bwd(c1, g, b, dmix, *, tm=512):
    S, C = c1.shape

    def body(c1_ref, g_ref, b_ref, dc_ref, dc1_ref, dg_ref, db_ref):
        yh, r = _ln_stats(c1_ref[...])
        y = yh * g_ref[...] + b_ref[...]
        sg = _sigmoid(y)
        dy = dc_ref[...] * (sg * (1.0 + y * (1.0 - sg)))
        dyh = dy * g_ref[...]
        dc1_ref[...] = r * (dyh - jnp.mean(dyh, axis=-1, keepdims=True)
                            - yh * jnp.mean(dyh * yh, axis=-1, keepdims=True))

        @pl.when(pl.program_id(0) == 0)
        def _():
            dg_ref[...] = jnp.zeros_like(dg_ref)
            db_ref[...] = jnp.zeros_like(db_ref)

        dg_ref[...] += jnp.sum(dy * yh, axis=0, keepdims=True)
        db_ref[...] += jnp.sum(dy, axis=0, keepdims=True)

    vec = pl.BlockSpec((1, C), lambda i: (0, 0))
    return pl.pallas_call(
        body, name="ln_silu_bwd", grid=(S // tm,),
        in_specs=[pl.BlockSpec((tm, C), lambda i: (i, 0)), vec, vec, pl.BlockSpec((tm, C), lambda i: (i, 1))],
        out_specs=[pl.BlockSpec((tm, C), lambda i: (i, 0)), vec, vec],
        out_shape=[jax.ShapeDtypeStruct((S, C), F32), jax.ShapeDtypeStruct((1, C), F32), jax.ShapeDtypeStruct((1, C), F32)],
        compiler_params=_params("arbitrary"),
    )(c1, g, b, dmix)


FFN_CB = 256


def _ffn_gate(pad_s, w_ref, bias_ref, r0, tm):
    n_rows = tm + FFN_PAD
    taps = _taps(_shifted(pad_s[pl.ds(r0, n_rows), :], n_rows), range(FFN_PAD - 2, FFN_PAD + 1), tm)
    g1 = bias_ref[...] + w_ref[0:1, :] * taps[6] + w_ref[1:2, :] * taps[7] + w_ref[2:3, :] * taps[8]
    return g1, taps


def _ffn_act_fwd(u, w, bias, guest=None, *, tm=256):
    S = u.shape[0]
    CB = FFN_CB
    nb = D_FF // CB

    def body(g_ref, v_ref, w_ref, bias_ref, o_ref, ot_ref, pad_s):
        pad_s[0:FFN_PAD, :] = jnp.zeros((FFN_PAD, CB), F32)

        def fill(i, carry):
            pad_s[pl.ds(pl.multiple_of(FFN_PAD + i * tm, SUBLANES), tm), :] = g_ref[pl.ds(pl.multiple_of(i * tm, tm), tm), :].astype(F32)
            return carry

        lax.fori_loop(0, S // tm, fill, 0)

        def act(i, carry):
            r0 = pl.multiple_of(i * tm, tm)
            g1, _ = _ffn_gate(pad_s, w_ref, bias_ref, r0, tm)
            a = g1 * _sigmoid(g1) * v_ref[pl.ds(r0, tm), :].astype(F32)
            o_ref[pl.ds(r0, tm), :] = a.astype(BF16)
            ot_ref[:, pl.ds(r0, tm)] = a.T.astype(BF16)
            return carry

        lax.fori_loop(0, S // tm, act, 0)

    (act, actt), guest_out = _hosted_call(
        body, guest, name="ffn_act_fwd", grid=(nb,),
        in_specs=[pl.BlockSpec((S, CB), lambda j: (0, j)), pl.BlockSpec((S, CB), lambda j: (0, nb + j)),
                  pl.BlockSpec((FFN_KERNEL, CB), lambda j: (0, j)), pl.BlockSpec((1, CB), lambda j: (0, j))],
        out_specs=[pl.BlockSpec((S, CB), lambda j: (0, j)), pl.BlockSpec((CB, S), lambda j: (j, 0))],
        out_shape=[jax.ShapeDtypeStruct((S, D_FF), BF16), jax.ShapeDtypeStruct((D_FF, S), BF16)],
        scratch_shapes=[pltpu.VMEM((S + FFN_PAD, CB), F32)], operands=[u, u, w, bias])
    return act, actt, guest_out


def _ffn_act_bwd(u, w, bias, dact, guest=None, *, tm=256):
    S = u.shape[0]
    CB = FFN_CB
    nb = D_FF // CB

    def body(g_ref, v_ref, w_ref, bias_ref, da_ref, dg_ref, dv_ref, dw_ref, dbias_ref, pad_s, dpad_s, dw_s):
        pad_s[0:FFN_PAD, :] = jnp.zeros((FFN_PAD, CB), F32)
        dpad_s[S:S + FFN_PAD, :] = jnp.zeros((FFN_PAD, CB), F32)
        dw_s[...] = jnp.zeros_like(dw_s)

        def fill(i, carry):
            pad_s[pl.ds(pl.multiple_of(FFN_PAD + i * tm, SUBLANES), tm), :] = g_ref[pl.ds(pl.multiple_of(i * tm, tm), tm), :].astype(F32)
            return carry

        lax.fori_loop(0, S // tm, fill, 0)

        def first(i, carry):
            r0 = pl.multiple_of(i * tm, tm)
            rows = pl.ds(r0, tm)
            g1, taps = _ffn_gate(pad_s, w_ref, bias_ref, r0, tm)
            sg = _sigmoid(g1)
            da = da_ref[rows, :].astype(F32)
            dv_ref[rows, :] = (da * g1 * sg).astype(BF16)
            dg1 = da * v_ref[rows, :].astype(F32) * (sg * (1.0 + g1 * (1.0 - sg)))
            dpad_s[rows, :] = dg1
            for k in range(FFN_KERNEL):
                dw_s[SUBLANES * k:SUBLANES * (k + 1), :] += _fold_rows(dg1 * taps[FFN_PAD - 2 + k])
            dw_s[SUBLANES * FFN_KERNEL:SUBLANES * (FFN_KERNEL + 1), :] += _fold_rows(dg1)
            return carry

        lax.fori_loop(0, S // tm, first, 0)

        def second(i, carry):
            r0 = pl.multiple_of(i * tm, tm)
            n_rows = tm + FFN_PAD
            taps = _taps(_shifted(dpad_s[pl.ds(r0, n_rows), :], n_rows), range(FFN_KERNEL), tm)
            dg_ref[pl.ds(r0, tm), :] = (w_ref[2:3, :] * taps[0] + w_ref[1:2, :] * taps[1] + w_ref[0:1, :] * taps[2]).astype(BF16)
            return carry

        lax.fori_loop(0, S // tm, second, 0)
        for k in range(FFN_KERNEL):
            dw_ref[k:k + 1, :] = jnp.sum(dw_s[SUBLANES * k:SUBLANES * (k + 1), :], axis=0, keepdims=True)
        dbias_ref[...] = jnp.sum(dw_s[SUBLANES * FFN_KERNEL:SUBLANES * (FFN_KERNEL + 1), :], axis=0, keepdims=True)

    col = lambda off: pl.BlockSpec((S, CB), lambda j: (0, off + j))
    wspec = pl.BlockSpec((FFN_KERNEL, CB), lambda j: (0, j))
    bspec = pl.BlockSpec((1, CB), lambda j: (0, j))
    out, guest_out = _hosted_call(
        body, guest, name="ffn_act_bwd", grid=(nb,),
        in_specs=[col(0), col(nb), wspec, bspec, col(0)],
        out_specs=[col(0), col(0), wspec, bspec],
        out_shape=[jax.ShapeDtypeStruct((S, D_FF), BF16)] * 2
        + [jax.ShapeDtypeStruct((FFN_KERNEL, D_FF), F32), jax.ShapeDtypeStruct((1, D_FF), F32)],
        scratch_shapes=[pltpu.VMEM((S + FFN_PAD, CB), F32), pltpu.VMEM((S + FFN_PAD, CB), F32),
                        pltpu.VMEM((SUBLANES * (FFN_KERNEL + 1), CB), F32)], operands=[u, u, w, bias, dact])
    return (*out, guest_out)


def _loss_grad(y, target, *, tm=512):
    S, D = y.shape

    def body(y_ref, t_ref, dy_ref, l_ref):
        d = y_ref[...] - t_ref[...]
        dy_ref[...] = d * (1.0 / D)

        @pl.when(pl.program_id(0) == 0)
        def _():
            l_ref[...] = jnp.zeros_like(l_ref)

        l_ref[...] += 0.5 * jnp.sum(jnp.mean(d * d, axis=-1, keepdims=True), axis=0, keepdims=True)

    dy, l = pl.pallas_call(
        body, name="loss_grad", grid=(S // tm,),
        in_specs=[pl.BlockSpec((tm, D), lambda i: (i, 0))] * 2,
        out_specs=[pl.BlockSpec((tm, D), lambda i: (i, 0)), pl.BlockSpec((SUBLANES, LANES), lambda i: (0, 0))],
        out_shape=[jax.ShapeDtypeStruct((S, D), F32), jax.ShapeDtypeStruct((SUBLANES, LANES), F32)],
        compiler_params=_params("arbitrary"),
    )(y, target)
    return dy, l[0, 0]


def _row_tile(rows, cap=512):
    t = min(rows, cap)
    while rows % t or t % SUBLANES:
        t -= 1
    return t


def _adam_update(w, g, m, v):
    m1 = ADAM_B1 * m + (1.0 - ADAM_B1) * g
    v1 = ADAM_B2 * v + (1.0 - ADAM_B2) * (g * g)
    m_hat = m1 / (1.0 - ADAM_B1 ** ADAM_STEP)
    v_hat = v1 / (1.0 - ADAM_B2 ** ADAM_STEP)
    return -ADAM_LR * (m_hat / (jnp.sqrt(v_hat) + ADAM_EPS) + ADAM_WD * w), m1, v1


def _adamw_small(w, g, m, v):
    def body(w_ref, g_ref, m_ref, v_ref, d_ref, nm_ref, nv_ref):
        d_ref[...], nm_ref[...], nv_ref[...] = _adam_update(w_ref[...], g_ref[...], m_ref[...], v_ref[...])

    vmem = pl.BlockSpec(memory_space=pltpu.VMEM)
    return pl.pallas_call(body, name="adamw_small", in_specs=[vmem] * 4, out_specs=[vmem] * 3,
                          out_shape=[jax.ShapeDtypeStruct(w.shape, F32)] * 3)(w, g, m, v)


def _adamw_stacked(w, gs, m, v, guest=None):
    L, R, C = w.shape
    tr = _row_tile(R, 256)
    nr = R // tr

    def body(w_ref, m_ref, v_ref, *rest):
        g_refs, (go_ref, d_ref, nm_ref, nv_ref) = rest[:L], rest[L:]
        for a, g_ref in enumerate(g_refs):
            @pl.when(pl.program_id(0) // nr == a)
            def _():
                gv = g_ref[...]
                go_ref[...] = gv
                d_ref[...], nm_ref[...], nv_ref[...] = _adam_update(w_ref[...], gv, m_ref[...], v_ref[...])

    stacked = pl.BlockSpec((None, tr, C), lambda i: (i // nr, i % nr, 0))
    return _hosted_call(
        body, guest, name="adamw_stacked", grid=(L * nr,),
        in_specs=[stacked] * 3 + [pl.BlockSpec((tr, C), lambda i, a=a: (jnp.where(i // nr == a, i % nr, 0), 0))
                                  for a in range(L)],
        out_specs=[stacked] * 4, out_shape=[jax.ShapeDtypeStruct((L, R, C), F32)] * 4,
        scratch_shapes=[], operands=[w, m, v, *gs])


CAST_STEPS = 4


def _cast_into_full(ws, layers, kinds, chip):
    n = len(ws)

    def body(chip_ref, *refs):
        for w_ref, o_ref in zip(refs[:n], refs[n:]):
            o_ref[...] = w_ref[...].astype(BF16)

    in_specs, out_specs, out_shape = [], [], []
    for w, layer, kind in zip(ws, layers, kinds):
        _, R, C = w.shape
        tr = R // CAST_STEPS
        assert tr % 16 == 0
        in_specs.append(pl.BlockSpec((None, tr, C), lambda i, chip_ref, layer=layer: (layer, i, 0)))
        if kind == "col":
            out_shape.append(jax.ShapeDtypeStruct((1, R, N_CHIPS * C), BF16))
            out_specs.append(pl.BlockSpec((None, tr, C), lambda i, chip_ref: (0, i, chip_ref[0])))
        else:
            out_shape.append(jax.ShapeDtypeStruct((1, N_CHIPS * R, C), BF16))
            out_specs.append(pl.BlockSpec((None, tr, C), lambda i, chip_ref: (0, chip_ref[0] * CAST_STEPS + i, 0)))
    return pl.pallas_call(
        body, name="cast_into_full",
        grid_spec=pltpu.PrefetchScalarGridSpec(num_scalar_prefetch=1, grid=(CAST_STEPS,),
                                               in_specs=in_specs, out_specs=out_specs),
        out_shape=out_shape, compiler_params=_params("parallel"),
    )(chip, *ws)


ADD_HALF_STEPS = 8
ADD_PARTS_STEPS = 2


def _add_half(g4s, las, c):
    n = len(g4s)

    def body(c_ref, *refs):
        for g_ref, la_ref, o_ref in zip(refs[:n], refs[n:2 * n], refs[2 * n:]):
            o_ref[...] = (g_ref[...].astype(F32) + la_ref[...].astype(F32)).astype(BF16)

    g_specs, la_specs, out_shape = [], [], []
    for g4 in g4s:
        L, _, H, W = g4.shape
        th = L * H // ADD_HALF_STEPS
        per = H // th
        assert th % 16 == 0 and per * th == H
        g_specs.append(pl.BlockSpec((None, None, th, W), lambda i, c_ref, per=per: (i // per, c_ref[0], i % per, 0)))
        la_specs.append(pl.BlockSpec((None, th, W), lambda i, c_ref, per=per: (i // per, i % per, 0)))
        out_shape.append(jax.ShapeDtypeStruct((L, H, W), BF16))
    return pl.pallas_call(
        body, name="add_half",
        grid_spec=pltpu.PrefetchScalarGridSpec(num_scalar_prefetch=1, grid=(ADD_HALF_STEPS,),
                                               in_specs=g_specs + la_specs, out_specs=la_specs),
        out_shape=out_shape, compiler_params=_params("parallel"),
    )(c, *g4s, *las)


def _add_parts(ps, lbs, place, kinds):
    n = len(ps)

    def body(s_ref, *refs):
        for p_ref, lb_ref, o_ref in zip(refs[:n], refs[n:2 * n], refs[2 * n:]):
            acc = p_ref[...].astype(F32)
            for k in range(N_CHIPS - 1):
                acc = acc + lb_ref[k].astype(F32)
            o_ref[...] = acc

    p_specs, lb_specs, out_specs, out_shape = [], [], [], []
    for lb, kind in zip(lbs, kinds):
        _, L, H, C = lb.shape
        th = H // ADD_PARTS_STEPS
        assert L == 1 and th % 16 == 0
        if kind == "col":
            p_specs.append(pl.BlockSpec((None, th, C), lambda i, s_ref: (0, i, s_ref[0])))
        else:
            p_specs.append(pl.BlockSpec((None, None, th, C), lambda i, s_ref: (0, s_ref[0], i, 0)))
        lb_specs.append(pl.BlockSpec((N_CHIPS - 1, None, th, C), lambda i, s_ref: (0, 0, i, 0)))
        out_specs.append(pl.BlockSpec((None, None, th, C), lambda i, s_ref: (0, s_ref[1], i, 0)))
        out_shape.append(jax.ShapeDtypeStruct((L, 2, H, C), F32))
    return pl.pallas_call(
        body, name="add_parts",
        grid_spec=pltpu.PrefetchScalarGridSpec(num_scalar_prefetch=1, grid=(ADD_PARTS_STEPS,),
                                               in_specs=p_specs + lb_specs, out_specs=out_specs),
        out_shape=out_shape, compiler_params=_params("parallel"),
    )(place, *ps, *lbs)


ANY = pl.BlockSpec(memory_space=pl.ANY)


def _place():
    x, y, c = lax.axis_index("x"), lax.axis_index("y"), lax.axis_index("c")
    chips = [(1 - x, y), (x, 1 - y), (1 - x, 1 - y)]
    return x, y, c, chips


def _comm_call(body, name, ins, out_shape, n_remote, n_local, aliases=None):
    scratch = [pltpu.SemaphoreType.DMA((n_remote,)), pltpu.SemaphoreType.DMA((n_remote,))]
    if n_local:
        scratch.append(pltpu.SemaphoreType.DMA((n_local,)))
    return pl.pallas_call(
        body, name=name, in_specs=[ANY] * len(ins), out_specs=[ANY] * len(out_shape), out_shape=out_shape,
        scratch_shapes=scratch, input_output_aliases=aliases or {},
        compiler_params=pltpu.CompilerParams(has_side_effects=True),
    )(*ins)


def _remote(src, dst, send, recv, k, to):
    return pltpu.make_async_remote_copy(src_ref=src, dst_ref=dst, send_sem=send.at[k], recv_sem=recv.at[k],
                                        device_id=to, device_id_type=MESH)


def _gather_weights(fulls, kinds):
    n = len(fulls)
    out_shape = [jax.ShapeDtypeStruct(f.shape, f.dtype) for f in fulls]

    def body(*refs):
        outs, (send, recv) = refs[n:2 * n], refs[2 * n:]
        first = _gather_copies(outs, kinds, send, recv, 0)
        for cp in first:
            cp.start()
        for cp in first:
            cp.wait()
        passed = _pass_on_copies(outs, kinds, send, recv, len(first))
        for cp in passed:
            cp.start()
        for cp in passed:
            cp.wait()

    return _comm_call(body, "gather_weights", fulls, out_shape, 2 * n * (N_CHIPS - 1), 0, {a: a for a in range(n)})


def _window(ref, kind, s, h):
    if kind == "col":
        H, C = ref.shape[1] // 2, ref.shape[2] // N_CHIPS
        return ref.at[:, pl.ds(pl.multiple_of(h * H, 16), H), pl.ds(pl.multiple_of(s * C, LANES), C)]
    R = ref.shape[1] // N_CHIPS
    return ref.at[:, pl.ds(pl.multiple_of(s * R + h * (R // 2), 16), R // 2), :]


def _gather_copies(outs, kinds, send, recv, sem0):
    x, y, c, chips = _place()
    me = 2 * x + y
    return [_remote(_window(o, kind, me, c), _window(o, kind, me, c), send, recv, sem0 + a * (N_CHIPS - 1) + k, (*chip, c))
            for a, (o, kind) in enumerate(zip(outs, kinds)) for k, chip in enumerate(chips)]


def _pass_on_copies(outs, kinds, send, recv, sem0):
    x, y, c, chips = _place()
    cps = []
    for a, (o, kind) in enumerate(zip(outs, kinds)):
        for k, chip in enumerate(chips):
            landed = _window(o, kind, 2 * chip[0] + chip[1], c)
            cps.append(_remote(landed, landed, send, recv, sem0 + a * (N_CHIPS - 1) + k, (x, y, 1 - c)))
    return cps


def _gather_small(shards):
    n = len(shards)
    out_shape = [jax.ShapeDtypeStruct((N_CHIPS,) + s.shape, s.dtype) for s in shards]

    def body(*refs):
        srcs, outs, (send, recv, loc) = refs[:n], refs[n:2 * n], refs[2 * n:]
        x, y, c, chips = _place()
        me = 2 * x + y
        remote, local = [], []
        for a in range(n):
            local.append(pltpu.make_async_copy(srcs[a], outs[a].at[me], loc.at[a]))
            for k, chip in enumerate(chips):
                remote.append(_remote(srcs[a], outs[a].at[me], send, recv, a * (N_CHIPS - 1) + k, (*chip, c)))
        for cp in local + remote:
            cp.start()
        for cp in remote + local:
            cp.wait()

    return _comm_call(body, "gather_small", shards, out_shape, n * (N_CHIPS - 1), n)


def _scatter_shapes(ps, kinds):
    out_shape = []
    for p, kind in zip(ps, kinds):
        L, H, C = (p.shape[0], p.shape[1], p.shape[2] // N_CHIPS) if kind == "col" else (p.shape[0], p.shape[2], p.shape[3])
        out_shape.append(jax.ShapeDtypeStruct((N_CHIPS - 1, L, H, C), p.dtype))
    return out_shape


def _scatter_copies(srcs, lbs, kinds, send, recv, sem0=0):
    x, y, c, chips = _place()
    cps = []
    for a, (src, lb, kind) in enumerate(zip(srcs, lbs, kinds)):
        C = lb.shape[3]
        for k, chip in enumerate(chips):
            s = 2 * chip[0] + chip[1]
            part = src.at[:, :, pl.ds(pl.multiple_of(s * C, LANES), C)] if kind == "col" else src.at[:, s]
            cps.append(_remote(part, lb.at[k], send, recv, sem0 + a * (N_CHIPS - 1) + k, (*chip, c)))
    return cps


def _allreduce_small(part):
    N = part.shape[0]

    def body(p_ref, o_ref, buf, send, recv):
        x, y, c, _ = _place()
        me = 4 * x + 2 * y + c
        buf[me] = p_ref[...]
        cps = []
        for k in range(1, N_DEV):
            peer = (1 - x if k & 4 else x, 1 - y if k & 2 else y, 1 - c if k & 1 else c)
            cps.append(_remote(p_ref, buf.at[me], send, recv, k - 1, peer))
        for cp in cps:
            cp.start()
        for cp in cps:
            cp.wait()
        acc = buf[0]
        for i in range(1, N_DEV):
            acc = acc + buf[i]
        o_ref[...] = acc

    vmem = pl.BlockSpec(memory_space=pltpu.VMEM)
    return pl.pallas_call(
        body, name="allreduce_small", in_specs=[vmem], out_specs=vmem,
        out_shape=jax.ShapeDtypeStruct((N, LANES), F32),
        scratch_shapes=[pltpu.VMEM((N_DEV, N, LANES), F32), pltpu.SemaphoreType.DMA((N_DEV - 1,)),
                        pltpu.SemaphoreType.DMA((N_DEV - 1,))],
        compiler_params=pltpu.CompilerParams(has_side_effects=True, vmem_limit_bytes=VMEM_LIMIT_BYTES),
    )(part)


AFTER_ATTENTION = ("w_out", "w_up", "w_down")


def _layer_fwd(x, p, full=None, l=0):
    comm = full is not None

    def guest(first, second):
        first, second = ([k for k in ks if comm and k[0] < len(full)] for ks in (first, second))
        if not first + second:
            return None, []
        return _gather_guest([full[i][n] for i, n in first], [BIG_KIND[n] for _, n in first],
                             [full[i][n] for i, n in second], [BIG_KIND[n] for _, n in second]), first + second

    def done(keys, bufs):
        for (i, n), buf in zip(keys, bufs):
            full[i][n] = buf

    def weight(n):
        return full[l][n][0] if comm else p[n]

    in_sweep = ([(l, "w_up")] if l else [(l, n) for n in AFTER_ATTENTION]) if comm else []
    g, keys = guest([], [(l, "w_out")] if l else [])
    h1t, proj, out = _norm_matmul(x, p["norm1_g"], weight("w_in"), g)
    done(keys, out)
    attn, ltot, attnt, out = _attn_fwd(proj, p["q_norm_g"], p["k_norm_g"],
                                       [full[i][n] for i, n in in_sweep], [BIG_KIND[n] for _, n in in_sweep])
    done(in_sweep, out)
    g, keys = guest([], in_sweep)
    c1, out = _glu_conv_fwd(proj, p["conv_dw_w"], p["conv_dw_b"], g)
    done(keys, out)
    c, ct = _ln_silu_fwd(c1, p["conv_ln_g"], p["conv_ln_b"])
    x_mid, _ = _matmul_res([attn, c], weight("w_out"), x)
    g, keys = guest([(l + 1, "w_down")], [])
    h2t, u, out = _norm_matmul(x_mid, p["norm2_g"], weight("w_up"), g)
    done(keys, out)
    g, keys = guest([(l + 1, "w_in")], [(l + 1, "w_down")])
    act, actt, out = _ffn_act_fwd(u, p["ffn_dw_w"], p["ffn_dw_b"], g)
    done(keys, out)
    g, keys = guest([(l + 1, "w_out")], [(l + 1, "w_in")])
    x_out, out = _matmul_res([act], weight("w_down"), x_mid, g)
    done(keys, out)
    saved = dict(x=x, h1t=h1t, proj=proj, attnt=attnt, ltot=ltot, c1=c1, ct=ct, x_mid=x_mid, h2t=h2t, u=u, actt=actt)
    return x_out, saved


def _halves_view(gs, kinds):
    return [g.reshape(1, 2, g.shape[0] // 2, g.shape[1]) if kind == "col"
            else g.reshape(N_CHIPS, 2, g.shape[0] // N_CHIPS // 2, g.shape[1]) for g, kind in zip(gs, kinds)]


def _add_halves(g4s, received, kinds, core):
    parts = _add_half(g4s, received, core)
    return [p if kind == "col" else p.reshape(1, N_CHIPS, p.shape[1], p.shape[2]) for p, kind in zip(parts, kinds)]


def _gather_guest(first, first_kinds, second, second_kinds):
    bufs = list(first) + list(second)
    n1 = len(first)

    def copies(ins, outs, send, recv, sem0=0):
        return (_gather_copies(outs[:n1], first_kinds, send, recv, sem0)
                + _pass_on_copies(outs[n1:], second_kinds, send, recv, sem0 + n1 * (N_CHIPS - 1)))

    return _Guest("gather", bufs, [jax.ShapeDtypeStruct(f.shape, f.dtype) for f in bufs],
                  {a: a for a in range(len(bufs))}, copies, len(bufs) * (N_CHIPS - 1))


def _scatter_guest(parts, kinds):
    return _Guest("scatter", list(parts), _scatter_shapes(parts, kinds), {},
                  lambda ins, outs, send, recv, sem0=0: _scatter_copies(ins, outs, kinds, send, recv, sem0),
                  len(parts) * (N_CHIPS - 1))


def _exchange_guest(g4s):
    def copies(ins, outs, send, recv, sem0=0):
        x, y, c, _ = _place()
        return [_remote(g.at[:, 1 - c], la, send, recv, sem0 + a, (x, y, 1 - c)) for a, (g, la) in enumerate(zip(ins, outs))]

    out_shape = [jax.ShapeDtypeStruct((g.shape[0],) + g.shape[2:], g.dtype) for g in g4s]
    return _Guest("exchange", list(g4s), out_shape, {}, copies, len(g4s))


def _share_guest(g4s):
    def copies(ins, outs, send, recv, sem0=0):
        x, y, c, _ = _place()
        return [_remote(o.at[:, c], o.at[:, c], send, recv, sem0 + a, (x, y, 1 - c)) for a, o in enumerate(outs)]

    n = len(g4s)
    return _Guest("share", list(g4s), [jax.ShapeDtypeStruct(g.shape, g.dtype) for g in g4s], {a: a for a in range(n)}, copies, n)


def _layer_bwd(dx_out, s, p, comm=None):
    g = {}
    on = comm is not None
    core, place, pending = comm if on else (None, None, [])
    n_p = len(pending)
    finished = lambda arrays: [a.reshape(2 * a.shape[2], a.shape[3]) for a in arrays]

    if n_p:
        g4s_p = _halves_view(pending, ["col"] * n_p)
    dact, received = _matmul_nt([dx_out], p["w_down"], BF16, _exchange_guest(g4s_p) if n_p else None)
    if n_p:
        parts_p = _add_halves(g4s_p, received, ["col"] * n_p, core)
    g["w_down"] = _matmul_tn([s["actt"]], [dx_out], tk=D_FF // 2, tn=D_MODEL)
    dgate, dval, g["ffn_dw_w"], g["ffn_dw_b"], landed_p = _ffn_act_bwd(
        s["u"], p["ffn_dw_w"], p["ffn_dw_b"], dact, _scatter_guest(parts_p, ["col"] * n_p) if n_p else None)
    guest = _share_guest(_add_parts(parts_p, landed_p, place, ["col"] * n_p)) if n_p else None
    dx_mid, g["norm2_g"], shared_p = _matmul_nt_rmsbwd([dgate, dval], p["w_up"], s["x_mid"], p["norm2_g"], dx_out, guest)
    g["w_up"] = _matmul_tn([s["h2t"]], [dgate, dval], tk=D_MODEL, tn=D_FF // 2)
    dmix, _ = _matmul_nt([dx_mid], p["w_out"], F32)
    g["w_out"] = _matmul_tn([s["attnt"], s["ct"]], [dx_mid], tk=ATTN_WIDTH, tn=D_MODEL)
    dc1, g["conv_ln_g"], g["conv_ln_b"] = _ln_silu_bwd(s["c1"], p["conv_ln_g"], p["conv_ln_b"], dmix)

    late = AFTER_ATTENTION
    parts, kinds = [], []
    if on:
        kinds = [BIG_KIND[n] for n in late]
        g4s = _halves_view([g[n] for n in late], kinds)
    da, db, g["conv_dw_w"], g["conv_dw_b"], received = _glu_conv_bwd(
        s["proj"], p["conv_dw_w"], dc1, _exchange_guest(g4s) if on else None)
    if on:
        parts = _add_halves(g4s, received, kinds, core)
    (dq, dk, dv, g["q_norm_g"], g["k_norm_g"]), landed = _attn_bwd(
        s["proj"], p["q_norm_g"], p["k_norm_g"], s["ltot"], dmix, parts, kinds)
    guest = _share_guest(_add_parts(parts, landed, place, kinds)) if on else None
    pieces = [dq, dk, dv, da, db]
    dx, g["norm1_g"], shared = _matmul_nt_rmsbwd(pieces, p["w_in"], s["x"], p["norm1_g"], dx_mid, guest)
    g["w_in"] = _matmul_tn([s["h1t"]], pieces, tk=D_MODEL, tn=ATTN_WIDTH)
    sums = {}
    if on:
        sums = dict(zip(late, finished(shared)), pending=finished(shared_p))
    return dx, g, sums


WEIGHTS = ("norm1_g", "w_in", "q_norm_g", "k_norm_g", "conv_dw_w", "conv_dw_b", "conv_ln_g", "conv_ln_b",
           "w_out", "norm2_g", "w_up", "ffn_dw_w", "ffn_dw_b", "w_down")
BIG = ("w_in", "w_out", "w_up", "w_down")
BIG_KIND = {"w_in": "col", "w_out": "row", "w_up": "col", "w_down": "row"}
SMALL_SHARDED = ("conv_dw_w", "ffn_dw_w")
REPLICATED = tuple(n for n in WEIGHTS if n not in BIG + SMALL_SHARDED)


def _pack(arrays):
    flat = jnp.concatenate([a.reshape(-1) for a in arrays])
    rows = -(-flat.shape[0] // (SUBLANES * LANES)) * SUBLANES
    return jnp.pad(flat, (0, rows * LANES - flat.shape[0])).reshape(rows, LANES)


def _unpack(packed, shapes):
    flat = packed.reshape(-1)
    out, off = [], 0
    for shape in shapes:
        size = 1
        for d in shape:
            size *= d
        out.append(flat[off:off + size].reshape(shape))
        off += size
    return out


def _unshard_last(stacked):
    n, L, K, C = stacked.shape
    return jnp.transpose(stacked, (1, 2, 0, 3)).reshape(L, K, n * C)


def kernel(x, norm1_g, w_in, q_norm_g, k_norm_g, conv_dw_w, conv_dw_b, conv_ln_g, conv_ln_b, w_out, norm2_g, w_up, ffn_dw_w, ffn_dw_b, w_down, loss_target, m_norm1_g, m_w_in, m_q_norm_g, m_k_norm_g, m_conv_dw_w, m_conv_dw_b, m_conv_ln_g, m_conv_ln_b, m_w_out, m_norm2_g, m_w_up, m_ffn_dw_w, m_ffn_dw_b, m_w_down, v_norm1_g, v_w_in, v_q_norm_g, v_k_norm_g, v_conv_dw_w, v_conv_dw_b, v_conv_ln_g, v_conv_ln_b, v_w_out, v_norm2_g, v_w_up, v_ffn_dw_w, v_ffn_dw_b, v_w_down):
    given = dict(locals())
    w = {n: given[n] for n in WEIGHTS}
    m = {n: given["m_" + n] for n in WEIGHTS}
    v = {n: given["v_" + n] for n in WEIGHTS}
    chip = 2 * lax.axis_index("x") + lax.axis_index("y")
    core = lax.axis_index("c")
    chip_arr = jnp.reshape(chip, (1,)).astype(jnp.int32)
    core_arr = jnp.reshape(core, (1,)).astype(jnp.int32)
    L = DEPTH

    place = jnp.concatenate([chip_arr, core_arr])

    first = _cast_into_full([w["w_in"]], [0], ["col"], chip_arr)
    rest = [(l, n) for l in range(L) for n in BIG if (l, n) != (0, "w_in")]
    cast = dict(zip(rest, _cast_into_full([w[n] for _, n in rest], [l for l, _ in rest],
                                          [BIG_KIND[n] for _, n in rest], chip_arr)))
    cast[0, "w_in"] = first[0]
    full = [{n: cast[l, n] for n in BIG} for l in range(L)]
    full[0]["w_in"] = _gather_weights([full[0]["w_in"]], ["col"])[0]
    small_full = {n: _unshard_last(stacked)
                  for n, stacked in zip(SMALL_SHARDED, _gather_small([w[n] for n in SMALL_SHARDED]))}
    params = []
    for l in range(L):
        p = {n: small_full[n][l] for n in SMALL_SHARDED}
        p.update({n: w[n][l][None] for n in REPLICATED})
        params.append(p)

    act = x[0]
    saved = []
    for l in range(L):
        act, s = _layer_fwd(act, params[l], full, l)
        saved.append(s)
    for l in range(L):
        params[l].update({n: full[l][n][0] for n in BIG})
    dx, loss_part = _loss_grad(act, loss_target[0])
    loss = lax.psum(loss_part, ("x", "y", "c"))

    grads = [None] * L
    summed = {}
    pending = []
    for l in reversed(range(L)):
        dx, grads[l], sums = _layer_bwd(dx, saved[l], params[l], (core_arr, place, pending))
        summed.update({(l, n): sums[n] for n in AFTER_ATTENTION})
        if pending:
            summed[l + 1, "w_in"] = sums["pending"][0]
        pending = [grads[l]["w_in"]]
    grad, delta, new_m, new_v = {}, {}, {}, {}

    def adamw(n, guest=None):
        (grad[n], delta[n], new_m[n], new_v[n]), out = _adamw_stacked(
            w[n], [summed[l, n] for l in range(L)], m[n], v[n], guest)
        return out

    g4s = _halves_view(pending, ["col"])
    parts = _add_halves(g4s, adamw("w_out", _exchange_guest(g4s)), ["col"], core_arr)
    halves = _add_parts(parts, adamw("w_up", _scatter_guest(parts, ["col"])), place, ["col"])
    shared = adamw("w_down", _share_guest(halves))[0]
    summed[0, "w_in"] = shared.reshape(2 * shared.shape[2], shared.shape[3])
    adamw("w_in")

    small = REPLICATED + SMALL_SHARDED
    small_grads = [jnp.stack([grads[l][n] for l in range(L)]) for n in small]
    small_sums = _unpack(_allreduce_small(_pack(small_grads)), [a.shape for a in small_grads])
    for n, g in zip(small, small_sums):
        if n in REPLICATED:
            grad[n] = g.reshape(w[n].shape)
        else:
            width = w[n].shape[2]
            grad[n] = lax.dynamic_slice_in_dim(g, chip * width, width, axis=2)
    for n in small:
        delta[n], new_m[n], new_v[n] = _adamw_small(w[n], grad[n], m[n], v[n])

    return (loss, dx[None], *[grad[n] for n in WEIGHTS], *[delta[n] for n in WEIGHTS],
            *[new_m[n] for n in WEIGHTS], *[new_v[n] for n in WEIGHTS])
```

```python
import jax
import jax.numpy as jnp
from jax import lax
from jax.experimental import pallas as pl
from jax.experimental.pallas import tpu as pltpu

F32 = jnp.float32
BF16 = jnp.bfloat16

DEPTH = 4
D_MODEL = 1024
HEADS = 8
HEAD_DIM = 64
ATTN_WIDTH = HEADS * HEAD_DIM
CONV_WIDTH = D_MODEL - ATTN_WIDTH
CONV_KERNEL = 31
D_FF = 2816
FFN_KERNEL = 3
EPS = 1e-6
ADAM_LR, ADAM_B1, ADAM_B2, ADAM_EPS, ADAM_WD, ADAM_STEP = 0.001, 0.9, 0.999, 1e-08, 0.01, 10

N_CHIPS = 4
N_DEV = 8
LANES = 128
SUBLANES = 8
VMEM_LIMIT_BYTES = 56 * 2**20
ATTN_TILE = 256
ATTN_HEADS_PER_STEP = 4
ATTN_BLOCK = ATTN_HEADS_PER_STEP * HEAD_DIM
CONV_PAD = 32
FFN_PAD = 8
MESH = pl.DeviceIdType.MESH


def _params(*sem):
    return pltpu.CompilerParams(dimension_semantics=sem if sem else None, vmem_limit_bytes=VMEM_LIMIT_BYTES)


class _Guest:
    def __init__(self, name, ins, out_shape, aliases, copies, n_sem):
        self.name, self.ins, self.out_shape, self.aliases, self.copies, self.n_sem = name, ins, out_shape, aliases, copies, n_sem


def _hosted_call(body, guest, *, name, grid, in_specs, out_specs, out_shape, scratch_shapes, operands):
    if guest is None:
        out = pl.pallas_call(body, name=name, grid=grid, in_specs=in_specs, out_specs=out_specs, out_shape=out_shape,
                             scratch_shapes=scratch_shapes, compiler_params=_params("arbitrary"))(*operands)
        return list(out), []
    n_in, n_out, n_scr = len(in_specs), len(out_specs), len(scratch_shapes)
    gi, go = len(guest.ins), len(guest.out_shape)

    def hosting(*refs):
        ins, g_in = refs[:n_in], refs[n_in:n_in + gi]
        outs, g_out = refs[n_in + gi:n_in + gi + n_out], refs[n_in + gi + n_out:n_in + gi + n_out + go]
        scratch, (send, recv) = refs[n_in + gi + n_out + go:-2], refs[-2:]

        @pl.when(pl.program_id(0) == 0)
        def _():
            for cp in guest.copies(g_in, g_out, send, recv):
                cp.start()

        body(*ins, *outs, *scratch)

        @pl.when(pl.program_id(0) == grid[0] - 1)
        def _():
            for cp in guest.copies(g_in, g_out, send, recv):
                cp.wait()

    out = pl.pallas_call(
        hosting, name=name + "_" + guest.name, grid=grid,
        in_specs=list(in_specs) + [ANY] * gi, out_specs=list(out_specs) + [ANY] * go,
        out_shape=list(out_shape) + list(guest.out_shape),
        scratch_shapes=list(scratch_shapes) + [pltpu.SemaphoreType.DMA((guest.n_sem,))] * 2,
        input_output_aliases={n_in + a: n_out + b for a, b in guest.aliases.items()},
        compiler_params=pltpu.CompilerParams(dimension_semantics=("arbitrary",), vmem_limit_bytes=VMEM_LIMIT_BYTES,
                                             has_side_effects=True),
    )(*operands, *guest.ins)
    return list(out[:n_out]), list(out[n_out:])


def _dot(a, b):
    return jnp.dot(a, b, preferred_element_type=F32)


def _dot_nt(a, b):
    return lax.dot_general(a, b, (((1,), (1,)), ((), ())), preferred_element_type=F32)


def _dot_tn(a, b):
    return lax.dot_general(a, b, (((0,), (0,)), ((), ())), preferred_element_type=F32)


def _sigmoid(x):
    return 1.0 / (1.0 + jnp.exp(-x))


def _norm_matmul(x, g, w, guest=None, *, out_dtype=F32, tm=256):
    S, D = x.shape
    N = w.shape[1]

    def body(x_ref, g_ref, w_ref, ht_ref, y_ref):
        xv = x_ref[...]
        h = xv * lax.rsqrt(jnp.mean(xv * xv, axis=-1, keepdims=True) + EPS) * g_ref[...]
        ht_ref[...] = h.T.astype(BF16)
        y_ref[...] = _dot(h.astype(BF16), w_ref[...]).astype(out_dtype)

    (ht, y), guest_out = _hosted_call(
        body, guest, name="norm_matmul", grid=(S // tm,),
        in_specs=[pl.BlockSpec((tm, D), lambda i: (i, 0)),
                  pl.BlockSpec((1, D), lambda i: (0, 0)),
                  pl.BlockSpec((D, N), lambda i: (0, 0))],
        out_specs=[pl.BlockSpec((D, tm), lambda i: (0, i)),
                   pl.BlockSpec((tm, N), lambda i: (i, 0))],
        out_shape=[jax.ShapeDtypeStruct((D, S), BF16), jax.ShapeDtypeStruct((S, N), out_dtype)],
        scratch_shapes=[], operands=[x, g, w])
    return ht, y, guest_out


def _matmul_res(pieces, w, res, guest=None, *, tm=512):
    S, N = res.shape
    K = w.shape[0]
    widths = [p.shape[1] for p in pieces]
    assert sum(widths) == K

    def body(*refs):
        p_refs, (w_ref, res_ref, o_ref) = refs[:len(pieces)], refs[len(pieces):]
        acc = res_ref[...]
        off = 0
        for p_ref, kp in zip(p_refs, widths):
            acc = acc + _dot(p_ref[...], w_ref[off:off + kp, :])
            off += kp
        o_ref[...] = acc

    (out,), guest_out = _hosted_call(
        body, guest, name="matmul_res", grid=(S // tm,),
        in_specs=[pl.BlockSpec((tm, kp), lambda i: (i, 0)) for kp in widths]
        + [pl.BlockSpec((K, N), lambda i: (0, 0)), pl.BlockSpec((tm, N), lambda i: (i, 0))],
        out_specs=[pl.BlockSpec((tm, N), lambda i: (i, 0))],
        out_shape=[jax.ShapeDtypeStruct((S, N), F32)],
        scratch_shapes=[], operands=[*pieces, w, res])
    return out, guest_out


def _nt_sum(p_refs, widths, w_ref):
    acc = None
    off = 0
    for p_ref, n_p in zip(p_refs, widths):
        d = _dot_nt(p_ref[...].astype(BF16), w_ref[:, off:off + n_p])
        acc = d if acc is None else acc + d
        off += n_p
    return acc


def _matmul_nt(pieces, w, out_dtype, guest=None, *, tm=512):
    S = pieces[0].shape[0]
    K, N = w.shape
    widths = [p.shape[1] for p in pieces]
    assert sum(widths) == N

    def body(*refs):
        p_refs, (w_ref, o_ref) = refs[:len(pieces)], refs[len(pieces):]
        o_ref[...] = _nt_sum(p_refs, widths, w_ref).astype(out_dtype)

    (out,), guest_out = _hosted_call(
        body, guest, name="matmul_nt", grid=(S // tm,),
        in_specs=[pl.BlockSpec((tm, n_p), lambda i: (i, 0)) for n_p in widths]
        + [pl.BlockSpec((K, N), lambda i: (0, 0))],
        out_specs=[pl.BlockSpec((tm, K), lambda i: (i, 0))],
        out_shape=[jax.ShapeDtypeStruct((S, K), out_dtype)],
        scratch_shapes=[], operands=[*pieces, w])
    return out, guest_out


def _matmul_nt_rmsbwd(pieces, w, x, g, dres, guest=None, *, tm=256):
    S, K = x.shape
    N = w.shape[1]
    widths = [p.shape[1] for p in pieces]
    assert sum(widths) == N

    def body(*refs):
        p_refs, (w_ref, x_ref, g_ref, dres_ref, dx_ref, dg_ref) = refs[:len(pieces)], refs[len(pieces):]
        dh = _nt_sum(p_refs, widths, w_ref)
        xv = x_ref[...]
        r = lax.rsqrt(jnp.mean(xv * xv, axis=-1, keepdims=True) + EPS)
        xh = xv * r
        dxh = dh * g_ref[...]
        dx_ref[...] = dres_ref[...] + r * (dxh - xh * jnp.mean(dxh * xh, axis=-1, keepdims=True))

        @pl.when(pl.program_id(0) == 0)
        def _():
            dg_ref[...] = jnp.zeros_like(dg_ref)

        dg_ref[...] += jnp.sum(dh * xh, axis=0, keepdims=True)

    (dx, dg), guest_out = _hosted_call(
        body, guest, name="matmul_nt_rmsbwd", grid=(S // tm,),
        in_specs=[pl.BlockSpec((tm, n_p), lambda i: (i, 0)) for n_p in widths]
        + [pl.BlockSpec((K, N), lambda i: (0, 0)), pl.BlockSpec((tm, K), lambda i: (i, 0)),
           pl.BlockSpec((1, K), lambda i: (0, 0)), pl.BlockSpec((tm, K), lambda i: (i, 0))],
        out_specs=[pl.BlockSpec((tm, K), lambda i: (i, 0)), pl.BlockSpec((1, K), lambda i: (0, 0))],
        out_shape=[jax.ShapeDtypeStruct((S, K), F32), jax.ShapeDtypeStruct((1, K), F32)],
        scratch_shapes=[], operands=[*pieces, w, x, g, dres])
    return dx, dg, guest_out


def _matmul_tn(xts, dys, *, tk, tn, ts=1024, transposed=True):
    S = dys[0].shape[0]
    xs = xts
    n_s = S // ts
    (mt,) = {x.shape[0 if transposed else 1] // tk for x in xts}
    (nt,) = {d.shape[1] // tn for d in dys}
    product = _dot if transposed else _dot_tn

    def body(*refs):
        x_refs, dy_refs, (o_ref, acc_ref) = refs[:len(xs)], refs[len(xs):len(xs) + len(dys)], refs[len(xs) + len(dys):]
        i, j, s = pl.program_id(0), pl.program_id(1), pl.program_id(2)

        @pl.when(s == 0)
        def _():
            acc_ref[...] = jnp.zeros_like(acc_ref)

        for a, x_ref in enumerate(x_refs):
            for b, dy_ref in enumerate(dy_refs):
                @pl.when((i // mt == a) & (j // nt == b))
                def _():
                    acc_ref[...] += product(x_ref[...], dy_ref[...].astype(BF16))

        @pl.when(s == n_s - 1)
        def _():
            o_ref[...] = acc_ref[...].astype(BF16)

    def x_map(a):
        if transposed:
            return lambda i, j, s: (jnp.where(i // mt == a, i % mt, 0), jnp.where(i // mt == a, s, 0))
        return lambda i, j, s: (jnp.where(i // mt == a, s, 0), jnp.where(i // mt == a, i % mt, 0))

    def dy_map(b):
        return lambda i, j, s: (jnp.where(j // nt == b, s, 0), jnp.where(j // nt == b, j % nt, 0))

    return pl.pallas_call(
        body, name="matmul_tn", grid=(len(xs) * mt, len(dys) * nt, n_s),
        in_specs=[pl.BlockSpec((tk, ts) if transposed else (ts, tk), x_map(a)) for a in range(len(xs))]
        + [pl.BlockSpec((ts, tn), dy_map(b)) for b in range(len(dys))],
        out_specs=pl.BlockSpec((tk, tn), lambda i, j, s: (i, j)),
        out_shape=jax.ShapeDtypeStruct((len(xs) * mt * tk, len(dys) * nt * tn), BF16),
        scratch_shapes=[pltpu.VMEM((tk, tn), F32)],
        compiler_params=_params("parallel", "parallel", "arbitrary"),
    )(*xs, *dys)


def _tri_consts():
    j = jnp.arange(ATTN_TILE)[:, None]
    s = jnp.arange(ATTN_TILE)[None, :]
    return (j > s).astype(BF16), (j <= s).astype(BF16), (j < s).astype(BF16)


SIGN_BIT = 0x80000000
WEIGHT_IS_ZERO = -104.0


def _log_terms(sn):
    minus_abs = lax.bitcast_convert_type(lax.bitcast_convert_type(sn, jnp.uint32) | jnp.uint32(SIGN_BIT), F32)
    lom = jnp.minimum(sn, 0.0) - jnp.log(1.0 + jnp.exp(minus_abs))
    return lom, lom - sn


def _causal_mask():
    t = lax.broadcasted_iota(jnp.int32, (ATTN_TILE, ATTN_TILE), 0)
    s = lax.broadcasted_iota(jnp.int32, (ATTN_TILE, ATTN_TILE), 1)
    return s < t


def _attn_prep(h, q_ref, k_ref, v_ref, qg_ref, kg_ref, qn_s, kn_s, vb_s, n_tiles):
    T = ATTN_TILE
    lanes = slice(HEAD_DIM * h, HEAD_DIM * (h + 1))
    scale = -(HEAD_DIM ** -0.5)

    def prep(i, carry):
        rows = pl.ds(pl.multiple_of(i * T, T), T)
        q = q_ref[rows, lanes]
        k = k_ref[rows, lanes]
        rq = lax.rsqrt(jnp.mean(q * q, axis=-1, keepdims=True) + EPS)
        rk = lax.rsqrt(jnp.mean(k * k, axis=-1, keepdims=True) + EPS)
        qn_s[h, rows, :] = (q * rq * qg_ref[...] * scale).astype(BF16)
        kn_s[h, rows, :] = (k * rk * kg_ref[...]).astype(BF16)
        vb_s[h, rows, :] = v_ref[rows, lanes].astype(BF16)
        return carry

    lax.fori_loop(0, n_tiles, prep, 0)


def _attn_fwd(proj, qg, kg, fulls=(), kinds=()):
    S = proj.shape[0]
    n_comm = len(fulls)
    T = ATTN_TILE
    n_tiles = S // T
    suffix, _, _ = _tri_consts()
    pairs = HEADS // ATTN_HEADS_PER_STEP
    q_blk, k_blk, v_blk = 0, ATTN_WIDTH // ATTN_BLOCK, 2 * ATTN_WIDTH // ATTN_BLOCK

    def body(*refs):
        q_ref, k_ref, v_ref, qg_ref, kg_ref, tri_ref = refs[:6]
        o_ref, lt_ref, ot_ref, start_ref, qn_s, kn_s, vb_s = refs[6 + n_comm:13 + n_comm]
        w_refs = refs[13 + n_comm:13 + 2 * n_comm]
        sems = refs[13 + 2 * n_comm:]
        step_id = pl.program_id(0)
        if n_comm:
            @pl.when(pl.program_id(0) == 0)
            def _():
                for cp in _gather_copies(w_refs, kinds, *sems, 0):
                    cp.start()

        heads = range(ATTN_HEADS_PER_STEP)
        for h in heads:
            _attn_prep(h, q_ref, k_ref, v_ref, qg_ref, kg_ref, qn_s, kn_s, vb_s, n_tiles)

        def q_tile(qi, carry0):
            qrows = pl.ds(pl.multiple_of(qi * T, T), T)
            qt = [qn_s[h, qrows, :] for h in heads]

            def tile(kj, carry, diag):
                krows = pl.ds(pl.multiple_of(kj * T, T), T)
                s = [_dot_nt(qt[h], kn_s[h, krows, :]) for h in heads]
                terms = [_log_terms(s[h]) for h in heads]
                lom = [terms[h][0] for h in heads]
                if diag:
                    mask = _causal_mask()
                    lom = [jnp.where(mask, lom[h], 0.0) for h in heads]
                lom = [lom[h].astype(BF16) for h in heads]
                tail = [_dot(lom[h], tri_ref[...]) for h in heads]
                w = [jnp.exp(terms[h][1] + tail[h] + carry[h][1]) for h in heads]
                if diag:
                    w = [jnp.where(mask, w[h], 0.0) for h in heads]
                return tuple((carry[h][0] + _dot(w[h].astype(BF16), vb_s[h, krows, :]),
                              carry[h][1] + (tail[h][:, 0:1] + lom[h][:, 0:1].astype(F32))) for h in heads)

            init = tuple((jnp.zeros((T, HEAD_DIM), F32), jnp.zeros((T, 1), F32)) for h in heads)
            def alive(cr):
                worst = cr[0][1]
                for h in heads[1:]:
                    worst = jnp.maximum(worst, cr[h][1])
                return jnp.max(worst) >= WEIGHT_IS_ZERO

            def step(state):
                t, _, cr = state
                cr = tile(qi - 1 - t, cr, False)
                return t + 1, alive(cr), cr

            first = tile(qi, init, True)
            swept, _, carry = lax.while_loop(lambda st: (st[0] < qi) & st[1], step, (jnp.int32(0), alive(first), first))
            start_ref[step_id, qi] = (qi - swept).astype(F32)
            o = jnp.concatenate([carry[h][0] for h in heads], axis=1)
            o_ref[qrows, :] = o.astype(BF16)
            ot_ref[:, qrows] = o.T.astype(BF16)
            for h in heads:
                lt_ref[qrows, HEAD_DIM * h:HEAD_DIM * (h + 1)] = jnp.broadcast_to(carry[h][1], (T, HEAD_DIM))
            return carry0

        lax.fori_loop(0, n_tiles, q_tile, 0)
        if n_comm:
            @pl.when(pl.program_id(0) == pairs - 1)
            def _():
                for cp in _gather_copies(w_refs, kinds, *sems, 0):
                    cp.wait()

    n_sem = n_comm * (N_CHIPS - 1)
    per_head = pl.BlockSpec((ATTN_HEADS_PER_STEP, S, HEAD_DIM), lambda p: (p, 0, 0))
    out = pl.pallas_call(
        body, name="attn_fwd_gather" if n_comm else "attn_fwd", grid=(pairs,),
        in_specs=[pl.BlockSpec((S, ATTN_BLOCK), lambda p: (0, q_blk + p)),
                  pl.BlockSpec((S, ATTN_BLOCK), lambda p: (0, k_blk + p)),
                  pl.BlockSpec((S, ATTN_BLOCK), lambda p: (0, v_blk + p)),
                  pl.BlockSpec((1, HEAD_DIM), lambda p: (0, 0)),
                  pl.BlockSpec((1, HEAD_DIM), lambda p: (0, 0)),
                  pl.BlockSpec((T, T), lambda p: (0, 0))] + [ANY] * n_comm,
        out_specs=[pl.BlockSpec((S, ATTN_BLOCK), lambda p: (0, p)),
                   pl.BlockSpec((None, S, ATTN_BLOCK), lambda p: (p, 0, 0)),
                   pl.BlockSpec((ATTN_BLOCK, S), lambda p: (p, 0)),
                   pl.BlockSpec(memory_space=pltpu.SMEM)] + [per_head] * 3 + [ANY] * n_comm,
        out_shape=[jax.ShapeDtypeStruct((S, ATTN_WIDTH), BF16),
                   jax.ShapeDtypeStruct((pairs, S, ATTN_BLOCK), F32),
                   jax.ShapeDtypeStruct((ATTN_WIDTH, S), BF16),
                   jax.ShapeDtypeStruct((pairs, n_tiles), F32)] + [jax.ShapeDtypeStruct((HEADS, S, HEAD_DIM), BF16)] * 3
        + [jax.ShapeDtypeStruct(f.shape, f.dtype) for f in fulls],
        scratch_shapes=[pltpu.SemaphoreType.DMA((n_sem,))] * 2 if n_comm else [],
        input_output_aliases={6 + a: 7 + a for a in range(n_comm)},
        compiler_params=pltpu.CompilerParams(dimension_semantics=("arbitrary",), vmem_limit_bytes=VMEM_LIMIT_BYTES,
                                             has_side_effects=bool(n_comm)),
    )(proj, proj, proj, qg, kg, suffix, *fulls)
    return out[0], (out[1], out[3], out[4], out[5], out[6]), out[2], list(out[7:])


def _attn_bwd(proj, qg, kg, ltot, dmix, parts=(), kinds=()):
    S = proj.shape[0]
    n_comm = len(parts)
    T = ATTN_TILE
    n_tiles = S // T
    _, prefix_incl, prefix_excl = _tri_consts()
    pairs = HEADS // ATTN_HEADS_PER_STEP
    q_blk, k_blk, v_blk = 0, ATTN_WIDTH // ATTN_BLOCK, 2 * ATTN_WIDTH // ATTN_BLOCK
    scale = HEAD_DIM ** -0.5

    def body(*refs):
        q_ref, k_ref, qg_ref, kg_ref, lt_ref, do_ref, ti_ref, te_ref, start_ref, qn_s, kn_s, vb_s = refs[:12]
        p_refs = refs[12:12 + n_comm]
        dq_ref, dk_ref, dv_ref, dqg_ref, dkg_ref = refs[12 + n_comm:17 + n_comm]
        lb_refs = refs[17 + n_comm:17 + 2 * n_comm]
        dq_s, dk_s, dv_s = refs[17 + 2 * n_comm:20 + 2 * n_comm]
        sems = refs[20 + 2 * n_comm:]
        step_id = pl.program_id(0)

        @pl.when(pl.program_id(0) == 0)
        def _():
            dqg_ref[...] = jnp.zeros_like(dqg_ref)
            dkg_ref[...] = jnp.zeros_like(dkg_ref)
            for cp in _scatter_copies(p_refs, lb_refs, kinds, *sems) if n_comm else ():
                cp.start()

        heads = range(ATTN_HEADS_PER_STEP)
        dk_s[...] = jnp.zeros_like(dk_s)
        dv_s[...] = jnp.zeros_like(dv_s)

        def q_tile(qi, carry0):
            qrows = pl.ds(pl.multiple_of(qi * T, T), T)
            qt = [qn_s[h, qrows, :] for h in heads]
            dob = [do_ref[qrows, HEAD_DIM * h:HEAD_DIM * (h + 1)].astype(BF16) for h in heads]
            lt = [lt_ref[qrows, HEAD_DIM * h:HEAD_DIM * h + 1] for h in heads]

            def tile(kj, carry, diag):
                krows = pl.ds(pl.multiple_of(kj * T, T), T)
                kt = [kn_s[h, krows, :] for h in heads]
                s = [_dot_nt(qt[h], kt[h]) for h in heads]
                dw = [_dot_nt(dob[h], vb_s[h, krows, :]) for h in heads]
                terms = [_log_terms(s[h]) for h in heads]
                lom = [terms[h][0] for h in heads]
                if diag:
                    mask = _causal_mask()
                    lom = [jnp.where(mask, lom[h], 0.0) for h in heads]
                pin = [_dot(lom[h].astype(BF16), ti_ref[...]) for h in heads]
                w = [jnp.exp(terms[h][1] + ((lt[h] - carry[h][1]) - pin[h])) for h in heads]
                if diag:
                    w = [jnp.where(mask, w[h], 0.0) for h in heads]
                e = [w[h] * dw[h] for h in heads]
                gex = [_dot(e[h].astype(BF16), te_ref[...]) for h in heads]
                dz = [e[h] - jnp.exp(terms[h][1]) * (e[h] + (carry[h][2] + gex[h])) for h in heads]
                if diag:
                    dz = [jnp.where(mask, dz[h], 0.0) for h in heads]
                new = []
                for h in heads:
                    dzb = dz[h].astype(BF16)
                    dk_s[h, krows, :] += _dot_tn(dzb, qt[h])
                    dv_s[h, krows, :] += _dot_tn(w[h].astype(BF16), dob[h])
                    new.append((carry[h][0] + _dot(dzb, kt[h]),
                                carry[h][1] + pin[h][:, T - 1:T],
                                carry[h][2] + gex[h][:, T - 1:T] + e[h][:, T - 1:T]))
                return tuple(new)

            zero = jnp.zeros((T, 1), F32)
            init = tuple((jnp.zeros((T, HEAD_DIM), F32), zero, zero) for h in heads)
            first = jnp.clip(start_ref[step_id, qi].astype(jnp.int32), 0, qi)
            last = tile(qi, lax.fori_loop(first, qi, lambda kj, cr: tile(kj, cr, False), init), True)
            for h in heads:
                dq_s[h, qrows, :] = last[h][0]
            return carry0

        lax.fori_loop(0, n_tiles, q_tile, 0)

        def finish(i, carry):
            rows = pl.ds(pl.multiple_of(i * T, T), T)
            new = []
            for h in heads:
                lanes = slice(HEAD_DIM * h, HEAD_DIM * (h + 1))
                q = q_ref[rows, lanes]
                k = k_ref[rows, lanes]
                rq = lax.rsqrt(jnp.mean(q * q, axis=-1, keepdims=True) + EPS)
                rk = lax.rsqrt(jnp.mean(k * k, axis=-1, keepdims=True) + EPS)
                qh = q * rq
                kh = k * rk
                dqn = dq_s[h, rows, :] * scale
                dkn = -dk_s[h, rows, :]
                dqh = dqn * qg_ref[...]
                dkh = dkn * kg_ref[...]
                dq_ref[rows, lanes] = (rq * (dqh - qh * jnp.mean(dqh * qh, axis=-1, keepdims=True))).astype(BF16)
                dk_ref[rows, lanes] = (rk * (dkh - kh * jnp.mean(dkh * kh, axis=-1, keepdims=True))).astype(BF16)
                dv_ref[rows, lanes] = dv_s[h, rows, :].astype(BF16)
                new.append(carry[2 * h] + jnp.sum(dqn * qh, axis=0, keepdims=True))
                new.append(carry[2 * h + 1] + jnp.sum(dkn * kh, axis=0, keepdims=True))
            return tuple(new)

        zero = jnp.zeros((1, HEAD_DIM), F32)
        sums = lax.fori_loop(0, n_tiles, finish, (zero,) * (2 * len(heads)))
        dqg_ref[...] += sum(sums[0::2])
        dkg_ref[...] += sum(sums[1::2])
        if n_comm:
            @pl.when(pl.program_id(0) == pairs - 1)
            def _():
                for cp in _scatter_copies(p_refs, lb_refs, kinds, *sems):
                    cp.wait()

    blk = lambda off: pl.BlockSpec((S, ATTN_BLOCK), lambda p: (0, off + p))
    row64 = pl.BlockSpec((1, HEAD_DIM), lambda p: (0, 0))
    tri = pl.BlockSpec((T, T), lambda p: (0, 0))
    n_sem = n_comm * (N_CHIPS - 1)
    out = pl.pallas_call(
        body, name="attn_bwd_scatter" if n_comm else "attn_bwd", grid=(pairs,),
        in_specs=[blk(q_blk), blk(k_blk), row64, row64,
                  pl.BlockSpec((None, S, ATTN_BLOCK), lambda p: (p, 0, 0)), blk(0), tri, tri,
                  pl.BlockSpec(memory_space=pltpu.SMEM)]
        + [pl.BlockSpec((ATTN_HEADS_PER_STEP, S, HEAD_DIM), lambda p: (p, 0, 0))] * 3 + [ANY] * n_comm,
        out_specs=[blk(0), blk(0), blk(0), row64, row64] + [ANY] * n_comm,
        out_shape=[jax.ShapeDtypeStruct((S, ATTN_WIDTH), BF16)] * 3 + [jax.ShapeDtypeStruct((1, HEAD_DIM), F32)] * 2
        + _scatter_shapes(parts, kinds),
        scratch_shapes=[pltpu.VMEM((ATTN_HEADS_PER_STEP, S, HEAD_DIM), F32)] * 3
        + ([pltpu.SemaphoreType.DMA((n_sem,))] * 2 if n_comm else []),
        compiler_params=pltpu.CompilerParams(dimension_semantics=("arbitrary",), vmem_limit_bytes=VMEM_LIMIT_BYTES,
                                             has_side_effects=bool(n_comm)),
    )(proj, proj, qg, kg, ltot[0], dmix, prefix_incl, prefix_excl, *ltot[1:], *parts)
    return out[:5], list(out[5:])


def _shifted(win, n_rows):
    return [win if b == 0 else pltpu.roll(win, n_rows - b, 0) for b in range(SUBLANES)]


def _taps(variants, offsets, tm):
    return {o: variants[o % SUBLANES][(o // SUBLANES) * SUBLANES:(o // SUBLANES) * SUBLANES + tm, :] for o in offsets}


def _fold_rows(a):
    return jnp.sum(a.reshape(a.shape[0] // SUBLANES, SUBLANES, a.shape[1]), axis=0)


def _glu_conv_fwd(proj, w, bias, guest=None, *, tm=256):
    S = proj.shape[0]
    CB = LANES
    a_blk, b_blk = 3 * ATTN_WIDTH // CB, (3 * ATTN_WIDTH + CONV_WIDTH) // CB
    n_rows = tm + CONV_PAD

    def body(a_ref, b_ref, w_ref, bias_ref, c1_ref, pad_s):
        pad_s[0:CONV_PAD, :] = jnp.zeros((CONV_PAD, CB), F32)

        def fill(i, carry):
            rows = pl.ds(pl.multiple_of(i * tm, tm), tm)
            pad_s[pl.ds(pl.multiple_of(CONV_PAD + i * tm, SUBLANES), tm), :] = a_ref[rows, :] * _sigmoid(b_ref[rows, :])
            return carry

        lax.fori_loop(0, S // tm, fill, 0)

        def conv(i, carry):
            r0 = pl.multiple_of(i * tm, tm)
            taps = _taps(_shifted(pad_s[pl.ds(r0, n_rows), :], n_rows), range(2, 2 + CONV_KERNEL), tm)
            acc = jnp.broadcast_to(bias_ref[...], (tm, CB))
            for k in range(CONV_KERNEL):
                acc = acc + w_ref[k:k + 1, :] * taps[k + 2]
            c1_ref[pl.ds(r0, tm), :] = acc
            return carry

        lax.fori_loop(0, S // tm, conv, 0)

    (c1,), guest_out = _hosted_call(
        body, guest, name="glu_conv_fwd", grid=(CONV_WIDTH // CB,),
        in_specs=[pl.BlockSpec((S, CB), lambda j: (0, a_blk + j)), pl.BlockSpec((S, CB), lambda j: (0, b_blk + j)),
                  pl.BlockSpec((CONV_KERNEL, CB), lambda j: (0, j)), pl.BlockSpec((1, CB), lambda j: (0, j))],
        out_specs=[pl.BlockSpec((S, CB), lambda j: (0, j))],
        out_shape=[jax.ShapeDtypeStruct((S, CONV_WIDTH), F32)],
        scratch_shapes=[pltpu.VMEM((S + CONV_PAD, CB), F32)], operands=[proj, proj, w, bias])
    return c1, guest_out


def _glu_conv_bwd(proj, w, dc1, guest=None, *, tm=256):
    S = proj.shape[0]
    CB = LANES
    a_blk, b_blk = 3 * ATTN_WIDTH // CB, (3 * ATTN_WIDTH + CONV_WIDTH) // CB
    n_rows = tm + CONV_PAD

    def body(a_ref, b_ref, w_ref, dc1_ref, da_ref, db_ref, dw_ref, dbias_ref, pad_s, dpad_s, dw_s):
        pad_s[0:CONV_PAD, :] = jnp.zeros((CONV_PAD, CB), F32)
        dpad_s[S:S + CONV_PAD, :] = jnp.zeros((CONV_PAD, CB), F32)
        dw_s[...] = jnp.zeros_like(dw_s)

        def fill(i, carry):
            rows = pl.ds(pl.multiple_of(i * tm, tm), tm)
            pad_s[pl.ds(pl.multiple_of(CONV_PAD + i * tm, SUBLANES), tm), :] = a_ref[rows, :] * _sigmoid(b_ref[rows, :])
            dpad_s[rows, :] = dc1_ref[rows, :]
            return carry

        lax.fori_loop(0, S // tm, fill, 0)

        def conv(i, carry):
            r0 = pl.multiple_of(i * tm, tm)
            rows = pl.ds(r0, tm)
            taps = _taps(_shifted(dpad_s[pl.ds(r0, n_rows), :], n_rows), range(CONV_KERNEL), tm)
            acc = jnp.zeros((tm, CB), F32)
            for k in range(CONV_KERNEL):
                acc = acc + w_ref[k:k + 1, :] * taps[CONV_KERNEL - 1 - k]
            a = a_ref[rows, :]
            sg = _sigmoid(b_ref[rows, :])
            da_ref[rows, :] = (acc * sg).astype(BF16)
            db_ref[rows, :] = (acc * a * sg * (1.0 - sg)).astype(BF16)
            d = taps[0]
            taps = _taps(_shifted(pad_s[pl.ds(r0, n_rows), :], n_rows), range(2, 2 + CONV_KERNEL), tm)
            for k in range(CONV_KERNEL):
                dw_s[SUBLANES * k:SUBLANES * (k + 1), :] += _fold_rows(d * taps[k + 2])
            dw_s[SUBLANES * CONV_KERNEL:SUBLANES * (CONV_KERNEL + 1), :] += _fold_rows(d)
            return carry

        lax.fori_loop(0, S // tm, conv, 0)
        for k in range(CONV_KERNEL):
            dw_ref[k:k + 1, :] = jnp.sum(dw_s[SUBLANES * k:SUBLANES * (k + 1), :], axis=0, keepdims=True)
        dbias_ref[...] = jnp.sum(dw_s[SUBLANES * CONV_KERNEL:SUBLANES * (CONV_KERNEL + 1), :], axis=0, keepdims=True)

    col = lambda off: pl.BlockSpec((S, CB), lambda j: (0, off + j))
    out, guest_out = _hosted_call(
        body, guest, name="glu_conv_bwd", grid=(CONV_WIDTH // CB,),
        in_specs=[col(a_blk), col(b_blk), pl.BlockSpec((CONV_KERNEL, CB), lambda j: (0, j)), col(0)],
        out_specs=[col(0), col(0), pl.BlockSpec((CONV_KERNEL, CB), lambda j: (0, j)), pl.BlockSpec((1, CB), lambda j: (0, j))],
        out_shape=[jax.ShapeDtypeStruct((S, CONV_WIDTH), BF16)] * 2
        + [jax.ShapeDtypeStruct((CONV_KERNEL, CONV_WIDTH), F32), jax.ShapeDtypeStruct((1, CONV_WIDTH), F32)],
        scratch_shapes=[pltpu.VMEM((S + CONV_PAD, CB), F32), pltpu.VMEM((S + CONV_PAD, CB), F32),
                        pltpu.VMEM((SUBLANES * (CONV_KERNEL + 1), CB), F32)], operands=[proj, proj, w, dc1])
    return (*out, guest_out)


def _ln_stats(c1):
    mu = jnp.mean(c1, axis=-1, keepdims=True)
    xc = c1 - mu
    r = lax.rsqrt(jnp.mean(xc * xc, axis=-1, keepdims=True) + EPS)
    return xc * r, r


def _ln_silu_fwd(c1, g, b, *, tm=512):
    S, C = c1.shape

    def body(c1_ref, g_ref, b_ref, c_ref, ct_ref):
        yh, _ = _ln_stats(c1_ref[...])
        y = yh * g_ref[...] + b_ref[...]
        c = y * _sigmoid(y)
        c_ref[...] = c.astype(BF16)
        ct_ref[...] = c.T.astype(BF16)

    vec = pl.BlockSpec((1, C), lambda i: (0, 0))
    return pl.pallas_call(
        body, name="ln_silu_fwd", grid=(S // tm,),
        in_specs=[pl.BlockSpec((tm, C), lambda i: (i, 0)), vec, vec],
        out_specs=[pl.BlockSpec((tm, C), lambda i: (i, 0)), pl.BlockSpec((C, tm), lambda i: (0, i))],
        out_shape=[jax.ShapeDtypeStruct((S, C), BF16), jax.ShapeDtypeStruct((C, S), BF16)],
        compiler_params=_params("parallel"),
    )(c1, g, b)


def _ln_silu_bwd(c1, g, b, dmix, *, tm=512):
    S, C = c1.shape

    def body(c1_ref, g_ref, b_ref, dc_ref, dc1_ref, dg_ref, db_ref):
        yh, r = _ln_stats(c1_ref[...])
        y = yh * g_ref[...] + b_ref[...]
        sg = _sigmoid(y)
        dy = dc_ref[...] * (sg * (1.0 + y * (1.0 - sg)))
        dyh = dy * g_ref[...]
        dc1_ref[...] = r * (dyh - jnp.mean(dyh, axis=-1, keepdims=True)
                            - yh * jnp.mean(dyh * yh, axis=-1, keepdims=True))

        @pl.when(pl.program_id(0) == 0)
        def _():
            dg_ref[...] = jnp.zeros_like(dg_ref)
            db_ref[...] = jnp.zeros_like(db_ref)

        dg_ref[...] += jnp.sum(dy * yh, axis=0, keepdims=True)
        db_ref[...] += jnp.sum(dy, axis=0, keepdims=True)

    vec = pl.BlockSpec((1, C), lambda i: (0, 0))
    return pl.pallas_call(
        body, name="ln_silu_bwd", grid=(S // tm,),
        in_specs=[pl.BlockSpec((tm, C), lambda i: (i, 0)), vec, vec, pl.BlockSpec((tm, C), lambda i: (i, 1))],
        out_specs=[pl.BlockSpec((tm, C), lambda i: (i, 0)), vec, vec],
        out_shape=[jax.ShapeDtypeStruct((S, C), F32), jax.ShapeDtypeStruct((1, C), F32), jax.ShapeDtypeStruct((1, C), F32)],
        compiler_params=_params("arbitrary"),
    )(c1, g, b, dmix)


FFN_CB = 256


def _ffn_gate(pad_s, w_ref, bias_ref, r0, tm):
    n_rows = tm + FFN_PAD
    taps = _taps(_shifted(pad_s[pl.ds(r0, n_rows), :], n_rows), range(FFN_PAD - 2, FFN_PAD + 1), tm)
    g1 = bias_ref[...] + w_ref[0:1, :] * taps[6] + w_ref[1:2, :] * taps[7] + w_ref[2:3, :] * taps[8]
    return g1, taps


def _ffn_act_fwd(u, w, bias, guest=None, *, tm=256):
    S = u.shape[0]
    CB = FFN_CB
    nb = D_FF // CB

    def body(g_ref, v_ref, w_ref, bias_ref, o_ref, pad_s):
        pad_s[0:FFN_PAD, :] = jnp.zeros((FFN_PAD, CB), F32)

        def fill(i, carry):
            pad_s[pl.ds(pl.multiple_of(FFN_PAD + i * tm, SUBLANES), tm), :] = g_ref[pl.ds(pl.multiple_of(i * tm, tm), tm), :].astype(F32)
            return carry

        lax.fori_loop(0, S // tm, fill, 0)

        def act(i, carry):
            r0 = pl.multiple_of(i * tm, tm)
            g1, _ = _ffn_gate(pad_s, w_ref, bias_ref, r0, tm)
            o_ref[pl.ds(r0, tm), :] = (g1 * _sigmoid(g1) * v_ref[pl.ds(r0, tm), :].astype(F32)).astype(BF16)
            return carry

        lax.fori_loop(0, S // tm, act, 0)

    (act,), guest_out = _hosted_call(
        body, guest, name="ffn_act_fwd", grid=(nb,),
        in_specs=[pl.BlockSpec((S, CB), lambda j: (0, j)), pl.BlockSpec((S, CB), lambda j: (0, nb + j)),
                  pl.BlockSpec((FFN_KERNEL, CB), lambda j: (0, j)), pl.BlockSpec((1, CB), lambda j: (0, j))],
        out_specs=[pl.BlockSpec((S, CB), lambda j: (0, j))],
        out_shape=[jax.ShapeDtypeStruct((S, D_FF), BF16)],
        scratch_shapes=[pltpu.VMEM((S + FFN_PAD, CB), F32)], operands=[u, u, w, bias])
    return act, guest_out


def _ffn_act_bwd(u, w, bias, dact, guest=None, *, tm=256):
    S = u.shape[0]
    CB = FFN_CB
    nb = D_FF // CB

    def body(g_ref, v_ref, w_ref, bias_ref, da_ref, dg_ref, dv_ref, dw_ref, dbias_ref, pad_s, dpad_s, dw_s):
        pad_s[0:FFN_PAD, :] = jnp.zeros((FFN_PAD, CB), F32)
        dpad_s[S:S + FFN_PAD, :] = jnp.zeros((FFN_PAD, CB), F32)
        dw_s[...] = jnp.zeros_like(dw_s)

        def fill(i, carry):
            pad_s[pl.ds(pl.multiple_of(FFN_PAD + i * tm, SUBLANES), tm), :] = g_ref[pl.ds(pl.multiple_of(i * tm, tm), tm), :].astype(F32)
            return carry

        lax.fori_loop(0, S // tm, fill, 0)

        def first(i, carry):
            r0 = pl.multiple_of(i * tm, tm)
            rows = pl.ds(r0, tm)
            g1, taps = _ffn_gate(pad_s, w_ref, bias_ref, r0, tm)
            sg = _sigmoid(g1)
            da = da_ref[rows, :].astype(F32)
            dv_ref[rows, :] = (da * g1 * sg).astype(BF16)
            dg1 = da * v_ref[rows, :].astype(F32) * (sg * (1.0 + g1 * (1.0 - sg)))
            dpad_s[rows, :] = dg1
            for k in range(FFN_KERNEL):
                dw_s[SUBLANES * k:SUBLANES * (k + 1), :] += _fold_rows(dg1 * taps[FFN_PAD - 2 + k])
            dw_s[SUBLANES * FFN_KERNEL:SUBLANES * (FFN_KERNEL + 1), :] += _fold_rows(dg1)
            return carry

        lax.fori_loop(0, S // tm, first, 0)

        def second(i, carry):
            r0 = pl.multiple_of(i * tm, tm)
            n_rows = tm + FFN_PAD
            taps = _taps(_shifted(dpad_s[pl.ds(r0, n_rows), :], n_rows), range(FFN_KERNEL), tm)
            dg_ref[pl.ds(r0, tm), :] = (w_ref[2:3, :] * taps[0] + w_ref[1:2, :] * taps[1] + w_ref[0:1, :] * taps[2]).astype(BF16)
            return carry

        lax.fori_loop(0, S // tm, second, 0)
        for k in range(FFN_KERNEL):
            dw_ref[k:k + 1, :] = jnp.sum(dw_s[SUBLANES * k:SUBLANES * (k + 1), :], axis=0, keepdims=True)
        dbias_ref[...] = jnp.sum(dw_s[SUBLANES * FFN_KERNEL:SUBLANES * (FFN_KERNEL + 1), :], axis=0, keepdims=True)

    col = lambda off: pl.BlockSpec((S, CB), lambda j: (0, off + j))
    wspec = pl.BlockSpec((FFN_KERNEL, CB), lambda j: (0, j))
    bspec = pl.BlockSpec((1, CB), lambda j: (0, j))
    out, guest_out = _hosted_call(
        body, guest, name="ffn_act_bwd", grid=(nb,),
        in_specs=[col(0), col(nb), wspec, bspec, col(0)],
        out_specs=[col(0), col(0), wspec, bspec],
        out_shape=[jax.ShapeDtypeStruct((S, D_FF), BF16)] * 2
        + [jax.ShapeDtypeStruct((FFN_KERNEL, D_FF), F32), jax.ShapeDtypeStruct((1, D_FF), F32)],
        scratch_shapes=[pltpu.VMEM((S + FFN_PAD, CB), F32), pltpu.VMEM((S + FFN_PAD, CB), F32),
                        pltpu.VMEM((SUBLANES * (FFN_KERNEL + 1), CB), F32)], operands=[u, u, w, bias, dact])
    return (*out, guest_out)


def _loss_grad(y, target, *, tm=512):
    S, D = y.shape

    def body(y_ref, t_ref, dy_ref, l_ref):
        d = y_ref[...] - t_ref[...]
        dy_ref[...] = d * (1.0 / D)

        @pl.when(pl.program_id(0) == 0)
        def _():
            l_ref[...] = jnp.zeros_like(l_ref)

        l_ref[...] += 0.5 * jnp.sum(jnp.mean(d * d, axis=-1, keepdims=True), axis=0, keepdims=True)

    dy, l = pl.pallas_call(
        body, name="loss_grad", grid=(S // tm,),
        in_specs=[pl.BlockSpec((tm, D), lambda i: (i, 0))] * 2,
        out_specs=[pl.BlockSpec((tm, D), lambda i: (i, 0)), pl.BlockSpec((SUBLANES, LANES), lambda i: (0, 0))],
        out_shape=[jax.ShapeDtypeStruct((S, D), F32), jax.ShapeDtypeStruct((SUBLANES, LANES), F32)],
        compiler_params=_params("arbitrary"),
    )(y, target)
    return dy, l[0, 0]


def _row_tile(rows, cap=512):
    t = min(rows, cap)
    while rows % t or t % SUBLANES:
        t -= 1
    return t


def _adam_update(w, g, m, v):
    m1 = ADAM_B1 * m + (1.0 - ADAM_B1) * g
    v1 = ADAM_B2 * v + (1.0 - ADAM_B2) * (g * g)
    m_hat = m1 / (1.0 - ADAM_B1 ** ADAM_STEP)
    v_hat = v1 / (1.0 - ADAM_B2 ** ADAM_STEP)
    return -ADAM_LR * (m_hat / (jnp.sqrt(v_hat) + ADAM_EPS) + ADAM_WD * w), m1, v1


def _adamw_small(w, g, m, v):
    def body(w_ref, g_ref, m_ref, v_ref, d_ref, nm_ref, nv_ref):
        d_ref[...], nm_ref[...], nv_ref[...] = _adam_update(w_ref[...], g_ref[...], m_ref[...], v_ref[...])

    vmem = pl.BlockSpec(memory_space=pltpu.VMEM)
    return pl.pallas_call(body, name="adamw_small", in_specs=[vmem] * 4, out_specs=[vmem] * 3,
                          out_shape=[jax.ShapeDtypeStruct(w.shape, F32)] * 3)(w, g, m, v)


def _adamw_stacked(w, gs, m, v):
    L, R, C = w.shape
    tr = _row_tile(R, 256)

    def body(w_ref, m_ref, v_ref, *rest):
        g_refs, (go_ref, d_ref, nm_ref, nv_ref) = rest[:L], rest[L:]
        for a, g_ref in enumerate(g_refs):
            @pl.when(pl.program_id(0) == a)
            def _():
                gv = g_ref[...]
                go_ref[...] = gv
                d_ref[...], nm_ref[...], nv_ref[...] = _adam_update(w_ref[...], gv, m_ref[...], v_ref[...])

    stacked = pl.BlockSpec((None, tr, C), lambda l, i: (l, i, 0))
    return pl.pallas_call(
        body, name="adamw_stacked", grid=(L, R // tr),
        in_specs=[stacked] * 3 + [pl.BlockSpec((tr, C), lambda l, i, a=a: (jnp.where(l == a, i, 0), 0)) for a in range(L)],
        out_specs=[stacked] * 4, out_shape=[jax.ShapeDtypeStruct((L, R, C), F32)] * 4,
        compiler_params=_params("parallel", "parallel"),
    )(w, m, v, *gs)


CAST_STEPS = 4


def _cast_into_full(ws, layers, kinds, chip):
    n = len(ws)

    def body(chip_ref, *refs):
        for w_ref, o_ref in zip(refs[:n], refs[n:]):
            o_ref[...] = w_ref[...].astype(BF16)

    in_specs, out_specs, out_shape = [], [], []
    for w, layer, kind in zip(ws, layers, kinds):
        _, R, C = w.shape
        tr = R // CAST_STEPS
        assert tr % 16 == 0
        in_specs.append(pl.BlockSpec((None, tr, C), lambda i, chip_ref, layer=layer: (layer, i, 0)))
        if kind == "col":
            out_shape.append(jax.ShapeDtypeStruct((1, R, N_CHIPS * C), BF16))
            out_specs.append(pl.BlockSpec((None, tr, C), lambda i, chip_ref: (0, i, chip_ref[0])))
        else:
            out_shape.append(jax.ShapeDtypeStruct((1, N_CHIPS * R, C), BF16))
            out_specs.append(pl.BlockSpec((None, tr, C), lambda i, chip_ref: (0, chip_ref[0] * CAST_STEPS + i, 0)))
    return pl.pallas_call(
        body, name="cast_into_full",
        grid_spec=pltpu.PrefetchScalarGridSpec(num_scalar_prefetch=1, grid=(CAST_STEPS,),
                                               in_specs=in_specs, out_specs=out_specs),
        out_shape=out_shape, compiler_params=_params("parallel"),
    )(chip, *ws)


ADD_HALF_STEPS = 8
ADD_PARTS_STEPS = 2


def _add_half(g4s, las, c):
    n = len(g4s)

    def body(c_ref, *refs):
        for g_ref, la_ref, o_ref in zip(refs[:n], refs[n:2 * n], refs[2 * n:]):
            o_ref[...] = (g_ref[...].astype(F32) + la_ref[...].astype(F32)).astype(BF16)

    g_specs, la_specs, out_shape = [], [], []
    for g4 in g4s:
        L, _, H, W = g4.shape
        th = L * H // ADD_HALF_STEPS
        per = H // th
        assert th % 16 == 0 and per * th == H
        g_specs.append(pl.BlockSpec((None, None, th, W), lambda i, c_ref, per=per: (i // per, c_ref[0], i % per, 0)))
        la_specs.append(pl.BlockSpec((None, th, W), lambda i, c_ref, per=per: (i // per, i % per, 0)))
        out_shape.append(jax.ShapeDtypeStruct((L, H, W), BF16))
    return pl.pallas_call(
        body, name="add_half",
        grid_spec=pltpu.PrefetchScalarGridSpec(num_scalar_prefetch=1, grid=(ADD_HALF_STEPS,),
                                               in_specs=g_specs + la_specs, out_specs=la_specs),
        out_shape=out_shape, compiler_params=_params("parallel"),
    )(c, *g4s, *las)


def _add_parts(ps, lbs, place, kinds):
    n = len(ps)

    def body(s_ref, *refs):
        for p_ref, lb_ref, o_ref in zip(refs[:n], refs[n:2 * n], refs[2 * n:]):
            acc = p_ref[...].astype(F32)
            for k in range(N_CHIPS - 1):
                acc = acc + lb_ref[k].astype(F32)
            o_ref[...] = acc

    p_specs, lb_specs, out_specs, out_shape = [], [], [], []
    for lb, kind in zip(lbs, kinds):
        _, L, H, C = lb.shape
        th = H // ADD_PARTS_STEPS
        assert L == 1 and th % 16 == 0
        if kind == "col":
            p_specs.append(pl.BlockSpec((None, th, C), lambda i, s_ref: (0, i, s_ref[0])))
        else:
            p_specs.append(pl.BlockSpec((None, None, th, C), lambda i, s_ref: (0, s_ref[0], i, 0)))
        lb_specs.append(pl.BlockSpec((N_CHIPS - 1, None, th, C), lambda i, s_ref: (0, 0, i, 0)))
        out_specs.append(pl.BlockSpec((None, None, th, C), lambda i, s_ref: (0, s_ref[1], i, 0)))
        out_shape.append(jax.ShapeDtypeStruct((L, 2, H, C), F32))
    return pl.pallas_call(
        body, name="add_parts",
        grid_spec=pltpu.PrefetchScalarGridSpec(num_scalar_prefetch=1, grid=(ADD_PARTS_STEPS,),
                                               in_specs=p_specs + lb_specs, out_specs=out_specs),
        out_shape=out_shape, compiler_params=_params("parallel"),
    )(place, *ps, *lbs)


ANY = pl.BlockSpec(memory_space=pl.ANY)


def _place():
    x, y, c = lax.axis_index("x"), lax.axis_index("y"), lax.axis_index("c")
    chips = [(1 - x, y), (x, 1 - y), (1 - x, 1 - y)]
    return x, y, c, chips


def _comm_call(body, name, ins, out_shape, n_remote, n_local, aliases=None):
    scratch = [pltpu.SemaphoreType.DMA((n_remote,)), pltpu.SemaphoreType.DMA((n_remote,))]
    if n_local:
        scratch.append(pltpu.SemaphoreType.DMA((n_local,)))
    return pl.pallas_call(
        body, name=name, in_specs=[ANY] * len(ins), out_specs=[ANY] * len(out_shape), out_shape=out_shape,
        scratch_shapes=scratch, input_output_aliases=aliases or {},
        compiler_params=pltpu.CompilerParams(has_side_effects=True),
    )(*ins)


def _remote(src, dst, send, recv, k, to):
    return pltpu.make_async_remote_copy(src_ref=src, dst_ref=dst, send_sem=send.at[k], recv_sem=recv.at[k],
                                        device_id=to, device_id_type=MESH)


def _gather_weights(fulls, kinds):
    n = len(fulls)
    out_shape = [jax.ShapeDtypeStruct(f.shape, f.dtype) for f in fulls]

    def body(*refs):
        outs, (send, recv) = refs[n:2 * n], refs[2 * n:]
        first = _gather_copies(outs, kinds, send, recv, 0)
        for cp in first:
            cp.start()
        for cp in first:
            cp.wait()
        passed = _pass_on_copies(outs, kinds, send, recv, len(first))
        for cp in passed:
            cp.start()
        for cp in passed:
            cp.wait()

    return _comm_call(body, "gather_weights", fulls, out_shape, 2 * n * (N_CHIPS - 1), 0, {a: a for a in range(n)})


def _window(ref, kind, s, h):
    if kind == "col":
        H, C = ref.shape[1] // 2, ref.shape[2] // N_CHIPS
        return ref.at[:, pl.ds(pl.multiple_of(h * H, 16), H), pl.ds(pl.multiple_of(s * C, LANES), C)]
    R = ref.shape[1] // N_CHIPS
    return ref.at[:, pl.ds(pl.multiple_of(s * R + h * (R // 2), 16), R // 2), :]


def _gather_copies(outs, kinds, send, recv, sem0):
    x, y, c, chips = _place()
    me = 2 * x + y
    return [_remote(_window(o, kind, me, c), _window(o, kind, me, c), send, recv, sem0 + a * (N_CHIPS - 1) + k, (*chip, c))
            for a, (o, kind) in enumerate(zip(outs, kinds)) for k, chip in enumerate(chips)]


def _pass_on_copies(outs, kinds, send, recv, sem0):
    x, y, c, chips = _place()
    cps = []
    for a, (o, kind) in enumerate(zip(outs, kinds)):
        for k, chip in enumerate(chips):
            landed = _window(o, kind, 2 * chip[0] + chip[1], c)
            cps.append(_remote(landed, landed, send, recv, sem0 + a * (N_CHIPS - 1) + k, (x, y, 1 - c)))
    return cps


def _gather_small(shards):
    n = len(shards)
    out_shape = [jax.ShapeDtypeStruct((N_CHIPS,) + s.shape, s.dtype) for s in shards]

    def body(*refs):
        srcs, outs, (send, recv, loc) = refs[:n], refs[n:2 * n], refs[2 * n:]
        x, y, c, chips = _place()
        me = 2 * x + y
        remote, local = [], []
        for a in range(n):
            local.append(pltpu.make_async_copy(srcs[a], outs[a].at[me], loc.at[a]))
            for k, chip in enumerate(chips):
                remote.append(_remote(srcs[a], outs[a].at[me], send, recv, a * (N_CHIPS - 1) + k, (*chip, c)))
        for cp in local + remote:
            cp.start()
        for cp in remote + local:
            cp.wait()

    return _comm_call(body, "gather_small", shards, out_shape, n * (N_CHIPS - 1), n)


def _exchange_halves(g4s):
    n = len(g4s)
    out_shape = [jax.ShapeDtypeStruct((g.shape[0],) + g.shape[2:], g.dtype) for g in g4s]

    def body(*refs):
        gs, las, (send, recv) = refs[:n], refs[n:2 * n], refs[2 * n:]
        x, y, c, _ = _place()
        cps = [_remote(gs[a].at[:, 1 - c], las[a], send, recv, a, (x, y, 1 - c)) for a in range(n)]
        for cp in cps:
            cp.start()
        for cp in cps:
            cp.wait()

    return _comm_call(body, "exchange_halves", g4s, out_shape, n, 0)


def _scatter_partials(ps, kinds):
    n = len(ps)

    def body(*refs):
        srcs, lbs, (send, recv) = refs[:n], refs[n:2 * n], refs[2 * n:]
        cps = _scatter_copies(srcs, lbs, kinds, send, recv)
        for cp in cps:
            cp.start()
        for cp in cps:
            cp.wait()

    return _comm_call(body, "scatter_partials", ps, _scatter_shapes(ps, kinds), n * (N_CHIPS - 1), 0)


def _scatter_shapes(ps, kinds):
    out_shape = []
    for p, kind in zip(ps, kinds):
        L, H, C = (p.shape[0], p.shape[1], p.shape[2] // N_CHIPS) if kind == "col" else (p.shape[0], p.shape[2], p.shape[3])
        out_shape.append(jax.ShapeDtypeStruct((N_CHIPS - 1, L, H, C), p.dtype))
    return out_shape


def _scatter_copies(srcs, lbs, kinds, send, recv, sem0=0):
    x, y, c, chips = _place()
    cps = []
    for a, (src, lb, kind) in enumerate(zip(srcs, lbs, kinds)):
        C = lb.shape[3]
        for k, chip in enumerate(chips):
            s = 2 * chip[0] + chip[1]
            part = src.at[:, :, pl.ds(pl.multiple_of(s * C, LANES), C)] if kind == "col" else src.at[:, s]
            cps.append(_remote(part, lb.at[k], send, recv, sem0 + a * (N_CHIPS - 1) + k, (*chip, c)))
    return cps


def _share_halves(g4s):
    n = len(g4s)
    out_shape = [jax.ShapeDtypeStruct(g.shape, g.dtype) for g in g4s]

    def body(*refs):
        outs, (send, recv) = refs[n:2 * n], refs[2 * n:]
        x, y, c, _ = _place()
        cps = [_remote(outs[a].at[:, c], outs[a].at[:, c], send, recv, a, (x, y, 1 - c)) for a in range(n)]
        for cp in cps:
            cp.start()
        for cp in cps:
            cp.wait()

    return _comm_call(body, "share_halves", g4s, out_shape, n, 0, {a: a for a in range(n)})


def _allreduce_small(part):
    N = part.shape[0]

    def body(p_ref, o_ref, buf, send, recv):
        x, y, c, _ = _place()
        me = 4 * x + 2 * y + c
        buf[me] = p_ref[...]
        cps = []
        for k in range(1, N_DEV):
            peer = (1 - x if k & 4 else x, 1 - y if k & 2 else y, 1 - c if k & 1 else c)
            cps.append(_remote(p_ref, buf.at[me], send, recv, k - 1, peer))
        for cp in cps:
            cp.start()
        for cp in cps:
            cp.wait()
        acc = buf[0]
        for i in range(1, N_DEV):
            acc = acc + buf[i]
        o_ref[...] = acc

    vmem = pl.BlockSpec(memory_space=pltpu.VMEM)
    return pl.pallas_call(
        body, name="allreduce_small", in_specs=[vmem], out_specs=vmem,
        out_shape=jax.ShapeDtypeStruct((N, LANES), F32),
        scratch_shapes=[pltpu.VMEM((N_DEV, N, LANES), F32), pltpu.SemaphoreType.DMA((N_DEV - 1,)),
                        pltpu.SemaphoreType.DMA((N_DEV - 1,))],
        compiler_params=pltpu.CompilerParams(has_side_effects=True, vmem_limit_bytes=VMEM_LIMIT_BYTES),
    )(part)


AFTER_ATTENTION = ("w_out", "w_up", "w_down")


def _layer_fwd(x, p, full=None, l=0):
    comm = full is not None

    def guest(first, second):
        first, second = ([k for k in ks if comm and k[0] < len(full)] for ks in (first, second))
        if not first + second:
            return None, []
        return _gather_guest([full[i][n] for i, n in first], [BIG_KIND[n] for _, n in first],
                             [full[i][n] for i, n in second], [BIG_KIND[n] for _, n in second]), first + second

    def done(keys, bufs):
        for (i, n), buf in zip(keys, bufs):
            full[i][n] = buf

    def weight(n):
        return full[l][n][0] if comm else p[n]

    in_sweep = ([(l, "w_up")] if l else [(l, n) for n in AFTER_ATTENTION]) if comm else []
    g, keys = guest([], [(l, "w_out")] if l else [])
    h1t, proj, out = _norm_matmul(x, p["norm1_g"], weight("w_in"), g)
    done(keys, out)
    attn, ltot, attnt, out = _attn_fwd(proj, p["q_norm_g"], p["k_norm_g"],
                                       [full[i][n] for i, n in in_sweep], [BIG_KIND[n] for _, n in in_sweep])
    done(in_sweep, out)
    g, keys = guest([], in_sweep)
    c1, out = _glu_conv_fwd(proj, p["conv_dw_w"], p["conv_dw_b"], g)
    done(keys, out)
    c, ct = _ln_silu_fwd(c1, p["conv_ln_g"], p["conv_ln_b"])
    x_mid, _ = _matmul_res([attn, c], weight("w_out"), x)
    g, keys = guest([(l + 1, "w_down")], [])
    h2t, u, out = _norm_matmul(x_mid, p["norm2_g"], weight("w_up"), g)
    done(keys, out)
    g, keys = guest([(l + 1, "w_in")], [(l + 1, "w_down")])
    act, out = _ffn_act_fwd(u, p["ffn_dw_w"], p["ffn_dw_b"], g)
    done(keys, out)
    g, keys = guest([(l + 1, "w_out")], [(l + 1, "w_in")])
    x_out, out = _matmul_res([act], weight("w_down"), x_mid, g)
    done(keys, out)
    saved = dict(x=x, h1t=h1t, proj=proj, attnt=attnt, ltot=ltot, c1=c1, ct=ct, x_mid=x_mid, h2t=h2t, u=u, act=act)
    return x_out, saved


def _chip_partials(gs, kinds, core):
    g4s = _halves_view(gs, kinds)
    return _add_halves(g4s, _exchange_halves(g4s), kinds, core)


def _halves_view(gs, kinds):
    return [g.reshape(1, 2, g.shape[0] // 2, g.shape[1]) if kind == "col"
            else g.reshape(N_CHIPS, 2, g.shape[0] // N_CHIPS // 2, g.shape[1]) for g, kind in zip(gs, kinds)]


def _add_halves(g4s, received, kinds, core):
    parts = _add_half(g4s, received, core)
    return [p if kind == "col" else p.reshape(1, N_CHIPS, p.shape[1], p.shape[2]) for p, kind in zip(parts, kinds)]


def _gather_guest(first, first_kinds, second, second_kinds):
    bufs = list(first) + list(second)
    n1 = len(first)

    def copies(ins, outs, send, recv, sem0=0):
        return (_gather_copies(outs[:n1], first_kinds, send, recv, sem0)
                + _pass_on_copies(outs[n1:], second_kinds, send, recv, sem0 + n1 * (N_CHIPS - 1)))

    return _Guest("gather", bufs, [jax.ShapeDtypeStruct(f.shape, f.dtype) for f in bufs],
                  {a: a for a in range(len(bufs))}, copies, len(bufs) * (N_CHIPS - 1))


def _scatter_guest(parts, kinds):
    return _Guest("scatter", list(parts), _scatter_shapes(parts, kinds), {},
                  lambda ins, outs, send, recv, sem0=0: _scatter_copies(ins, outs, kinds, send, recv, sem0),
                  len(parts) * (N_CHIPS - 1))


def _exchange_guest(g4s):
    def copies(ins, outs, send, recv, sem0=0):
        x, y, c, _ = _place()
        return [_remote(g.at[:, 1 - c], la, send, recv, sem0 + a, (x, y, 1 - c)) for a, (g, la) in enumerate(zip(ins, outs))]

    out_shape = [jax.ShapeDtypeStruct((g.shape[0],) + g.shape[2:], g.dtype) for g in g4s]
    return _Guest("exchange", list(g4s), out_shape, {}, copies, len(g4s))


def _share_guest(g4s):
    def copies(ins, outs, send, recv, sem0=0):
        x, y, c, _ = _place()
        return [_remote(o.at[:, c], o.at[:, c], send, recv, sem0 + a, (x, y, 1 - c)) for a, o in enumerate(outs)]

    n = len(g4s)
    return _Guest("share", list(g4s), [jax.ShapeDtypeStruct(g.shape, g.dtype) for g in g4s], {a: a for a in range(n)}, copies, n)


def _owned_sums(parts, landed, kinds, place):
    halves = _add_parts(parts, landed, place, kinds)
    return [g.reshape(2 * g.shape[2], g.shape[3]) for g in _share_halves(halves)]


def _layer_bwd(dx_out, s, p, comm=None):
    g = {}
    on = comm is not None
    core, place, pending = comm if on else (None, None, [])
    n_p = len(pending)
    finished = lambda arrays: [a.reshape(2 * a.shape[2], a.shape[3]) for a in arrays]

    if n_p:
        g4s_p = _halves_view(pending, ["col"] * n_p)
    dact, received = _matmul_nt([dx_out], p["w_down"], BF16, _exchange_guest(g4s_p) if n_p else None)
    if n_p:
        parts_p = _add_halves(g4s_p, received, ["col"] * n_p, core)
    g["w_down"] = _matmul_tn([s["act"]], [dx_out], tk=D_FF // 2, tn=D_MODEL, transposed=False)
    dgate, dval, g["ffn_dw_w"], g["ffn_dw_b"], landed_p = _ffn_act_bwd(
        s["u"], p["ffn_dw_w"], p["ffn_dw_b"], dact, _scatter_guest(parts_p, ["col"] * n_p) if n_p else None)
    guest = _share_guest(_add_parts(parts_p, landed_p, place, ["col"] * n_p)) if n_p else None
    dx_mid, g["norm2_g"], shared_p = _matmul_nt_rmsbwd([dgate, dval], p["w_up"], s["x_mid"], p["norm2_g"], dx_out, guest)
    g["w_up"] = _matmul_tn([s["h2t"]], [dgate, dval], tk=D_MODEL, tn=D_FF // 2)
    dmix, _ = _matmul_nt([dx_mid], p["w_out"], F32)
    g["w_out"] = _matmul_tn([s["attnt"], s["ct"]], [dx_mid], tk=ATTN_WIDTH, tn=D_MODEL)
    dc1, g["conv_ln_g"], g["conv_ln_b"] = _ln_silu_bwd(s["c1"], p["conv_ln_g"], p["conv_ln_b"], dmix)

    late = AFTER_ATTENTION
    parts, kinds = [], []
    if on:
        kinds = [BIG_KIND[n] for n in late]
        g4s = _halves_view([g[n] for n in late], kinds)
    da, db, g["conv_dw_w"], g["conv_dw_b"], received = _glu_conv_bwd(
        s["proj"], p["conv_dw_w"], dc1, _exchange_guest(g4s) if on else None)
    if on:
        parts = _add_halves(g4s, received, kinds, core)
    (dq, dk, dv, g["q_norm_g"], g["k_norm_g"]), landed = _attn_bwd(
        s["proj"], p["q_norm_g"], p["k_norm_g"], s["ltot"], dmix, parts, kinds)
    guest = _share_guest(_add_parts(parts, landed, place, kinds)) if on else None
    pieces = [dq, dk, dv, da, db]
    dx, g["norm1_g"], shared = _matmul_nt_rmsbwd(pieces, p["w_in"], s["x"], p["norm1_g"], dx_mid, guest)
    g["w_in"] = _matmul_tn([s["h1t"]], pieces, tk=D_MODEL, tn=ATTN_WIDTH)
    sums = {}
    if on:
        sums = dict(zip(late, finished(shared)), pending=finished(shared_p))
    return dx, g, sums


WEIGHTS = ("norm1_g", "w_in", "q_norm_g", "k_norm_g", "conv_dw_w", "conv_dw_b", "conv_ln_g", "conv_ln_b",
           "w_out", "norm2_g", "w_up", "ffn_dw_w", "ffn_dw_b", "w_down")
BIG = ("w_in", "w_out", "w_up", "w_down")
BIG_KIND = {"w_in": "col", "w_out": "row", "w_up": "col", "w_down": "row"}
SMALL_SHARDED = ("conv_dw_w", "ffn_dw_w")
REPLICATED = tuple(n for n in WEIGHTS if n not in BIG + SMALL_SHARDED)


def _pack(arrays):
    flat = jnp.concatenate([a.reshape(-1) for a in arrays])
    rows = -(-flat.shape[0] // (SUBLANES * LANES)) * SUBLANES
    return jnp.pad(flat, (0, rows * LANES - flat.shape[0])).reshape(rows, LANES)


def _unpack(packed, shapes):
    flat = packed.reshape(-1)
    out, off = [], 0
    for shape in shapes:
        size = 1
        for d in shape:
            size *= d
        out.append(flat[off:off + size].reshape(shape))
        off += size
    return out


def _unshard_last(stacked):
    n, L, K, C = stacked.shape
    return jnp.transpose(stacked, (1, 2, 0, 3)).reshape(L, K, n * C)


def kernel(x, norm1_g, w_in, q_norm_g, k_norm_g, conv_dw_w, conv_dw_b, conv_ln_g, conv_ln_b, w_out, norm2_g, w_up, ffn_dw_w, ffn_dw_b, w_down, loss_target, m_norm1_g, m_w_in, m_q_norm_g, m_k_norm_g, m_conv_dw_w, m_conv_dw_b, m_conv_ln_g, m_conv_ln_b, m_w_out, m_norm2_g, m_w_up, m_ffn_dw_w, m_ffn_dw_b, m_w_down, v_norm1_g, v_w_in, v_q_norm_g, v_k_norm_g, v_conv_dw_w, v_conv_dw_b, v_conv_ln_g, v_conv_ln_b, v_w_out, v_norm2_g, v_w_up, v_ffn_dw_w, v_ffn_dw_b, v_w_down):
    given = dict(locals())
    w = {n: given[n] for n in WEIGHTS}
    m = {n: given["m_" + n] for n in WEIGHTS}
    v = {n: given["v_" + n] for n in WEIGHTS}
    chip = 2 * lax.axis_index("x") + lax.axis_index("y")
    core = lax.axis_index("c")
    chip_arr = jnp.reshape(chip, (1,)).astype(jnp.int32)
    core_arr = jnp.reshape(core, (1,)).astype(jnp.int32)
    L = DEPTH

    place = jnp.concatenate([chip_arr, core_arr])

    first = _cast_into_full([w["w_in"]], [0], ["col"], chip_arr)
    rest = [(l, n) for l in range(L) for n in BIG if (l, n) != (0, "w_in")]
    cast = dict(zip(rest, _cast_into_full([w[n] for _, n in rest], [l for l, _ in rest],
                                          [BIG_KIND[n] for _, n in rest], chip_arr)))
    cast[0, "w_in"] = first[0]
    full = [{n: cast[l, n] for n in BIG} for l in range(L)]
    full[0]["w_in"] = _gather_weights([full[0]["w_in"]], ["col"])[0]
    small_full = {n: _unshard_last(stacked)
                  for n, stacked in zip(SMALL_SHARDED, _gather_small([w[n] for n in SMALL_SHARDED]))}
    params = []
    for l in range(L):
        p = {n: small_full[n][l] for n in SMALL_SHARDED}
        p.update({n: w[n][l][None] for n in REPLICATED})
        params.append(p)

    act = x[0]
    saved = []
    for l in range(L):
        act, s = _layer_fwd(act, params[l], full, l)
        saved.append(s)
    for l in range(L):
        params[l].update({n: full[l][n][0] for n in BIG})
    dx, loss_part = _loss_grad(act, loss_target[0])
    loss = lax.psum(loss_part, ("x", "y", "c"))

    grads = [None] * L
    summed = {}
    pending = []
    for l in reversed(range(L)):
        dx, grads[l], sums = _layer_bwd(dx, saved[l], params[l], (core_arr, place, pending))
        summed.update({(l, n): sums[n] for n in AFTER_ATTENTION})
        if pending:
            summed[l + 1, "w_in"] = sums["pending"][0]
        pending = [grads[l]["w_in"]]
    parts = _chip_partials(pending, ["col"], core_arr)
    summed[0, "w_in"] = _owned_sums(parts, _scatter_partials(parts, ["col"]), ["col"], place)[0]

    grad, delta, new_m, new_v = {}, {}, {}, {}
    for n in BIG:
        grad[n], delta[n], new_m[n], new_v[n] = _adamw_stacked(w[n], [summed[l, n] for l in range(L)], m[n], v[n])

    small = REPLICATED + SMALL_SHARDED
    small_grads = [jnp.stack([grads[l][n] for l in range(L)]) for n in small]
    small_sums = _unpack(_allreduce_small(_pack(small_grads)), [a.shape for a in small_grads])
    for n, g in zip(small, small_sums):
        if n in REPLICATED:
            grad[n] = g.reshape(w[n].shape)
        else:
            width = w[n].shape[2]
            grad[n] = lax.dynamic_slice_in_dim(g, chip * width, width, axis=2)
    for n in small:
        delta[n], new_m[n], new_v[n] = _adamw_small(w[n], grad[n], m[n], v[n])

    return (loss, dx[None], *[grad[n] for n in WEIGHTS], *[delta[n] for n in WEIGHTS],
            *[new_m[n] for n in WEIGHTS], *[new_v[n] for n in WEIGHTS])
```

```python
import jax
import jax.numpy as jnp
from jax import lax
from jax.experimental import pallas as pl
from jax.experimental.pallas import tpu as pltpu

F32 = jnp.float32
BF16 = jnp.bfloat16

DEPTH = 4
D_MODEL = 1024
HEADS = 8
HEAD_DIM = 64
ATTN_WIDTH = HEADS * HEAD_DIM
CONV_WIDTH = D_MODEL - ATTN_WIDTH
CONV_KERNEL = 31
D_FF = 2816
FFN_KERNEL = 3
EPS = 1e-6
ADAM_LR, ADAM_B1, ADAM_B2, ADAM_EPS, ADAM_WD, ADAM_STEP = 0.001, 0.9, 0.999, 1e-08, 0.01, 10

N_CHIPS = 4
N_DEV = 8
LANES = 128
SUBLANES = 8
VMEM_LIMIT_BYTES = 56 * 2**20
ATTN_TILE = 256
ATTN_HEADS_PER_STEP = 4
ATTN_BLOCK = ATTN_HEADS_PER_STEP * HEAD_DIM
CONV_PAD = 32
FFN_PAD = 8
MESH = pl.DeviceIdType.MESH


def _params(*sem):
    return pltpu.CompilerParams(dimension_semantics=sem if sem else None, vmem_limit_bytes=VMEM_LIMIT_BYTES)


class _Guest:
    def __init__(self, name, ins, out_shape, aliases, copies, n_sem):
        self.name, self.ins, self.out_shape, self.aliases, self.copies, self.n_sem = name, ins, out_shape, aliases, copies, n_sem


def _hosted_call(body, guest, *, name, grid, in_specs, out_specs, out_shape, scratch_shapes, operands):
    if guest is None:
        out = pl.pallas_call(body, name=name, grid=grid, in_specs=in_specs, out_specs=out_specs, out_shape=out_shape,
                             scratch_shapes=scratch_shapes, compiler_params=_params("arbitrary"))(*operands)
        return list(out), []
    n_in, n_out, n_scr = len(in_specs), len(out_specs), len(scratch_shapes)
    gi, go = len(guest.ins), len(guest.out_shape)

    def hosting(*refs):
        ins, g_in = refs[:n_in], refs[n_in:n_in + gi]
        outs, g_out = refs[n_in + gi:n_in + gi + n_out], refs[n_in + gi + n_out:n_in + gi + n_out + go]
        scratch, (send, recv) = refs[n_in + gi + n_out + go:-2], refs[-2:]

        @pl.when(pl.program_id(0) == 0)
        def _():
            for cp in guest.copies(g_in, g_out, send, recv):
                cp.start()

        body(*ins, *outs, *scratch)

        @pl.when(pl.program_id(0) == grid[0] - 1)
        def _():
            for cp in guest.copies(g_in, g_out, send, recv):
                cp.wait()

    out = pl.pallas_call(
        hosting, name=name + "_" + guest.name, grid=grid,
        in_specs=list(in_specs) + [ANY] * gi, out_specs=list(out_specs) + [ANY] * go,
        out_shape=list(out_shape) + list(guest.out_shape),
        scratch_shapes=list(scratch_shapes) + [pltpu.SemaphoreType.DMA((guest.n_sem,))] * 2,
        input_output_aliases={n_in + a: n_out + b for a, b in guest.aliases.items()},
        compiler_params=pltpu.CompilerParams(dimension_semantics=("arbitrary",), vmem_limit_bytes=VMEM_LIMIT_BYTES,
                                             has_side_effects=True),
    )(*operands, *guest.ins)
    return list(out[:n_out]), list(out[n_out:])


def _dot(a, b):
    return jnp.dot(a, b, preferred_element_type=F32)


def _dot_nt(a, b):
    return lax.dot_general(a, b, (((1,), (1,)), ((), ())), preferred_element_type=F32)


def _dot_tn(a, b):
    return lax.dot_general(a, b, (((0,), (0,)), ((), ())), preferred_element_type=F32)


def _sigmoid(x):
    return 1.0 / (1.0 + jnp.exp(-x))


def _norm_matmul(x, g, w, guest=None, *, out_dtype=F32, tm=256):
    S, D = x.shape
    N = w.shape[1]

    def body(x_ref, g_ref, w_ref, ht_ref, y_ref):
        xv = x_ref[...]
        h = xv * lax.rsqrt(jnp.mean(xv * xv, axis=-1, keepdims=True) + EPS) * g_ref[...]
        ht_ref[...] = h.T.astype(BF16)
        y_ref[...] = _dot(h.astype(BF16), w_ref[...]).astype(out_dtype)

    (ht, y), guest_out = _hosted_call(
        body, guest, name="norm_matmul", grid=(S // tm,),
        in_specs=[pl.BlockSpec((tm, D), lambda i: (i, 0)),
                  pl.BlockSpec((1, D), lambda i: (0, 0)),
                  pl.BlockSpec((D, N), lambda i: (0, 0))],
        out_specs=[pl.BlockSpec((D, tm), lambda i: (0, i)),
                   pl.BlockSpec((tm, N), lambda i: (i, 0))],
        out_shape=[jax.ShapeDtypeStruct((D, S), BF16), jax.ShapeDtypeStruct((S, N), out_dtype)],
        scratch_shapes=[], operands=[x, g, w])
    return ht, y, guest_out


def _matmul_res(pieces, w, res, guest=None, *, tm=512):
    S, N = res.shape
    K = w.shape[0]
    widths = [p.shape[1] for p in pieces]
    assert sum(widths) == K

    def body(*refs):
        p_refs, (w_ref, res_ref, o_ref) = refs[:len(pieces)], refs[len(pieces):]
        acc = res_ref[...]
        off = 0
        for p_ref, kp in zip(p_refs, widths):
            acc = acc + _dot(p_ref[...], w_ref[off:off + kp, :])
            off += kp
        o_ref[...] = acc

    (out,), guest_out = _hosted_call(
        body, guest, name="matmul_res", grid=(S // tm,),
        in_specs=[pl.BlockSpec((tm, kp), lambda i: (i, 0)) for kp in widths]
        + [pl.BlockSpec((K, N), lambda i: (0, 0)), pl.BlockSpec((tm, N), lambda i: (i, 0))],
        out_specs=[pl.BlockSpec((tm, N), lambda i: (i, 0))],
        out_shape=[jax.ShapeDtypeStruct((S, N), F32)],
        scratch_shapes=[], operands=[*pieces, w, res])
    return out, guest_out


def _nt_sum(p_refs, widths, w_ref):
    acc = None
    off = 0
    for p_ref, n_p in zip(p_refs, widths):
        d = _dot_nt(p_ref[...].astype(BF16), w_ref[:, off:off + n_p])
        acc = d if acc is None else acc + d
        off += n_p
    return acc


def _matmul_nt(pieces, w, out_dtype, guest=None, *, tm=512):
    S = pieces[0].shape[0]
    K, N = w.shape
    widths = [p.shape[1] for p in pieces]
    assert sum(widths) == N

    def body(*refs):
        p_refs, (w_ref, o_ref) = refs[:len(pieces)], refs[len(pieces):]
        o_ref[...] = _nt_sum(p_refs, widths, w_ref).astype(out_dtype)

    (out,), guest_out = _hosted_call(
        body, guest, name="matmul_nt", grid=(S // tm,),
        in_specs=[pl.BlockSpec((tm, n_p), lambda i: (i, 0)) for n_p in widths]
        + [pl.BlockSpec((K, N), lambda i: (0, 0))],
        out_specs=[pl.BlockSpec((tm, K), lambda i: (i, 0))],
        out_shape=[jax.ShapeDtypeStruct((S, K), out_dtype)],
        scratch_shapes=[], operands=[*pieces, w])
    return out, guest_out


def _matmul_nt_rmsbwd(pieces, w, x, g, dres, guest=None, *, tm=256):
    S, K = x.shape
    N = w.shape[1]
    widths = [p.shape[1] for p in pieces]
    assert sum(widths) == N

    def body(*refs):
        p_refs, (w_ref, x_ref, g_ref, dres_ref, dx_ref, dg_ref) = refs[:len(pieces)], refs[len(pieces):]
        dh = _nt_sum(p_refs, widths, w_ref)
        xv = x_ref[...]
        r = lax.rsqrt(jnp.mean(xv * xv, axis=-1, keepdims=True) + EPS)
        xh = xv * r
        dxh = dh * g_ref[...]
        dx_ref[...] = dres_ref[...] + r * (dxh - xh * jnp.mean(dxh * xh, axis=-1, keepdims=True))

        @pl.when(pl.program_id(0) == 0)
        def _():
            dg_ref[...] = jnp.zeros_like(dg_ref)

        dg_ref[...] += jnp.sum(dh * xh, axis=0, keepdims=True)

    (dx, dg), guest_out = _hosted_call(
        body, guest, name="matmul_nt_rmsbwd", grid=(S // tm,),
        in_specs=[pl.BlockSpec((tm, n_p), lambda i: (i, 0)) for n_p in widths]
        + [pl.BlockSpec((K, N), lambda i: (0, 0)), pl.BlockSpec((tm, K), lambda i: (i, 0)),
           pl.BlockSpec((1, K), lambda i: (0, 0)), pl.BlockSpec((tm, K), lambda i: (i, 0))],
        out_specs=[pl.BlockSpec((tm, K), lambda i: (i, 0)), pl.BlockSpec((1, K), lambda i: (0, 0))],
        out_shape=[jax.ShapeDtypeStruct((S, K), F32), jax.ShapeDtypeStruct((1, K), F32)],
        scratch_shapes=[], operands=[*pieces, w, x, g, dres])
    return dx, dg, guest_out


def _matmul_tn(xts, dys, *, tk, tn, ts=1024, transposed=True):
    S = dys[0].shape[0]
    xs = xts
    n_s = S // ts
    (mt,) = {x.shape[0 if transposed else 1] // tk for x in xts}
    (nt,) = {d.shape[1] // tn for d in dys}
    product = _dot if transposed else _dot_tn

    def body(*refs):
        x_refs, dy_refs, (o_ref, acc_ref) = refs[:len(xs)], refs[len(xs):len(xs) + len(dys)], refs[len(xs) + len(dys):]
        i, j, s = pl.program_id(0), pl.program_id(1), pl.program_id(2)

        @pl.when(s == 0)
        def _():
            acc_ref[...] = jnp.zeros_like(acc_ref)

        for a, x_ref in enumerate(x_refs):
            for b, dy_ref in enumerate(dy_refs):
                @pl.when((i // mt == a) & (j // nt == b))
                def _():
                    acc_ref[...] += product(x_ref[...], dy_ref[...].astype(BF16))

        @pl.when(s == n_s - 1)
        def _():
            o_ref[...] = acc_ref[...].astype(BF16)

    def x_map(a):
        if transposed:
            return lambda i, j, s: (jnp.where(i // mt == a, i % mt, 0), jnp.where(i // mt == a, s, 0))
        return lambda i, j, s: (jnp.where(i // mt == a, s, 0), jnp.where(i // mt == a, i % mt, 0))

    def dy_map(b):
        return lambda i, j, s: (jnp.where(j // nt == b, s, 0), jnp.where(j // nt == b, j % nt, 0))

    return pl.pallas_call(
        body, name="matmul_tn", grid=(len(xs) * mt, len(dys) * nt, n_s),
        in_specs=[pl.BlockSpec((tk, ts) if transposed else (ts, tk), x_map(a)) for a in range(len(xs))]
        + [pl.BlockSpec((ts, tn), dy_map(b)) for b in range(len(dys))],
        out_specs=pl.BlockSpec((tk, tn), lambda i, j, s: (i, j)),
        out_shape=jax.ShapeDtypeStruct((len(xs) * mt * tk, len(dys) * nt * tn), BF16),
        scratch_shapes=[pltpu.VMEM((tk, tn), F32)],
        compiler_params=_params("parallel", "parallel", "arbitrary"),
    )(*xs, *dys)


def _tri_consts():
    j = jnp.arange(ATTN_TILE)[:, None]
    s = jnp.arange(ATTN_TILE)[None, :]
    return (j > s).astype(BF16), (j <= s).astype(BF16), (j < s).astype(BF16)


SIGN_BIT = 0x80000000
WEIGHT_IS_ZERO = -104.0


def _log_terms(sn):
    minus_abs = lax.bitcast_convert_type(lax.bitcast_convert_type(sn, jnp.uint32) | jnp.uint32(SIGN_BIT), F32)
    lom = jnp.minimum(sn, 0.0) - jnp.log(1.0 + jnp.exp(minus_abs))
    return lom, lom - sn


def _causal_mask():
    t = lax.broadcasted_iota(jnp.int32, (ATTN_TILE, ATTN_TILE), 0)
    s = lax.broadcasted_iota(jnp.int32, (ATTN_TILE, ATTN_TILE), 1)
    return s < t


def _attn_prep(h, q_ref, k_ref, v_ref, qg_ref, kg_ref, qn_s, kn_s, vb_s, n_tiles):
    T = ATTN_TILE
    lanes = slice(HEAD_DIM * h, HEAD_DIM * (h + 1))
    scale = -(HEAD_DIM ** -0.5)

    def prep(i, carry):
        rows = pl.ds(pl.multiple_of(i * T, T), T)
        q = q_ref[rows, lanes]
        k = k_ref[rows, lanes]
        rq = lax.rsqrt(jnp.mean(q * q, axis=-1, keepdims=True) + EPS)
        rk = lax.rsqrt(jnp.mean(k * k, axis=-1, keepdims=True) + EPS)
        qn_s[h, rows, :] = (q * rq * qg_ref[...] * scale).astype(BF16)
        kn_s[h, rows, :] = (k * rk * kg_ref[...]).astype(BF16)
        vb_s[h, rows, :] = v_ref[rows, lanes].astype(BF16)
        return carry

    lax.fori_loop(0, n_tiles, prep, 0)


def _attn_fwd(proj, qg, kg, fulls=(), kinds=()):
    S = proj.shape[0]
    n_comm = len(fulls)
    T = ATTN_TILE
    n_tiles = S // T
    suffix, _, _ = _tri_consts()
    pairs = HEADS // ATTN_HEADS_PER_STEP
    q_blk, k_blk, v_blk = 0, ATTN_WIDTH // ATTN_BLOCK, 2 * ATTN_WIDTH // ATTN_BLOCK

    def body(*refs):
        q_ref, k_ref, v_ref, qg_ref, kg_ref, tri_ref = refs[:6]
        o_ref, lt_ref, ot_ref, start_ref, qn_s, kn_s, vb_s = refs[6 + n_comm:13 + n_comm]
        w_refs = refs[13 + n_comm:13 + 2 * n_comm]
        sems = refs[13 + 2 * n_comm:]
        step_id = pl.program_id(0)
        if n_comm:
            @pl.when(pl.program_id(0) == 0)
            def _():
                for cp in _gather_copies(w_refs, kinds, *sems, 0):
                    cp.start()

        heads = range(ATTN_HEADS_PER_STEP)
        for h in heads:
            _attn_prep(h, q_ref, k_ref, v_ref, qg_ref, kg_ref, qn_s, kn_s, vb_s, n_tiles)

        def q_tile(qi, carry0):
            qrows = pl.ds(pl.multiple_of(qi * T, T), T)
            qt = [qn_s[h, qrows, :] for h in heads]

            def tile(kj, carry, diag):
                krows = pl.ds(pl.multiple_of(kj * T, T), T)
                s = [_dot_nt(qt[h], kn_s[h, krows, :]) for h in heads]
                terms = [_log_terms(s[h]) for h in heads]
                lom = [terms[h][0] for h in heads]
                if diag:
                    mask = _causal_mask()
                    lom = [jnp.where(mask, lom[h], 0.0) for h in heads]
                lom = [lom[h].astype(BF16) for h in heads]
                tail = [_dot(lom[h], tri_ref[...]) for h in heads]
                w = [jnp.exp(terms[h][1] + tail[h] + carry[h][1]) for h in heads]
                if diag:
                    w = [jnp.where(mask, w[h], 0.0) for h in heads]
                return tuple((carry[h][0] + _dot(w[h].astype(BF16), vb_s[h, krows, :]),
                              carry[h][1] + (tail[h][:, 0:1] + lom[h][:, 0:1].astype(F32))) for h in heads)

            init = tuple((jnp.zeros((T, HEAD_DIM), F32), jnp.zeros((T, 1), F32)) for h in heads)
            def alive(cr):
                worst = cr[0][1]
                for h in heads[1:]:
                    worst = jnp.maximum(worst, cr[h][1])
                return jnp.max(worst) >= WEIGHT_IS_ZERO

            def step(state):
                t, _, cr = state
                cr = tile(qi - 1 - t, cr, False)
                return t + 1, alive(cr), cr

            first = tile(qi, init, True)
            swept, _, carry = lax.while_loop(lambda st: (st[0] < qi) & st[1], step, (jnp.int32(0), alive(first), first))
            start_ref[step_id, qi] = (qi - swept).astype(F32)
            o = jnp.concatenate([carry[h][0] for h in heads], axis=1)
            o_ref[qrows, :] = o.astype(BF16)
            ot_ref[:, qrows] = o.T.astype(BF16)
            for h in heads:
                lt_ref[qrows, HEAD_DIM * h:HEAD_DIM * (h + 1)] = jnp.broadcast_to(carry[h][1], (T, HEAD_DIM))
            return carry0

        lax.fori_loop(0, n_tiles, q_tile, 0)
        if n_comm:
            @pl.when(pl.program_id(0) == pairs - 1)
            def _():
                for cp in _gather_copies(w_refs, kinds, *sems, 0):
                    cp.wait()

    n_sem = n_comm * (N_CHIPS - 1)
    per_head = pl.BlockSpec((ATTN_HEADS_PER_STEP, S, HEAD_DIM), lambda p: (p, 0, 0))
    out = pl.pallas_call(
        body, name="attn_fwd_gather" if n_comm else "attn_fwd", grid=(pairs,),
        in_specs=[pl.BlockSpec((S, ATTN_BLOCK), lambda p: (0, q_blk + p)),
                  pl.BlockSpec((S, ATTN_BLOCK), lambda p: (0, k_blk + p)),
                  pl.BlockSpec((S, ATTN_BLOCK), lambda p: (0, v_blk + p)),
                  pl.BlockSpec((1, HEAD_DIM), lambda p: (0, 0)),
                  pl.BlockSpec((1, HEAD_DIM), lambda p: (0, 0)),
                  pl.BlockSpec((T, T), lambda p: (0, 0))] + [ANY] * n_comm,
        out_specs=[pl.BlockSpec((S, ATTN_BLOCK), lambda p: (0, p)),
                   pl.BlockSpec((None, S, ATTN_BLOCK), lambda p: (p, 0, 0)),
                   pl.BlockSpec((ATTN_BLOCK, S), lambda p: (p, 0)),
                   pl.BlockSpec(memory_space=pltpu.SMEM)] + [per_head] * 3 + [ANY] * n_comm,
        out_shape=[jax.ShapeDtypeStruct((S, ATTN_WIDTH), BF16),
                   jax.ShapeDtypeStruct((pairs, S, ATTN_BLOCK), F32),
                   jax.ShapeDtypeStruct((ATTN_WIDTH, S), BF16),
                   jax.ShapeDtypeStruct((pairs, n_tiles), F32)] + [jax.ShapeDtypeStruct((HEADS, S, HEAD_DIM), BF16)] * 3
        + [jax.ShapeDtypeStruct(f.shape, f.dtype) for f in fulls],
        scratch_shapes=[pltpu.SemaphoreType.DMA((n_sem,))] * 2 if n_comm else [],
        input_output_aliases={6 + a: 7 + a for a in range(n_comm)},
        compiler_params=pltpu.CompilerParams(dimension_semantics=("arbitrary",), vmem_limit_bytes=VMEM_LIMIT_BYTES,
                                             has_side_effects=bool(n_comm)),
    )(proj, proj, proj, qg, kg, suffix, *fulls)
    return out[0], (out[1], out[3], out[4], out[5], out[6]), out[2], list(out[7:])


def _attn_bwd(proj, qg, kg, ltot, dmix, parts=(), kinds=()):
    S = proj.shape[0]
    n_comm = len(parts)
    T = ATTN_TILE
    n_tiles = S // T
    _, prefix_incl, prefix_excl = _tri_consts()
    pairs = HEADS // ATTN_HEADS_PER_STEP
    q_blk, k_blk, v_blk = 0, ATTN_WIDTH // ATTN_BLOCK, 2 * ATTN_WIDTH // ATTN_BLOCK
    scale = HEAD_DIM ** -0.5

    def body(*refs):
        q_ref, k_ref, qg_ref, kg_ref, lt_ref, do_ref, ti_ref, te_ref, start_ref, qn_s, kn_s, vb_s = refs[:12]
        p_refs = refs[12:12 + n_comm]
        dq_ref, dk_ref, dv_ref, dqg_ref, dkg_ref = refs[12 + n_comm:17 + n_comm]
        lb_refs = refs[17 + n_comm:17 + 2 * n_comm]
        dq_s, dk_s, dv_s = refs[17 + 2 * n_comm:20 + 2 * n_comm]
        sems = refs[20 + 2 * n_comm:]
        step_id = pl.program_id(0)

        @pl.when(pl.program_id(0) == 0)
        def _():
            dqg_ref[...] = jnp.zeros_like(dqg_ref)
            dkg_ref[...] = jnp.zeros_like(dkg_ref)
            for cp in _scatter_copies(p_refs, lb_refs, kinds, *sems) if n_comm else ():
                cp.start()

        heads = range(ATTN_HEADS_PER_STEP)
        dk_s[...] = jnp.zeros_like(dk_s)
        dv_s[...] = jnp.zeros_like(dv_s)

        def q_tile(qi, carry0):
            qrows = pl.ds(pl.multiple_of(qi * T, T), T)
            qt = [qn_s[h, qrows, :] for h in heads]
            dob = [do_ref[qrows, HEAD_DIM * h:HEAD_DIM * (h + 1)].astype(BF16) for h in heads]
            lt = [lt_ref[qrows, HEAD_DIM * h:HEAD_DIM * h + 1] for h in heads]

            def tile(kj, carry, diag):
                krows = pl.ds(pl.multiple_of(kj * T, T), T)
                kt = [kn_s[h, krows, :] for h in heads]
                s = [_dot_nt(qt[h], kt[h]) for h in heads]
                dw = [_dot_nt(dob[h], vb_s[h, krows, :]) for h in heads]
                terms = [_log_terms(s[h]) for h in heads]
                lom = [terms[h][0] for h in heads]
                if diag:
                    mask = _causal_mask()
                    lom = [jnp.where(mask, lom[h], 0.0) for h in heads]
                pin = [_dot(lom[h].astype(BF16), ti_ref[...]) for h in heads]
                w = [jnp.exp(terms[h][1] + ((lt[h] - carry[h][1]) - pin[h])) for h in heads]
                if diag:
                    w = [jnp.where(mask, w[h], 0.0) for h in heads]
                e = [w[h] * dw[h] for h in heads]
                gex = [_dot(e[h].astype(BF16), te_ref[...]) for h in heads]
                dz = [e[h] - jnp.exp(terms[h][1]) * (e[h] + (carry[h][2] + gex[h])) for h in heads]
                if diag:
                    dz = [jnp.where(mask, dz[h], 0.0) for h in heads]
                new = []
                for h in heads:
                    dzb = dz[h].astype(BF16)
                    dk_s[h, krows, :] += _dot_tn(dzb, qt[h])
                    dv_s[h, krows, :] += _dot_tn(w[h].astype(BF16), dob[h])
                    new.append((carry[h][0] + _dot(dzb, kt[h]),
                                carry[h][1] + pin[h][:, T - 1:T],
                                carry[h][2] + gex[h][:, T - 1:T] + e[h][:, T - 1:T]))
                return tuple(new)

            zero = jnp.zeros((T, 1), F32)
            init = tuple((jnp.zeros((T, HEAD_DIM), F32), zero, zero) for h in heads)
            first = jnp.clip(start_ref[step_id, qi].astype(jnp.int32), 0, qi)
            last = tile(qi, lax.fori_loop(first, qi, lambda kj, cr: tile(kj, cr, False), init), True)
            for h in heads:
                dq_s[h, qrows, :] = last[h][0]
            return carry0

        lax.fori_loop(0, n_tiles, q_tile, 0)

        def finish(i, carry):
            rows = pl.ds(pl.multiple_of(i * T, T), T)
            new = []
            for h in heads:
                lanes = slice(HEAD_DIM * h, HEAD_DIM * (h + 1))
                q = q_ref[rows, lanes]
                k = k_ref[rows, lanes]
                rq = lax.rsqrt(jnp.mean(q * q, axis=-1, keepdims=True) + EPS)
                rk = lax.rsqrt(jnp.mean(k * k, axis=-1, keepdims=True) + EPS)
                qh = q * rq
                kh = k * rk
                dqn = dq_s[h, rows, :] * scale
                dkn = -dk_s[h, rows, :]
                dqh = dqn * qg_ref[...]
                dkh = dkn * kg_ref[...]
                dq_ref[rows, lanes] = (rq * (dqh - qh * jnp.mean(dqh * qh, axis=-1, keepdims=True))).astype(BF16)
                dk_ref[rows, lanes] = (rk * (dkh - kh * jnp.mean(dkh * kh, axis=-1, keepdims=True))).astype(BF16)
                dv_ref[rows, lanes] = dv_s[h, rows, :].astype(BF16)
                new.append(carry[2 * h] + jnp.sum(dqn * qh, axis=0, keepdims=True))
                new.append(carry[2 * h + 1] + jnp.sum(dkn * kh, axis=0, keepdims=True))
            return tuple(new)

        zero = jnp.zeros((1, HEAD_DIM), F32)
        sums = lax.fori_loop(0, n_tiles, finish, (zero,) * (2 * len(heads)))
        dqg_ref[...] += sum(sums[0::2])
        dkg_ref[...] += sum(sums[1::2])
        if n_comm:
            @pl.when(pl.program_id(0) == pairs - 1)
            def _():
                for cp in _scatter_copies(p_refs, lb_refs, kinds, *sems):
                    cp.wait()

    blk = lambda off: pl.BlockSpec((S, ATTN_BLOCK), lambda p: (0, off + p))
    row64 = pl.BlockSpec((1, HEAD_DIM), lambda p: (0, 0))
    tri = pl.BlockSpec((T, T), lambda p: (0, 0))
    n_sem = n_comm * (N_CHIPS - 1)
    out = pl.pallas_call(
        body, name="attn_bwd_scatter" if n_comm else "attn_bwd", grid=(pairs,),
        in_specs=[blk(q_blk), blk(k_blk), row64, row64,
                  pl.BlockSpec((None, S, ATTN_BLOCK), lambda p: (p, 0, 0)), blk(0), tri, tri,
                  pl.BlockSpec(memory_space=pltpu.SMEM)]
        + [pl.BlockSpec((ATTN_HEADS_PER_STEP, S, HEAD_DIM), lambda p: (p, 0, 0))] * 3 + [ANY] * n_comm,
        out_specs=[blk(0), blk(0), blk(0), row64, row64] + [ANY] * n_comm,
        out_shape=[jax.ShapeDtypeStruct((S, ATTN_WIDTH), BF16)] * 3 + [jax.ShapeDtypeStruct((1, HEAD_DIM), F32)] * 2
        + _scatter_shapes(parts, kinds),
        scratch_shapes=[pltpu.VMEM((ATTN_HEADS_PER_STEP, S, HEAD_DIM), F32)] * 3
        + ([pltpu.SemaphoreType.DMA((n_sem,))] * 2 if n_comm else []),
        compiler_params=pltpu.CompilerParams(dimension_semantics=("arbitrary",), vmem_limit_bytes=VMEM_LIMIT_BYTES,
                                             has_side_effects=bool(n_comm)),
    )(proj, proj, qg, kg, ltot[0], dmix, prefix_incl, prefix_excl, *ltot[1:], *parts)
    return out[:5], list(out[5:])


def _shifted(win, n_rows):
    return [win if b == 0 else pltpu.roll(win, n_rows - b, 0) for b in range(SUBLANES)]


def _taps(variants, offsets, tm):
    return {o: variants[o % SUBLANES][(o // SUBLANES) * SUBLANES:(o // SUBLANES) * SUBLANES + tm, :] for o in offsets}


def _fold_rows(a):
    return jnp.sum(a.reshape(a.shape[0] // SUBLANES, SUBLANES, a.shape[1]), axis=0)


def _glu_conv_fwd(proj, w, bias, guest=None, *, tm=256):
    S = proj.shape[0]
    CB = LANES
    a_blk, b_blk = 3 * ATTN_WIDTH // CB, (3 * ATTN_WIDTH + CONV_WIDTH) // CB
    n_rows = tm + CONV_PAD

    def body(a_ref, b_ref, w_ref, bias_ref, c1_ref, pad_s):
        pad_s[0:CONV_PAD, :] = jnp.zeros((CONV_PAD, CB), F32)

        def fill(i, carry):
            rows = pl.ds(pl.multiple_of(i * tm, tm), tm)
            pad_s[pl.ds(pl.multiple_of(CONV_PAD + i * tm, SUBLANES), tm), :] = a_ref[rows, :] * _sigmoid(b_ref[rows, :])
            return carry

        lax.fori_loop(0, S // tm, fill, 0)

        def conv(i, carry):
            r0 = pl.multiple_of(i * tm, tm)
            taps = _taps(_shifted(pad_s[pl.ds(r0, n_rows), :], n_rows), range(2, 2 + CONV_KERNEL), tm)
            acc = jnp.broadcast_to(bias_ref[...], (tm, CB))
            for k in range(CONV_KERNEL):
                acc = acc + w_ref[k:k + 1, :] * taps[k + 2]
            c1_ref[pl.ds(r0, tm), :] = acc
            return carry

        lax.fori_loop(0, S // tm, conv, 0)

    (c1,), guest_out = _hosted_call(
        body, guest, name="glu_conv_fwd", grid=(CONV_WIDTH // CB,),
        in_specs=[pl.BlockSpec((S, CB), lambda j: (0, a_blk + j)), pl.BlockSpec((S, CB), lambda j: (0, b_blk + j)),
                  pl.BlockSpec((CONV_KERNEL, CB), lambda j: (0, j)), pl.BlockSpec((1, CB), lambda j: (0, j))],
        out_specs=[pl.BlockSpec((S, CB), lambda j: (0, j))],
        out_shape=[jax.ShapeDtypeStruct((S, CONV_WIDTH), F32)],
        scratch_shapes=[pltpu.VMEM((S + CONV_PAD, CB), F32)], operands=[proj, proj, w, bias])
    return c1, guest_out


def _glu_conv_bwd(proj, w, dc1, guest=None, *, tm=256):
    S = proj.shape[0]
    CB = LANES
    a_blk, b_blk = 3 * ATTN_WIDTH // CB, (3 * ATTN_WIDTH + CONV_WIDTH) // CB
    n_rows = tm + CONV_PAD

    def body(a_ref, b_ref, w_ref, dc1_ref, da_ref, db_ref, dw_ref, dbias_ref, pad_s, dpad_s, dw_s):
        pad_s[0:CONV_PAD, :] = jnp.zeros((CONV_PAD, CB), F32)
        dpad_s[S:S + CONV_PAD, :] = jnp.zeros((CONV_PAD, CB), F32)
        dw_s[...] = jnp.zeros_like(dw_s)

        def fill(i, carry):
            rows = pl.ds(pl.multiple_of(i * tm, tm), tm)
            pad_s[pl.ds(pl.multiple_of(CONV_PAD + i * tm, SUBLANES), tm), :] = a_ref[rows, :] * _sigmoid(b_ref[rows, :])
            dpad_s[rows, :] = dc1_ref[rows, :]
            return carry

        lax.fori_loop(0, S // tm, fill, 0)

        def conv(i, carry):
            r0 = pl.multiple_of(i * tm, tm)
            rows = pl.ds(r0, tm)
            taps = _taps(_shifted(dpad_s[pl.ds(r0, n_rows), :], n_rows), range(CONV_KERNEL), tm)
            acc = jnp.zeros((tm, CB), F32)
            for k in range(CONV_KERNEL):
                acc = acc + w_ref[k:k + 1, :] * taps[CONV_KERNEL - 1 - k]
            a = a_ref[rows, :]
            sg = _sigmoid(b_ref[rows, :])
            da_ref[rows, :] = (acc * sg).astype(BF16)
            db_ref[rows, :] = (acc * a * sg * (1.0 - sg)).astype(BF16)
            d = taps[0]
            taps = _taps(_shifted(pad_s[pl.ds(r0, n_rows), :], n_rows), range(2, 2 + CONV_KERNEL), tm)
            for k in range(CONV_KERNEL):
                dw_s[SUBLANES * k:SUBLANES * (k + 1), :] += _fold_rows(d * taps[k + 2])
            dw_s[SUBLANES * CONV_KERNEL:SUBLANES * (CONV_KERNEL + 1), :] += _fold_rows(d)
            return carry

        lax.fori_loop(0, S // tm, conv, 0)
        for k in range(CONV_KERNEL):
            dw_ref[k:k + 1, :] = jnp.sum(dw_s[SUBLANES * k:SUBLANES * (k + 1), :], axis=0, keepdims=True)
        dbias_ref[...] = jnp.sum(dw_s[SUBLANES * CONV_KERNEL:SUBLANES * (CONV_KERNEL + 1), :], axis=0, keepdims=True)

    col = lambda off: pl.BlockSpec((S, CB), lambda j: (0, off + j))
    out, guest_out = _hosted_call(
        body, guest, name="glu_conv_bwd", grid=(CONV_WIDTH // CB,),
        in_specs=[col(a_blk), col(b_blk), pl.BlockSpec((CONV_KERNEL, CB), lambda j: (0, j)), col(0)],
        out_specs=[col(0), col(0), pl.BlockSpec((CONV_KERNEL, CB), lambda j: (0, j)), pl.BlockSpec((1, CB), lambda j: (0, j))],
        out_shape=[jax.ShapeDtypeStruct((S, CONV_WIDTH), BF16)] * 2
        + [jax.ShapeDtypeStruct((CONV_KERNEL, CONV_WIDTH), F32), jax.ShapeDtypeStruct((1, CONV_WIDTH), F32)],
        scratch_shapes=[pltpu.VMEM((S + CONV_PAD, CB), F32), pltpu.VMEM((S + CONV_PAD, CB), F32),
                        pltpu.VMEM((SUBLANES * (CONV_KERNEL + 1), CB), F32)], operands=[proj, proj, w, dc1])
    return (*out, guest_out)


def _ln_stats(c1):
    mu = jnp.mean(c1, axis=-1, keepdims=True)
    xc = c1 - mu
    r = lax.rsqrt(jnp.mean(xc * xc, axis=-1, keepdims=True) + EPS)
    return xc * r, r


def _ln_silu_fwd(c1, g, b, *, tm=512):
    S, C = c1.shape

    def body(c1_ref, g_ref, b_ref, c_ref, ct_ref):
        yh, _ = _ln_stats(c1_ref[...])
        y = yh * g_ref[...] + b_ref[...]
        c = y * _sigmoid(y)
        c_ref[...] = c.astype(BF16)
        ct_ref[...] = c.T.astype(BF16)

    vec = pl.BlockSpec((1, C), lambda i: (0, 0))
    return pl.pallas_call(
        body, name="ln_silu_fwd", grid=(S // tm,),
        in_specs=[pl.BlockSpec((tm, C), lambda i: (i, 0)), vec, vec],
        out_specs=[pl.BlockSpec((tm, C), lambda i: (i, 0)), pl.BlockSpec((C, tm), lambda i: (0, i))],
        out_shape=[jax.ShapeDtypeStruct((S, C), BF16), jax.ShapeDtypeStruct((C, S), BF16)],
        compiler_params=_params("parallel"),
    )(c1, g, b)


def _ln_silu_bwd(c1, g, b, dmix, *, tm=512):
    S, C = c1.shape

    def body(c1_ref, g_ref, b_ref, dc_ref, dc1_ref, dg_ref, db_ref):
        yh, r = _ln_stats(c1_ref[...])
        y = yh * g_ref[...] + b_ref[...]
        sg = _sigmoid(y)
        dy = dc_ref[...] * (sg * (1.0 + y * (1.0 - sg)))
        dyh = dy * g_ref[...]
        dc1_ref[...] = r * (dyh - jnp.mean(dyh, axis=-1, keepdims=True)
                            - yh * jnp.mean(dyh * yh, axis=-1, keepdims=True))

        @pl.when(pl.program_id(0) == 0)
        def _():
            dg_ref[...] = jnp.zeros_like(dg_ref)
            db_ref[...] = jnp.zeros_like(db_ref)

        dg_ref[...] += jnp.sum(dy * yh, axis=0, keepdims=True)
        db_ref[...] += jnp.sum(dy, axis=0, keepdims=True)

    vec = pl.BlockSpec((1, C), lambda i: (0, 0))
    return pl.pallas_call(
        body, name="ln_silu_bwd", grid=(S // tm,),
        in_specs=[pl.BlockSpec((tm, C), lambda i: (i, 0)), vec, vec, pl.BlockSpec((tm, C), lambda i: (i, 1))],
        out_specs=[pl.BlockSpec((tm, C), lambda i: (i, 0)), vec, vec],
        out_shape=[jax.ShapeDtypeStruct((S, C), F32), jax.ShapeDtypeStruct((1, C), F32), jax.ShapeDtypeStruct((1, C), F32)],
        compiler_params=_params("arbitrary"),
    )(c1, g, b, dmix)


FFN_CB = 256


def _ffn_gate(pad_s, w_ref, bias_ref, r0, tm):
    n_rows = tm + FFN_PAD
    taps = _taps(_shifted(pad_s[pl.ds(r0, n_rows), :], n_rows), range(FFN_PAD - 2, FFN_PAD + 1), tm)
    g1 = bias_ref[...] + w_ref[0:1, :] * taps[6] + w_ref[1:2, :] * taps[7] + w_ref[2:3, :] * taps[8]
    return g1, taps


def _ffn_act_fwd(u, w, bias, guest=None, *, tm=256):
    S = u.shape[0]
    CB = FFN_CB
    nb = D_FF // CB

    def body(g_ref, v_ref, w_ref, bias_ref, o_ref, pad_s):
        pad_s[0:FFN_PAD, :] = jnp.zeros((FFN_PAD, CB), F32)

        def fill(i, carry):
            pad_s[pl.ds(pl.multiple_of(FFN_PAD + i * tm, SUBLANES), tm), :] = g_ref[pl.ds(pl.multiple_of(i * tm, tm), tm), :].astype(F32)
            return carry

        lax.fori_loop(0, S // tm, fill, 0)

        def act(i, carry):
            r0 = pl.multiple_of(i * tm, tm)
            g1, _ = _ffn_gate(pad_s, w_ref, bias_ref, r0, tm)
            o_ref[pl.ds(r0, tm), :] = (g1 * _sigmoid(g1) * v_ref[pl.ds(r0, tm), :].astype(F32)).astype(BF16)
            return carry

        lax.fori_loop(0, S // tm, act, 0)

    (act,), guest_out = _hosted_call(
        body, guest, name="ffn_act_fwd", grid=(nb,),
        in_specs=[pl.BlockSpec((S, CB), lambda j: (0, j)), pl.BlockSpec((S, CB), lambda j: (0, nb + j)),
                  pl.BlockSpec((FFN_KERNEL, CB), lambda j: (0, j)), pl.BlockSpec((1, CB), lambda j: (0, j))],
        out_specs=[pl.BlockSpec((S, CB), lambda j: (0, j))],
        out_shape=[jax.ShapeDtypeStruct((S, D_FF), BF16)],
        scratch_shapes=[pltpu.VMEM((S + FFN_PAD, CB), F32)], operands=[u, u, w, bias])
    return act, guest_out


def _ffn_act_bwd(u, w, bias, dact, guest=None, *, tm=256):
    S = u.shape[0]
    CB = FFN_CB
    nb = D_FF // CB

    def body(g_ref, v_ref, w_ref, bias_ref, da_ref, dg_ref, dv_ref, dw_ref, dbias_ref, pad_s, dpad_s, dw_s):
        pad_s[0:FFN_PAD, :] = jnp.zeros((FFN_PAD, CB), F32)
        dpad_s[S:S + FFN_PAD, :] = jnp.zeros((FFN_PAD, CB), F32)
        dw_s[...] = jnp.zeros_like(dw_s)

        def fill(i, carry):
            pad_s[pl.ds(pl.multiple_of(FFN_PAD + i * tm, SUBLANES), tm), :] = g_ref[pl.ds(pl.multiple_of(i * tm, tm), tm), :].astype(F32)
            return carry

        lax.fori_loop(0, S // tm, fill, 0)

        def first(i, carry):
            r0 = pl.multiple_of(i * tm, tm)
            rows = pl.ds(r0, tm)
            g1, taps = _ffn_gate(pad_s, w_ref, bias_ref, r0, tm)
            sg = _sigmoid(g1)
            da = da_ref[rows, :].astype(F32)
            dv_ref[rows, :] = (da * g1 * sg).astype(BF16)
            dg1 = da * v_ref[rows, :].astype(F32) * (sg * (1.0 + g1 * (1.0 - sg)))
            dpad_s[rows, :] = dg1
            for k in range(FFN_KERNEL):
                dw_s[SUBLANES * k:SUBLANES * (k + 1), :] += _fold_rows(dg1 * taps[FFN_PAD - 2 + k])
            dw_s[SUBLANES * FFN_KERNEL:SUBLANES * (FFN_KERNEL + 1), :] += _fold_rows(dg1)
            return carry

        lax.fori_loop(0, S // tm, first, 0)

        def second(i, carry):
            r0 = pl.multiple_of(i * tm, tm)
            n_rows = tm + FFN_PAD
            taps = _taps(_shifted(dpad_s[pl.ds(r0, n_rows), :], n_rows), range(FFN_KERNEL), tm)
            dg_ref[pl.ds(r0, tm), :] = (w_ref[2:3, :] * taps[0] + w_ref[1:2, :] * taps[1] + w_ref[0:1, :] * taps[2]).astype(BF16)
            return carry

        lax.fori_loop(0, S // tm, second, 0)
        for k in range(FFN_KERNEL):
            dw_ref[k:k + 1, :] = jnp.sum(dw_s[SUBLANES * k:SUBLANES * (k + 1), :], axis=0, keepdims=True)
        dbias_ref[...] = jnp.sum(dw_s[SUBLANES * FFN_KERNEL:SUBLANES * (FFN_KERNEL + 1), :], axis=0, keepdims=True)

    col = lambda off: pl.BlockSpec((S, CB), lambda j: (0, off + j))
    wspec = pl.BlockSpec((FFN_KERNEL, CB), lambda j: (0, j))
    bspec = pl.BlockSpec((1, CB), lambda j: (0, j))
    out, guest_out = _hosted_call(
        body, guest, name="ffn_act_bwd", grid=(nb,),
        in_specs=[col(0), col(nb), wspec, bspec, col(0)],
        out_specs=[col(0), col(0), wspec, bspec],
        out_shape=[jax.ShapeDtypeStruct((S, D_FF), BF16)] * 2
        + [jax.ShapeDtypeStruct((FFN_KERNEL, D_FF), F32), jax.ShapeDtypeStruct((1, D_FF), F32)],
        scratch_shapes=[pltpu.VMEM((S + FFN_PAD, CB), F32), pltpu.VMEM((S + FFN_PAD, CB), F32),
                        pltpu.VMEM((SUBLANES * (FFN_KERNEL + 1), CB), F32)], operands=[u, u, w, bias, dact])
    return (*out, guest_out)


def _loss_grad(y, target, *, tm=512):
    S, D = y.shape

    def body(y_ref, t_ref, dy_ref, l_ref):
        d = y_ref[...] - t_ref[...]
        dy_ref[...] = d * (1.0 / D)

        @pl.when(pl.program_id(0) == 0)
        def _():
            l_ref[...] = jnp.zeros_like(l_ref)

        l_ref[...] += 0.5 * jnp.sum(jnp.mean(d * d, axis=-1, keepdims=True), axis=0, keepdims=True)

    dy, l = pl.pallas_call(
        body, name="loss_grad", grid=(S // tm,),
        in_specs=[pl.BlockSpec((tm, D), lambda i: (i, 0))] * 2,
        out_specs=[pl.BlockSpec((tm, D), lambda i: (i, 0)), pl.BlockSpec((SUBLANES, LANES), lambda i: (0, 0))],
        out_shape=[jax.ShapeDtypeStruct((S, D), F32), jax.ShapeDtypeStruct((SUBLANES, LANES), F32)],
        compiler_params=_params("arbitrary"),
    )(y, target)
    return dy, l[0, 0]


def _row_tile(rows, cap=512):
    t = min(rows, cap)
    while rows % t or t % SUBLANES:
        t -= 1
    return t


def _adam_update(w, g, m, v):
    m1 = ADAM_B1 * m + (1.0 - ADAM_B1) * g
    v1 = ADAM_B2 * v + (1.0 - ADAM_B2) * (g * g)
    m_hat = m1 / (1.0 - ADAM_B1 ** ADAM_STEP)
    v_hat = v1 / (1.0 - ADAM_B2 ** ADAM_STEP)
    return -ADAM_LR * (m_hat / (jnp.sqrt(v_hat) + ADAM_EPS) + ADAM_WD * w), m1, v1


def _adamw_small(w, g, m, v):
    def body(w_ref, g_ref, m_ref, v_ref, d_ref, nm_ref, nv_ref):
        d_ref[...], nm_ref[...], nv_ref[...] = _adam_update(w_ref[...], g_ref[...], m_ref[...], v_ref[...])

    vmem = pl.BlockSpec(memory_space=pltpu.VMEM)
    return pl.pallas_call(body, name="adamw_small", in_specs=[vmem] * 4, out_specs=[vmem] * 3,
                          out_shape=[jax.ShapeDtypeStruct(w.shape, F32)] * 3)(w, g, m, v)


def _adamw_stacked(w, gs, m, v):
    L, R, C = w.shape
    tr = _row_tile(R, 256)

    def body(w_ref, m_ref, v_ref, *rest):
        g_refs, (go_ref, d_ref, nm_ref, nv_ref) = rest[:L], rest[L:]
        for a, g_ref in enumerate(g_refs):
            @pl.when(pl.program_id(0) == a)
            def _():
                gv = g_ref[...]
                go_ref[...] = gv
                d_ref[...], nm_ref[...], nv_ref[...] = _adam_update(w_ref[...], gv, m_ref[...], v_ref[...])

    stacked = pl.BlockSpec((None, tr, C), lambda l, i: (l, i, 0))
    return pl.pallas_call(
        body, name="adamw_stacked", grid=(L, R // tr),
        in_specs=[stacked] * 3 + [pl.BlockSpec((tr, C), lambda l, i, a=a: (jnp.where(l == a, i, 0), 0)) for a in range(L)],
        out_specs=[stacked] * 4, out_shape=[jax.ShapeDtypeStruct((L, R, C), F32)] * 4,
        compiler_params=_params("parallel", "parallel"),
    )(w, m, v, *gs)


CAST_STEPS = 4


def _cast_into_full(ws, layers, kinds, chip):
    n = len(ws)

    def body(chip_ref, *refs):
        for w_ref, o_ref in zip(refs[:n], refs[n:]):
            o_ref[...] = w_ref[...].astype(BF16)

    in_specs, out_specs, out_shape = [], [], []
    for w, layer, kind in zip(ws, layers, kinds):
        _, R, C = w.shape
        tr = R // CAST_STEPS
        assert tr % 16 == 0
        in_specs.append(pl.BlockSpec((None, tr, C), lambda i, chip_ref, layer=layer: (layer, i, 0)))
        if kind == "col":
            out_shape.append(jax.ShapeDtypeStruct((1, R, N_CHIPS * C), BF16))
            out_specs.append(pl.BlockSpec((None, tr, C), lambda i, chip_ref: (0, i, chip_ref[0])))
        else:
            out_shape.append(jax.ShapeDtypeStruct((1, N_CHIPS * R, C), BF16))
            out_specs.append(pl.BlockSpec((None, tr, C), lambda i, chip_ref: (0, chip_ref[0] * CAST_STEPS + i, 0)))
    return pl.pallas_call(
        body, name="cast_into_full",
        grid_spec=pltpu.PrefetchScalarGridSpec(num_scalar_prefetch=1, grid=(CAST_STEPS,),
                                               in_specs=in_specs, out_specs=out_specs),
        out_shape=out_shape, compiler_params=_params("parallel"),
    )(chip, *ws)


ADD_HALF_STEPS = 8
ADD_PARTS_STEPS = 2


def _add_half(g4s, las, c):
    n = len(g4s)

    def body(c_ref, *refs):
        for g_ref, la_ref, o_ref in zip(refs[:n], refs[n:2 * n], refs[2 * n:]):
            o_ref[...] = (g_ref[...].astype(F32) + la_ref[...].astype(F32)).astype(BF16)

    g_specs, la_specs, out_shape = [], [], []
    for g4 in g4s:
        L, _, H, W = g4.shape
        th = L * H // ADD_HALF_STEPS
        per = H // th
        assert th % 16 == 0 and per * th == H
        g_specs.append(pl.BlockSpec((None, None, th, W), lambda i, c_ref, per=per: (i // per, c_ref[0], i % per, 0)))
        la_specs.append(pl.BlockSpec((None, th, W), lambda i, c_ref, per=per: (i // per, i % per, 0)))
        out_shape.append(jax.ShapeDtypeStruct((L, H, W), BF16))
    return pl.pallas_call(
        body, name="add_half",
        grid_spec=pltpu.PrefetchScalarGridSpec(num_scalar_prefetch=1, grid=(ADD_HALF_STEPS,),
                                               in_specs=g_specs + la_specs, out_specs=la_specs),
        out_shape=out_shape, compiler_params=_params("parallel"),
    )(c, *g4s, *las)


def _add_parts(ps, lbs, place, kinds):
    n = len(ps)

    def body(s_ref, *refs):
        for p_ref, lb_ref, o_ref in zip(refs[:n], refs[n:2 * n], refs[2 * n:]):
            acc = p_ref[...].astype(F32)
            for k in range(N_CHIPS - 1):
                acc = acc + lb_ref[k].astype(F32)
            o_ref[...] = acc

    p_specs, lb_specs, out_specs, out_shape = [], [], [], []
    for lb, kind in zip(lbs, kinds):
        _, L, H, C = lb.shape
        th = H // ADD_PARTS_STEPS
        assert L == 1 and th % 16 == 0
        if kind == "col":
            p_specs.append(pl.BlockSpec((None, th, C), lambda i, s_ref: (0, i, s_ref[0])))
        else:
            p_specs.append(pl.BlockSpec((None, None, th, C), lambda i, s_ref: (0, s_ref[0], i, 0)))
        lb_specs.append(pl.BlockSpec((N_CHIPS - 1, None, th, C), lambda i, s_ref: (0, 0, i, 0)))
        out_specs.append(pl.BlockSpec((None, None, th, C), lambda i, s_ref: (0, s_ref[1], i, 0)))
        out_shape.append(jax.ShapeDtypeStruct((L, 2, H, C), F32))
    return pl.pallas_call(
        body, name="add_parts",
        grid_spec=pltpu.PrefetchScalarGridSpec(num_scalar_prefetch=1, grid=(ADD_PARTS_STEPS,),
                                               in_specs=p_specs + lb_specs, out_specs=out_specs),
        out_shape=out_shape, compiler_params=_params("parallel"),
    )(place, *ps, *lbs)


ANY = pl.BlockSpec(memory_space=pl.ANY)


def _place():
    x, y, c = lax.axis_index("x"), lax.axis_index("y"), lax.axis_index("c")
    chips = [(1 - x, y), (x, 1 - y), (1 - x, 1 - y)]
    return x, y, c, chips


def _comm_call(body, name, ins, out_shape, n_remote, n_local, aliases=None):
    scratch = [pltpu.SemaphoreType.DMA((n_remote,)), pltpu.SemaphoreType.DMA((n_remote,))]
    if n_local:
        scratch.append(pltpu.SemaphoreType.DMA((n_local,)))
    return pl.pallas_call(
        body, name=name, in_specs=[ANY] * len(ins), out_specs=[ANY] * len(out_shape), out_shape=out_shape,
        scratch_shapes=scratch, input_output_aliases=aliases or {},
        compiler_params=pltpu.CompilerParams(has_side_effects=True),
    )(*ins)


def _remote(src, dst, send, recv, k, to):
    return pltpu.make_async_remote_copy(src_ref=src, dst_ref=dst, send_sem=send.at[k], recv_sem=recv.at[k],
                                        device_id=to, device_id_type=MESH)


def _gather_weights(fulls, kinds):
    n = len(fulls)
    out_shape = [jax.ShapeDtypeStruct(f.shape, f.dtype) for f in fulls]

    def body(*refs):
        outs, (send, recv) = refs[n:2 * n], refs[2 * n:]
        first = _gather_copies(outs, kinds, send, recv, 0)
        for cp in first:
            cp.start()
        for cp in first:
            cp.wait()
        passed = _pass_on_copies(outs, kinds, send, recv, len(first))
        for cp in passed:
            cp.start()
        for cp in passed:
            cp.wait()

    return _comm_call(body, "gather_weights", fulls, out_shape, 2 * n * (N_CHIPS - 1), 0, {a: a for a in range(n)})


def _window(ref, kind, s, h):
    if kind == "col":
        H, C = ref.shape[1] // 2, ref.shape[2] // N_CHIPS
        return ref.at[:, pl.ds(pl.multiple_of(h * H, 16), H), pl.ds(pl.multiple_of(s * C, LANES), C)]
    R = ref.shape[1] // N_CHIPS
    return ref.at[:, pl.ds(pl.multiple_of(s * R + h * (R // 2), 16), R // 2), :]


def _gather_copies(outs, kinds, send, recv, sem0):
    x, y, c, chips = _place()
    me = 2 * x + y
    return [_remote(_window(o, kind, me, c), _window(o, kind, me, c), send, recv, sem0 + a * (N_CHIPS - 1) + k, (*chip, c))
            for a, (o, kind) in enumerate(zip(outs, kinds)) for k, chip in enumerate(chips)]


def _pass_on_copies(outs, kinds, send, recv, sem0):
    x, y, c, chips = _place()
    cps = []
    for a, (o, kind) in enumerate(zip(outs, kinds)):
        for k, chip in enumerate(chips):
            landed = _window(o, kind, 2 * chip[0] + chip[1], c)
            cps.append(_remote(landed, landed, send, recv, sem0 + a * (N_CHIPS - 1) + k, (x, y, 1 - c)))
    return cps


def _gather_small(shards):
    n = len(shards)
    out_shape = [jax.ShapeDtypeStruct((N_CHIPS,) + s.shape, s.dtype) for s in shards]

    def body(*refs):
        srcs, outs, (send, recv, loc) = refs[:n], refs[n:2 * n], refs[2 * n:]
        x, y, c, chips = _place()
        me = 2 * x + y
        remote, local = [], []
        for a in range(n):
            local.append(pltpu.make_async_copy(srcs[a], outs[a].at[me], loc.at[a]))
            for k, chip in enumerate(chips):
                remote.append(_remote(srcs[a], outs[a].at[me], send, recv, a * (N_CHIPS - 1) + k, (*chip, c)))
        for cp in local + remote:
            cp.start()
        for cp in remote + local:
            cp.wait()

    return _comm_call(body, "gather_small", shards, out_shape, n * (N_CHIPS - 1), n)


def _exchange_halves(g4s):
    n = len(g4s)
    out_shape = [jax.ShapeDtypeStruct((g.shape[0],) + g.shape[2:], g.dtype) for g in g4s]

    def body(*refs):
        gs, las, (send, recv) = refs[:n], refs[n:2 * n], refs[2 * n:]
        x, y, c, _ = _place()
        cps = [_remote(gs[a].at[:, 1 - c], las[a], send, recv, a, (x, y, 1 - c)) for a in range(n)]
        for cp in cps:
            cp.start()
        for cp in cps:
            cp.wait()

    return _comm_call(body, "exchange_halves", g4s, out_shape, n, 0)


def _scatter_partials(ps, kinds):
    n = len(ps)

    def body(*refs):
        srcs, lbs, (send, recv) = refs[:n], refs[n:2 * n], refs[2 * n:]
        cps = _scatter_copies(srcs, lbs, kinds, send, recv)
        for cp in cps:
            cp.start()
        for cp in cps:
            cp.wait()

    return _comm_call(body, "scatter_partials", ps, _scatter_shapes(ps, kinds), n * (N_CHIPS - 1), 0)


def _scatter_shapes(ps, kinds):
    out_shape = []
    for p, kind in zip(ps, kinds):
        L, H, C = (p.shape[0], p.shape[1], p.shape[2] // N_CHIPS) if kind == "col" else (p.shape[0], p.shape[2], p.shape[3])
        out_shape.append(jax.ShapeDtypeStruct((N_CHIPS - 1, L, H, C), p.dtype))
    return out_shape


def _scatter_copies(srcs, lbs, kinds, send, recv, sem0=0):
    x, y, c, chips = _place()
    cps = []
    for a, (src, lb, kind) in enumerate(zip(srcs, lbs, kinds)):
        C = lb.shape[3]
        for k, chip in enumerate(chips):
            s = 2 * chip[0] + chip[1]
            part = src.at[:, :, pl.ds(pl.multiple_of(s * C, LANES), C)] if kind == "col" else src.at[:, s]
            cps.append(_remote(part, lb.at[k], send, recv, sem0 + a * (N_CHIPS - 1) + k, (*chip, c)))
    return cps


def _share_halves(g4s):
    n = len(g4s)
    out_shape = [jax.ShapeDtypeStruct(g.shape, g.dtype) for g in g4s]

    def body(*refs):
        outs, (send, recv) = refs[n:2 * n], refs[2 * n:]
        x, y, c, _ = _place()
        cps = [_remote(outs[a].at[:, c], outs[a].at[:, c], send, recv, a, (x, y, 1 - c)) for a in range(n)]
        for cp in cps:
            cp.start()
        for cp in cps:
            cp.wait()

    return _comm_call(body, "share_halves", g4s, out_shape, n, 0, {a: a for a in range(n)})


def _allreduce_small(part):
    N = part.shape[0]

    def body(p_ref, o_ref, buf, send, recv):
        x, y, c, _ = _place()
        me = 4 * x + 2 * y + c
        buf[me] = p_ref[...]
        cps = []
        for k in range(1, N_DEV):
            peer = (1 - x if k & 4 else x, 1 - y if k & 2 else y, 1 - c if k & 1 else c)
            cps.append(_remote(p_ref, buf.at[me], send, recv, k - 1, peer))
        for cp in cps:
            cp.start()
        for cp in cps:
            cp.wait()
        acc = buf[0]
        for i in range(1, N_DEV):
            acc = acc + buf[i]
        o_ref[...] = acc

    vmem = pl.BlockSpec(memory_space=pltpu.VMEM)
    return pl.pallas_call(
        body, name="allreduce_small", in_specs=[vmem], out_specs=vmem,
        out_shape=jax.ShapeDtypeStruct((N, LANES), F32),
        scratch_shapes=[pltpu.VMEM((N_DEV, N, LANES), F32), pltpu.SemaphoreType.DMA((N_DEV - 1,)),
                        pltpu.SemaphoreType.DMA((N_DEV - 1,))],
        compiler_params=pltpu.CompilerParams(has_side_effects=True, vmem_limit_bytes=VMEM_LIMIT_BYTES),
    )(part)


AFTER_ATTENTION = ("w_out", "w_up", "w_down")


def _layer_fwd(x, p, full=None, l=0):
    comm = full is not None

    def guest(first, second):
        first, second = ([k for k in ks if comm and k[0] < len(full)] for ks in (first, second))
        if not first + second:
            return None, []
        return _gather_guest([full[i][n] for i, n in first], [BIG_KIND[n] for _, n in first],
                             [full[i][n] for i, n in second], [BIG_KIND[n] for _, n in second]), first + second

    def done(keys, bufs):
        for (i, n), buf in zip(keys, bufs):
            full[i][n] = buf

    def weight(n):
        return full[l][n][0] if comm else p[n]

    in_sweep = ([(l, "w_up")] if l else [(l, "w_out"), (l, "w_up")]) if comm else []
    own_down = [] if l else [(l, "w_down")]
    g, keys = guest([], [(l, "w_out")] if l else [])
    h1t, proj, out = _norm_matmul(x, p["norm1_g"], weight("w_in"), g)
    done(keys, out)
    attn, ltot, attnt, out = _attn_fwd(proj, p["q_norm_g"], p["k_norm_g"],
                                       [full[i][n] for i, n in in_sweep], [BIG_KIND[n] for _, n in in_sweep])
    done(in_sweep, out)
    g, keys = guest([], in_sweep)
    c1, out = _glu_conv_fwd(proj, p["conv_dw_w"], p["conv_dw_b"], g)
    done(keys, out)
    c, ct = _ln_silu_fwd(c1, p["conv_ln_g"], p["conv_ln_b"])
    x_mid, _ = _matmul_res([attn, c], weight("w_out"), x)
    g, keys = guest(own_down + [(l + 1, "w_down")], [])
    h2t, u, out = _norm_matmul(x_mid, p["norm2_g"], weight("w_up"), g)
    done(keys, out)
    g, keys = guest([(l + 1, "w_in")], own_down + [(l + 1, "w_down")])
    act, out = _ffn_act_fwd(u, p["ffn_dw_w"], p["ffn_dw_b"], g)
    done(keys, out)
    g, keys = guest([(l + 1, "w_out")], [(l + 1, "w_in")])
    x_out, out = _matmul_res([act], weight("w_down"), x_mid, g)
    done(keys, out)
    saved = dict(x=x, h1t=h1t, proj=proj, attnt=attnt, ltot=ltot, c1=c1, ct=ct, x_mid=x_mid, h2t=h2t, u=u, act=act)
    return x_out, saved


def _chip_partials(gs, kinds, core):
    g4s = _halves_view(gs, kinds)
    return _add_halves(g4s, _exchange_halves(g4s), kinds, core)


def _halves_view(gs, kinds):
    return [g.reshape(1, 2, g.shape[0] // 2, g.shape[1]) if kind == "col"
            else g.reshape(N_CHIPS, 2, g.shape[0] // N_CHIPS // 2, g.shape[1]) for g, kind in zip(gs, kinds)]


def _add_halves(g4s, received, kinds, core):
    parts = _add_half(g4s, received, core)
    return [p if kind == "col" else p.reshape(1, N_CHIPS, p.shape[1], p.shape[2]) for p, kind in zip(parts, kinds)]


def _gather_guest(first, first_kinds, second, second_kinds):
    bufs = list(first) + list(second)
    n1 = len(first)

    def copies(ins, outs, send, recv, sem0=0):
        return (_gather_copies(outs[:n1], first_kinds, send, recv, sem0)
                + _pass_on_copies(outs[n1:], second_kinds, send, recv, sem0 + n1 * (N_CHIPS - 1)))

    return _Guest("gather", bufs, [jax.ShapeDtypeStruct(f.shape, f.dtype) for f in bufs],
                  {a: a for a in range(len(bufs))}, copies, len(bufs) * (N_CHIPS - 1))


def _scatter_guest(parts, kinds):
    return _Guest("scatter", list(parts), _scatter_shapes(parts, kinds), {},
                  lambda ins, outs, send, recv, sem0=0: _scatter_copies(ins, outs, kinds, send, recv, sem0),
                  len(parts) * (N_CHIPS - 1))


def _exchange_guest(g4s):
    def copies(ins, outs, send, recv, sem0=0):
        x, y, c, _ = _place()
        return [_remote(g.at[:, 1 - c], la, send, recv, sem0 + a, (x, y, 1 - c)) for a, (g, la) in enumerate(zip(ins, outs))]

    out_shape = [jax.ShapeDtypeStruct((g.shape[0],) + g.shape[2:], g.dtype) for g in g4s]
    return _Guest("exchange", list(g4s), out_shape, {}, copies, len(g4s))


def _share_guest(g4s):
    def copies(ins, outs, send, recv, sem0=0):
        x, y, c, _ = _place()
        return [_remote(o.at[:, c], o.at[:, c], send, recv, sem0 + a, (x, y, 1 - c)) for a, o in enumerate(outs)]

    n = len(g4s)
    return _Guest("share", list(g4s), [jax.ShapeDtypeStruct(g.shape, g.dtype) for g in g4s], {a: a for a in range(n)}, copies, n)


def _owned_sums(parts, landed, kinds, place):
    halves = _add_parts(parts, landed, place, kinds)
    return [g.reshape(2 * g.shape[2], g.shape[3]) for g in _share_halves(halves)]


def _layer_bwd(dx_out, s, p, comm=None):
    g = {}
    on = comm is not None
    core, place, pending = comm if on else (None, None, [])
    n_p = len(pending)
    finished = lambda arrays: [a.reshape(2 * a.shape[2], a.shape[3]) for a in arrays]

    if n_p:
        g4s_p = _halves_view(pending, ["col"] * n_p)
    dact, received = _matmul_nt([dx_out], p["w_down"], BF16, _exchange_guest(g4s_p) if n_p else None)
    if n_p:
        parts_p = _add_halves(g4s_p, received, ["col"] * n_p, core)
    g["w_down"] = _matmul_tn([s["act"]], [dx_out], tk=D_FF // 2, tn=D_MODEL, transposed=False)
    dgate, dval, g["ffn_dw_w"], g["ffn_dw_b"], landed_p = _ffn_act_bwd(
        s["u"], p["ffn_dw_w"], p["ffn_dw_b"], dact, _scatter_guest(parts_p, ["col"] * n_p) if n_p else None)
    guest = _share_guest(_add_parts(parts_p, landed_p, place, ["col"] * n_p)) if n_p else None
    dx_mid, g["norm2_g"], shared_p = _matmul_nt_rmsbwd([dgate, dval], p["w_up"], s["x_mid"], p["norm2_g"], dx_out, guest)
    g["w_up"] = _matmul_tn([s["h2t"]], [dgate, dval], tk=D_MODEL, tn=D_FF // 2)
    dmix, _ = _matmul_nt([dx_mid], p["w_out"], F32)
    g["w_out"] = _matmul_tn([s["attnt"], s["ct"]], [dx_mid], tk=ATTN_WIDTH, tn=D_MODEL)
    dc1, g["conv_ln_g"], g["conv_ln_b"] = _ln_silu_bwd(s["c1"], p["conv_ln_g"], p["conv_ln_b"], dmix)

    late = AFTER_ATTENTION
    parts, kinds = [], []
    if on:
        kinds = [BIG_KIND[n] for n in late]
        g4s = _halves_view([g[n] for n in late], kinds)
    da, db, g["conv_dw_w"], g["conv_dw_b"], received = _glu_conv_bwd(
        s["proj"], p["conv_dw_w"], dc1, _exchange_guest(g4s) if on else None)
    if on:
        parts = _add_halves(g4s, received, kinds, core)
    (dq, dk, dv, g["q_norm_g"], g["k_norm_g"]), landed = _attn_bwd(
        s["proj"], p["q_norm_g"], p["k_norm_g"], s["ltot"], dmix, parts, kinds)
    guest = _share_guest(_add_parts(parts, landed, place, kinds)) if on else None
    pieces = [dq, dk, dv, da, db]
    dx, g["norm1_g"], shared = _matmul_nt_rmsbwd(pieces, p["w_in"], s["x"], p["norm1_g"], dx_mid, guest)
    g["w_in"] = _matmul_tn([s["h1t"]], pieces, tk=D_MODEL, tn=ATTN_WIDTH)
    sums = {}
    if on:
        sums = dict(zip(late, finished(shared)), pending=finished(shared_p))
    return dx, g, sums


WEIGHTS = ("norm1_g", "w_in", "q_norm_g", "k_norm_g", "conv_dw_w", "conv_dw_b", "conv_ln_g", "conv_ln_b",
           "w_out", "norm2_g", "w_up", "ffn_dw_w", "ffn_dw_b", "w_down")
BIG = ("w_in", "w_out", "w_up", "w_down")
BIG_KIND = {"w_in": "col", "w_out": "row", "w_up": "col", "w_down": "row"}
SMALL_SHARDED = ("conv_dw_w", "ffn_dw_w")
REPLICATED = tuple(n for n in WEIGHTS if n not in BIG + SMALL_SHARDED)


def _pack(arrays):
    flat = jnp.concatenate([a.reshape(-1) for a in arrays])
    rows = -(-flat.shape[0] // (SUBLANES * LANES)) * SUBLANES
    return jnp.pad(flat, (0, rows * LANES - flat.shape[0])).reshape(rows, LANES)


def _unpack(packed, shapes):
    flat = packed.reshape(-1)
    out, off = [], 0
    for shape in shapes:
        size = 1
        for d in shape:
            size *= d
        out.append(flat[off:off + size].reshape(shape))
        off += size
    return out


def _unshard_last(stacked):
    n, L, K, C = stacked.shape
    return jnp.transpose(stacked, (1, 2, 0, 3)).reshape(L, K, n * C)


def kernel(x, norm1_g, w_in, q_norm_g, k_norm_g, conv_dw_w, conv_dw_b, conv_ln_g, conv_ln_b, w_out, norm2_g, w_up, ffn_dw_w, ffn_dw_b, w_down, loss_target, m_norm1_g, m_w_in, m_q_norm_g, m_k_norm_g, m_conv_dw_w, m_conv_dw_b, m_conv_ln_g, m_conv_ln_b, m_w_out, m_norm2_g, m_w_up, m_ffn_dw_w, m_ffn_dw_b, m_w_down, v_norm1_g, v_w_in, v_q_norm_g, v_k_norm_g, v_conv_dw_w, v_conv_dw_b, v_conv_ln_g, v_conv_ln_b, v_w_out, v_norm2_g, v_w_up, v_ffn_dw_w, v_ffn_dw_b, v_w_down):
    given = dict(locals())
    w = {n: given[n] for n in WEIGHTS}
    m = {n: given["m_" + n] for n in WEIGHTS}
    v = {n: given["v_" + n] for n in WEIGHTS}
    chip = 2 * lax.axis_index("x") + lax.axis_index("y")
    core = lax.axis_index("c")
    chip_arr = jnp.reshape(chip, (1,)).astype(jnp.int32)
    core_arr = jnp.reshape(core, (1,)).astype(jnp.int32)
    L = DEPTH

    place = jnp.concatenate([chip_arr, core_arr])

    first = _cast_into_full([w["w_in"]], [0], ["col"], chip_arr)
    rest = [(l, n) for l in range(L) for n in BIG if (l, n) != (0, "w_in")]
    cast = dict(zip(rest, _cast_into_full([w[n] for _, n in rest], [l for l, _ in rest],
                                          [BIG_KIND[n] for _, n in rest], chip_arr)))
    cast[0, "w_in"] = first[0]
    full = [{n: cast[l, n] for n in BIG} for l in range(L)]
    full[0]["w_in"] = _gather_weights([full[0]["w_in"]], ["col"])[0]
    small_full = {n: _unshard_last(stacked)
                  for n, stacked in zip(SMALL_SHARDED, _gather_small([w[n] for n in SMALL_SHARDED]))}
    params = []
    for l in range(L):
        p = {n: small_full[n][l] for n in SMALL_SHARDED}
        p.update({n: w[n][l][None] for n in REPLICATED})
        params.append(p)

    act = x[0]
    saved = []
    for l in range(L):
        act, s = _layer_fwd(act, params[l], full, l)
        saved.append(s)
    for l in range(L):
        params[l].update({n: full[l][n][0] for n in BIG})
    dx, loss_part = _loss_grad(act, loss_target[0])
    loss = lax.psum(loss_part, ("x", "y", "c"))

    grads = [None] * L
    summed = {}
    pending = []
    for l in reversed(range(L)):
        dx, grads[l], sums = _layer_bwd(dx, saved[l], params[l], (core_arr, place, pending))
        summed.update({(l, n): sums[n] for n in AFTER_ATTENTION})
        if pending:
            summed[l + 1, "w_in"] = sums["pending"][0]
        pending = [grads[l]["w_in"]]
    parts = _chip_partials(pending, ["col"], core_arr)
    summed[0, "w_in"] = _owned_sums(parts, _scatter_partials(parts, ["col"]), ["col"], place)[0]

    grad, delta, new_m, new_v = {}, {}, {}, {}
    for n in BIG:
        grad[n], delta[n], new_m[n], new_v[n] = _adamw_stacked(w[n], [summed[l, n] for l in range(L)], m[n], v[n])

    small = REPLICATED + SMALL_SHARDED
    small_grads = [jnp.stack([grads[l][n] for l in range(L)]) for n in small]
    small_sums = _unpack(_allreduce_small(_pack(small_grads)), [a.shape for a in small_grads])
    for n, g in zip(small, small_sums):
        if n in REPLICATED:
            grad[n] = g.reshape(w[n].shape)
        else:
            width = w[n].shape[2]
            grad[n] = lax.dynamic_slice_in_dim(g, chip * width, width, axis=2)
    for n in small:
        delta[n], new_m[n], new_v[n] = _adamw_small(w[n], grad[n], m[n], v[n])

    return (loss, dx[None], *[grad[n] for n in WEIGHTS], *[delta[n] for n in WEIGHTS],
            *[new_m[n] for n in WEIGHTS], *[new_v[n] for n in WEIGHTS])
```

```python
import jax
import jax.numpy as jnp
from jax import lax
from jax.experimental import pallas as pl
from jax.experimental.pallas import tpu as pltpu

F32 = jnp.float32
BF16 = jnp.bfloat16

DEPTH = 4
D_MODEL = 1024
HEADS = 8
HEAD_DIM = 64
ATTN_WIDTH = HEADS * HEAD_DIM
CONV_WIDTH = D_MODEL - ATTN_WIDTH
CONV_KERNEL = 31
D_FF = 2816
FFN_KERNEL = 3
EPS = 1e-6
ADAM_LR, ADAM_B1, ADAM_B2, ADAM_EPS, ADAM_WD, ADAM_STEP = 0.001, 0.9, 0.999, 1e-08, 0.01, 10

N_CHIPS = 4
N_DEV = 8
LANES = 128
SUBLANES = 8
VMEM_LIMIT_BYTES = 56 * 2**20
ATTN_TILE = 256
ATTN_HEADS_PER_STEP = 4
ATTN_BLOCK = ATTN_HEADS_PER_STEP * HEAD_DIM
CONV_PAD = 32
FFN_PAD = 8
MESH = pl.DeviceIdType.MESH


def _params(*sem):
    return pltpu.CompilerParams(dimension_semantics=sem if sem else None, vmem_limit_bytes=VMEM_LIMIT_BYTES)


class _Guest:
    def __init__(self, name, ins, out_shape, aliases, copies, n_sem):
        self.name, self.ins, self.out_shape, self.aliases, self.copies, self.n_sem = name, ins, out_shape, aliases, copies, n_sem


def _hosted_call(body, guest, *, name, grid, in_specs, out_specs, out_shape, scratch_shapes, operands):
    if guest is None:
        out = pl.pallas_call(body, name=name, grid=grid, in_specs=in_specs, out_specs=out_specs, out_shape=out_shape,
                             scratch_shapes=scratch_shapes, compiler_params=_params("arbitrary"))(*operands)
        return list(out), []
    n_in, n_out, n_scr = len(in_specs), len(out_specs), len(scratch_shapes)
    gi, go = len(guest.ins), len(guest.out_shape)

    def hosting(*refs):
        ins, g_in = refs[:n_in], refs[n_in:n_in + gi]
        outs, g_out = refs[n_in + gi:n_in + gi + n_out], refs[n_in + gi + n_out:n_in + gi + n_out + go]
        scratch, (send, recv) = refs[n_in + gi + n_out + go:-2], refs[-2:]

        @pl.when(pl.program_id(0) == 0)
        def _():
            for cp in guest.copies(g_in, g_out, send, recv):
                cp.start()

        body(*ins, *outs, *scratch)

        @pl.when(pl.program_id(0) == grid[0] - 1)
        def _():
            for cp in guest.copies(g_in, g_out, send, recv):
                cp.wait()

    out = pl.pallas_call(
        hosting, name=name + "_" + guest.name, grid=grid,
        in_specs=list(in_specs) + [ANY] * gi, out_specs=list(out_specs) + [ANY] * go,
        out_shape=list(out_shape) + list(guest.out_shape),
        scratch_shapes=list(scratch_shapes) + [pltpu.SemaphoreType.DMA((guest.n_sem,))] * 2,
        input_output_aliases={n_in + a: n_out + b for a, b in guest.aliases.items()},
        compiler_params=pltpu.CompilerParams(dimension_semantics=("arbitrary",), vmem_limit_bytes=VMEM_LIMIT_BYTES,
                                             has_side_effects=True),
    )(*operands, *guest.ins)
    return list(out[:n_out]), list(out[n_out:])


def _dot(a, b):
    return jnp.dot(a, b, preferred_element_type=F32)


def _dot_nt(a, b):
    return lax.dot_general(a, b, (((1,), (1,)), ((), ())), preferred_element_type=F32)


def _dot_tn(a, b):
    return lax.dot_general(a, b, (((0,), (0,)), ((), ())), preferred_element_type=F32)


def _sigmoid(x):
    return 1.0 / (1.0 + jnp.exp(-x))


def _norm_matmul(x, g, w, guest=None, *, out_dtype=F32, tm=256):
    S, D = x.shape
    N = w.shape[1]

    def body(x_ref, g_ref, w_ref, ht_ref, y_ref):
        xv = x_ref[...]
        h = xv * lax.rsqrt(jnp.mean(xv * xv, axis=-1, keepdims=True) + EPS) * g_ref[...]
        ht_ref[...] = h.T.astype(BF16)
        y_ref[...] = _dot(h.astype(BF16), w_ref[...]).astype(out_dtype)

    (ht, y), guest_out = _hosted_call(
        body, guest, name="norm_matmul", grid=(S // tm,),
        in_specs=[pl.BlockSpec((tm, D), lambda i: (i, 0)),
                  pl.BlockSpec((1, D), lambda i: (0, 0)),
                  pl.BlockSpec((D, N), lambda i: (0, 0))],
        out_specs=[pl.BlockSpec((D, tm), lambda i: (0, i)),
                   pl.BlockSpec((tm, N), lambda i: (i, 0))],
        out_shape=[jax.ShapeDtypeStruct((D, S), BF16), jax.ShapeDtypeStruct((S, N), out_dtype)],
        scratch_shapes=[], operands=[x, g, w])
    return ht, y, guest_out


def _matmul_res(pieces, w, res, guest=None, *, tm=512):
    S, N = res.shape
    K = w.shape[0]
    widths = [p.shape[1] for p in pieces]
    assert sum(widths) == K

    def body(*refs):
        p_refs, (w_ref, res_ref, o_ref) = refs[:len(pieces)], refs[len(pieces):]
        acc = res_ref[...]
        off = 0
        for p_ref, kp in zip(p_refs, widths):
            acc = acc + _dot(p_ref[...], w_ref[off:off + kp, :])
            off += kp
        o_ref[...] = acc

    (out,), guest_out = _hosted_call(
        body, guest, name="matmul_res", grid=(S // tm,),
        in_specs=[pl.BlockSpec((tm, kp), lambda i: (i, 0)) for kp in widths]
        + [pl.BlockSpec((K, N), lambda i: (0, 0)), pl.BlockSpec((tm, N), lambda i: (i, 0))],
        out_specs=[pl.BlockSpec((tm, N), lambda i: (i, 0))],
        out_shape=[jax.ShapeDtypeStruct((S, N), F32)],
        scratch_shapes=[], operands=[*pieces, w, res])
    return out, guest_out


def _nt_sum(p_refs, widths, w_ref):
    acc = None
    off = 0
    for p_ref, n_p in zip(p_refs, widths):
        d = _dot_nt(p_ref[...].astype(BF16), w_ref[:, off:off + n_p])
        acc = d if acc is None else acc + d
        off += n_p
    return acc


def _matmul_nt(pieces, w, out_dtype, guest=None, *, tm=512):
    S = pieces[0].shape[0]
    K, N = w.shape
    widths = [p.shape[1] for p in pieces]
    assert sum(widths) == N

    def body(*refs):
        p_refs, (w_ref, o_ref) = refs[:len(pieces)], refs[len(pieces):]
        o_ref[...] = _nt_sum(p_refs, widths, w_ref).astype(out_dtype)

    (out,), guest_out = _hosted_call(
        body, guest, name="matmul_nt", grid=(S // tm,),
        in_specs=[pl.BlockSpec((tm, n_p), lambda i: (i, 0)) for n_p in widths]
        + [pl.BlockSpec((K, N), lambda i: (0, 0))],
        out_specs=[pl.BlockSpec((tm, K), lambda i: (i, 0))],
        out_shape=[jax.ShapeDtypeStruct((S, K), out_dtype)],
        scratch_shapes=[], operands=[*pieces, w])
    return out, guest_out


def _matmul_nt_rmsbwd(pieces, w, x, g, dres, guest=None, *, tm=256):
    S, K = x.shape
    N = w.shape[1]
    widths = [p.shape[1] for p in pieces]
    assert sum(widths) == N

    def body(*refs):
        p_refs, (w_ref, x_ref, g_ref, dres_ref, dx_ref, dg_ref) = refs[:len(pieces)], refs[len(pieces):]
        dh = _nt_sum(p_refs, widths, w_ref)
        xv = x_ref[...]
        r = lax.rsqrt(jnp.mean(xv * xv, axis=-1, keepdims=True) + EPS)
        xh = xv * r
        dxh = dh * g_ref[...]
        dx_ref[...] = dres_ref[...] + r * (dxh - xh * jnp.mean(dxh * xh, axis=-1, keepdims=True))

        @pl.when(pl.program_id(0) == 0)
        def _():
            dg_ref[...] = jnp.zeros_like(dg_ref)

        dg_ref[...] += jnp.sum(dh * xh, axis=0, keepdims=True)

    (dx, dg), guest_out = _hosted_call(
        body, guest, name="matmul_nt_rmsbwd", grid=(S // tm,),
        in_specs=[pl.BlockSpec((tm, n_p), lambda i: (i, 0)) for n_p in widths]
        + [pl.BlockSpec((K, N), lambda i: (0, 0)), pl.BlockSpec((tm, K), lambda i: (i, 0)),
           pl.BlockSpec((1, K), lambda i: (0, 0)), pl.BlockSpec((tm, K), lambda i: (i, 0))],
        out_specs=[pl.BlockSpec((tm, K), lambda i: (i, 0)), pl.BlockSpec((1, K), lambda i: (0, 0))],
        out_shape=[jax.ShapeDtypeStruct((S, K), F32), jax.ShapeDtypeStruct((1, K), F32)],
        scratch_shapes=[], operands=[*pieces, w, x, g, dres])
    return dx, dg, guest_out


def _matmul_tn(xts, dys, *, tk, tn, ts=2048, transposed=True):
    S = dys[0].shape[0]
    xs = xts
    n_s = S // ts
    (mt,) = {x.shape[0 if transposed else 1] // tk for x in xts}
    (nt,) = {d.shape[1] // tn for d in dys}
    product = _dot if transposed else _dot_tn

    def body(*refs):
        x_refs, dy_refs, (o_ref, acc_ref) = refs[:len(xs)], refs[len(xs):len(xs) + len(dys)], refs[len(xs) + len(dys):]
        i, j, s = pl.program_id(0), pl.program_id(1), pl.program_id(2)

        @pl.when(s == 0)
        def _():
            acc_ref[...] = jnp.zeros_like(acc_ref)

        for a, x_ref in enumerate(x_refs):
            for b, dy_ref in enumerate(dy_refs):
                @pl.when((i // mt == a) & (j // nt == b))
                def _():
                    acc_ref[...] += product(x_ref[...], dy_ref[...].astype(BF16))

        @pl.when(s == n_s - 1)
        def _():
            o_ref[...] = acc_ref[...].astype(BF16)

    def x_map(a):
        if transposed:
            return lambda i, j, s: (jnp.where(i // mt == a, i % mt, 0), jnp.where(i // mt == a, s, 0))
        return lambda i, j, s: (jnp.where(i // mt == a, s, 0), jnp.where(i // mt == a, i % mt, 0))

    def dy_map(b):
        return lambda i, j, s: (jnp.where(j // nt == b, s, 0), jnp.where(j // nt == b, j % nt, 0))

    return pl.pallas_call(
        body, name="matmul_tn", grid=(len(xs) * mt, len(dys) * nt, n_s),
        in_specs=[pl.BlockSpec((tk, ts) if transposed else (ts, tk), x_map(a)) for a in range(len(xs))]
        + [pl.BlockSpec((ts, tn), dy_map(b)) for b in range(len(dys))],
        out_specs=pl.BlockSpec((tk, tn), lambda i, j, s: (i, j)),
        out_shape=jax.ShapeDtypeStruct((len(xs) * mt * tk, len(dys) * nt * tn), BF16),
        scratch_shapes=[pltpu.VMEM((tk, tn), F32)],
        compiler_params=_params("parallel", "parallel", "arbitrary"),
    )(*xs, *dys)


def _tri_consts():
    j = jnp.arange(ATTN_TILE)[:, None]
    s = jnp.arange(ATTN_TILE)[None, :]
    return (j > s).astype(BF16), (j <= s).astype(BF16), (j < s).astype(BF16)


SIGN_BIT = 0x80000000
WEIGHT_IS_ZERO = -104.0


def _log_terms(sn):
    minus_abs = lax.bitcast_convert_type(lax.bitcast_convert_type(sn, jnp.uint32) | jnp.uint32(SIGN_BIT), F32)
    lom = jnp.minimum(sn, 0.0) - jnp.log(1.0 + jnp.exp(minus_abs))
    return lom, lom - sn


def _causal_mask():
    t = lax.broadcasted_iota(jnp.int32, (ATTN_TILE, ATTN_TILE), 0)
    s = lax.broadcasted_iota(jnp.int32, (ATTN_TILE, ATTN_TILE), 1)
    return s < t


def _attn_prep(h, q_ref, k_ref, v_ref, qg_ref, kg_ref, qn_s, kn_s, vb_s, n_tiles):
    T = ATTN_TILE
    lanes = slice(HEAD_DIM * h, HEAD_DIM * (h + 1))
    scale = -(HEAD_DIM ** -0.5)

    def prep(i, carry):
        rows = pl.ds(pl.multiple_of(i * T, T), T)
        q = q_ref[rows, lanes]
        k = k_ref[rows, lanes]
        rq = lax.rsqrt(jnp.mean(q * q, axis=-1, keepdims=True) + EPS)
        rk = lax.rsqrt(jnp.mean(k * k, axis=-1, keepdims=True) + EPS)
        qn_s[h, rows, :] = (q * rq * qg_ref[...] * scale).astype(BF16)
        kn_s[h, rows, :] = (k * rk * kg_ref[...]).astype(BF16)
        vb_s[h, rows, :] = v_ref[rows, lanes].astype(BF16)
        return carry

    lax.fori_loop(0, n_tiles, prep, 0)


def _attn_fwd(proj, qg, kg, fulls=(), kinds=()):
    S = proj.shape[0]
    n_comm = len(fulls)
    T = ATTN_TILE
    n_tiles = S // T
    suffix, _, _ = _tri_consts()
    pairs = HEADS // ATTN_HEADS_PER_STEP
    q_blk, k_blk, v_blk = 0, ATTN_WIDTH // ATTN_BLOCK, 2 * ATTN_WIDTH // ATTN_BLOCK

    def body(*refs):
        q_ref, k_ref, v_ref, qg_ref, kg_ref, tri_ref = refs[:6]
        o_ref, lt_ref, ot_ref, start_ref, qn_s, kn_s, vb_s = refs[6 + n_comm:13 + n_comm]
        w_refs = refs[13 + n_comm:13 + 2 * n_comm]
        sems = refs[13 + 2 * n_comm:]
        step_id = pl.program_id(0)
        if n_comm:
            @pl.when(pl.program_id(0) == 0)
            def _():
                for cp in _gather_copies(w_refs, kinds, *sems, 0):
                    cp.start()

        heads = range(ATTN_HEADS_PER_STEP)
        for h in heads:
            _attn_prep(h, q_ref, k_ref, v_ref, qg_ref, kg_ref, qn_s, kn_s, vb_s, n_tiles)

        def q_tile(qi, carry0):
            qrows = pl.ds(pl.multiple_of(qi * T, T), T)
            qt = [qn_s[h, qrows, :] for h in heads]

            def tile(kj, carry, diag):
                krows = pl.ds(pl.multiple_of(kj * T, T), T)
                s = [_dot_nt(qt[h], kn_s[h, krows, :]) for h in heads]
                terms = [_log_terms(s[h]) for h in heads]
                lom = [terms[h][0] for h in heads]
                if diag:
                    mask = _causal_mask()
                    lom = [jnp.where(mask, lom[h], 0.0) for h in heads]
                lom = [lom[h].astype(BF16) for h in heads]
                tail = [_dot(lom[h], tri_ref[...]) for h in heads]
                w = [jnp.exp(terms[h][1] + tail[h] + carry[h][1]) for h in heads]
                if diag:
                    w = [jnp.where(mask, w[h], 0.0) for h in heads]
                return tuple((carry[h][0] + _dot(w[h].astype(BF16), vb_s[h, krows, :]),
                              carry[h][1] + (tail[h][:, 0:1] + lom[h][:, 0:1].astype(F32))) for h in heads)

            init = tuple((jnp.zeros((T, HEAD_DIM), F32), jnp.zeros((T, 1), F32)) for h in heads)
            def alive(cr):
                worst = cr[0][1]
                for h in heads[1:]:
                    worst = jnp.maximum(worst, cr[h][1])
                return jnp.max(worst) >= WEIGHT_IS_ZERO

            def step(state):
                t, _, cr = state
                cr = tile(qi - 1 - t, cr, False)
                return t + 1, alive(cr), cr

            first = tile(qi, init, True)
            swept, _, carry = lax.while_loop(lambda st: (st[0] < qi) & st[1], step, (jnp.int32(0), alive(first), first))
            start_ref[step_id, qi] = (qi - swept).astype(F32)
            o = jnp.concatenate([carry[h][0] for h in heads], axis=1)
            o_ref[qrows, :] = o.astype(BF16)
            ot_ref[:, qrows] = o.T.astype(BF16)
            for h in heads:
                lt_ref[qrows, HEAD_DIM * h:HEAD_DIM * (h + 1)] = jnp.broadcast_to(carry[h][1], (T, HEAD_DIM))
            return carry0

        lax.fori_loop(0, n_tiles, q_tile, 0)
        if n_comm:
            @pl.when(pl.program_id(0) == pairs - 1)
            def _():
                for cp in _gather_copies(w_refs, kinds, *sems, 0):
                    cp.wait()

    n_sem = n_comm * (N_CHIPS - 1)
    per_head = pl.BlockSpec((ATTN_HEADS_PER_STEP, S, HEAD_DIM), lambda p: (p, 0, 0))
    out = pl.pallas_call(
        body, name="attn_fwd_gather" if n_comm else "attn_fwd", grid=(pairs,),
        in_specs=[pl.BlockSpec((S, ATTN_BLOCK), lambda p: (0, q_blk + p)),
                  pl.BlockSpec((S, ATTN_BLOCK), lambda p: (0, k_blk + p)),
                  pl.BlockSpec((S, ATTN_BLOCK), lambda p: (0, v_blk + p)),
                  pl.BlockSpec((1, HEAD_DIM), lambda p: (0, 0)),
                  pl.BlockSpec((1, HEAD_DIM), lambda p: (0, 0)),
                  pl.BlockSpec((T, T), lambda p: (0, 0))] + [ANY] * n_comm,
        out_specs=[pl.BlockSpec((S, ATTN_BLOCK), lambda p: (0, p)),
                   pl.BlockSpec((None, S, ATTN_BLOCK), lambda p: (p, 0, 0)),
                   pl.BlockSpec((ATTN_BLOCK, S), lambda p: (p, 0)),
                   pl.BlockSpec(memory_space=pltpu.SMEM)] + [per_head] * 3 + [ANY] * n_comm,
        out_shape=[jax.ShapeDtypeStruct((S, ATTN_WIDTH), BF16),
                   jax.ShapeDtypeStruct((pairs, S, ATTN_BLOCK), F32),
                   jax.ShapeDtypeStruct((ATTN_WIDTH, S), BF16),
                   jax.ShapeDtypeStruct((pairs, n_tiles), F32)] + [jax.ShapeDtypeStruct((HEADS, S, HEAD_DIM), BF16)] * 3
        + [jax.ShapeDtypeStruct(f.shape, f.dtype) for f in fulls],
        scratch_shapes=[pltpu.SemaphoreType.DMA((n_sem,))] * 2 if n_comm else [],
        input_output_aliases={6 + a: 7 + a for a in range(n_comm)},
        compiler_params=pltpu.CompilerParams(dimension_semantics=("arbitrary",), vmem_limit_bytes=VMEM_LIMIT_BYTES,
                                             has_side_effects=bool(n_comm)),
    )(proj, proj, proj, qg, kg, suffix, *fulls)
    return out[0], (out[1], out[3], out[4], out[5], out[6]), out[2], list(out[7:])


def _attn_bwd(proj, qg, kg, ltot, dmix, parts=(), kinds=()):
    S = proj.shape[0]
    n_comm = len(parts)
    T = ATTN_TILE
    n_tiles = S // T
    _, prefix_incl, prefix_excl = _tri_consts()
    pairs = HEADS // ATTN_HEADS_PER_STEP
    q_blk, k_blk, v_blk = 0, ATTN_WIDTH // ATTN_BLOCK, 2 * ATTN_WIDTH // ATTN_BLOCK
    scale = HEAD_DIM ** -0.5

    def body(*refs):
        q_ref, k_ref, qg_ref, kg_ref, lt_ref, do_ref, ti_ref, te_ref, start_ref, qn_s, kn_s, vb_s = refs[:12]
        p_refs = refs[12:12 + n_comm]
        dq_ref, dk_ref, dv_ref, dqg_ref, dkg_ref = refs[12 + n_comm:17 + n_comm]
        lb_refs = refs[17 + n_comm:17 + 2 * n_comm]
        dq_s, dk_s, dv_s = refs[17 + 2 * n_comm:20 + 2 * n_comm]
        sems = refs[20 + 2 * n_comm:]
        step_id = pl.program_id(0)

        @pl.when(pl.program_id(0) == 0)
        def _():
            dqg_ref[...] = jnp.zeros_like(dqg_ref)
            dkg_ref[...] = jnp.zeros_like(dkg_ref)
            for cp in _scatter_copies(p_refs, lb_refs, kinds, *sems) if n_comm else ():
                cp.start()

        heads = range(ATTN_HEADS_PER_STEP)
        dk_s[...] = jnp.zeros_like(dk_s)
        dv_s[...] = jnp.zeros_like(dv_s)

        def q_tile(qi, carry0):
            qrows = pl.ds(pl.multiple_of(qi * T, T), T)
            qt = [qn_s[h, qrows, :] for h in heads]
            dob = [do_ref[qrows, HEAD_DIM * h:HEAD_DIM * (h + 1)].astype(BF16) for h in heads]
            lt = [lt_ref[qrows, HEAD_DIM * h:HEAD_DIM * h + 1] for h in heads]

            def tile(kj, carry, diag):
                krows = pl.ds(pl.multiple_of(kj * T, T), T)
                kt = [kn_s[h, krows, :] for h in heads]
                s = [_dot_nt(qt[h], kt[h]) for h in heads]
                dw = [_dot_nt(dob[h], vb_s[h, krows, :]) for h in heads]
                terms = [_log_terms(s[h]) for h in heads]
                lom = [terms[h][0] for h in heads]
                if diag:
                    mask = _causal_mask()
                    lom = [jnp.where(mask, lom[h], 0.0) for h in heads]
                pin = [_dot(lom[h].astype(BF16), ti_ref[...]) for h in heads]
                w = [jnp.exp(terms[h][1] + ((lt[h] - carry[h][1]) - pin[h])) for h in heads]
                if diag:
                    w = [jnp.where(mask, w[h], 0.0) for h in heads]
                e = [w[h] * dw[h] for h in heads]
                gex = [_dot(e[h].astype(BF16), te_ref[...]) for h in heads]
                dz = [e[h] - jnp.exp(terms[h][1]) * (e[h] + (carry[h][2] + gex[h])) for h in heads]
                if diag:
                    dz = [jnp.where(mask, dz[h], 0.0) for h in heads]
                new = []
                for h in heads:
                    dzb = dz[h].astype(BF16)
                    dk_s[h, krows, :] += _dot_tn(dzb, qt[h])
                    dv_s[h, krows, :] += _dot_tn(w[h].astype(BF16), dob[h])
                    new.append((carry[h][0] + _dot(dzb, kt[h]),
                                carry[h][1] + pin[h][:, T - 1:T],
                                carry[h][2] + gex[h][:, T - 1:T] + e[h][:, T - 1:T]))
                return tuple(new)

            zero = jnp.zeros((T, 1), F32)
            init = tuple((jnp.zeros((T, HEAD_DIM), F32), zero, zero) for h in heads)
            first = jnp.clip(start_ref[step_id, qi].astype(jnp.int32), 0, qi)
            last = tile(qi, lax.fori_loop(first, qi, lambda kj, cr: tile(kj, cr, False), init), True)
            for h in heads:
                dq_s[h, qrows, :] = last[h][0]
            return carry0

        lax.fori_loop(0, n_tiles, q_tile, 0)

        def finish(i, carry):
            rows = pl.ds(pl.multiple_of(i * T, T), T)
            new = []
            for h in heads:
                lanes = slice(HEAD_DIM * h, HEAD_DIM * (h + 1))
                q = q_ref[rows, lanes]
                k = k_ref[rows, lanes]
                rq = lax.rsqrt(jnp.mean(q * q, axis=-1, keepdims=True) + EPS)
                rk = lax.rsqrt(jnp.mean(k * k, axis=-1, keepdims=True) + EPS)
                qh = q * rq
                kh = k * rk
                dqn = dq_s[h, rows, :] * scale
                dkn = -dk_s[h, rows, :]
                dqh = dqn * qg_ref[...]
                dkh = dkn * kg_ref[...]
                dq_ref[rows, lanes] = (rq * (dqh - qh * jnp.mean(dqh * qh, axis=-1, keepdims=True))).astype(BF16)
                dk_ref[rows, lanes] = (rk * (dkh - kh * jnp.mean(dkh * kh, axis=-1, keepdims=True))).astype(BF16)
                dv_ref[rows, lanes] = dv_s[h, rows, :].astype(BF16)
                new.append(carry[2 * h] + jnp.sum(dqn * qh, axis=0, keepdims=True))
                new.append(carry[2 * h + 1] + jnp.sum(dkn * kh, axis=0, keepdims=True))
            return tuple(new)

        zero = jnp.zeros((1, HEAD_DIM), F32)
        sums = lax.fori_loop(0, n_tiles, finish, (zero,) * (2 * len(heads)))
        dqg_ref[...] += sum(sums[0::2])
        dkg_ref[...] += sum(sums[1::2])
        if n_comm:
            @pl.when(pl.program_id(0) == pairs - 1)
            def _():
                for cp in _scatter_copies(p_refs, lb_refs, kinds, *sems):
                    cp.wait()

    blk = lambda off: pl.BlockSpec((S, ATTN_BLOCK), lambda p: (0, off + p))
    row64 = pl.BlockSpec((1, HEAD_DIM), lambda p: (0, 0))
    tri = pl.BlockSpec((T, T), lambda p: (0, 0))
    n_sem = n_comm * (N_CHIPS - 1)
    out = pl.pallas_call(
        body, name="attn_bwd_scatter" if n_comm else "attn_bwd", grid=(pairs,),
        in_specs=[blk(q_blk), blk(k_blk), row64, row64,
                  pl.BlockSpec((None, S, ATTN_BLOCK), lambda p: (p, 0, 0)), blk(0), tri, tri,
                  pl.BlockSpec(memory_space=pltpu.SMEM)]
        + [pl.BlockSpec((ATTN_HEADS_PER_STEP, S, HEAD_DIM), lambda p: (p, 0, 0))] * 3 + [ANY] * n_comm,
        out_specs=[blk(0), blk(0), blk(0), row64, row64] + [ANY] * n_comm,
        out_shape=[jax.ShapeDtypeStruct((S, ATTN_WIDTH), BF16)] * 3 + [jax.ShapeDtypeStruct((1, HEAD_DIM), F32)] * 2
        + _scatter_shapes(parts, kinds),
        scratch_shapes=[pltpu.VMEM((ATTN_HEADS_PER_STEP, S, HEAD_DIM), F32)] * 3
        + ([pltpu.SemaphoreType.DMA((n_sem,))] * 2 if n_comm else []),
        compiler_params=pltpu.CompilerParams(dimension_semantics=("arbitrary",), vmem_limit_bytes=VMEM_LIMIT_BYTES,
                                             has_side_effects=bool(n_comm)),
    )(proj, proj, qg, kg, ltot[0], dmix, prefix_incl, prefix_excl, *ltot[1:], *parts)
    return out[:5], list(out[5:])


def _shifted(win, n_rows):
    return [win if b == 0 else pltpu.roll(win, n_rows - b, 0) for b in range(SUBLANES)]


def _taps(variants, offsets, tm):
    return {o: variants[o % SUBLANES][(o // SUBLANES) * SUBLANES:(o // SUBLANES) * SUBLANES + tm, :] for o in offsets}


def _fold_rows(a):
    return jnp.sum(a.reshape(a.shape[0] // SUBLANES, SUBLANES, a.shape[1]), axis=0)


def _glu_conv_fwd(proj, w, bias, guest=None, *, tm=256):
    S = proj.shape[0]
    CB = LANES
    a_blk, b_blk = 3 * ATTN_WIDTH // CB, (3 * ATTN_WIDTH + CONV_WIDTH) // CB
    n_rows = tm + CONV_PAD

    def body(a_ref, b_ref, w_ref, bias_ref, c1_ref, pad_s):
        pad_s[0:CONV_PAD, :] = jnp.zeros((CONV_PAD, CB), F32)

        def fill(i, carry):
            rows = pl.ds(pl.multiple_of(i * tm, tm), tm)
            pad_s[pl.ds(pl.multiple_of(CONV_PAD + i * tm, SUBLANES), tm), :] = a_ref[rows, :] * _sigmoid(b_ref[rows, :])
            return carry

        lax.fori_loop(0, S // tm, fill, 0)

        def conv(i, carry):
            r0 = pl.multiple_of(i * tm, tm)
            taps = _taps(_shifted(pad_s[pl.ds(r0, n_rows), :], n_rows), range(2, 2 + CONV_KERNEL), tm)
            acc = jnp.broadcast_to(bias_ref[...], (tm, CB))
            for k in range(CONV_KERNEL):
                acc = acc + w_ref[k:k + 1, :] * taps[k + 2]
            c1_ref[pl.ds(r0, tm), :] = acc
            return carry

        lax.fori_loop(0, S // tm, conv, 0)

    (c1,), guest_out = _hosted_call(
        body, guest, name="glu_conv_fwd", grid=(CONV_WIDTH // CB,),
        in_specs=[pl.BlockSpec((S, CB), lambda j: (0, a_blk + j)), pl.BlockSpec((S, CB), lambda j: (0, b_blk + j)),
                  pl.BlockSpec((CONV_KERNEL, CB), lambda j: (0, j)), pl.BlockSpec((1, CB), lambda j: (0, j))],
        out_specs=[pl.BlockSpec((S, CB), lambda j: (0, j))],
        out_shape=[jax.ShapeDtypeStruct((S, CONV_WIDTH), F32)],
        scratch_shapes=[pltpu.VMEM((S + CONV_PAD, CB), F32)], operands=[proj, proj, w, bias])
    return c1, guest_out


def _glu_conv_bwd(proj, w, dc1, guest=None, *, tm=256):
    S = proj.shape[0]
    CB = LANES
    a_blk, b_blk = 3 * ATTN_WIDTH // CB, (3 * ATTN_WIDTH + CONV_WIDTH) // CB
    n_rows = tm + CONV_PAD

    def body(a_ref, b_ref, w_ref, dc1_ref, da_ref, db_ref, dw_ref, dbias_ref, pad_s, dpad_s, dw_s):
        pad_s[0:CONV_PAD, :] = jnp.zeros((CONV_PAD, CB), F32)
        dpad_s[S:S + CONV_PAD, :] = jnp.zeros((CONV_PAD, CB), F32)
        dw_s[...] = jnp.zeros_like(dw_s)

        def fill(i, carry):
            rows = pl.ds(pl.multiple_of(i * tm, tm), tm)
            pad_s[pl.ds(pl.multiple_of(CONV_PAD + i * tm, SUBLANES), tm), :] = a_ref[rows, :] * _sigmoid(b_ref[rows, :])
            dpad_s[rows, :] = dc1_ref[rows, :]
            return carry

        lax.fori_loop(0, S // tm, fill, 0)

        def conv(i, carry):
            r0 = pl.multiple_of(i * tm, tm)
            rows = pl.ds(r0, tm)
            taps = _taps(_shifted(dpad_s[pl.ds(r0, n_rows), :], n_rows), range(CONV_KERNEL), tm)
            acc = jnp.zeros((tm, CB), F32)
            for k in range(CONV_KERNEL):
                acc = acc + w_ref[k:k + 1, :] * taps[CONV_KERNEL - 1 - k]
            a = a_ref[rows, :]
            sg = _sigmoid(b_ref[rows, :])
            da_ref[rows, :] = (acc * sg).astype(BF16)
            db_ref[rows, :] = (acc * a * sg * (1.0 - sg)).astype(BF16)
            d = taps[0]
            taps = _taps(_shifted(pad_s[pl.ds(r0, n_rows), :], n_rows), range(2, 2 + CONV_KERNEL), tm)
            for k in range(CONV_KERNEL):
                dw_s[SUBLANES * k:SUBLANES * (k + 1), :] += _fold_rows(d * taps[k + 2])
            dw_s[SUBLANES * CONV_KERNEL:SUBLANES * (CONV_KERNEL + 1), :] += _fold_rows(d)
            return carry

        lax.fori_loop(0, S // tm, conv, 0)
        for k in range(CONV_KERNEL):
            dw_ref[k:k + 1, :] = jnp.sum(dw_s[SUBLANES * k:SUBLANES * (k + 1), :], axis=0, keepdims=True)
        dbias_ref[...] = jnp.sum(dw_s[SUBLANES * CONV_KERNEL:SUBLANES * (CONV_KERNEL + 1), :], axis=0, keepdims=True)

    col = lambda off: pl.BlockSpec((S, CB), lambda j: (0, off + j))
    out, guest_out = _hosted_call(
        body, guest, name="glu_conv_bwd", grid=(CONV_WIDTH // CB,),
        in_specs=[col(a_blk), col(b_blk), pl.BlockSpec((CONV_KERNEL, CB), lambda j: (0, j)), col(0)],
        out_specs=[col(0), col(0), pl.BlockSpec((CONV_KERNEL, CB), lambda j: (0, j)), pl.BlockSpec((1, CB), lambda j: (0, j))],
        out_shape=[jax.ShapeDtypeStruct((S, CONV_WIDTH), BF16)] * 2
        + [jax.ShapeDtypeStruct((CONV_KERNEL, CONV_WIDTH), F32), jax.ShapeDtypeStruct((1, CONV_WIDTH), F32)],
        scratch_shapes=[pltpu.VMEM((S + CONV_PAD, CB), F32), pltpu.VMEM((S + CONV_PAD, CB), F32),
                        pltpu.VMEM((SUBLANES * (CONV_KERNEL + 1), CB), F32)], operands=[proj, proj, w, dc1])
    return (*out, guest_out)


def _ln_stats(c1):
    mu = jnp.mean(c1, axis=-1, keepdims=True)
    xc = c1 - mu
    r = lax.rsqrt(jnp.mean(xc * xc, axis=-1, keepdims=True) + EPS)
    return xc * r, r


def _ln_silu_fwd(c1, g, b, *, tm=512):
    S, C = c1.shape

    def body(c1_ref, g_ref, b_ref, c_ref, ct_ref):
        yh, _ = _ln_stats(c1_ref[...])
        y = yh * g_ref[...] + b_ref[...]
        c = y * _sigmoid(y)
        c_ref[...] = c.astype(BF16)
        ct_ref[...] = c.T.astype(BF16)

    vec = pl.BlockSpec((1, C), lambda i: (0, 0))
    return pl.pallas_call(
        body, name="ln_silu_fwd", grid=(S // tm,),
        in_specs=[pl.BlockSpec((tm, C), lambda i: (i, 0)), vec, vec],
        out_specs=[pl.BlockSpec((tm, C), lambda i: (i, 0)), pl.BlockSpec((C, tm), lambda i: (0, i))],
        out_shape=[jax.ShapeDtypeStruct((S, C), BF16), jax.ShapeDtypeStruct((C, S), BF16)],
        compiler_params=_params("parallel"),
    )(c1, g, b)


def _ln_silu_bwd(c1, g, b, dmix, *, tm=512):
    S, C = c1.shape

    def body(c1_ref, g_ref, b_ref, dc_ref, dc1_ref, dg_ref, db_ref):
        yh, r = _ln_stats(c1_ref[...])
        y = yh * g_ref[...] + b_ref[...]
        sg = _sigmoid(y)
        dy = dc_ref[...] * (sg * (1.0 + y * (1.0 - sg)))
        dyh = dy * g_ref[...]
        dc1_ref[...] = r * (dyh - jnp.mean(dyh, axis=-1, keepdims=True)
                            - yh * jnp.mean(dyh * yh, axis=-1, keepdims=True))

        @pl.when(pl.program_id(0) == 0)
        def _():
            dg_ref[...] = jnp.zeros_like(dg_ref)
            db_ref[...] = jnp.zeros_like(db_ref)

        dg_ref[...] += jnp.sum(dy * yh, axis=0, keepdims=True)
        db_ref[...] += jnp.sum(dy, axis=0, keepdims=True)

    vec = pl.BlockSpec((1, C), lambda i: (0, 0))
    return pl.pallas_call(
        body, name="ln_silu_bwd", grid=(S // tm,),
        in_specs=[pl.BlockSpec((tm, C), lambda i: (i, 0)), vec, vec, pl.BlockSpec((tm, C), lambda i: (i, 1))],
        out_specs=[pl.BlockSpec((tm, C), lambda i: (i, 0)), vec, vec],
        out_shape=[jax.ShapeDtypeStruct((S, C), F32), jax.ShapeDtypeStruct((1, C), F32), jax.ShapeDtypeStruct((1, C), F32)],
        compiler_params=_params("arbitrary"),
    )(c1, g, b, dmix)


FFN_CB = 256


def _ffn_gate(pad_s, w_ref, bias_ref, r0, tm):
    n_rows = tm + FFN_PAD
    taps = _taps(_shifted(pad_s[pl.ds(r0, n_rows), :], n_rows), range(FFN_PAD - 2, FFN_PAD + 1), tm)
    g1 = bias_ref[...] + w_ref[0:1, :] * taps[6] + w_ref[1:2, :] * taps[7] + w_ref[2:3, :] * taps[8]
    return g1, taps


def _ffn_act_fwd(u, w, bias, guest=None, *, tm=256):
    S = u.shape[0]
    CB = FFN_CB
    nb = D_FF // CB

    def body(g_ref, v_ref, w_ref, bias_ref, o_ref, pad_s):
        pad_s[0:FFN_PAD, :] = jnp.zeros((FFN_PAD, CB), F32)

        def fill(i, carry):
            pad_s[pl.ds(pl.multiple_of(FFN_PAD + i * tm, SUBLANES), tm), :] = g_ref[pl.ds(pl.multiple_of(i * tm, tm), tm), :].astype(F32)
            return carry

        lax.fori_loop(0, S // tm, fill, 0)

        def act(i, carry):
            r0 = pl.multiple_of(i * tm, tm)
            g1, _ = _ffn_gate(pad_s, w_ref, bias_ref, r0, tm)
            o_ref[pl.ds(r0, tm), :] = (g1 * _sigmoid(g1) * v_ref[pl.ds(r0, tm), :].astype(F32)).astype(BF16)
            return carry

        lax.fori_loop(0, S // tm, act, 0)

    (act,), guest_out = _hosted_call(
        body, guest, name="ffn_act_fwd", grid=(nb,),
        in_specs=[pl.BlockSpec((S, CB), lambda j: (0, j)), pl.BlockSpec((S, CB), lambda j: (0, nb + j)),
                  pl.BlockSpec((FFN_KERNEL, CB), lambda j: (0, j)), pl.BlockSpec((1, CB), lambda j: (0, j))],
        out_specs=[pl.BlockSpec((S, CB), lambda j: (0, j))],
        out_shape=[jax.ShapeDtypeStruct((S, D_FF), BF16)],
        scratch_shapes=[pltpu.VMEM((S + FFN_PAD, CB), F32)], operands=[u, u, w, bias])
    return act, guest_out


def _ffn_act_bwd(u, w, bias, dact, guest=None, *, tm=256):
    S = u.shape[0]
    CB = FFN_CB
    nb = D_FF // CB

    def body(g_ref, v_ref, w_ref, bias_ref, da_ref, dg_ref, dv_ref, dw_ref, dbias_ref, pad_s, dpad_s, dw_s):
        pad_s[0:FFN_PAD, :] = jnp.zeros((FFN_PAD, CB), F32)
        dpad_s[S:S + FFN_PAD, :] = jnp.zeros((FFN_PAD, CB), F32)
        dw_s[...] = jnp.zeros_like(dw_s)

        def fill(i, carry):
            pad_s[pl.ds(pl.multiple_of(FFN_PAD + i * tm, SUBLANES), tm), :] = g_ref[pl.ds(pl.multiple_of(i * tm, tm), tm), :].astype(F32)
            return carry

        lax.fori_loop(0, S // tm, fill, 0)

        def first(i, carry):
            r0 = pl.multiple_of(i * tm, tm)
            rows = pl.ds(r0, tm)
            g1, taps = _ffn_gate(pad_s, w_ref, bias_ref, r0, tm)
            sg = _sigmoid(g1)
            da = da_ref[rows, :].astype(F32)
            dv_ref[rows, :] = (da * g1 * sg).astype(BF16)
            dg1 = da * v_ref[rows, :].astype(F32) * (sg * (1.0 + g1 * (1.0 - sg)))
            dpad_s[rows, :] = dg1
            for k in range(FFN_KERNEL):
                dw_s[SUBLANES * k:SUBLANES * (k + 1), :] += _fold_rows(dg1 * taps[FFN_PAD - 2 + k])
            dw_s[SUBLANES * FFN_KERNEL:SUBLANES * (FFN_KERNEL + 1), :] += _fold_rows(dg1)
            return carry

        lax.fori_loop(0, S // tm, first, 0)

        def second(i, carry):
            r0 = pl.multiple_of(i * tm, tm)
            n_rows = tm + FFN_PAD
            taps = _taps(_shifted(dpad_s[pl.ds(r0, n_rows), :], n_rows), range(FFN_KERNEL), tm)
            dg_ref[pl.ds(r0, tm), :] = (w_ref[2:3, :] * taps[0] + w_ref[1:2, :] * taps[1] + w_ref[0:1, :] * taps[2]).astype(BF16)
            return carry

        lax.fori_loop(0, S // tm, second, 0)
        for k in range(FFN_KERNEL):
            dw_ref[k:k + 1, :] = jnp.sum(dw_s[SUBLANES * k:SUBLANES * (k + 1), :], axis=0, keepdims=True)
        dbias_ref[...] = jnp.sum(dw_s[SUBLANES * FFN_KERNEL:SUBLANES * (FFN_KERNEL + 1), :], axis=0, keepdims=True)

    col = lambda off: pl.BlockSpec((S, CB), lambda j: (0, off + j))
    wspec = pl.BlockSpec((FFN_KERNEL, CB), lambda j: (0, j))
    bspec = pl.BlockSpec((1, CB), lambda j: (0, j))
    out, guest_out = _hosted_call(
        body, guest, name="ffn_act_bwd", grid=(nb,),
        in_specs=[col(0), col(nb), wspec, bspec, col(0)],
        out_specs=[col(0), col(0), wspec, bspec],
        out_shape=[jax.ShapeDtypeStruct((S, D_FF), BF16)] * 2
        + [jax.ShapeDtypeStruct((FFN_KERNEL, D_FF), F32), jax.ShapeDtypeStruct((1, D_FF), F32)],
        scratch_shapes=[pltpu.VMEM((S + FFN_PAD, CB), F32), pltpu.VMEM((S + FFN_PAD, CB), F32),
                        pltpu.VMEM((SUBLANES * (FFN_KERNEL + 1), CB), F32)], operands=[u, u, w, bias, dact])
    return (*out, guest_out)


def _loss_grad(y, target, *, tm=512):
    S, D = y.shape

    def body(y_ref, t_ref, dy_ref, l_ref):
        d = y_ref[...] - t_ref[...]
        dy_ref[...] = d * (1.0 / D)

        @pl.when(pl.program_id(0) == 0)
        def _():
            l_ref[...] = jnp.zeros_like(l_ref)

        l_ref[...] += 0.5 * jnp.sum(jnp.mean(d * d, axis=-1, keepdims=True), axis=0, keepdims=True)

    dy, l = pl.pallas_call(
        body, name="loss_grad", grid=(S // tm,),
        in_specs=[pl.BlockSpec((tm, D), lambda i: (i, 0))] * 2,
        out_specs=[pl.BlockSpec((tm, D), lambda i: (i, 0)), pl.BlockSpec((SUBLANES, LANES), lambda i: (0, 0))],
        out_shape=[jax.ShapeDtypeStruct((S, D), F32), jax.ShapeDtypeStruct((SUBLANES, LANES), F32)],
        compiler_params=_params("arbitrary"),
    )(y, target)
    return dy, l[0, 0]


def _row_tile(rows, cap=512):
    t = min(rows, cap)
    while rows % t or t % SUBLANES:
        t -= 1
    return t


def _adam_update(w, g, m, v):
    m1 = ADAM_B1 * m + (1.0 - ADAM_B1) * g
    v1 = ADAM_B2 * v + (1.0 - ADAM_B2) * (g * g)
    m_hat = m1 / (1.0 - ADAM_B1 ** ADAM_STEP)
    v_hat = v1 / (1.0 - ADAM_B2 ** ADAM_STEP)
    return -ADAM_LR * (m_hat / (jnp.sqrt(v_hat) + ADAM_EPS) + ADAM_WD * w), m1, v1


def _adamw_small(w, g, m, v):
    def body(w_ref, g_ref, m_ref, v_ref, d_ref, nm_ref, nv_ref):
        d_ref[...], nm_ref[...], nv_ref[...] = _adam_update(w_ref[...], g_ref[...], m_ref[...], v_ref[...])

    vmem = pl.BlockSpec(memory_space=pltpu.VMEM)
    return pl.pallas_call(body, name="adamw_small", in_specs=[vmem] * 4, out_specs=[vmem] * 3,
                          out_shape=[jax.ShapeDtypeStruct(w.shape, F32)] * 3)(w, g, m, v)


def _adamw_stacked(w, gs, m, v):
    L, R, C = w.shape
    tr = _row_tile(R, 256)

    def body(w_ref, m_ref, v_ref, *rest):
        g_refs, (go_ref, d_ref, nm_ref, nv_ref) = rest[:L], rest[L:]
        for a, g_ref in enumerate(g_refs):
            @pl.when(pl.program_id(0) == a)
            def _():
                gv = g_ref[...]
                go_ref[...] = gv
                d_ref[...], nm_ref[...], nv_ref[...] = _adam_update(w_ref[...], gv, m_ref[...], v_ref[...])

    stacked = pl.BlockSpec((None, tr, C), lambda l, i: (l, i, 0))
    return pl.pallas_call(
        body, name="adamw_stacked", grid=(L, R // tr),
        in_specs=[stacked] * 3 + [pl.BlockSpec((tr, C), lambda l, i, a=a: (jnp.where(l == a, i, 0), 0)) for a in range(L)],
        out_specs=[stacked] * 4, out_shape=[jax.ShapeDtypeStruct((L, R, C), F32)] * 4,
        compiler_params=_params("parallel", "parallel"),
    )(w, m, v, *gs)


CAST_STEPS = 4


def _cast_into_full(ws, layers, kinds, chip):
    n = len(ws)

    def body(chip_ref, *refs):
        for w_ref, o_ref in zip(refs[:n], refs[n:]):
            o_ref[...] = w_ref[...].astype(BF16)

    in_specs, out_specs, out_shape = [], [], []
    for w, layer, kind in zip(ws, layers, kinds):
        _, R, C = w.shape
        tr = R // CAST_STEPS
        assert tr % 16 == 0
        in_specs.append(pl.BlockSpec((None, tr, C), lambda i, chip_ref, layer=layer: (layer, i, 0)))
        if kind == "col":
            out_shape.append(jax.ShapeDtypeStruct((1, R, N_CHIPS * C), BF16))
            out_specs.append(pl.BlockSpec((None, tr, C), lambda i, chip_ref: (0, i, chip_ref[0])))
        else:
            out_shape.append(jax.ShapeDtypeStruct((1, N_CHIPS * R, C), BF16))
            out_specs.append(pl.BlockSpec((None, tr, C), lambda i, chip_ref: (0, chip_ref[0] * CAST_STEPS + i, 0)))
    return pl.pallas_call(
        body, name="cast_into_full",
        grid_spec=pltpu.PrefetchScalarGridSpec(num_scalar_prefetch=1, grid=(CAST_STEPS,),
                                               in_specs=in_specs, out_specs=out_specs),
        out_shape=out_shape, compiler_params=_params("parallel"),
    )(chip, *ws)


ADD_HALF_STEPS = 8
ADD_PARTS_STEPS = 2


def _add_half(g4s, las, c):
    n = len(g4s)

    def body(c_ref, *refs):
        for g_ref, la_ref, o_ref in zip(refs[:n], refs[n:2 * n], refs[2 * n:]):
            o_ref[...] = (g_ref[...].astype(F32) + la_ref[...].astype(F32)).astype(BF16)

    g_specs, la_specs, out_shape = [], [], []
    for g4 in g4s:
        L, _, H, W = g4.shape
        th = L * H // ADD_HALF_STEPS
        per = H // th
        assert th % 16 == 0 and per * th == H
        g_specs.append(pl.BlockSpec((None, None, th, W), lambda i, c_ref, per=per: (i // per, c_ref[0], i % per, 0)))
        la_specs.append(pl.BlockSpec((None, th, W), lambda i, c_ref, per=per: (i // per, i % per, 0)))
        out_shape.append(jax.ShapeDtypeStruct((L, H, W), BF16))
    return pl.pallas_call(
        body, name="add_half",
        grid_spec=pltpu.PrefetchScalarGridSpec(num_scalar_prefetch=1, grid=(ADD_HALF_STEPS,),
                                               in_specs=g_specs + la_specs, out_specs=la_specs),
        out_shape=out_shape, compiler_params=_params("parallel"),
    )(c, *g4s, *las)


def _add_parts(ps, lbs, place, kinds):
    n = len(ps)

    def body(s_ref, *refs):
        for p_ref, lb_ref, o_ref in zip(refs[:n], refs[n:2 * n], refs[2 * n:]):
            acc = p_ref[...].astype(F32)
            for k in range(N_CHIPS - 1):
                acc = acc + lb_ref[k].astype(F32)
            o_ref[...] = acc

    p_specs, lb_specs, out_specs, out_shape = [], [], [], []
    for lb, kind in zip(lbs, kinds):
        _, L, H, C = lb.shape
        th = H // ADD_PARTS_STEPS
        assert L == 1 and th % 16 == 0
        if kind == "col":
            p_specs.append(pl.BlockSpec((None, th, C), lambda i, s_ref: (0, i, s_ref[0])))
        else:
            p_specs.append(pl.BlockSpec((None, None, th, C), lambda i, s_ref: (0, s_ref[0], i, 0)))
        lb_specs.append(pl.BlockSpec((N_CHIPS - 1, None, th, C), lambda i, s_ref: (0, 0, i, 0)))
        out_specs.append(pl.BlockSpec((None, None, th, C), lambda i, s_ref: (0, s_ref[1], i, 0)))
        out_shape.append(jax.ShapeDtypeStruct((L, 2, H, C), F32))
    return pl.pallas_call(
        body, name="add_parts",
        grid_spec=pltpu.PrefetchScalarGridSpec(num_scalar_prefetch=1, grid=(ADD_PARTS_STEPS,),
                                               in_specs=p_specs + lb_specs, out_specs=out_specs),
        out_shape=out_shape, compiler_params=_params("parallel"),
    )(place, *ps, *lbs)


ANY = pl.BlockSpec(memory_space=pl.ANY)


def _place():
    x, y, c = lax.axis_index("x"), lax.axis_index("y"), lax.axis_index("c")
    chips = [(1 - x, y), (x, 1 - y), (1 - x, 1 - y)]
    return x, y, c, chips


def _comm_call(body, name, ins, out_shape, n_remote, n_local, aliases=None):
    scratch = [pltpu.SemaphoreType.DMA((n_remote,)), pltpu.SemaphoreType.DMA((n_remote,))]
    if n_local:
        scratch.append(pltpu.SemaphoreType.DMA((n_local,)))
    return pl.pallas_call(
        body, name=name, in_specs=[ANY] * len(ins), out_specs=[ANY] * len(out_shape), out_shape=out_shape,
        scratch_shapes=scratch, input_output_aliases=aliases or {},
        compiler_params=pltpu.CompilerParams(has_side_effects=True),
    )(*ins)


def _remote(src, dst, send, recv, k, to):
    return pltpu.make_async_remote_copy(src_ref=src, dst_ref=dst, send_sem=send.at[k], recv_sem=recv.at[k],
                                        device_id=to, device_id_type=MESH)


def _gather_weights(fulls, kinds):
    n = len(fulls)
    out_shape = [jax.ShapeDtypeStruct(f.shape, f.dtype) for f in fulls]

    def body(*refs):
        outs, (send, recv) = refs[n:2 * n], refs[2 * n:]
        first = _gather_copies(outs, kinds, send, recv, 0)
        for cp in first:
            cp.start()
        for cp in first:
            cp.wait()
        passed = _pass_on_copies(outs, kinds, send, recv, len(first))
        for cp in passed:
            cp.start()
        for cp in passed:
            cp.wait()

    return _comm_call(body, "gather_weights", fulls, out_shape, 2 * n * (N_CHIPS - 1), 0, {a: a for a in range(n)})


def _window(ref, kind, s, h):
    if kind == "col":
        H, C = ref.shape[1] // 2, ref.shape[2] // N_CHIPS
        return ref.at[:, pl.ds(pl.multiple_of(h * H, 16), H), pl.ds(pl.multiple_of(s * C, LANES), C)]
    R = ref.shape[1] // N_CHIPS
    return ref.at[:, pl.ds(pl.multiple_of(s * R + h * (R // 2), 16), R // 2), :]


def _gather_copies(outs, kinds, send, recv, sem0):
    x, y, c, chips = _place()
    me = 2 * x + y
    return [_remote(_window(o, kind, me, c), _window(o, kind, me, c), send, recv, sem0 + a * (N_CHIPS - 1) + k, (*chip, c))
            for a, (o, kind) in enumerate(zip(outs, kinds)) for k, chip in enumerate(chips)]


def _pass_on_copies(outs, kinds, send, recv, sem0):
    x, y, c, chips = _place()
    cps = []
    for a, (o, kind) in enumerate(zip(outs, kinds)):
        for k, chip in enumerate(chips):
            landed = _window(o, kind, 2 * chip[0] + chip[1], c)
            cps.append(_remote(landed, landed, send, recv, sem0 + a * (N_CHIPS - 1) + k, (x, y, 1 - c)))
    return cps


def _gather_small(shards):
    n = len(shards)
    out_shape = [jax.ShapeDtypeStruct((N_CHIPS,) + s.shape, s.dtype) for s in shards]

    def body(*refs):
        srcs, outs, (send, recv, loc) = refs[:n], refs[n:2 * n], refs[2 * n:]
        x, y, c, chips = _place()
        me = 2 * x + y
        remote, local = [], []
        for a in range(n):
            local.append(pltpu.make_async_copy(srcs[a], outs[a].at[me], loc.at[a]))
            for k, chip in enumerate(chips):
                remote.append(_remote(srcs[a], outs[a].at[me], send, recv, a * (N_CHIPS - 1) + k, (*chip, c)))
        for cp in local + remote:
            cp.start()
        for cp in remote + local:
            cp.wait()

    return _comm_call(body, "gather_small", shards, out_shape, n * (N_CHIPS - 1), n)


def _exchange_halves(g4s):
    n = len(g4s)
    out_shape = [jax.ShapeDtypeStruct((g.shape[0],) + g.shape[2:], g.dtype) for g in g4s]

    def body(*refs):
        gs, las, (send, recv) = refs[:n], refs[n:2 * n], refs[2 * n:]
        x, y, c, _ = _place()
        cps = [_remote(gs[a].at[:, 1 - c], las[a], send, recv, a, (x, y, 1 - c)) for a in range(n)]
        for cp in cps:
            cp.start()
        for cp in cps:
            cp.wait()

    return _comm_call(body, "exchange_halves", g4s, out_shape, n, 0)


def _scatter_partials(ps, kinds):
    n = len(ps)

    def body(*refs):
        srcs, lbs, (send, recv) = refs[:n], refs[n:2 * n], refs[2 * n:]
        cps = _scatter_copies(srcs, lbs, kinds, send, recv)
        for cp in cps:
            cp.start()
        for cp in cps:
            cp.wait()

    return _comm_call(body, "scatter_partials", ps, _scatter_shapes(ps, kinds), n * (N_CHIPS - 1), 0)


def _scatter_shapes(ps, kinds):
    out_shape = []
    for p, kind in zip(ps, kinds):
        L, H, C = (p.shape[0], p.shape[1], p.shape[2] // N_CHIPS) if kind == "col" else (p.shape[0], p.shape[2], p.shape[3])
        out_shape.append(jax.ShapeDtypeStruct((N_CHIPS - 1, L, H, C), p.dtype))
    return out_shape


def _scatter_copies(srcs, lbs, kinds, send, recv, sem0=0):
    x, y, c, chips = _place()
    cps = []
    for a, (src, lb, kind) in enumerate(zip(srcs, lbs, kinds)):
        C = lb.shape[3]
        for k, chip in enumerate(chips):
            s = 2 * chip[0] + chip[1]
            part = src.at[:, :, pl.ds(pl.multiple_of(s * C, LANES), C)] if kind == "col" else src.at[:, s]
            cps.append(_remote(part, lb.at[k], send, recv, sem0 + a * (N_CHIPS - 1) + k, (*chip, c)))
    return cps


def _share_halves(g4s):
    n = len(g4s)
    out_shape = [jax.ShapeDtypeStruct(g.shape, g.dtype) for g in g4s]

    def body(*refs):
        outs, (send, recv) = refs[n:2 * n], refs[2 * n:]
        x, y, c, _ = _place()
        cps = [_remote(outs[a].at[:, c], outs[a].at[:, c], send, recv, a, (x, y, 1 - c)) for a in range(n)]
        for cp in cps:
            cp.start()
        for cp in cps:
            cp.wait()

    return _comm_call(body, "share_halves", g4s, out_shape, n, 0, {a: a for a in range(n)})


def _allreduce_small(part):
    N = part.shape[0]

    def body(p_ref, o_ref, buf, send, recv):
        x, y, c, _ = _place()
        me = 4 * x + 2 * y + c
        buf[me] = p_ref[...]
        cps = []
        for k in range(1, N_DEV):
            peer = (1 - x if k & 4 else x, 1 - y if k & 2 else y, 1 - c if k & 1 else c)
            cps.append(_remote(p_ref, buf.at[me], send, recv, k - 1, peer))
        for cp in cps:
            cp.start()
        for cp in cps:
            cp.wait()
        acc = buf[0]
        for i in range(1, N_DEV):
            acc = acc + buf[i]
        o_ref[...] = acc

    vmem = pl.BlockSpec(memory_space=pltpu.VMEM)
    return pl.pallas_call(
        body, name="allreduce_small", in_specs=[vmem], out_specs=vmem,
        out_shape=jax.ShapeDtypeStruct((N, LANES), F32),
        scratch_shapes=[pltpu.VMEM((N_DEV, N, LANES), F32), pltpu.SemaphoreType.DMA((N_DEV - 1,)),
                        pltpu.SemaphoreType.DMA((N_DEV - 1,))],
        compiler_params=pltpu.CompilerParams(has_side_effects=True, vmem_limit_bytes=VMEM_LIMIT_BYTES),
    )(part)


AFTER_ATTENTION = ("w_out", "w_up", "w_down")


def _layer_fwd(x, p, full=None, l=0):
    comm = full is not None

    def guest(first, second):
        first, second = ([k for k in ks if comm and k[0] < len(full)] for ks in (first, second))
        if not first + second:
            return None, []
        return _gather_guest([full[i][n] for i, n in first], [BIG_KIND[n] for _, n in first],
                             [full[i][n] for i, n in second], [BIG_KIND[n] for _, n in second]), first + second

    def done(keys, bufs):
        for (i, n), buf in zip(keys, bufs):
            full[i][n] = buf

    def weight(n):
        return full[l][n][0] if comm else p[n]

    in_sweep = ([(l, "w_up")] if l else [(l, n) for n in AFTER_ATTENTION]) if comm else []
    g, keys = guest([], [(l, "w_out")] if l else [])
    h1t, proj, out = _norm_matmul(x, p["norm1_g"], weight("w_in"), g)
    done(keys, out)
    attn, ltot, attnt, out = _attn_fwd(proj, p["q_norm_g"], p["k_norm_g"],
                                       [full[i][n] for i, n in in_sweep], [BIG_KIND[n] for _, n in in_sweep])
    done(in_sweep, out)
    g, keys = guest([], in_sweep)
    c1, out = _glu_conv_fwd(proj, p["conv_dw_w"], p["conv_dw_b"], g)
    done(keys, out)
    c, ct = _ln_silu_fwd(c1, p["conv_ln_g"], p["conv_ln_b"])
    x_mid, _ = _matmul_res([attn, c], weight("w_out"), x)
    g, keys = guest([(l + 1, "w_down")], [])
    h2t, u, out = _norm_matmul(x_mid, p["norm2_g"], weight("w_up"), g)
    done(keys, out)
    g, keys = guest([(l + 1, "w_in")], [(l + 1, "w_down")])
    act, out = _ffn_act_fwd(u, p["ffn_dw_w"], p["ffn_dw_b"], g)
    done(keys, out)
    g, keys = guest([(l + 1, "w_out")], [(l + 1, "w_in")])
    x_out, out = _matmul_res([act], weight("w_down"), x_mid, g)
    done(keys, out)
    saved = dict(x=x, h1t=h1t, proj=proj, attnt=attnt, ltot=ltot, c1=c1, ct=ct, x_mid=x_mid, h2t=h2t, u=u, act=act)
    return x_out, saved


def _chip_partials(gs, kinds, core):
    g4s = _halves_view(gs, kinds)
    return _add_halves(g4s, _exchange_halves(g4s), kinds, core)


def _halves_view(gs, kinds):
    return [g.reshape(1, 2, g.shape[0] // 2, g.shape[1]) if kind == "col"
            else g.reshape(N_CHIPS, 2, g.shape[0] // N_CHIPS // 2, g.shape[1]) for g, kind in zip(gs, kinds)]


def _add_halves(g4s, received, kinds, core):
    parts = _add_half(g4s, received, core)
    return [p if kind == "col" else p.reshape(1, N_CHIPS, p.shape[1], p.shape[2]) for p, kind in zip(parts, kinds)]


def _gather_guest(first, first_kinds, second, second_kinds):
    bufs = list(first) + list(second)
    n1 = len(first)

    def copies(ins, outs, send, recv, sem0=0):
        return (_gather_copies(outs[:n1], first_kinds, send, recv, sem0)
                + _pass_on_copies(outs[n1:], second_kinds, send, recv, sem0 + n1 * (N_CHIPS - 1)))

    return _Guest("gather", bufs, [jax.ShapeDtypeStruct(f.shape, f.dtype) for f in bufs],
                  {a: a for a in range(len(bufs))}, copies, len(bufs) * (N_CHIPS - 1))


def _scatter_guest(parts, kinds):
    return _Guest("scatter", list(parts), _scatter_shapes(parts, kinds), {},
                  lambda ins, outs, send, recv, sem0=0: _scatter_copies(ins, outs, kinds, send, recv, sem0),
                  len(parts) * (N_CHIPS - 1))


def _exchange_guest(g4s):
    def copies(ins, outs, send, recv, sem0=0):
        x, y, c, _ = _place()
        return [_remote(g.at[:, 1 - c], la, send, recv, sem0 + a, (x, y, 1 - c)) for a, (g, la) in enumerate(zip(ins, outs))]

    out_shape = [jax.ShapeDtypeStruct((g.shape[0],) + g.shape[2:], g.dtype) for g in g4s]
    return _Guest("exchange", list(g4s), out_shape, {}, copies, len(g4s))


def _share_guest(g4s):
    def copies(ins, outs, send, recv, sem0=0):
        x, y, c, _ = _place()
        return [_remote(o.at[:, c], o.at[:, c], send, recv, sem0 + a, (x, y, 1 - c)) for a, o in enumerate(outs)]

    n = len(g4s)
    return _Guest("share", list(g4s), [jax.ShapeDtypeStruct(g.shape, g.dtype) for g in g4s], {a: a for a in range(n)}, copies, n)


def _owned_sums(parts, landed, kinds, place):
    halves = _add_parts(parts, landed, place, kinds)
    return [g.reshape(2 * g.shape[2], g.shape[3]) for g in _share_halves(halves)]


def _layer_bwd(dx_out, s, p, comm=None):
    g = {}
    on = comm is not None
    core, place, pending = comm if on else (None, None, [])
    n_p = len(pending)
    finished = lambda arrays: [a.reshape(2 * a.shape[2], a.shape[3]) for a in arrays]

    if n_p:
        g4s_p = _halves_view(pending, ["col"] * n_p)
    dact, received = _matmul_nt([dx_out], p["w_down"], BF16, _exchange_guest(g4s_p) if n_p else None)
    if n_p:
        parts_p = _add_halves(g4s_p, received, ["col"] * n_p, core)
    g["w_down"] = _matmul_tn([s["act"]], [dx_out], tk=D_FF // 2, tn=D_MODEL, transposed=False)
    dgate, dval, g["ffn_dw_w"], g["ffn_dw_b"], landed_p = _ffn_act_bwd(
        s["u"], p["ffn_dw_w"], p["ffn_dw_b"], dact, _scatter_guest(parts_p, ["col"] * n_p) if n_p else None)
    guest = _share_guest(_add_parts(parts_p, landed_p, place, ["col"] * n_p)) if n_p else None
    dx_mid, g["norm2_g"], shared_p = _matmul_nt_rmsbwd([dgate, dval], p["w_up"], s["x_mid"], p["norm2_g"], dx_out, guest)
    g["w_up"] = _matmul_tn([s["h2t"]], [dgate, dval], tk=D_MODEL, tn=D_FF // 2)
    dmix, _ = _matmul_nt([dx_mid], p["w_out"], F32)
    g["w_out"] = _matmul_tn([s["attnt"], s["ct"]], [dx_mid], tk=ATTN_WIDTH, tn=D_MODEL)
    dc1, g["conv_ln_g"], g["conv_ln_b"] = _ln_silu_bwd(s["c1"], p["conv_ln_g"], p["conv_ln_b"], dmix)

    late = AFTER_ATTENTION
    parts, kinds = [], []
    if on:
        kinds = [BIG_KIND[n] for n in late]
        g4s = _halves_view([g[n] for n in late], kinds)
    da, db, g["conv_dw_w"], g["conv_dw_b"], received = _glu_conv_bwd(
        s["proj"], p["conv_dw_w"], dc1, _exchange_guest(g4s) if on else None)
    if on:
        parts = _add_halves(g4s, received, kinds, core)
    (dq, dk, dv, g["q_norm_g"], g["k_norm_g"]), landed = _attn_bwd(
        s["proj"], p["q_norm_g"], p["k_norm_g"], s["ltot"], dmix, parts, kinds)
    guest = _share_guest(_add_parts(parts, landed, place, kinds)) if on else None
    pieces = [dq, dk, dv, da, db]
    dx, g["norm1_g"], shared = _matmul_nt_rmsbwd(pieces, p["w_in"], s["x"], p["norm1_g"], dx_mid, guest)
    g["w_in"] = _matmul_tn([s["h1t"]], pieces, tk=D_MODEL, tn=ATTN_WIDTH)
    sums = {}
    if on:
        sums = dict(zip(late, finished(shared)), pending=finished(shared_p))
    return dx, g, sums


WEIGHTS = ("norm1_g", "w_in", "q_norm_g", "k_norm_g", "conv_dw_w", "conv_dw_b", "conv_ln_g", "conv_ln_b",
           "w_out", "norm2_g", "w_up", "ffn_dw_w", "ffn_dw_b", "w_down")
BIG = ("w_in", "w_out", "w_up", "w_down")
BIG_KIND = {"w_in": "col", "w_out": "row", "w_up": "col", "w_down": "row"}
SMALL_SHARDED = ("conv_dw_w", "ffn_dw_w")
REPLICATED = tuple(n for n in WEIGHTS if n not in BIG + SMALL_SHARDED)


def _pack(arrays):
    flat = jnp.concatenate([a.reshape(-1) for a in arrays])
    rows = -(-flat.shape[0] // (SUBLANES * LANES)) * SUBLANES
    return jnp.pad(flat, (0, rows * LANES - flat.shape[0])).reshape(rows, LANES)


def _unpack(packed, shapes):
    flat = packed.reshape(-1)
    out, off = [], 0
    for shape in shapes:
        size = 1
        for d in shape:
            size *= d
        out.append(flat[off:off + size].reshape(shape))
        off += size
    return out


def _unshard_last(stacked):
    n, L, K, C = stacked.shape
    return jnp.transpose(stacked, (1, 2, 0, 3)).reshape(L, K, n * C)


def kernel(x, norm1_g, w_in, q_norm_g, k_norm_g, conv_dw_w, conv_dw_b, conv_ln_g, conv_ln_b, w_out, norm2_g, w_up, ffn_dw_w, ffn_dw_b, w_down, loss_target, m_norm1_g, m_w_in, m_q_norm_g, m_k_norm_g, m_conv_dw_w, m_conv_dw_b, m_conv_ln_g, m_conv_ln_b, m_w_out, m_norm2_g, m_w_up, m_ffn_dw_w, m_ffn_dw_b, m_w_down, v_norm1_g, v_w_in, v_q_norm_g, v_k_norm_g, v_conv_dw_w, v_conv_dw_b, v_conv_ln_g, v_conv_ln_b, v_w_out, v_norm2_g, v_w_up, v_ffn_dw_w, v_ffn_dw_b, v_w_down):
    given = dict(locals())
    w = {n: given[n] for n in WEIGHTS}
    m = {n: given["m_" + n] for n in WEIGHTS}
    v = {n: given["v_" + n] for n in WEIGHTS}
    chip = 2 * lax.axis_index("x") + lax.axis_index("y")
    core = lax.axis_index("c")
    chip_arr = jnp.reshape(chip, (1,)).astype(jnp.int32)
    core_arr = jnp.reshape(core, (1,)).astype(jnp.int32)
    L = DEPTH

    place = jnp.concatenate([chip_arr, core_arr])

    first = _cast_into_full([w["w_in"]], [0], ["col"], chip_arr)
    rest = [(l, n) for l in range(L) for n in BIG if (l, n) != (0, "w_in")]
    cast = dict(zip(rest, _cast_into_full([w[n] for _, n in rest], [l for l, _ in rest],
                                          [BIG_KIND[n] for _, n in rest], chip_arr)))
    cast[0, "w_in"] = first[0]
    full = [{n: cast[l, n] for n in BIG} for l in range(L)]
    full[0]["w_in"] = _gather_weights([full[0]["w_in"]], ["col"])[0]
    small_full = {n: _unshard_last(stacked)
                  for n, stacked in zip(SMALL_SHARDED, _gather_small([w[n] for n in SMALL_SHARDED]))}
    params = []
    for l in range(L):
        p = {n: small_full[n][l] for n in SMALL_SHARDED}
        p.update({n: w[n][l][None] for n in REPLICATED})
        params.append(p)

    act = x[0]
    saved = []
    for l in range(L):
        act, s = _layer_fwd(act, params[l], full, l)
        saved.append(s)
    for l in range(L):
        params[l].update({n: full[l][n][0] for n in BIG})
    dx, loss_part = _loss_grad(act, loss_target[0])
    loss = lax.psum(loss_part, ("x", "y", "c"))

    grads = [None] * L
    summed = {}
    pending = []
    for l in reversed(range(L)):
        dx, grads[l], sums = _layer_bwd(dx, saved[l], params[l], (core_arr, place, pending))
        summed.update({(l, n): sums[n] for n in AFTER_ATTENTION})
        if pending:
            summed[l + 1, "w_in"] = sums["pending"][0]
        pending = [grads[l]["w_in"]]
    parts = _chip_partials(pending, ["col"], core_arr)
    summed[0, "w_in"] = _owned_sums(parts, _scatter_partials(parts, ["col"]), ["col"], place)[0]

    grad, delta, new_m, new_v = {}, {}, {}, {}
    for n in BIG:
        grad[n], delta[n], new_m[n], new_v[n] = _adamw_stacked(w[n], [summed[l, n] for l in range(L)], m[n], v[n])

    small = REPLICATED + SMALL_SHARDED
    small_grads = [jnp.stack([grads[l][n] for l in range(L)]) for n in small]
    small_sums = _unpack(_allreduce_small(_pack(small_grads)), [a.shape for a in small_grads])
    for n, g in zip(small, small_sums):
        if n in REPLICATED:
            grad[n] = g.reshape(w[n].shape)
        else:
            width = w[n].shape[2]
            grad[n] = lax.dynamic_slice_in_dim(g, chip * width, width, axis=2)
    for n in small:
        delta[n], new_m[n], new_v[n] = _adamw_small(w[n], grad[n], m[n], v[n])

    return (loss, dx[None], *[grad[n] for n in WEIGHTS], *[delta[n] for n in WEIGHTS],
            *[new_m[n] for n in WEIGHTS], *[new_v[n] for n in WEIGHTS])
```

```python
import jax
import jax.numpy as jnp
from jax import lax
from jax.experimental import pallas as pl
from jax.experimental.pallas import tpu as pltpu

F32 = jnp.float32
BF16 = jnp.bfloat16

DEPTH = 4
D_MODEL = 1024
HEADS = 8
HEAD_DIM = 64
ATTN_WIDTH = HEADS * HEAD_DIM
CONV_WIDTH = D_MODEL - ATTN_WIDTH
CONV_KERNEL = 31
D_FF = 2816
FFN_KERNEL = 3
EPS = 1e-6
ADAM_LR, ADAM_B1, ADAM_B2, ADAM_EPS, ADAM_WD, ADAM_STEP = 0.001, 0.9, 0.999, 1e-08, 0.01, 10

N_CHIPS = 4
N_DEV = 8
LANES = 128
SUBLANES = 8
VMEM_LIMIT_BYTES = 56 * 2**20
ATTN_TILE = 256
ATTN_HEADS_PER_STEP = 4
ATTN_BLOCK = ATTN_HEADS_PER_STEP * HEAD_DIM
CONV_PAD = 32
FFN_PAD = 8
MESH = pl.DeviceIdType.MESH


def _params(*sem):
    return pltpu.CompilerParams(dimension_semantics=sem if sem else None, vmem_limit_bytes=VMEM_LIMIT_BYTES)


class _Guest:
    def __init__(self, name, ins, out_shape, aliases, copies, n_sem):
        self.name, self.ins, self.out_shape, self.aliases, self.copies, self.n_sem = name, ins, out_shape, aliases, copies, n_sem


def _hosted_call(body, guest, *, name, grid, in_specs, out_specs, out_shape, scratch_shapes, operands):
    if guest is None:
        out = pl.pallas_call(body, name=name, grid=grid, in_specs=in_specs, out_specs=out_specs, out_shape=out_shape,
                             scratch_shapes=scratch_shapes, compiler_params=_params("arbitrary"))(*operands)
        return list(out), []
    n_in, n_out, n_scr = len(in_specs), len(out_specs), len(scratch_shapes)
    gi, go = len(guest.ins), len(guest.out_shape)

    def hosting(*refs):
        ins, g_in = refs[:n_in], refs[n_in:n_in + gi]
        outs, g_out = refs[n_in + gi:n_in + gi + n_out], refs[n_in + gi + n_out:n_in + gi + n_out + go]
        scratch, (send, recv) = refs[n_in + gi + n_out + go:-2], refs[-2:]

        @pl.when(pl.program_id(0) == 0)
        def _():
            for cp in guest.copies(g_in, g_out, send, recv):
                cp.start()

        body(*ins, *outs, *scratch)

        @pl.when(pl.program_id(0) == grid[0] - 1)
        def _():
            for cp in guest.copies(g_in, g_out, send, recv):
                cp.wait()

    out = pl.pallas_call(
        hosting, name=name + "_" + guest.name, grid=grid,
        in_specs=list(in_specs) + [ANY] * gi, out_specs=list(out_specs) + [ANY] * go,
        out_shape=list(out_shape) + list(guest.out_shape),
        scratch_shapes=list(scratch_shapes) + [pltpu.SemaphoreType.DMA((guest.n_sem,))] * 2,
        input_output_aliases={n_in + a: n_out + b for a, b in guest.aliases.items()},
        compiler_params=pltpu.CompilerParams(dimension_semantics=("arbitrary",), vmem_limit_bytes=VMEM_LIMIT_BYTES,
                                             has_side_effects=True),
    )(*operands, *guest.ins)
    return list(out[:n_out]), list(out[n_out:])


def _dot(a, b):
    return jnp.dot(a, b, preferred_element_type=F32)


def _dot_nt(a, b):
    return lax.dot_general(a, b, (((1,), (1,)), ((), ())), preferred_element_type=F32)


def _dot_tn(a, b):
    return lax.dot_general(a, b, (((0,), (0,)), ((), ())), preferred_element_type=F32)


def _sigmoid(x):
    return 1.0 / (1.0 + jnp.exp(-x))


def _norm_matmul(x, g, w, guest=None, *, out_dtype=F32, tm=256):
    S, D = x.shape
    N = w.shape[1]

    def body(x_ref, g_ref, w_ref, ht_ref, y_ref):
        xv = x_ref[...]
        h = xv * lax.rsqrt(jnp.mean(xv * xv, axis=-1, keepdims=True) + EPS) * g_ref[...]
        ht_ref[...] = h.T.astype(BF16)
        y_ref[...] = _dot(h.astype(BF16), w_ref[...]).astype(out_dtype)

    (ht, y), guest_out = _hosted_call(
        body, guest, name="norm_matmul", grid=(S // tm,),
        in_specs=[pl.BlockSpec((tm, D), lambda i: (i, 0)),
                  pl.BlockSpec((1, D), lambda i: (0, 0)),
                  pl.BlockSpec((D, N), lambda i: (0, 0))],
        out_specs=[pl.BlockSpec((D, tm), lambda i: (0, i)),
                   pl.BlockSpec((tm, N), lambda i: (i, 0))],
        out_shape=[jax.ShapeDtypeStruct((D, S), BF16), jax.ShapeDtypeStruct((S, N), out_dtype)],
        scratch_shapes=[], operands=[x, g, w])
    return ht, y, guest_out


def _matmul_res(pieces, w, res, guest=None, *, tm=512):
    S, N = res.shape
    K = w.shape[0]
    widths = [p.shape[1] for p in pieces]
    assert sum(widths) == K

    def body(*refs):
        p_refs, (w_ref, res_ref, o_ref) = refs[:len(pieces)], refs[len(pieces):]
        acc = res_ref[...]
        off = 0
        for p_ref, kp in zip(p_refs, widths):
            acc = acc + _dot(p_ref[...], w_ref[off:off + kp, :])
            off += kp
        o_ref[...] = acc

    (out,), guest_out = _hosted_call(
        body, guest, name="matmul_res", grid=(S // tm,),
        in_specs=[pl.BlockSpec((tm, kp), lambda i: (i, 0)) for kp in widths]
        + [pl.BlockSpec((K, N), lambda i: (0, 0)), pl.BlockSpec((tm, N), lambda i: (i, 0))],
        out_specs=[pl.BlockSpec((tm, N), lambda i: (i, 0))],
        out_shape=[jax.ShapeDtypeStruct((S, N), F32)],
        scratch_shapes=[], operands=[*pieces, w, res])
    return out, guest_out


def _nt_sum(p_refs, widths, w_ref):
    acc = None
    off = 0
    for p_ref, n_p in zip(p_refs, widths):
        d = _dot_nt(p_ref[...].astype(BF16), w_ref[:, off:off + n_p])
        acc = d if acc is None else acc + d
        off += n_p
    return acc


def _matmul_nt(pieces, w, out_dtype, guest=None, *, tm=512):
    S = pieces[0].shape[0]
    K, N = w.shape
    widths = [p.shape[1] for p in pieces]
    assert sum(widths) == N

    def body(*refs):
        p_refs, (w_ref, o_ref) = refs[:len(pieces)], refs[len(pieces):]
        o_ref[...] = _nt_sum(p_refs, widths, w_ref).astype(out_dtype)

    (out,), guest_out = _hosted_call(
        body, guest, name="matmul_nt", grid=(S // tm,),
        in_specs=[pl.BlockSpec((tm, n_p), lambda i: (i, 0)) for n_p in widths]
        + [pl.BlockSpec((K, N), lambda i: (0, 0))],
        out_specs=[pl.BlockSpec((tm, K), lambda i: (i, 0))],
        out_shape=[jax.ShapeDtypeStruct((S, K), out_dtype)],
        scratch_shapes=[], operands=[*pieces, w])
    return out, guest_out


def _matmul_nt_rmsbwd(pieces, w, x, g, dres, guest=None, *, tm=256):
    S, K = x.shape
    N = w.shape[1]
    widths = [p.shape[1] for p in pieces]
    assert sum(widths) == N

    def body(*refs):
        p_refs, (w_ref, x_ref, g_ref, dres_ref, dx_ref, dg_ref) = refs[:len(pieces)], refs[len(pieces):]
        dh = _nt_sum(p_refs, widths, w_ref)
        xv = x_ref[...]
        r = lax.rsqrt(jnp.mean(xv * xv, axis=-1, keepdims=True) + EPS)
        xh = xv * r
        dxh = dh * g_ref[...]
        dx_ref[...] = dres_ref[...] + r * (dxh - xh * jnp.mean(dxh * xh, axis=-1, keepdims=True))

        @pl.when(pl.program_id(0) == 0)
        def _():
            dg_ref[...] = jnp.zeros_like(dg_ref)

        dg_ref[...] += jnp.sum(dh * xh, axis=0, keepdims=True)

    (dx, dg), guest_out = _hosted_call(
        body, guest, name="matmul_nt_rmsbwd", grid=(S // tm,),
        in_specs=[pl.BlockSpec((tm, n_p), lambda i: (i, 0)) for n_p in widths]
        + [pl.BlockSpec((K, N), lambda i: (0, 0)), pl.BlockSpec((tm, K), lambda i: (i, 0)),
           pl.BlockSpec((1, K), lambda i: (0, 0)), pl.BlockSpec((tm, K), lambda i: (i, 0))],
        out_specs=[pl.BlockSpec((tm, K), lambda i: (i, 0)), pl.BlockSpec((1, K), lambda i: (0, 0))],
        out_shape=[jax.ShapeDtypeStruct((S, K), F32), jax.ShapeDtypeStruct((1, K), F32)],
        scratch_shapes=[], operands=[*pieces, w, x, g, dres])
    return dx, dg, guest_out


def _matmul_tn(xts, dys, *, tk, tn, ts=1024, transposed=True):
    S = dys[0].shape[0]
    xs = xts
    n_s = S // ts
    (mt,) = {x.shape[0 if transposed else 1] // tk for x in xts}
    (nt,) = {d.shape[1] // tn for d in dys}
    product = _dot if transposed else _dot_tn

    def body(*refs):
        x_refs, dy_refs, (o_ref, acc_ref) = refs[:len(xs)], refs[len(xs):len(xs) + len(dys)], refs[len(xs) + len(dys):]
        i, j, s = pl.program_id(0), pl.program_id(1), pl.program_id(2)

        @pl.when(s == 0)
        def _():
            acc_ref[...] = jnp.zeros_like(acc_ref)

        for a, x_ref in enumerate(x_refs):
            for b, dy_ref in enumerate(dy_refs):
                @pl.when((i // mt == a) & (j // nt == b))
                def _():
                    acc_ref[...] += product(x_ref[...], dy_ref[...].astype(BF16))

        @pl.when(s == n_s - 1)
        def _():
            o_ref[...] = acc_ref[...].astype(BF16)

    def x_map(a):
        if transposed:
            return lambda i, j, s: (jnp.where(i // mt == a, i % mt, 0), jnp.where(i // mt == a, s, 0))
        return lambda i, j, s: (jnp.where(i // mt == a, s, 0), jnp.where(i // mt == a, i % mt, 0))

    def dy_map(b):
        return lambda i, j, s: (jnp.where(j // nt == b, s, 0), jnp.where(j // nt == b, j % nt, 0))

    return pl.pallas_call(
        body, name="matmul_tn", grid=(len(xs) * mt, len(dys) * nt, n_s),
        in_specs=[pl.BlockSpec((tk, ts) if transposed else (ts, tk), x_map(a)) for a in range(len(xs))]
        + [pl.BlockSpec((ts, tn), dy_map(b)) for b in range(len(dys))],
        out_specs=pl.BlockSpec((tk, tn), lambda i, j, s: (i, j)),
        out_shape=jax.ShapeDtypeStruct((len(xs) * mt * tk, len(dys) * nt * tn), BF16),
        scratch_shapes=[pltpu.VMEM((tk, tn), F32)],
        compiler_params=_params("parallel", "parallel", "arbitrary"),
    )(*xs, *dys)


def _tri_consts():
    j = jnp.arange(ATTN_TILE)[:, None]
    s = jnp.arange(ATTN_TILE)[None, :]
    return (j > s).astype(BF16), (j <= s).astype(BF16), (j < s).astype(BF16)


SIGN_BIT = 0x80000000
WEIGHT_IS_ZERO = -104.0


def _log_terms(sn):
    minus_abs = lax.bitcast_convert_type(lax.bitcast_convert_type(sn, jnp.uint32) | jnp.uint32(SIGN_BIT), F32)
    lom = jnp.minimum(sn, 0.0) - jnp.log(1.0 + jnp.exp(minus_abs))
    return lom, lom - sn


def _causal_mask():
    t = lax.broadcasted_iota(jnp.int32, (ATTN_TILE, ATTN_TILE), 0)
    s = lax.broadcasted_iota(jnp.int32, (ATTN_TILE, ATTN_TILE), 1)
    return s < t


def _attn_prep(j, q_ref, k_ref, v_ref, qg_ref, kg_ref, qn_s, kn_s, vb_s, n_tiles):
    T = ATTN_TILE
    lanes = slice(LANES * j, LANES * (j + 1))
    scale = -(HEAD_DIM ** -0.5)

    def rstd(a):
        first = lax.broadcasted_iota(jnp.int32, a.shape, 1) < HEAD_DIM
        sq = a * a
        s0 = jnp.sum(jnp.where(first, sq, 0.0), axis=-1, keepdims=True)
        s1 = jnp.sum(jnp.where(first, 0.0, sq), axis=-1, keepdims=True)
        return lax.rsqrt(jnp.where(first, s0, s1) * (1.0 / HEAD_DIM) + EPS)

    def prep(i, carry):
        rows = pl.ds(pl.multiple_of(i * T, T), T)
        q = q_ref[rows, lanes]
        k = k_ref[rows, lanes]
        qg = jnp.concatenate([qg_ref[...], qg_ref[...]], axis=1)
        kg = jnp.concatenate([kg_ref[...], kg_ref[...]], axis=1)
        for ref, both in ((qn_s, (q * rstd(q) * qg * scale).astype(BF16)), (kn_s, (k * rstd(k) * kg).astype(BF16)),
                          (vb_s, v_ref[rows, lanes].astype(BF16))):
            ref[2 * j, rows, :] = both[:, :HEAD_DIM]
            ref[2 * j + 1, rows, :] = both[:, HEAD_DIM:]
        return carry

    lax.fori_loop(0, n_tiles, prep, 0)


def _attn_fwd(proj, qg, kg, fulls=(), kinds=()):
    S = proj.shape[0]
    n_comm = len(fulls)
    T = ATTN_TILE
    n_tiles = S // T
    suffix, _, _ = _tri_consts()
    pairs = HEADS // ATTN_HEADS_PER_STEP
    q_blk, k_blk, v_blk = 0, ATTN_WIDTH // ATTN_BLOCK, 2 * ATTN_WIDTH // ATTN_BLOCK

    def body(*refs):
        q_ref, k_ref, v_ref, qg_ref, kg_ref, tri_ref = refs[:6]
        o_ref, lt_ref, ot_ref, start_ref, qn_s, kn_s, vb_s = refs[6 + n_comm:13 + n_comm]
        w_refs = refs[13 + n_comm:13 + 2 * n_comm]
        sems = refs[13 + 2 * n_comm:]
        step_id = pl.program_id(0)
        if n_comm:
            @pl.when(pl.program_id(0) == 0)
            def _():
                for cp in _gather_copies(w_refs, kinds, *sems, 0):
                    cp.start()

        heads = range(ATTN_HEADS_PER_STEP)
        for j in range(ATTN_HEADS_PER_STEP // 2):
            _attn_prep(j, q_ref, k_ref, v_ref, qg_ref, kg_ref, qn_s, kn_s, vb_s, n_tiles)

        def q_tile(qi, carry0):
            qrows = pl.ds(pl.multiple_of(qi * T, T), T)
            qt = [qn_s[h, qrows, :] for h in heads]

            def tile(kj, carry, diag):
                krows = pl.ds(pl.multiple_of(kj * T, T), T)
                s = [_dot_nt(qt[h], kn_s[h, krows, :]) for h in heads]
                terms = [_log_terms(s[h]) for h in heads]
                lom = [terms[h][0] for h in heads]
                if diag:
                    mask = _causal_mask()
                    lom = [jnp.where(mask, lom[h], 0.0) for h in heads]
                lom = [lom[h].astype(BF16) for h in heads]
                tail = [_dot(lom[h], tri_ref[...]) for h in heads]
                w = [jnp.exp(terms[h][1] + tail[h] + carry[h][1]) for h in heads]
                if diag:
                    w = [jnp.where(mask, w[h], 0.0) for h in heads]
                return tuple((carry[h][0] + _dot(w[h].astype(BF16), vb_s[h, krows, :]),
                              carry[h][1] + (tail[h][:, 0:1] + lom[h][:, 0:1].astype(F32))) for h in heads)

            init = tuple((jnp.zeros((T, HEAD_DIM), F32), jnp.zeros((T, 1), F32)) for h in heads)
            def alive(cr):
                worst = cr[0][1]
                for h in heads[1:]:
                    worst = jnp.maximum(worst, cr[h][1])
                return jnp.max(worst) >= WEIGHT_IS_ZERO

            def step(state):
                t, _, cr = state
                cr = tile(qi - 1 - t, cr, False)
                return t + 1, alive(cr), cr

            first = tile(qi, init, True)
            swept, _, carry = lax.while_loop(lambda st: (st[0] < qi) & st[1], step, (jnp.int32(0), alive(first), first))
            start_ref[step_id, qi] = (qi - swept).astype(F32)
            o = jnp.concatenate([carry[h][0] for h in heads], axis=1)
            o_ref[qrows, :] = o.astype(BF16)
            ot_ref[:, qrows] = o.T.astype(BF16)
            for h in heads:
                lt_ref[qrows, HEAD_DIM * h:HEAD_DIM * (h + 1)] = jnp.broadcast_to(carry[h][1], (T, HEAD_DIM))
            return carry0

        lax.fori_loop(0, n_tiles, q_tile, 0)
        if n_comm:
            @pl.when(pl.program_id(0) == pairs - 1)
            def _():
                for cp in _gather_copies(w_refs, kinds, *sems, 0):
                    cp.wait()

    n_sem = n_comm * (N_CHIPS - 1)
    per_head = pl.BlockSpec((ATTN_HEADS_PER_STEP, S, HEAD_DIM), lambda p: (p, 0, 0))
    out = pl.pallas_call(
        body, name="attn_fwd_gather" if n_comm else "attn_fwd", grid=(pairs,),
        in_specs=[pl.BlockSpec((S, ATTN_BLOCK), lambda p: (0, q_blk + p)),
                  pl.BlockSpec((S, ATTN_BLOCK), lambda p: (0, k_blk + p)),
                  pl.BlockSpec((S, ATTN_BLOCK), lambda p: (0, v_blk + p)),
                  pl.BlockSpec((1, HEAD_DIM), lambda p: (0, 0)),
                  pl.BlockSpec((1, HEAD_DIM), lambda p: (0, 0)),
                  pl.BlockSpec((T, T), lambda p: (0, 0))] + [ANY] * n_comm,
        out_specs=[pl.BlockSpec((S, ATTN_BLOCK), lambda p: (0, p)),
                   pl.BlockSpec((None, S, ATTN_BLOCK), lambda p: (p, 0, 0)),
                   pl.BlockSpec((ATTN_BLOCK, S), lambda p: (p, 0)),
                   pl.BlockSpec(memory_space=pltpu.SMEM)] + [per_head] * 3 + [ANY] * n_comm,
        out_shape=[jax.ShapeDtypeStruct((S, ATTN_WIDTH), BF16),
                   jax.ShapeDtypeStruct((pairs, S, ATTN_BLOCK), F32),
                   jax.ShapeDtypeStruct((ATTN_WIDTH, S), BF16),
                   jax.ShapeDtypeStruct((pairs, n_tiles), F32)] + [jax.ShapeDtypeStruct((HEADS, S, HEAD_DIM), BF16)] * 3
        + [jax.ShapeDtypeStruct(f.shape, f.dtype) for f in fulls],
        scratch_shapes=[pltpu.SemaphoreType.DMA((n_sem,))] * 2 if n_comm else [],
        input_output_aliases={6 + a: 7 + a for a in range(n_comm)},
        compiler_params=pltpu.CompilerParams(dimension_semantics=("arbitrary",), vmem_limit_bytes=VMEM_LIMIT_BYTES,
                                             has_side_effects=bool(n_comm)),
    )(proj, proj, proj, qg, kg, suffix, *fulls)
    return out[0], (out[1], out[3], out[4], out[5], out[6]), out[2], list(out[7:])


def _attn_bwd(proj, qg, kg, ltot, dmix, parts=(), kinds=()):
    S = proj.shape[0]
    n_comm = len(parts)
    T = ATTN_TILE
    n_tiles = S // T
    _, prefix_incl, prefix_excl = _tri_consts()
    pairs = HEADS // ATTN_HEADS_PER_STEP
    q_blk, k_blk, v_blk = 0, ATTN_WIDTH // ATTN_BLOCK, 2 * ATTN_WIDTH // ATTN_BLOCK
    scale = HEAD_DIM ** -0.5

    def body(*refs):
        q_ref, k_ref, qg_ref, kg_ref, lt_ref, do_ref, ti_ref, te_ref, start_ref, qn_s, kn_s, vb_s = refs[:12]
        p_refs = refs[12:12 + n_comm]
        dq_ref, dk_ref, dv_ref, dqg_ref, dkg_ref = refs[12 + n_comm:17 + n_comm]
        lb_refs = refs[17 + n_comm:17 + 2 * n_comm]
        dq_s, dk_s, dv_s = refs[17 + 2 * n_comm:20 + 2 * n_comm]
        sems = refs[20 + 2 * n_comm:]
        step_id = pl.program_id(0)

        @pl.when(pl.program_id(0) == 0)
        def _():
            dqg_ref[...] = jnp.zeros_like(dqg_ref)
            dkg_ref[...] = jnp.zeros_like(dkg_ref)
            for cp in _scatter_copies(p_refs, lb_refs, kinds, *sems) if n_comm else ():
                cp.start()

        heads = range(ATTN_HEADS_PER_STEP)
        dk_s[...] = jnp.zeros_like(dk_s)
        dv_s[...] = jnp.zeros_like(dv_s)

        def q_tile(qi, carry0):
            qrows = pl.ds(pl.multiple_of(qi * T, T), T)
            qt = [qn_s[h, qrows, :] for h in heads]
            dob = [do_ref[qrows, HEAD_DIM * h:HEAD_DIM * (h + 1)].astype(BF16) for h in heads]
            lt = [lt_ref[qrows, HEAD_DIM * h:HEAD_DIM * h + 1] for h in heads]

            def tile(kj, carry, diag):
                krows = pl.ds(pl.multiple_of(kj * T, T), T)
                kt = [kn_s[h, krows, :] for h in heads]
                s = [_dot_nt(qt[h], kt[h]) for h in heads]
                dw = [_dot_nt(dob[h], vb_s[h, krows, :]) for h in heads]
                terms = [_log_terms(s[h]) for h in heads]
                lom = [terms[h][0] for h in heads]
                if diag:
                    mask = _causal_mask()
                    lom = [jnp.where(mask, lom[h], 0.0) for h in heads]
                pin = [_dot(lom[h].astype(BF16), ti_ref[...]) for h in heads]
                w = [jnp.exp(terms[h][1] + ((lt[h] - carry[h][1]) - pin[h])) for h in heads]
                if diag:
                    w = [jnp.where(mask, w[h], 0.0) for h in heads]
                e = [w[h] * dw[h] for h in heads]
                gex = [_dot(e[h].astype(BF16), te_ref[...]) for h in heads]
                dz = [e[h] - jnp.exp(terms[h][1]) * (e[h] + (carry[h][2] + gex[h])) for h in heads]
                if diag:
                    dz = [jnp.where(mask, dz[h], 0.0) for h in heads]
                new = []
                for h in heads:
                    dzb = dz[h].astype(BF16)
                    dk_s[h, krows, :] += _dot_tn(dzb, qt[h])
                    dv_s[h, krows, :] += _dot_tn(w[h].astype(BF16), dob[h])
                    new.append((carry[h][0] + _dot(dzb, kt[h]),
                                carry[h][1] + pin[h][:, T - 1:T],
                                carry[h][2] + gex[h][:, T - 1:T] + e[h][:, T - 1:T]))
                return tuple(new)

            zero = jnp.zeros((T, 1), F32)
            init = tuple((jnp.zeros((T, HEAD_DIM), F32), zero, zero) for h in heads)
            first = jnp.clip(start_ref[step_id, qi].astype(jnp.int32), 0, qi)
            last = tile(qi, lax.fori_loop(first, qi, lambda kj, cr: tile(kj, cr, False), init), True)
            for h in heads:
                dq_s[h, qrows, :] = last[h][0]
            return carry0

        lax.fori_loop(0, n_tiles, q_tile, 0)

        def finish(i, carry):
            rows = pl.ds(pl.multiple_of(i * T, T), T)
            new = []
            for h in heads:
                lanes = slice(HEAD_DIM * h, HEAD_DIM * (h + 1))
                q = q_ref[rows, lanes]
                k = k_ref[rows, lanes]
                rq = lax.rsqrt(jnp.mean(q * q, axis=-1, keepdims=True) + EPS)
                rk = lax.rsqrt(jnp.mean(k * k, axis=-1, keepdims=True) + EPS)
                qh = q * rq
                kh = k * rk
                dqn = dq_s[h, rows, :] * scale
                dkn = -dk_s[h, rows, :]
                dqh = dqn * qg_ref[...]
                dkh = dkn * kg_ref[...]
                dq_ref[rows, lanes] = (rq * (dqh - qh * jnp.mean(dqh * qh, axis=-1, keepdims=True))).astype(BF16)
                dk_ref[rows, lanes] = (rk * (dkh - kh * jnp.mean(dkh * kh, axis=-1, keepdims=True))).astype(BF16)
                dv_ref[rows, lanes] = dv_s[h, rows, :].astype(BF16)
                new.append(carry[2 * h] + jnp.sum(dqn * qh, axis=0, keepdims=True))
                new.append(carry[2 * h + 1] + jnp.sum(dkn * kh, axis=0, keepdims=True))
            return tuple(new)

        zero = jnp.zeros((1, HEAD_DIM), F32)
        sums = lax.fori_loop(0, n_tiles, finish, (zero,) * (2 * len(heads)))
        dqg_ref[...] += sum(sums[0::2])
        dkg_ref[...] += sum(sums[1::2])
        if n_comm:
            @pl.when(pl.program_id(0) == pairs - 1)
            def _():
                for cp in _scatter_copies(p_refs, lb_refs, kinds, *sems):
                    cp.wait()

    blk = lambda off: pl.BlockSpec((S, ATTN_BLOCK), lambda p: (0, off + p))
    row64 = pl.BlockSpec((1, HEAD_DIM), lambda p: (0, 0))
    tri = pl.BlockSpec((T, T), lambda p: (0, 0))
    n_sem = n_comm * (N_CHIPS - 1)
    out = pl.pallas_call(
        body, name="attn_bwd_scatter" if n_comm else "attn_bwd", grid=(pairs,),
        in_specs=[blk(q_blk), blk(k_blk), row64, row64,
                  pl.BlockSpec((None, S, ATTN_BLOCK), lambda p: (p, 0, 0)), blk(0), tri, tri,
                  pl.BlockSpec(memory_space=pltpu.SMEM)]
        + [pl.BlockSpec((ATTN_HEADS_PER_STEP, S, HEAD_DIM), lambda p: (p, 0, 0))] * 3 + [ANY] * n_comm,
        out_specs=[blk(0), blk(0), blk(0), row64, row64] + [ANY] * n_comm,
        out_shape=[jax.ShapeDtypeStruct((S, ATTN_WIDTH), BF16)] * 3 + [jax.ShapeDtypeStruct((1, HEAD_DIM), F32)] * 2
        + _scatter_shapes(parts, kinds),
        scratch_shapes=[pltpu.VMEM((ATTN_HEADS_PER_STEP, S, HEAD_DIM), F32)] * 3
        + ([pltpu.SemaphoreType.DMA((n_sem,))] * 2 if n_comm else []),
        compiler_params=pltpu.CompilerParams(dimension_semantics=("arbitrary",), vmem_limit_bytes=VMEM_LIMIT_BYTES,
                                             has_side_effects=bool(n_comm)),
    )(proj, proj, qg, kg, ltot[0], dmix, prefix_incl, prefix_excl, *ltot[1:], *parts)
    return out[:5], list(out[5:])


def _shifted(win, n_rows):
    return [win if b == 0 else pltpu.roll(win, n_rows - b, 0) for b in range(SUBLANES)]


def _taps(variants, offsets, tm):
    return {o: variants[o % SUBLANES][(o // SUBLANES) * SUBLANES:(o // SUBLANES) * SUBLANES + tm, :] for o in offsets}


def _fold_rows(a):
    return jnp.sum(a.reshape(a.shape[0] // SUBLANES, SUBLANES, a.shape[1]), axis=0)


def _glu_conv_fwd(proj, w, bias, guest=None, *, tm=256):
    S = proj.shape[0]
    CB = LANES
    a_blk, b_blk = 3 * ATTN_WIDTH // CB, (3 * ATTN_WIDTH + CONV_WIDTH) // CB
    n_rows = tm + CONV_PAD

    def body(a_ref, b_ref, w_ref, bias_ref, c1_ref, pad_s):
        pad_s[0:CONV_PAD, :] = jnp.zeros((CONV_PAD, CB), F32)

        def fill(i, carry):
            rows = pl.ds(pl.multiple_of(i * tm, tm), tm)
            pad_s[pl.ds(pl.multiple_of(CONV_PAD + i * tm, SUBLANES), tm), :] = a_ref[rows, :] * _sigmoid(b_ref[rows, :])
            return carry

        lax.fori_loop(0, S // tm, fill, 0)

        def conv(i, carry):
            r0 = pl.multiple_of(i * tm, tm)
            taps = _taps(_shifted(pad_s[pl.ds(r0, n_rows), :], n_rows), range(2, 2 + CONV_KERNEL), tm)
            acc = jnp.broadcast_to(bias_ref[...], (tm, CB))
            for k in range(CONV_KERNEL):
                acc = acc + w_ref[k:k + 1, :] * taps[k + 2]
            c1_ref[pl.ds(r0, tm), :] = acc
            return carry

        lax.fori_loop(0, S // tm, conv, 0)

    (c1,), guest_out = _hosted_call(
        body, guest, name="glu_conv_fwd", grid=(CONV_WIDTH // CB,),
        in_specs=[pl.BlockSpec((S, CB), lambda j: (0, a_blk + j)), pl.BlockSpec((S, CB), lambda j: (0, b_blk + j)),
                  pl.BlockSpec((CONV_KERNEL, CB), lambda j: (0, j)), pl.BlockSpec((1, CB), lambda j: (0, j))],
        out_specs=[pl.BlockSpec((S, CB), lambda j: (0, j))],
        out_shape=[jax.ShapeDtypeStruct((S, CONV_WIDTH), F32)],
        scratch_shapes=[pltpu.VMEM((S + CONV_PAD, CB), F32)], operands=[proj, proj, w, bias])
    return c1, guest_out


def _glu_conv_bwd(proj, w, dc1, guest=None, *, tm=256):
    S = proj.shape[0]
    CB = LANES
    a_blk, b_blk = 3 * ATTN_WIDTH // CB, (3 * ATTN_WIDTH + CONV_WIDTH) // CB
    n_rows = tm + CONV_PAD

    def body(a_ref, b_ref, w_ref, dc1_ref, da_ref, db_ref, dw_ref, dbias_ref, pad_s, dpad_s, dw_s):
        pad_s[0:CONV_PAD, :] = jnp.zeros((CONV_PAD, CB), F32)
        dpad_s[S:S + CONV_PAD, :] = jnp.zeros((CONV_PAD, CB), F32)
        dw_s[...] = jnp.zeros_like(dw_s)

        def fill(i, carry):
            rows = pl.ds(pl.multiple_of(i * tm, tm), tm)
            pad_s[pl.ds(pl.multiple_of(CONV_PAD + i * tm, SUBLANES), tm), :] = a_ref[rows, :] * _sigmoid(b_ref[rows, :])
            dpad_s[rows, :] = dc1_ref[rows, :]
            return carry

        lax.fori_loop(0, S // tm, fill, 0)

        def conv(i, carry):
            r0 = pl.multiple_of(i * tm, tm)
            rows = pl.ds(r0, tm)
            taps = _taps(_shifted(dpad_s[pl.ds(r0, n_rows), :], n_rows), range(CONV_KERNEL), tm)
            acc = jnp.zeros((tm, CB), F32)
            for k in range(CONV_KERNEL):
                acc = acc + w_ref[k:k + 1, :] * taps[CONV_KERNEL - 1 - k]
            a = a_ref[rows, :]
            sg = _sigmoid(b_ref[rows, :])
            da_ref[rows, :] = (acc * sg).astype(BF16)
            db_ref[rows, :] = (acc * a * sg * (1.0 - sg)).astype(BF16)
            d = taps[0]
            taps = _taps(_shifted(pad_s[pl.ds(r0, n_rows), :], n_rows), range(2, 2 + CONV_KERNEL), tm)
            for k in range(CONV_KERNEL):
                dw_s[SUBLANES * k:SUBLANES * (k + 1), :] += _fold_rows(d * taps[k + 2])
            dw_s[SUBLANES * CONV_KERNEL:SUBLANES * (CONV_KERNEL + 1), :] += _fold_rows(d)
            return carry

        lax.fori_loop(0, S // tm, conv, 0)
        for k in range(CONV_KERNEL):
            dw_ref[k:k + 1, :] = jnp.sum(dw_s[SUBLANES * k:SUBLANES * (k + 1), :], axis=0, keepdims=True)
        dbias_ref[...] = jnp.sum(dw_s[SUBLANES * CONV_KERNEL:SUBLANES * (CONV_KERNEL + 1), :], axis=0, keepdims=True)

    col = lambda off: pl.BlockSpec((S, CB), lambda j: (0, off + j))
    out, guest_out = _hosted_call(
        body, guest, name="glu_conv_bwd", grid=(CONV_WIDTH // CB,),
        in_specs=[col(a_blk), col(b_blk), pl.BlockSpec((CONV_KERNEL, CB), lambda j: (0, j)), col(0)],
        out_specs=[col(0), col(0), pl.BlockSpec((CONV_KERNEL, CB), lambda j: (0, j)), pl.BlockSpec((1, CB), lambda j: (0, j))],
        out_shape=[jax.ShapeDtypeStruct((S, CONV_WIDTH), BF16)] * 2
        + [jax.ShapeDtypeStruct((CONV_KERNEL, CONV_WIDTH), F32), jax.ShapeDtypeStruct((1, CONV_WIDTH), F32)],
        scratch_shapes=[pltpu.VMEM((S + CONV_PAD, CB), F32), pltpu.VMEM((S + CONV_PAD, CB), F32),
                        pltpu.VMEM((SUBLANES * (CONV_KERNEL + 1), CB), F32)], operands=[proj, proj, w, dc1])
    return (*out, guest_out)


def _ln_stats(c1):
    mu = jnp.mean(c1, axis=-1, keepdims=True)
    xc = c1 - mu
    r = lax.rsqrt(jnp.mean(xc * xc, axis=-1, keepdims=True) + EPS)
    return xc * r, r


def _ln_silu_fwd(c1, g, b, *, tm=512):
    S, C = c1.shape

    def body(c1_ref, g_ref, b_ref, c_ref, ct_ref):
        yh, _ = _ln_stats(c1_ref[...])
        y = yh * g_ref[...] + b_ref[...]
        c = y * _sigmoid(y)
        c_ref[...] = c.astype(BF16)
        ct_ref[...] = c.T.astype(BF16)

    vec = pl.BlockSpec((1, C), lambda i: (0, 0))
    return pl.pallas_call(
        body, name="ln_silu_fwd", grid=(S // tm,),
        in_specs=[pl.BlockSpec((tm, C), lambda i: (i, 0)), vec, vec],
        out_specs=[pl.BlockSpec((tm, C), lambda i: (i, 0)), pl.BlockSpec((C, tm), lambda i: (0, i))],
        out_shape=[jax.ShapeDtypeStruct((S, C), BF16), jax.ShapeDtypeStruct((C, S), BF16)],
        compiler_params=_params("parallel"),
    )(c1, g, b)


def _ln_silu_bwd(c1, g, b, dmix, *, tm=512):
    S, C = c1.shape

    def body(c1_ref, g_ref, b_ref, dc_ref, dc1_ref, dg_ref, db_ref):
        yh, r = _ln_stats(c1_ref[...])
        y = yh * g_ref[...] + b_ref[...]
        sg = _sigmoid(y)
        dy = dc_ref[...] * (sg * (1.0 + y * (1.0 - sg)))
        dyh = dy * g_ref[...]
        dc1_ref[...] = r * (dyh - jnp.mean(dyh, axis=-1, keepdims=True)
                            - yh * jnp.mean(dyh * yh, axis=-1, keepdims=True))

        @pl.when(pl.program_id(0) == 0)
        def _():
            dg_ref[...] = jnp.zeros_like(dg_ref)
            db_ref[...] = jnp.zeros_like(db_ref)

        dg_ref[...] += jnp.sum(dy * yh, axis=0, keepdims=True)
        db_ref[...] += jnp.sum(dy, axis=0, keepdims=True)

    vec = pl.BlockSpec((1, C), lambda i: (0, 0))
    return pl.pallas_call(
        body, name="ln_silu_bwd", grid=(S // tm,),
        in_specs=[pl.BlockSpec((tm, C), lambda i: (i, 0)), vec, vec, pl.BlockSpec((tm, C), lambda i: (i, 1))],
        out_specs=[pl.BlockSpec((tm, C), lambda i: (i, 0)), vec, vec],
        out_shape=[jax.ShapeDtypeStruct((S, C), F32), jax.ShapeDtypeStruct((1, C), F32), jax.ShapeDtypeStruct((1, C), F32)],
        compiler_params=_params("arbitrary"),
    )(c1, g, b, dmix)


FFN_CB = 256


def _ffn_gate(pad_s, w_ref, bias_ref, r0, tm):
    n_rows = tm + FFN_PAD
    taps = _taps(_shifted(pad_s[pl.ds(r0, n_rows), :], n_rows), range(FFN_PAD - 2, FFN_PAD + 1), tm)
    g1 = bias_ref[...] + w_ref[0:1, :] * taps[6] + w_ref[1:2, :] * taps[7] + w_ref[2:3, :] * taps[8]
    return g1, taps


def _ffn_act_fwd(u, w, bias, guest=None, *, tm=256):
    S = u.shape[0]
    CB = FFN_CB
    nb = D_FF // CB

    def body(g_ref, v_ref, w_ref, bias_ref, o_ref, pad_s):
        pad_s[0:FFN_PAD, :] = jnp.zeros((FFN_PAD, CB), F32)

        def fill(i, carry):
            pad_s[pl.ds(pl.multiple_of(FFN_PAD + i * tm, SUBLANES), tm), :] = g_ref[pl.ds(pl.multiple_of(i * tm, tm), tm), :].astype(F32)
            return carry

        lax.fori_loop(0, S // tm, fill, 0)

        def act(i, carry):
            r0 = pl.multiple_of(i * tm, tm)
            g1, _ = _ffn_gate(pad_s, w_ref, bias_ref, r0, tm)
            o_ref[pl.ds(r0, tm), :] = (g1 * _sigmoid(g1) * v_ref[pl.ds(r0, tm), :].astype(F32)).astype(BF16)
            return carry

        lax.fori_loop(0, S // tm, act, 0)

    (act,), guest_out = _hosted_call(
        body, guest, name="ffn_act_fwd", grid=(nb,),
        in_specs=[pl.BlockSpec((S, CB), lambda j: (0, j)), pl.BlockSpec((S, CB), lambda j: (0, nb + j)),
                  pl.BlockSpec((FFN_KERNEL, CB), lambda j: (0, j)), pl.BlockSpec((1, CB), lambda j: (0, j))],
        out_specs=[pl.BlockSpec((S, CB), lambda j: (0, j))],
        out_shape=[jax.ShapeDtypeStruct((S, D_FF), BF16)],
        scratch_shapes=[pltpu.VMEM((S + FFN_PAD, CB), F32)], operands=[u, u, w, bias])
    return act, guest_out


def _ffn_act_bwd(u, w, bias, dact, guest=None, *, tm=256):
    S = u.shape[0]
    CB = FFN_CB
    nb = D_FF // CB

    def body(g_ref, v_ref, w_ref, bias_ref, da_ref, dg_ref, dv_ref, dw_ref, dbias_ref, pad_s, dpad_s, dw_s):
        pad_s[0:FFN_PAD, :] = jnp.zeros((FFN_PAD, CB), F32)
        dpad_s[S:S + FFN_PAD, :] = jnp.zeros((FFN_PAD, CB), F32)
        dw_s[...] = jnp.zeros_like(dw_s)

        def fill(i, carry):
            pad_s[pl.ds(pl.multiple_of(FFN_PAD + i * tm, SUBLANES), tm), :] = g_ref[pl.ds(pl.multiple_of(i * tm, tm), tm), :].astype(F32)
            return carry

        lax.fori_loop(0, S // tm, fill, 0)

        def first(i, carry):
            r0 = pl.multiple_of(i * tm, tm)
            rows = pl.ds(r0, tm)
            g1, taps = _ffn_gate(pad_s, w_ref, bias_ref, r0, tm)
            sg = _sigmoid(g1)
            da = da_ref[rows, :].astype(F32)
            dv_ref[rows, :] = (da * g1 * sg).astype(BF16)
            dg1 = da * v_ref[rows, :].astype(F32) * (sg * (1.0 + g1 * (1.0 - sg)))
            dpad_s[rows, :] = dg1
            for k in range(FFN_KERNEL):
                dw_s[SUBLANES * k:SUBLANES * (k + 1), :] += _fold_rows(dg1 * taps[FFN_PAD - 2 + k])
            dw_s[SUBLANES * FFN_KERNEL:SUBLANES * (FFN_KERNEL + 1), :] += _fold_rows(dg1)
            return carry

        lax.fori_loop(0, S // tm, first, 0)

        def second(i, carry):
            r0 = pl.multiple_of(i * tm, tm)
            n_rows = tm + FFN_PAD
            taps = _taps(_shifted(dpad_s[pl.ds(r0, n_rows), :], n_rows), range(FFN_KERNEL), tm)
            dg_ref[pl.ds(r0, tm), :] = (w_ref[2:3, :] * taps[0] + w_ref[1:2, :] * taps[1] + w_ref[0:1, :] * taps[2]).astype(BF16)
            return carry

        lax.fori_loop(0, S // tm, second, 0)
        for k in range(FFN_KERNEL):
            dw_ref[k:k + 1, :] = jnp.sum(dw_s[SUBLANES * k:SUBLANES * (k + 1), :], axis=0, keepdims=True)
        dbias_ref[...] = jnp.sum(dw_s[SUBLANES * FFN_KERNEL:SUBLANES * (FFN_KERNEL + 1), :], axis=0, keepdims=True)

    col = lambda off: pl.BlockSpec((S, CB), lambda j: (0, off + j))
    wspec = pl.BlockSpec((FFN_KERNEL, CB), lambda j: (0, j))
    bspec = pl.BlockSpec((1, CB), lambda j: (0, j))
    out, guest_out = _hosted_call(
        body, guest, name="ffn_act_bwd", grid=(nb,),
        in_specs=[col(0), col(nb), wspec, bspec, col(0)],
        out_specs=[col(0), col(0), wspec, bspec],
        out_shape=[jax.ShapeDtypeStruct((S, D_FF), BF16)] * 2
        + [jax.ShapeDtypeStruct((FFN_KERNEL, D_FF), F32), jax.ShapeDtypeStruct((1, D_FF), F32)],
        scratch_shapes=[pltpu.VMEM((S + FFN_PAD, CB), F32), pltpu.VMEM((S + FFN_PAD, CB), F32),
                        pltpu.VMEM((SUBLANES * (FFN_KERNEL + 1), CB), F32)], operands=[u, u, w, bias, dact])
    return (*out, guest_out)


def _loss_grad(y, target, *, tm=512):
    S, D = y.shape

    def body(y_ref, t_ref, dy_ref, l_ref):
        d = y_ref[...] - t_ref[...]
        dy_ref[...] = d * (1.0 / D)

        @pl.when(pl.program_id(0) == 0)
        def _():
            l_ref[...] = jnp.zeros_like(l_ref)

        l_ref[...] += 0.5 * jnp.sum(jnp.mean(d * d, axis=-1, keepdims=True), axis=0, keepdims=True)

    dy, l = pl.pallas_call(
        body, name="loss_grad", grid=(S // tm,),
        in_specs=[pl.BlockSpec((tm, D), lambda i: (i, 0))] * 2,
        out_specs=[pl.BlockSpec((tm, D), lambda i: (i, 0)), pl.BlockSpec((SUBLANES, LANES), lambda i: (0, 0))],
        out_shape=[jax.ShapeDtypeStruct((S, D), F32), jax.ShapeDtypeStruct((SUBLANES, LANES), F32)],
        compiler_params=_params("arbitrary"),
    )(y, target)
    return dy, l[0, 0]


def _row_tile(rows, cap=512):
    t = min(rows, cap)
    while rows % t or t % SUBLANES:
        t -= 1
    return t


def _adam_update(w, g, m, v):
    m1 = ADAM_B1 * m + (1.0 - ADAM_B1) * g
    v1 = ADAM_B2 * v + (1.0 - ADAM_B2) * (g * g)
    m_hat = m1 / (1.0 - ADAM_B1 ** ADAM_STEP)
    v_hat = v1 / (1.0 - ADAM_B2 ** ADAM_STEP)
    return -ADAM_LR * (m_hat / (jnp.sqrt(v_hat) + ADAM_EPS) + ADAM_WD * w), m1, v1


def _adamw_small(w, g, m, v):
    def body(w_ref, g_ref, m_ref, v_ref, d_ref, nm_ref, nv_ref):
        d_ref[...], nm_ref[...], nv_ref[...] = _adam_update(w_ref[...], g_ref[...], m_ref[...], v_ref[...])

    vmem = pl.BlockSpec(memory_space=pltpu.VMEM)
    return pl.pallas_call(body, name="adamw_small", in_specs=[vmem] * 4, out_specs=[vmem] * 3,
                          out_shape=[jax.ShapeDtypeStruct(w.shape, F32)] * 3)(w, g, m, v)


def _adamw_stacked(w, gs, m, v):
    L, R, C = w.shape
    tr = _row_tile(R, 256)

    def body(w_ref, m_ref, v_ref, *rest):
        g_refs, (go_ref, d_ref, nm_ref, nv_ref) = rest[:L], rest[L:]
        for a, g_ref in enumerate(g_refs):
            @pl.when(pl.program_id(0) == a)
            def _():
                gv = g_ref[...]
                go_ref[...] = gv
                d_ref[...], nm_ref[...], nv_ref[...] = _adam_update(w_ref[...], gv, m_ref[...], v_ref[...])

    stacked = pl.BlockSpec((None, tr, C), lambda l, i: (l, i, 0))
    return pl.pallas_call(
        body, name="adamw_stacked", grid=(L, R // tr),
        in_specs=[stacked] * 3 + [pl.BlockSpec((tr, C), lambda l, i, a=a: (jnp.where(l == a, i, 0), 0)) for a in range(L)],
        out_specs=[stacked] * 4, out_shape=[jax.ShapeDtypeStruct((L, R, C), F32)] * 4,
        compiler_params=_params("parallel", "parallel"),
    )(w, m, v, *gs)


CAST_STEPS = 4


def _cast_into_full(ws, layers, kinds, chip):
    n = len(ws)

    def body(chip_ref, *refs):
        for w_ref, o_ref in zip(refs[:n], refs[n:]):
            o_ref[...] = w_ref[...].astype(BF16)

    in_specs, out_specs, out_shape = [], [], []
    for w, layer, kind in zip(ws, layers, kinds):
        _, R, C = w.shape
        tr = R // CAST_STEPS
        assert tr % 16 == 0
        in_specs.append(pl.BlockSpec((None, tr, C), lambda i, chip_ref, layer=layer: (layer, i, 0)))
        if kind == "col":
            out_shape.append(jax.ShapeDtypeStruct((1, R, N_CHIPS * C), BF16))
            out_specs.append(pl.BlockSpec((None, tr, C), lambda i, chip_ref: (0, i, chip_ref[0])))
        else:
            out_shape.append(jax.ShapeDtypeStruct((1, N_CHIPS * R, C), BF16))
            out_specs.append(pl.BlockSpec((None, tr, C), lambda i, chip_ref: (0, chip_ref[0] * CAST_STEPS + i, 0)))
    return pl.pallas_call(
        body, name="cast_into_full",
        grid_spec=pltpu.PrefetchScalarGridSpec(num_scalar_prefetch=1, grid=(CAST_STEPS,),
                                               in_specs=in_specs, out_specs=out_specs),
        out_shape=out_shape, compiler_params=_params("parallel"),
    )(chip, *ws)


ADD_HALF_STEPS = 8
ADD_PARTS_STEPS = 2


def _add_half(g4s, las, c):
    n = len(g4s)

    def body(c_ref, *refs):
        for g_ref, la_ref, o_ref in zip(refs[:n], refs[n:2 * n], refs[2 * n:]):
            o_ref[...] = (g_ref[...].astype(F32) + la_ref[...].astype(F32)).astype(BF16)

    g_specs, la_specs, out_shape = [], [], []
    for g4 in g4s:
        L, _, H, W = g4.shape
        th = L * H // ADD_HALF_STEPS
        per = H // th
        assert th % 16 == 0 and per * th == H
        g_specs.append(pl.BlockSpec((None, None, th, W), lambda i, c_ref, per=per: (i // per, c_ref[0], i % per, 0)))
        la_specs.append(pl.BlockSpec((None, th, W), lambda i, c_ref, per=per: (i // per, i % per, 0)))
        out_shape.append(jax.ShapeDtypeStruct((L, H, W), BF16))
    return pl.pallas_call(
        body, name="add_half",
        grid_spec=pltpu.PrefetchScalarGridSpec(num_scalar_prefetch=1, grid=(ADD_HALF_STEPS,),
                                               in_specs=g_specs + la_specs, out_specs=la_specs),
        out_shape=out_shape, compiler_params=_params("parallel"),
    )(c, *g4s, *las)


def _add_parts(ps, lbs, place, kinds):
    n = len(ps)

    def body(s_ref, *refs):
        for p_ref, lb_ref, o_ref in zip(refs[:n], refs[n:2 * n], refs[2 * n:]):
            acc = p_ref[...].astype(F32)
            for k in range(N_CHIPS - 1):
                acc = acc + lb_ref[k].astype(F32)
            o_ref[...] = acc

    p_specs, lb_specs, out_specs, out_shape = [], [], [], []
    for lb, kind in zip(lbs, kinds):
        _, L, H, C = lb.shape
        th = H // ADD_PARTS_STEPS
        assert L == 1 and th % 16 == 0
        if kind == "col":
            p_specs.append(pl.BlockSpec((None, th, C), lambda i, s_ref: (0, i, s_ref[0])))
        else:
            p_specs.append(pl.BlockSpec((None, None, th, C), lambda i, s_ref: (0, s_ref[0], i, 0)))
        lb_specs.append(pl.BlockSpec((N_CHIPS - 1, None, th, C), lambda i, s_ref: (0, 0, i, 0)))
        out_specs.append(pl.BlockSpec((None, None, th, C), lambda i, s_ref: (0, s_ref[1], i, 0)))
        out_shape.append(jax.ShapeDtypeStruct((L, 2, H, C), F32))
    return pl.pallas_call(
        body, name="add_parts",
        grid_spec=pltpu.PrefetchScalarGridSpec(num_scalar_prefetch=1, grid=(ADD_PARTS_STEPS,),
                                               in_specs=p_specs + lb_specs, out_specs=out_specs),
        out_shape=out_shape, compiler_params=_params("parallel"),
    )(place, *ps, *lbs)


ANY = pl.BlockSpec(memory_space=pl.ANY)


def _place():
    x, y, c = lax.axis_index("x"), lax.axis_index("y"), lax.axis_index("c")
    chips = [(1 - x, y), (x, 1 - y), (1 - x, 1 - y)]
    return x, y, c, chips


def _comm_call(body, name, ins, out_shape, n_remote, n_local, aliases=None):
    scratch = [pltpu.SemaphoreType.DMA((n_remote,)), pltpu.SemaphoreType.DMA((n_remote,))]
    if n_local:
        scratch.append(pltpu.SemaphoreType.DMA((n_local,)))
    return pl.pallas_call(
        body, name=name, in_specs=[ANY] * len(ins), out_specs=[ANY] * len(out_shape), out_shape=out_shape,
        scratch_shapes=scratch, input_output_aliases=aliases or {},
        compiler_params=pltpu.CompilerParams(has_side_effects=True),
    )(*ins)


def _remote(src, dst, send, recv, k, to):
    return pltpu.make_async_remote_copy(src_ref=src, dst_ref=dst, send_sem=send.at[k], recv_sem=recv.at[k],
                                        device_id=to, device_id_type=MESH)


def _gather_weights(fulls, kinds):
    n = len(fulls)
    out_shape = [jax.ShapeDtypeStruct(f.shape, f.dtype) for f in fulls]

    def body(*refs):
        outs, (send, recv) = refs[n:2 * n], refs[2 * n:]
        first = _gather_copies(outs, kinds, send, recv, 0)
        for cp in first:
            cp.start()
        for cp in first:
            cp.wait()
        passed = _pass_on_copies(outs, kinds, send, recv, len(first))
        for cp in passed:
            cp.start()
        for cp in passed:
            cp.wait()

    return _comm_call(body, "gather_weights", fulls, out_shape, 2 * n * (N_CHIPS - 1), 0, {a: a for a in range(n)})


def _window(ref, kind, s, h):
    if kind == "col":
        H, C = ref.shape[1] // 2, ref.shape[2] // N_CHIPS
        return ref.at[:, pl.ds(pl.multiple_of(h * H, 16), H), pl.ds(pl.multiple_of(s * C, LANES), C)]
    R = ref.shape[1] // N_CHIPS
    return ref.at[:, pl.ds(pl.multiple_of(s * R + h * (R // 2), 16), R // 2), :]


def _gather_copies(outs, kinds, send, recv, sem0):
    x, y, c, chips = _place()
    me = 2 * x + y
    return [_remote(_window(o, kind, me, c), _window(o, kind, me, c), send, recv, sem0 + a * (N_CHIPS - 1) + k, (*chip, c))
            for a, (o, kind) in enumerate(zip(outs, kinds)) for k, chip in enumerate(chips)]


def _pass_on_copies(outs, kinds, send, recv, sem0):
    x, y, c, chips = _place()
    cps = []
    for a, (o, kind) in enumerate(zip(outs, kinds)):
        for k, chip in enumerate(chips):
            landed = _window(o, kind, 2 * chip[0] + chip[1], c)
            cps.append(_remote(landed, landed, send, recv, sem0 + a * (N_CHIPS - 1) + k, (x, y, 1 - c)))
    return cps


def _gather_small(shards):
    n = len(shards)
    out_shape = [jax.ShapeDtypeStruct((N_CHIPS,) + s.shape, s.dtype) for s in shards]

    def body(*refs):
        srcs, outs, (send, recv, loc) = refs[:n], refs[n:2 * n], refs[2 * n:]
        x, y, c, chips = _place()
        me = 2 * x + y
        remote, local = [], []
        for a in range(n):
            local.append(pltpu.make_async_copy(srcs[a], outs[a].at[me], loc.at[a]))
            for k, chip in enumerate(chips):
                remote.append(_remote(srcs[a], outs[a].at[me], send, recv, a * (N_CHIPS - 1) + k, (*chip, c)))
        for cp in local + remote:
            cp.start()
        for cp in remote + local:
            cp.wait()

    return _comm_call(body, "gather_small", shards, out_shape, n * (N_CHIPS - 1), n)


def _exchange_halves(g4s):
    n = len(g4s)
    out_shape = [jax.ShapeDtypeStruct((g.shape[0],) + g.shape[2:], g.dtype) for g in g4s]

    def body(*refs):
        gs, las, (send, recv) = refs[:n], refs[n:2 * n], refs[2 * n:]
        x, y, c, _ = _place()
        cps = [_remote(gs[a].at[:, 1 - c], las[a], send, recv, a, (x, y, 1 - c)) for a in range(n)]
        for cp in cps:
            cp.start()
        for cp in cps:
            cp.wait()

    return _comm_call(body, "exchange_halves", g4s, out_shape, n, 0)


def _scatter_partials(ps, kinds):
    n = len(ps)

    def body(*refs):
        srcs, lbs, (send, recv) = refs[:n], refs[n:2 * n], refs[2 * n:]
        cps = _scatter_copies(srcs, lbs, kinds, send, recv)
        for cp in cps:
            cp.start()
        for cp in cps:
            cp.wait()

    return _comm_call(body, "scatter_partials", ps, _scatter_shapes(ps, kinds), n * (N_CHIPS - 1), 0)


def _scatter_shapes(ps, kinds):
    out_shape = []
    for p, kind in zip(ps, kinds):
        L, H, C = (p.shape[0], p.shape[1], p.shape[2] // N_CHIPS) if kind == "col" else (p.shape[0], p.shape[2], p.shape[3])
        out_shape.append(jax.ShapeDtypeStruct((N_CHIPS - 1, L, H, C), p.dtype))
    return out_shape


def _scatter_copies(srcs, lbs, kinds, send, recv, sem0=0):
    x, y, c, chips = _place()
    cps = []
    for a, (src, lb, kind) in enumerate(zip(srcs, lbs, kinds)):
        C = lb.shape[3]
        for k, chip in enumerate(chips):
            s = 2 * chip[0] + chip[1]
            part = src.at[:, :, pl.ds(pl.multiple_of(s * C, LANES), C)] if kind == "col" else src.at[:, s]
            cps.append(_remote(part, lb.at[k], send, recv, sem0 + a * (N_CHIPS - 1) + k, (*chip, c)))
    return cps


def _share_halves(g4s):
    n = len(g4s)
    out_shape = [jax.ShapeDtypeStruct(g.shape, g.dtype) for g in g4s]

    def body(*refs):
        outs, (send, recv) = refs[n:2 * n], refs[2 * n:]
        x, y, c, _ = _place()
        cps = [_remote(outs[a].at[:, c], outs[a].at[:, c], send, recv, a, (x, y, 1 - c)) for a in range(n)]
        for cp in cps:
            cp.start()
        for cp in cps:
            cp.wait()

    return _comm_call(body, "share_halves", g4s, out_shape, n, 0, {a: a for a in range(n)})


def _allreduce_small(part):
    N = part.shape[0]

    def body(p_ref, o_ref, buf, send, recv):
        x, y, c, _ = _place()
        me = 4 * x + 2 * y + c
        buf[me] = p_ref[...]
        cps = []
        for k in range(1, N_DEV):
            peer = (1 - x if k & 4 else x, 1 - y if k & 2 else y, 1 - c if k & 1 else c)
            cps.append(_remote(p_ref, buf.at[me], send, recv, k - 1, peer))
        for cp in cps:
            cp.start()
        for cp in cps:
            cp.wait()
        acc = buf[0]
        for i in range(1, N_DEV):
            acc = acc + buf[i]
        o_ref[...] = acc

    vmem = pl.BlockSpec(memory_space=pltpu.VMEM)
    return pl.pallas_call(
        body, name="allreduce_small", in_specs=[vmem], out_specs=vmem,
        out_shape=jax.ShapeDtypeStruct((N, LANES), F32),
        scratch_shapes=[pltpu.VMEM((N_DEV, N, LANES), F32), pltpu.SemaphoreType.DMA((N_DEV - 1,)),
                        pltpu.SemaphoreType.DMA((N_DEV - 1,))],
        compiler_params=pltpu.CompilerParams(has_side_effects=True, vmem_limit_bytes=VMEM_LIMIT_BYTES),
    )(part)


AFTER_ATTENTION = ("w_out", "w_up", "w_down")


def _layer_fwd(x, p, full=None, l=0):
    comm = full is not None

    def guest(first, second):
        first, second = ([k for k in ks if comm and k[0] < len(full)] for ks in (first, second))
        if not first + second:
            return None, []
        return _gather_guest([full[i][n] for i, n in first], [BIG_KIND[n] for _, n in first],
                             [full[i][n] for i, n in second], [BIG_KIND[n] for _, n in second]), first + second

    def done(keys, bufs):
        for (i, n), buf in zip(keys, bufs):
            full[i][n] = buf

    def weight(n):
        return full[l][n][0] if comm else p[n]

    in_sweep = ([(l, "w_up")] if l else [(l, n) for n in AFTER_ATTENTION]) if comm else []
    g, keys = guest([], [(l, "w_out")] if l else [])
    h1t, proj, out = _norm_matmul(x, p["norm1_g"], weight("w_in"), g)
    done(keys, out)
    attn, ltot, attnt, out = _attn_fwd(proj, p["q_norm_g"], p["k_norm_g"],
                                       [full[i][n] for i, n in in_sweep], [BIG_KIND[n] for _, n in in_sweep])
    done(in_sweep, out)
    g, keys = guest([], in_sweep)
    c1, out = _glu_conv_fwd(proj, p["conv_dw_w"], p["conv_dw_b"], g)
    done(keys, out)
    c, ct = _ln_silu_fwd(c1, p["conv_ln_g"], p["conv_ln_b"])
    x_mid, _ = _matmul_res([attn, c], weight("w_out"), x)
    g, keys = guest([(l + 1, "w_down")], [])
    h2t, u, out = _norm_matmul(x_mid, p["norm2_g"], weight("w_up"), g)
    done(keys, out)
    g, keys = guest([(l + 1, "w_in")], [(l + 1, "w_down")])
    act, out = _ffn_act_fwd(u, p["ffn_dw_w"], p["ffn_dw_b"], g)
    done(keys, out)
    g, keys = guest([(l + 1, "w_out")], [(l + 1, "w_in")])
    x_out, out = _matmul_res([act], weight("w_down"), x_mid, g)
    done(keys, out)
    saved = dict(x=x, h1t=h1t, proj=proj, attnt=attnt, ltot=ltot, c1=c1, ct=ct, x_mid=x_mid, h2t=h2t, u=u, act=act)
    return x_out, saved


def _chip_partials(gs, kinds, core):
    g4s = _halves_view(gs, kinds)
    return _add_halves(g4s, _exchange_halves(g4s), kinds, core)


def _halves_view(gs, kinds):
    return [g.reshape(1, 2, g.shape[0] // 2, g.shape[1]) if kind == "col"
            else g.reshape(N_CHIPS, 2, g.shape[0] // N_CHIPS // 2, g.shape[1]) for g, kind in zip(gs, kinds)]


def _add_halves(g4s, received, kinds, core):
    parts = _add_half(g4s, received, core)
    return [p if kind == "col" else p.reshape(1, N_CHIPS, p.shape[1], p.shape[2]) for p, kind in zip(parts, kinds)]


def _gather_guest(first, first_kinds, second, second_kinds):
    bufs = list(first) + list(second)
    n1 = len(first)

    def copies(ins, outs, send, recv, sem0=0):
        return (_gather_copies(outs[:n1], first_kinds, send, recv, sem0)
                + _pass_on_copies(outs[n1:], second_kinds, send, recv, sem0 + n1 * (N_CHIPS - 1)))

    return _Guest("gather", bufs, [jax.ShapeDtypeStruct(f.shape, f.dtype) for f in bufs],
                  {a: a for a in range(len(bufs))}, copies, len(bufs) * (N_CHIPS - 1))


def _scatter_guest(parts, kinds):
    return _Guest("scatter", list(parts), _scatter_shapes(parts, kinds), {},
                  lambda ins, outs, send, recv, sem0=0: _scatter_copies(ins, outs, kinds, send, recv, sem0),
                  len(parts) * (N_CHIPS - 1))


def _exchange_guest(g4s):
    def copies(ins, outs, send, recv, sem0=0):
        x, y, c, _ = _place()
        return [_remote(g.at[:, 1 - c], la, send, recv, sem0 + a, (x, y, 1 - c)) for a, (g, la) in enumerate(zip(ins, outs))]

    out_shape = [jax.ShapeDtypeStruct((g.shape[0],) + g.shape[2:], g.dtype) for g in g4s]
    return _Guest("exchange", list(g4s), out_shape, {}, copies, len(g4s))


def _share_guest(g4s):
    def copies(ins, outs, send, recv, sem0=0):
        x, y, c, _ = _place()
        return [_remote(o.at[:, c], o.at[:, c], send, recv, sem0 + a, (x, y, 1 - c)) for a, o in enumerate(outs)]

    n = len(g4s)
    return _Guest("share", list(g4s), [jax.ShapeDtypeStruct(g.shape, g.dtype) for g in g4s], {a: a for a in range(n)}, copies, n)


def _owned_sums(parts, landed, kinds, place):
    halves = _add_parts(parts, landed, place, kinds)
    return [g.reshape(2 * g.shape[2], g.shape[3]) for g in _share_halves(halves)]


def _layer_bwd(dx_out, s, p, comm=None):
    g = {}
    on = comm is not None
    core, place, pending = comm if on else (None, None, [])
    n_p = len(pending)
    finished = lambda arrays: [a.reshape(2 * a.shape[2], a.shape[3]) for a in arrays]

    if n_p:
        g4s_p = _halves_view(pending, ["col"] * n_p)
    dact, received = _matmul_nt([dx_out], p["w_down"], BF16, _exchange_guest(g4s_p) if n_p else None)
    if n_p:
        parts_p = _add_halves(g4s_p, received, ["col"] * n_p, core)
    g["w_down"] = _matmul_tn([s["act"]], [dx_out], tk=D_FF // 2, tn=D_MODEL, transposed=False)
    dgate, dval, g["ffn_dw_w"], g["ffn_dw_b"], landed_p = _ffn_act_bwd(
        s["u"], p["ffn_dw_w"], p["ffn_dw_b"], dact, _scatter_guest(parts_p, ["col"] * n_p) if n_p else None)
    guest = _share_guest(_add_parts(parts_p, landed_p, place, ["col"] * n_p)) if n_p else None
    dx_mid, g["norm2_g"], shared_p = _matmul_nt_rmsbwd([dgate, dval], p["w_up"], s["x_mid"], p["norm2_g"], dx_out, guest)
    g["w_up"] = _matmul_tn([s["h2t"]], [dgate, dval], tk=D_MODEL, tn=D_FF // 2)
    dmix, _ = _matmul_nt([dx_mid], p["w_out"], F32)
    g["w_out"] = _matmul_tn([s["attnt"], s["ct"]], [dx_mid], tk=ATTN_WIDTH, tn=D_MODEL)
    dc1, g["conv_ln_g"], g["conv_ln_b"] = _ln_silu_bwd(s["c1"], p["conv_ln_g"], p["conv_ln_b"], dmix)

    late = AFTER_ATTENTION
    parts, kinds = [], []
    if on:
        kinds = [BIG_KIND[n] for n in late]
        g4s = _halves_view([g[n] for n in late], kinds)
    da, db, g["conv_dw_w"], g["conv_dw_b"], received = _glu_conv_bwd(
        s["proj"], p["conv_dw_w"], dc1, _exchange_guest(g4s) if on else None)
    if on:
        parts = _add_halves(g4s, received, kinds, core)
    (dq, dk, dv, g["q_norm_g"], g["k_norm_g"]), landed = _attn_bwd(
        s["proj"], p["q_norm_g"], p["k_norm_g"], s["ltot"], dmix, parts, kinds)
    guest = _share_guest(_add_parts(parts, landed, place, kinds)) if on else None
    pieces = [dq, dk, dv, da, db]
    dx, g["norm1_g"], shared = _matmul_nt_rmsbwd(pieces, p["w_in"], s["x"], p["norm1_g"], dx_mid, guest)
    g["w_in"] = _matmul_tn([s["h1t"]], pieces, tk=D_MODEL, tn=ATTN_WIDTH)
    sums = {}
    if on:
        sums = dict(zip(late, finished(shared)), pending=finished(shared_p))
    return dx, g, sums


WEIGHTS = ("norm1_g", "w_in", "q_norm_g", "k_norm_g", "conv_dw_w", "conv_dw_b", "conv_ln_g", "conv_ln_b",
           "w_out", "norm2_g", "w_up", "ffn_dw_w", "ffn_dw_b", "w_down")
BIG = ("w_in", "w_out", "w_up", "w_down")
BIG_KIND = {"w_in": "col", "w_out": "row", "w_up": "col", "w_down": "row"}
SMALL_SHARDED = ("conv_dw_w", "ffn_dw_w")
REPLICATED = tuple(n for n in WEIGHTS if n not in BIG + SMALL_SHARDED)


def _pack(arrays):
    flat = jnp.concatenate([a.reshape(-1) for a in arrays])
    rows = -(-flat.shape[0] // (SUBLANES * LANES)) * SUBLANES
    return jnp.pad(flat, (0, rows * LANES - flat.shape[0])).reshape(rows, LANES)


def _unpack(packed, shapes):
    flat = packed.reshape(-1)
    out, off = [], 0
    for shape in shapes:
        size = 1
        for d in shape:
            size *= d
        out.append(flat[off:off + size].reshape(shape))
        off += size
    return out


def _unshard_last(stacked):
    n, L, K, C = stacked.shape
    return jnp.transpose(stacked, (1, 2, 0, 3)).reshape(L, K, n * C)


def kernel(x, norm1_g, w_in, q_norm_g, k_norm_g, conv_dw_w, conv_dw_b, conv_ln_g, conv_ln_b, w_out, norm2_g, w_up, ffn_dw_w, ffn_dw_b, w_down, loss_target, m_norm1_g, m_w_in, m_q_norm_g, m_k_norm_g, m_conv_dw_w, m_conv_dw_b, m_conv_ln_g, m_conv_ln_b, m_w_out, m_norm2_g, m_w_up, m_ffn_dw_w, m_ffn_dw_b, m_w_down, v_norm1_g, v_w_in, v_q_norm_g, v_k_norm_g, v_conv_dw_w, v_conv_dw_b, v_conv_ln_g, v_conv_ln_b, v_w_out, v_norm2_g, v_w_up, v_ffn_dw_w, v_ffn_dw_b, v_w_down):
    given = dict(locals())
    w = {n: given[n] for n in WEIGHTS}
    m = {n: given["m_" + n] for n in WEIGHTS}
    v = {n: given["v_" + n] for n in WEIGHTS}
    chip = 2 * lax.axis_index("x") + lax.axis_index("y")
    core = lax.axis_index("c")
    chip_arr = jnp.reshape(chip, (1,)).astype(jnp.int32)
    core_arr = jnp.reshape(core, (1,)).astype(jnp.int32)
    L = DEPTH

    place = jnp.concatenate([chip_arr, core_arr])

    first = _cast_into_full([w["w_in"]], [0], ["col"], chip_arr)
    rest = [(l, n) for l in range(L) for n in BIG if (l, n) != (0, "w_in")]
    cast = dict(zip(rest, _cast_into_full([w[n] for _, n in rest], [l for l, _ in rest],
                                          [BIG_KIND[n] for _, n in rest], chip_arr)))
    cast[0, "w_in"] = first[0]
    full = [{n: cast[l, n] for n in BIG} for l in range(L)]
    full[0]["w_in"] = _gather_weights([full[0]["w_in"]], ["col"])[0]
    small_full = {n: _unshard_last(stacked)
                  for n, stacked in zip(SMALL_SHARDED, _gather_small([w[n] for n in SMALL_SHARDED]))}
    params = []
    for l in range(L):
        p = {n: small_full[n][l] for n in SMALL_SHARDED}
        p.update({n: w[n][l][None] for n in REPLICATED})
        params.append(p)

    act = x[0]
    saved = []
    for l in range(L):
        act, s = _layer_fwd(act, params[l], full, l)
        saved.append(s)
    for l in range(L):
        params[l].update({n: full[l][n][0] for n in BIG})
    dx, loss_part = _loss_grad(act, loss_target[0])
    loss = lax.psum(loss_part, ("x", "y", "c"))

    grads = [None] * L
    summed = {}
    pending = []
    for l in reversed(range(L)):
        dx, grads[l], sums = _layer_bwd(dx, saved[l], params[l], (core_arr, place, pending))
        summed.update({(l, n): sums[n] for n in AFTER_ATTENTION})
        if pending:
            summed[l + 1, "w_in"] = sums["pending"][0]
        pending = [grads[l]["w_in"]]
    parts = _chip_partials(pending, ["col"], core_arr)
    summed[0, "w_in"] = _owned_sums(parts, _scatter_partials(parts, ["col"]), ["col"], place)[0]

    grad, delta, new_m, new_v = {}, {}, {}, {}
    for n in BIG:
        grad[n], delta[n], new_m[n], new_v[n] = _adamw_stacked(w[n], [summed[l, n] for l in range(L)], m[n], v[n])

    small = REPLICATED + SMALL_SHARDED
    small_grads = [jnp.stack([grads[l][n] for l in range(L)]) for n in small]
    small_sums = _unpack(_allreduce_small(_pack(small_grads)), [a.shape for a in small_grads])
    for n, g in zip(small, small_sums):
        if n in REPLICATED:
            grad[n] = g.reshape(w[n].shape)
        else:
            width = w[n].shape[2]
            grad[n] = lax.dynamic_slice_in_dim(g, chip * width, width, axis=2)
    for n in small:
        delta[n], new_m[n], new_v[n] = _adamw_small(w[n], grad[n], m[n], v[n])

    return (loss, dx[None], *[grad[n] for n in WEIGHTS], *[delta[n] for n in WEIGHTS],
            *[new_m[n] for n in WEIGHTS], *[new_v[n] for n in WEIGHTS])
```
